```python
import jax, jax.numpy as jnp
from jax import lax
import numpy as np

D_MODEL = 1024
BATCH = 8
SEQ = 2048
DEPTH = 4

MIX_WIDTH = D_MODEL
POOL_WIDTH = MIX_WIDTH // 4
POOL_WINDOWS = (2, 4, 8, 16)
POOL_GROUPS = len(POOL_WINDOWS)
POOL_GROUP_DIM = POOL_WIDTH // POOL_GROUPS
HEAD_DIM = 64
ATTN_WIDTH = MIX_WIDTH - POOL_WIDTH
N_HEADS = ATTN_WIDTH // HEAD_DIM
DILATED_PATTERNS = ((128, 1), (512, 4), (2048, 16))
ROPE_THETA = 500000.0
ROPE_DIM = HEAD_DIM // 4
D_FF = 2816
IN_PROJ_WIDTH = POOL_WIDTH + 3 * ATTN_WIDTH
NORM_EPS = 1e-6
MASK_VALUE = -1e30

kernel_name = "hybrid_pool_dilated_attn_macaron_encoder"


def rmsnorm(x, g):
    xf = x.astype(jnp.float32)
    y = xf * lax.rsqrt(jnp.mean(xf * xf, axis=-1, keepdims=True) + NORM_EPS)
    return (y * g.astype(jnp.float32)).astype(x.dtype)


def swiglu(h, w_gate, w_up, w_down):
    return (jax.nn.silu(h @ w_gate) * (h @ w_up)) @ w_down


def rope_tables(positions):
    inv_freq = ROPE_THETA ** (-jnp.arange(0, ROPE_DIM, 2, dtype=jnp.float32) / ROPE_DIM)
    ang = positions.astype(jnp.float32)[..., None] * inv_freq
    return jnp.cos(ang)[:, :, None, :], jnp.sin(ang)[:, :, None, :]


def apply_partial_rope(t, cos, sin):
    tf = t.astype(jnp.float32)
    half = ROPE_DIM // 2
    t1, t2, rest = tf[..., :half], tf[..., half:ROPE_DIM], tf[..., ROPE_DIM:]
    rot = jnp.concatenate([t1 * cos - t2 * sin, t2 * cos + t1 * sin, rest], axis=-1)
    return rot.astype(t.dtype)


def multiscale_pool(v, pool_w, pool_scale):
    B, S, _ = v.shape
    vf = v.astype(jnp.float32).reshape(B, S, POOL_GROUPS, POOL_GROUP_DIM)
    cs = jnp.pad(lax.cumsum(vf, axis=1), ((0, 0), (1, 0), (0, 0), (0, 0)))
    pos = jnp.arange(S)
    means = []
    for g, w in enumerate(POOL_WINDOWS):
        lo = jnp.maximum(pos - w // 2, 0)
        hi = jnp.minimum(pos + w - 1 - w // 2, S - 1)
        cnt = (hi - lo + 1).astype(jnp.float32)
        means.append((cs[:, hi + 1, g] - cs[:, lo, g]) / cnt[None, :, None])
    pooled = jnp.stack(means, axis=2)
    diff = (pooled - vf).astype(v.dtype)
    y = jnp.einsum('bsgc,gcd->bsgd', diff, pool_w).reshape(B, S, POOL_WIDTH)
    return y * pool_scale


def dilated_branch(q, k, v, window, dilation):
    B, S, H, Dh = q.shape
    half = window // (2 * dilation)
    blk = half
    L = S // dilation
    nb = -(-L // blk)
    Lp = nb * blk

    def to_compressed(t):
        t = t.astype(jnp.float32).reshape(B, L, dilation, H, Dh)
        return jnp.pad(t, ((0, 0), (0, Lp - L), (0, 0), (0, 0), (0, 0)))

    def band(t):
        tp = jnp.pad(t, ((0, 0), (blk, blk), (0, 0), (0, 0), (0, 0)))
        tp = tp.reshape(B, nb + 2, blk, dilation, H, Dh)
        return jnp.concatenate([tp[:, :-2], tp[:, 1:-1], tp[:, 2:]], axis=2)

    qb = to_compressed(q).reshape(B, nb, blk, dilation, H, Dh)
    kb = band(to_compressed(k))
    vb = band(to_compressed(v))

    t_idx = jnp.arange(nb)[:, None] * blk + jnp.arange(blk)[None, :]
    j_idx = jnp.arange(nb)[:, None] * blk - blk + jnp.arange(3 * blk)[None, :]
    jj = j_idx[:, None, :]
    valid = (jnp.abs(jj - t_idx[:, :, None]) <= half) & (jj >= 0) & (jj < L)

    scale = 1.0 / np.sqrt(Dh)
    s = jnp.einsum('bnqrhd,bnkrhd->bnrhqk', qb, kb) * scale
    s = jnp.where(valid[None, :, None, None], s, MASK_VALUE)
    m = jnp.max(s, axis=-1, keepdims=True)
    p = jnp.exp(s - m)
    denom = jnp.sum(p, axis=-1)
    num = jnp.einsum('bnrhqk,bnkrhd->bnqrhd', p, vb)

    num = num.reshape(B, Lp, dilation, H, Dh)[:, :L].reshape(B, S, H, Dh)

    def stat_back(t):
        t = jnp.transpose(t, (0, 1, 4, 2, 3)).reshape(B, Lp, dilation, H)
        return t[:, :L].reshape(B, S, H)

    return num, stat_back(m[..., 0]), stat_back(denom)


def dilated_mixture_attention(q, k, v):
    branches = [dilated_branch(q, k, v, w, d) for (w, d) in DILATED_PATTERNS]
    m_all = jnp.stack([b[1] for b in branches], axis=0)
    wts = jnp.exp(m_all - jnp.max(m_all, axis=0, keepdims=True))
    num = sum(wts[i][..., None] * branches[i][0] for i in range(len(branches)))
    den = sum(wts[i] * branches[i][2] for i in range(len(branches)))
    return (num / den[..., None]).astype(q.dtype)


def _fwd_setup_inputs(seed: int = 0) -> dict:
    key = jax.random.key(seed)
    ks = jax.random.split(key, 20)
    f32 = jnp.float32

    def normal(k, shape, fan_in):
        return jax.random.normal(k, shape, f32) * (fan_in ** -0.5)

    def gain(k, shape):
        return jnp.ones(shape, f32) + 0.02 * jax.random.normal(k, shape, f32)

    x = jax.random.normal(ks[0], (BATCH, SEQ, D_MODEL), f32)
    start = jax.random.randint(ks[1], (BATCH, 1), 0, 4096, dtype=jnp.int32)
    positions = start + jnp.arange(SEQ, dtype=jnp.int32)[None, :]
    return {
        "x": x,
        "positions": positions,
        "ffn1_norm": gain(ks[2], (DEPTH, D_MODEL)),
        "ffn1_w_gate": normal(ks[3], (DEPTH, D_MODEL, D_FF), D_MODEL),
        "ffn1_w_up": normal(ks[4], (DEPTH, D_MODEL, D_FF), D_MODEL),
        "ffn1_w_down": normal(ks[5], (DEPTH, D_FF, D_MODEL), D_FF),
        "mix_norm": gain(ks[6], (DEPTH, D_MODEL)),
        "w_in": normal(ks[7], (DEPTH, D_MODEL, IN_PROJ_WIDTH), D_MODEL),
        "pool_w": normal(ks[8], (DEPTH, POOL_GROUPS, POOL_GROUP_DIM, POOL_GROUP_DIM), POOL_GROUP_DIM),
        "pool_scale": gain(ks[9], (DEPTH, POOL_WIDTH)),
        "w_out": normal(ks[10], (DEPTH, MIX_WIDTH, D_MODEL), MIX_WIDTH),
        "ffn2_norm": gain(ks[11], (DEPTH, D_MODEL)),
        "ffn2_w_gate": normal(ks[12], (DEPTH, D_MODEL, D_FF), D_MODEL),
        "ffn2_w_up": normal(ks[13], (DEPTH, D_MODEL, D_FF), D_MODEL),
        "ffn2_w_down": normal(ks[14], (DEPTH, D_FF, D_MODEL), D_FF),
        "final_norm": gain(ks[15], (D_MODEL,)),
    }


def _fwd_reference(x, positions, ffn1_norm, ffn1_w_gate, ffn1_w_up, ffn1_w_down, mix_norm, w_in,
              pool_w, pool_scale, w_out, ffn2_norm, ffn2_w_gate, ffn2_w_up, ffn2_w_down, final_norm):
    B, S, _ = x.shape
    cos, sin = rope_tables(positions)
    for l in range(DEPTH):
        x = x + 0.5 * swiglu(rmsnorm(x, ffn1_norm[l]), ffn1_w_gate[l], ffn1_w_up[l], ffn1_w_down[l])

        h = rmsnorm(x, mix_norm[l])
        proj = h @ w_in[l]
        v_pool = proj[..., :POOL_WIDTH]
        q = proj[..., POOL_WIDTH:POOL_WIDTH + ATTN_WIDTH].reshape(B, S, N_HEADS, HEAD_DIM)
        k = proj[..., POOL_WIDTH + ATTN_WIDTH:POOL_WIDTH + 2 * ATTN_WIDTH].reshape(B, S, N_HEADS, HEAD_DIM)
        v = proj[..., POOL_WIDTH + 2 * ATTN_WIDTH:].reshape(B, S, N_HEADS, HEAD_DIM)

        y_pool = multiscale_pool(v_pool, pool_w[l], pool_scale[l])
        q = apply_partial_rope(q, cos, sin)
        k = apply_partial_rope(k, cos, sin)
        y_attn = dilated_mixture_attention(q, k, v).reshape(B, S, ATTN_WIDTH)

        mixed = jnp.concatenate([y_pool.astype(x.dtype), y_attn.astype(x.dtype)], axis=-1)
        x = x + mixed @ w_out[l]

        x = x + 0.5 * swiglu(rmsnorm(x, ffn2_norm[l]), ffn2_w_gate[l], ffn2_w_up[l], ffn2_w_down[l])
    return rmsnorm(x, final_norm)


import jax as _jax
import jax.numpy as _jnp

TWIN_FORMAT = 'train_step'
FWD_PARAMS = ['x', 'positions', 'ffn1_norm', 'ffn1_w_gate', 'ffn1_w_up', 'ffn1_w_down', 'mix_norm', 'w_in', 'pool_w', 'pool_scale', 'w_out', 'ffn2_norm', 'ffn2_w_gate', 'ffn2_w_up', 'ffn2_w_down', 'final_norm']
TWIN_WEIGHTS = ['ffn1_norm', 'ffn1_w_gate', 'ffn1_w_up', 'ffn1_w_down', 'mix_norm', 'w_in', 'pool_w', 'pool_scale', 'w_out', 'ffn2_norm', 'ffn2_w_gate', 'ffn2_w_up', 'ffn2_w_down', 'final_norm']
TWIN_DIFF_INPUT = 'x'
TWIN_INPUTS = ['x', 'positions', 'ffn1_norm', 'ffn1_w_gate', 'ffn1_w_up', 'ffn1_w_down', 'mix_norm', 'w_in', 'pool_w', 'pool_scale', 'w_out', 'ffn2_norm', 'ffn2_w_gate', 'ffn2_w_up', 'ffn2_w_down', 'final_norm', 'loss_target', 'm_ffn1_norm', 'm_ffn1_w_gate', 'm_ffn1_w_up', 'm_ffn1_w_down', 'm_mix_norm', 'm_w_in', 'm_pool_w', 'm_pool_scale', 'm_w_out', 'm_ffn2_norm', 'm_ffn2_w_gate', 'm_ffn2_w_up', 'm_ffn2_w_down', 'm_final_norm', 'v_ffn1_norm', 'v_ffn1_w_gate', 'v_ffn1_w_up', 'v_ffn1_w_down', 'v_mix_norm', 'v_w_in', 'v_pool_w', 'v_pool_scale', 'v_w_out', 'v_ffn2_norm', 'v_ffn2_w_gate', 'v_ffn2_w_up', 'v_ffn2_w_down', 'v_final_norm']
TWIN_OUTPUTS = ['loss', 'grad_x', 'grad_ffn1_norm', 'grad_ffn1_w_gate', 'grad_ffn1_w_up', 'grad_ffn1_w_down', 'grad_mix_norm', 'grad_w_in', 'grad_pool_w', 'grad_pool_scale', 'grad_w_out', 'grad_ffn2_norm', 'grad_ffn2_w_gate', 'grad_ffn2_w_up', 'grad_ffn2_w_down', 'grad_final_norm', 'delta_ffn1_norm', 'delta_ffn1_w_gate', 'delta_ffn1_w_up', 'delta_ffn1_w_down', 'delta_mix_norm', 'delta_w_in', 'delta_pool_w', 'delta_pool_scale', 'delta_w_out', 'delta_ffn2_norm', 'delta_ffn2_w_gate', 'delta_ffn2_w_up', 'delta_ffn2_w_down', 'delta_final_norm', 'new_m_ffn1_norm', 'new_m_ffn1_w_gate', 'new_m_ffn1_w_up', 'new_m_ffn1_w_down', 'new_m_mix_norm', 'new_m_w_in', 'new_m_pool_w', 'new_m_pool_scale', 'new_m_w_out', 'new_m_ffn2_norm', 'new_m_ffn2_w_gate', 'new_m_ffn2_w_up', 'new_m_ffn2_w_down', 'new_m_final_norm', 'new_v_ffn1_norm', 'new_v_ffn1_w_gate', 'new_v_ffn1_w_up', 'new_v_ffn1_w_down', 'new_v_mix_norm', 'new_v_w_in', 'new_v_pool_w', 'new_v_pool_scale', 'new_v_w_out', 'new_v_ffn2_norm', 'new_v_ffn2_w_gate', 'new_v_ffn2_w_up', 'new_v_ffn2_w_down', 'new_v_final_norm']
TWIN_LEAF_KINDS = {'loss': 'loss', 'grad_x': 'grad_x', 'grad_ffn1_norm': 'grad_w', 'grad_ffn1_w_gate': 'grad_w', 'grad_ffn1_w_up': 'grad_w', 'grad_ffn1_w_down': 'grad_w', 'grad_mix_norm': 'grad_w', 'grad_w_in': 'grad_w', 'grad_pool_w': 'grad_w', 'grad_pool_scale': 'grad_w', 'grad_w_out': 'grad_w', 'grad_ffn2_norm': 'grad_w', 'grad_ffn2_w_gate': 'grad_w', 'grad_ffn2_w_up': 'grad_w', 'grad_ffn2_w_down': 'grad_w', 'grad_final_norm': 'grad_w', 'delta_ffn1_norm': 'delta_w', 'delta_ffn1_w_gate': 'delta_w', 'delta_ffn1_w_up': 'delta_w', 'delta_ffn1_w_down': 'delta_w', 'delta_mix_norm': 'delta_w', 'delta_w_in': 'delta_w', 'delta_pool_w': 'delta_w', 'delta_pool_scale': 'delta_w', 'delta_w_out': 'delta_w', 'delta_ffn2_norm': 'delta_w', 'delta_ffn2_w_gate': 'delta_w', 'delta_ffn2_w_up': 'delta_w', 'delta_ffn2_w_down': 'delta_w', 'delta_final_norm': 'delta_w', 'new_m_ffn1_norm': 'new_m', 'new_m_ffn1_w_gate': 'new_m', 'new_m_ffn1_w_up': 'new_m', 'new_m_ffn1_w_down': 'new_m', 'new_m_mix_norm': 'new_m', 'new_m_w_in': 'new_m', 'new_m_pool_w': 'new_m', 'new_m_pool_scale': 'new_m', 'new_m_w_out': 'new_m', 'new_m_ffn2_norm': 'new_m', 'new_m_ffn2_w_gate': 'new_m', 'new_m_ffn2_w_up': 'new_m', 'new_m_ffn2_w_down': 'new_m', 'new_m_final_norm': 'new_m', 'new_v_ffn1_norm': 'new_v', 'new_v_ffn1_w_gate': 'new_v', 'new_v_ffn1_w_up': 'new_v', 'new_v_ffn1_w_down': 'new_v', 'new_v_mix_norm': 'new_v', 'new_v_w_in': 'new_v', 'new_v_pool_w': 'new_v', 'new_v_pool_scale': 'new_v', 'new_v_w_out': 'new_v', 'new_v_ffn2_norm': 'new_v', 'new_v_ffn2_w_gate': 'new_v', 'new_v_ffn2_w_up': 'new_v', 'new_v_ffn2_w_down': 'new_v', 'new_v_final_norm': 'new_v'}


def _forward(args):
    return _fwd_reference(*[args[k] for k in FWD_PARAMS])


def _output_shape():
    out = _jax.eval_shape(lambda: _forward(_fwd_setup_inputs(0)))
    return out.shape, out.dtype

N_MICROBATCH = 1
ADAM_LR = 0.001
ADAM_B1 = 0.9
ADAM_B2 = 0.999
ADAM_EPS = 1e-08
ADAM_WD = 0.01
ADAM_STEP = 10
PER_EXAMPLE_BATCH_AXIS = {'x': 0, 'positions': 0, 'loss_target': 0}
SHARED_INPUTS = []
_WEIGHT_DTYPES = {'ffn1_norm': _jnp.float32, 'ffn1_w_gate': _jnp.float32, 'ffn1_w_up': _jnp.float32, 'ffn1_w_down': _jnp.float32, 'mix_norm': _jnp.float32, 'w_in': _jnp.float32, 'pool_w': _jnp.float32, 'pool_scale': _jnp.float32, 'w_out': _jnp.float32, 'ffn2_norm': _jnp.float32, 'ffn2_w_gate': _jnp.float32, 'ffn2_w_up': _jnp.float32, 'ffn2_w_down': _jnp.float32, 'final_norm': _jnp.float32}
MOMENT_SCALE = {'ffn1_norm': 5.106777e-02, 'ffn1_w_gate': 2.134816e-02, 'ffn1_w_up': 2.071001e-02, 'ffn1_w_down': 3.424778e-02, 'mix_norm': 5.326200e-02, 'w_in': 3.319917e-02, 'pool_w': 9.407404e-02, 'pool_scale': 9.686750e-02, 'w_out': 4.917773e-02, 'ffn2_norm': 4.291703e-02, 'ffn2_w_gate': 1.870099e-02, 'ffn2_w_up': 1.810563e-02, 'ffn2_w_down': 3.001515e-02, 'final_norm': 1.599345e+01}


def _to_microbatches(a, axis):
    t = _jnp.moveaxis(a, axis, 0)
    t = t.reshape((N_MICROBATCH, t.shape[0] // N_MICROBATCH) + t.shape[1:])
    return _jnp.moveaxis(t, 1, axis + 1)


def setup_inputs(seed: int = 0) -> dict:
    inp = _fwd_setup_inputs(seed)
    key = _jax.random.fold_in(_jax.random.key(seed), 7919)
    shape, _ = _output_shape()
    out = dict(inp)
    out["loss_target"] = _jax.random.normal(_jax.random.fold_in(key, 0), shape, _jnp.float32)
    for i, name in enumerate(TWIN_WEIGHTS):
        w = inp[name].astype(_jnp.float32)
        if MOMENT_SCALE is None:
            s = _jnp.sqrt(_jnp.mean(_jnp.square(w)) + 1e-30)
        else:
            s = MOMENT_SCALE[name]
        km, kv = _jax.random.split(_jax.random.fold_in(key, i + 1))
        out[name] = w
        out["m_" + name] = s * _jax.random.normal(km, w.shape, _jnp.float32)
        out["v_" + name] = (s * s) * _jax.random.uniform(kv, w.shape, _jnp.float32, 0.5, 1.5)
    if N_MICROBATCH > 1:
        for name, axis in PER_EXAMPLE_BATCH_AXIS.items():
            out[name] = _to_microbatches(out[name], axis)
    return {'x': out['x'], 'positions': out['positions'], 'ffn1_norm': out['ffn1_norm'], 'ffn1_w_gate': out['ffn1_w_gate'], 'ffn1_w_up': out['ffn1_w_up'], 'ffn1_w_down': out['ffn1_w_down'], 'mix_norm': out['mix_norm'], 'w_in': out['w_in'], 'pool_w': out['pool_w'], 'pool_scale': out['pool_scale'], 'w_out': out['w_out'], 'ffn2_norm': out['ffn2_norm'], 'ffn2_w_gate': out['ffn2_w_gate'], 'ffn2_w_up': out['ffn2_w_up'], 'ffn2_w_down': out['ffn2_w_down'], 'final_norm': out['final_norm'], 'loss_target': out['loss_target'], 'm_ffn1_norm': out['m_ffn1_norm'], 'm_ffn1_w_gate': out['m_ffn1_w_gate'], 'm_ffn1_w_up': out['m_ffn1_w_up'], 'm_ffn1_w_down': out['m_ffn1_w_down'], 'm_mix_norm': out['m_mix_norm'], 'm_w_in': out['m_w_in'], 'm_pool_w': out['m_pool_w'], 'm_pool_scale': out['m_pool_scale'], 'm_w_out': out['m_w_out'], 'm_ffn2_norm': out['m_ffn2_norm'], 'm_ffn2_w_gate': out['m_ffn2_w_gate'], 'm_ffn2_w_up': out['m_ffn2_w_up'], 'm_ffn2_w_down': out['m_ffn2_w_down'], 'm_final_norm': out['m_final_norm'], 'v_ffn1_norm': out['v_ffn1_norm'], 'v_ffn1_w_gate': out['v_ffn1_w_gate'], 'v_ffn1_w_up': out['v_ffn1_w_up'], 'v_ffn1_w_down': out['v_ffn1_w_down'], 'v_mix_norm': out['v_mix_norm'], 'v_w_in': out['v_w_in'], 'v_pool_w': out['v_pool_w'], 'v_pool_scale': out['v_pool_scale'], 'v_w_out': out['v_w_out'], 'v_ffn2_norm': out['v_ffn2_norm'], 'v_ffn2_w_gate': out['v_ffn2_w_gate'], 'v_ffn2_w_up': out['v_ffn2_w_up'], 'v_ffn2_w_down': out['v_ffn2_w_down'], 'v_final_norm': out['v_final_norm']}


def _loss(weights, diff, rest, loss_target):
    with _jax.named_scope("forward"):
        args = {**rest, TWIN_DIFF_INPUT: diff, **{k: w.astype(_WEIGHT_DTYPES[k]) for k, w in weights.items()}}
        y = _forward(args)
    with _jax.named_scope("loss_head"):
        err = _jnp.square(y.astype(_jnp.float32) - loss_target)
        return 0.5 * _jnp.sum(_jnp.mean(err, axis=-1)) if err.ndim else 0.5 * err


def _adamw(w, g, m, v):
    m = ADAM_B1 * m + (1.0 - ADAM_B1) * g
    v = ADAM_B2 * v + (1.0 - ADAM_B2) * _jnp.square(g)
    m_hat = m / (1.0 - ADAM_B1 ** ADAM_STEP)
    v_hat = v / (1.0 - ADAM_B2 ** ADAM_STEP)
    delta = -ADAM_LR * (m_hat / (_jnp.sqrt(v_hat) + ADAM_EPS) + ADAM_WD * w)
    return delta, m, v


def reference(x, positions, ffn1_norm, ffn1_w_gate, ffn1_w_up, ffn1_w_down, mix_norm, w_in, pool_w, pool_scale, w_out, ffn2_norm, ffn2_w_gate, ffn2_w_up, ffn2_w_down, final_norm, loss_target, m_ffn1_norm, m_ffn1_w_gate, m_ffn1_w_up, m_ffn1_w_down, m_mix_norm, m_w_in, m_pool_w, m_pool_scale, m_w_out, m_ffn2_norm, m_ffn2_w_gate, m_ffn2_w_up, m_ffn2_w_down, m_final_norm, v_ffn1_norm, v_ffn1_w_gate, v_ffn1_w_up, v_ffn1_w_down, v_mix_norm, v_w_in, v_pool_w, v_pool_scale, v_w_out, v_ffn2_norm, v_ffn2_w_gate, v_ffn2_w_up, v_ffn2_w_down, v_final_norm):
    given = dict(x=x, positions=positions, ffn1_norm=ffn1_norm, ffn1_w_gate=ffn1_w_gate, ffn1_w_up=ffn1_w_up, ffn1_w_down=ffn1_w_down, mix_norm=mix_norm, w_in=w_in, pool_w=pool_w, pool_scale=pool_scale, w_out=w_out, ffn2_norm=ffn2_norm, ffn2_w_gate=ffn2_w_gate, ffn2_w_up=ffn2_w_up, ffn2_w_down=ffn2_w_down, final_norm=final_norm, loss_target=loss_target, m_ffn1_norm=m_ffn1_norm, m_ffn1_w_gate=m_ffn1_w_gate, m_ffn1_w_up=m_ffn1_w_up, m_ffn1_w_down=m_ffn1_w_down, m_mix_norm=m_mix_norm, m_w_in=m_w_in, m_pool_w=m_pool_w, m_pool_scale=m_pool_scale, m_w_out=m_w_out, m_ffn2_norm=m_ffn2_norm, m_ffn2_w_gate=m_ffn2_w_gate, m_ffn2_w_up=m_ffn2_w_up, m_ffn2_w_down=m_ffn2_w_down, m_final_norm=m_final_norm, v_ffn1_norm=v_ffn1_norm, v_ffn1_w_gate=v_ffn1_w_gate, v_ffn1_w_up=v_ffn1_w_up, v_ffn1_w_down=v_ffn1_w_down, v_mix_norm=v_mix_norm, v_w_in=v_w_in, v_pool_w=v_pool_w, v_pool_scale=v_pool_scale, v_w_out=v_w_out, v_ffn2_norm=v_ffn2_norm, v_ffn2_w_gate=v_ffn2_w_gate, v_ffn2_w_up=v_ffn2_w_up, v_ffn2_w_down=v_ffn2_w_down, v_final_norm=v_final_norm)
    weights = {n: given[n] for n in TWIN_WEIGHTS}
    shared = {n: given[n] for n in SHARED_INPUTS}
    per_example = {n: given[n] for n in ['x', 'positions']}
    grad_fn = _jax.value_and_grad(_loss, argnums=(0, 1))

    def one_microbatch(ex, loss_target):
        ex = dict(ex)
        diff = ex.pop(TWIN_DIFF_INPUT)
        return grad_fn(weights, diff, {**shared, **ex}, loss_target)

    if N_MICROBATCH == 1:
        loss, (grad_w, grad_x) = one_microbatch(per_example, given["loss_target"])
    else:
        def body(carry, xs):
            loss_sum, grad_sum = carry
            l_k, (gw_k, gx_k) = one_microbatch(xs[0], xs[1])
            with _jax.named_scope("update"):
                return (loss_sum + l_k, _jax.tree.map(_jnp.add, grad_sum, gw_k)), gx_k

        init = (_jnp.zeros((), _jnp.float32), _jax.tree.map(_jnp.zeros_like, weights))
        (loss, grad_w), grad_x = _jax.lax.scan(body, init, (per_example, given["loss_target"]))
    with _jax.named_scope("update"):
        delta_w, new_m, new_v = {}, {}, {}
        for n in TWIN_WEIGHTS:
            delta_w[n], new_m[n], new_v[n] = _adamw(weights[n], grad_w[n], given["m_" + n], given["v_" + n])
    return (loss, grad_x, *[grad_w[n] for n in TWIN_WEIGHTS], *[delta_w[n] for n in TWIN_WEIGHTS],
            *[new_m[n] for n in TWIN_WEIGHTS], *[new_v[n] for n in TWIN_WEIGHTS])
```

```python
import functools

import jax
import jax.numpy as jnp
import numpy as np
from jax import lax
from jax.experimental import pallas as pl
from jax.experimental.pallas import tpu as pltpu

f32 = jnp.float32
bf16 = jnp.bfloat16

S = 2048
D = 1024
DEPTH = 4
NSH = 4
FS = 704
PROJ = 2560
PS = 640
PW = 256
AW = 768
NPAIR = 6
NORM_EPS = 1e-6
MASK_VALUE = -1e30
ROPE_THETA = 500000.0
DILATIONS = (1, 4, 16)
QBLK = 128
NBLK = S // QBLK
TM = 512
VMEM_LIMIT = 56 * 1024 * 1024

ADAM_LR = 0.001
ADAM_B1 = 0.9
ADAM_B2 = 0.999
ADAM_EPS = 1e-08
ADAM_WD = 0.01
ADAM_STEP = 10

MESH = pl.DeviceIdType.MESH
ANY = pl.BlockSpec(memory_space=pl.ANY)

BIG = (("g1", D, FS), ("u1", D, FS), ("d1", FS, D), ("wi", D, PS), ("wo", PW, D), ("g2", D, FS), ("u2", D, FS), ("d2", FS, D))
BIG_SRC = {"g1": "ffn1_w_gate", "u1": "ffn1_w_up", "d1": "ffn1_w_down", "wi": "w_in", "wo": "w_out",
           "g2": "ffn2_w_gate", "u2": "ffn2_w_up", "d2": "ffn2_w_down"}


def _cp(*sem):
    return pltpu.CompilerParams(dimension_semantics=sem if sem else None, vmem_limit_bytes=VMEM_LIMIT)


def _sds(shape, dt):
    return jax.ShapeDtypeStruct(shape, dt)


def _dot(a, b):
    return jnp.dot(a, b, preferred_element_type=f32)


def _dot_nt(a, b):
    return lax.dot_general(a, b, (((1,), (1,)), ((), ())), preferred_element_type=f32)


def _dot_tn(a, b):
    return lax.dot_general(a, b, (((0,), (0,)), ((), ())), preferred_element_type=f32)


def _rms_fwd(x, g):
    def body(x_ref, g_ref, h_ref):
        xf = x_ref[...]
        r = lax.rsqrt(jnp.mean(xf * xf, axis=-1, keepdims=True) + NORM_EPS)
        h_ref[...] = ((xf * r) * g_ref[...]).astype(bf16)

    return pl.pallas_call(
        body, out_shape=_sds((S, D), bf16), grid=(S // TM,),
        in_specs=[pl.BlockSpec((TM, D), lambda i: (i, 0)), pl.BlockSpec((1, D), lambda i: (0, 0))],
        out_specs=pl.BlockSpec((TM, D), lambda i: (i, 0)), name="rms_fwd", compiler_params=_cp("parallel"))(x, g)


def _ffn_up(h, wg, wu):
    def body(h_ref, wg_ref, wu_ref, a_ref, b_ref, u_ref):
        hh = h_ref[...]
        a = _dot(hh, wg_ref[0])
        b = _dot(hh, wu_ref[0])
        a_ref[0] = a
        b_ref[0] = b
        u_ref[0] = (a * (1.0 / (1.0 + jnp.exp(-a))) * b).astype(bf16)

    wspec = pl.BlockSpec((1, D, FS), lambda s, i: (s, 0, 0))
    ospec = pl.BlockSpec((1, TM, FS), lambda s, i: (s, i, 0))
    return pl.pallas_call(
        body, out_shape=(_sds((NSH, S, FS), f32), _sds((NSH, S, FS), f32), _sds((NSH, S, FS), bf16)),
        grid=(NSH, S // TM), in_specs=[pl.BlockSpec((TM, D), lambda s, i: (i, 0)), wspec, wspec],
        out_specs=(ospec, ospec, ospec), name="ffn_up", compiler_params=_cp("parallel", "parallel"))(h, wg, wu)


def _ffn_down(x, u, wd):
    def body(x_ref, u_ref, wd_ref, o_ref, acc_ref):
        s = pl.program_id(1)

        @pl.when(s == 0)
        def _():
            acc_ref[...] = jnp.zeros_like(acc_ref)

        acc_ref[...] += _dot(u_ref[0], wd_ref[0])

        @pl.when(s == NSH - 1)
        def _():
            o_ref[...] = x_ref[...] + 0.5 * acc_ref[...]

    return pl.pallas_call(
        body, out_shape=_sds((S, D), f32), grid=(S // TM, NSH),
        in_specs=[pl.BlockSpec((TM, D), lambda i, s: (i, 0)), pl.BlockSpec((1, TM, FS), lambda i, s: (s, i, 0)),
                  pl.BlockSpec((1, FS, D), lambda i, s: (s, 0, 0))],
        out_specs=pl.BlockSpec((TM, D), lambda i, s: (i, 0)), scratch_shapes=[pltpu.VMEM((TM, D), f32)],
        name="ffn_down", compiler_params=_cp("parallel", "arbitrary"))(x, u, wd)


def _in_proj(h, wi):
    def body(h_ref, w_ref, o_ref):
        o_ref[...] = _dot(h_ref[...], w_ref[0])

    return pl.pallas_call(
        body, out_shape=_sds((S, PROJ), f32), grid=(NSH, S // TM),
        in_specs=[pl.BlockSpec((TM, D), lambda s, i: (i, 0)), pl.BlockSpec((1, D, PS), lambda s, i: (s, 0, 0))],
        out_specs=pl.BlockSpec((TM, PS), lambda s, i: (i, s)), name="in_proj", compiler_params=_cp("parallel", "parallel"))(h, wi)


def _out_proj(x, mixed, wo):
    def body(x_ref, m_ref, w_ref, o_ref):
        o_ref[...] = x_ref[...] + _dot(m_ref[...], w_ref[...].reshape(D, D))

    return pl.pallas_call(
        body, out_shape=_sds((S, D), f32), grid=(S // TM,),
        in_specs=[pl.BlockSpec((TM, D), lambda i: (i, 0)), pl.BlockSpec((TM, D), lambda i: (i, 0)),
                  pl.BlockSpec((NSH, PW, D), lambda i: (0, 0, 0))],
        out_specs=pl.BlockSpec((TM, D), lambda i: (i, 0)), name="out_proj", compiler_params=_cp("parallel"))(x, mixed, wo)


def _out_proj_bwd(dx, wo):
    def body(dx_ref, w_ref, o_ref):
        o_ref[...] = _dot_nt(dx_ref[...].astype(bf16), w_ref[...].reshape(D, D))

    return pl.pallas_call(
        body, out_shape=_sds((S, D), f32), grid=(S // TM,),
        in_specs=[pl.BlockSpec((TM, D), lambda i: (i, 0)), pl.BlockSpec((NSH, PW, D), lambda i: (0, 0, 0))],
        out_specs=pl.BlockSpec((TM, D), lambda i: (i, 0)), name="out_proj_bwd", compiler_params=_cp("parallel"))(dx, wo)


def _ffn_bwd_hidden(dx, wd, a, b):
    def body(dx_ref, wd_ref, a_ref, b_ref, da_ref, db_ref):
        dy = (0.5 * dx_ref[...]).astype(bf16)
        du = _dot_nt(dy, wd_ref[0])
        a = a_ref[0]
        sig = 1.0 / (1.0 + jnp.exp(-a))
        da_ref[0] = (du * b_ref[0] * (sig * (1.0 + a * (1.0 - sig)))).astype(bf16)
        db_ref[0] = (du * (a * sig)).astype(bf16)

    hspec = pl.BlockSpec((1, TM, FS), lambda s, i: (s, i, 0))
    return pl.pallas_call(
        body, out_shape=(_sds((NSH, S, FS), bf16), _sds((NSH, S, FS), bf16)), grid=(NSH, S // TM),
        in_specs=[pl.BlockSpec((TM, D), lambda s, i: (i, 0)), pl.BlockSpec((1, FS, D), lambda s, i: (s, 0, 0)), hspec, hspec],
        out_specs=(hspec, hspec), name="ffn_bwd_hidden", compiler_params=_cp("parallel", "parallel"))(dx, wd, a, b)


def _norm_bwd_tail(acc, x_ref, dxin_ref, g_ref, dxo_ref, dg_ref, first):
    xf = x_ref[...]
    r = lax.rsqrt(jnp.mean(xf * xf, axis=-1, keepdims=True) + NORM_EPS)
    xhat = xf * r
    dhg = acc * g_ref[...]
    dxo_ref[...] = dxin_ref[...] + r * (dhg - xhat * jnp.mean(dhg * xhat, axis=-1, keepdims=True))
    part = jnp.sum(acc * xhat, axis=0, keepdims=True)

    @pl.when(first)
    def _():
        dg_ref[...] = part

    @pl.when(jnp.logical_not(first))
    def _():
        dg_ref[...] += part


def _ffn_bwd_dx(dx, x_in, g, da, db, wg, wu):
    def body(dx_ref, x_ref, g_ref, da_ref, db_ref, wg_ref, wu_ref, dxo_ref, dg_ref, acc_ref):
        i, s = pl.program_id(0), pl.program_id(1)

        @pl.when(s == 0)
        def _():
            acc_ref[...] = jnp.zeros_like(acc_ref)

        acc_ref[...] += _dot_nt(da_ref[0], wg_ref[0]) + _dot_nt(db_ref[0], wu_ref[0])

        @pl.when(s == NSH - 1)
        def _():
            _norm_bwd_tail(acc_ref[...], x_ref, dx_ref, g_ref, dxo_ref, dg_ref, i == 0)

    tok = pl.BlockSpec((TM, D), lambda i, s: (i, 0))
    vec = pl.BlockSpec((1, D), lambda i, s: (0, 0))
    hid = pl.BlockSpec((1, TM, FS), lambda i, s: (s, i, 0))
    wsp = pl.BlockSpec((1, D, FS), lambda i, s: (s, 0, 0))
    return pl.pallas_call(
        body, out_shape=(_sds((S, D), f32), _sds((1, D), f32)), grid=(S // TM, NSH),
        in_specs=[tok, tok, vec, hid, hid, wsp, wsp], out_specs=(tok, vec), scratch_shapes=[pltpu.VMEM((TM, D), f32)],
        name="ffn_bwd_dx", compiler_params=_cp("arbitrary", "arbitrary"))(dx, x_in, g, da, db, wg, wu)


def _in_proj_bwd_dx(dx, x_in, g, dproj, wi):
    def body(dx_ref, x_ref, g_ref, dp_ref, w_ref, dxo_ref, dg_ref, acc_ref):
        i, s = pl.program_id(0), pl.program_id(1)

        @pl.when(s == 0)
        def _():
            acc_ref[...] = jnp.zeros_like(acc_ref)

        acc_ref[...] += _dot_nt(dp_ref[...], w_ref[0])

        @pl.when(s == NSH - 1)
        def _():
            _norm_bwd_tail(acc_ref[...], x_ref, dx_ref, g_ref, dxo_ref, dg_ref, i == 0)

    tok = pl.BlockSpec((TM, D), lambda i, s: (i, 0))
    vec = pl.BlockSpec((1, D), lambda i, s: (0, 0))
    return pl.pallas_call(
        body, out_shape=(_sds((S, D), f32), _sds((1, D), f32)), grid=(S // TM, NSH),
        in_specs=[tok, tok, vec, pl.BlockSpec((TM, PS), lambda i, s: (i, s)), pl.BlockSpec((1, D, PS), lambda i, s: (s, 0, 0))],
        out_specs=(tok, vec), scratch_shapes=[pltpu.VMEM((TM, D), f32)],
        name="in_proj_bwd_dx", compiler_params=_cp("arbitrary", "arbitrary"))(dx, x_in, g, dproj, wi)


def _dw(lhs, rhs, lhs_spec, rhs_spec, rows, cols, name, rhs_scale=None):
    def body(l_ref, r_ref, o_ref):
        l = l_ref[...].reshape(S, rows)
        r = r_ref[...].reshape(S, cols)
        if rhs_scale is not None:
            r = (rhs_scale * r).astype(bf16)
        o_ref[...] = _dot_tn(l, r).astype(bf16).reshape(1, 2, rows // 2, cols)

    return pl.pallas_call(
        body, out_shape=_sds((NSH, 2, rows // 2, cols), bf16), grid=(NSH,), in_specs=[lhs_spec, rhs_spec],
        out_specs=pl.BlockSpec((1, 2, rows // 2, cols), lambda s: (s, 0, 0, 0)), name=name, compiler_params=_cp("parallel"))(lhs, rhs)


_WHOLE_TOK = pl.BlockSpec((S, D), lambda s: (0, 0))
_SHARD_HID = pl.BlockSpec((1, S, FS), lambda s: (s, 0, 0))


def _dw_up(h, da):
    return _dw(h, da, _WHOLE_TOK, _SHARD_HID, D, FS, "dw_up")


def _dw_down(u, dx):
    return _dw(u, dx, _SHARD_HID, _WHOLE_TOK, FS, D, "dw_down", rhs_scale=0.5)


def _dw_in(h, dproj):
    return _dw(h, dproj, _WHOLE_TOK, pl.BlockSpec((S, PS), lambda s: (0, s)), D, PS, "dw_in")


def _dw_out(mixed, dx):
    return _dw(mixed, dx, pl.BlockSpec((S, PW), lambda s: (0, s)), _WHOLE_TOK, PW, D, "dw_out", rhs_scale=1.0)


def _final_loss(x, g, target):
    def body(x_ref, g_ref, t_ref, loss_ref, dx_ref, dg_ref):
        i = pl.program_id(0)
        xf = x_ref[...]
        r = lax.rsqrt(jnp.mean(xf * xf, axis=-1, keepdims=True) + NORM_EPS)
        xhat = xf * r
        err = xhat * g_ref[...] - t_ref[...]
        dy = err * (1.0 / D)
        dhg = dy * g_ref[...]
        dx_ref[...] = r * (dhg - xhat * jnp.mean(dhg * xhat, axis=-1, keepdims=True))
        part = jnp.sum(dy * xhat, axis=0, keepdims=True)
        lpart = jnp.zeros((8, 128), f32) + 0.5 * jnp.sum(jnp.mean(err * err, axis=-1, keepdims=True))

        @pl.when(i == 0)
        def _():
            dg_ref[...] = part
            loss_ref[...] = lpart

        @pl.when(i != 0)
        def _():
            dg_ref[...] += part
            loss_ref[...] += lpart

    tok = pl.BlockSpec((TM, D), lambda i: (i, 0))
    vec = pl.BlockSpec((1, D), lambda i: (0, 0))
    return pl.pallas_call(
        body, out_shape=(_sds((8, 128), f32), _sds((S, D), f32), _sds((1, D), f32)), grid=(S // TM,),
        in_specs=[tok, vec, tok], out_specs=(pl.BlockSpec((8, 128), lambda i: (0, 0)), tok, vec),
        name="final_loss", compiler_params=_cp("arbitrary"))(x, g, target)


def _shift_down(x, k, row):
    return jnp.where(row >= k, pltpu.roll(x, k, axis=0), 0.0)


def _shift_up(x, k, row):
    return jnp.where(row < S - k, pltpu.roll(x, S - k, axis=0), 0.0)


def _pool_geometry():
    row = lax.broadcasted_iota(jnp.int32, (S, PW), 0)
    grp = lax.broadcasted_iota(jnp.int32, (S, PW), 1) // 64
    half = jnp.where(grp == 0, 1, jnp.where(grp == 1, 2, jnp.where(grp == 2, 4, 8)))
    hi = jnp.minimum(row + half - 1, S - 1)
    lo = jnp.maximum(row - half, 0)
    return row, grp, (hi - lo + 1).astype(f32)


def _by_group(grp, v0, v1, v2, v3):
    return jnp.where(grp == 0, v0, jnp.where(grp == 1, v1, jnp.where(grp == 2, v2, v3)))


def _window_sums(x, row, grp, transpose):
    l1, r1 = x, x
    l2, r2 = l1 + _shift_down(l1, 1, row), r1 + _shift_up(r1, 1, row)
    l4, r4 = l2 + _shift_down(l2, 2, row), r2 + _shift_up(r2, 2, row)
    l8, r8 = l4 + _shift_down(l4, 4, row), r4 + _shift_up(r4, 4, row)
    lsel = _by_group(grp, l1, l2, l4, l8)
    rsel = _by_group(grp, r1, r2, r4, r8)
    if transpose:
        return lsel + _shift_up(rsel, 1, row)
    return _shift_down(lsel, 1, row) + rsel


def _pool_fwd(proj, wbd, scale):
    def body(v_ref, w_ref, sc_ref, mixed_ref, diff_ref):
        row, grp, cnt = _pool_geometry()
        v = v_ref[...]
        diff = (_window_sums(v, row, grp, False) / cnt - v).astype(bf16)
        diff_ref[...] = diff
        mixed_ref[...] = (_dot(diff, w_ref[...].astype(bf16)) * sc_ref[...]).astype(bf16)

    col = pl.BlockSpec((S, PW), lambda i: (0, 0))
    return pl.pallas_call(
        body, out_shape=(_sds((S, D), bf16), _sds((S, PW), bf16)), grid=(1,),
        in_specs=[col, pl.BlockSpec((PW, PW), lambda i: (0, 0)), pl.BlockSpec((1, PW), lambda i: (0, 0))],
        out_specs=(col, col), name="pool_fwd", compiler_params=_cp("arbitrary"))(proj, wbd, scale)


def _pool_bwd(dmixed, diff, wbd, scale, dproj):
    def body(dy_ref, diff_ref, w_ref, sc_ref, dproj_in, dv_ref, dw_ref, dsc_ref):
        del dproj_in
        row, grp, cnt = _pool_geometry()
        dy = dy_ref[...]
        diff = diff_ref[...]
        w = w_ref[...].astype(bf16)
        dsc_ref[...] = jnp.sum(dy * _dot(diff, w), axis=0, keepdims=True)
        dys = (dy * sc_ref[...]).astype(bf16)
        dw_ref[...] = _dot_tn(diff, dys)
        ddiff = _dot_nt(dys, w)
        dv_ref[...] = (_window_sums(ddiff / cnt, row, grp, True) - ddiff).astype(bf16)

    col = pl.BlockSpec((S, PW), lambda i: (0, 0))
    return pl.pallas_call(
        body, out_shape=(_sds((S, PROJ), bf16), _sds((PW, PW), f32), _sds((1, PW), f32)), grid=(1,),
        in_specs=[col, col, pl.BlockSpec((PW, PW), lambda i: (0, 0)), pl.BlockSpec((1, PW), lambda i: (0, 0)), ANY],
        out_specs=(col, pl.BlockSpec((PW, PW), lambda i: (0, 0)), pl.BlockSpec((1, PW), lambda i: (0, 0))),
        input_output_aliases={4: 0}, name="pool_bwd", compiler_params=_cp("arbitrary"))(dmixed, diff, wbd, scale, dproj)


def _rope_tables(pos_col, freq_row):
    def body(p_ref, f_ref, c_ref, a_ref, b_ref):
        ang = p_ref[...].astype(f32) * f_ref[...]
        l64 = lax.broadcasted_iota(jnp.int32, (S, 128), 1) % 64
        cos, sin = jnp.cos(ang), jnp.sin(ang)
        c_ref[...] = jnp.where(l64 < 16, cos, 1.0)
        a_ref[...] = jnp.where(l64 < 8, -sin, 0.0)
        b_ref[...] = jnp.where((l64 >= 8) & (l64 < 16), sin, 0.0)

    t = _sds((S, 128), f32)
    return pl.pallas_call(body, out_shape=(t, t, t), name="rope_tables", compiler_params=_cp())(pos_col, freq_row)


def _rope(t, c, a, b):
    return t * c + pltpu.roll(t, 120, axis=1) * a + pltpu.roll(t, 8, axis=1) * b


def _rope_bwd(g, c, a, b):
    return g * c + pltpu.roll(g * a, 8, axis=1) + pltpu.roll(g * b, 120, axis=1)


def _perm_load(ref, d):
    if d == 1:
        return ref[...]
    n = S // d
    return jnp.concatenate([ref[pl.ds(r, n, stride=d), :] for r in range(d)], axis=0)


def _unperm_store(ref, val, d):
    if d == 1:
        ref[...] = val
        return
    n = S // d
    for r in range(d):
        ref[pl.ds(r, n, stride=d), :] = val[r * n:(r + 1) * n, :]


def _band(xp):
    z = jnp.zeros((64, 128), bf16)
    p = jnp.concatenate([z, xp, z], axis=0).reshape(NBLK + 1, QBLK, 128)
    return jnp.concatenate([p[:NBLK], p[1:]], axis=1)


def _unband(xb):
    z = jnp.zeros((1, QBLK, 128), f32)
    p = jnp.concatenate([xb[:, :QBLK], z], axis=0) + jnp.concatenate([z, xb[:, QBLK:]], axis=0)
    return p.reshape(S + QBLK, 128)[64:S + 64]


def _band_mask(d):
    blocks_per_class = NBLK // d
    n = lax.broadcasted_iota(jnp.int32, (NBLK, 1, 2 * QBLK), 0) & (blocks_per_class - 1)
    be = lax.broadcasted_iota(jnp.int32, (NBLK, 1, 2 * QBLK), 2)
    a = lax.broadcasted_iota(jnp.int32, (1, 2 * QBLK, 2 * QBLK), 1) & (QBLK - 1)
    b = lax.broadcasted_iota(jnp.int32, (1, 2 * QBLK, 2 * QBLK), 2)
    band = (b >= a) & (b <= a + 128)
    edge = ((be >= 64) | (n != 0)) & ((be < QBLK + 64) | (n != blocks_per_class - 1))
    return band & edge


def _stack_heads(xb, lo):
    z = jnp.zeros_like(xb)
    return jnp.concatenate([jnp.where(lo, xb, z), jnp.where(lo, z, xb)], axis=1)


def _unstack_heads(x2, lo):
    return jnp.where(lo, x2[:, :QBLK], x2[:, QBLK:])


def _rows_to_lanes(col2, lo):
    return jnp.where(lo, jnp.broadcast_to(col2[:, :QBLK], (NBLK, QBLK, 128)), jnp.broadcast_to(col2[:, QBLK:], (NBLK, QBLK, 128)))


def _bmm_nt(a, b):
    return jnp.einsum('nqd,nkd->nqk', a, b, preferred_element_type=f32)


def _bmm_nn(a, b):
    return jnp.einsum('nqk,nkd->nqd', a, b, preferred_element_type=f32)


def _bmm_tn(a, b):
    return jnp.einsum('nqk,nqd->nkd', a, b, preferred_element_type=f32)


def _attn_fwd(proj, tc, ta, tb, mixed):
    def body(q_ref, k_ref, v_ref, c_ref, a_ref, b_ref, mixed_in, mixed_ref, o_ref, lse_ref, qn, kn, t_num, t_m, t_den):
        del mixed_in
        lo = lax.broadcasted_iota(jnp.int32, (1, 1, 128), 2) < 64
        c, a, b = c_ref[...], a_ref[...], b_ref[...]
        qn[...] = _rope(q_ref[...], c, a, b)
        kn[...] = _rope(k_ref[...], c, a, b)
        run = None
        for d in DILATIONS:
            q2 = _stack_heads(_perm_load(qn, d).astype(bf16).reshape(NBLK, QBLK, 128), lo)
            kb = _band(_perm_load(kn, d).astype(bf16))
            vb = _band(_perm_load(v_ref, d).astype(bf16))
            s = jnp.where(_band_mask(d), _bmm_nt(q2, kb) * 0.125, MASK_VALUE)
            m = jnp.max(s, axis=2, keepdims=True)
            p = jnp.exp(s - m)
            den = jnp.sum(p, axis=2, keepdims=True)
            num = _unstack_heads(_bmm_nn(p.astype(bf16), vb), lo)
            _unperm_store(t_num, num.reshape(S, 128), d)
            _unperm_store(t_m, _rows_to_lanes(m, lo).reshape(S, 128), d)
            _unperm_store(t_den, _rows_to_lanes(den, lo).reshape(S, 128), d)
            if run is None:
                run = (t_m[...], t_num[...], t_den[...])
            else:
                m_new = jnp.maximum(run[0], t_m[...])
                w_old, w_new = jnp.exp(run[0] - m_new), jnp.exp(t_m[...] - m_new)
                run = (m_new, w_old * run[1] + w_new * t_num[...], w_old * run[2] + w_new * t_den[...])
        out = run[1] / run[2]
        o_ref[...] = out
        mixed_ref[...] = out.astype(bf16)
        lse_ref[...] = run[0] + jnp.log(run[2])

    def col(off):
        return pl.BlockSpec((S, 128), lambda j, off=off: (0, off + j))

    tab = pl.BlockSpec((S, 128), lambda j: (0, 0))
    scr = pltpu.VMEM((S, 128), f32)
    return pl.pallas_call(
        body, out_shape=(_sds((S, D), bf16), _sds((S, AW), f32), _sds((S, AW), f32)), grid=(NPAIR,),
        in_specs=[col(2), col(8), col(14), tab, tab, tab, ANY], out_specs=(col(2), col(0), col(0)),
        scratch_shapes=[scr, scr, scr, scr, scr], input_output_aliases={6: 0}, name="attn_fwd",
        compiler_params=_cp("arbitrary"))(proj, proj, proj, tc, ta, tb, mixed)


def _attn_bwd(proj, tc, ta, tb, o, lse, dmixed):
    def body(q_ref, k_ref, v_ref, c_ref, a_ref, b_ref, o_ref, lse_ref, do_ref, dp_ref, qn, kn, tmp, dk_s, dv_s):
        t = pl.program_id(1)

        @pl.when(t == 0)
        def _():
            lo = lax.broadcasted_iota(jnp.int32, (1, 1, 128), 2) < 64
            c, a, b = c_ref[...], a_ref[...], b_ref[...]
            qn[...] = _rope(q_ref[...], c, a, b)
            kn[...] = _rope(k_ref[...], c, a, b)
            dq = dk = dv = None
            for d in DILATIONS:
                q2 = _stack_heads(_perm_load(qn, d).astype(bf16).reshape(NBLK, QBLK, 128), lo)
                kb = _band(_perm_load(kn, d).astype(bf16))
                vb = _band(_perm_load(v_ref, d).astype(bf16))
                dob = _perm_load(do_ref, d).reshape(NBLK, QBLK, 128)
                ob = _perm_load(o_ref, d).reshape(NBLK, QBLK, 128)
                lsb = _perm_load(lse_ref, d).reshape(NBLK, QBLK, 128)
                do2 = _stack_heads(dob.astype(bf16), lo)
                delta2 = jnp.sum(_stack_heads(dob * ob, lo), axis=2, keepdims=True)
                lse2 = jnp.max(jnp.concatenate([jnp.where(lo, lsb, MASK_VALUE), jnp.where(lo, MASK_VALUE, lsb)], axis=1),
                               axis=2, keepdims=True)
                s = _bmm_nt(q2, kb) * 0.125
                p = jnp.where(_band_mask(d), jnp.exp(s - lse2), 0.0)
                ds = (p * (_bmm_nt(do2, vb) - delta2) * 0.125).astype(bf16)
                pb = p.astype(bf16)
                dq_b = _unstack_heads(_bmm_nn(ds, kb), lo).reshape(S, 128)
                dk_b = _unband(_bmm_tn(ds, q2))
                dv_b = _unband(_bmm_tn(pb, do2))
                acc = []
                for prev, new in ((dq, dq_b), (dk, dk_b), (dv, dv_b)):
                    _unperm_store(tmp, new, d)
                    acc.append(tmp[...] if prev is None else prev + tmp[...])
                dq, dk, dv = acc
            dp_ref[...] = _rope_bwd(dq, c, a, b).astype(bf16)
            dk_s[...] = _rope_bwd(dk, c, a, b).astype(bf16)
            dv_s[...] = dv.astype(bf16)

        @pl.when(t == 1)
        def _():
            dp_ref[...] = dk_s[...]

        @pl.when(t == 2)
        def _():
            dp_ref[...] = dv_s[...]

    def col(off):
        return pl.BlockSpec((S, 128), lambda j, t, off=off: (0, off + j))

    tab = pl.BlockSpec((S, 128), lambda j, t: (0, 0))
    scr = pltpu.VMEM((S, 128), f32)
    scb = pltpu.VMEM((S, 128), bf16)
    return pl.pallas_call(
        body, out_shape=_sds((S, PROJ), bf16), grid=(NPAIR, 3),
        in_specs=[col(2), col(8), col(14), tab, tab, tab, col(0), col(0), col(2)],
        out_specs=pl.BlockSpec((S, 128), lambda j, t: (0, 2 + NPAIR * t + j)),
        scratch_shapes=[scr, scr, scr, scb, scb], name="attn_bwd",
        compiler_params=_cp("arbitrary", "arbitrary"))(proj, proj, proj, tc, ta, tb, o, lse, dmixed)


def _block_diag(w4):
    out = jnp.zeros((PW, PW), w4.dtype)
    for g in range(4):
        out = out.at[64 * g:64 * (g + 1), 64 * g:64 * (g + 1)].set(w4[g])
    return out


def _diag_blocks(w):
    return jnp.stack([w[64 * g:64 * (g + 1), 64 * g:64 * (g + 1)] for g in range(4)])


def _rope_inputs(positions):
    inv_freq = ROPE_THETA ** (-jnp.arange(0, 16, 2, dtype=f32) / 16)
    l64 = np.arange(128) % 64
    idx = np.where(l64 < 16, l64 % 8, 0)
    return positions.reshape(S, 1), inv_freq[idx].reshape(1, 128)


def _forward_backward(x, positions, target, gathered, small):
    tc, ta, tb = _rope_tables(*_rope_inputs(positions))
    saved = []
    for l in range(DEPTH):
        w = gathered[l]
        g1, gm, g2 = (small[k][l].reshape(1, D) for k in ("ffn1_norm", "mix_norm", "ffn2_norm"))
        wbd = _block_diag(small["pool_w"][l])
        psc = small["pool_scale"][l].reshape(1, PW)
        x0 = x
        h1 = _rms_fwd(x0, g1)
        a1, b1, u1 = _ffn_up(h1, w["g1"], w["u1"])
        x1 = _ffn_down(x0, u1, w["d1"])
        h2 = _rms_fwd(x1, gm)
        proj = _in_proj(h2, w["wi"])
        mixed, diff = _pool_fwd(proj, wbd, psc)
        mixed, o, lse = _attn_fwd(proj, tc, ta, tb, mixed)
        x2 = _out_proj(x1, mixed, w["wo"])
        h3 = _rms_fwd(x2, g2)
        a2, b2, u2 = _ffn_up(h3, w["g2"], w["u2"])
        x = _ffn_down(x2, u2, w["d2"])
        saved.append(dict(x0=x0, h1=h1, a1=a1, b1=b1, u1=u1, x1=x1, h2=h2, proj=proj, mixed=mixed, diff=diff, o=o,
                          lse=lse, x2=x2, h3=h3, a2=a2, b2=b2, u2=u2, g1=g1, gm=gm, g2=g2, wbd=wbd, psc=psc))
    loss, dx, dgf = _final_loss(x, small["final_norm"].reshape(1, D), target)

    big = [None] * DEPTH
    sg = {k: [None] * DEPTH for k in ("ffn1_norm", "mix_norm", "pool_w", "pool_scale", "ffn2_norm")}
    for l in reversed(range(DEPTH)):
        w, sv = gathered[l], saved[l]
        gr = {}
        da, db = _ffn_bwd_hidden(dx, w["d2"], sv["a2"], sv["b2"])
        gr["d2"] = _dw_down(sv["u2"], dx)
        gr["g2"] = _dw_up(sv["h3"], da)
        gr["u2"] = _dw_up(sv["h3"], db)
        dx, sg["ffn2_norm"][l] = _ffn_bwd_dx(dx, sv["x2"], sv["g2"], da, db, w["g2"], w["u2"])
        gr["wo"] = _dw_out(sv["mixed"], dx)
        dmixed = _out_proj_bwd(dx, w["wo"])
        dproj = _attn_bwd(sv["proj"], tc, ta, tb, sv["o"], sv["lse"], dmixed)
        dproj, dwbd, dpsc = _pool_bwd(dmixed, sv["diff"], sv["wbd"], sv["psc"], dproj)
        sg["pool_w"][l] = _diag_blocks(dwbd)
        sg["pool_scale"][l] = dpsc
        gr["wi"] = _dw_in(sv["h2"], dproj)
        dx, sg["mix_norm"][l] = _in_proj_bwd_dx(dx, sv["x1"], sv["gm"], dproj, w["wi"])
        da, db = _ffn_bwd_hidden(dx, w["d1"], sv["a1"], sv["b1"])
        gr["d1"] = _dw_down(sv["u1"], dx)
        gr["g1"] = _dw_up(sv["h1"], da)
        gr["u1"] = _dw_up(sv["h1"], db)
        dx, sg["ffn1_norm"][l] = _ffn_bwd_dx(dx, sv["x0"], sv["g1"], da, db, w["g1"], w["u1"])
        big[l] = gr
    sg["final_norm"] = dgf
    return loss, dx, big, sg


def _place():
    x, y, c = lax.axis_index("x"), lax.axis_index("y"), lax.axis_index("c")
    chips = [(1 - x, y), (x, 1 - y), (1 - x, 1 - y)]
    return x, y, c, chips


def _cast_layer(params, l):
    def body(*refs):
        for i_ref, o_ref in zip(refs[:8], refs[8:]):
            o_ref[...] = i_ref[...].astype(bf16)

    ins, in_specs, out_shape, out_specs = [], [], [], []
    for name, rows, cols in BIG:
        q = rows // 4
        ins.append(params[BIG_SRC[name]])
        in_specs.append(pl.BlockSpec((1, q, cols), lambda i, l=l: (l, i, 0)))
        out_shape.append(_sds((2, rows // 2, cols), bf16))
        out_specs.append(pl.BlockSpec((1, q, cols), lambda i: (i // 2, i % 2, 0)))
    return pl.pallas_call(body, out_shape=out_shape, grid=(4,), in_specs=in_specs, out_specs=out_specs,
                          name=f"cast_layer{l}", compiler_params=_cp("parallel"))(*ins)


def _gather_layer(local):
    n = len(local)

    def body(*refs):
        ins, outs = refs[:n], refs[n:2 * n]
        send_sems, recv_sems, local_sems = refs[2 * n:]
        x, y, c, chips = _place()
        me = 2 * x + y
        sibling = (x, y, 1 - c)
        own, sends, passed = [], [], []
        for t in range(n):
            cp = pltpu.make_async_copy(ins[t], outs[t].at[me], local_sems.at[t])
            cp.start()
            own.append(cp)
            for k, chip in enumerate(chips):
                cp = pltpu.make_async_remote_copy(
                    src_ref=ins[t].at[c], dst_ref=outs[t].at[me, c], send_sem=send_sems.at[t, k], recv_sem=recv_sems.at[t, k],
                    device_id=(chip[0], chip[1], c), device_id_type=MESH)
                cp.start()
                sends.append(cp)
        for t in range(n):
            for k, chip in enumerate(chips):
                blk = outs[t].at[2 * chip[0] + chip[1], c]
                pltpu.make_async_remote_copy(
                    src_ref=blk, dst_ref=blk, send_sem=send_sems.at[t, k], recv_sem=recv_sems.at[t, k],
                    device_id=(chip[0], chip[1], c), device_id_type=MESH).wait_recv()
                cp = pltpu.make_async_remote_copy(
                    src_ref=blk, dst_ref=blk, send_sem=send_sems.at[t, 3 + k], recv_sem=recv_sems.at[t, 3 + k],
                    device_id=sibling, device_id_type=MESH)
                cp.start()
                passed.append(cp)
        for t in range(n):
            for k, chip in enumerate(chips):
                blk = outs[t].at[2 * chip[0] + chip[1], 1 - c]
                pltpu.make_async_remote_copy(
                    src_ref=blk, dst_ref=blk, send_sem=send_sems.at[t, 3 + k], recv_sem=recv_sems.at[t, 3 + k],
                    device_id=sibling, device_id_type=MESH).wait_recv()
        for cp in sends + passed:
            cp.wait_send()
        for cp in own:
            cp.wait()

    out_shape = [_sds((NSH,) + a.shape, bf16) for a in local]
    return pl.pallas_call(
        body, out_shape=out_shape, in_specs=[ANY] * n, out_specs=[ANY] * n,
        scratch_shapes=[pltpu.SemaphoreType.DMA((n, 6)), pltpu.SemaphoreType.DMA((n, 6)), pltpu.SemaphoreType.DMA((n,))],
        name="gather_layer")(*local)


def _sibling_swap(grads):
    n = len(grads)

    def body(*refs):
        ins, outs = refs[:n], refs[n:2 * n]
        send_sems, recv_sems = refs[2 * n:]
        x, y, c, _ = _place()
        cps = []
        for t in range(n):
            for s in range(NSH):
                cp = pltpu.make_async_remote_copy(
                    src_ref=ins[t].at[s, 1 - c], dst_ref=outs[t].at[s], send_sem=send_sems.at[t, s], recv_sem=recv_sems.at[t, s],
                    device_id=(x, y, 1 - c), device_id_type=MESH)
                cp.start()
                cps.append(cp)
        for cp in cps:
            cp.wait()

    out_shape = [_sds((NSH,) + a.shape[2:], bf16) for a in grads]
    return pl.pallas_call(
        body, out_shape=out_shape, in_specs=[ANY] * n, out_specs=[ANY] * n,
        scratch_shapes=[pltpu.SemaphoreType.DMA((n, NSH)), pltpu.SemaphoreType.DMA((n, NSH))],
        name="sibling_swap")(*grads)


def _row_tile(h):
    return h // 2 if h % 32 == 0 else h


def _pair_sum(grads, got, c_idx):
    n = len(grads)

    def body(c_ref, *refs):
        del c_ref
        for t in range(n):
            refs[2 * n + t][...] = (refs[t][...].astype(f32).reshape(refs[n + t].shape) + refs[n + t][...].astype(f32)).astype(bf16)

    in_specs, out_shape, out_specs = [], [], []
    for a in grads:
        h, cols = a.shape[2:]
        in_specs.append(pl.BlockSpec((1, 1, _row_tile(h), cols), lambda s, i, c: (s, c[0], i, 0)))
    for a in grads:
        h, cols = a.shape[2:]
        in_specs.append(pl.BlockSpec((1, _row_tile(h), cols), lambda s, i, c: (s, i, 0)))
        out_shape.append(_sds((NSH, h, cols), bf16))
        out_specs.append(pl.BlockSpec((1, _row_tile(h), cols), lambda s, i, c: (s, i, 0)))
    return pl.pallas_call(
        body, out_shape=out_shape,
        grid_spec=pltpu.PrefetchScalarGridSpec(num_scalar_prefetch=1, grid=(NSH, 2), in_specs=in_specs, out_specs=out_specs),
        name="pair_sum", compiler_params=_cp("parallel", "parallel"))(c_idx, *grads, *got)


def _chip_exchange(psum):
    n = len(psum)

    def body(*refs):
        ins, outs = refs[:n], refs[n:2 * n]
        send_sems, recv_sems, local_sems = refs[2 * n:]
        x, y, c, chips = _place()
        me = 2 * x + y
        cps, own = [], []
        for t in range(n):
            cp = pltpu.make_async_copy(ins[t].at[me], outs[t].at[me], local_sems.at[t])
            cp.start()
            own.append(cp)
            for k, chip in enumerate(chips):
                cp = pltpu.make_async_remote_copy(
                    src_ref=ins[t].at[2 * chip[0] + chip[1]], dst_ref=outs[t].at[me], send_sem=send_sems.at[t, k],
                    recv_sem=recv_sems.at[t, k], device_id=(chip[0], chip[1], c), device_id_type=MESH)
                cp.start()
                cps.append(cp)
        for t in range(n):
            for k, chip in enumerate(chips):
                blk = outs[t].at[2 * chip[0] + chip[1]]
                pltpu.make_async_remote_copy(
                    src_ref=blk, dst_ref=blk, send_sem=send_sems.at[t, k], recv_sem=recv_sems.at[t, k],
                    device_id=(chip[0], chip[1], c), device_id_type=MESH).wait_recv()
        for cp in cps:
            cp.wait_send()
        for cp in own:
            cp.wait()

    out_shape = [_sds(a.shape, bf16) for a in psum]
    return pl.pallas_call(
        body, out_shape=out_shape, in_specs=[ANY] * n, out_specs=[ANY] * n,
        scratch_shapes=[pltpu.SemaphoreType.DMA((n, 3)), pltpu.SemaphoreType.DMA((n, 3)), pltpu.SemaphoreType.DMA((n,))],
        name="chip_exchange")(*psum)


def _chip_sum(parts):
    n = len(parts)

    def body(*refs):
        s = pl.program_id(1)
        for t in range(n):
            @pl.when(s == 0)
            def _(t=t):
                refs[n + t][...] = refs[t][0].astype(f32)

            @pl.when(s != 0)
            def _(t=t):
                refs[n + t][...] += refs[t][0].astype(f32)

    in_specs, out_shape, out_specs = [], [], []
    for a in parts:
        _, h, cols = a.shape
        r = _row_tile(h)
        in_specs.append(pl.BlockSpec((1, r, cols), lambda i, s: (s, i, 0)))
        out_shape.append(_sds((h, cols), f32))
        out_specs.append(pl.BlockSpec((r, cols), lambda i, s: (i, 0)))
    return pl.pallas_call(body, out_shape=out_shape, grid=(2, NSH), in_specs=in_specs, out_specs=out_specs,
                          name="chip_sum", compiler_params=_cp("parallel", "arbitrary"))(*parts)


def _sibling_share(halves, full, l):
    n = len(halves)

    def body(*refs):
        ins, outs = refs[:n], refs[2 * n:3 * n]
        send_sems, recv_sems, local_sems = refs[3 * n:]
        x, y, c, _ = _place()
        sibling = (x, y, 1 - c)
        cps = []
        for t in range(n):
            cp = pltpu.make_async_copy(ins[t], outs[t].at[l, c], local_sems.at[t])
            cp.start()
            cps.append(cp)
            cp = pltpu.make_async_remote_copy(
                src_ref=ins[t], dst_ref=outs[t].at[l, c], send_sem=send_sems.at[t], recv_sem=recv_sems.at[t],
                device_id=sibling, device_id_type=MESH)
            cp.start()
            cps.append(cp)
        for t in range(n):
            blk = outs[t].at[l, 1 - c]
            pltpu.make_async_remote_copy(
                src_ref=blk, dst_ref=blk, send_sem=send_sems.at[t], recv_sem=recv_sems.at[t],
                device_id=sibling, device_id_type=MESH).wait_recv()
        for t in range(n):
            cps[2 * t].wait()
            cps[2 * t + 1].wait_send()

    out_shape = [_sds(a.shape, f32) for a in full]
    return pl.pallas_call(
        body, out_shape=out_shape, in_specs=[ANY] * (2 * n), out_specs=[ANY] * n,
        input_output_aliases={n + t: t for t in range(n)},
        scratch_shapes=[pltpu.SemaphoreType.DMA((n,)), pltpu.SemaphoreType.DMA((n,)), pltpu.SemaphoreType.DMA((n,))],
        name=f"sibling_share{l}")(*halves, *full)


SMALL_ROWS = 632


def _pack_small(per_layer, final_vec, loss_row):
    rows = []
    for l in range(DEPTH):
        for k in ("ffn1_norm", "mix_norm", "ffn2_norm"):
            rows.append(per_layer[k][l].reshape(8, 128))
        rows.append(per_layer["pool_w"][l].reshape(128, 128))
        rows.append(per_layer["pool_scale"][l].reshape(2, 128))
    rows.append(final_vec.reshape(8, 128))
    rows.append(loss_row.reshape(1, 128))
    used = DEPTH * (24 + 128 + 2) + 9
    rows.append(jnp.zeros((SMALL_ROWS - used, 128), f32))
    return jnp.concatenate(rows, axis=0)


def _unpack_small(buf):
    out = {k: [] for k in ("ffn1_norm", "mix_norm", "ffn2_norm", "pool_w", "pool_scale")}
    r = 0
    for l in range(DEPTH):
        for k in ("ffn1_norm", "mix_norm", "ffn2_norm"):
            out[k].append(buf[r:r + 8].reshape(D))
            r += 8
        out["pool_w"].append(buf[r:r + 128].reshape(4, 64, 64))
        r += 128
        out["pool_scale"].append(buf[r:r + 2].reshape(PW))
        r += 2
    res = {k: jnp.stack(v) for k, v in out.items()}
    res["final_norm"] = buf[r:r + 8].reshape(D)
    res["loss"] = buf[r + 8, 0]
    return res


def _allreduce_small(buf):
    def body(in_ref, out_ref, slots, send_sems, recv_sems):
        x, y, c, _ = _place()
        me = 4 * x + 2 * y + c
        slots[me] = in_ref[...]
        peers = []
        for k in range(1, 8):
            px, py, pc = x ^ (k >> 2), y ^ ((k >> 1) & 1), c ^ (k & 1)
            cp = pltpu.make_async_remote_copy(
                src_ref=in_ref, dst_ref=slots.at[me], send_sem=send_sems.at[k - 1], recv_sem=recv_sems.at[k - 1],
                device_id=(px, py, pc), device_id_type=MESH)
            cp.start()
            peers.append(cp)
        for k in range(1, 8):
            px, py, pc = x ^ (k >> 2), y ^ ((k >> 1) & 1), c ^ (k & 1)
            slot = 4 * px + 2 * py + pc
            pltpu.make_async_remote_copy(
                src_ref=slots.at[slot], dst_ref=slots.at[slot], send_sem=send_sems.at[k - 1], recv_sem=recv_sems.at[k - 1],
                device_id=(px, py, pc), device_id_type=MESH).wait_recv()
        for cp in peers:
            cp.wait_send()
        acc = slots[0]
        for j in range(1, 8):
            acc = acc + slots[j]
        out_ref[...] = acc

    return pl.pallas_call(
        body, out_shape=_sds((SMALL_ROWS, 128), f32),
        in_specs=[pl.BlockSpec(memory_space=pltpu.VMEM)], out_specs=pl.BlockSpec(memory_space=pltpu.VMEM),
        scratch_shapes=[pltpu.VMEM((8, SMALL_ROWS, 128), f32), pltpu.SemaphoreType.DMA((7,)), pltpu.SemaphoreType.DMA((7,))],
        name="allreduce_small", compiler_params=_cp())(buf)


def _adamw_math(w, g, m, v):
    m = ADAM_B1 * m + (1.0 - ADAM_B1) * g
    v = ADAM_B2 * v + (1.0 - ADAM_B2) * (g * g)
    m_hat = m / (1.0 - ADAM_B1 ** ADAM_STEP)
    v_hat = v / (1.0 - ADAM_B2 ** ADAM_STEP)
    return -ADAM_LR * (m_hat / (jnp.sqrt(v_hat) + ADAM_EPS) + ADAM_WD * w), m, v


def _adamw(w, g, m, v, name):
    def body(w_ref, g_ref, m_ref, v_ref, go_ref, d_ref, mo_ref, vo_ref):
        g = g_ref[...]
        d, mn, vn = _adamw_math(w_ref[...], g, m_ref[...], v_ref[...])
        go_ref[...] = g
        d_ref[...] = d
        mo_ref[...] = mn
        vo_ref[...] = vn

    n, rows, cols = w.shape
    r = rows // 4 if rows % 32 == 0 else rows
    spec = pl.BlockSpec((1, r, cols), lambda i, j: (i, j, 0))
    out = _sds(w.shape, f32)
    return pl.pallas_call(body, out_shape=(out, out, out, out), grid=(n, rows // r), in_specs=[spec] * 4,
                          out_specs=(spec,) * 4, name=name, compiler_params=_cp("parallel", "parallel"))(w, g, m, v)


SMALL_NAMES = ("ffn1_norm", "mix_norm", "pool_w", "pool_scale", "ffn2_norm", "final_norm")
WEIGHT_ORDER = ("ffn1_norm", "ffn1_w_gate", "ffn1_w_up", "ffn1_w_down", "mix_norm", "w_in", "pool_w", "pool_scale", "w_out",
                "ffn2_norm", "ffn2_w_gate", "ffn2_w_up", "ffn2_w_down", "final_norm")


def _pack_small_params(p):
    per_layer = {k: [p[k][l] for l in range(DEPTH)] for k in ("ffn1_norm", "mix_norm", "ffn2_norm", "pool_w", "pool_scale")}
    return _pack_small(per_layer, p["final_norm"], jnp.zeros((128,), f32))


def kernel(x, positions, ffn1_norm, ffn1_w_gate, ffn1_w_up, ffn1_w_down, mix_norm, w_in, pool_w, pool_scale, w_out, ffn2_norm, ffn2_w_gate, ffn2_w_up, ffn2_w_down, final_norm, loss_target, m_ffn1_norm, m_ffn1_w_gate, m_ffn1_w_up, m_ffn1_w_down, m_mix_norm, m_w_in, m_pool_w, m_pool_scale, m_w_out, m_ffn2_norm, m_ffn2_w_gate, m_ffn2_w_up, m_ffn2_w_down, m_final_norm, v_ffn1_norm, v_ffn1_w_gate, v_ffn1_w_up, v_ffn1_w_down, v_mix_norm, v_w_in, v_pool_w, v_pool_scale, v_w_out, v_ffn2_norm, v_ffn2_w_gate, v_ffn2_w_up, v_ffn2_w_down, v_final_norm):
    params = dict(ffn1_norm=ffn1_norm, ffn1_w_gate=ffn1_w_gate, ffn1_w_up=ffn1_w_up, ffn1_w_down=ffn1_w_down,
                  mix_norm=mix_norm, w_in=w_in, pool_w=pool_w, pool_scale=pool_scale, w_out=w_out, ffn2_norm=ffn2_norm,
                  ffn2_w_gate=ffn2_w_gate, ffn2_w_up=ffn2_w_up, ffn2_w_down=ffn2_w_down, final_norm=final_norm)
    mom_m = dict(ffn1_norm=m_ffn1_norm, ffn1_w_gate=m_ffn1_w_gate, ffn1_w_up=m_ffn1_w_up, ffn1_w_down=m_ffn1_w_down,
                 mix_norm=m_mix_norm, w_in=m_w_in, pool_w=m_pool_w, pool_scale=m_pool_scale, w_out=m_w_out,
                 ffn2_norm=m_ffn2_norm, ffn2_w_gate=m_ffn2_w_gate, ffn2_w_up=m_ffn2_w_up, ffn2_w_down=m_ffn2_w_down,
                 final_norm=m_final_norm)
    mom_v = dict(ffn1_norm=v_ffn1_norm, ffn1_w_gate=v_ffn1_w_gate, ffn1_w_up=v_ffn1_w_up, ffn1_w_down=v_ffn1_w_down,
                 mix_norm=v_mix_norm, w_in=v_w_in, pool_w=v_pool_w, pool_scale=v_pool_scale, w_out=v_w_out,
                 ffn2_norm=v_ffn2_norm, ffn2_w_gate=v_ffn2_w_gate, ffn2_w_up=v_ffn2_w_up, ffn2_w_down=v_ffn2_w_down,
                 final_norm=v_final_norm)
    names = [t[0] for t in BIG]

    gathered = []
    for l in range(DEPTH):
        got = _gather_layer(_cast_layer(params, l))
        gathered.append({nm: a.reshape(NSH, rows, cols) for (nm, rows, cols), a in zip(BIG, got)})

    loss, dx, big, sg = _forward_backward(x.reshape(S, D), positions, loss_target.reshape(S, D), gathered, params)

    c_idx = lax.axis_index("c").reshape(1).astype(jnp.int32)
    full = [lax.empty((DEPTH, 2, rows // 2, cols), f32) for _, rows, cols in BIG]
    for l in reversed(range(DEPTH)):
        grads = [big[l][nm] for nm in names]
        psum = _pair_sum(grads, _sibling_swap(grads), c_idx)
        halves = _chip_sum(_chip_exchange(psum))
        full = _sibling_share(halves, full, l)
    big_grad = {BIG_SRC[nm]: a.reshape(DEPTH, rows, cols) for (nm, rows, cols), a in zip(BIG, full)}

    per_layer = {k: sg[k] for k in ("ffn1_norm", "mix_norm", "ffn2_norm", "pool_w", "pool_scale")}
    small_sum = _allreduce_small(_pack_small(per_layer, sg["final_norm"], loss[0]))
    gs, ds_, ms, vs = _adamw(_pack_small_params(params).reshape(1, SMALL_ROWS, 128), small_sum.reshape(1, SMALL_ROWS, 128),
                             _pack_small_params(mom_m).reshape(1, SMALL_ROWS, 128),
                             _pack_small_params(mom_v).reshape(1, SMALL_ROWS, 128), "adamw_small")
    small_out = [_unpack_small(a.reshape(SMALL_ROWS, 128)) for a in (gs, ds_, ms, vs)]

    grad, delta, new_m, new_v = {}, {}, {}, {}
    for k in WEIGHT_ORDER:
        if k in SMALL_NAMES:
            grad[k], delta[k], new_m[k], new_v[k] = (so[k] for so in small_out)
        else:
            grad[k], delta[k], new_m[k], new_v[k] = _adamw(params[k], big_grad[k], mom_m[k], mom_v[k], "adamw_" + k)
    return (small_out[0]["loss"], dx.reshape(1, S, D), *[grad[k] for k in WEIGHT_ORDER], *[delta[k] for k in WEIGHT_ORDER],
            *[new_m[k] for k in WEIGHT_ORDER], *[new_v[k] for k in WEIGHT_ORDER])
```

```python
import functools

import jax
import jax.numpy as jnp
import numpy as np
from jax import lax
from jax.experimental import pallas as pl
from jax.experimental.pallas import tpu as pltpu

f32 = jnp.float32
bf16 = jnp.bfloat16

S = 2048
D = 1024
DEPTH = 4
NSH = 4
FS = 704
PROJ = 2560
PS = 640
PW = 256
AW = 768
NPAIR = 6
NORM_EPS = 1e-6
MASK_VALUE = -1e30
ROPE_THETA = 500000.0
DILATIONS = (1, 4, 16)
QBLK = 128
NBLK = S // QBLK
TM = 512
VMEM_LIMIT = 56 * 1024 * 1024

ADAM_LR = 0.001
ADAM_B1 = 0.9
ADAM_B2 = 0.999
ADAM_EPS = 1e-08
ADAM_WD = 0.01
ADAM_STEP = 10

MESH = pl.DeviceIdType.MESH
ANY = pl.BlockSpec(memory_space=pl.ANY)

BIG = (("g1", D, FS), ("u1", D, FS), ("d1", FS, D), ("wi", D, PS), ("wo", PW, D), ("g2", D, FS), ("u2", D, FS), ("d2", FS, D))
BIG_SRC = {"g1": "ffn1_w_gate", "u1": "ffn1_w_up", "d1": "ffn1_w_down", "wi": "w_in", "wo": "w_out",
           "g2": "ffn2_w_gate", "u2": "ffn2_w_up", "d2": "ffn2_w_down"}


def _cp(*sem):
    return pltpu.CompilerParams(dimension_semantics=sem if sem else None, vmem_limit_bytes=VMEM_LIMIT)


def _sds(shape, dt):
    return jax.ShapeDtypeStruct(shape, dt)


def _dot(a, b):
    return jnp.dot(a, b, preferred_element_type=f32)


def _dot_nt(a, b):
    return lax.dot_general(a, b, (((1,), (1,)), ((), ())), preferred_element_type=f32)


def _dot_tn(a, b):
    return lax.dot_general(a, b, (((0,), (0,)), ((), ())), preferred_element_type=f32)


def _rms_fwd(x, g):
    def body(x_ref, g_ref, h_ref):
        xf = x_ref[...]
        r = lax.rsqrt(jnp.mean(xf * xf, axis=-1, keepdims=True) + NORM_EPS)
        h_ref[...] = ((xf * r) * g_ref[...]).astype(bf16)

    return pl.pallas_call(
        body, out_shape=_sds((S, D), bf16), grid=(S // TM,),
        in_specs=[pl.BlockSpec((TM, D), lambda i: (i, 0)), pl.BlockSpec((1, D), lambda i: (0, 0))],
        out_specs=pl.BlockSpec((TM, D), lambda i: (i, 0)), name="rms_fwd", compiler_params=_cp("parallel"))(x, g)


def _ffn_up(h, wg, wu):
    def body(h_ref, wg_ref, wu_ref, a_ref, b_ref, u_ref):
        hh = h_ref[...]
        a = _dot(hh, wg_ref[0])
        b = _dot(hh, wu_ref[0])
        a_ref[0] = a
        b_ref[0] = b
        u_ref[0] = (a * (1.0 / (1.0 + jnp.exp(-a))) * b).astype(bf16)

    wspec = pl.BlockSpec((1, D, FS), lambda s, i: (s, 0, 0))
    ospec = pl.BlockSpec((1, TM, FS), lambda s, i: (s, i, 0))
    return pl.pallas_call(
        body, out_shape=(_sds((NSH, S, FS), f32), _sds((NSH, S, FS), f32), _sds((NSH, S, FS), bf16)),
        grid=(NSH, S // TM), in_specs=[pl.BlockSpec((TM, D), lambda s, i: (i, 0)), wspec, wspec],
        out_specs=(ospec, ospec, ospec), name="ffn_up", compiler_params=_cp("parallel", "parallel"))(h, wg, wu)


def _ffn_down(x, u, wd):
    def body(x_ref, u_ref, wd_ref, o_ref, acc_ref):
        s = pl.program_id(1)

        @pl.when(s == 0)
        def _():
            acc_ref[...] = jnp.zeros_like(acc_ref)

        acc_ref[...] += _dot(u_ref[0], wd_ref[0])

        @pl.when(s == NSH - 1)
        def _():
            o_ref[...] = x_ref[...] + 0.5 * acc_ref[...]

    return pl.pallas_call(
        body, out_shape=_sds((S, D), f32), grid=(S // TM, NSH),
        in_specs=[pl.BlockSpec((TM, D), lambda i, s: (i, 0)), pl.BlockSpec((1, TM, FS), lambda i, s: (s, i, 0)),
                  pl.BlockSpec((1, FS, D), lambda i, s: (s, 0, 0))],
        out_specs=pl.BlockSpec((TM, D), lambda i, s: (i, 0)), scratch_shapes=[pltpu.VMEM((TM, D), f32)],
        name="ffn_down", compiler_params=_cp("parallel", "arbitrary"))(x, u, wd)


def _in_proj(h, wi):
    def body(h_ref, w_ref, o_ref):
        o_ref[...] = _dot(h_ref[...], w_ref[0])

    return pl.pallas_call(
        body, out_shape=_sds((S, PROJ), f32), grid=(NSH, S // TM),
        in_specs=[pl.BlockSpec((TM, D), lambda s, i: (i, 0)), pl.BlockSpec((1, D, PS), lambda s, i: (s, 0, 0))],
        out_specs=pl.BlockSpec((TM, PS), lambda s, i: (i, s)), name="in_proj", compiler_params=_cp("parallel", "parallel"))(h, wi)


def _out_proj(x, mixed, wo):
    def body(x_ref, m_ref, w_ref, o_ref):
        o_ref[...] = x_ref[...] + _dot(m_ref[...], w_ref[...].reshape(D, D))

    return pl.pallas_call(
        body, out_shape=_sds((S, D), f32), grid=(S // TM,),
        in_specs=[pl.BlockSpec((TM, D), lambda i: (i, 0)), pl.BlockSpec((TM, D), lambda i: (i, 0)),
                  pl.BlockSpec((NSH, PW, D), lambda i: (0, 0, 0))],
        out_specs=pl.BlockSpec((TM, D), lambda i: (i, 0)), name="out_proj", compiler_params=_cp("parallel"))(x, mixed, wo)


def _out_proj_bwd(dx, wo):
    def body(dx_ref, w_ref, o_ref):
        o_ref[...] = _dot_nt(dx_ref[...].astype(bf16), w_ref[...].reshape(D, D))

    return pl.pallas_call(
        body, out_shape=_sds((S, D), f32), grid=(S // TM,),
        in_specs=[pl.BlockSpec((TM, D), lambda i: (i, 0)), pl.BlockSpec((NSH, PW, D), lambda i: (0, 0, 0))],
        out_specs=pl.BlockSpec((TM, D), lambda i: (i, 0)), name="out_proj_bwd", compiler_params=_cp("parallel"))(dx, wo)


def _ffn_bwd_hidden(dx, wd, a, b):
    def body(dx_ref, wd_ref, a_ref, b_ref, da_ref, db_ref):
        dy = (0.5 * dx_ref[...]).astype(bf16)
        du = _dot_nt(dy, wd_ref[0])
        a = a_ref[0]
        sig = 1.0 / (1.0 + jnp.exp(-a))
        da_ref[0] = (du * b_ref[0] * (sig * (1.0 + a * (1.0 - sig)))).astype(bf16)
        db_ref[0] = (du * (a * sig)).astype(bf16)

    hspec = pl.BlockSpec((1, TM, FS), lambda s, i: (s, i, 0))
    return pl.pallas_call(
        body, out_shape=(_sds((NSH, S, FS), bf16), _sds((NSH, S, FS), bf16)), grid=(NSH, S // TM),
        in_specs=[pl.BlockSpec((TM, D), lambda s, i: (i, 0)), pl.BlockSpec((1, FS, D), lambda s, i: (s, 0, 0)), hspec, hspec],
        out_specs=(hspec, hspec), name="ffn_bwd_hidden", compiler_params=_cp("parallel", "parallel"))(dx, wd, a, b)


def _norm_bwd_tail(acc, x_ref, dxin_ref, g_ref, dxo_ref, dg_ref, first):
    xf = x_ref[...]
    r = lax.rsqrt(jnp.mean(xf * xf, axis=-1, keepdims=True) + NORM_EPS)
    xhat = xf * r
    dhg = acc * g_ref[...]
    dxo_ref[...] = dxin_ref[...] + r * (dhg - xhat * jnp.mean(dhg * xhat, axis=-1, keepdims=True))
    part = jnp.sum(acc * xhat, axis=0, keepdims=True)

    @pl.when(first)
    def _():
        dg_ref[...] = part

    @pl.when(jnp.logical_not(first))
    def _():
        dg_ref[...] += part


def _ffn_bwd_dx(dx, x_in, g, da, db, wg, wu):
    def body(dx_ref, x_ref, g_ref, da_ref, db_ref, wg_ref, wu_ref, dxo_ref, dg_ref, acc_ref):
        i, s = pl.program_id(0), pl.program_id(1)

        @pl.when(s == 0)
        def _():
            acc_ref[...] = jnp.zeros_like(acc_ref)

        acc_ref[...] += _dot_nt(da_ref[0], wg_ref[0]) + _dot_nt(db_ref[0], wu_ref[0])

        @pl.when(s == NSH - 1)
        def _():
            _norm_bwd_tail(acc_ref[...], x_ref, dx_ref, g_ref, dxo_ref, dg_ref, i == 0)

    tok = pl.BlockSpec((TM, D), lambda i, s: (i, 0))
    vec = pl.BlockSpec((1, D), lambda i, s: (0, 0))
    hid = pl.BlockSpec((1, TM, FS), lambda i, s: (s, i, 0))
    wsp = pl.BlockSpec((1, D, FS), lambda i, s: (s, 0, 0))
    return pl.pallas_call(
        body, out_shape=(_sds((S, D), f32), _sds((1, D), f32)), grid=(S // TM, NSH),
        in_specs=[tok, tok, vec, hid, hid, wsp, wsp], out_specs=(tok, vec), scratch_shapes=[pltpu.VMEM((TM, D), f32)],
        name="ffn_bwd_dx", compiler_params=_cp("arbitrary", "arbitrary"))(dx, x_in, g, da, db, wg, wu)


def _in_proj_bwd_dx(dx, x_in, g, dproj, wi):
    def body(dx_ref, x_ref, g_ref, dp_ref, w_ref, dxo_ref, dg_ref, acc_ref):
        i, s = pl.program_id(0), pl.program_id(1)

        @pl.when(s == 0)
        def _():
            acc_ref[...] = jnp.zeros_like(acc_ref)

        acc_ref[...] += _dot_nt(dp_ref[...], w_ref[0])

        @pl.when(s == NSH - 1)
        def _():
            _norm_bwd_tail(acc_ref[...], x_ref, dx_ref, g_ref, dxo_ref, dg_ref, i == 0)

    tok = pl.BlockSpec((TM, D), lambda i, s: (i, 0))
    vec = pl.BlockSpec((1, D), lambda i, s: (0, 0))
    return pl.pallas_call(
        body, out_shape=(_sds((S, D), f32), _sds((1, D), f32)), grid=(S // TM, NSH),
        in_specs=[tok, tok, vec, pl.BlockSpec((TM, PS), lambda i, s: (i, s)), pl.BlockSpec((1, D, PS), lambda i, s: (s, 0, 0))],
        out_specs=(tok, vec), scratch_shapes=[pltpu.VMEM((TM, D), f32)],
        name="in_proj_bwd_dx", compiler_params=_cp("arbitrary", "arbitrary"))(dx, x_in, g, dproj, wi)


def _dw(lhs, rhs, lhs_spec, rhs_spec, rows, cols, name, rhs_scale=None):
    def body(l_ref, r_ref, o_ref):
        l = l_ref[...].reshape(S, rows)
        r = r_ref[...].reshape(S, cols)
        if rhs_scale is not None:
            r = (rhs_scale * r).astype(bf16)
        o_ref[...] = _dot_tn(l, r).astype(bf16).reshape(1, 2, rows // 2, cols)

    return pl.pallas_call(
        body, out_shape=_sds((NSH, 2, rows // 2, cols), bf16), grid=(NSH,), in_specs=[lhs_spec, rhs_spec],
        out_specs=pl.BlockSpec((1, 2, rows // 2, cols), lambda s: (s, 0, 0, 0)), name=name, compiler_params=_cp("parallel"))(lhs, rhs)


_WHOLE_TOK = pl.BlockSpec((S, D), lambda s: (0, 0))
_SHARD_HID = pl.BlockSpec((1, S, FS), lambda s: (s, 0, 0))


def _dw_up(h, da):
    return _dw(h, da, _WHOLE_TOK, _SHARD_HID, D, FS, "dw_up")


def _dw_down(u, dx):
    return _dw(u, dx, _SHARD_HID, _WHOLE_TOK, FS, D, "dw_down", rhs_scale=0.5)


def _dw_in(h, dproj):
    return _dw(h, dproj, _WHOLE_TOK, pl.BlockSpec((S, PS), lambda s: (0, s)), D, PS, "dw_in")


def _dw_out(mixed, dx):
    return _dw(mixed, dx, pl.BlockSpec((S, PW), lambda s: (0, s)), _WHOLE_TOK, PW, D, "dw_out", rhs_scale=1.0)


def _final_loss(x, g, target):
    def body(x_ref, g_ref, t_ref, loss_ref, dx_ref, dg_ref):
        i = pl.program_id(0)
        xf = x_ref[...]
        r = lax.rsqrt(jnp.mean(xf * xf, axis=-1, keepdims=True) + NORM_EPS)
        xhat = xf * r
        err = xhat * g_ref[...] - t_ref[...]
        dy = err * (1.0 / D)
        dhg = dy * g_ref[...]
        dx_ref[...] = r * (dhg - xhat * jnp.mean(dhg * xhat, axis=-1, keepdims=True))
        part = jnp.sum(dy * xhat, axis=0, keepdims=True)
        lpart = jnp.zeros((8, 128), f32) + 0.5 * jnp.sum(jnp.mean(err * err, axis=-1, keepdims=True))

        @pl.when(i == 0)
        def _():
            dg_ref[...] = part
            loss_ref[...] = lpart

        @pl.when(i != 0)
        def _():
            dg_ref[...] += part
            loss_ref[...] += lpart

    tok = pl.BlockSpec((TM, D), lambda i: (i, 0))
    vec = pl.BlockSpec((1, D), lambda i: (0, 0))
    return pl.pallas_call(
        body, out_shape=(_sds((8, 128), f32), _sds((S, D), f32), _sds((1, D), f32)), grid=(S // TM,),
        in_specs=[tok, vec, tok], out_specs=(pl.BlockSpec((8, 128), lambda i: (0, 0)), tok, vec),
        name="final_loss", compiler_params=_cp("arbitrary"))(x, g, target)


def _shift_down(x, k, row):
    return jnp.where(row >= k, pltpu.roll(x, k, axis=0), 0.0)


def _shift_up(x, k, row):
    return jnp.where(row < S - k, pltpu.roll(x, S - k, axis=0), 0.0)


def _pool_geometry():
    row = lax.broadcasted_iota(jnp.int32, (S, PW), 0)
    grp = lax.broadcasted_iota(jnp.int32, (S, PW), 1) // 64
    half = jnp.where(grp == 0, 1, jnp.where(grp == 1, 2, jnp.where(grp == 2, 4, 8)))
    hi = jnp.minimum(row + half - 1, S - 1)
    lo = jnp.maximum(row - half, 0)
    return row, grp, (hi - lo + 1).astype(f32)


def _by_group(grp, v0, v1, v2, v3):
    return jnp.where(grp == 0, v0, jnp.where(grp == 1, v1, jnp.where(grp == 2, v2, v3)))


def _window_sums(x, row, grp, transpose):
    l1, r1 = x, x
    l2, r2 = l1 + _shift_down(l1, 1, row), r1 + _shift_up(r1, 1, row)
    l4, r4 = l2 + _shift_down(l2, 2, row), r2 + _shift_up(r2, 2, row)
    l8, r8 = l4 + _shift_down(l4, 4, row), r4 + _shift_up(r4, 4, row)
    lsel = _by_group(grp, l1, l2, l4, l8)
    rsel = _by_group(grp, r1, r2, r4, r8)
    if transpose:
        return lsel + _shift_up(rsel, 1, row)
    return _shift_down(lsel, 1, row) + rsel


def _pool_fwd(proj, wbd, scale):
    def body(v_ref, w_ref, sc_ref, mixed_ref, diff_ref):
        row, grp, cnt = _pool_geometry()
        v = v_ref[...]
        diff = (_window_sums(v, row, grp, False) / cnt - v).astype(bf16)
        diff_ref[...] = diff
        mixed_ref[...] = (_dot(diff, w_ref[...].astype(bf16)) * sc_ref[...]).astype(bf16)

    col = pl.BlockSpec((S, PW), lambda i: (0, 0))
    return pl.pallas_call(
        body, out_shape=(_sds((S, D), bf16), _sds((S, PW), bf16)), grid=(1,),
        in_specs=[col, pl.BlockSpec((PW, PW), lambda i: (0, 0)), pl.BlockSpec((1, PW), lambda i: (0, 0))],
        out_specs=(col, col), name="pool_fwd", compiler_params=_cp("arbitrary"))(proj, wbd, scale)


def _pool_bwd(dmixed, diff, wbd, scale, dproj):
    def body(dy_ref, diff_ref, w_ref, sc_ref, dproj_in, dv_ref, dw_ref, dsc_ref):
        del dproj_in
        row, grp, cnt = _pool_geometry()
        dy = dy_ref[...]
        diff = diff_ref[...]
        w = w_ref[...].astype(bf16)
        dsc_ref[...] = jnp.sum(dy * _dot(diff, w), axis=0, keepdims=True)
        dys = (dy * sc_ref[...]).astype(bf16)
        dw_ref[...] = _dot_tn(diff, dys)
        ddiff = _dot_nt(dys, w)
        dv_ref[...] = (_window_sums(ddiff / cnt, row, grp, True) - ddiff).astype(bf16)

    col = pl.BlockSpec((S, PW), lambda i: (0, 0))
    return pl.pallas_call(
        body, out_shape=(_sds((S, PROJ), bf16), _sds((PW, PW), f32), _sds((1, PW), f32)), grid=(1,),
        in_specs=[col, col, pl.BlockSpec((PW, PW), lambda i: (0, 0)), pl.BlockSpec((1, PW), lambda i: (0, 0)), ANY],
        out_specs=(col, pl.BlockSpec((PW, PW), lambda i: (0, 0)), pl.BlockSpec((1, PW), lambda i: (0, 0))),
        input_output_aliases={4: 0}, name="pool_bwd", compiler_params=_cp("arbitrary"))(dmixed, diff, wbd, scale, dproj)


def _rope_tables(pos_col, freq_row):
    def body(p_ref, f_ref, c_ref, a_ref, b_ref):
        ang = p_ref[...].astype(f32) * f_ref[...]
        l64 = lax.broadcasted_iota(jnp.int32, (S, 128), 1) % 64
        cos, sin = jnp.cos(ang), jnp.sin(ang)
        c_ref[...] = jnp.where(l64 < 16, cos, 1.0)
        a_ref[...] = jnp.where(l64 < 8, -sin, 0.0)
        b_ref[...] = jnp.where((l64 >= 8) & (l64 < 16), sin, 0.0)

    t = _sds((S, 128), f32)
    return pl.pallas_call(body, out_shape=(t, t, t), name="rope_tables", compiler_params=_cp())(pos_col, freq_row)


def _rope(t, c, a, b):
    return t * c + pltpu.roll(t, 120, axis=1) * a + pltpu.roll(t, 8, axis=1) * b


def _rope_bwd(g, c, a, b):
    return g * c + pltpu.roll(g * a, 8, axis=1) + pltpu.roll(g * b, 120, axis=1)


def _perm_load(ref, d):
    if d == 1:
        return ref[...]
    n = S // d
    return jnp.concatenate([ref[pl.ds(r, n, stride=d), :] for r in range(d)], axis=0)


def _unperm_store(ref, val, d):
    if d == 1:
        ref[...] = val
        return
    n = S // d
    for r in range(d):
        ref[pl.ds(r, n, stride=d), :] = val[r * n:(r + 1) * n, :]


def _band(xp):
    z = jnp.zeros((64, 128), bf16)
    p = jnp.concatenate([z, xp, z], axis=0).reshape(NBLK + 1, QBLK, 128)
    return jnp.concatenate([p[:NBLK], p[1:]], axis=1)


def _unband(xb):
    z = jnp.zeros((1, QBLK, 128), f32)
    p = jnp.concatenate([xb[:, :QBLK], z], axis=0) + jnp.concatenate([z, xb[:, QBLK:]], axis=0)
    return p.reshape(S + QBLK, 128)[64:S + 64]


def _band_mask(d):
    blocks_per_class = NBLK // d
    n = lax.broadcasted_iota(jnp.int32, (NBLK, 1, 2 * QBLK), 0) & (blocks_per_class - 1)
    be = lax.broadcasted_iota(jnp.int32, (NBLK, 1, 2 * QBLK), 2)
    a = lax.broadcasted_iota(jnp.int32, (1, 2 * QBLK, 2 * QBLK), 1) & (QBLK - 1)
    b = lax.broadcasted_iota(jnp.int32, (1, 2 * QBLK, 2 * QBLK), 2)
    band = (b >= a) & (b <= a + 128)
    edge = ((be >= 64) | (n != 0)) & ((be < QBLK + 64) | (n != blocks_per_class - 1))
    return band & edge


def _stack_heads(xb, lo):
    z = jnp.zeros_like(xb)
    return jnp.concatenate([jnp.where(lo, xb, z), jnp.where(lo, z, xb)], axis=1)


def _unstack_heads(x2, lo):
    return jnp.where(lo, x2[:, :QBLK], x2[:, QBLK:])


def _rows_to_lanes(col2, lo):
    return jnp.where(lo, jnp.broadcast_to(col2[:, :QBLK], (NBLK, QBLK, 128)), jnp.broadcast_to(col2[:, QBLK:], (NBLK, QBLK, 128)))


def _bmm_nt(a, b):
    return jnp.einsum('nqd,nkd->nqk', a, b, preferred_element_type=f32)


def _bmm_nn(a, b):
    return jnp.einsum('nqk,nkd->nqd', a, b, preferred_element_type=f32)


def _bmm_tn(a, b):
    return jnp.einsum('nqk,nqd->nkd', a, b, preferred_element_type=f32)


def _attn_fwd(proj, tc, ta, tb, mixed):
    def body(q_ref, k_ref, v_ref, c_ref, a_ref, b_ref, mixed_in, mixed_ref, o_ref, lse_ref, qn, kn, t_num, t_m, t_den):
        del mixed_in
        lo = lax.broadcasted_iota(jnp.int32, (1, 1, 128), 2) < 64
        c, a, b = c_ref[...], a_ref[...], b_ref[...]
        qn[...] = _rope(q_ref[...], c, a, b)
        kn[...] = _rope(k_ref[...], c, a, b)
        run = None
        for d in DILATIONS:
            q2 = _stack_heads(_perm_load(qn, d).astype(bf16).reshape(NBLK, QBLK, 128), lo)
            kb = _band(_perm_load(kn, d).astype(bf16))
            vb = _band(_perm_load(v_ref, d).astype(bf16))
            s = jnp.where(_band_mask(d), _bmm_nt(q2, kb) * 0.125, MASK_VALUE)
            m = jnp.max(s, axis=2, keepdims=True)
            p = jnp.exp(s - m)
            den = jnp.sum(p, axis=2, keepdims=True)
            num = _unstack_heads(_bmm_nn(p.astype(bf16), vb), lo)
            _unperm_store(t_num, num.reshape(S, 128), d)
            _unperm_store(t_m, _rows_to_lanes(m, lo).reshape(S, 128), d)
            _unperm_store(t_den, _rows_to_lanes(den, lo).reshape(S, 128), d)
            if run is None:
                run = (t_m[...], t_num[...], t_den[...])
            else:
                m_new = jnp.maximum(run[0], t_m[...])
                w_old, w_new = jnp.exp(run[0] - m_new), jnp.exp(t_m[...] - m_new)
                run = (m_new, w_old * run[1] + w_new * t_num[...], w_old * run[2] + w_new * t_den[...])
        out = run[1] / run[2]
        o_ref[...] = out
        mixed_ref[...] = out.astype(bf16)
        lse_ref[...] = run[0] + jnp.log(run[2])

    def col(off):
        return pl.BlockSpec((S, 128), lambda j, off=off: (0, off + j))

    tab = pl.BlockSpec((S, 128), lambda j: (0, 0))
    scr = pltpu.VMEM((S, 128), f32)
    return pl.pallas_call(
        body, out_shape=(_sds((S, D), bf16), _sds((S, AW), f32), _sds((S, AW), f32)), grid=(NPAIR,),
        in_specs=[col(2), col(8), col(14), tab, tab, tab, ANY], out_specs=(col(2), col(0), col(0)),
        scratch_shapes=[scr, scr, scr, scr, scr], input_output_aliases={6: 0}, name="attn_fwd",
        compiler_params=_cp("arbitrary"))(proj, proj, proj, tc, ta, tb, mixed)


def _attn_bwd(proj, tc, ta, tb, o, lse, dmixed):
    def body(q_ref, k_ref, v_ref, c_ref, a_ref, b_ref, o_ref, lse_ref, do_ref, dp_ref, qn, kn, tmp, dk_s, dv_s):
        t = pl.program_id(1)

        @pl.when(t == 0)
        def _():
            lo = lax.broadcasted_iota(jnp.int32, (1, 1, 128), 2) < 64
            c, a, b = c_ref[...], a_ref[...], b_ref[...]
            qn[...] = _rope(q_ref[...], c, a, b)
            kn[...] = _rope(k_ref[...], c, a, b)
            dq = dk = dv = None
            for d in DILATIONS:
                q2 = _stack_heads(_perm_load(qn, d).astype(bf16).reshape(NBLK, QBLK, 128), lo)
                kb = _band(_perm_load(kn, d).astype(bf16))
                vb = _band(_perm_load(v_ref, d).astype(bf16))
                dob = _perm_load(do_ref, d).reshape(NBLK, QBLK, 128)
                ob = _perm_load(o_ref, d).reshape(NBLK, QBLK, 128)
                lsb = _perm_load(lse_ref, d).reshape(NBLK, QBLK, 128)
                do2 = _stack_heads(dob.astype(bf16), lo)
                delta2 = jnp.sum(_stack_heads(dob * ob, lo), axis=2, keepdims=True)
                lse2 = jnp.max(jnp.concatenate([jnp.where(lo, lsb, MASK_VALUE), jnp.where(lo, MASK_VALUE, lsb)], axis=1),
                               axis=2, keepdims=True)
                s = _bmm_nt(q2, kb) * 0.125
                p = jnp.where(_band_mask(d), jnp.exp(s - lse2), 0.0)
                ds = (p * (_bmm_nt(do2, vb) - delta2) * 0.125).astype(bf16)
                pb = p.astype(bf16)
                dq_b = _unstack_heads(_bmm_nn(ds, kb), lo).reshape(S, 128)
                dk_b = _unband(_bmm_tn(ds, q2))
                dv_b = _unband(_bmm_tn(pb, do2))
                acc = []
                for prev, new in ((dq, dq_b), (dk, dk_b), (dv, dv_b)):
                    _unperm_store(tmp, new, d)
                    acc.append(tmp[...] if prev is None else prev + tmp[...])
                dq, dk, dv = acc
            dp_ref[...] = _rope_bwd(dq, c, a, b).astype(bf16)
            dk_s[...] = _rope_bwd(dk, c, a, b).astype(bf16)
            dv_s[...] = dv.astype(bf16)

        @pl.when(t == 1)
        def _():
            dp_ref[...] = dk_s[...]

        @pl.when(t == 2)
        def _():
            dp_ref[...] = dv_s[...]

    def col(off):
        return pl.BlockSpec((S, 128), lambda j, t, off=off: (0, off + j))

    tab = pl.BlockSpec((S, 128), lambda j, t: (0, 0))
    scr = pltpu.VMEM((S, 128), f32)
    scb = pltpu.VMEM((S, 128), bf16)
    return pl.pallas_call(
        body, out_shape=_sds((S, PROJ), bf16), grid=(NPAIR, 3),
        in_specs=[col(2), col(8), col(14), tab, tab, tab, col(0), col(0), col(2)],
        out_specs=pl.BlockSpec((S, 128), lambda j, t: (0, 2 + NPAIR * t + j)),
        scratch_shapes=[scr, scr, scr, scb, scb], name="attn_bwd",
        compiler_params=_cp("arbitrary", "arbitrary"))(proj, proj, proj, tc, ta, tb, o, lse, dmixed)


def _block_diag(w4):
    out = jnp.zeros((PW, PW), w4.dtype)
    for g in range(4):
        out = out.at[64 * g:64 * (g + 1), 64 * g:64 * (g + 1)].set(w4[g])
    return out


def _diag_blocks(w):
    return jnp.stack([w[64 * g:64 * (g + 1), 64 * g:64 * (g + 1)] for g in range(4)])


def _rope_inputs(positions):
    inv_freq = ROPE_THETA ** (-jnp.arange(0, 16, 2, dtype=f32) / 16)
    l64 = np.arange(128) % 64
    idx = np.where(l64 < 16, l64 % 8, 0)
    return positions.reshape(S, 1), inv_freq[idx].reshape(1, 128)


def _forward_backward(x, positions, target, gathered, small):
    tc, ta, tb = _rope_tables(*_rope_inputs(positions))
    saved = []
    for l in range(DEPTH):
        w = gathered[l]
        g1, gm, g2 = (small[k][l].reshape(1, D) for k in ("ffn1_norm", "mix_norm", "ffn2_norm"))
        wbd = _block_diag(small["pool_w"][l])
        psc = small["pool_scale"][l].reshape(1, PW)
        x0 = x
        h1 = _rms_fwd(x0, g1)
        a1, b1, u1 = _ffn_up(h1, w["g1"], w["u1"])
        x1 = _ffn_down(x0, u1, w["d1"])
        h2 = _rms_fwd(x1, gm)
        proj = _in_proj(h2, w["wi"])
        mixed, diff = _pool_fwd(proj, wbd, psc)
        mixed, o, lse = _attn_fwd(proj, tc, ta, tb, mixed)
        x2 = _out_proj(x1, mixed, w["wo"])
        h3 = _rms_fwd(x2, g2)
        a2, b2, u2 = _ffn_up(h3, w["g2"], w["u2"])
        x = _ffn_down(x2, u2, w["d2"])
        saved.append(dict(x0=x0, h1=h1, a1=a1, b1=b1, u1=u1, x1=x1, h2=h2, proj=proj, mixed=mixed, diff=diff, o=o,
                          lse=lse, x2=x2, h3=h3, a2=a2, b2=b2, u2=u2, g1=g1, gm=gm, g2=g2, wbd=wbd, psc=psc))
    loss, dx, dgf = _final_loss(x, small["final_norm"].reshape(1, D), target)

    big = [None] * DEPTH
    sg = {k: [None] * DEPTH for k in ("ffn1_norm", "mix_norm", "pool_w", "pool_scale", "ffn2_norm")}
    for l in reversed(range(DEPTH)):
        w, sv = gathered[l], saved[l]
        gr = {}
        da, db = _ffn_bwd_hidden(dx, w["d2"], sv["a2"], sv["b2"])
        gr["d2"] = _dw_down(sv["u2"], dx)
        gr["g2"] = _dw_up(sv["h3"], da)
        gr["u2"] = _dw_up(sv["h3"], db)
        dx, sg["ffn2_norm"][l] = _ffn_bwd_dx(dx, sv["x2"], sv["g2"], da, db, w["g2"], w["u2"])
        gr["wo"] = _dw_out(sv["mixed"], dx)
        dmixed = _out_proj_bwd(dx, w["wo"])
        dproj = _attn_bwd(sv["proj"], tc, ta, tb, sv["o"], sv["lse"], dmixed)
        dproj, dwbd, dpsc = _pool_bwd(dmixed, sv["diff"], sv["wbd"], sv["psc"], dproj)
        sg["pool_w"][l] = _diag_blocks(dwbd)
        sg["pool_scale"][l] = dpsc
        gr["wi"] = _dw_in(sv["h2"], dproj)
        dx, sg["mix_norm"][l] = _in_proj_bwd_dx(dx, sv["x1"], sv["gm"], dproj, w["wi"])
        da, db = _ffn_bwd_hidden(dx, w["d1"], sv["a1"], sv["b1"])
        gr["d1"] = _dw_down(sv["u1"], dx)
        gr["g1"] = _dw_up(sv["h1"], da)
        gr["u1"] = _dw_up(sv["h1"], db)
        dx, sg["ffn1_norm"][l] = _ffn_bwd_dx(dx, sv["x0"], sv["g1"], da, db, w["g1"], w["u1"])
        big[l] = gr
    sg["final_norm"] = dgf
    return loss, dx, big, sg


def _place():
    x, y, c = lax.axis_index("x"), lax.axis_index("y"), lax.axis_index("c")
    chips = [(1 - x, y), (x, 1 - y), (1 - x, 1 - y)]
    return x, y, c, chips


def _cast_layer(params, l, place):
    def body(p_ref, *refs):
        del p_ref
        for i_ref, o_ref in zip(refs[:8], refs[8:]):
            o_ref[...] = i_ref[...].astype(bf16).reshape(o_ref.shape)

    ins, in_specs, out_shape, out_specs = [], [], [], []
    for name, rows, cols in BIG:
        q = rows // 4
        ins.append(params[BIG_SRC[name]])
        in_specs.append(pl.BlockSpec((1, q, cols), lambda i, p, l=l: (l, i, 0)))
        out_shape.append(_sds((NSH, 2, rows // 2, cols), bf16))
        out_specs.append(pl.BlockSpec((1, 1, q, cols), lambda i, p: (p[1], i // 2, i % 2, 0)))
    return pl.pallas_call(
        body, out_shape=out_shape,
        grid_spec=pltpu.PrefetchScalarGridSpec(num_scalar_prefetch=1, grid=(4,), in_specs=in_specs, out_specs=out_specs),
        name=f"cast_layer{l}", compiler_params=_cp("parallel"))(place, *ins)


def _gather_layer(bufs):
    n = len(bufs)

    def body(*refs):
        outs = refs[n:2 * n]
        send_sems, recv_sems = refs[2 * n:]
        x, y, c, chips = _place()
        me = 2 * x + y
        sibling = (x, y, 1 - c)
        sends, passed = [], []
        for t in range(n):
            for k, chip in enumerate(chips):
                blk = outs[t].at[me, c]
                cp = pltpu.make_async_remote_copy(
                    src_ref=blk, dst_ref=blk, send_sem=send_sems.at[t, k], recv_sem=recv_sems.at[t, k],
                    device_id=(chip[0], chip[1], c), device_id_type=MESH)
                cp.start()
                sends.append(cp)
        for t in range(n):
            for k, chip in enumerate(chips):
                blk = outs[t].at[2 * chip[0] + chip[1], c]
                pltpu.make_async_remote_copy(
                    src_ref=blk, dst_ref=blk, send_sem=send_sems.at[t, k], recv_sem=recv_sems.at[t, k],
                    device_id=(chip[0], chip[1], c), device_id_type=MESH).wait_recv()
                cp = pltpu.make_async_remote_copy(
                    src_ref=blk, dst_ref=blk, send_sem=send_sems.at[t, 3 + k], recv_sem=recv_sems.at[t, 3 + k],
                    device_id=sibling, device_id_type=MESH)
                cp.start()
                passed.append(cp)
        for t in range(n):
            for k, chip in enumerate(chips):
                blk = outs[t].at[2 * chip[0] + chip[1], 1 - c]
                pltpu.make_async_remote_copy(
                    src_ref=blk, dst_ref=blk, send_sem=send_sems.at[t, 3 + k], recv_sem=recv_sems.at[t, 3 + k],
                    device_id=sibling, device_id_type=MESH).wait_recv()
        for cp in sends + passed:
            cp.wait_send()

    out_shape = [_sds(a.shape, bf16) for a in bufs]
    return pl.pallas_call(
        body, out_shape=out_shape, in_specs=[ANY] * n, out_specs=[ANY] * n, input_output_aliases={t: t for t in range(n)},
        scratch_shapes=[pltpu.SemaphoreType.DMA((n, 6)), pltpu.SemaphoreType.DMA((n, 6))], name="gather_layer")(*bufs)


def _sibling_swap(grads):
    n = len(grads)

    def body(*refs):
        ins, outs = refs[:n], refs[n:2 * n]
        send_sems, recv_sems = refs[2 * n:]
        x, y, c, _ = _place()
        cps = []
        for t in range(n):
            for s in range(NSH):
                cp = pltpu.make_async_remote_copy(
                    src_ref=ins[t].at[s, 1 - c], dst_ref=outs[t].at[s], send_sem=send_sems.at[t, s], recv_sem=recv_sems.at[t, s],
                    device_id=(x, y, 1 - c), device_id_type=MESH)
                cp.start()
                cps.append(cp)
        for cp in cps:
            cp.wait()

    out_shape = [_sds((NSH,) + a.shape[2:], bf16) for a in grads]
    return pl.pallas_call(
        body, out_shape=out_shape, in_specs=[ANY] * n, out_specs=[ANY] * n,
        scratch_shapes=[pltpu.SemaphoreType.DMA((n, NSH)), pltpu.SemaphoreType.DMA((n, NSH))],
        name="sibling_swap")(*grads)


def _row_tile(h):
    return h // 2 if h % 32 == 0 else h


def _pair_sum(grads, got, c_idx):
    n = len(grads)

    def body(c_ref, *refs):
        del c_ref
        for t in range(n):
            refs[2 * n + t][...] = (refs[t][...].astype(f32).reshape(refs[n + t].shape) + refs[n + t][...].astype(f32)).astype(bf16)

    in_specs, out_shape, out_specs = [], [], []
    for a in grads:
        h, cols = a.shape[2:]
        in_specs.append(pl.BlockSpec((1, 1, _row_tile(h), cols), lambda s, i, c: (s, c[0], i, 0)))
    for a in grads:
        h, cols = a.shape[2:]
        in_specs.append(pl.BlockSpec((1, _row_tile(h), cols), lambda s, i, c: (s, i, 0)))
        out_shape.append(_sds((NSH, h, cols), bf16))
        out_specs.append(pl.BlockSpec((1, _row_tile(h), cols), lambda s, i, c: (s, i, 0)))
    return pl.pallas_call(
        body, out_shape=out_shape,
        grid_spec=pltpu.PrefetchScalarGridSpec(num_scalar_prefetch=1, grid=(NSH, 2), in_specs=in_specs, out_specs=out_specs),
        name="pair_sum", compiler_params=_cp("parallel", "parallel"))(c_idx, *grads, *got)


def _chip_exchange(psum):
    n = len(psum)

    def body(*refs):
        ins, outs = refs[:n], refs[n:2 * n]
        send_sems, recv_sems = refs[2 * n:]
        x, y, c, chips = _place()
        me = 2 * x + y
        cps = []
        for t in range(n):
            for k, chip in enumerate(chips):
                cp = pltpu.make_async_remote_copy(
                    src_ref=ins[t].at[2 * chip[0] + chip[1]], dst_ref=outs[t].at[me], send_sem=send_sems.at[t, k],
                    recv_sem=recv_sems.at[t, k], device_id=(chip[0], chip[1], c), device_id_type=MESH)
                cp.start()
                cps.append(cp)
        for t in range(n):
            for k, chip in enumerate(chips):
                blk = outs[t].at[2 * chip[0] + chip[1]]
                pltpu.make_async_remote_copy(
                    src_ref=blk, dst_ref=blk, send_sem=send_sems.at[t, k], recv_sem=recv_sems.at[t, k],
                    device_id=(chip[0], chip[1], c), device_id_type=MESH).wait_recv()
        for cp in cps:
            cp.wait_send()

    out_shape = [_sds(a.shape, bf16) for a in psum]
    return pl.pallas_call(
        body, out_shape=out_shape, in_specs=[ANY] * n, out_specs=[ANY] * n,
        scratch_shapes=[pltpu.SemaphoreType.DMA((n, 3)), pltpu.SemaphoreType.DMA((n, 3))], name="chip_exchange")(*psum)


def _chip_sum(psum, parts, full, place, l):
    n = len(parts)

    def body(p_ref, *refs):
        s = pl.program_id(1)
        for t in range(n):
            val = jnp.where(s == p_ref[1], refs[t][0], refs[n + t][0]).astype(f32)
            out = refs[3 * n + t]

            @pl.when(s == 0)
            def _(out=out, val=val):
                out[0, 0] = val

            @pl.when(s != 0)
            def _(out=out, val=val):
                out[0, 0] += val

    own_specs, part_specs, out_shape, out_specs = [], [], [], []
    for a in parts:
        _, h, cols = a.shape
        r = _row_tile(h)
        own_specs.append(pl.BlockSpec((1, r, cols), lambda i, s, p: (p[1], i, 0)))
        part_specs.append(pl.BlockSpec((1, r, cols), lambda i, s, p: (jnp.where(s == p[1], (s + 1) % NSH, s), i, 0)))
        out_shape.append(_sds((DEPTH, 2, h, cols), f32))
        out_specs.append(pl.BlockSpec((1, 1, r, cols), lambda i, s, p, l=l: (l, p[0], i, 0)))
    return pl.pallas_call(
        body, out_shape=out_shape,
        grid_spec=pltpu.PrefetchScalarGridSpec(num_scalar_prefetch=1, grid=(2, NSH), in_specs=own_specs + part_specs + [ANY] * n,
                                               out_specs=out_specs),
        input_output_aliases={1 + 2 * n + t: t for t in range(n)}, name=f"chip_sum{l}",
        compiler_params=_cp("parallel", "arbitrary"))(place, *psum, *parts, *full)


def _sibling_share(full, l):
    n = len(full)

    def body(*refs):
        outs = refs[n:2 * n]
        send_sems, recv_sems = refs[2 * n:]
        x, y, c, _ = _place()
        sibling = (x, y, 1 - c)
        cps = []
        for t in range(n):
            blk = outs[t].at[l, c]
            cp = pltpu.make_async_remote_copy(
                src_ref=blk, dst_ref=blk, send_sem=send_sems.at[t], recv_sem=recv_sems.at[t], device_id=sibling, device_id_type=MESH)
            cp.start()
            cps.append(cp)
        for t in range(n):
            blk = outs[t].at[l, 1 - c]
            pltpu.make_async_remote_copy(
                src_ref=blk, dst_ref=blk, send_sem=send_sems.at[t], recv_sem=recv_sems.at[t],
                device_id=sibling, device_id_type=MESH).wait_recv()
        for cp in cps:
            cp.wait_send()

    out_shape = [_sds(a.shape, f32) for a in full]
    return pl.pallas_call(
        body, out_shape=out_shape, in_specs=[ANY] * n, out_specs=[ANY] * n, input_output_aliases={t: t for t in range(n)},
        scratch_shapes=[pltpu.SemaphoreType.DMA((n,)), pltpu.SemaphoreType.DMA((n,))], name=f"sibling_share{l}")(*full)


SMALL_ROWS = 656


def _pack_small(per_layer, final_vec, loss_tile):
    rows = []
    for l in range(DEPTH):
        for k in ("ffn1_norm", "mix_norm", "ffn2_norm"):
            rows.append(per_layer[k][l].reshape(8, 128))
        rows.append(per_layer["pool_w"][l].reshape(128, 128))
        rows.append(jnp.pad(per_layer["pool_scale"][l].reshape(2, 128), ((0, 6), (0, 0))))
    rows.append(final_vec.reshape(8, 128))
    rows.append(loss_tile)
    return jnp.concatenate(rows, axis=0)


def _unpack_small(buf):
    out = {k: [] for k in ("ffn1_norm", "mix_norm", "ffn2_norm", "pool_w", "pool_scale")}
    r = 0
    for l in range(DEPTH):
        for k in ("ffn1_norm", "mix_norm", "ffn2_norm"):
            out[k].append(buf[r:r + 8].reshape(D))
            r += 8
        out["pool_w"].append(buf[r:r + 128].reshape(4, 64, 64))
        r += 128
        out["pool_scale"].append(buf[r:r + 2].reshape(PW))
        r += 8
    res = {k: jnp.stack(v) for k, v in out.items()}
    res["final_norm"] = buf[r:r + 8].reshape(D)
    res["loss"] = buf[r + 8, 0]
    return res


def _allreduce_small(buf):
    def body(in_ref, out_ref, slots, send_sems, recv_sems):
        x, y, c, _ = _place()
        me = 4 * x + 2 * y + c
        slots[me] = in_ref[...]
        peers = []
        for k in range(1, 8):
            px, py, pc = x ^ (k >> 2), y ^ ((k >> 1) & 1), c ^ (k & 1)
            cp = pltpu.make_async_remote_copy(
                src_ref=in_ref, dst_ref=slots.at[me], send_sem=send_sems.at[k - 1], recv_sem=recv_sems.at[k - 1],
                device_id=(px, py, pc), device_id_type=MESH)
            cp.start()
            peers.append(cp)
        for k in range(1, 8):
            px, py, pc = x ^ (k >> 2), y ^ ((k >> 1) & 1), c ^ (k & 1)
            slot = 4 * px + 2 * py + pc
            pltpu.make_async_remote_copy(
                src_ref=slots.at[slot], dst_ref=slots.at[slot], send_sem=send_sems.at[k - 1], recv_sem=recv_sems.at[k - 1],
                device_id=(px, py, pc), device_id_type=MESH).wait_recv()
        for cp in peers:
            cp.wait_send()
        acc = slots[0]
        for j in range(1, 8):
            acc = acc + slots[j]
        out_ref[...] = acc

    return pl.pallas_call(
        body, out_shape=_sds((SMALL_ROWS, 128), f32),
        in_specs=[pl.BlockSpec(memory_space=pltpu.VMEM)], out_specs=pl.BlockSpec(memory_space=pltpu.VMEM),
        scratch_shapes=[pltpu.VMEM((8, SMALL_ROWS, 128), f32), pltpu.SemaphoreType.DMA((7,)), pltpu.SemaphoreType.DMA((7,))],
        name="allreduce_small", compiler_params=_cp())(buf)


def _adamw_math(w, g, m, v):
    m = ADAM_B1 * m + (1.0 - ADAM_B1) * g
    v = ADAM_B2 * v + (1.0 - ADAM_B2) * (g * g)
    m_hat = m / (1.0 - ADAM_B1 ** ADAM_STEP)
    v_hat = v / (1.0 - ADAM_B2 ** ADAM_STEP)
    return -ADAM_LR * (m_hat / (jnp.sqrt(v_hat) + ADAM_EPS) + ADAM_WD * w), m, v


def _adamw(w, g, m, v, name):
    def body(w_ref, g_ref, m_ref, v_ref, go_ref, d_ref, mo_ref, vo_ref):
        g = g_ref[...]
        d, mn, vn = _adamw_math(w_ref[...], g, m_ref[...], v_ref[...])
        go_ref[...] = g
        d_ref[...] = d
        mo_ref[...] = mn
        vo_ref[...] = vn

    n, rows, cols = w.shape
    r = rows // 4 if rows % 32 == 0 else rows
    spec = pl.BlockSpec((1, r, cols), lambda i, j: (i, j, 0))
    out = _sds(w.shape, f32)
    return pl.pallas_call(body, out_shape=(out, out, out, out), grid=(n, rows // r), in_specs=[spec] * 4,
                          out_specs=(spec,) * 4, name=name, compiler_params=_cp("parallel", "parallel"))(w, g, m, v)


SMALL_NAMES = ("ffn1_norm", "mix_norm", "pool_w", "pool_scale", "ffn2_norm", "final_norm")
WEIGHT_ORDER = ("ffn1_norm", "ffn1_w_gate", "ffn1_w_up", "ffn1_w_down", "mix_norm", "w_in", "pool_w", "pool_scale", "w_out",
                "ffn2_norm", "ffn2_w_gate", "ffn2_w_up", "ffn2_w_down", "final_norm")


def _pack_small_params(p):
    per_layer = {k: [p[k][l] for l in range(DEPTH)] for k in ("ffn1_norm", "mix_norm", "ffn2_norm", "pool_w", "pool_scale")}
    return _pack_small(per_layer, p["final_norm"], jnp.zeros((8, 128), f32))


def kernel(x, positions, ffn1_norm, ffn1_w_gate, ffn1_w_up, ffn1_w_down, mix_norm, w_in, pool_w, pool_scale, w_out, ffn2_norm, ffn2_w_gate, ffn2_w_up, ffn2_w_down, final_norm, loss_target, m_ffn1_norm, m_ffn1_w_gate, m_ffn1_w_up, m_ffn1_w_down, m_mix_norm, m_w_in, m_pool_w, m_pool_scale, m_w_out, m_ffn2_norm, m_ffn2_w_gate, m_ffn2_w_up, m_ffn2_w_down, m_final_norm, v_ffn1_norm, v_ffn1_w_gate, v_ffn1_w_up, v_ffn1_w_down, v_mix_norm, v_w_in, v_pool_w, v_pool_scale, v_w_out, v_ffn2_norm, v_ffn2_w_gate, v_ffn2_w_up, v_ffn2_w_down, v_final_norm):
    params = dict(ffn1_norm=ffn1_norm, ffn1_w_gate=ffn1_w_gate, ffn1_w_up=ffn1_w_up, ffn1_w_down=ffn1_w_down,
                  mix_norm=mix_norm, w_in=w_in, pool_w=pool_w, pool_scale=pool_scale, w_out=w_out, ffn2_norm=ffn2_norm,
                  ffn2_w_gate=ffn2_w_gate, ffn2_w_up=ffn2_w_up, ffn2_w_down=ffn2_w_down, final_norm=final_norm)
    mom_m = dict(ffn1_norm=m_ffn1_norm, ffn1_w_gate=m_ffn1_w_gate, ffn1_w_up=m_ffn1_w_up, ffn1_w_down=m_ffn1_w_down,
                 mix_norm=m_mix_norm, w_in=m_w_in, pool_w=m_pool_w, pool_scale=m_pool_scale, w_out=m_w_out,
                 ffn2_norm=m_ffn2_norm, ffn2_w_gate=m_ffn2_w_gate, ffn2_w_up=m_ffn2_w_up, ffn2_w_down=m_ffn2_w_down,
                 final_norm=m_final_norm)
    mom_v = dict(ffn1_norm=v_ffn1_norm, ffn1_w_gate=v_ffn1_w_gate, ffn1_w_up=v_ffn1_w_up, ffn1_w_down=v_ffn1_w_down,
                 mix_norm=v_mix_norm, w_in=v_w_in, pool_w=v_pool_w, pool_scale=v_pool_scale, w_out=v_w_out,
                 ffn2_norm=v_ffn2_norm, ffn2_w_gate=v_ffn2_w_gate, ffn2_w_up=v_ffn2_w_up, ffn2_w_down=v_ffn2_w_down,
                 final_norm=v_final_norm)
    names = [t[0] for t in BIG]

    place = jnp.stack([lax.axis_index("c"), 2 * lax.axis_index("x") + lax.axis_index("y")]).astype(jnp.int32)
    gathered = []
    for l in range(DEPTH):
        got = _gather_layer(_cast_layer(params, l, place))
        gathered.append({nm: a.reshape(NSH, rows, cols) for (nm, rows, cols), a in zip(BIG, got)})

    loss, dx, big, sg = _forward_backward(x.reshape(S, D), positions, loss_target.reshape(S, D), gathered, params)

    full = [lax.empty((DEPTH, 2, rows // 2, cols), f32) for _, rows, cols in BIG]
    for l in reversed(range(DEPTH)):
        grads = [big[l][nm] for nm in names]
        psum = _pair_sum(grads, _sibling_swap(grads), place)
        full = _sibling_share(_chip_sum(psum, _chip_exchange(psum), full, place, l), l)
    big_grad = {BIG_SRC[nm]: a.reshape(DEPTH, rows, cols) for (nm, rows, cols), a in zip(BIG, full)}

    per_layer = {k: sg[k] for k in ("ffn1_norm", "mix_norm", "ffn2_norm", "pool_w", "pool_scale")}
    small_sum = _allreduce_small(_pack_small(per_layer, sg["final_norm"], loss))
    gs, ds_, ms, vs = _adamw(_pack_small_params(params).reshape(1, SMALL_ROWS, 128), small_sum.reshape(1, SMALL_ROWS, 128),
                             _pack_small_params(mom_m).reshape(1, SMALL_ROWS, 128),
                             _pack_small_params(mom_v).reshape(1, SMALL_ROWS, 128), "adamw_small")
    small_out = [_unpack_small(a.reshape(SMALL_ROWS, 128)) for a in (gs, ds_, ms, vs)]

    grad, delta, new_m, new_v = {}, {}, {}, {}
    for k in WEIGHT_ORDER:
        if k in SMALL_NAMES:
            grad[k], delta[k], new_m[k], new_v[k] = (so[k] for so in small_out)
        else:
            grad[k], delta[k], new_m[k], new_v[k] = _adamw(params[k], big_grad[k], mom_m[k], mom_v[k], "adamw_" + k)
    return (small_out[0]["loss"], dx.reshape(1, S, D), *[grad[k] for k in WEIGHT_ORDER], *[delta[k] for k in WEIGHT_ORDER],
            *[new_m[k] for k in WEIGHT_ORDER], *[new_v[k] for k in WEIGHT_ORDER])
```

```python
import functools

import jax
import jax.numpy as jnp
import numpy as np
from jax import lax
from jax.experimental import pallas as pl
from jax.experimental.pallas import tpu as pltpu

f32 = jnp.float32
bf16 = jnp.bfloat16

S = 2048
D = 1024
DEPTH = 4
NSH = 4
FS = 704
PROJ = 2560
PS = 640
PW = 256
AW = 768
NPAIR = 6
NORM_EPS = 1e-6
MASK_VALUE = -1e30
ROPE_THETA = 500000.0
DILATIONS = (1, 4, 16)
QBLK = 128
NBLK = S // QBLK
TM = 512
VMEM_LIMIT = 56 * 1024 * 1024

ADAM_LR = 0.001
ADAM_B1 = 0.9
ADAM_B2 = 0.999
ADAM_EPS = 1e-08
ADAM_WD = 0.01
ADAM_STEP = 10

MESH = pl.DeviceIdType.MESH
ANY = pl.BlockSpec(memory_space=pl.ANY)

BIG = (("g1", D, FS), ("u1", D, FS), ("d1", FS, D), ("wi", D, PS), ("wo", PW, D), ("g2", D, FS), ("u2", D, FS), ("d2", FS, D))
BIG_SRC = {"g1": "ffn1_w_gate", "u1": "ffn1_w_up", "d1": "ffn1_w_down", "wi": "w_in", "wo": "w_out",
           "g2": "ffn2_w_gate", "u2": "ffn2_w_up", "d2": "ffn2_w_down"}


def _cp(*sem):
    return pltpu.CompilerParams(dimension_semantics=sem if sem else None, vmem_limit_bytes=VMEM_LIMIT)


def _sds(shape, dt):
    return jax.ShapeDtypeStruct(shape, dt)


def _dot(a, b):
    return jnp.dot(a, b, preferred_element_type=f32)


def _dot_nt(a, b):
    return lax.dot_general(a, b, (((1,), (1,)), ((), ())), preferred_element_type=f32)


def _dot_tn(a, b):
    return lax.dot_general(a, b, (((0,), (0,)), ((), ())), preferred_element_type=f32)


def _dep(dep):
    return ([], []) if dep is None else ([ANY], [dep])


def _rms_fwd(x, g, dep=None):
    def body(x_ref, g_ref, *rest):
        h_ref = rest[-1]
        xf = x_ref[...]
        r = lax.rsqrt(jnp.mean(xf * xf, axis=-1, keepdims=True) + NORM_EPS)
        h_ref[...] = ((xf * r) * g_ref[...]).astype(bf16)

    dspec, dop = _dep(dep)
    return pl.pallas_call(
        body, out_shape=_sds((S, D), bf16), grid=(S // TM,),
        in_specs=[pl.BlockSpec((TM, D), lambda i: (i, 0)), pl.BlockSpec((1, D), lambda i: (0, 0))] + dspec,
        out_specs=pl.BlockSpec((TM, D), lambda i: (i, 0)), name="rms_fwd", compiler_params=_cp("parallel"))(x, g, *dop)


def _ffn_up(h, wg, wu):
    def body(h_ref, wg_ref, wu_ref, a_ref, b_ref, u_ref):
        hh = h_ref[...]
        a = _dot(hh, wg_ref[0])
        b = _dot(hh, wu_ref[0])
        a_ref[0] = a
        b_ref[0] = b
        u_ref[0] = (a * (1.0 / (1.0 + jnp.exp(-a))) * b).astype(bf16)

    wspec = pl.BlockSpec((1, D, FS), lambda s, i: (s, 0, 0))
    ospec = pl.BlockSpec((1, TM, FS), lambda s, i: (s, i, 0))
    return pl.pallas_call(
        body, out_shape=(_sds((NSH, S, FS), f32), _sds((NSH, S, FS), f32), _sds((NSH, S, FS), bf16)),
        grid=(NSH, S // TM), in_specs=[pl.BlockSpec((TM, D), lambda s, i: (i, 0)), wspec, wspec],
        out_specs=(ospec, ospec, ospec), name="ffn_up", compiler_params=_cp("parallel", "parallel"))(h, wg, wu)


def _ffn_down(x, u, wd):
    def body(x_ref, u_ref, wd_ref, o_ref, acc_ref):
        s = pl.program_id(1)

        @pl.when(s == 0)
        def _():
            acc_ref[...] = jnp.zeros_like(acc_ref)

        acc_ref[...] += _dot(u_ref[0], wd_ref[0])

        @pl.when(s == NSH - 1)
        def _():
            o_ref[...] = x_ref[...] + 0.5 * acc_ref[...]

    return pl.pallas_call(
        body, out_shape=_sds((S, D), f32), grid=(S // TM, NSH),
        in_specs=[pl.BlockSpec((TM, D), lambda i, s: (i, 0)), pl.BlockSpec((1, TM, FS), lambda i, s: (s, i, 0)),
                  pl.BlockSpec((1, FS, D), lambda i, s: (s, 0, 0))],
        out_specs=pl.BlockSpec((TM, D), lambda i, s: (i, 0)), scratch_shapes=[pltpu.VMEM((TM, D), f32)],
        name="ffn_down", compiler_params=_cp("parallel", "arbitrary"))(x, u, wd)


def _in_proj(h, wi):
    def body(h_ref, w_ref, o_ref):
        o_ref[...] = _dot(h_ref[...], w_ref[0])

    return pl.pallas_call(
        body, out_shape=_sds((S, PROJ), f32), grid=(NSH, S // TM),
        in_specs=[pl.BlockSpec((TM, D), lambda s, i: (i, 0)), pl.BlockSpec((1, D, PS), lambda s, i: (s, 0, 0))],
        out_specs=pl.BlockSpec((TM, PS), lambda s, i: (i, s)), name="in_proj", compiler_params=_cp("parallel", "parallel"))(h, wi)


def _out_proj(x, mixed, wo):
    def body(x_ref, m_ref, w_ref, o_ref):
        o_ref[...] = x_ref[...] + _dot(m_ref[...], w_ref[...].reshape(D, D))

    return pl.pallas_call(
        body, out_shape=_sds((S, D), f32), grid=(S // TM,),
        in_specs=[pl.BlockSpec((TM, D), lambda i: (i, 0)), pl.BlockSpec((TM, D), lambda i: (i, 0)),
                  pl.BlockSpec((NSH, PW, D), lambda i: (0, 0, 0))],
        out_specs=pl.BlockSpec((TM, D), lambda i: (i, 0)), name="out_proj", compiler_params=_cp("parallel"))(x, mixed, wo)


def _out_proj_bwd(dx, wo):
    def body(dx_ref, w_ref, o_ref):
        o_ref[...] = _dot_nt(dx_ref[...].astype(bf16), w_ref[...].reshape(D, D))

    return pl.pallas_call(
        body, out_shape=_sds((S, D), f32), grid=(S // TM,),
        in_specs=[pl.BlockSpec((TM, D), lambda i: (i, 0)), pl.BlockSpec((NSH, PW, D), lambda i: (0, 0, 0))],
        out_specs=pl.BlockSpec((TM, D), lambda i: (i, 0)), name="out_proj_bwd", compiler_params=_cp("parallel"))(dx, wo)


def _ffn_bwd_hidden(dx, wd, a, b, dep=None):
    def body(dx_ref, wd_ref, a_ref, b_ref, *rest):
        da_ref, db_ref = rest[-2:]
        dy = (0.5 * dx_ref[...]).astype(bf16)
        du = _dot_nt(dy, wd_ref[0])
        a = a_ref[0]
        sig = 1.0 / (1.0 + jnp.exp(-a))
        da_ref[0] = (du * b_ref[0] * (sig * (1.0 + a * (1.0 - sig)))).astype(bf16)
        db_ref[0] = (du * (a * sig)).astype(bf16)

    hspec = pl.BlockSpec((1, TM, FS), lambda s, i: (s, i, 0))
    dspec, dop = _dep(dep)
    return pl.pallas_call(
        body, out_shape=(_sds((NSH, S, FS), bf16), _sds((NSH, S, FS), bf16)), grid=(NSH, S // TM),
        in_specs=[pl.BlockSpec((TM, D), lambda s, i: (i, 0)), pl.BlockSpec((1, FS, D), lambda s, i: (s, 0, 0)), hspec, hspec] + dspec,
        out_specs=(hspec, hspec), name="ffn_bwd_hidden", compiler_params=_cp("parallel", "parallel"))(dx, wd, a, b, *dop)


def _norm_bwd_tail(acc, x_ref, dxin_ref, g_ref, dxo_ref, dg_ref, first):
    xf = x_ref[...]
    r = lax.rsqrt(jnp.mean(xf * xf, axis=-1, keepdims=True) + NORM_EPS)
    xhat = xf * r
    dhg = acc * g_ref[...]
    dxo_ref[...] = dxin_ref[...] + r * (dhg - xhat * jnp.mean(dhg * xhat, axis=-1, keepdims=True))
    part = jnp.sum(acc * xhat, axis=0, keepdims=True)

    @pl.when(first)
    def _():
        dg_ref[...] = part

    @pl.when(jnp.logical_not(first))
    def _():
        dg_ref[...] += part


def _ffn_bwd_dx(dx, x_in, g, da, db, wg, wu):
    def body(dx_ref, x_ref, g_ref, da_ref, db_ref, wg_ref, wu_ref, dxo_ref, dg_ref, acc_ref):
        i, s = pl.program_id(0), pl.program_id(1)

        @pl.when(s == 0)
        def _():
            acc_ref[...] = jnp.zeros_like(acc_ref)

        acc_ref[...] += _dot_nt(da_ref[0], wg_ref[0]) + _dot_nt(db_ref[0], wu_ref[0])

        @pl.when(s == NSH - 1)
        def _():
            _norm_bwd_tail(acc_ref[...], x_ref, dx_ref, g_ref, dxo_ref, dg_ref, i == 0)

    tok = pl.BlockSpec((TM, D), lambda i, s: (i, 0))
    vec = pl.BlockSpec((1, D), lambda i, s: (0, 0))
    hid = pl.BlockSpec((1, TM, FS), lambda i, s: (s, i, 0))
    wsp = pl.BlockSpec((1, D, FS), lambda i, s: (s, 0, 0))
    return pl.pallas_call(
        body, out_shape=(_sds((S, D), f32), _sds((1, D), f32)), grid=(S // TM, NSH),
        in_specs=[tok, tok, vec, hid, hid, wsp, wsp], out_specs=(tok, vec), scratch_shapes=[pltpu.VMEM((TM, D), f32)],
        name="ffn_bwd_dx", compiler_params=_cp("arbitrary", "arbitrary"))(dx, x_in, g, da, db, wg, wu)


def _in_proj_bwd_dx(dx, x_in, g, dproj, wi):
    def body(dx_ref, x_ref, g_ref, dp_ref, w_ref, dxo_ref, dg_ref, acc_ref):
        i, s = pl.program_id(0), pl.program_id(1)

        @pl.when(s == 0)
        def _():
            acc_ref[...] = jnp.zeros_like(acc_ref)

        acc_ref[...] += _dot_nt(dp_ref[...], w_ref[0])

        @pl.when(s == NSH - 1)
        def _():
            _norm_bwd_tail(acc_ref[...], x_ref, dx_ref, g_ref, dxo_ref, dg_ref, i == 0)

    tok = pl.BlockSpec((TM, D), lambda i, s: (i, 0))
    vec = pl.BlockSpec((1, D), lambda i, s: (0, 0))
    return pl.pallas_call(
        body, out_shape=(_sds((S, D), f32), _sds((1, D), f32)), grid=(S // TM, NSH),
        in_specs=[tok, tok, vec, pl.BlockSpec((TM, PS), lambda i, s: (i, s)), pl.BlockSpec((1, D, PS), lambda i, s: (s, 0, 0))],
        out_specs=(tok, vec), scratch_shapes=[pltpu.VMEM((TM, D), f32)],
        name="in_proj_bwd_dx", compiler_params=_cp("arbitrary", "arbitrary"))(dx, x_in, g, dproj, wi)


def _dw(lhs, rhs, lhs_spec, rhs_spec, rows, cols, name, rhs_scale=None, dep=None):
    def body(l_ref, r_ref, *rest):
        o_ref = rest[-1]
        l = l_ref[...].reshape(S, rows)
        r = r_ref[...].reshape(S, cols)
        if rhs_scale is not None:
            r = (rhs_scale * r).astype(bf16)
        o_ref[...] = _dot_tn(l, r).astype(bf16).reshape(1, 2, rows // 2, cols)

    dspec, dop = _dep(dep)
    return pl.pallas_call(
        body, out_shape=_sds((NSH, 2, rows // 2, cols), bf16), grid=(NSH,), in_specs=[lhs_spec, rhs_spec] + dspec,
        out_specs=pl.BlockSpec((1, 2, rows // 2, cols), lambda s: (s, 0, 0, 0)), name=name, compiler_params=_cp("parallel"))(lhs, rhs, *dop)


_WHOLE_TOK = pl.BlockSpec((S, D), lambda s: (0, 0))
_SHARD_HID = pl.BlockSpec((1, S, FS), lambda s: (s, 0, 0))


def _dw_up(h, da):
    return _dw(h, da, _WHOLE_TOK, _SHARD_HID, D, FS, "dw_up")


def _dw_down(u, dx, dep=None):
    return _dw(u, dx, _SHARD_HID, _WHOLE_TOK, FS, D, "dw_down", rhs_scale=0.5, dep=dep)


def _dw_in(h, dproj):
    return _dw(h, dproj, _WHOLE_TOK, pl.BlockSpec((S, PS), lambda s: (0, s)), D, PS, "dw_in")


def _dw_out(mixed, dx):
    return _dw(mixed, dx, pl.BlockSpec((S, PW), lambda s: (0, s)), _WHOLE_TOK, PW, D, "dw_out", rhs_scale=1.0)


def _final_loss(x, g, target):
    def body(x_ref, g_ref, t_ref, loss_ref, dx_ref, dg_ref):
        i = pl.program_id(0)
        xf = x_ref[...]
        r = lax.rsqrt(jnp.mean(xf * xf, axis=-1, keepdims=True) + NORM_EPS)
        xhat = xf * r
        err = xhat * g_ref[...] - t_ref[...]
        dy = err * (1.0 / D)
        dhg = dy * g_ref[...]
        dx_ref[...] = r * (dhg - xhat * jnp.mean(dhg * xhat, axis=-1, keepdims=True))
        part = jnp.sum(dy * xhat, axis=0, keepdims=True)
        lpart = jnp.zeros((8, 128), f32) + 0.5 * jnp.sum(jnp.mean(err * err, axis=-1, keepdims=True))

        @pl.when(i == 0)
        def _():
            dg_ref[...] = part
            loss_ref[...] = lpart

        @pl.when(i != 0)
        def _():
            dg_ref[...] += part
            loss_ref[...] += lpart

    tok = pl.BlockSpec((TM, D), lambda i: (i, 0))
    vec = pl.BlockSpec((1, D), lambda i: (0, 0))
    return pl.pallas_call(
        body, out_shape=(_sds((8, 128), f32), _sds((S, D), f32), _sds((1, D), f32)), grid=(S // TM,),
        in_specs=[tok, vec, tok], out_specs=(pl.BlockSpec((8, 128), lambda i: (0, 0)), tok, vec),
        name="final_loss", compiler_params=_cp("arbitrary"))(x, g, target)


def _shift_down(x, k, row):
    return jnp.where(row >= k, pltpu.roll(x, k, axis=0), 0.0)


def _shift_up(x, k, row):
    return jnp.where(row < S - k, pltpu.roll(x, S - k, axis=0), 0.0)


def _pool_geometry():
    row = lax.broadcasted_iota(jnp.int32, (S, PW), 0)
    grp = lax.broadcasted_iota(jnp.int32, (S, PW), 1) // 64
    half = jnp.where(grp == 0, 1, jnp.where(grp == 1, 2, jnp.where(grp == 2, 4, 8)))
    hi = jnp.minimum(row + half - 1, S - 1)
    lo = jnp.maximum(row - half, 0)
    return row, grp, (hi - lo + 1).astype(f32)


def _by_group(grp, v0, v1, v2, v3):
    return jnp.where(grp == 0, v0, jnp.where(grp == 1, v1, jnp.where(grp == 2, v2, v3)))


def _window_sums(x, row, grp, transpose):
    l1, r1 = x, x
    l2, r2 = l1 + _shift_down(l1, 1, row), r1 + _shift_up(r1, 1, row)
    l4, r4 = l2 + _shift_down(l2, 2, row), r2 + _shift_up(r2, 2, row)
    l8, r8 = l4 + _shift_down(l4, 4, row), r4 + _shift_up(r4, 4, row)
    lsel = _by_group(grp, l1, l2, l4, l8)
    rsel = _by_group(grp, r1, r2, r4, r8)
    if transpose:
        return lsel + _shift_up(rsel, 1, row)
    return _shift_down(lsel, 1, row) + rsel


def _pool_fwd(proj, wbd, scale):
    def body(v_ref, w_ref, sc_ref, mixed_ref, diff_ref):
        row, grp, cnt = _pool_geometry()
        v = v_ref[...]
        diff = (_window_sums(v, row, grp, False) / cnt - v).astype(bf16)
        diff_ref[...] = diff
        mixed_ref[...] = (_dot(diff, w_ref[...].astype(bf16)) * sc_ref[...]).astype(bf16)

    col = pl.BlockSpec((S, PW), lambda i: (0, 0))
    return pl.pallas_call(
        body, out_shape=(_sds((S, D), bf16), _sds((S, PW), bf16)), grid=(1,),
        in_specs=[col, pl.BlockSpec((PW, PW), lambda i: (0, 0)), pl.BlockSpec((1, PW), lambda i: (0, 0))],
        out_specs=(col, col), name="pool_fwd", compiler_params=_cp("arbitrary"))(proj, wbd, scale)


def _pool_bwd(dmixed, diff, wbd, scale, dproj):
    def body(dy_ref, diff_ref, w_ref, sc_ref, dproj_in, dv_ref, dw_ref, dsc_ref):
        del dproj_in
        row, grp, cnt = _pool_geometry()
        dy = dy_ref[...]
        diff = diff_ref[...]
        w = w_ref[...].astype(bf16)
        dsc_ref[...] = jnp.sum(dy * _dot(diff, w), axis=0, keepdims=True)
        dys = (dy * sc_ref[...]).astype(bf16)
        dw_ref[...] = _dot_tn(diff, dys)
        ddiff = _dot_nt(dys, w)
        dv_ref[...] = (_window_sums(ddiff / cnt, row, grp, True) - ddiff).astype(bf16)

    col = pl.BlockSpec((S, PW), lambda i: (0, 0))
    return pl.pallas_call(
        body, out_shape=(_sds((S, PROJ), bf16), _sds((PW, PW), f32), _sds((1, PW), f32)), grid=(1,),
        in_specs=[col, col, pl.BlockSpec((PW, PW), lambda i: (0, 0)), pl.BlockSpec((1, PW), lambda i: (0, 0)), ANY],
        out_specs=(col, pl.BlockSpec((PW, PW), lambda i: (0, 0)), pl.BlockSpec((1, PW), lambda i: (0, 0))),
        input_output_aliases={4: 0}, name="pool_bwd", compiler_params=_cp("arbitrary"))(dmixed, diff, wbd, scale, dproj)


def _rope_tables(pos_col, freq_row):
    def body(p_ref, f_ref, c_ref, a_ref, b_ref):
        ang = p_ref[...].astype(f32) * f_ref[...]
        l64 = lax.broadcasted_iota(jnp.int32, (S, 128), 1) % 64
        cos, sin = jnp.cos(ang), jnp.sin(ang)
        c_ref[...] = jnp.where(l64 < 16, cos, 1.0)
        a_ref[...] = jnp.where(l64 < 8, -sin, 0.0)
        b_ref[...] = jnp.where((l64 >= 8) & (l64 < 16), sin, 0.0)

    t = _sds((S, 128), f32)
    return pl.pallas_call(body, out_shape=(t, t, t), name="rope_tables", compiler_params=_cp())(pos_col, freq_row)


def _rope(t, c, a, b):
    return t * c + pltpu.roll(t, 120, axis=1) * a + pltpu.roll(t, 8, axis=1) * b


def _rope_bwd(g, c, a, b):
    return g * c + pltpu.roll(g * a, 8, axis=1) + pltpu.roll(g * b, 120, axis=1)


def _perm_load(ref, d):
    if d == 1:
        return ref[...]
    n = S // d
    return jnp.concatenate([ref[pl.ds(r, n, stride=d), :] for r in range(d)], axis=0)


def _unperm_store(ref, val, d):
    if d == 1:
        ref[...] = val
        return
    n = S // d
    for r in range(d):
        ref[pl.ds(r, n, stride=d), :] = val[r * n:(r + 1) * n, :]


def _band(xp):
    z = jnp.zeros((64, 128), bf16)
    p = jnp.concatenate([z, xp, z], axis=0).reshape(NBLK + 1, QBLK, 128)
    return jnp.concatenate([p[:NBLK], p[1:]], axis=1)


def _unband(xb):
    z = jnp.zeros((1, QBLK, 128), f32)
    p = jnp.concatenate([xb[:, :QBLK], z], axis=0) + jnp.concatenate([z, xb[:, QBLK:]], axis=0)
    return p.reshape(S + QBLK, 128)[64:S + 64]


def _band_mask(d):
    blocks_per_class = NBLK // d
    n = lax.broadcasted_iota(jnp.int32, (NBLK, 1, 2 * QBLK), 0) & (blocks_per_class - 1)
    be = lax.broadcasted_iota(jnp.int32, (NBLK, 1, 2 * QBLK), 2)
    a = lax.broadcasted_iota(jnp.int32, (1, 2 * QBLK, 2 * QBLK), 1) & (QBLK - 1)
    b = lax.broadcasted_iota(jnp.int32, (1, 2 * QBLK, 2 * QBLK), 2)
    band = (b >= a) & (b <= a + 128)
    edge = ((be >= 64) | (n != 0)) & ((be < QBLK + 64) | (n != blocks_per_class - 1))
    return band & edge


def _stack_heads(xb, lo):
    z = jnp.zeros_like(xb)
    return jnp.concatenate([jnp.where(lo, xb, z), jnp.where(lo, z, xb)], axis=1)


def _unstack_heads(x2, lo):
    return jnp.where(lo, x2[:, :QBLK], x2[:, QBLK:])


def _rows_to_lanes(col2, lo):
    return jnp.where(lo, jnp.broadcast_to(col2[:, :QBLK], (NBLK, QBLK, 128)), jnp.broadcast_to(col2[:, QBLK:], (NBLK, QBLK, 128)))


def _bmm_nt(a, b):
    return jnp.einsum('nqd,nkd->nqk', a, b, preferred_element_type=f32)


def _bmm_nn(a, b):
    return jnp.einsum('nqk,nkd->nqd', a, b, preferred_element_type=f32)


def _bmm_tn(a, b):
    return jnp.einsum('nqk,nqd->nkd', a, b, preferred_element_type=f32)


def _attn_fwd(proj, tc, ta, tb, mixed):
    def body(q_ref, k_ref, v_ref, c_ref, a_ref, b_ref, mixed_in, mixed_ref, o_ref, lse_ref, qn, kn, t_num, t_m, t_den):
        del mixed_in
        lo = lax.broadcasted_iota(jnp.int32, (1, 1, 128), 2) < 64
        c, a, b = c_ref[...], a_ref[...], b_ref[...]
        qn[...] = _rope(q_ref[...], c, a, b)
        kn[...] = _rope(k_ref[...], c, a, b)
        run = None
        for d in DILATIONS:
            q2 = _stack_heads(_perm_load(qn, d).astype(bf16).reshape(NBLK, QBLK, 128), lo)
            kb = _band(_perm_load(kn, d).astype(bf16))
            vb = _band(_perm_load(v_ref, d).astype(bf16))
            s = jnp.where(_band_mask(d), _bmm_nt(q2, kb) * 0.125, MASK_VALUE)
            m = jnp.max(s, axis=2, keepdims=True)
            p = jnp.exp(s - m)
            den = jnp.sum(p, axis=2, keepdims=True)
            num = _unstack_heads(_bmm_nn(p.astype(bf16), vb), lo)
            _unperm_store(t_num, num.reshape(S, 128), d)
            _unperm_store(t_m, _rows_to_lanes(m, lo).reshape(S, 128), d)
            _unperm_store(t_den, _rows_to_lanes(den, lo).reshape(S, 128), d)
            if run is None:
                run = (t_m[...], t_num[...], t_den[...])
            else:
                m_new = jnp.maximum(run[0], t_m[...])
                w_old, w_new = jnp.exp(run[0] - m_new), jnp.exp(t_m[...] - m_new)
                run = (m_new, w_old * run[1] + w_new * t_num[...], w_old * run[2] + w_new * t_den[...])
        out = run[1] / run[2]
        o_ref[...] = out
        mixed_ref[...] = out.astype(bf16)
        lse_ref[...] = run[0] + jnp.log(run[2])

    def col(off):
        return pl.BlockSpec((S, 128), lambda j, off=off: (0, off + j))

    tab = pl.BlockSpec((S, 128), lambda j: (0, 0))
    scr = pltpu.VMEM((S, 128), f32)
    return pl.pallas_call(
        body, out_shape=(_sds((S, D), bf16), _sds((S, AW), f32), _sds((S, AW), f32)), grid=(NPAIR,),
        in_specs=[col(2), col(8), col(14), tab, tab, tab, ANY], out_specs=(col(2), col(0), col(0)),
        scratch_shapes=[scr, scr, scr, scr, scr], input_output_aliases={6: 0}, name="attn_fwd",
        compiler_params=_cp("arbitrary"))(proj, proj, proj, tc, ta, tb, mixed)


def _attn_bwd(proj, tc, ta, tb, o, lse, dmixed):
    def body(q_ref, k_ref, v_ref, c_ref, a_ref, b_ref, o_ref, lse_ref, do_ref, dp_ref, qn, kn, tmp, dk_s, dv_s):
        t = pl.program_id(1)

        @pl.when(t == 0)
        def _():
            lo = lax.broadcasted_iota(jnp.int32, (1, 1, 128), 2) < 64
            c, a, b = c_ref[...], a_ref[...], b_ref[...]
            qn[...] = _rope(q_ref[...], c, a, b)
            kn[...] = _rope(k_ref[...], c, a, b)
            dq = dk = dv = None
            for d in DILATIONS:
                q2 = _stack_heads(_perm_load(qn, d).astype(bf16).reshape(NBLK, QBLK, 128), lo)
                kb = _band(_perm_load(kn, d).astype(bf16))
                vb = _band(_perm_load(v_ref, d).astype(bf16))
                dob = _perm_load(do_ref, d).reshape(NBLK, QBLK, 128)
                ob = _perm_load(o_ref, d).reshape(NBLK, QBLK, 128)
                lsb = _perm_load(lse_ref, d).reshape(NBLK, QBLK, 128)
                do2 = _stack_heads(dob.astype(bf16), lo)
                delta2 = jnp.sum(_stack_heads(dob * ob, lo), axis=2, keepdims=True)
                lse2 = jnp.max(jnp.concatenate([jnp.where(lo, lsb, MASK_VALUE), jnp.where(lo, MASK_VALUE, lsb)], axis=1),
                               axis=2, keepdims=True)
                s = _bmm_nt(q2, kb) * 0.125
                p = jnp.where(_band_mask(d), jnp.exp(s - lse2), 0.0)
                ds = (p * (_bmm_nt(do2, vb) - delta2) * 0.125).astype(bf16)
                pb = p.astype(bf16)
                dq_b = _unstack_heads(_bmm_nn(ds, kb), lo).reshape(S, 128)
                dk_b = _unband(_bmm_tn(ds, q2))
                dv_b = _unband(_bmm_tn(pb, do2))
                acc = []
                for prev, new in ((dq, dq_b), (dk, dk_b), (dv, dv_b)):
                    _unperm_store(tmp, new, d)
                    acc.append(tmp[...] if prev is None else prev + tmp[...])
                dq, dk, dv = acc
            dp_ref[...] = _rope_bwd(dq, c, a, b).astype(bf16)
            dk_s[...] = _rope_bwd(dk, c, a, b).astype(bf16)
            dv_s[...] = dv.astype(bf16)

        @pl.when(t == 1)
        def _():
            dp_ref[...] = dk_s[...]

        @pl.when(t == 2)
        def _():
            dp_ref[...] = dv_s[...]

    def col(off):
        return pl.BlockSpec((S, 128), lambda j, t, off=off: (0, off + j))

    tab = pl.BlockSpec((S, 128), lambda j, t: (0, 0))
    scr = pltpu.VMEM((S, 128), f32)
    scb = pltpu.VMEM((S, 128), bf16)
    return pl.pallas_call(
        body, out_shape=_sds((S, PROJ), bf16), grid=(NPAIR, 3),
        in_specs=[col(2), col(8), col(14), tab, tab, tab, col(0), col(0), col(2)],
        out_specs=pl.BlockSpec((S, 128), lambda j, t: (0, 2 + NPAIR * t + j)),
        scratch_shapes=[scr, scr, scr, scb, scb], name="attn_bwd",
        compiler_params=_cp("arbitrary", "arbitrary"))(proj, proj, proj, tc, ta, tb, o, lse, dmixed)


def _block_diag(w4):
    out = jnp.zeros((PW, PW), w4.dtype)
    for g in range(4):
        out = out.at[64 * g:64 * (g + 1), 64 * g:64 * (g + 1)].set(w4[g])
    return out


def _diag_blocks(w):
    return jnp.stack([w[64 * g:64 * (g + 1), 64 * g:64 * (g + 1)] for g in range(4)])


def _rope_inputs(positions):
    inv_freq = ROPE_THETA ** (-jnp.arange(0, 16, 2, dtype=f32) / 16)
    l64 = np.arange(128) % 64
    idx = np.where(l64 < 16, l64 % 8, 0)
    return positions.reshape(S, 1), inv_freq[idx].reshape(1, 128)


def _layer_fwd(x, w, small, l, tabs, dep=None):
    g1, gm, g2 = (small[k][l].reshape(1, D) for k in ("ffn1_norm", "mix_norm", "ffn2_norm"))
    wbd = _block_diag(small["pool_w"][l])
    psc = small["pool_scale"][l].reshape(1, PW)
    h1 = _rms_fwd(x, g1, dep)
    a1, b1, u1 = _ffn_up(h1, w["g1"], w["u1"])
    x1 = _ffn_down(x, u1, w["d1"])
    h2 = _rms_fwd(x1, gm)
    proj = _in_proj(h2, w["wi"])
    mixed, diff = _pool_fwd(proj, wbd, psc)
    mixed, o, lse = _attn_fwd(proj, *tabs, mixed)
    x2 = _out_proj(x1, mixed, w["wo"])
    h3 = _rms_fwd(x2, g2)
    a2, b2, u2 = _ffn_up(h3, w["g2"], w["u2"])
    out = _ffn_down(x2, u2, w["d2"])
    return out, dict(x0=x, h1=h1, a1=a1, b1=b1, u1=u1, x1=x1, h2=h2, proj=proj, mixed=mixed, diff=diff, o=o, lse=lse,
                     x2=x2, h3=h3, a2=a2, b2=b2, u2=u2, g1=g1, gm=gm, g2=g2, wbd=wbd, psc=psc)


def _layer_bwd(dx, w, sv, tabs, dep=None):
    gr, sg = {}, {}
    da, db = _ffn_bwd_hidden(dx, w["d2"], sv["a2"], sv["b2"], dep)
    gr["d2"] = _dw_down(sv["u2"], dx, dep)
    gr["g2"] = _dw_up(sv["h3"], da)
    gr["u2"] = _dw_up(sv["h3"], db)
    dx, sg["ffn2_norm"] = _ffn_bwd_dx(dx, sv["x2"], sv["g2"], da, db, w["g2"], w["u2"])
    gr["wo"] = _dw_out(sv["mixed"], dx)
    dmixed = _out_proj_bwd(dx, w["wo"])
    dproj = _attn_bwd(sv["proj"], *tabs, sv["o"], sv["lse"], dmixed)
    dproj, dwbd, sg["pool_scale"] = _pool_bwd(dmixed, sv["diff"], sv["wbd"], sv["psc"], dproj)
    sg["pool_w"] = _diag_blocks(dwbd)
    gr["wi"] = _dw_in(sv["h2"], dproj)
    dx, sg["mix_norm"] = _in_proj_bwd_dx(dx, sv["x1"], sv["gm"], dproj, w["wi"])
    da, db = _ffn_bwd_hidden(dx, w["d1"], sv["a1"], sv["b1"])
    gr["d1"] = _dw_down(sv["u1"], dx)
    gr["g1"] = _dw_up(sv["h1"], da)
    gr["u1"] = _dw_up(sv["h1"], db)
    dx, sg["ffn1_norm"] = _ffn_bwd_dx(dx, sv["x0"], sv["g1"], da, db, w["g1"], w["u1"])
    return dx, gr, sg


def _forward_backward(x, positions, target, gathered, small):
    tabs = _rope_tables(*_rope_inputs(positions))
    saved = []
    for l in range(DEPTH):
        x, sv = _layer_fwd(x, gathered[l], small, l, tabs)
        saved.append(sv)
    loss, dx, dgf = _final_loss(x, small["final_norm"].reshape(1, D), target)
    big = [None] * DEPTH
    sg = {k: [None] * DEPTH for k in ("ffn1_norm", "mix_norm", "pool_w", "pool_scale", "ffn2_norm")}
    for l in reversed(range(DEPTH)):
        dx, big[l], sgl = _layer_bwd(dx, gathered[l], saved[l], tabs)
        for k, v in sgl.items():
            sg[k][l] = v
    sg["final_norm"] = dgf
    return loss, dx, big, sg


def _place():
    x, y, c = lax.axis_index("x"), lax.axis_index("y"), lax.axis_index("c")
    chips = [(1 - x, y), (x, 1 - y), (1 - x, 1 - y)]
    return x, y, c, chips


def _cast_layer(params, l, place):
    def body(p_ref, *refs):
        del p_ref
        for i_ref, o_ref in zip(refs[:8], refs[8:]):
            o_ref[...] = i_ref[...].astype(bf16).reshape(o_ref.shape)

    ins, in_specs, out_shape, out_specs = [], [], [], []
    for name, rows, cols in BIG:
        q = rows // 4
        ins.append(params[BIG_SRC[name]])
        in_specs.append(pl.BlockSpec((1, q, cols), lambda i, p, l=l: (l, i, 0)))
        out_shape.append(_sds((NSH, 2, rows // 2, cols), bf16))
        out_specs.append(pl.BlockSpec((1, 1, q, cols), lambda i, p: (p[1], i // 2, i % 2, 0)))
    return pl.pallas_call(
        body, out_shape=out_shape,
        grid_spec=pltpu.PrefetchScalarGridSpec(num_scalar_prefetch=1, grid=(4,), in_specs=in_specs, out_specs=out_specs),
        name=f"cast_layer{l}", compiler_params=_cp("parallel"))(place, *ins)


HBM = pl.BlockSpec(memory_space=pltpu.HBM)
SEM = pl.BlockSpec(memory_space=pltpu.SEMAPHORE)
_SPLIT = pltpu.CompilerParams(has_side_effects=pltpu.SideEffectType.DATAFLOW_SIDE_EFFECTING)


def _hbm(arrays):
    return [pltpu.with_memory_space_constraint(a, pltpu.HBM) for a in arrays]


def _chip_copies(src_of, dst_of, send_sems, recv_sems, n):
    x, y, c, chips = _place()
    me = 2 * x + y
    out = []
    for t in range(n):
        for k, chip in enumerate(chips):
            peer = 2 * chip[0] + chip[1]
            send = pltpu.make_async_remote_copy(
                src_ref=src_of(t, peer), dst_ref=dst_of(t, me), send_sem=send_sems.at[3 * t + k], recv_sem=recv_sems.at[3 * t + k],
                device_id=(chip[0], chip[1], c), device_id_type=MESH)
            land = pltpu.make_async_remote_copy(
                src_ref=src_of(t, peer), dst_ref=dst_of(t, peer), send_sem=send_sems.at[3 * t + k], recv_sem=recv_sems.at[3 * t + k],
                device_id=(chip[0], chip[1], c), device_id_type=MESH)
            out.append((send, land))
    return out


def _exchange_start(src, land, after, src_of, dst_of, name):
    n, m = len(src), len(src) + len(land)

    def body(*refs):
        src_refs = refs[:n]
        land_refs = refs[n:m] if land else src_refs
        send_sems, recv_sems = refs[m + 1], refs[m + 2]
        token = refs[-1]
        for send, _ in _chip_copies(lambda t, s: src_of(src_refs[t], s), lambda t, s: dst_of(land_refs[t], s), send_sems, recv_sems, n):
            send.start()
        token[...] = jnp.zeros_like(token)

    arrays = list(src) + list(land)
    out_shape = ([pltpu.SemaphoreType.DMA((3 * n,)), pltpu.SemaphoreType.DMA((3 * n,))] + [pltpu.HBM(a.shape, a.dtype) for a in arrays]
                 + [_sds((8, 128), f32)])
    res = pl.pallas_call(
        body, out_shape=out_shape, in_specs=[HBM] * m + [ANY], out_specs=[SEM, SEM] + [HBM] * m + [pl.BlockSpec(memory_space=pltpu.VMEM)],
        input_output_aliases={i: 2 + i for i in range(m)}, name=name, compiler_params=_SPLIT)(*_hbm(arrays), after)
    return res[0], res[1], list(res[2:2 + n]), list(res[2 + n:2 + m]), res[-1]


def _exchange_wait(send_sems, recv_sems, src, land, after, src_of, dst_of, name):
    n, m = len(src), len(src) + len(land)

    def body(*refs):
        src_refs = refs[:n]
        land_refs = refs[n:m] if land else src_refs
        send_sems, recv_sems = refs[m], refs[m + 1]
        for send, land_cp in _chip_copies(lambda t, s: src_of(src_refs[t], s), lambda t, s: dst_of(land_refs[t], s), send_sems, recv_sems, n):
            send.wait_send()
            land_cp.wait_recv()

    arrays = list(src) + list(land)
    res = pl.pallas_call(
        body, out_shape=[pltpu.HBM(a.shape, a.dtype) for a in arrays], in_specs=[HBM] * m + [SEM, SEM, ANY], out_specs=[HBM] * m,
        input_output_aliases={i: i for i in range(m)}, name=name, compiler_params=_SPLIT)(*arrays, send_sems, recv_sems, after)
    return list(res[:n]), list(res[n:])


def _own_half(ref, s):
    x, y, c, _ = _place()
    return ref.at[2 * x + y, c]


def _slot_half(ref, s):
    return ref.at[s, lax.axis_index("c")]


def _slot(ref, s):
    return ref.at[s]


def _gather_forward(bufs):
    n = len(bufs)

    def body(*refs):
        outs = refs[n:2 * n]
        send_sems, recv_sems = refs[2 * n:]
        x, y, c, chips = _place()
        sibling = (x, y, 1 - c)
        passed = []
        for t in range(n):
            for k, chip in enumerate(chips):
                blk = outs[t].at[2 * chip[0] + chip[1], c]
                cp = pltpu.make_async_remote_copy(
                    src_ref=blk, dst_ref=blk, send_sem=send_sems.at[t, k], recv_sem=recv_sems.at[t, k],
                    device_id=sibling, device_id_type=MESH)
                cp.start()
                passed.append(cp)
        for t in range(n):
            for k, chip in enumerate(chips):
                blk = outs[t].at[2 * chip[0] + chip[1], 1 - c]
                pltpu.make_async_remote_copy(
                    src_ref=blk, dst_ref=blk, send_sem=send_sems.at[t, k], recv_sem=recv_sems.at[t, k],
                    device_id=sibling, device_id_type=MESH).wait_recv()
        for cp in passed:
            cp.wait_send()

    out_shape = [_sds(a.shape, bf16) for a in bufs]
    return pl.pallas_call(
        body, out_shape=out_shape, in_specs=[ANY] * n, out_specs=[ANY] * n, input_output_aliases={t: t for t in range(n)},
        scratch_shapes=[pltpu.SemaphoreType.DMA((n, 3)), pltpu.SemaphoreType.DMA((n, 3))], name="gather_forward")(*bufs)


def _sibling_swap(grads):
    n = len(grads)

    def body(*refs):
        ins, outs = refs[:n], refs[n:2 * n]
        send_sems, recv_sems = refs[2 * n:]
        x, y, c, _ = _place()
        cps = []
        for t in range(n):
            for s in range(NSH):
                cp = pltpu.make_async_remote_copy(
                    src_ref=ins[t].at[s, 1 - c], dst_ref=outs[t].at[s], send_sem=send_sems.at[t, s], recv_sem=recv_sems.at[t, s],
                    device_id=(x, y, 1 - c), device_id_type=MESH)
                cp.start()
                cps.append(cp)
        for cp in cps:
            cp.wait()

    out_shape = [_sds((NSH,) + a.shape[2:], bf16) for a in grads]
    return pl.pallas_call(
        body, out_shape=out_shape, in_specs=[ANY] * n, out_specs=[ANY] * n,
        scratch_shapes=[pltpu.SemaphoreType.DMA((n, NSH)), pltpu.SemaphoreType.DMA((n, NSH))],
        name="sibling_swap")(*grads)


def _row_tile(h):
    return h // 2 if h % 32 == 0 else h


def _pair_sum(grads, got, c_idx):
    n = len(grads)

    def body(c_ref, *refs):
        del c_ref
        for t in range(n):
            refs[2 * n + t][...] = (refs[t][...].astype(f32).reshape(refs[n + t].shape) + refs[n + t][...].astype(f32)).astype(bf16)

    in_specs, out_shape, out_specs = [], [], []
    for a in grads:
        h, cols = a.shape[2:]
        in_specs.append(pl.BlockSpec((1, 1, _row_tile(h), cols), lambda s, i, c: (s, c[0], i, 0)))
    for a in grads:
        h, cols = a.shape[2:]
        in_specs.append(pl.BlockSpec((1, _row_tile(h), cols), lambda s, i, c: (s, i, 0)))
        out_shape.append(_sds((NSH, h, cols), bf16))
        out_specs.append(pl.BlockSpec((1, _row_tile(h), cols), lambda s, i, c: (s, i, 0)))
    return pl.pallas_call(
        body, out_shape=out_shape,
        grid_spec=pltpu.PrefetchScalarGridSpec(num_scalar_prefetch=1, grid=(NSH, 2), in_specs=in_specs, out_specs=out_specs),
        name="pair_sum", compiler_params=_cp("parallel", "parallel"))(c_idx, *grads, *got)


def _chip_sum(psum, parts, full, place, l):
    n = len(parts)

    def body(p_ref, *refs):
        s = pl.program_id(1)
        for t in range(n):
            val = jnp.where(s == p_ref[1], refs[t][0], refs[n + t][0]).astype(f32)
            out = refs[3 * n + t]

            @pl.when(s == 0)
            def _(out=out, val=val):
                out[0, 0] = val

            @pl.when(s != 0)
            def _(out=out, val=val):
                out[0, 0] += val

    own_specs, part_specs, out_shape, out_specs = [], [], [], []
    for a in parts:
        _, h, cols = a.shape
        r = _row_tile(h)
        own_specs.append(pl.BlockSpec((1, r, cols), lambda i, s, p: (p[1], i, 0)))
        part_specs.append(pl.BlockSpec((1, r, cols), lambda i, s, p: (jnp.where(s == p[1], (s + 1) % NSH, s), i, 0)))
        out_shape.append(_sds((DEPTH, 2, h, cols), f32))
        out_specs.append(pl.BlockSpec((1, 1, r, cols), lambda i, s, p, l=l: (l, p[0], i, 0)))
    return pl.pallas_call(
        body, out_shape=out_shape,
        grid_spec=pltpu.PrefetchScalarGridSpec(num_scalar_prefetch=1, grid=(2, NSH), in_specs=own_specs + part_specs + [ANY] * n,
                                               out_specs=out_specs),
        input_output_aliases={1 + 2 * n + t: t for t in range(n)}, name=f"chip_sum{l}",
        compiler_params=_cp("parallel", "arbitrary"))(place, *psum, *parts, *full)


def _sibling_share(full, l):
    n = len(full)

    def body(*refs):
        outs = refs[n:2 * n]
        send_sems, recv_sems = refs[2 * n:]
        x, y, c, _ = _place()
        sibling = (x, y, 1 - c)
        cps = []
        for t in range(n):
            blk = outs[t].at[l, c]
            cp = pltpu.make_async_remote_copy(
                src_ref=blk, dst_ref=blk, send_sem=send_sems.at[t], recv_sem=recv_sems.at[t], device_id=sibling, device_id_type=MESH)
            cp.start()
            cps.append(cp)
        for t in range(n):
            blk = outs[t].at[l, 1 - c]
            pltpu.make_async_remote_copy(
                src_ref=blk, dst_ref=blk, send_sem=send_sems.at[t], recv_sem=recv_sems.at[t],
                device_id=sibling, device_id_type=MESH).wait_recv()
        for cp in cps:
            cp.wait_send()

    out_shape = [_sds(a.shape, f32) for a in full]
    return pl.pallas_call(
        body, out_shape=out_shape, in_specs=[ANY] * n, out_specs=[ANY] * n, input_output_aliases={t: t for t in range(n)},
        scratch_shapes=[pltpu.SemaphoreType.DMA((n,)), pltpu.SemaphoreType.DMA((n,))], name=f"sibling_share{l}")(*full)


SMALL_ROWS = 656


def _pack_small(per_layer, final_vec, loss_tile):
    rows = []
    for l in range(DEPTH):
        for k in ("ffn1_norm", "mix_norm", "ffn2_norm"):
            rows.append(per_layer[k][l].reshape(8, 128))
        rows.append(per_layer["pool_w"][l].reshape(128, 128))
        rows.append(jnp.pad(per_layer["pool_scale"][l].reshape(2, 128), ((0, 6), (0, 0))))
    rows.append(final_vec.reshape(8, 128))
    rows.append(loss_tile)
    return jnp.concatenate(rows, axis=0)


def _unpack_small(buf):
    out = {k: [] for k in ("ffn1_norm", "mix_norm", "ffn2_norm", "pool_w", "pool_scale")}
    r = 0
    for l in range(DEPTH):
        for k in ("ffn1_norm", "mix_norm", "ffn2_norm"):
            out[k].append(buf[r:r + 8].reshape(D))
            r += 8
        out["pool_w"].append(buf[r:r + 128].reshape(4, 64, 64))
        r += 128
        out["pool_scale"].append(buf[r:r + 2].reshape(PW))
        r += 8
    res = {k: jnp.stack(v) for k, v in out.items()}
    res["final_norm"] = buf[r:r + 8].reshape(D)
    res["loss"] = buf[r + 8, 0]
    return res


def _allreduce_small(buf):
    def body(in_ref, out_ref, slots, send_sems, recv_sems):
        x, y, c, _ = _place()
        me = 4 * x + 2 * y + c
        slots[me] = in_ref[...]
        peers = []
        for k in range(1, 8):
            px, py, pc = x ^ (k >> 2), y ^ ((k >> 1) & 1), c ^ (k & 1)
            cp = pltpu.make_async_remote_copy(
                src_ref=in_ref, dst_ref=slots.at[me], send_sem=send_sems.at[k - 1], recv_sem=recv_sems.at[k - 1],
                device_id=(px, py, pc), device_id_type=MESH)
            cp.start()
            peers.append(cp)
        for k in range(1, 8):
            px, py, pc = x ^ (k >> 2), y ^ ((k >> 1) & 1), c ^ (k & 1)
            slot = 4 * px + 2 * py + pc
            pltpu.make_async_remote_copy(
                src_ref=slots.at[slot], dst_ref=slots.at[slot], send_sem=send_sems.at[k - 1], recv_sem=recv_sems.at[k - 1],
                device_id=(px, py, pc), device_id_type=MESH).wait_recv()
        for cp in peers:
            cp.wait_send()
        acc = slots[0]
        for j in range(1, 8):
            acc = acc + slots[j]
        out_ref[...] = acc

    return pl.pallas_call(
        body, out_shape=_sds((SMALL_ROWS, 128), f32),
        in_specs=[pl.BlockSpec(memory_space=pltpu.VMEM)], out_specs=pl.BlockSpec(memory_space=pltpu.VMEM),
        scratch_shapes=[pltpu.VMEM((8, SMALL_ROWS, 128), f32), pltpu.SemaphoreType.DMA((7,)), pltpu.SemaphoreType.DMA((7,))],
        name="allreduce_small", compiler_params=_cp())(buf)


def _adamw_math(w, g, m, v):
    m = ADAM_B1 * m + (1.0 - ADAM_B1) * g
    v = ADAM_B2 * v + (1.0 - ADAM_B2) * (g * g)
    m_hat = m / (1.0 - ADAM_B1 ** ADAM_STEP)
    v_hat = v / (1.0 - ADAM_B2 ** ADAM_STEP)
    return -ADAM_LR * (m_hat / (jnp.sqrt(v_hat) + ADAM_EPS) + ADAM_WD * w), m, v


def _adamw(w, g, m, v, name):
    def body(w_ref, g_ref, m_ref, v_ref, go_ref, d_ref, mo_ref, vo_ref):
        g = g_ref[...]
        d, mn, vn = _adamw_math(w_ref[...], g, m_ref[...], v_ref[...])
        go_ref[...] = g
        d_ref[...] = d
        mo_ref[...] = mn
        vo_ref[...] = vn

    n, rows, cols = w.shape
    r = rows // 4 if rows % 32 == 0 else rows
    spec = pl.BlockSpec((1, r, cols), lambda i, j: (i, j, 0))
    out = _sds(w.shape, f32)
    return pl.pallas_call(body, out_shape=(out, out, out, out), grid=(n, rows // r), in_specs=[spec] * 4,
                          out_specs=(spec,) * 4, name=name, compiler_params=_cp("parallel", "parallel"))(w, g, m, v)


SMALL_NAMES = ("ffn1_norm", "mix_norm", "pool_w", "pool_scale", "ffn2_norm", "final_norm")
WEIGHT_ORDER = ("ffn1_norm", "ffn1_w_gate", "ffn1_w_up", "ffn1_w_down", "mix_norm", "w_in", "pool_w", "pool_scale", "w_out",
                "ffn2_norm", "ffn2_w_gate", "ffn2_w_up", "ffn2_w_down", "final_norm")


def _pack_small_params(p):
    per_layer = {k: [p[k][l] for l in range(DEPTH)] for k in ("ffn1_norm", "mix_norm", "ffn2_norm", "pool_w", "pool_scale")}
    return _pack_small(per_layer, p["final_norm"], jnp.zeros((8, 128), f32))


def kernel(x, positions, ffn1_norm, ffn1_w_gate, ffn1_w_up, ffn1_w_down, mix_norm, w_in, pool_w, pool_scale, w_out, ffn2_norm, ffn2_w_gate, ffn2_w_up, ffn2_w_down, final_norm, loss_target, m_ffn1_norm, m_ffn1_w_gate, m_ffn1_w_up, m_ffn1_w_down, m_mix_norm, m_w_in, m_pool_w, m_pool_scale, m_w_out, m_ffn2_norm, m_ffn2_w_gate, m_ffn2_w_up, m_ffn2_w_down, m_final_norm, v_ffn1_norm, v_ffn1_w_gate, v_ffn1_w_up, v_ffn1_w_down, v_mix_norm, v_w_in, v_pool_w, v_pool_scale, v_w_out, v_ffn2_norm, v_ffn2_w_gate, v_ffn2_w_up, v_ffn2_w_down, v_final_norm):
    params = dict(ffn1_norm=ffn1_norm, ffn1_w_gate=ffn1_w_gate, ffn1_w_up=ffn1_w_up, ffn1_w_down=ffn1_w_down,
                  mix_norm=mix_norm, w_in=w_in, pool_w=pool_w, pool_scale=pool_scale, w_out=w_out, ffn2_norm=ffn2_norm,
                  ffn2_w_gate=ffn2_w_gate, ffn2_w_up=ffn2_w_up, ffn2_w_down=ffn2_w_down, final_norm=final_norm)
    mom_m = dict(ffn1_norm=m_ffn1_norm, ffn1_w_gate=m_ffn1_w_gate, ffn1_w_up=m_ffn1_w_up, ffn1_w_down=m_ffn1_w_down,
                 mix_norm=m_mix_norm, w_in=m_w_in, pool_w=m_pool_w, pool_scale=m_pool_scale, w_out=m_w_out,
                 ffn2_norm=m_ffn2_norm, ffn2_w_gate=m_ffn2_w_gate, ffn2_w_up=m_ffn2_w_up, ffn2_w_down=m_ffn2_w_down,
                 final_norm=m_final_norm)
    mom_v = dict(ffn1_norm=v_ffn1_norm, ffn1_w_gate=v_ffn1_w_gate, ffn1_w_up=v_ffn1_w_up, ffn1_w_down=v_ffn1_w_down,
                 mix_norm=v_mix_norm, w_in=v_w_in, pool_w=v_pool_w, pool_scale=v_pool_scale, w_out=v_w_out,
                 ffn2_norm=v_ffn2_norm, ffn2_w_gate=v_ffn2_w_gate, ffn2_w_up=v_ffn2_w_up, ffn2_w_down=v_ffn2_w_down,
                 final_norm=v_final_norm)
    names = [t[0] for t in BIG]

    place = jnp.stack([lax.axis_index("c"), 2 * lax.axis_index("x") + lax.axis_index("y")]).astype(jnp.int32)
    def gather_start(l, after):
        return _exchange_start(_cast_layer(params, l, place), [], after, _own_half, _slot_half, f"gather_start{l}")

    def gather_end(started, after, l):
        send_sems, recv_sems, bufs, _, _ = started
        bufs, _ = _exchange_wait(send_sems, recv_sems, bufs, [], after, _own_half, _slot_half, f"gather_wait{l}")
        return {nm: a.reshape(NSH, rows, cols) for (nm, rows, cols), a in zip(BIG, _gather_forward(bufs))}

    tabs = _rope_tables(*_rope_inputs(positions))
    h = x.reshape(S, D)
    weights, saved = [], []
    started = gather_start(0, place)
    after = started[-1]
    for l in range(DEPTH):
        weights.append(gather_end(started, after, l))
        dep = None
        if l + 1 < DEPTH:
            started = gather_start(l + 1, weights[l]["g1"])
            dep = started[-1]
        h, sv = _layer_fwd(h, weights[l], params, l, tabs, dep)
        saved.append(sv)
        after = h
    loss, dx, dgf = _final_loss(h, final_norm.reshape(1, D), loss_target.reshape(S, D))

    full = [lax.empty((DEPTH, 2, rows // 2, cols), f32) for _, rows, cols in BIG]
    sg = {k: [None] * DEPTH for k in ("ffn1_norm", "mix_norm", "pool_w", "pool_scale", "ffn2_norm")}
    sg["final_norm"] = dgf

    def reduce_end(started, after, l, full):
        send_sems, recv_sems, psum, parts, _ = started
        psum, parts = _exchange_wait(send_sems, recv_sems, psum, parts, after, _slot, _slot, f"grad_wait{l}")
        return _sibling_share(_chip_sum(psum, parts, full, place, l), l)

    started, dep = None, None
    for l in reversed(range(DEPTH)):
        dx, gr, sgl = _layer_bwd(dx, weights[l], saved[l], tabs, dep)
        for k, v in sgl.items():
            sg[k][l] = v
        if started is not None:
            full = reduce_end(started, dx, l + 1, full)
        grads = [gr[nm] for nm in names]
        psum = _pair_sum(grads, _sibling_swap(grads), place)
        parts = [lax.empty(a.shape, bf16) for a in psum]
        started = _exchange_start(psum, parts, place, _slot, _slot, f"grad_start{l}")
        dep = started[-1]
    full = reduce_end(started, dep, 0, full)
    big_grad = {BIG_SRC[nm]: a.reshape(DEPTH, rows, cols) for (nm, rows, cols), a in zip(BIG, full)}

    per_layer = {k: sg[k] for k in ("ffn1_norm", "mix_norm", "ffn2_norm", "pool_w", "pool_scale")}
    small_sum = _allreduce_small(_pack_small(per_layer, sg["final_norm"], loss))
    gs, ds_, ms, vs = _adamw(_pack_small_params(params).reshape(1, SMALL_ROWS, 128), small_sum.reshape(1, SMALL_ROWS, 128),
                             _pack_small_params(mom_m).reshape(1, SMALL_ROWS, 128),
                             _pack_small_params(mom_v).reshape(1, SMALL_ROWS, 128), "adamw_small")
    small_out = [_unpack_small(a.reshape(SMALL_ROWS, 128)) for a in (gs, ds_, ms, vs)]

    grad, delta, new_m, new_v = {}, {}, {}, {}
    for k in WEIGHT_ORDER:
        if k in SMALL_NAMES:
            grad[k], delta[k], new_m[k], new_v[k] = (so[k] for so in small_out)
        else:
            grad[k], delta[k], new_m[k], new_v[k] = _adamw(params[k], big_grad[k], mom_m[k], mom_v[k], "adamw_" + k)
    return (small_out[0]["loss"], dx.reshape(1, S, D), *[grad[k] for k in WEIGHT_ORDER], *[delta[k] for k in WEIGHT_ORDER],
            *[new_m[k] for k in WEIGHT_ORDER], *[new_v[k] for k in WEIGHT_ORDER])
```

```python
import functools

import jax
import jax.numpy as jnp
import numpy as np
from jax import lax
from jax.experimental import pallas as pl
from jax.experimental.pallas import tpu as pltpu

f32 = jnp.float32
bf16 = jnp.bfloat16

S = 2048
D = 1024
DEPTH = 4
NSH = 4
FS = 704
PROJ = 2560
PS = 640
PW = 256
AW = 768
NPAIR = 6
NORM_EPS = 1e-6
MASK_VALUE = -1e30
ROPE_THETA = 500000.0
DILATIONS = (1, 4, 16)
QBLK = 128
NBLK = S // QBLK
TM = 512
VMEM_LIMIT = 56 * 1024 * 1024

ADAM_LR = 0.001
ADAM_B1 = 0.9
ADAM_B2 = 0.999
ADAM_EPS = 1e-08
ADAM_WD = 0.01
ADAM_STEP = 10

MESH = pl.DeviceIdType.MESH
ANY = pl.BlockSpec(memory_space=pl.ANY)

BIG = (("g1", FS, D), ("u1", FS, D), ("d1", FS, D), ("wi", D, PS), ("wo", PW, D), ("g2", FS, D), ("u2", FS, D), ("d2", FS, D))
TRANSPOSED = ("ffn1_w_gate", "ffn1_w_up", "ffn2_w_gate", "ffn2_w_up")
BIG_SRC = {"g1": "ffn1_w_gate", "u1": "ffn1_w_up", "d1": "ffn1_w_down", "wi": "w_in", "wo": "w_out",
           "g2": "ffn2_w_gate", "u2": "ffn2_w_up", "d2": "ffn2_w_down"}


def _cp(*sem):
    return pltpu.CompilerParams(dimension_semantics=sem if sem else None, vmem_limit_bytes=VMEM_LIMIT)


def _sds(shape, dt):
    return jax.ShapeDtypeStruct(shape, dt)


def _dot(a, b):
    return jnp.dot(a, b, preferred_element_type=f32)


def _dot_nt(a, b):
    return lax.dot_general(a, b, (((1,), (1,)), ((), ())), preferred_element_type=f32)


def _dot_tn(a, b):
    return lax.dot_general(a, b, (((0,), (0,)), ((), ())), preferred_element_type=f32)


def _dep(dep):
    return ([], []) if dep is None else ([ANY], [dep])


def _rms_fwd(x, g, dep=None):
    def body(x_ref, g_ref, *rest):
        h_ref = rest[-1]
        xf = x_ref[...]
        r = lax.rsqrt(jnp.mean(xf * xf, axis=-1, keepdims=True) + NORM_EPS)
        h_ref[...] = ((xf * r) * g_ref[...]).astype(bf16)

    dspec, dop = _dep(dep)
    return pl.pallas_call(
        body, out_shape=_sds((S, D), bf16), grid=(S // TM,),
        in_specs=[pl.BlockSpec((TM, D), lambda i: (i, 0)), pl.BlockSpec((1, D), lambda i: (0, 0))] + dspec,
        out_specs=pl.BlockSpec((TM, D), lambda i: (i, 0)), name="rms_fwd", compiler_params=_cp("parallel"))(x, g, *dop)


def _ffn_up(h, wg, wu):
    def body(h_ref, wg_ref, wu_ref, a_ref, b_ref, u_ref):
        hh = h_ref[...]
        a = _dot_nt(hh, wg_ref[0])
        b = _dot_nt(hh, wu_ref[0])
        a_ref[0] = a
        b_ref[0] = b
        u_ref[0] = (a * (1.0 / (1.0 + jnp.exp(-a))) * b).astype(bf16)

    wspec = pl.BlockSpec((1, FS, D), lambda s, i: (s, 0, 0))
    ospec = pl.BlockSpec((1, TM, FS), lambda s, i: (s, i, 0))
    return pl.pallas_call(
        body, out_shape=(_sds((NSH, S, FS), f32), _sds((NSH, S, FS), f32), _sds((NSH, S, FS), bf16)),
        grid=(NSH, S // TM), in_specs=[pl.BlockSpec((TM, D), lambda s, i: (i, 0)), wspec, wspec],
        out_specs=(ospec, ospec, ospec), name="ffn_up", compiler_params=_cp("parallel", "parallel"))(h, wg, wu)


def _ffn_down(x, u, wd):
    def body(x_ref, u_ref, wd_ref, o_ref, acc_ref):
        s = pl.program_id(1)

        @pl.when(s == 0)
        def _():
            acc_ref[...] = jnp.zeros_like(acc_ref)

        acc_ref[...] += _dot(u_ref[0], wd_ref[0])

        @pl.when(s == NSH - 1)
        def _():
            o_ref[...] = x_ref[...] + 0.5 * acc_ref[...]

    return pl.pallas_call(
        body, out_shape=_sds((S, D), f32), grid=(S // TM, NSH),
        in_specs=[pl.BlockSpec((TM, D), lambda i, s: (i, 0)), pl.BlockSpec((1, TM, FS), lambda i, s: (s, i, 0)),
                  pl.BlockSpec((1, FS, D), lambda i, s: (s, 0, 0))],
        out_specs=pl.BlockSpec((TM, D), lambda i, s: (i, 0)), scratch_shapes=[pltpu.VMEM((TM, D), f32)],
        name="ffn_down", compiler_params=_cp("parallel", "arbitrary"))(x, u, wd)


def _in_proj(h, wi):
    def body(h_ref, w_ref, o_ref):
        o_ref[...] = _dot(h_ref[...], w_ref[0])

    return pl.pallas_call(
        body, out_shape=_sds((S, PROJ), f32), grid=(NSH, S // TM),
        in_specs=[pl.BlockSpec((TM, D), lambda s, i: (i, 0)), pl.BlockSpec((1, D, PS), lambda s, i: (s, 0, 0))],
        out_specs=pl.BlockSpec((TM, PS), lambda s, i: (i, s)), name="in_proj", compiler_params=_cp("parallel", "parallel"))(h, wi)


def _out_proj(x, mixed, wo):
    def body(x_ref, m_ref, w_ref, o_ref):
        o_ref[...] = x_ref[...] + _dot(m_ref[...], w_ref[...].reshape(D, D))

    return pl.pallas_call(
        body, out_shape=_sds((S, D), f32), grid=(S // TM,),
        in_specs=[pl.BlockSpec((TM, D), lambda i: (i, 0)), pl.BlockSpec((TM, D), lambda i: (i, 0)),
                  pl.BlockSpec((NSH, PW, D), lambda i: (0, 0, 0))],
        out_specs=pl.BlockSpec((TM, D), lambda i: (i, 0)), name="out_proj", compiler_params=_cp("parallel"))(x, mixed, wo)


def _out_proj_bwd(dx, wo):
    def body(dx_ref, w_ref, o_ref):
        o_ref[...] = _dot_nt(dx_ref[...].astype(bf16), w_ref[...].reshape(D, D))

    return pl.pallas_call(
        body, out_shape=_sds((S, D), f32), grid=(S // TM,),
        in_specs=[pl.BlockSpec((TM, D), lambda i: (i, 0)), pl.BlockSpec((NSH, PW, D), lambda i: (0, 0, 0))],
        out_specs=pl.BlockSpec((TM, D), lambda i: (i, 0)), name="out_proj_bwd", compiler_params=_cp("parallel"))(dx, wo)


def _ffn_bwd_hidden(dx, wd, a, b, dep=None):
    def body(dx_ref, wd_ref, a_ref, b_ref, *rest):
        da_ref, db_ref = rest[-2:]
        dy = (0.5 * dx_ref[...]).astype(bf16)
        du = _dot_nt(dy, wd_ref[0])
        a = a_ref[0]
        sig = 1.0 / (1.0 + jnp.exp(-a))
        da_ref[0] = (du * b_ref[0] * (sig * (1.0 + a * (1.0 - sig)))).astype(bf16)
        db_ref[0] = (du * (a * sig)).astype(bf16)

    hspec = pl.BlockSpec((1, TM, FS), lambda s, i: (s, i, 0))
    dspec, dop = _dep(dep)
    return pl.pallas_call(
        body, out_shape=(_sds((NSH, S, FS), bf16), _sds((NSH, S, FS), bf16)), grid=(NSH, S // TM),
        in_specs=[pl.BlockSpec((TM, D), lambda s, i: (i, 0)), pl.BlockSpec((1, FS, D), lambda s, i: (s, 0, 0)), hspec, hspec] + dspec,
        out_specs=(hspec, hspec), name="ffn_bwd_hidden", compiler_params=_cp("parallel", "parallel"))(dx, wd, a, b, *dop)


def _norm_bwd_tail(acc, x_ref, dxin_ref, g_ref, dxo_ref, dg_ref, first):
    xf = x_ref[...]
    r = lax.rsqrt(jnp.mean(xf * xf, axis=-1, keepdims=True) + NORM_EPS)
    xhat = xf * r
    dhg = acc * g_ref[...]
    dxo_ref[...] = dxin_ref[...] + r * (dhg - xhat * jnp.mean(dhg * xhat, axis=-1, keepdims=True))
    part = jnp.sum(acc * xhat, axis=0, keepdims=True)

    @pl.when(first)
    def _():
        dg_ref[...] = part

    @pl.when(jnp.logical_not(first))
    def _():
        dg_ref[...] += part


def _ffn_bwd_dx(dx, x_in, g, da, db, wg, wu):
    def body(dx_ref, x_ref, g_ref, da_ref, db_ref, wg_ref, wu_ref, dxo_ref, dg_ref, acc_ref):
        i, s = pl.program_id(0), pl.program_id(1)

        @pl.when(s == 0)
        def _():
            acc_ref[...] = jnp.zeros_like(acc_ref)

        acc_ref[...] += _dot(da_ref[0], wg_ref[0]) + _dot(db_ref[0], wu_ref[0])

        @pl.when(s == NSH - 1)
        def _():
            _norm_bwd_tail(acc_ref[...], x_ref, dx_ref, g_ref, dxo_ref, dg_ref, i == 0)

    tok = pl.BlockSpec((TM, D), lambda i, s: (i, 0))
    vec = pl.BlockSpec((1, D), lambda i, s: (0, 0))
    hid = pl.BlockSpec((1, TM, FS), lambda i, s: (s, i, 0))
    wsp = pl.BlockSpec((1, FS, D), lambda i, s: (s, 0, 0))
    return pl.pallas_call(
        body, out_shape=(_sds((S, D), f32), _sds((1, D), f32)), grid=(S // TM, NSH),
        in_specs=[tok, tok, vec, hid, hid, wsp, wsp], out_specs=(tok, vec), scratch_shapes=[pltpu.VMEM((TM, D), f32)],
        name="ffn_bwd_dx", compiler_params=_cp("arbitrary", "arbitrary"))(dx, x_in, g, da, db, wg, wu)


def _in_proj_bwd_dx(dx, x_in, g, dproj, wi):
    def body(dx_ref, x_ref, g_ref, dp_ref, w_ref, dxo_ref, dg_ref, acc_ref):
        i, s = pl.program_id(0), pl.program_id(1)

        @pl.when(s == 0)
        def _():
            acc_ref[...] = jnp.zeros_like(acc_ref)

        acc_ref[...] += _dot_nt(dp_ref[...], w_ref[0])

        @pl.when(s == NSH - 1)
        def _():
            _norm_bwd_tail(acc_ref[...], x_ref, dx_ref, g_ref, dxo_ref, dg_ref, i == 0)

    tok = pl.BlockSpec((TM, D), lambda i, s: (i, 0))
    vec = pl.BlockSpec((1, D), lambda i, s: (0, 0))
    return pl.pallas_call(
        body, out_shape=(_sds((S, D), f32), _sds((1, D), f32)), grid=(S // TM, NSH),
        in_specs=[tok, tok, vec, pl.BlockSpec((TM, PS), lambda i, s: (i, s)), pl.BlockSpec((1, D, PS), lambda i, s: (s, 0, 0))],
        out_specs=(tok, vec), scratch_shapes=[pltpu.VMEM((TM, D), f32)],
        name="in_proj_bwd_dx", compiler_params=_cp("arbitrary", "arbitrary"))(dx, x_in, g, dproj, wi)


def _dw(lhs, rhs, lhs_spec, rhs_spec, rows, cols, name, rhs_scale=None, dep=None):
    def body(l_ref, r_ref, *rest):
        o_ref = rest[-1]
        l = l_ref[...].reshape(S, rows)
        r = r_ref[...].reshape(S, cols)
        if rhs_scale is not None:
            r = (rhs_scale * r).astype(bf16)
        o_ref[...] = _dot_tn(l, r).astype(bf16).reshape(1, 2, rows // 2, cols)

    dspec, dop = _dep(dep)
    return pl.pallas_call(
        body, out_shape=_sds((NSH, 2, rows // 2, cols), bf16), grid=(NSH,), in_specs=[lhs_spec, rhs_spec] + dspec,
        out_specs=pl.BlockSpec((1, 2, rows // 2, cols), lambda s: (s, 0, 0, 0)), name=name, compiler_params=_cp("parallel"))(lhs, rhs, *dop)


_WHOLE_TOK = pl.BlockSpec((S, D), lambda s: (0, 0))
_SHARD_HID = pl.BlockSpec((1, S, FS), lambda s: (s, 0, 0))


def _dw_up(h, da):
    return _dw(da, h, _SHARD_HID, _WHOLE_TOK, FS, D, "dw_up")


def _dw_down(u, dx, dep=None):
    return _dw(u, dx, _SHARD_HID, _WHOLE_TOK, FS, D, "dw_down", rhs_scale=0.5, dep=dep)


def _dw_in(h, dproj):
    return _dw(h, dproj, _WHOLE_TOK, pl.BlockSpec((S, PS), lambda s: (0, s)), D, PS, "dw_in")


def _dw_out(mixed, dx):
    return _dw(mixed, dx, pl.BlockSpec((S, PW), lambda s: (0, s)), _WHOLE_TOK, PW, D, "dw_out", rhs_scale=1.0)


def _final_loss(x, g, target):
    def body(x_ref, g_ref, t_ref, loss_ref, dx_ref, dg_ref):
        i = pl.program_id(0)
        xf = x_ref[...]
        r = lax.rsqrt(jnp.mean(xf * xf, axis=-1, keepdims=True) + NORM_EPS)
        xhat = xf * r
        err = xhat * g_ref[...] - t_ref[...]
        dy = err * (1.0 / D)
        dhg = dy * g_ref[...]
        dx_ref[...] = r * (dhg - xhat * jnp.mean(dhg * xhat, axis=-1, keepdims=True))
        part = jnp.sum(dy * xhat, axis=0, keepdims=True)
        lpart = jnp.zeros((8, 128), f32) + 0.5 * jnp.sum(jnp.mean(err * err, axis=-1, keepdims=True))

        @pl.when(i == 0)
        def _():
            dg_ref[...] = part
            loss_ref[...] = lpart

        @pl.when(i != 0)
        def _():
            dg_ref[...] += part
            loss_ref[...] += lpart

    tok = pl.BlockSpec((TM, D), lambda i: (i, 0))
    vec = pl.BlockSpec((1, D), lambda i: (0, 0))
    return pl.pallas_call(
        body, out_shape=(_sds((8, 128), f32), _sds((S, D), f32), _sds((1, D), f32)), grid=(S // TM,),
        in_specs=[tok, vec, tok], out_specs=(pl.BlockSpec((8, 128), lambda i: (0, 0)), tok, vec),
        name="final_loss", compiler_params=_cp("arbitrary"))(x, g, target)


def _shift_down(x, k, row):
    return jnp.where(row >= k, pltpu.roll(x, k, axis=0), 0.0)


def _shift_up(x, k, row):
    return jnp.where(row < S - k, pltpu.roll(x, S - k, axis=0), 0.0)


def _pool_geometry():
    row = lax.broadcasted_iota(jnp.int32, (S, PW), 0)
    grp = lax.broadcasted_iota(jnp.int32, (S, PW), 1) // 64
    half = jnp.where(grp == 0, 1, jnp.where(grp == 1, 2, jnp.where(grp == 2, 4, 8)))
    hi = jnp.minimum(row + half - 1, S - 1)
    lo = jnp.maximum(row - half, 0)
    return row, grp, (hi - lo + 1).astype(f32)


def _by_group(grp, v0, v1, v2, v3):
    return jnp.where(grp == 0, v0, jnp.where(grp == 1, v1, jnp.where(grp == 2, v2, v3)))


def _window_sums(x, row, grp, transpose):
    l1, r1 = x, x
    l2, r2 = l1 + _shift_down(l1, 1, row), r1 + _shift_up(r1, 1, row)
    l4, r4 = l2 + _shift_down(l2, 2, row), r2 + _shift_up(r2, 2, row)
    l8, r8 = l4 + _shift_down(l4, 4, row), r4 + _shift_up(r4, 4, row)
    lsel = _by_group(grp, l1, l2, l4, l8)
    rsel = _by_group(grp, r1, r2, r4, r8)
    if transpose:
        return lsel + _shift_up(rsel, 1, row)
    return _shift_down(lsel, 1, row) + rsel


def _pool_fwd(proj, wbd, scale):
    def body(v_ref, w_ref, sc_ref, mixed_ref, diff_ref):
        row, grp, cnt = _pool_geometry()
        v = v_ref[...]
        diff = (_window_sums(v, row, grp, False) / cnt - v).astype(bf16)
        diff_ref[...] = diff
        mixed_ref[...] = (_dot(diff, w_ref[...].astype(bf16)) * sc_ref[...]).astype(bf16)

    col = pl.BlockSpec((S, PW), lambda i: (0, 0))
    return pl.pallas_call(
        body, out_shape=(_sds((S, D), bf16), _sds((S, PW), bf16)), grid=(1,),
        in_specs=[col, pl.BlockSpec((PW, PW), lambda i: (0, 0)), pl.BlockSpec((1, PW), lambda i: (0, 0))],
        out_specs=(col, col), name="pool_fwd", compiler_params=_cp("arbitrary"))(proj, wbd, scale)


def _pool_bwd(dmixed, diff, wbd, scale, dproj):
    def body(dy_ref, diff_ref, w_ref, sc_ref, dproj_in, dv_ref, dw_ref, dsc_ref):
        del dproj_in
        row, grp, cnt = _pool_geometry()
        dy = dy_ref[...]
        diff = diff_ref[...]
        w = w_ref[...].astype(bf16)
        dsc_ref[...] = jnp.sum(dy * _dot(diff, w), axis=0, keepdims=True)
        dys = (dy * sc_ref[...]).astype(bf16)
        dw_ref[...] = _dot_tn(diff, dys)
        ddiff = _dot_nt(dys, w)
        dv_ref[...] = (_window_sums(ddiff / cnt, row, grp, True) - ddiff).astype(bf16)

    col = pl.BlockSpec((S, PW), lambda i: (0, 0))
    return pl.pallas_call(
        body, out_shape=(_sds((S, PROJ), bf16), _sds((PW, PW), f32), _sds((1, PW), f32)), grid=(1,),
        in_specs=[col, col, pl.BlockSpec((PW, PW), lambda i: (0, 0)), pl.BlockSpec((1, PW), lambda i: (0, 0)), ANY],
        out_specs=(col, pl.BlockSpec((PW, PW), lambda i: (0, 0)), pl.BlockSpec((1, PW), lambda i: (0, 0))),
        input_output_aliases={4: 0}, name="pool_bwd", compiler_params=_cp("arbitrary"))(dmixed, diff, wbd, scale, dproj)


def _rope_tables(pos_col, freq_row):
    def body(p_ref, f_ref, c_ref, a_ref, b_ref):
        ang = p_ref[...].astype(f32) * f_ref[...]
        l64 = lax.broadcasted_iota(jnp.int32, (S, 128), 1) % 64
        cos, sin = jnp.cos(ang), jnp.sin(ang)
        c_ref[...] = jnp.where(l64 < 16, cos, 1.0)
        a_ref[...] = jnp.where(l64 < 8, -sin, 0.0)
        b_ref[...] = jnp.where((l64 >= 8) & (l64 < 16), sin, 0.0)

    t = _sds((S, 128), f32)
    return pl.pallas_call(body, out_shape=(t, t, t), name="rope_tables", compiler_params=_cp())(pos_col, freq_row)


def _rope(t, c, a, b):
    return t * c + pltpu.roll(t, 120, axis=1) * a + pltpu.roll(t, 8, axis=1) * b


def _rope_bwd(g, c, a, b):
    return g * c + pltpu.roll(g * a, 8, axis=1) + pltpu.roll(g * b, 120, axis=1)


def _perm_load(ref, d):
    if d == 1:
        return ref[...]
    n = S // d
    return jnp.concatenate([ref[pl.ds(r, n, stride=d), :] for r in range(d)], axis=0)


def _unperm_store(ref, val, d):
    if d == 1:
        ref[...] = val
        return
    n = S // d
    for r in range(d):
        ref[pl.ds(r, n, stride=d), :] = val[r * n:(r + 1) * n, :]


def _band(xp):
    z = jnp.zeros((64, 128), bf16)
    p = jnp.concatenate([z, xp, z], axis=0).reshape(NBLK + 1, QBLK, 128)
    return jnp.concatenate([p[:NBLK], p[1:]], axis=1)


def _unband(xb):
    z = jnp.zeros((1, QBLK, 128), f32)
    p = jnp.concatenate([xb[:, :QBLK], z], axis=0) + jnp.concatenate([z, xb[:, QBLK:]], axis=0)
    return p.reshape(S + QBLK, 128)[64:S + 64]


def _band_mask(d):
    blocks_per_class = NBLK // d
    n = lax.broadcasted_iota(jnp.int32, (NBLK, 1, 2 * QBLK), 0) & (blocks_per_class - 1)
    be = lax.broadcasted_iota(jnp.int32, (NBLK, 1, 2 * QBLK), 2)
    a = lax.broadcasted_iota(jnp.int32, (1, 2 * QBLK, 2 * QBLK), 1) & (QBLK - 1)
    b = lax.broadcasted_iota(jnp.int32, (1, 2 * QBLK, 2 * QBLK), 2)
    band = (b >= a) & (b <= a + 128)
    edge = ((be >= 64) | (n != 0)) & ((be < QBLK + 64) | (n != blocks_per_class - 1))
    return band & edge


def _stack_heads(xb, lo):
    z = jnp.zeros_like(xb)
    return jnp.concatenate([jnp.where(lo, xb, z), jnp.where(lo, z, xb)], axis=1)


def _unstack_heads(x2, lo):
    return jnp.where(lo, x2[:, :QBLK], x2[:, QBLK:])


def _rows_to_lanes(col2, lo):
    return jnp.where(lo, jnp.broadcast_to(col2[:, :QBLK], (NBLK, QBLK, 128)), jnp.broadcast_to(col2[:, QBLK:], (NBLK, QBLK, 128)))


def _bmm_nt(a, b):
    return jnp.einsum('nqd,nkd->nqk', a, b, preferred_element_type=f32)


def _bmm_nn(a, b):
    return jnp.einsum('nqk,nkd->nqd', a, b, preferred_element_type=f32)


def _bmm_tn(a, b):
    return jnp.einsum('nqk,nqd->nkd', a, b, preferred_element_type=f32)


def _attn_fwd(proj, tc, ta, tb, mixed):
    def body(q_ref, k_ref, v_ref, c_ref, a_ref, b_ref, mixed_in, mixed_ref, o_ref, lse_ref, qn, kn, t_num, t_m, t_den):
        del mixed_in
        lo = lax.broadcasted_iota(jnp.int32, (1, 1, 128), 2) < 64
        c, a, b = c_ref[...], a_ref[...], b_ref[...]
        qn[...] = _rope(q_ref[...], c, a, b)
        kn[...] = _rope(k_ref[...], c, a, b)
        run = None
        for d in DILATIONS:
            q2 = _stack_heads(_perm_load(qn, d).astype(bf16).reshape(NBLK, QBLK, 128), lo)
            kb = _band(_perm_load(kn, d).astype(bf16))
            vb = _band(_perm_load(v_ref, d).astype(bf16))
            s = jnp.where(_band_mask(d), _bmm_nt(q2, kb) * 0.125, MASK_VALUE)
            m = jnp.max(s, axis=2, keepdims=True)
            p = jnp.exp(s - m)
            den = jnp.sum(p, axis=2, keepdims=True)
            num = _unstack_heads(_bmm_nn(p.astype(bf16), vb), lo)
            _unperm_store(t_num, num.reshape(S, 128), d)
            _unperm_store(t_m, _rows_to_lanes(m, lo).reshape(S, 128), d)
            _unperm_store(t_den, _rows_to_lanes(den, lo).reshape(S, 128), d)
            if run is None:
                run = (t_m[...], t_num[...], t_den[...])
            else:
                m_new = jnp.maximum(run[0], t_m[...])
                w_old, w_new = jnp.exp(run[0] - m_new), jnp.exp(t_m[...] - m_new)
                run = (m_new, w_old * run[1] + w_new * t_num[...], w_old * run[2] + w_new * t_den[...])
        out = run[1] / run[2]
        o_ref[...] = out
        mixed_ref[...] = out.astype(bf16)
        lse_ref[...] = run[0] + jnp.log(run[2])

    def col(off):
        return pl.BlockSpec((S, 128), lambda j, off=off: (0, off + j))

    tab = pl.BlockSpec((S, 128), lambda j: (0, 0))
    scr = pltpu.VMEM((S, 128), f32)
    return pl.pallas_call(
        body, out_shape=(_sds((S, D), bf16), _sds((S, AW), f32), _sds((S, AW), f32)), grid=(NPAIR,),
        in_specs=[col(2), col(8), col(14), tab, tab, tab, ANY], out_specs=(col(2), col(0), col(0)),
        scratch_shapes=[scr, scr, scr, scr, scr], input_output_aliases={6: 0}, name="attn_fwd",
        compiler_params=_cp("arbitrary"))(proj, proj, proj, tc, ta, tb, mixed)


def _attn_bwd(proj, tc, ta, tb, o, lse, dmixed):
    def body(q_ref, k_ref, v_ref, c_ref, a_ref, b_ref, o_ref, lse_ref, do_ref, dp_ref, qn, kn, tmp, dk_s, dv_s):
        t = pl.program_id(1)

        @pl.when(t == 0)
        def _():
            lo = lax.broadcasted_iota(jnp.int32, (1, 1, 128), 2) < 64
            c, a, b = c_ref[...], a_ref[...], b_ref[...]
            qn[...] = _rope(q_ref[...], c, a, b)
            kn[...] = _rope(k_ref[...], c, a, b)
            dq = dk = dv = None
            for d in DILATIONS:
                q2 = _stack_heads(_perm_load(qn, d).astype(bf16).reshape(NBLK, QBLK, 128), lo)
                kb = _band(_perm_load(kn, d).astype(bf16))
                vb = _band(_perm_load(v_ref, d).astype(bf16))
                dob = _perm_load(do_ref, d).reshape(NBLK, QBLK, 128)
                ob = _perm_load(o_ref, d).reshape(NBLK, QBLK, 128)
                lsb = _perm_load(lse_ref, d).reshape(NBLK, QBLK, 128)
                do2 = _stack_heads(dob.astype(bf16), lo)
                delta2 = jnp.sum(_stack_heads(dob * ob, lo), axis=2, keepdims=True)
                lse2 = jnp.max(jnp.concatenate([jnp.where(lo, lsb, MASK_VALUE), jnp.where(lo, MASK_VALUE, lsb)], axis=1),
                               axis=2, keepdims=True)
                s = _bmm_nt(q2, kb) * 0.125
                p = jnp.where(_band_mask(d), jnp.exp(s - lse2), 0.0)
                ds = (p * (_bmm_nt(do2, vb) - delta2) * 0.125).astype(bf16)
                pb = p.astype(bf16)
                dq_b = _unstack_heads(_bmm_nn(ds, kb), lo).reshape(S, 128)
                dk_b = _unband(_bmm_tn(ds, q2))
                dv_b = _unband(_bmm_tn(pb, do2))
                acc = []
                for prev, new in ((dq, dq_b), (dk, dk_b), (dv, dv_b)):
                    _unperm_store(tmp, new, d)
                    acc.append(tmp[...] if prev is None else prev + tmp[...])
                dq, dk, dv = acc
            dp_ref[...] = _rope_bwd(dq, c, a, b).astype(bf16)
            dk_s[...] = _rope_bwd(dk, c, a, b).astype(bf16)
            dv_s[...] = dv.astype(bf16)

        @pl.when(t == 1)
        def _():
            dp_ref[...] = dk_s[...]

        @pl.when(t == 2)
        def _():
            dp_ref[...] = dv_s[...]

    def col(off):
        return pl.BlockSpec((S, 128), lambda j, t, off=off: (0, off + j))

    tab = pl.BlockSpec((S, 128), lambda j, t: (0, 0))
    scr = pltpu.VMEM((S, 128), f32)
    scb = pltpu.VMEM((S, 128), bf16)
    return pl.pallas_call(
        body, out_shape=_sds((S, PROJ), bf16), grid=(NPAIR, 3),
        in_specs=[col(2), col(8), col(14), tab, tab, tab, col(0), col(0), col(2)],
        out_specs=pl.BlockSpec((S, 128), lambda j, t: (0, 2 + NPAIR * t + j)),
        scratch_shapes=[scr, scr, scr, scb, scb], name="attn_bwd",
        compiler_params=_cp("arbitrary", "arbitrary"))(proj, proj, proj, tc, ta, tb, o, lse, dmixed)


def _block_diag(w4):
    out = jnp.zeros((PW, PW), w4.dtype)
    for g in range(4):
        out = out.at[64 * g:64 * (g + 1), 64 * g:64 * (g + 1)].set(w4[g])
    return out


def _diag_blocks(w):
    return jnp.stack([w[64 * g:64 * (g + 1), 64 * g:64 * (g + 1)] for g in range(4)])


def _rope_inputs(positions):
    inv_freq = ROPE_THETA ** (-jnp.arange(0, 16, 2, dtype=f32) / 16)
    l64 = np.arange(128) % 64
    idx = np.where(l64 < 16, l64 % 8, 0)
    return positions.reshape(S, 1), inv_freq[idx].reshape(1, 128)


def _layer_fwd(x, w, small, l, tabs, dep=None):
    g1, gm, g2 = (small[k][l].reshape(1, D) for k in ("ffn1_norm", "mix_norm", "ffn2_norm"))
    wbd = _block_diag(small["pool_w"][l])
    psc = small["pool_scale"][l].reshape(1, PW)
    h1 = _rms_fwd(x, g1, dep)
    a1, b1, u1 = _ffn_up(h1, w["g1"], w["u1"])
    x1 = _ffn_down(x, u1, w["d1"])
    h2 = _rms_fwd(x1, gm)
    proj = _in_proj(h2, w["wi"])
    mixed, diff = _pool_fwd(proj, wbd, psc)
    mixed, o, lse = _attn_fwd(proj, *tabs, mixed)
    x2 = _out_proj(x1, mixed, w["wo"])
    h3 = _rms_fwd(x2, g2)
    a2, b2, u2 = _ffn_up(h3, w["g2"], w["u2"])
    out = _ffn_down(x2, u2, w["d2"])
    return out, dict(x0=x, h1=h1, a1=a1, b1=b1, u1=u1, x1=x1, h2=h2, proj=proj, mixed=mixed, diff=diff, o=o, lse=lse,
                     x2=x2, h3=h3, a2=a2, b2=b2, u2=u2, g1=g1, gm=gm, g2=g2, wbd=wbd, psc=psc)


def _layer_bwd(dx, w, sv, tabs, dep=None):
    gr, sg = {}, {}
    da, db = _ffn_bwd_hidden(dx, w["d2"], sv["a2"], sv["b2"], dep)
    gr["d2"] = _dw_down(sv["u2"], dx, dep)
    gr["g2"] = _dw_up(sv["h3"], da)
    gr["u2"] = _dw_up(sv["h3"], db)
    dx, sg["ffn2_norm"] = _ffn_bwd_dx(dx, sv["x2"], sv["g2"], da, db, w["g2"], w["u2"])
    gr["wo"] = _dw_out(sv["mixed"], dx)
    dmixed = _out_proj_bwd(dx, w["wo"])
    dproj = _attn_bwd(sv["proj"], *tabs, sv["o"], sv["lse"], dmixed)
    dproj, dwbd, sg["pool_scale"] = _pool_bwd(dmixed, sv["diff"], sv["wbd"], sv["psc"], dproj)
    sg["pool_w"] = _diag_blocks(dwbd)
    gr["wi"] = _dw_in(sv["h2"], dproj)
    dx, sg["mix_norm"] = _in_proj_bwd_dx(dx, sv["x1"], sv["gm"], dproj, w["wi"])
    da, db = _ffn_bwd_hidden(dx, w["d1"], sv["a1"], sv["b1"])
    gr["d1"] = _dw_down(sv["u1"], dx)
    gr["g1"] = _dw_up(sv["h1"], da)
    gr["u1"] = _dw_up(sv["h1"], db)
    dx, sg["ffn1_norm"] = _ffn_bwd_dx(dx, sv["x0"], sv["g1"], da, db, w["g1"], w["u1"])
    return dx, gr, sg


def _forward_backward(x, positions, target, gathered, small):
    tabs = _rope_tables(*_rope_inputs(positions))
    saved = []
    for l in range(DEPTH):
        x, sv = _layer_fwd(x, gathered[l], small, l, tabs)
        saved.append(sv)
    loss, dx, dgf = _final_loss(x, small["final_norm"].reshape(1, D), target)
    big = [None] * DEPTH
    sg = {k: [None] * DEPTH for k in ("ffn1_norm", "mix_norm", "pool_w", "pool_scale", "ffn2_norm")}
    for l in reversed(range(DEPTH)):
        dx, big[l], sgl = _layer_bwd(dx, gathered[l], saved[l], tabs)
        for k, v in sgl.items():
            sg[k][l] = v
    sg["final_norm"] = dgf
    return loss, dx, big, sg


def _place():
    x, y, c = lax.axis_index("x"), lax.axis_index("y"), lax.axis_index("c")
    chips = [(1 - x, y), (x, 1 - y), (1 - x, 1 - y)]
    return x, y, c, chips


def _cast_layer(params, l, place):
    def body(p_ref, *refs):
        del p_ref
        for i_ref, o_ref in zip(refs[:8], refs[8:]):
            o_ref[...] = i_ref[...].astype(bf16).reshape(o_ref.shape)

    ins, in_specs, out_shape, out_specs = [], [], [], []
    for name, rows, cols in BIG:
        q = rows // 4
        ins.append(params[BIG_SRC[name]])
        in_specs.append(pl.BlockSpec((1, q, cols), lambda i, p, l=l: (l, i, 0)))
        out_shape.append(_sds((NSH, 2, rows // 2, cols), bf16))
        out_specs.append(pl.BlockSpec((1, 1, q, cols), lambda i, p: (p[1], i // 2, i % 2, 0)))
    return pl.pallas_call(
        body, out_shape=out_shape,
        grid_spec=pltpu.PrefetchScalarGridSpec(num_scalar_prefetch=1, grid=(4,), in_specs=in_specs, out_specs=out_specs),
        name=f"cast_layer{l}", compiler_params=_cp("parallel"))(place, *ins)


HBM = pl.BlockSpec(memory_space=pltpu.HBM)
SEM = pl.BlockSpec(memory_space=pltpu.SEMAPHORE)
_SPLIT = pltpu.CompilerParams(has_side_effects=pltpu.SideEffectType.DATAFLOW_SIDE_EFFECTING)


def _hbm(arrays):
    return [pltpu.with_memory_space_constraint(a, pltpu.HBM) for a in arrays]


def _chip_copies(src_of, dst_of, send_sems, recv_sems, n):
    x, y, c, chips = _place()
    me = 2 * x + y
    out = []
    for t in range(n):
        for k, chip in enumerate(chips):
            peer = 2 * chip[0] + chip[1]
            send = pltpu.make_async_remote_copy(
                src_ref=src_of(t, peer), dst_ref=dst_of(t, me), send_sem=send_sems.at[3 * t + k], recv_sem=recv_sems.at[3 * t + k],
                device_id=(chip[0], chip[1], c), device_id_type=MESH)
            land = pltpu.make_async_remote_copy(
                src_ref=src_of(t, peer), dst_ref=dst_of(t, peer), send_sem=send_sems.at[3 * t + k], recv_sem=recv_sems.at[3 * t + k],
                device_id=(chip[0], chip[1], c), device_id_type=MESH)
            out.append((send, land))
    return out


def _exchange_start(src, land, after, src_of, dst_of, name):
    n, m = len(src), len(src) + len(land)

    def body(*refs):
        src_refs = refs[:n]
        land_refs = refs[n:m] if land else src_refs
        send_sems, recv_sems = refs[m + 1], refs[m + 2]
        token = refs[-1]
        for send, _ in _chip_copies(lambda t, s: src_of(src_refs[t], s), lambda t, s: dst_of(land_refs[t], s), send_sems, recv_sems, n):
            send.start()
        token[...] = jnp.zeros_like(token)

    arrays = list(src) + list(land)
    out_shape = ([pltpu.SemaphoreType.DMA((3 * n,)), pltpu.SemaphoreType.DMA((3 * n,))] + [pltpu.HBM(a.shape, a.dtype) for a in arrays]
                 + [_sds((8, 128), f32)])
    res = pl.pallas_call(
        body, out_shape=out_shape, in_specs=[HBM] * m + [ANY], out_specs=[SEM, SEM] + [HBM] * m + [pl.BlockSpec(memory_space=pltpu.VMEM)],
        input_output_aliases={i: 2 + i for i in range(m)}, name=name, compiler_params=_SPLIT)(*_hbm(arrays), after)
    return res[0], res[1], list(res[2:2 + n]), list(res[2 + n:2 + m]), res[-1]


def _exchange_wait(send_sems, recv_sems, src, land, after, src_of, dst_of, name):
    n, m = len(src), len(src) + len(land)

    def body(*refs):
        src_refs = refs[:n]
        land_refs = refs[n:m] if land else src_refs
        send_sems, recv_sems = refs[m], refs[m + 1]
        for send, land_cp in _chip_copies(lambda t, s: src_of(src_refs[t], s), lambda t, s: dst_of(land_refs[t], s), send_sems, recv_sems, n):
            send.wait_send()
            land_cp.wait_recv()

    arrays = list(src) + list(land)
    res = pl.pallas_call(
        body, out_shape=[pltpu.HBM(a.shape, a.dtype) for a in arrays], in_specs=[HBM] * m + [SEM, SEM, ANY], out_specs=[HBM] * m,
        input_output_aliases={i: i for i in range(m)}, name=name, compiler_params=_SPLIT)(*arrays, send_sems, recv_sems, after)
    return list(res[:n]), list(res[n:])


def _own_half(ref, s):
    x, y, c, _ = _place()
    return ref.at[2 * x + y, c]


def _slot_half(ref, s):
    return ref.at[s, lax.axis_index("c")]


def _slot(ref, s):
    return ref.at[s]


def _gather_forward(bufs):
    n = len(bufs)

    def body(*refs):
        outs = refs[n:2 * n]
        send_sems, recv_sems = refs[2 * n:]
        x, y, c, chips = _place()
        sibling = (x, y, 1 - c)
        passed = []
        for t in range(n):
            for k, chip in enumerate(chips):
                blk = outs[t].at[2 * chip[0] + chip[1], c]
                cp = pltpu.make_async_remote_copy(
                    src_ref=blk, dst_ref=blk, send_sem=send_sems.at[t, k], recv_sem=recv_sems.at[t, k],
                    device_id=sibling, device_id_type=MESH)
                cp.start()
                passed.append(cp)
        for t in range(n):
            for k, chip in enumerate(chips):
                blk = outs[t].at[2 * chip[0] + chip[1], 1 - c]
                pltpu.make_async_remote_copy(
                    src_ref=blk, dst_ref=blk, send_sem=send_sems.at[t, k], recv_sem=recv_sems.at[t, k],
                    device_id=sibling, device_id_type=MESH).wait_recv()
        for cp in passed:
            cp.wait_send()

    out_shape = [_sds(a.shape, bf16) for a in bufs]
    return pl.pallas_call(
        body, out_shape=out_shape, in_specs=[ANY] * n, out_specs=[ANY] * n, input_output_aliases={t: t for t in range(n)},
        scratch_shapes=[pltpu.SemaphoreType.DMA((n, 3)), pltpu.SemaphoreType.DMA((n, 3))], name="gather_forward")(*bufs)


def _sibling_swap(grads):
    n = len(grads)

    def body(*refs):
        ins, outs = refs[:n], refs[n:2 * n]
        send_sems, recv_sems = refs[2 * n:]
        x, y, c, _ = _place()
        cps = []
        for t in range(n):
            for s in range(NSH):
                cp = pltpu.make_async_remote_copy(
                    src_ref=ins[t].at[s, 1 - c], dst_ref=outs[t].at[s], send_sem=send_sems.at[t, s], recv_sem=recv_sems.at[t, s],
                    device_id=(x, y, 1 - c), device_id_type=MESH)
                cp.start()
                cps.append(cp)
        for cp in cps:
            cp.wait()

    out_shape = [_sds((NSH,) + a.shape[2:], bf16) for a in grads]
    return pl.pallas_call(
        body, out_shape=out_shape, in_specs=[ANY] * n, out_specs=[ANY] * n,
        scratch_shapes=[pltpu.SemaphoreType.DMA((n, NSH)), pltpu.SemaphoreType.DMA((n, NSH))],
        name="sibling_swap")(*grads)


def _row_tile(h):
    return h // 2 if h % 32 == 0 else h


def _pair_sum(grads, got, c_idx):
    n = len(grads)

    def body(c_ref, *refs):
        del c_ref
        for t in range(n):
            refs[2 * n + t][...] = (refs[t][...].astype(f32).reshape(refs[n + t].shape) + refs[n + t][...].astype(f32)).astype(bf16)

    in_specs, out_shape, out_specs = [], [], []
    for a in grads:
        h, cols = a.shape[2:]
        in_specs.append(pl.BlockSpec((1, 1, _row_tile(h), cols), lambda s, i, c: (s, c[0], i, 0)))
    for a in grads:
        h, cols = a.shape[2:]
        in_specs.append(pl.BlockSpec((1, _row_tile(h), cols), lambda s, i, c: (s, i, 0)))
        out_shape.append(_sds((NSH, h, cols), bf16))
        out_specs.append(pl.BlockSpec((1, _row_tile(h), cols), lambda s, i, c: (s, i, 0)))
    return pl.pallas_call(
        body, out_shape=out_shape,
        grid_spec=pltpu.PrefetchScalarGridSpec(num_scalar_prefetch=1, grid=(NSH, 2), in_specs=in_specs, out_specs=out_specs),
        name="pair_sum", compiler_params=_cp("parallel", "parallel"))(c_idx, *grads, *got)


def _chip_sum(psum, parts, full, place, l, name):
    n = len(parts)

    def body(p_ref, *refs):
        s = pl.program_id(1)
        for t in range(n):
            val = jnp.where(s == p_ref[1], refs[t][0], refs[n + t][0]).astype(f32)
            out = refs[3 * n + t]

            @pl.when(s == 0)
            def _(out=out, val=val):
                out[0, 0] = val

            @pl.when(s != 0)
            def _(out=out, val=val):
                out[0, 0] += val

    own_specs, part_specs, out_shape, out_specs = [], [], [], []
    for a, fl in zip(parts, full):
        _, h, cols = a.shape
        r = _row_tile(h)
        own_specs.append(pl.BlockSpec((1, r, cols), lambda i, s, p: (p[1], i, 0)))
        part_specs.append(pl.BlockSpec((1, r, cols), lambda i, s, p: (jnp.where(s == p[1], (s + 1) % NSH, s), i, 0)))
        out_shape.append(_sds(fl.shape, f32))
        out_specs.append(pl.BlockSpec((1, 1, r, cols), lambda i, s, p, l=l: (l, p[0], i, 0)))
    return pl.pallas_call(
        body, out_shape=out_shape,
        grid_spec=pltpu.PrefetchScalarGridSpec(num_scalar_prefetch=1, grid=(2, NSH), in_specs=own_specs + part_specs + [ANY] * n,
                                               out_specs=out_specs),
        input_output_aliases={1 + 2 * n + t: t for t in range(n)}, name=name,
        compiler_params=_cp("parallel", "arbitrary"))(place, *psum, *parts, *full)


def _sibling_share(full, l, name):
    n = len(full)

    def body(*refs):
        outs = refs[n:2 * n]
        send_sems, recv_sems = refs[2 * n:]
        x, y, c, _ = _place()
        sibling = (x, y, 1 - c)
        cps = []
        for t in range(n):
            blk = outs[t].at[l, c]
            cp = pltpu.make_async_remote_copy(
                src_ref=blk, dst_ref=blk, send_sem=send_sems.at[t], recv_sem=recv_sems.at[t], device_id=sibling, device_id_type=MESH)
            cp.start()
            cps.append(cp)
        for t in range(n):
            blk = outs[t].at[l, 1 - c]
            pltpu.make_async_remote_copy(
                src_ref=blk, dst_ref=blk, send_sem=send_sems.at[t], recv_sem=recv_sems.at[t],
                device_id=sibling, device_id_type=MESH).wait_recv()
        for cp in cps:
            cp.wait_send()

    out_shape = [_sds(a.shape, f32) for a in full]
    return pl.pallas_call(
        body, out_shape=out_shape, in_specs=[ANY] * n, out_specs=[ANY] * n, input_output_aliases={t: t for t in range(n)},
        scratch_shapes=[pltpu.SemaphoreType.DMA((n,)), pltpu.SemaphoreType.DMA((n,))], name=name)(*full)


SMALL_ROWS = 656


def _pack_small(per_layer, final_vec, loss_tile):
    rows = []
    for l in range(DEPTH):
        for k in ("ffn1_norm", "mix_norm", "ffn2_norm"):
            rows.append(per_layer[k][l].reshape(8, 128))
        rows.append(per_layer["pool_w"][l].reshape(128, 128))
        rows.append(jnp.pad(per_layer["pool_scale"][l].reshape(2, 128), ((0, 6), (0, 0))))
    rows.append(final_vec.reshape(8, 128))
    rows.append(loss_tile)
    return jnp.concatenate(rows, axis=0)


def _unpack_small(buf):
    out = {k: [] for k in ("ffn1_norm", "mix_norm", "ffn2_norm", "pool_w", "pool_scale")}
    r = 0
    for l in range(DEPTH):
        for k in ("ffn1_norm", "mix_norm", "ffn2_norm"):
            out[k].append(buf[r:r + 8].reshape(D))
            r += 8
        out["pool_w"].append(buf[r:r + 128].reshape(4, 64, 64))
        r += 128
        out["pool_scale"].append(buf[r:r + 2].reshape(PW))
        r += 8
    res = {k: jnp.stack(v) for k, v in out.items()}
    res["final_norm"] = buf[r:r + 8].reshape(D)
    res["loss"] = buf[r + 8, 0]
    return res


def _allreduce_small(buf):
    def body(in_ref, out_ref, slots, send_sems, recv_sems):
        x, y, c, _ = _place()
        me = 4 * x + 2 * y + c
        slots[me] = in_ref[...]
        peers = []
        for k in range(1, 8):
            px, py, pc = x ^ (k >> 2), y ^ ((k >> 1) & 1), c ^ (k & 1)
            cp = pltpu.make_async_remote_copy(
                src_ref=in_ref, dst_ref=slots.at[me], send_sem=send_sems.at[k - 1], recv_sem=recv_sems.at[k - 1],
                device_id=(px, py, pc), device_id_type=MESH)
            cp.start()
            peers.append(cp)
        for k in range(1, 8):
            px, py, pc = x ^ (k >> 2), y ^ ((k >> 1) & 1), c ^ (k & 1)
            slot = 4 * px + 2 * py + pc
            pltpu.make_async_remote_copy(
                src_ref=slots.at[slot], dst_ref=slots.at[slot], send_sem=send_sems.at[k - 1], recv_sem=recv_sems.at[k - 1],
                device_id=(px, py, pc), device_id_type=MESH).wait_recv()
        for cp in peers:
            cp.wait_send()
        acc = slots[0]
        for j in range(1, 8):
            acc = acc + slots[j]
        out_ref[...] = acc

    return pl.pallas_call(
        body, out_shape=_sds((SMALL_ROWS, 128), f32),
        in_specs=[pl.BlockSpec(memory_space=pltpu.VMEM)], out_specs=pl.BlockSpec(memory_space=pltpu.VMEM),
        scratch_shapes=[pltpu.VMEM((8, SMALL_ROWS, 128), f32), pltpu.SemaphoreType.DMA((7,)), pltpu.SemaphoreType.DMA((7,))],
        name="allreduce_small", compiler_params=_cp())(buf)


def _adamw_math(w, g, m, v):
    m = ADAM_B1 * m + (1.0 - ADAM_B1) * g
    v = ADAM_B2 * v + (1.0 - ADAM_B2) * (g * g)
    m_hat = m / (1.0 - ADAM_B1 ** ADAM_STEP)
    v_hat = v / (1.0 - ADAM_B2 ** ADAM_STEP)
    return -ADAM_LR * (m_hat / (jnp.sqrt(v_hat) + ADAM_EPS) + ADAM_WD * w), m, v


def _adamw(w, g, m, v, name, first=0, prev=None, dep=None):
    def body(w_ref, g_ref, m_ref, v_ref, *rest):
        go_ref, d_ref, mo_ref, vo_ref = rest[-4:]
        g = g_ref[...]
        d, mn, vn = _adamw_math(w_ref[...], g, m_ref[...], v_ref[...])
        go_ref[...] = g
        d_ref[...] = d
        mo_ref[...] = mn
        vo_ref[...] = vn

    _, rows, cols = w.shape
    r = rows // 4 if rows % 32 == 0 else rows
    spec = pl.BlockSpec((1, r, cols), lambda i, j: (first + i, j, 0))
    gspec = pl.BlockSpec((1, r, cols), lambda i, j: (i, j, 0))
    out = _sds(w.shape, f32)
    extra = [] if prev is None else list(prev)
    dspec, dop = _dep(dep)
    return pl.pallas_call(
        body, out_shape=(out, out, out, out), grid=(g.shape[0], rows // r), in_specs=[spec, gspec, spec, spec] + [ANY] * len(extra) + dspec,
        out_specs=(spec,) * 4, input_output_aliases={4 + i: i for i in range(len(extra))}, name=name,
        compiler_params=_cp("parallel", "parallel"))(w, g, m, v, *extra, *dop)


SMALL_NAMES = ("ffn1_norm", "mix_norm", "pool_w", "pool_scale", "ffn2_norm", "final_norm")
WEIGHT_ORDER = ("ffn1_norm", "ffn1_w_gate", "ffn1_w_up", "ffn1_w_down", "mix_norm", "w_in", "pool_w", "pool_scale", "w_out",
                "ffn2_norm", "ffn2_w_gate", "ffn2_w_up", "ffn2_w_down", "final_norm")


def _pack_small_params(p):
    per_layer = {k: [p[k][l] for l in range(DEPTH)] for k in ("ffn1_norm", "mix_norm", "ffn2_norm", "pool_w", "pool_scale")}
    return _pack_small(per_layer, p["final_norm"], jnp.zeros((8, 128), f32))


def kernel(x, positions, ffn1_norm, ffn1_w_gate, ffn1_w_up, ffn1_w_down, mix_norm, w_in, pool_w, pool_scale, w_out, ffn2_norm, ffn2_w_gate, ffn2_w_up, ffn2_w_down, final_norm, loss_target, m_ffn1_norm, m_ffn1_w_gate, m_ffn1_w_up, m_ffn1_w_down, m_mix_norm, m_w_in, m_pool_w, m_pool_scale, m_w_out, m_ffn2_norm, m_ffn2_w_gate, m_ffn2_w_up, m_ffn2_w_down, m_final_norm, v_ffn1_norm, v_ffn1_w_gate, v_ffn1_w_up, v_ffn1_w_down, v_mix_norm, v_w_in, v_pool_w, v_pool_scale, v_w_out, v_ffn2_norm, v_ffn2_w_gate, v_ffn2_w_up, v_ffn2_w_down, v_final_norm):
    params = dict(ffn1_norm=ffn1_norm, ffn1_w_gate=ffn1_w_gate, ffn1_w_up=ffn1_w_up, ffn1_w_down=ffn1_w_down,
                  mix_norm=mix_norm, w_in=w_in, pool_w=pool_w, pool_scale=pool_scale, w_out=w_out, ffn2_norm=ffn2_norm,
                  ffn2_w_gate=ffn2_w_gate, ffn2_w_up=ffn2_w_up, ffn2_w_down=ffn2_w_down, final_norm=final_norm)
    mom_m = dict(ffn1_norm=m_ffn1_norm, ffn1_w_gate=m_ffn1_w_gate, ffn1_w_up=m_ffn1_w_up, ffn1_w_down=m_ffn1_w_down,
                 mix_norm=m_mix_norm, w_in=m_w_in, pool_w=m_pool_w, pool_scale=m_pool_scale, w_out=m_w_out,
                 ffn2_norm=m_ffn2_norm, ffn2_w_gate=m_ffn2_w_gate, ffn2_w_up=m_ffn2_w_up, ffn2_w_down=m_ffn2_w_down,
                 final_norm=m_final_norm)
    mom_v = dict(ffn1_norm=v_ffn1_norm, ffn1_w_gate=v_ffn1_w_gate, ffn1_w_up=v_ffn1_w_up, ffn1_w_down=v_ffn1_w_down,
                 mix_norm=v_mix_norm, w_in=v_w_in, pool_w=v_pool_w, pool_scale=v_pool_scale, w_out=v_w_out,
                 ffn2_norm=v_ffn2_norm, ffn2_w_gate=v_ffn2_w_gate, ffn2_w_up=v_ffn2_w_up, ffn2_w_down=v_ffn2_w_down,
                 final_norm=v_final_norm)
    names = [t[0] for t in BIG]
    for d in (params, mom_m, mom_v):
        for k in TRANSPOSED:
            d[k] = jnp.swapaxes(d[k], 1, 2)

    place = jnp.stack([lax.axis_index("c"), 2 * lax.axis_index("x") + lax.axis_index("y")]).astype(jnp.int32)
    def gather_start(l, after):
        return _exchange_start(_cast_layer(params, l, place), [], after, _own_half, _slot_half, f"gather_start{l}")

    def gather_end(started, after, l):
        send_sems, recv_sems, bufs, _, _ = started
        bufs, _ = _exchange_wait(send_sems, recv_sems, bufs, [], after, _own_half, _slot_half, f"gather_wait{l}")
        return {nm: a.reshape(NSH, rows, cols) for (nm, rows, cols), a in zip(BIG, _gather_forward(bufs))}

    tabs = _rope_tables(*_rope_inputs(positions))
    h = x.reshape(S, D)
    weights, saved = [], []
    started = gather_start(0, place)
    after = started[-1]
    for l in range(DEPTH):
        weights.append(gather_end(started, after, l))
        dep = None
        if l + 1 < DEPTH:
            started = gather_start(l + 1, weights[l]["g1"])
            dep = started[-1]
        h, sv = _layer_fwd(h, weights[l], params, l, tabs, dep)
        saved.append(sv)
        after = h
    loss, dx, dgf = _final_loss(h, final_norm.reshape(1, D), loss_target.reshape(S, D))

    upper = [lax.empty((DEPTH - 1, 2, rows // 2, cols), f32) for _, rows, cols in BIG]
    lower = [lax.empty((1, 2, rows // 2, cols), f32) for _, rows, cols in BIG]
    sg = {k: [None] * DEPTH for k in ("ffn1_norm", "mix_norm", "pool_w", "pool_scale", "ffn2_norm")}
    sg["final_norm"] = dgf

    def reduce_end(started, after, l, full, slot):
        send_sems, recv_sems, psum, parts, _ = started
        psum, parts = _exchange_wait(send_sems, recv_sems, psum, parts, after, _slot, _slot, f"grad_wait{l}")
        return _sibling_share(_chip_sum(psum, parts, full, place, slot, f"chip_sum{l}"), slot, f"sibling_share{l}")

    started, dep = None, None
    for l in reversed(range(DEPTH)):
        dx, gr, sgl = _layer_bwd(dx, weights[l], saved[l], tabs, dep)
        for k, v in sgl.items():
            sg[k][l] = v
        if started is not None:
            upper = reduce_end(started, dx, l + 1, upper, l)
        grads = [gr[nm] for nm in names]
        psum = _pair_sum(grads, _sibling_swap(grads), place)
        parts = [lax.empty(a.shape, bf16) for a in psum]
        started = _exchange_start(psum, parts, place, _slot, _slot, f"grad_start{l}")
        dep = started[-1]

    big_out = {}
    for (nm, rows, cols), g in zip(BIG, upper):
        k = BIG_SRC[nm]
        big_out[k] = _adamw(params[k], g.reshape(DEPTH - 1, rows, cols), mom_m[k], mom_v[k], "adamw_upper_" + k, first=1, dep=dep)
        dep = big_out[k][1]
    lower = reduce_end(started, dep, 0, lower, 0)
    for (nm, rows, cols), g in zip(BIG, lower):
        k = BIG_SRC[nm]
        big_out[k] = _adamw(params[k], g.reshape(1, rows, cols), mom_m[k], mom_v[k], "adamw_lower_" + k, first=0, prev=big_out[k])

    per_layer = {k: sg[k] for k in ("ffn1_norm", "mix_norm", "ffn2_norm", "pool_w", "pool_scale")}
    small_sum = _allreduce_small(_pack_small(per_layer, sg["final_norm"], loss))
    gs, ds_, ms, vs = _adamw(_pack_small_params(params).reshape(1, SMALL_ROWS, 128), small_sum.reshape(1, SMALL_ROWS, 128),
                             _pack_small_params(mom_m).reshape(1, SMALL_ROWS, 128),
                             _pack_small_params(mom_v).reshape(1, SMALL_ROWS, 128), "adamw_small")
    small_out = [_unpack_small(a.reshape(SMALL_ROWS, 128)) for a in (gs, ds_, ms, vs)]

    grad, delta, new_m, new_v = {}, {}, {}, {}
    for k in WEIGHT_ORDER:
        if k in SMALL_NAMES:
            grad[k], delta[k], new_m[k], new_v[k] = (so[k] for so in small_out)
        else:
            grad[k], delta[k], new_m[k], new_v[k] = big_out[k]
    for d in (grad, delta, new_m, new_v):
        for k in TRANSPOSED:
            d[k] = jnp.swapaxes(d[k], 1, 2)
    return (small_out[0]["loss"], dx.reshape(1, S, D), *[grad[k] for k in WEIGHT_ORDER], *[delta[k] for k in WEIGHT_ORDER],
            *[new_m[k] for k in WEIGHT_ORDER], *[new_v[k] for k in WEIGHT_ORDER])
```

```python
import functools

import jax
import jax.numpy as jnp
import numpy as np
from jax import lax
from jax.experimental import pallas as pl
from jax.experimental.pallas import tpu as pltpu

f32 = jnp.float32
bf16 = jnp.bfloat16

S = 2048
D = 1024
DEPTH = 4
NSH = 4
FS = 704
PROJ = 2560
PS = 640
PW = 256
AW = 768
NPAIR = 6
NORM_EPS = 1e-6
MASK_VALUE = -1e30
ROPE_THETA = 500000.0
DILATIONS = (1, 4, 16)
QBLK = 128
NBLK = S // QBLK
TM = 512
TMW = 1024
VMEM_LIMIT = 56 * 1024 * 1024

ADAM_LR = 0.001
ADAM_B1 = 0.9
ADAM_B2 = 0.999
ADAM_EPS = 1e-08
ADAM_WD = 0.01
ADAM_STEP = 10

MESH = pl.DeviceIdType.MESH
ANY = pl.BlockSpec(memory_space=pl.ANY)

BIG = (("g1", FS, D), ("u1", FS, D), ("d1", FS, D), ("wi", D, PS), ("wo", PW, D), ("g2", FS, D), ("u2", FS, D), ("d2", FS, D))
TRANSPOSED = ("ffn1_w_gate", "ffn1_w_up", "ffn2_w_gate", "ffn2_w_up")
BIG_SRC = {"g1": "ffn1_w_gate", "u1": "ffn1_w_up", "d1": "ffn1_w_down", "wi": "w_in", "wo": "w_out",
           "g2": "ffn2_w_gate", "u2": "ffn2_w_up", "d2": "ffn2_w_down"}


def _cp(*sem):
    return pltpu.CompilerParams(dimension_semantics=sem if sem else None, vmem_limit_bytes=VMEM_LIMIT)


def _sds(shape, dt):
    return jax.ShapeDtypeStruct(shape, dt)


def _dot(a, b):
    return jnp.dot(a, b, preferred_element_type=f32)


def _dot_nt(a, b):
    return lax.dot_general(a, b, (((1,), (1,)), ((), ())), preferred_element_type=f32)


def _dot_tn(a, b):
    return lax.dot_general(a, b, (((0,), (0,)), ((), ())), preferred_element_type=f32)


def _dep(dep):
    return ([], []) if dep is None else ([ANY], [dep])


def _rms_fwd(x, g, dep=None):
    def body(x_ref, g_ref, *rest):
        h_ref = rest[-1]
        xf = x_ref[...]
        r = lax.rsqrt(jnp.mean(xf * xf, axis=-1, keepdims=True) + NORM_EPS)
        h_ref[...] = ((xf * r) * g_ref[...]).astype(bf16)

    dspec, dop = _dep(dep)
    return pl.pallas_call(
        body, out_shape=_sds((S, D), bf16), grid=(S // TM,),
        in_specs=[pl.BlockSpec((TM, D), lambda i: (i, 0)), pl.BlockSpec((1, D), lambda i: (0, 0))] + dspec,
        out_specs=pl.BlockSpec((TM, D), lambda i: (i, 0)), name="rms_fwd", compiler_params=_cp("parallel"))(x, g, *dop)


def _ffn_up(h, wg, wu):
    def body(h_ref, wg_ref, wu_ref, a_ref, b_ref, u_ref):
        hh = h_ref[...]
        a = _dot_nt(hh, wg_ref[0])
        b = _dot_nt(hh, wu_ref[0])
        a_ref[0] = a.astype(bf16)
        b_ref[0] = b.astype(bf16)
        u_ref[0] = (a * (1.0 / (1.0 + jnp.exp(-a))) * b).astype(bf16)

    wspec = pl.BlockSpec((1, FS, D), lambda s, i: (s, 0, 0))
    ospec = pl.BlockSpec((1, TM, FS), lambda s, i: (s, i, 0))
    return pl.pallas_call(
        body, out_shape=(_sds((NSH, S, FS), bf16), _sds((NSH, S, FS), bf16), _sds((NSH, S, FS), bf16)),
        grid=(NSH, S // TM), in_specs=[pl.BlockSpec((TM, D), lambda s, i: (i, 0)), wspec, wspec],
        out_specs=(ospec, ospec, ospec), name="ffn_up", compiler_params=_cp("parallel", "parallel"))(h, wg, wu)


def _ffn_down(x, u, wd):
    def body(x_ref, u_ref, wd_ref, o_ref, acc_ref):
        s = pl.program_id(1)

        @pl.when(s == 0)
        def _():
            acc_ref[...] = jnp.zeros_like(acc_ref)

        acc_ref[...] += _dot(u_ref[0], wd_ref[0])

        @pl.when(s == NSH - 1)
        def _():
            o_ref[...] = x_ref[...] + 0.5 * acc_ref[...]

    return pl.pallas_call(
        body, out_shape=_sds((S, D), f32), grid=(S // TMW, NSH),
        in_specs=[pl.BlockSpec((TMW, D), lambda i, s: (i, 0)), pl.BlockSpec((1, TMW, FS), lambda i, s: (s, i, 0)),
                  pl.BlockSpec((1, FS, D), lambda i, s: (s, 0, 0))],
        out_specs=pl.BlockSpec((TMW, D), lambda i, s: (i, 0)), scratch_shapes=[pltpu.VMEM((TMW, D), f32)],
        name="ffn_down", compiler_params=_cp("parallel", "arbitrary"))(x, u, wd)


def _in_proj(h, wi):
    def body(h_ref, w_ref, o_ref):
        o_ref[...] = _dot(h_ref[...], w_ref[0])

    return pl.pallas_call(
        body, out_shape=_sds((S, PROJ), f32), grid=(NSH, S // TM),
        in_specs=[pl.BlockSpec((TM, D), lambda s, i: (i, 0)), pl.BlockSpec((1, D, PS), lambda s, i: (s, 0, 0))],
        out_specs=pl.BlockSpec((TM, PS), lambda s, i: (i, s)), name="in_proj", compiler_params=_cp("parallel", "parallel"))(h, wi)


def _out_proj(x, mixed, wo):
    def body(x_ref, m_ref, w_ref, o_ref):
        o_ref[...] = x_ref[...] + _dot(m_ref[...], w_ref[...].reshape(D, D))

    return pl.pallas_call(
        body, out_shape=_sds((S, D), f32), grid=(S // TM,),
        in_specs=[pl.BlockSpec((TM, D), lambda i: (i, 0)), pl.BlockSpec((TM, D), lambda i: (i, 0)),
                  pl.BlockSpec((NSH, PW, D), lambda i: (0, 0, 0))],
        out_specs=pl.BlockSpec((TM, D), lambda i: (i, 0)), name="out_proj", compiler_params=_cp("parallel"))(x, mixed, wo)


def _out_proj_bwd(dx, wo):
    def body(dx_ref, w_ref, o_ref):
        o_ref[...] = _dot_nt(dx_ref[...].astype(bf16), w_ref[...].reshape(D, D))

    return pl.pallas_call(
        body, out_shape=_sds((S, D), f32), grid=(S // TM,),
        in_specs=[pl.BlockSpec((TM, D), lambda i: (i, 0)), pl.BlockSpec((NSH, PW, D), lambda i: (0, 0, 0))],
        out_specs=pl.BlockSpec((TM, D), lambda i: (i, 0)), name="out_proj_bwd", compiler_params=_cp("parallel"))(dx, wo)


def _ffn_bwd_mid(dx, h, a, b, wd, dep=None):
    nt = S // TM

    def body(dx_ref, h_ref, a_ref, b_ref, wd_ref, *rest):
        da_ref, db_ref, dwd_ref, dwg_ref, dwu_ref, acc_d, acc_g, acc_u = rest[-8:]
        i = pl.program_id(1)
        dy = (0.5 * dx_ref[...]).astype(bf16)
        hh = h_ref[...]
        du = _dot_nt(dy, wd_ref[0])
        a = a_ref[0].astype(f32)
        b = b_ref[0].astype(f32)
        sig = 1.0 / (1.0 + jnp.exp(-a))
        silu = a * sig
        da = (du * b * (sig * (1.0 + a * (1.0 - sig)))).astype(bf16)
        db = (du * silu).astype(bf16)
        da_ref[0] = da
        db_ref[0] = db
        pd = _dot_tn((silu * b).astype(bf16), dy)
        pg = _dot_tn(da, hh)
        pu = _dot_tn(db, hh)

        @pl.when(i == 0)
        def _():
            acc_d[...] = pd
            acc_g[...] = pg
            acc_u[...] = pu

        @pl.when(i != 0)
        def _():
            acc_d[...] += pd
            acc_g[...] += pg
            acc_u[...] += pu

        @pl.when(i == nt - 1)
        def _():
            dwd_ref[...] = acc_d[...].astype(bf16).reshape(dwd_ref.shape)
            dwg_ref[...] = acc_g[...].astype(bf16).reshape(dwg_ref.shape)
            dwu_ref[...] = acc_u[...].astype(bf16).reshape(dwu_ref.shape)

    tok = pl.BlockSpec((TM, D), lambda s, i: (i, 0))
    hid = pl.BlockSpec((1, TM, FS), lambda s, i: (s, i, 0))
    wsp = pl.BlockSpec((1, 2, FS // 2, D), lambda s, i: (s, 0, 0, 0))
    hidden = _sds((NSH, S, FS), bf16)
    wgrad = _sds((NSH, 2, FS // 2, D), bf16)
    dspec, dop = _dep(dep)
    return pl.pallas_call(
        body, out_shape=(hidden, hidden, wgrad, wgrad, wgrad), grid=(NSH, nt),
        in_specs=[tok, tok, hid, hid, pl.BlockSpec((1, FS, D), lambda s, i: (s, 0, 0))] + dspec,
        out_specs=(hid, hid, wsp, wsp, wsp), scratch_shapes=[pltpu.VMEM((FS, D), f32)] * 3, name="ffn_bwd_mid",
        compiler_params=_cp("parallel", "arbitrary"))(dx, h, a, b, wd, *dop)


def _norm_bwd_tail(acc, x_ref, dxin_ref, g_ref, dxo_ref, dg_ref, first):
    xf = x_ref[...]
    r = lax.rsqrt(jnp.mean(xf * xf, axis=-1, keepdims=True) + NORM_EPS)
    xhat = xf * r
    dhg = acc * g_ref[...]
    dxo_ref[...] = dxin_ref[...] + r * (dhg - xhat * jnp.mean(dhg * xhat, axis=-1, keepdims=True))
    part = jnp.sum(acc * xhat, axis=0, keepdims=True)

    @pl.when(first)
    def _():
        dg_ref[...] = part

    @pl.when(jnp.logical_not(first))
    def _():
        dg_ref[...] += part


def _ffn_bwd_dx(dx, x_in, g, da, db, wg, wu):
    def body(dx_ref, x_ref, g_ref, da_ref, db_ref, wg_ref, wu_ref, dxo_ref, dg_ref, acc_ref):
        i, s = pl.program_id(0), pl.program_id(1)

        @pl.when(s == 0)
        def _():
            acc_ref[...] = jnp.zeros_like(acc_ref)

        acc_ref[...] += _dot(da_ref[0], wg_ref[0]) + _dot(db_ref[0], wu_ref[0])

        @pl.when(s == NSH - 1)
        def _():
            _norm_bwd_tail(acc_ref[...], x_ref, dx_ref, g_ref, dxo_ref, dg_ref, i == 0)

    tok = pl.BlockSpec((TMW, D), lambda i, s: (i, 0))
    vec = pl.BlockSpec((1, D), lambda i, s: (0, 0))
    hid = pl.BlockSpec((1, TMW, FS), lambda i, s: (s, i, 0))
    wsp = pl.BlockSpec((1, FS, D), lambda i, s: (s, 0, 0))
    return pl.pallas_call(
        body, out_shape=(_sds((S, D), f32), _sds((1, D), f32)), grid=(S // TMW, NSH),
        in_specs=[tok, tok, vec, hid, hid, wsp, wsp], out_specs=(tok, vec), scratch_shapes=[pltpu.VMEM((TMW, D), f32)],
        name="ffn_bwd_dx", compiler_params=_cp("arbitrary", "arbitrary"))(dx, x_in, g, da, db, wg, wu)


def _in_proj_bwd_dx(dx, x_in, g, dproj, wi):
    def body(dx_ref, x_ref, g_ref, dp_ref, w_ref, dxo_ref, dg_ref, acc_ref):
        i, s = pl.program_id(0), pl.program_id(1)

        @pl.when(s == 0)
        def _():
            acc_ref[...] = jnp.zeros_like(acc_ref)

        acc_ref[...] += _dot_nt(dp_ref[...], w_ref[0])

        @pl.when(s == NSH - 1)
        def _():
            _norm_bwd_tail(acc_ref[...], x_ref, dx_ref, g_ref, dxo_ref, dg_ref, i == 0)

    tok = pl.BlockSpec((TMW, D), lambda i, s: (i, 0))
    vec = pl.BlockSpec((1, D), lambda i, s: (0, 0))
    return pl.pallas_call(
        body, out_shape=(_sds((S, D), f32), _sds((1, D), f32)), grid=(S // TMW, NSH),
        in_specs=[tok, tok, vec, pl.BlockSpec((TMW, PS), lambda i, s: (i, s)), pl.BlockSpec((1, D, PS), lambda i, s: (s, 0, 0))],
        out_specs=(tok, vec), scratch_shapes=[pltpu.VMEM((TMW, D), f32)],
        name="in_proj_bwd_dx", compiler_params=_cp("arbitrary", "arbitrary"))(dx, x_in, g, dproj, wi)


def _dw(lhs, rhs, lhs_spec, rhs_spec, rows, cols, name, cast_rhs=False):
    def body(l_ref, r_ref, o_ref):
        r = r_ref[...].astype(bf16) if cast_rhs else r_ref[...]
        o_ref[...] = _dot_tn(l_ref[...], r).astype(bf16).reshape(1, 2, rows // 2, cols)

    return pl.pallas_call(
        body, out_shape=_sds((NSH, 2, rows // 2, cols), bf16), grid=(NSH,), in_specs=[lhs_spec, rhs_spec],
        out_specs=pl.BlockSpec((1, 2, rows // 2, cols), lambda s: (s, 0, 0, 0)), name=name, compiler_params=_cp("parallel"))(lhs, rhs)


_WHOLE_TOK = pl.BlockSpec((S, D), lambda s: (0, 0))


def _dw_in(h, dproj):
    return _dw(h, dproj, _WHOLE_TOK, pl.BlockSpec((S, PS), lambda s: (0, s)), D, PS, "dw_in")


def _dw_out(mixed, dx):
    return _dw(mixed, dx, pl.BlockSpec((S, PW), lambda s: (0, s)), _WHOLE_TOK, PW, D, "dw_out", cast_rhs=True)


def _final_loss(x, g, target):
    def body(x_ref, g_ref, t_ref, loss_ref, dx_ref, dg_ref):
        i = pl.program_id(0)
        xf = x_ref[...]
        r = lax.rsqrt(jnp.mean(xf * xf, axis=-1, keepdims=True) + NORM_EPS)
        xhat = xf * r
        err = xhat * g_ref[...] - t_ref[...]
        dy = err * (1.0 / D)
        dhg = dy * g_ref[...]
        dx_ref[...] = r * (dhg - xhat * jnp.mean(dhg * xhat, axis=-1, keepdims=True))
        part = jnp.sum(dy * xhat, axis=0, keepdims=True)
        lpart = jnp.zeros((8, 128), f32) + 0.5 * jnp.sum(jnp.mean(err * err, axis=-1, keepdims=True))

        @pl.when(i == 0)
        def _():
            dg_ref[...] = part
            loss_ref[...] = lpart

        @pl.when(i != 0)
        def _():
            dg_ref[...] += part
            loss_ref[...] += lpart

    tok = pl.BlockSpec((TM, D), lambda i: (i, 0))
    vec = pl.BlockSpec((1, D), lambda i: (0, 0))
    return pl.pallas_call(
        body, out_shape=(_sds((8, 128), f32), _sds((S, D), f32), _sds((1, D), f32)), grid=(S // TM,),
        in_specs=[tok, vec, tok], out_specs=(pl.BlockSpec((8, 128), lambda i: (0, 0)), tok, vec),
        name="final_loss", compiler_params=_cp("arbitrary"))(x, g, target)


def _shift_down(x, k, row):
    return jnp.where(row >= k, pltpu.roll(x, k, axis=0), 0.0)


def _shift_up(x, k, row):
    return jnp.where(row < S - k, pltpu.roll(x, S - k, axis=0), 0.0)


def _pool_geometry():
    row = lax.broadcasted_iota(jnp.int32, (S, PW), 0)
    grp = lax.broadcasted_iota(jnp.int32, (S, PW), 1) // 64
    half = jnp.where(grp == 0, 1, jnp.where(grp == 1, 2, jnp.where(grp == 2, 4, 8)))
    hi = jnp.minimum(row + half - 1, S - 1)
    lo = jnp.maximum(row - half, 0)
    return row, grp, (hi - lo + 1).astype(f32)


def _by_group(grp, v0, v1, v2, v3):
    return jnp.where(grp == 0, v0, jnp.where(grp == 1, v1, jnp.where(grp == 2, v2, v3)))


def _window_sums(x, row, grp, transpose):
    l1, r1 = x, x
    l2, r2 = l1 + _shift_down(l1, 1, row), r1 + _shift_up(r1, 1, row)
    l4, r4 = l2 + _shift_down(l2, 2, row), r2 + _shift_up(r2, 2, row)
    l8, r8 = l4 + _shift_down(l4, 4, row), r4 + _shift_up(r4, 4, row)
    lsel = _by_group(grp, l1, l2, l4, l8)
    rsel = _by_group(grp, r1, r2, r4, r8)
    if transpose:
        return lsel + _shift_up(rsel, 1, row)
    return _shift_down(lsel, 1, row) + rsel


def _pool_fwd(proj, wbd, scale):
    def body(v_ref, w_ref, sc_ref, mixed_ref, diff_ref):
        row, grp, cnt = _pool_geometry()
        v = v_ref[...]
        diff = (_window_sums(v, row, grp, False) / cnt - v).astype(bf16)
        diff_ref[...] = diff
        mixed_ref[...] = (_dot(diff, w_ref[...].astype(bf16)) * sc_ref[...]).astype(bf16)

    col = pl.BlockSpec((S, PW), lambda i: (0, 0))
    return pl.pallas_call(
        body, out_shape=(_sds((S, D), bf16), _sds((S, PW), bf16)), grid=(1,),
        in_specs=[col, pl.BlockSpec((PW, PW), lambda i: (0, 0)), pl.BlockSpec((1, PW), lambda i: (0, 0))],
        out_specs=(col, col), name="pool_fwd", compiler_params=_cp("arbitrary"))(proj, wbd, scale)


def _pool_bwd(dmixed, diff, wbd, scale, dproj):
    def body(dy_ref, diff_ref, w_ref, sc_ref, dproj_in, dv_ref, dw_ref, dsc_ref):
        del dproj_in
        row, grp, cnt = _pool_geometry()
        dy = dy_ref[...]
        diff = diff_ref[...]
        w = w_ref[...].astype(bf16)
        dsc_ref[...] = jnp.sum(dy * _dot(diff, w), axis=0, keepdims=True)
        dys = (dy * sc_ref[...]).astype(bf16)
        dw_ref[...] = _dot_tn(diff, dys)
        ddiff = _dot_nt(dys, w)
        dv_ref[...] = (_window_sums(ddiff / cnt, row, grp, True) - ddiff).astype(bf16)

    col = pl.BlockSpec((S, PW), lambda i: (0, 0))
    return pl.pallas_call(
        body, out_shape=(_sds((S, PROJ), bf16), _sds((PW, PW), f32), _sds((1, PW), f32)), grid=(1,),
        in_specs=[col, col, pl.BlockSpec((PW, PW), lambda i: (0, 0)), pl.BlockSpec((1, PW), lambda i: (0, 0)), ANY],
        out_specs=(col, pl.BlockSpec((PW, PW), lambda i: (0, 0)), pl.BlockSpec((1, PW), lambda i: (0, 0))),
        input_output_aliases={4: 0}, name="pool_bwd", compiler_params=_cp("arbitrary"))(dmixed, diff, wbd, scale, dproj)


def _rope_tables(pos_col, freq_row):
    def body(p_ref, f_ref, c_ref, a_ref, b_ref):
        ang = p_ref[...].astype(f32) * f_ref[...]
        l64 = lax.broadcasted_iota(jnp.int32, (S, 128), 1) % 64
        cos, sin = jnp.cos(ang), jnp.sin(ang)
        c_ref[...] = jnp.where(l64 < 16, cos, 1.0)
        a_ref[...] = jnp.where(l64 < 8, -sin, 0.0)
        b_ref[...] = jnp.where((l64 >= 8) & (l64 < 16), sin, 0.0)

    t = _sds((S, 128), f32)
    return pl.pallas_call(body, out_shape=(t, t, t), name="rope_tables", compiler_params=_cp())(pos_col, freq_row)


def _rope(t, c, a, b):
    return t * c + pltpu.roll(t, 120, axis=1) * a + pltpu.roll(t, 8, axis=1) * b


def _rope_bwd(g, c, a, b):
    return g * c + pltpu.roll(g * a, 8, axis=1) + pltpu.roll(g * b, 120, axis=1)


def _perm_load(ref, d):
    if d == 1:
        return ref[...]
    n = S // d
    return jnp.concatenate([ref[pl.ds(r, n, stride=d), :] for r in range(d)], axis=0)


def _unperm_store(ref, val, d):
    if d == 1:
        ref[...] = val
        return
    n = S // d
    for r in range(d):
        ref[pl.ds(r, n, stride=d), :] = val[r * n:(r + 1) * n, :]


def _band(xp):
    z = jnp.zeros((64, 128), bf16)
    p = jnp.concatenate([z, xp, z], axis=0).reshape(NBLK + 1, QBLK, 128)
    return jnp.concatenate([p[:NBLK], p[1:]], axis=1)


def _unband(xb):
    z = jnp.zeros((1, QBLK, 128), f32)
    p = jnp.concatenate([xb[:, :QBLK], z], axis=0) + jnp.concatenate([z, xb[:, QBLK:]], axis=0)
    return p.reshape(S + QBLK, 128)[64:S + 64]


def _band_mask(d):
    blocks_per_class = NBLK // d
    n = lax.broadcasted_iota(jnp.int32, (NBLK, 1, 2 * QBLK), 0) & (blocks_per_class - 1)
    be = lax.broadcasted_iota(jnp.int32, (NBLK, 1, 2 * QBLK), 2)
    a = lax.broadcasted_iota(jnp.int32, (1, 2 * QBLK, 2 * QBLK), 1) & (QBLK - 1)
    b = lax.broadcasted_iota(jnp.int32, (1, 2 * QBLK, 2 * QBLK), 2)
    band = (b >= a) & (b <= a + 128)
    edge = ((be >= 64) | (n != 0)) & ((be < QBLK + 64) | (n != blocks_per_class - 1))
    return band & edge


def _stack_heads(xb, lo):
    z = jnp.zeros_like(xb)
    return jnp.concatenate([jnp.where(lo, xb, z), jnp.where(lo, z, xb)], axis=1)


def _unstack_heads(x2, lo):
    return jnp.where(lo, x2[:, :QBLK], x2[:, QBLK:])


def _rows_to_lanes(col2, lo):
    return jnp.where(lo, jnp.broadcast_to(col2[:, :QBLK], (NBLK, QBLK, 128)), jnp.broadcast_to(col2[:, QBLK:], (NBLK, QBLK, 128)))


def _bmm_nt(a, b):
    return jnp.einsum('nqd,nkd->nqk', a, b, preferred_element_type=f32)


def _bmm_nn(a, b):
    return jnp.einsum('nqk,nkd->nqd', a, b, preferred_element_type=f32)


def _bmm_tn(a, b):
    return jnp.einsum('nqk,nqd->nkd', a, b, preferred_element_type=f32)


def _attn_fwd(proj, tc, ta, tb, mixed):
    def body(q_ref, k_ref, v_ref, c_ref, a_ref, b_ref, mixed_in, mixed_ref, o_ref, lse_ref, qn, kn, t_num, t_m, t_den):
        del mixed_in
        lo = lax.broadcasted_iota(jnp.int32, (1, 1, 128), 2) < 64
        c, a, b = c_ref[...], a_ref[...], b_ref[...]
        qn[...] = _rope(q_ref[...], c, a, b)
        kn[...] = _rope(k_ref[...], c, a, b)
        run = None
        for d in DILATIONS:
            q2 = _stack_heads(_perm_load(qn, d).astype(bf16).reshape(NBLK, QBLK, 128), lo)
            kb = _band(_perm_load(kn, d).astype(bf16))
            vb = _band(_perm_load(v_ref, d).astype(bf16))
            s = jnp.where(_band_mask(d), _bmm_nt(q2, kb) * 0.125, MASK_VALUE)
            m = jnp.max(s, axis=2, keepdims=True)
            p = jnp.exp(s - m)
            den = jnp.sum(p, axis=2, keepdims=True)
            num = _unstack_heads(_bmm_nn(p.astype(bf16), vb), lo)
            _unperm_store(t_num, num.reshape(S, 128), d)
            _unperm_store(t_m, _rows_to_lanes(m, lo).reshape(S, 128), d)
            _unperm_store(t_den, _rows_to_lanes(den, lo).reshape(S, 128), d)
            if run is None:
                run = (t_m[...], t_num[...], t_den[...])
            else:
                m_new = jnp.maximum(run[0], t_m[...])
                w_old, w_new = jnp.exp(run[0] - m_new), jnp.exp(t_m[...] - m_new)
                run = (m_new, w_old * run[1] + w_new * t_num[...], w_old * run[2] + w_new * t_den[...])
        out = run[1] / run[2]
        o_ref[...] = out
        mixed_ref[...] = out.astype(bf16)
        lse_ref[...] = run[0] + jnp.log(run[2])

    def col(off):
        return pl.BlockSpec((S, 128), lambda j, off=off: (0, off + j))

    tab = pl.BlockSpec((S, 128), lambda j: (0, 0))
    scr = pltpu.VMEM((S, 128), f32)
    return pl.pallas_call(
        body, out_shape=(_sds((S, D), bf16), _sds((S, AW), f32), _sds((S, AW), f32)), grid=(NPAIR,),
        in_specs=[col(2), col(8), col(14), tab, tab, tab, ANY], out_specs=(col(2), col(0), col(0)),
        scratch_shapes=[scr, scr, scr, scr, scr], input_output_aliases={6: 0}, name="attn_fwd",
        compiler_params=_cp("arbitrary"))(proj, proj, proj, tc, ta, tb, mixed)


def _attn_bwd(proj, tc, ta, tb, o, lse, dmixed):
    def body(q_ref, k_ref, v_ref, c_ref, a_ref, b_ref, o_ref, lse_ref, do_ref, dp_ref, qn, kn, tmp, dk_s, dv_s):
        t = pl.program_id(1)

        @pl.when(t == 0)
        def _():
            lo = lax.broadcasted_iota(jnp.int32, (1, 1, 128), 2) < 64
            c, a, b = c_ref[...], a_ref[...], b_ref[...]
            qn[...] = _rope(q_ref[...], c, a, b)
            kn[...] = _rope(k_ref[...], c, a, b)
            dq = dk = dv = None
            for d in DILATIONS:
                q2 = _stack_heads(_perm_load(qn, d).astype(bf16).reshape(NBLK, QBLK, 128), lo)
                kb = _band(_perm_load(kn, d).astype(bf16))
                vb = _band(_perm_load(v_ref, d).astype(bf16))
                dob = _perm_load(do_ref, d).reshape(NBLK, QBLK, 128)
                ob = _perm_load(o_ref, d).reshape(NBLK, QBLK, 128)
                lsb = _perm_load(lse_ref, d).reshape(NBLK, QBLK, 128)
                do2 = _stack_heads(dob.astype(bf16), lo)
                delta2 = jnp.sum(_stack_heads(dob * ob, lo), axis=2, keepdims=True)
                lse2 = jnp.max(jnp.concatenate([jnp.where(lo, lsb, MASK_VALUE), jnp.where(lo, MASK_VALUE, lsb)], axis=1),
                               axis=2, keepdims=True)
                s = _bmm_nt(q2, kb) * 0.125
                p = jnp.where(_band_mask(d), jnp.exp(s - lse2), 0.0)
                ds = (p * (_bmm_nt(do2, vb) - delta2) * 0.125).astype(bf16)
                pb = p.astype(bf16)
                dq_b = _unstack_heads(_bmm_nn(ds, kb), lo).reshape(S, 128)
                dk_b = _unband(_bmm_tn(ds, q2))
                dv_b = _unband(_bmm_tn(pb, do2))
                acc = []
                for prev, new in ((dq, dq_b), (dk, dk_b), (dv, dv_b)):
                    _unperm_store(tmp, new, d)
                    acc.append(tmp[...] if prev is None else prev + tmp[...])
                dq, dk, dv = acc
            dp_ref[...] = _rope_bwd(dq, c, a, b).astype(bf16)
            dk_s[...] = _rope_bwd(dk, c, a, b).astype(bf16)
            dv_s[...] = dv.astype(bf16)

        @pl.when(t == 1)
        def _():
            dp_ref[...] = dk_s[...]

        @pl.when(t == 2)
        def _():
            dp_ref[...] = dv_s[...]

    def col(off):
        return pl.BlockSpec((S, 128), lambda j, t, off=off: (0, off + j))

    tab = pl.BlockSpec((S, 128), lambda j, t: (0, 0))
    scr = pltpu.VMEM((S, 128), f32)
    scb = pltpu.VMEM((S, 128), bf16)
    return pl.pallas_call(
        body, out_shape=_sds((S, PROJ), bf16), grid=(NPAIR, 3),
        in_specs=[col(2), col(8), col(14), tab, tab, tab, col(0), col(0), col(2)],
        out_specs=pl.BlockSpec((S, 128), lambda j, t: (0, 2 + NPAIR * t + j)),
        scratch_shapes=[scr, scr, scr, scb, scb], name="attn_bwd",
        compiler_params=_cp("arbitrary", "arbitrary"))(proj, proj, proj, tc, ta, tb, o, lse, dmixed)


def _block_diag(w4):
    out = jnp.zeros((PW, PW), w4.dtype)
    for g in range(4):
        out = out.at[64 * g:64 * (g + 1), 64 * g:64 * (g + 1)].set(w4[g])
    return out


def _diag_blocks(w):
    return jnp.stack([w[64 * g:64 * (g + 1), 64 * g:64 * (g + 1)] for g in range(4)])


def _rope_inputs(positions):
    inv_freq = ROPE_THETA ** (-jnp.arange(0, 16, 2, dtype=f32) / 16)
    l64 = np.arange(128) % 64
    idx = np.where(l64 < 16, l64 % 8, 0)
    return positions.reshape(S, 1), inv_freq[idx].reshape(1, 128)


def _layer_fwd(x, w, small, l, tabs, dep=None):
    g1, gm, g2 = (small[k][l].reshape(1, D) for k in ("ffn1_norm", "mix_norm", "ffn2_norm"))
    wbd = _block_diag(small["pool_w"][l])
    psc = small["pool_scale"][l].reshape(1, PW)
    h1 = _rms_fwd(x, g1, dep)
    a1, b1, u1 = _ffn_up(h1, w["g1"], w["u1"])
    x1 = _ffn_down(x, u1, w["d1"])
    h2 = _rms_fwd(x1, gm)
    proj = _in_proj(h2, w["wi"])
    mixed, diff = _pool_fwd(proj, wbd, psc)
    mixed, o, lse = _attn_fwd(proj, *tabs, mixed)
    x2 = _out_proj(x1, mixed, w["wo"])
    h3 = _rms_fwd(x2, g2)
    a2, b2, u2 = _ffn_up(h3, w["g2"], w["u2"])
    out = _ffn_down(x2, u2, w["d2"])
    return out, dict(x0=x, h1=h1, a1=a1, b1=b1, u1=u1, x1=x1, h2=h2, proj=proj, mixed=mixed, diff=diff, o=o, lse=lse,
                     x2=x2, h3=h3, a2=a2, b2=b2, u2=u2, g1=g1, gm=gm, g2=g2, wbd=wbd, psc=psc)


def _layer_bwd(dx, w, sv, tabs, dep=None):
    gr, sg = {}, {}
    da, db, gr["d2"], gr["g2"], gr["u2"] = _ffn_bwd_mid(dx, sv["h3"], sv["a2"], sv["b2"], w["d2"], dep)
    dx, sg["ffn2_norm"] = _ffn_bwd_dx(dx, sv["x2"], sv["g2"], da, db, w["g2"], w["u2"])
    gr["wo"] = _dw_out(sv["mixed"], dx)
    dmixed = _out_proj_bwd(dx, w["wo"])
    dproj = _attn_bwd(sv["proj"], *tabs, sv["o"], sv["lse"], dmixed)
    dproj, dwbd, sg["pool_scale"] = _pool_bwd(dmixed, sv["diff"], sv["wbd"], sv["psc"], dproj)
    sg["pool_w"] = _diag_blocks(dwbd)
    gr["wi"] = _dw_in(sv["h2"], dproj)
    dx, sg["mix_norm"] = _in_proj_bwd_dx(dx, sv["x1"], sv["gm"], dproj, w["wi"])
    da, db, gr["d1"], gr["g1"], gr["u1"] = _ffn_bwd_mid(dx, sv["h1"], sv["a1"], sv["b1"], w["d1"])
    dx, sg["ffn1_norm"] = _ffn_bwd_dx(dx, sv["x0"], sv["g1"], da, db, w["g1"], w["u1"])
    return dx, gr, sg


def _forward_backward(x, positions, target, gathered, small):
    tabs = _rope_tables(*_rope_inputs(positions))
    saved = []
    for l in range(DEPTH):
        x, sv = _layer_fwd(x, gathered[l], small, l, tabs)
        saved.append(sv)
    loss, dx, dgf = _final_loss(x, small["final_norm"].reshape(1, D), target)
    big = [None] * DEPTH
    sg = {k: [None] * DEPTH for k in ("ffn1_norm", "mix_norm", "pool_w", "pool_scale", "ffn2_norm")}
    for l in reversed(range(DEPTH)):
        dx, big[l], sgl = _layer_bwd(dx, gathered[l], saved[l], tabs)
        for k, v in sgl.items():
            sg[k][l] = v
    sg["final_norm"] = dgf
    return loss, dx, big, sg


def _place():
    x, y, c = lax.axis_index("x"), lax.axis_index("y"), lax.axis_index("c")
    chips = [(1 - x, y), (x, 1 - y), (1 - x, 1 - y)]
    return x, y, c, chips


def _cast_layer(params, l, place):
    def body(p_ref, *refs):
        del p_ref
        for i_ref, o_ref in zip(refs[:8], refs[8:]):
            o_ref[...] = i_ref[...].astype(bf16).reshape(o_ref.shape)

    ins, in_specs, out_shape, out_specs = [], [], [], []
    for name, rows, cols in BIG:
        q = rows // 4
        ins.append(params[BIG_SRC[name]])
        in_specs.append(pl.BlockSpec((1, q, cols), lambda i, p, l=l: (l, i, 0)))
        out_shape.append(_sds((NSH, 2, rows // 2, cols), bf16))
        out_specs.append(pl.BlockSpec((1, 1, q, cols), lambda i, p: (p[1], i // 2, i % 2, 0)))
    return pl.pallas_call(
        body, out_shape=out_shape,
        grid_spec=pltpu.PrefetchScalarGridSpec(num_scalar_prefetch=1, grid=(4,), in_specs=in_specs, out_specs=out_specs),
        name=f"cast_layer{l}", compiler_params=_cp("parallel"))(place, *ins)


HBM = pl.BlockSpec(memory_space=pltpu.HBM)
SEM = pl.BlockSpec(memory_space=pltpu.SEMAPHORE)
_SPLIT = pltpu.CompilerParams(has_side_effects=pltpu.SideEffectType.DATAFLOW_SIDE_EFFECTING)


def _hbm(arrays):
    return [pltpu.with_memory_space_constraint(a, pltpu.HBM) for a in arrays]


def _chip_copies(src_of, dst_of, send_sems, recv_sems, n):
    x, y, c, chips = _place()
    me = 2 * x + y
    out = []
    for t in range(n):
        for k, chip in enumerate(chips):
            peer = 2 * chip[0] + chip[1]
            send = pltpu.make_async_remote_copy(
                src_ref=src_of(t, peer), dst_ref=dst_of(t, me), send_sem=send_sems.at[3 * t + k], recv_sem=recv_sems.at[3 * t + k],
                device_id=(chip[0], chip[1], c), device_id_type=MESH)
            land = pltpu.make_async_remote_copy(
                src_ref=src_of(t, peer), dst_ref=dst_of(t, peer), send_sem=send_sems.at[3 * t + k], recv_sem=recv_sems.at[3 * t + k],
                device_id=(chip[0], chip[1], c), device_id_type=MESH)
            out.append((send, land))
    return out


def _exchange_start(src, land, after, src_of, dst_of, name):
    n, m = len(src), len(src) + len(land)

    def body(*refs):
        src_refs = refs[:n]
        land_refs = refs[n:m] if land else src_refs
        send_sems, recv_sems = refs[m + 1], refs[m + 2]
        token = refs[-1]
        for send, _ in _chip_copies(lambda t, s: src_of(src_refs[t], s), lambda t, s: dst_of(land_refs[t], s), send_sems, recv_sems, n):
            send.start()
        token[...] = jnp.zeros_like(token)

    arrays = list(src) + list(land)
    out_shape = ([pltpu.SemaphoreType.DMA((3 * n,)), pltpu.SemaphoreType.DMA((3 * n,))] + [pltpu.HBM(a.shape, a.dtype) for a in arrays]
                 + [_sds((8, 128), f32)])
    res = pl.pallas_call(
        body, out_shape=out_shape, in_specs=[HBM] * m + [ANY], out_specs=[SEM, SEM] + [HBM] * m + [pl.BlockSpec(memory_space=pltpu.VMEM)],
        input_output_aliases={i: 2 + i for i in range(m)}, name=name, compiler_params=_SPLIT)(*_hbm(arrays), after)
    return res[0], res[1], list(res[2:2 + n]), list(res[2 + n:2 + m]), res[-1]


def _exchange_wait(send_sems, recv_sems, src, land, after, src_of, dst_of, name):
    n, m = len(src), len(src) + len(land)

    def body(*refs):
        src_refs = refs[:n]
        land_refs = refs[n:m] if land else src_refs
        send_sems, recv_sems = refs[m], refs[m + 1]
        for send, land_cp in _chip_copies(lambda t, s: src_of(src_refs[t], s), lambda t, s: dst_of(land_refs[t], s), send_sems, recv_sems, n):
            send.wait_send()
            land_cp.wait_recv()

    arrays = list(src) + list(land)
    res = pl.pallas_call(
        body, out_shape=[pltpu.HBM(a.shape, a.dtype) for a in arrays], in_specs=[HBM] * m + [SEM, SEM, ANY], out_specs=[HBM] * m,
        input_output_aliases={i: i for i in range(m)}, name=name, compiler_params=_SPLIT)(*arrays, send_sems, recv_sems, after)
    return list(res[:n]), list(res[n:])


def _own_half(ref, s):
    x, y, c, _ = _place()
    return ref.at[2 * x + y, c]


def _slot_half(ref, s):
    return ref.at[s, lax.axis_index("c")]


def _slot(ref, s):
    return ref.at[s]


def _gather_forward(bufs):
    n = len(bufs)

    def body(*refs):
        outs = refs[n:2 * n]
        send_sems, recv_sems = refs[2 * n:]
        x, y, c, chips = _place()
        sibling = (x, y, 1 - c)
        passed = []
        for t in range(n):
            for k, chip in enumerate(chips):
                blk = outs[t].at[2 * chip[0] + chip[1], c]
                cp = pltpu.make_async_remote_copy(
                    src_ref=blk, dst_ref=blk, send_sem=send_sems.at[t, k], recv_sem=recv_sems.at[t, k],
                    device_id=sibling, device_id_type=MESH)
                cp.start()
                passed.append(cp)
        for t in range(n):
            for k, chip in enumerate(chips):
                blk = outs[t].at[2 * chip[0] + chip[1], 1 - c]
                pltpu.make_async_remote_copy(
                    src_ref=blk, dst_ref=blk, send_sem=send_sems.at[t, k], recv_sem=recv_sems.at[t, k],
                    device_id=sibling, device_id_type=MESH).wait_recv()
        for cp in passed:
            cp.wait_send()

    out_shape = [_sds(a.shape, bf16) for a in bufs]
    return pl.pallas_call(
        body, out_shape=out_shape, in_specs=[ANY] * n, out_specs=[ANY] * n, input_output_aliases={t: t for t in range(n)},
        scratch_shapes=[pltpu.SemaphoreType.DMA((n, 3)), pltpu.SemaphoreType.DMA((n, 3))], name="gather_forward")(*bufs)


def _sibling_swap(grads):
    n = len(grads)

    def body(*refs):
        ins, outs = refs[:n], refs[n:2 * n]
        send_sems, recv_sems = refs[2 * n:]
        x, y, c, _ = _place()
        cps = []
        for t in range(n):
            for s in range(NSH):
                cp = pltpu.make_async_remote_copy(
                    src_ref=ins[t].at[s, 1 - c], dst_ref=outs[t].at[s], send_sem=send_sems.at[t, s], recv_sem=recv_sems.at[t, s],
                    device_id=(x, y, 1 - c), device_id_type=MESH)
                cp.start()
                cps.append(cp)
        for cp in cps:
            cp.wait()

    out_shape = [_sds((NSH,) + a.shape[2:], bf16) for a in grads]
    return pl.pallas_call(
        body, out_shape=out_shape, in_specs=[ANY] * n, out_specs=[ANY] * n,
        scratch_shapes=[pltpu.SemaphoreType.DMA((n, NSH)), pltpu.SemaphoreType.DMA((n, NSH))],
        name="sibling_swap")(*grads)


def _row_tile(h):
    return h // 2 if h % 32 == 0 else h


def _pair_sum(grads, got, c_idx):
    n = len(grads)

    def body(c_ref, *refs):
        del c_ref
        for t in range(n):
            refs[2 * n + t][...] = (refs[t][...].astype(f32).reshape(refs[n + t].shape) + refs[n + t][...].astype(f32)).astype(bf16)

    in_specs, out_shape, out_specs = [], [], []
    for a in grads:
        h, cols = a.shape[2:]
        in_specs.append(pl.BlockSpec((1, 1, _row_tile(h), cols), lambda s, i, c: (s, c[0], i, 0)))
    for a in grads:
        h, cols = a.shape[2:]
        in_specs.append(pl.BlockSpec((1, _row_tile(h), cols), lambda s, i, c: (s, i, 0)))
        out_shape.append(_sds((NSH, h, cols), bf16))
        out_specs.append(pl.BlockSpec((1, _row_tile(h), cols), lambda s, i, c: (s, i, 0)))
    return pl.pallas_call(
        body, out_shape=out_shape,
        grid_spec=pltpu.PrefetchScalarGridSpec(num_scalar_prefetch=1, grid=(NSH, 2), in_specs=in_specs, out_specs=out_specs),
        name="pair_sum", compiler_params=_cp("parallel", "parallel"))(c_idx, *grads, *got)


def _chip_sum(psum, parts, full, place, l, name):
    n = len(parts)

    def body(p_ref, *refs):
        s = pl.program_id(1)
        for t in range(n):
            val = jnp.where(s == p_ref[1], refs[t][0], refs[n + t][0]).astype(f32)
            out = refs[3 * n + t]

            @pl.when(s == 0)
            def _(out=out, val=val):
                out[0, 0] = val

            @pl.when(s != 0)
            def _(out=out, val=val):
                out[0, 0] += val

    own_specs, part_specs, out_shape, out_specs = [], [], [], []
    for a, fl in zip(parts, full):
        _, h, cols = a.shape
        r = _row_tile(h)
        own_specs.append(pl.BlockSpec((1, r, cols), lambda i, s, p: (p[1], i, 0)))
        part_specs.append(pl.BlockSpec((1, r, cols), lambda i, s, p: (jnp.where(s == p[1], (s + 1) % NSH, s), i, 0)))
        out_shape.append(_sds(fl.shape, f32))
        out_specs.append(pl.BlockSpec((1, 1, r, cols), lambda i, s, p, l=l: (l, p[0], i, 0)))
    return pl.pallas_call(
        body, out_shape=out_shape,
        grid_spec=pltpu.PrefetchScalarGridSpec(num_scalar_prefetch=1, grid=(2, NSH), in_specs=own_specs + part_specs + [ANY] * n,
                                               out_specs=out_specs),
        input_output_aliases={1 + 2 * n + t: t for t in range(n)}, name=name,
        compiler_params=_cp("parallel", "arbitrary"))(place, *psum, *parts, *full)


def _sibling_share(full, l, name):
    n = len(full)

    def body(*refs):
        outs = refs[n:2 * n]
        send_sems, recv_sems = refs[2 * n:]
        x, y, c, _ = _place()
        sibling = (x, y, 1 - c)
        cps = []
        for t in range(n):
            blk = outs[t].at[l, c]
            cp = pltpu.make_async_remote_copy(
                src_ref=blk, dst_ref=blk, send_sem=send_sems.at[t], recv_sem=recv_sems.at[t], device_id=sibling, device_id_type=MESH)
            cp.start()
            cps.append(cp)
        for t in range(n):
            blk = outs[t].at[l, 1 - c]
            pltpu.make_async_remote_copy(
                src_ref=blk, dst_ref=blk, send_sem=send_sems.at[t], recv_sem=recv_sems.at[t],
                device_id=sibling, device_id_type=MESH).wait_recv()
        for cp in cps:
            cp.wait_send()

    out_shape = [_sds(a.shape, f32) for a in full]
    return pl.pallas_call(
        body, out_shape=out_shape, in_specs=[ANY] * n, out_specs=[ANY] * n, input_output_aliases={t: t for t in range(n)},
        scratch_shapes=[pltpu.SemaphoreType.DMA((n,)), pltpu.SemaphoreType.DMA((n,))], name=name)(*full)


SMALL_ROWS = 656


def _pack_small(per_layer, final_vec, loss_tile):
    rows = []
    for l in range(DEPTH):
        for k in ("ffn1_norm", "mix_norm", "ffn2_norm"):
            rows.append(per_layer[k][l].reshape(8, 128))
        rows.append(per_layer["pool_w"][l].reshape(128, 128))
        rows.append(jnp.pad(per_layer["pool_scale"][l].reshape(2, 128), ((0, 6), (0, 0))))
    rows.append(final_vec.reshape(8, 128))
    rows.append(loss_tile)
    return jnp.concatenate(rows, axis=0)


def _unpack_small(buf):
    out = {k: [] for k in ("ffn1_norm", "mix_norm", "ffn2_norm", "pool_w", "pool_scale")}
    r = 0
    for l in range(DEPTH):
        for k in ("ffn1_norm", "mix_norm", "ffn2_norm"):
            out[k].append(buf[r:r + 8].reshape(D))
            r += 8
        out["pool_w"].append(buf[r:r + 128].reshape(4, 64, 64))
        r += 128
        out["pool_scale"].append(buf[r:r + 2].reshape(PW))
        r += 8
    res = {k: jnp.stack(v) for k, v in out.items()}
    res["final_norm"] = buf[r:r + 8].reshape(D)
    res["loss"] = buf[r + 8, 0]
    return res


def _allreduce_small(buf):
    def body(in_ref, out_ref, slots, send_sems, recv_sems):
        x, y, c, _ = _place()
        me = 4 * x + 2 * y + c
        slots[me] = in_ref[...]
        peers = []
        for k in range(1, 8):
            px, py, pc = x ^ (k >> 2), y ^ ((k >> 1) & 1), c ^ (k & 1)
            cp = pltpu.make_async_remote_copy(
                src_ref=in_ref, dst_ref=slots.at[me], send_sem=send_sems.at[k - 1], recv_sem=recv_sems.at[k - 1],
                device_id=(px, py, pc), device_id_type=MESH)
            cp.start()
            peers.append(cp)
        for k in range(1, 8):
            px, py, pc = x ^ (k >> 2), y ^ ((k >> 1) & 1), c ^ (k & 1)
            slot = 4 * px + 2 * py + pc
            pltpu.make_async_remote_copy(
                src_ref=slots.at[slot], dst_ref=slots.at[slot], send_sem=send_sems.at[k - 1], recv_sem=recv_sems.at[k - 1],
                device_id=(px, py, pc), device_id_type=MESH).wait_recv()
        for cp in peers:
            cp.wait_send()
        acc = slots[0]
        for j in range(1, 8):
            acc = acc + slots[j]
        out_ref[...] = acc

    return pl.pallas_call(
        body, out_shape=_sds((SMALL_ROWS, 128), f32),
        in_specs=[pl.BlockSpec(memory_space=pltpu.VMEM)], out_specs=pl.BlockSpec(memory_space=pltpu.VMEM),
        scratch_shapes=[pltpu.VMEM((8, SMALL_ROWS, 128), f32), pltpu.SemaphoreType.DMA((7,)), pltpu.SemaphoreType.DMA((7,))],
        name="allreduce_small", compiler_params=_cp())(buf)


def _adamw_math(w, g, m, v):
    m = ADAM_B1 * m + (1.0 - ADAM_B1) * g
    v = ADAM_B2 * v + (1.0 - ADAM_B2) * (g * g)
    m_hat = m / (1.0 - ADAM_B1 ** ADAM_STEP)
    v_hat = v / (1.0 - ADAM_B2 ** ADAM_STEP)
    return -ADAM_LR * (m_hat / (jnp.sqrt(v_hat) + ADAM_EPS) + ADAM_WD * w), m, v


def _adamw(w, g, m, v, name, first=0, prev=None, dep=None):
    def body(w_ref, g_ref, m_ref, v_ref, *rest):
        go_ref, d_ref, mo_ref, vo_ref = rest[-4:]
        g = g_ref[...]
        d, mn, vn = _adamw_math(w_ref[...], g, m_ref[...], v_ref[...])
        go_ref[...] = g
        d_ref[...] = d
        mo_ref[...] = mn
        vo_ref[...] = vn

    _, rows, cols = w.shape
    r = rows // 4 if rows % 32 == 0 else rows
    spec = pl.BlockSpec((1, r, cols), lambda i, j: (first + i, j, 0))
    gspec = pl.BlockSpec((1, r, cols), lambda i, j: (i, j, 0))
    out = _sds(w.shape, f32)
    extra = [] if prev is None else list(prev)
    dspec, dop = _dep(dep)
    return pl.pallas_call(
        body, out_shape=(out, out, out, out), grid=(g.shape[0], rows // r), in_specs=[spec, gspec, spec, spec] + [ANY] * len(extra) + dspec,
        out_specs=(spec,) * 4, input_output_aliases={4 + i: i for i in range(len(extra))}, name=name,
        compiler_params=_cp("parallel", "parallel"))(w, g, m, v, *extra, *dop)


SMALL_NAMES = ("ffn1_norm", "mix_norm", "pool_w", "pool_scale", "ffn2_norm", "final_norm")
WEIGHT_ORDER = ("ffn1_norm", "ffn1_w_gate", "ffn1_w_up", "ffn1_w_down", "mix_norm", "w_in", "pool_w", "pool_scale", "w_out",
                "ffn2_norm", "ffn2_w_gate", "ffn2_w_up", "ffn2_w_down", "final_norm")


def _pack_small_params(p):
    per_layer = {k: [p[k][l] for l in range(DEPTH)] for k in ("ffn1_norm", "mix_norm", "ffn2_norm", "pool_w", "pool_scale")}
    return _pack_small(per_layer, p["final_norm"], jnp.zeros((8, 128), f32))


def kernel(x, positions, ffn1_norm, ffn1_w_gate, ffn1_w_up, ffn1_w_down, mix_norm, w_in, pool_w, pool_scale, w_out, ffn2_norm, ffn2_w_gate, ffn2_w_up, ffn2_w_down, final_norm, loss_target, m_ffn1_norm, m_ffn1_w_gate, m_ffn1_w_up, m_ffn1_w_down, m_mix_norm, m_w_in, m_pool_w, m_pool_scale, m_w_out, m_ffn2_norm, m_ffn2_w_gate, m_ffn2_w_up, m_ffn2_w_down, m_final_norm, v_ffn1_norm, v_ffn1_w_gate, v_ffn1_w_up, v_ffn1_w_down, v_mix_norm, v_w_in, v_pool_w, v_pool_scale, v_w_out, v_ffn2_norm, v_ffn2_w_gate, v_ffn2_w_up, v_ffn2_w_down, v_final_norm):
    params = dict(ffn1_norm=ffn1_norm, ffn1_w_gate=ffn1_w_gate, ffn1_w_up=ffn1_w_up, ffn1_w_down=ffn1_w_down,
                  mix_norm=mix_norm, w_in=w_in, pool_w=pool_w, pool_scale=pool_scale, w_out=w_out, ffn2_norm=ffn2_norm,
                  ffn2_w_gate=ffn2_w_gate, ffn2_w_up=ffn2_w_up, ffn2_w_down=ffn2_w_down, final_norm=final_norm)
    mom_m = dict(ffn1_norm=m_ffn1_norm, ffn1_w_gate=m_ffn1_w_gate, ffn1_w_up=m_ffn1_w_up, ffn1_w_down=m_ffn1_w_down,
                 mix_norm=m_mix_norm, w_in=m_w_in, pool_w=m_pool_w, pool_scale=m_pool_scale, w_out=m_w_out,
                 ffn2_norm=m_ffn2_norm, ffn2_w_gate=m_ffn2_w_gate, ffn2_w_up=m_ffn2_w_up, ffn2_w_down=m_ffn2_w_down,
                 final_norm=m_final_norm)
    mom_v = dict(ffn1_norm=v_ffn1_norm, ffn1_w_gate=v_ffn1_w_gate, ffn1_w_up=v_ffn1_w_up, ffn1_w_down=v_ffn1_w_down,
                 mix_norm=v_mix_norm, w_in=v_w_in, pool_w=v_pool_w, pool_scale=v_pool_scale, w_out=v_w_out,
                 ffn2_norm=v_ffn2_norm, ffn2_w_gate=v_ffn2_w_gate, ffn2_w_up=v_ffn2_w_up, ffn2_w_down=v_ffn2_w_down,
                 final_norm=v_final_norm)
    names = [t[0] for t in BIG]
    for d in (params, mom_m, mom_v):
        for k in TRANSPOSED:
            d[k] = jnp.swapaxes(d[k], 1, 2)

    place = jnp.stack([lax.axis_index("c"), 2 * lax.axis_index("x") + lax.axis_index("y")]).astype(jnp.int32)
    def gather_start(l, after):
        return _exchange_start(_cast_layer(params, l, place), [], after, _own_half, _slot_half, f"gather_start{l}")

    def gather_end(started, after, l):
        send_sems, recv_sems, bufs, _, _ = started
        bufs, _ = _exchange_wait(send_sems, recv_sems, bufs, [], after, _own_half, _slot_half, f"gather_wait{l}")
        return {nm: a.reshape(NSH, rows, cols) for (nm, rows, cols), a in zip(BIG, _gather_forward(bufs))}

    tabs = _rope_tables(*_rope_inputs(positions))
    h = x.reshape(S, D)
    weights, saved = [], []
    started = gather_start(0, place)
    after = started[-1]
    for l in range(DEPTH):
        weights.append(gather_end(started, after, l))
        dep = None
        if l + 1 < DEPTH:
            started = gather_start(l + 1, weights[l]["g1"])
            dep = started[-1]
        h, sv = _layer_fwd(h, weights[l], params, l, tabs, dep)
        saved.append(sv)
        after = h
    loss, dx, dgf = _final_loss(h, final_norm.reshape(1, D), loss_target.reshape(S, D))

    upper = [lax.empty((DEPTH - 1, 2, rows // 2, cols), f32) for _, rows, cols in BIG]
    lower = [lax.empty((1, 2, rows // 2, cols), f32) for _, rows, cols in BIG]
    sg = {k: [None] * DEPTH for k in ("ffn1_norm", "mix_norm", "pool_w", "pool_scale", "ffn2_norm")}
    sg["final_norm"] = dgf

    def reduce_end(started, after, l, full, slot):
        send_sems, recv_sems, psum, parts, _ = started
        psum, parts = _exchange_wait(send_sems, recv_sems, psum, parts, after, _slot, _slot, f"grad_wait{l}")
        return _sibling_share(_chip_sum(psum, parts, full, place, slot, f"chip_sum{l}"), slot, f"sibling_share{l}")

    started, dep = None, None
    for l in reversed(range(DEPTH)):
        dx, gr, sgl = _layer_bwd(dx, weights[l], saved[l], tabs, dep)
        for k, v in sgl.items():
            sg[k][l] = v
        if started is not None:
            upper = reduce_end(started, dx, l + 1, upper, l)
        grads = [gr[nm] for nm in names]
        psum = _pair_sum(grads, _sibling_swap(grads), place)
        parts = [lax.empty(a.shape, bf16) for a in psum]
        started = _exchange_start(psum, parts, place, _slot, _slot, f"grad_start{l}")
        dep = started[-1]

    big_out = {}
    for (nm, rows, cols), g in zip(BIG, upper):
        k = BIG_SRC[nm]
        big_out[k] = _adamw(params[k], g.reshape(DEPTH - 1, rows, cols), mom_m[k], mom_v[k], "adamw_upper_" + k, first=1, dep=dep)
        dep = big_out[k][1]
    lower = reduce_end(started, dep, 0, lower, 0)
    for (nm, rows, cols), g in zip(BIG, lower):
        k = BIG_SRC[nm]
        big_out[k] = _adamw(params[k], g.reshape(1, rows, cols), mom_m[k], mom_v[k], "adamw_lower_" + k, first=0, prev=big_out[k])

    per_layer = {k: sg[k] for k in ("ffn1_norm", "mix_norm", "ffn2_norm", "pool_w", "pool_scale")}
    small_sum = _allreduce_small(_pack_small(per_layer, sg["final_norm"], loss))
    gs, ds_, ms, vs = _adamw(_pack_small_params(params).reshape(1, SMALL_ROWS, 128), small_sum.reshape(1, SMALL_ROWS, 128),
                             _pack_small_params(mom_m).reshape(1, SMALL_ROWS, 128),
                             _pack_small_params(mom_v).reshape(1, SMALL_ROWS, 128), "adamw_small")
    small_out = [_unpack_small(a.reshape(SMALL_ROWS, 128)) for a in (gs, ds_, ms, vs)]

    grad, delta, new_m, new_v = {}, {}, {}, {}
    for k in WEIGHT_ORDER:
        if k in SMALL_NAMES:
            grad[k], delta[k], new_m[k], new_v[k] = (so[k] for so in small_out)
        else:
            grad[k], delta[k], new_m[k], new_v[k] = big_out[k]
    for d in (grad, delta, new_m, new_v):
        for k in TRANSPOSED:
            d[k] = jnp.swapaxes(d[k], 1, 2)
    return (small_out[0]["loss"], dx.reshape(1, S, D), *[grad[k] for k in WEIGHT_ORDER], *[delta[k] for k in WEIGHT_ORDER],
            *[new_m[k] for k in WEIGHT_ORDER], *[new_v[k] for k in WEIGHT_ORDER])
```

```python
import functools

import jax
import jax.numpy as jnp
import numpy as np
from jax import lax
from jax.experimental import pallas as pl
from jax.experimental.pallas import tpu as pltpu

f32 = jnp.float32
bf16 = jnp.bfloat16

S = 2048
D = 1024
DEPTH = 4
NSH = 4
FS = 704
PROJ = 2560
PS = 640
PW = 256
AW = 768
NPAIR = 6
NORM_EPS = 1e-6
MASK_VALUE = -1e30
ROPE_THETA = 500000.0
DILATIONS = (1, 4, 16)
QBLK = 128
NBLK = S // QBLK
TM = 512
EW_ROWS = 16
VMEM_LIMIT = 56 * 1024 * 1024

ADAM_LR = 0.001
ADAM_B1 = 0.9
ADAM_B2 = 0.999
ADAM_EPS = 1e-08
ADAM_WD = 0.01
ADAM_STEP = 10

MESH = pl.DeviceIdType.MESH
ANY = pl.BlockSpec(memory_space=pl.ANY)

BIG = (("g1", FS, D), ("u1", FS, D), ("d1", FS, D), ("wi", D, PS), ("wo", PW, D), ("g2", FS, D), ("u2", FS, D), ("d2", FS, D))
TRANSPOSED = ("ffn1_w_gate", "ffn1_w_up", "ffn2_w_gate", "ffn2_w_up")
BIG_SRC = {"g1": "ffn1_w_gate", "u1": "ffn1_w_up", "d1": "ffn1_w_down", "wi": "w_in", "wo": "w_out",
           "g2": "ffn2_w_gate", "u2": "ffn2_w_up", "d2": "ffn2_w_down"}


def _cp(*sem):
    return pltpu.CompilerParams(dimension_semantics=sem if sem else None, vmem_limit_bytes=VMEM_LIMIT)


def _sds(shape, dt):
    return jax.ShapeDtypeStruct(shape, dt)


def _dot(a, b):
    return jnp.dot(a, b, preferred_element_type=f32)


def _dot_nt(a, b):
    return lax.dot_general(a, b, (((1,), (1,)), ((), ())), preferred_element_type=f32)


def _dot_tn(a, b):
    return lax.dot_general(a, b, (((0,), (0,)), ((), ())), preferred_element_type=f32)


def _dep(dep):
    return ([], []) if dep is None else ([ANY], [dep])


def _rms_fwd(x, g, dep=None):
    def body(x_ref, g_ref, *rest):
        h_ref = rest[-1]
        xf = x_ref[...]
        r = lax.rsqrt(jnp.mean(xf * xf, axis=-1, keepdims=True) + NORM_EPS)
        h_ref[...] = ((xf * r) * g_ref[...]).astype(bf16)

    dspec, dop = _dep(dep)
    return pl.pallas_call(
        body, out_shape=_sds((S, D), bf16), grid=(S // TM,),
        in_specs=[pl.BlockSpec((TM, D), lambda i: (i, 0)), pl.BlockSpec((1, D), lambda i: (0, 0))] + dspec,
        out_specs=pl.BlockSpec((TM, D), lambda i: (i, 0)), name="rms_fwd", compiler_params=_cp("parallel"))(x, g, *dop)


def _ffn_up(h, wg, wu):
    def body(h_ref, wg_ref, wu_ref, a_ref, b_ref, u_ref):
        hh = h_ref[...]
        for s in range(NSH):
            a = _dot_nt(hh, wg_ref[s])
            b = _dot_nt(hh, wu_ref[s])
            a_ref[s] = a.astype(bf16)
            b_ref[s] = b.astype(bf16)
            u_ref[s] = (a * (1.0 / (1.0 + jnp.exp(-a))) * b).astype(bf16)

    wspec = pl.BlockSpec((NSH, FS, D), lambda i: (0, 0, 0))
    ospec = pl.BlockSpec((NSH, TM, FS), lambda i: (0, i, 0))
    return pl.pallas_call(
        body, out_shape=(_sds((NSH, S, FS), bf16), _sds((NSH, S, FS), bf16), _sds((NSH, S, FS), bf16)),
        grid=(S // TM,), in_specs=[pl.BlockSpec((TM, D), lambda i: (i, 0)), wspec, wspec],
        out_specs=(ospec, ospec, ospec), name="ffn_up", compiler_params=_cp("parallel"))(h, wg, wu)


def _ffn_down(x, u, wd):
    def body(x_ref, u_ref, wd_ref, o_ref):
        acc = _dot(u_ref[0], wd_ref[0])
        for s in range(1, NSH):
            acc = acc + _dot(u_ref[s], wd_ref[s])
        o_ref[...] = x_ref[...] + 0.5 * acc

    return pl.pallas_call(
        body, out_shape=_sds((S, D), f32), grid=(S // TM,),
        in_specs=[pl.BlockSpec((TM, D), lambda i: (i, 0)), pl.BlockSpec((NSH, TM, FS), lambda i: (0, i, 0)),
                  pl.BlockSpec((NSH, FS, D), lambda i: (0, 0, 0))],
        out_specs=pl.BlockSpec((TM, D), lambda i: (i, 0)), name="ffn_down", compiler_params=_cp("parallel"))(x, u, wd)


def _in_proj(h, wi):
    def body(h_ref, w_ref, o_ref):
        hh = h_ref[...]
        for s in range(NSH):
            o_ref[:, PS * s:PS * (s + 1)] = _dot(hh, w_ref[s])

    return pl.pallas_call(
        body, out_shape=_sds((S, PROJ), f32), grid=(S // TM,),
        in_specs=[pl.BlockSpec((TM, D), lambda i: (i, 0)), pl.BlockSpec((NSH, D, PS), lambda i: (0, 0, 0))],
        out_specs=pl.BlockSpec((TM, PROJ), lambda i: (i, 0)), name="in_proj", compiler_params=_cp("parallel"))(h, wi)


def _out_proj(x, mixed, wo):
    def body(x_ref, m_ref, w_ref, o_ref):
        o_ref[...] = x_ref[...] + _dot(m_ref[...], w_ref[...].reshape(D, D))

    return pl.pallas_call(
        body, out_shape=_sds((S, D), f32), grid=(S // TM,),
        in_specs=[pl.BlockSpec((TM, D), lambda i: (i, 0)), pl.BlockSpec((TM, D), lambda i: (i, 0)),
                  pl.BlockSpec((NSH, PW, D), lambda i: (0, 0, 0))],
        out_specs=pl.BlockSpec((TM, D), lambda i: (i, 0)), name="out_proj", compiler_params=_cp("parallel"))(x, mixed, wo)


def _out_proj_bwd(dx, wo):
    def body(dx_ref, w_ref, o_ref):
        o_ref[...] = _dot_nt(dx_ref[...].astype(bf16), w_ref[...].reshape(D, D))

    return pl.pallas_call(
        body, out_shape=_sds((S, D), f32), grid=(S // TM,),
        in_specs=[pl.BlockSpec((TM, D), lambda i: (i, 0)), pl.BlockSpec((NSH, PW, D), lambda i: (0, 0, 0))],
        out_specs=pl.BlockSpec((TM, D), lambda i: (i, 0)), name="out_proj_bwd", compiler_params=_cp("parallel"))(dx, wo)


def _ffn_bwd_mid(dx, h, a, b, wd, dep=None):
    nt = S // TM

    def body(dx_ref, h_ref, a_ref, b_ref, wd_ref, *rest):
        da_ref, db_ref, dwd_ref, dwg_ref, dwu_ref, acc_d, acc_g, acc_u, du_s, u_s = rest[-10:]
        i = pl.program_id(1)
        dy = (0.5 * dx_ref[...]).astype(bf16)
        hh = h_ref[...]
        du_s[...] = _dot_nt(dy, wd_ref[0])

        for c in range(TM // EW_ROWS):
            r = slice(c * EW_ROWS, (c + 1) * EW_ROWS)
            du = du_s[r, :]
            a = a_ref[0, r, :].astype(f32)
            b = b_ref[0, r, :].astype(f32)
            sig = 1.0 / (1.0 + jnp.exp(-a))
            silu = a * sig
            da_ref[0, r, :] = (du * b * (sig * (1.0 + a * (1.0 - sig)))).astype(bf16)
            db_ref[0, r, :] = (du * silu).astype(bf16)
            u_s[r, :] = (silu * b).astype(bf16)
        pd = _dot_tn(u_s[...], dy)
        pg = _dot_tn(da_ref[0], hh)
        pu = _dot_tn(db_ref[0], hh)

        @pl.when(i == 0)
        def _():
            acc_d[...] = pd
            acc_g[...] = pg
            acc_u[...] = pu

        @pl.when(i != 0)
        def _():
            acc_d[...] += pd
            acc_g[...] += pg
            acc_u[...] += pu

        @pl.when(i == nt - 1)
        def _():
            dwd_ref[...] = acc_d[...].astype(bf16).reshape(dwd_ref.shape)
            dwg_ref[...] = acc_g[...].astype(bf16).reshape(dwg_ref.shape)
            dwu_ref[...] = acc_u[...].astype(bf16).reshape(dwu_ref.shape)

    tok = pl.BlockSpec((TM, D), lambda s, i: (i, 0))
    hid = pl.BlockSpec((1, TM, FS), lambda s, i: (s, i, 0))
    wsp = pl.BlockSpec((1, 2, FS // 2, D), lambda s, i: (s, 0, 0, 0))
    hidden = _sds((NSH, S, FS), bf16)
    wgrad = _sds((NSH, 2, FS // 2, D), bf16)
    dspec, dop = _dep(dep)
    return pl.pallas_call(
        body, out_shape=(hidden, hidden, wgrad, wgrad, wgrad), grid=(NSH, nt),
        in_specs=[tok, tok, hid, hid, pl.BlockSpec((1, FS, D), lambda s, i: (s, 0, 0))] + dspec,
        out_specs=(hid, hid, wsp, wsp, wsp),
        scratch_shapes=[pltpu.VMEM((FS, D), f32)] * 3 + [pltpu.VMEM((TM, FS), f32), pltpu.VMEM((TM, FS), bf16)], name="ffn_bwd_mid",
        compiler_params=_cp("parallel", "arbitrary"))(dx, h, a, b, wd, *dop)


def _norm_bwd_tail(acc, x_ref, dxin_ref, g_ref, dxo_ref, dg_ref, first):
    xf = x_ref[...]
    r = lax.rsqrt(jnp.mean(xf * xf, axis=-1, keepdims=True) + NORM_EPS)
    xhat = xf * r
    dhg = acc * g_ref[...]
    dxo_ref[...] = dxin_ref[...] + r * (dhg - xhat * jnp.mean(dhg * xhat, axis=-1, keepdims=True))
    part = jnp.sum(acc * xhat, axis=0, keepdims=True)

    @pl.when(first)
    def _():
        dg_ref[...] = part

    @pl.when(jnp.logical_not(first))
    def _():
        dg_ref[...] += part


def _ffn_bwd_dx(dx, x_in, g, da, db, wg, wu):
    def body(dx_ref, x_ref, g_ref, da_ref, db_ref, wg_ref, wu_ref, dxo_ref, dg_ref):
        acc = _dot(da_ref[0], wg_ref[0])
        acc = acc + _dot(db_ref[0], wu_ref[0])
        for s in range(1, NSH):
            acc = acc + _dot(da_ref[s], wg_ref[s])
            acc = acc + _dot(db_ref[s], wu_ref[s])
        _norm_bwd_tail(acc, x_ref, dx_ref, g_ref, dxo_ref, dg_ref, pl.program_id(0) == 0)

    tok = pl.BlockSpec((TM, D), lambda i: (i, 0))
    vec = pl.BlockSpec((1, D), lambda i: (0, 0))
    hid = pl.BlockSpec((NSH, TM, FS), lambda i: (0, i, 0))
    wsp = pl.BlockSpec((NSH, FS, D), lambda i: (0, 0, 0))
    return pl.pallas_call(
        body, out_shape=(_sds((S, D), f32), _sds((1, D), f32)), grid=(S // TM,),
        in_specs=[tok, tok, vec, hid, hid, wsp, wsp], out_specs=(tok, vec),
        name="ffn_bwd_dx", compiler_params=_cp("arbitrary"))(dx, x_in, g, da, db, wg, wu)


def _in_proj_bwd_dx(dx, x_in, g, dproj, wi):
    def body(dx_ref, x_ref, g_ref, dp_ref, w_ref, dxo_ref, dg_ref):
        acc = _dot_nt(dp_ref[:, 0:PS], w_ref[0])
        for s in range(1, NSH):
            acc = acc + _dot_nt(dp_ref[:, PS * s:PS * (s + 1)], w_ref[s])
        _norm_bwd_tail(acc, x_ref, dx_ref, g_ref, dxo_ref, dg_ref, pl.program_id(0) == 0)

    tok = pl.BlockSpec((TM, D), lambda i: (i, 0))
    vec = pl.BlockSpec((1, D), lambda i: (0, 0))
    return pl.pallas_call(
        body, out_shape=(_sds((S, D), f32), _sds((1, D), f32)), grid=(S // TM,),
        in_specs=[tok, tok, vec, pl.BlockSpec((TM, PROJ), lambda i: (i, 0)), pl.BlockSpec((NSH, D, PS), lambda i: (0, 0, 0))],
        out_specs=(tok, vec), name="in_proj_bwd_dx", compiler_params=_cp("arbitrary"))(dx, x_in, g, dproj, wi)


def _dw(lhs, rhs, lhs_spec, rhs_spec, rows, cols, name, cast_rhs=False):
    def body(l_ref, r_ref, o_ref):
        r = r_ref[...].astype(bf16) if cast_rhs else r_ref[...]
        o_ref[...] = _dot_tn(l_ref[...], r).astype(bf16).reshape(1, 2, rows // 2, cols)

    return pl.pallas_call(
        body, out_shape=_sds((NSH, 2, rows // 2, cols), bf16), grid=(NSH,), in_specs=[lhs_spec, rhs_spec],
        out_specs=pl.BlockSpec((1, 2, rows // 2, cols), lambda s: (s, 0, 0, 0)), name=name, compiler_params=_cp("parallel"))(lhs, rhs)


_WHOLE_TOK = pl.BlockSpec((S, D), lambda s: (0, 0))


def _dw_in(h, dproj):
    return _dw(h, dproj, _WHOLE_TOK, pl.BlockSpec((S, PS), lambda s: (0, s)), D, PS, "dw_in")


def _dw_out(mixed, dx):
    return _dw(mixed, dx, pl.BlockSpec((S, PW), lambda s: (0, s)), _WHOLE_TOK, PW, D, "dw_out", cast_rhs=True)


def _final_loss(x, g, target):
    def body(x_ref, g_ref, t_ref, loss_ref, dx_ref, dg_ref):
        i = pl.program_id(0)
        xf = x_ref[...]
        r = lax.rsqrt(jnp.mean(xf * xf, axis=-1, keepdims=True) + NORM_EPS)
        xhat = xf * r
        err = xhat * g_ref[...] - t_ref[...]
        dy = err * (1.0 / D)
        dhg = dy * g_ref[...]
        dx_ref[...] = r * (dhg - xhat * jnp.mean(dhg * xhat, axis=-1, keepdims=True))
        part = jnp.sum(dy * xhat, axis=0, keepdims=True)
        lpart = jnp.zeros((8, 128), f32) + 0.5 * jnp.sum(jnp.mean(err * err, axis=-1, keepdims=True))

        @pl.when(i == 0)
        def _():
            dg_ref[...] = part
            loss_ref[...] = lpart

        @pl.when(i != 0)
        def _():
            dg_ref[...] += part
            loss_ref[...] += lpart

    tok = pl.BlockSpec((TM, D), lambda i: (i, 0))
    vec = pl.BlockSpec((1, D), lambda i: (0, 0))
    return pl.pallas_call(
        body, out_shape=(_sds((8, 128), f32), _sds((S, D), f32), _sds((1, D), f32)), grid=(S // TM,),
        in_specs=[tok, vec, tok], out_specs=(pl.BlockSpec((8, 128), lambda i: (0, 0)), tok, vec),
        name="final_loss", compiler_params=_cp("arbitrary"))(x, g, target)


def _shift_down(x, k, row):
    return jnp.where(row >= k, pltpu.roll(x, k, axis=0), 0.0)


def _shift_up(x, k, row):
    return jnp.where(row < S - k, pltpu.roll(x, S - k, axis=0), 0.0)


def _pool_geometry():
    row = lax.broadcasted_iota(jnp.int32, (S, PW), 0)
    grp = lax.broadcasted_iota(jnp.int32, (S, PW), 1) // 64
    half = jnp.where(grp == 0, 1, jnp.where(grp == 1, 2, jnp.where(grp == 2, 4, 8)))
    hi = jnp.minimum(row + half - 1, S - 1)
    lo = jnp.maximum(row - half, 0)
    return row, grp, (hi - lo + 1).astype(f32)


def _by_group(grp, v0, v1, v2, v3):
    return jnp.where(grp == 0, v0, jnp.where(grp == 1, v1, jnp.where(grp == 2, v2, v3)))


def _window_sums(x, row, grp, transpose):
    l1, r1 = x, x
    l2, r2 = l1 + _shift_down(l1, 1, row), r1 + _shift_up(r1, 1, row)
    l4, r4 = l2 + _shift_down(l2, 2, row), r2 + _shift_up(r2, 2, row)
    l8, r8 = l4 + _shift_down(l4, 4, row), r4 + _shift_up(r4, 4, row)
    lsel = _by_group(grp, l1, l2, l4, l8)
    rsel = _by_group(grp, r1, r2, r4, r8)
    if transpose:
        return lsel + _shift_up(rsel, 1, row)
    return _shift_down(lsel, 1, row) + rsel


def _pool_fwd(proj, wbd, scale):
    def body(v_ref, w_ref, sc_ref, mixed_ref, diff_ref):
        row, grp, cnt = _pool_geometry()
        v = v_ref[...]
        diff = (_window_sums(v, row, grp, False) / cnt - v).astype(bf16)
        diff_ref[...] = diff
        mixed_ref[...] = (_dot(diff, w_ref[...].astype(bf16)) * sc_ref[...]).astype(bf16)

    col = pl.BlockSpec((S, PW), lambda i: (0, 0))
    return pl.pallas_call(
        body, out_shape=(_sds((S, D), bf16), _sds((S, PW), bf16)), grid=(1,),
        in_specs=[col, pl.BlockSpec((PW, PW), lambda i: (0, 0)), pl.BlockSpec((1, PW), lambda i: (0, 0))],
        out_specs=(col, col), name="pool_fwd", compiler_params=_cp("arbitrary"))(proj, wbd, scale)


def _pool_bwd(dmixed, diff, wbd, scale, dproj):
    def body(dy_ref, diff_ref, w_ref, sc_ref, dproj_in, dv_ref, dw_ref, dsc_ref):
        del dproj_in
        row, grp, cnt = _pool_geometry()
        dy = dy_ref[...]
        diff = diff_ref[...]
        w = w_ref[...].astype(bf16)
        dsc_ref[...] = jnp.sum(dy * _dot(diff, w), axis=0, keepdims=True)
        dys = (dy * sc_ref[...]).astype(bf16)
        dw_ref[...] = _dot_tn(diff, dys)
        ddiff = _dot_nt(dys, w)
        dv_ref[...] = (_window_sums(ddiff / cnt, row, grp, True) - ddiff).astype(bf16)

    col = pl.BlockSpec((S, PW), lambda i: (0, 0))
    return pl.pallas_call(
        body, out_shape=(_sds((S, PROJ), bf16), _sds((PW, PW), f32), _sds((1, PW), f32)), grid=(1,),
        in_specs=[col, col, pl.BlockSpec((PW, PW), lambda i: (0, 0)), pl.BlockSpec((1, PW), lambda i: (0, 0)), ANY],
        out_specs=(col, pl.BlockSpec((PW, PW), lambda i: (0, 0)), pl.BlockSpec((1, PW), lambda i: (0, 0))),
        input_output_aliases={4: 0}, name="pool_bwd", compiler_params=_cp("arbitrary"))(dmixed, diff, wbd, scale, dproj)


def _rope_tables(pos_col, freq_row):
    def body(p_ref, f_ref, c_ref, a_ref, b_ref):
        ang = p_ref[...].astype(f32) * f_ref[...]
        l64 = lax.broadcasted_iota(jnp.int32, (S, 128), 1) % 64
        cos, sin = jnp.cos(ang), jnp.sin(ang)
        c_ref[...] = jnp.where(l64 < 16, cos, 1.0)
        a_ref[...] = jnp.where(l64 < 8, -sin, 0.0)
        b_ref[...] = jnp.where((l64 >= 8) & (l64 < 16), sin, 0.0)

    t = _sds((S, 128), f32)
    return pl.pallas_call(body, out_shape=(t, t, t), name="rope_tables", compiler_params=_cp())(pos_col, freq_row)


def _rope(t, c, a, b):
    return t * c + pltpu.roll(t, 120, axis=1) * a + pltpu.roll(t, 8, axis=1) * b


def _rope_bwd(g, c, a, b):
    return g * c + pltpu.roll(g * a, 8, axis=1) + pltpu.roll(g * b, 120, axis=1)


def _perm_load(ref, d):
    if d == 1:
        return ref[...]
    n = S // d
    return jnp.concatenate([ref[pl.ds(r, n, stride=d), :] for r in range(d)], axis=0)


def _unperm_store(ref, val, d):
    if d == 1:
        ref[...] = val
        return
    n = S // d
    for r in range(d):
        ref[pl.ds(r, n, stride=d), :] = val[r * n:(r + 1) * n, :]


def _band(xp):
    z = jnp.zeros((64, 128), bf16)
    p = jnp.concatenate([z, xp, z], axis=0).reshape(NBLK + 1, QBLK, 128)
    return jnp.concatenate([p[:NBLK], p[1:]], axis=1)


def _unband(xb):
    z = jnp.zeros((1, QBLK, 128), f32)
    p = jnp.concatenate([xb[:, :QBLK], z], axis=0) + jnp.concatenate([z, xb[:, QBLK:]], axis=0)
    return p.reshape(S + QBLK, 128)[64:S + 64]


def _band_mask(d):
    blocks_per_class = NBLK // d
    n = lax.broadcasted_iota(jnp.int32, (NBLK, 1, 2 * QBLK), 0) & (blocks_per_class - 1)
    be = lax.broadcasted_iota(jnp.int32, (NBLK, 1, 2 * QBLK), 2)
    a = lax.broadcasted_iota(jnp.int32, (1, 2 * QBLK, 2 * QBLK), 1) & (QBLK - 1)
    b = lax.broadcasted_iota(jnp.int32, (1, 2 * QBLK, 2 * QBLK), 2)
    band = (b >= a) & (b <= a + 128)
    edge = ((be >= 64) | (n != 0)) & ((be < QBLK + 64) | (n != blocks_per_class - 1))
    return band & edge


def _stack_heads(xb, lo):
    z = jnp.zeros_like(xb)
    return jnp.concatenate([jnp.where(lo, xb, z), jnp.where(lo, z, xb)], axis=1)


def _unstack_heads(x2, lo):
    return jnp.where(lo, x2[:, :QBLK], x2[:, QBLK:])


def _rows_to_lanes(col2, lo):
    return jnp.where(lo, jnp.broadcast_to(col2[:, :QBLK], (NBLK, QBLK, 128)), jnp.broadcast_to(col2[:, QBLK:], (NBLK, QBLK, 128)))


def _bmm_nt(a, b):
    return jnp.einsum('nqd,nkd->nqk', a, b, preferred_element_type=f32)


def _bmm_nn(a, b):
    return jnp.einsum('nqk,nkd->nqd', a, b, preferred_element_type=f32)


def _bmm_tn(a, b):
    return jnp.einsum('nqk,nqd->nkd', a, b, preferred_element_type=f32)


def _attn_fwd(proj, tc, ta, tb, mixed):
    def body(q_ref, k_ref, v_ref, c_ref, a_ref, b_ref, mixed_in, mixed_ref, o_ref, lse_ref, qn, kn, t_num, t_m, t_den):
        del mixed_in
        lo = lax.broadcasted_iota(jnp.int32, (1, 1, 128), 2) < 64
        c, a, b = c_ref[...], a_ref[...], b_ref[...]
        qn[...] = _rope(q_ref[...], c, a, b)
        kn[...] = _rope(k_ref[...], c, a, b)
        run = None
        for d in DILATIONS:
            q2 = _stack_heads(_perm_load(qn, d).astype(bf16).reshape(NBLK, QBLK, 128), lo)
            kb = _band(_perm_load(kn, d).astype(bf16))
            vb = _band(_perm_load(v_ref, d).astype(bf16))
            s = jnp.where(_band_mask(d), _bmm_nt(q2, kb) * 0.125, MASK_VALUE)
            m = jnp.max(s, axis=2, keepdims=True)
            p = jnp.exp(s - m)
            den = jnp.sum(p, axis=2, keepdims=True)
            num = _unstack_heads(_bmm_nn(p.astype(bf16), vb), lo)
            _unperm_store(t_num, num.reshape(S, 128), d)
            _unperm_store(t_m, _rows_to_lanes(m, lo).reshape(S, 128), d)
            _unperm_store(t_den, _rows_to_lanes(den, lo).reshape(S, 128), d)
            if run is None:
                run = (t_m[...], t_num[...], t_den[...])
            else:
                m_new = jnp.maximum(run[0], t_m[...])
                w_old, w_new = jnp.exp(run[0] - m_new), jnp.exp(t_m[...] - m_new)
                run = (m_new, w_old * run[1] + w_new * t_num[...], w_old * run[2] + w_new * t_den[...])
        out = run[1] / run[2]
        o_ref[...] = out
        mixed_ref[...] = out.astype(bf16)
        lse_ref[...] = run[0] + jnp.log(run[2])

    def col(off):
        return pl.BlockSpec((S, 128), lambda j, off=off: (0, off + j))

    tab = pl.BlockSpec((S, 128), lambda j: (0, 0))
    scr = pltpu.VMEM((S, 128), f32)
    return pl.pallas_call(
        body, out_shape=(_sds((S, D), bf16), _sds((S, AW), f32), _sds((S, AW), f32)), grid=(NPAIR,),
        in_specs=[col(2), col(8), col(14), tab, tab, tab, ANY], out_specs=(col(2), col(0), col(0)),
        scratch_shapes=[scr, scr, scr, scr, scr], input_output_aliases={6: 0}, name="attn_fwd",
        compiler_params=_cp("arbitrary"))(proj, proj, proj, tc, ta, tb, mixed)


def _attn_bwd(proj, tc, ta, tb, o, lse, dmixed):
    def body(q_ref, k_ref, v_ref, c_ref, a_ref, b_ref, o_ref, lse_ref, do_ref, dp_ref, qn, kn, tmp, dk_s, dv_s):
        t = pl.program_id(1)

        @pl.when(t == 0)
        def _():
            lo = lax.broadcasted_iota(jnp.int32, (1, 1, 128), 2) < 64
            c, a, b = c_ref[...], a_ref[...], b_ref[...]
            qn[...] = _rope(q_ref[...], c, a, b)
            kn[...] = _rope(k_ref[...], c, a, b)
            dq = dk = dv = None
            for d in DILATIONS:
                q2 = _stack_heads(_perm_load(qn, d).astype(bf16).reshape(NBLK, QBLK, 128), lo)
                kb = _band(_perm_load(kn, d).astype(bf16))
                vb = _band(_perm_load(v_ref, d).astype(bf16))
                dob = _perm_load(do_ref, d).reshape(NBLK, QBLK, 128)
                ob = _perm_load(o_ref, d).reshape(NBLK, QBLK, 128)
                lsb = _perm_load(lse_ref, d).reshape(NBLK, QBLK, 128)
                do2 = _stack_heads(dob.astype(bf16), lo)
                delta2 = jnp.sum(_stack_heads(dob * ob, lo), axis=2, keepdims=True)
                lse2 = jnp.max(jnp.concatenate([jnp.where(lo, lsb, MASK_VALUE), jnp.where(lo, MASK_VALUE, lsb)], axis=1),
                               axis=2, keepdims=True)
                s = _bmm_nt(q2, kb) * 0.125
                p = jnp.where(_band_mask(d), jnp.exp(s - lse2), 0.0)
                ds = (p * (_bmm_nt(do2, vb) - delta2) * 0.125).astype(bf16)
                pb = p.astype(bf16)
                dq_b = _unstack_heads(_bmm_nn(ds, kb), lo).reshape(S, 128)
                dk_b = _unband(_bmm_tn(ds, q2))
                dv_b = _unband(_bmm_tn(pb, do2))
                acc = []
                for prev, new in ((dq, dq_b), (dk, dk_b), (dv, dv_b)):
                    _unperm_store(tmp, new, d)
                    acc.append(tmp[...] if prev is None else prev + tmp[...])
                dq, dk, dv = acc
            dp_ref[...] = _rope_bwd(dq, c, a, b).astype(bf16)
            dk_s[...] = _rope_bwd(dk, c, a, b).astype(bf16)
            dv_s[...] = dv.astype(bf16)

        @pl.when(t == 1)
        def _():
            dp_ref[...] = dk_s[...]

        @pl.when(t == 2)
        def _():
            dp_ref[...] = dv_s[...]

    def col(off):
        return pl.BlockSpec((S, 128), lambda j, t, off=off: (0, off + j))

    tab = pl.BlockSpec((S, 128), lambda j, t: (0, 0))
    scr = pltpu.VMEM((S, 128), f32)
    scb = pltpu.VMEM((S, 128), bf16)
    return pl.pallas_call(
        body, out_shape=_sds((S, PROJ), bf16), grid=(NPAIR, 3),
        in_specs=[col(2), col(8), col(14), tab, tab, tab, col(0), col(0), col(2)],
        out_specs=pl.BlockSpec((S, 128), lambda j, t: (0, 2 + NPAIR * t + j)),
        scratch_shapes=[scr, scr, scr, scb, scb], name="attn_bwd",
        compiler_params=_cp("arbitrary", "arbitrary"))(proj, proj, proj, tc, ta, tb, o, lse, dmixed)


def _block_diag(w4):
    out = jnp.zeros((PW, PW), w4.dtype)
    for g in range(4):
        out = out.at[64 * g:64 * (g + 1), 64 * g:64 * (g + 1)].set(w4[g])
    return out


def _diag_blocks(w):
    return jnp.stack([w[64 * g:64 * (g + 1), 64 * g:64 * (g + 1)] for g in range(4)])


def _rope_inputs(positions):
    inv_freq = ROPE_THETA ** (-jnp.arange(0, 16, 2, dtype=f32) / 16)
    l64 = np.arange(128) % 64
    idx = np.where(l64 < 16, l64 % 8, 0)
    return positions.reshape(S, 1), inv_freq[idx].reshape(1, 128)


def _layer_fwd(x, w, small, l, tabs, dep=None):
    g1, gm, g2 = (small[k][l].reshape(1, D) for k in ("ffn1_norm", "mix_norm", "ffn2_norm"))
    wbd = _block_diag(small["pool_w"][l])
    psc = small["pool_scale"][l].reshape(1, PW)
    h1 = _rms_fwd(x, g1, dep)
    a1, b1, u1 = _ffn_up(h1, w["g1"], w["u1"])
    x1 = _ffn_down(x, u1, w["d1"])
    h2 = _rms_fwd(x1, gm)
    proj = _in_proj(h2, w["wi"])
    mixed, diff = _pool_fwd(proj, wbd, psc)
    mixed, o, lse = _attn_fwd(proj, *tabs, mixed)
    x2 = _out_proj(x1, mixed, w["wo"])
    h3 = _rms_fwd(x2, g2)
    a2, b2, u2 = _ffn_up(h3, w["g2"], w["u2"])
    out = _ffn_down(x2, u2, w["d2"])
    return out, dict(x0=x, h1=h1, a1=a1, b1=b1, u1=u1, x1=x1, h2=h2, proj=proj, mixed=mixed, diff=diff, o=o, lse=lse,
                     x2=x2, h3=h3, a2=a2, b2=b2, u2=u2, g1=g1, gm=gm, g2=g2, wbd=wbd, psc=psc)


def _layer_bwd(dx, w, sv, tabs, dep=None):
    gr, sg = {}, {}
    da, db, gr["d2"], gr["g2"], gr["u2"] = _ffn_bwd_mid(dx, sv["h3"], sv["a2"], sv["b2"], w["d2"], dep)
    dx, sg["ffn2_norm"] = _ffn_bwd_dx(dx, sv["x2"], sv["g2"], da, db, w["g2"], w["u2"])
    gr["wo"] = _dw_out(sv["mixed"], dx)
    dmixed = _out_proj_bwd(dx, w["wo"])
    dproj = _attn_bwd(sv["proj"], *tabs, sv["o"], sv["lse"], dmixed)
    dproj, dwbd, sg["pool_scale"] = _pool_bwd(dmixed, sv["diff"], sv["wbd"], sv["psc"], dproj)
    sg["pool_w"] = _diag_blocks(dwbd)
    gr["wi"] = _dw_in(sv["h2"], dproj)
    dx, sg["mix_norm"] = _in_proj_bwd_dx(dx, sv["x1"], sv["gm"], dproj, w["wi"])
    da, db, gr["d1"], gr["g1"], gr["u1"] = _ffn_bwd_mid(dx, sv["h1"], sv["a1"], sv["b1"], w["d1"])
    dx, sg["ffn1_norm"] = _ffn_bwd_dx(dx, sv["x0"], sv["g1"], da, db, w["g1"], w["u1"])
    return dx, gr, sg


def _forward_backward(x, positions, target, gathered, small):
    tabs = _rope_tables(*_rope_inputs(positions))
    saved = []
    for l in range(DEPTH):
        x, sv = _layer_fwd(x, gathered[l], small, l, tabs)
        saved.append(sv)
    loss, dx, dgf = _final_loss(x, small["final_norm"].reshape(1, D), target)
    big = [None] * DEPTH
    sg = {k: [None] * DEPTH for k in ("ffn1_norm", "mix_norm", "pool_w", "pool_scale", "ffn2_norm")}
    for l in reversed(range(DEPTH)):
        dx, big[l], sgl = _layer_bwd(dx, gathered[l], saved[l], tabs)
        for k, v in sgl.items():
            sg[k][l] = v
    sg["final_norm"] = dgf
    return loss, dx, big, sg


def _place():
    x, y, c = lax.axis_index("x"), lax.axis_index("y"), lax.axis_index("c")
    chips = [(1 - x, y), (x, 1 - y), (1 - x, 1 - y)]
    return x, y, c, chips


def _cast_layer(params, l, place):
    def body(p_ref, *refs):
        del p_ref
        for i_ref, o_ref in zip(refs[:8], refs[8:]):
            o_ref[...] = i_ref[...].astype(bf16).reshape(o_ref.shape)

    ins, in_specs, out_shape, out_specs = [], [], [], []
    for name, rows, cols in BIG:
        q = rows // 4
        ins.append(params[BIG_SRC[name]])
        in_specs.append(pl.BlockSpec((1, q, cols), lambda i, p, l=l: (l, i, 0)))
        out_shape.append(_sds((NSH, 2, rows // 2, cols), bf16))
        out_specs.append(pl.BlockSpec((1, 1, q, cols), lambda i, p: (p[1], i // 2, i % 2, 0)))
    return pl.pallas_call(
        body, out_shape=out_shape,
        grid_spec=pltpu.PrefetchScalarGridSpec(num_scalar_prefetch=1, grid=(4,), in_specs=in_specs, out_specs=out_specs),
        name=f"cast_layer{l}", compiler_params=_cp("parallel"))(place, *ins)


HBM = pl.BlockSpec(memory_space=pltpu.HBM)
SEM = pl.BlockSpec(memory_space=pltpu.SEMAPHORE)
_SPLIT = pltpu.CompilerParams(has_side_effects=pltpu.SideEffectType.DATAFLOW_SIDE_EFFECTING)


def _hbm(arrays):
    return [pltpu.with_memory_space_constraint(a, pltpu.HBM) for a in arrays]


def _chip_copies(src_of, dst_of, send_sems, recv_sems, n):
    x, y, c, chips = _place()
    me = 2 * x + y
    out = []
    for t in range(n):
        for k, chip in enumerate(chips):
            peer = 2 * chip[0] + chip[1]
            send = pltpu.make_async_remote_copy(
                src_ref=src_of(t, peer), dst_ref=dst_of(t, me), send_sem=send_sems.at[3 * t + k], recv_sem=recv_sems.at[3 * t + k],
                device_id=(chip[0], chip[1], c), device_id_type=MESH)
            land = pltpu.make_async_remote_copy(
                src_ref=src_of(t, peer), dst_ref=dst_of(t, peer), send_sem=send_sems.at[3 * t + k], recv_sem=recv_sems.at[3 * t + k],
                device_id=(chip[0], chip[1], c), device_id_type=MESH)
            out.append((send, land))
    return out


def _exchange_start(src, land, after, src_of, dst_of, name):
    n, m = len(src), len(src) + len(land)

    def body(*refs):
        src_refs = refs[:n]
        land_refs = refs[n:m] if land else src_refs
        send_sems, recv_sems = refs[m + 1], refs[m + 2]
        token = refs[-1]
        for send, _ in _chip_copies(lambda t, s: src_of(src_refs[t], s), lambda t, s: dst_of(land_refs[t], s), send_sems, recv_sems, n):
            send.start()
        token[...] = jnp.zeros_like(token)

    arrays = list(src) + list(land)
    out_shape = ([pltpu.SemaphoreType.DMA((3 * n,)), pltpu.SemaphoreType.DMA((3 * n,))] + [pltpu.HBM(a.shape, a.dtype) for a in arrays]
                 + [_sds((8, 128), f32)])
    res = pl.pallas_call(
        body, out_shape=out_shape, in_specs=[HBM] * m + [ANY], out_specs=[SEM, SEM] + [HBM] * m + [pl.BlockSpec(memory_space=pltpu.VMEM)],
        input_output_aliases={i: 2 + i for i in range(m)}, name=name, compiler_params=_SPLIT)(*_hbm(arrays), after)
    return res[0], res[1], list(res[2:2 + n]), list(res[2 + n:2 + m]), res[-1]


def _exchange_wait(send_sems, recv_sems, src, land, after, src_of, dst_of, name):
    n, m = len(src), len(src) + len(land)

    def body(*refs):
        src_refs = refs[:n]
        land_refs = refs[n:m] if land else src_refs
        send_sems, recv_sems = refs[m], refs[m + 1]
        for send, land_cp in _chip_copies(lambda t, s: src_of(src_refs[t], s), lambda t, s: dst_of(land_refs[t], s), send_sems, recv_sems, n):
            send.wait_send()
            land_cp.wait_recv()

    arrays = list(src) + list(land)
    res = pl.pallas_call(
        body, out_shape=[pltpu.HBM(a.shape, a.dtype) for a in arrays], in_specs=[HBM] * m + [SEM, SEM, ANY], out_specs=[HBM] * m,
        input_output_aliases={i: i for i in range(m)}, name=name, compiler_params=_SPLIT)(*arrays, send_sems, recv_sems, after)
    return list(res[:n]), list(res[n:])


def _own_half(ref, s):
    x, y, c, _ = _place()
    return ref.at[2 * x + y, c]


def _slot_half(ref, s):
    return ref.at[s, lax.axis_index("c")]


def _slot(ref, s):
    return ref.at[s]


def _gather_forward(bufs):
    n = len(bufs)

    def body(*refs):
        outs = refs[n:2 * n]
        send_sems, recv_sems = refs[2 * n:]
        x, y, c, chips = _place()
        sibling = (x, y, 1 - c)
        passed = []
        for t in range(n):
            for k, chip in enumerate(chips):
                blk = outs[t].at[2 * chip[0] + chip[1], c]
                cp = pltpu.make_async_remote_copy(
                    src_ref=blk, dst_ref=blk, send_sem=send_sems.at[t, k], recv_sem=recv_sems.at[t, k],
                    device_id=sibling, device_id_type=MESH)
                cp.start()
                passed.append(cp)
        for t in range(n):
            for k, chip in enumerate(chips):
                blk = outs[t].at[2 * chip[0] + chip[1], 1 - c]
                pltpu.make_async_remote_copy(
                    src_ref=blk, dst_ref=blk, send_sem=send_sems.at[t, k], recv_sem=recv_sems.at[t, k],
                    device_id=sibling, device_id_type=MESH).wait_recv()
        for cp in passed:
            cp.wait_send()

    out_shape = [_sds(a.shape, bf16) for a in bufs]
    return pl.pallas_call(
        body, out_shape=out_shape, in_specs=[ANY] * n, out_specs=[ANY] * n, input_output_aliases={t: t for t in range(n)},
        scratch_shapes=[pltpu.SemaphoreType.DMA((n, 3)), pltpu.SemaphoreType.DMA((n, 3))], name="gather_forward")(*bufs)


def _sibling_swap(grads):
    n = len(grads)

    def body(*refs):
        ins, outs = refs[:n], refs[n:2 * n]
        send_sems, recv_sems = refs[2 * n:]
        x, y, c, _ = _place()
        cps = []
        for t in range(n):
            for s in range(NSH):
                cp = pltpu.make_async_remote_copy(
                    src_ref=ins[t].at[s, 1 - c], dst_ref=outs[t].at[s], send_sem=send_sems.at[t, s], recv_sem=recv_sems.at[t, s],
                    device_id=(x, y, 1 - c), device_id_type=MESH)
                cp.start()
                cps.append(cp)
        for cp in cps:
            cp.wait()

    out_shape = [_sds((NSH,) + a.shape[2:], bf16) for a in grads]
    return pl.pallas_call(
        body, out_shape=out_shape, in_specs=[ANY] * n, out_specs=[ANY] * n,
        scratch_shapes=[pltpu.SemaphoreType.DMA((n, NSH)), pltpu.SemaphoreType.DMA((n, NSH))],
        name="sibling_swap")(*grads)


def _row_tile(h):
    return h // 2 if h % 32 == 0 else h


def _pair_sum(grads, got, c_idx):
    n = len(grads)

    def body(c_ref, *refs):
        del c_ref
        for t in range(n):
            refs[2 * n + t][...] = (refs[t][...].astype(f32).reshape(refs[n + t].shape) + refs[n + t][...].astype(f32)).astype(bf16)

    in_specs, out_shape, out_specs = [], [], []
    for a in grads:
        h, cols = a.shape[2:]
        in_specs.append(pl.BlockSpec((1, 1, _row_tile(h), cols), lambda s, i, c: (s, c[0], i, 0)))
    for a in grads:
        h, cols = a.shape[2:]
        in_specs.append(pl.BlockSpec((1, _row_tile(h), cols), lambda s, i, c: (s, i, 0)))
        out_shape.append(_sds((NSH, h, cols), bf16))
        out_specs.append(pl.BlockSpec((1, _row_tile(h), cols), lambda s, i, c: (s, i, 0)))
    return pl.pallas_call(
        body, out_shape=out_shape,
        grid_spec=pltpu.PrefetchScalarGridSpec(num_scalar_prefetch=1, grid=(NSH, 2), in_specs=in_specs, out_specs=out_specs),
        name="pair_sum", compiler_params=_cp("parallel", "parallel"))(c_idx, *grads, *got)


def _chip_sum(psum, parts, full, place, l, name):
    n = len(parts)

    def body(p_ref, *refs):
        s = pl.program_id(1)
        for t in range(n):
            val = jnp.where(s == p_ref[1], refs[t][0], refs[n + t][0]).astype(f32)
            out = refs[3 * n + t]

            @pl.when(s == 0)
            def _(out=out, val=val):
                out[0, 0] = val

            @pl.when(s != 0)
            def _(out=out, val=val):
                out[0, 0] += val

    own_specs, part_specs, out_shape, out_specs = [], [], [], []
    for a, fl in zip(parts, full):
        _, h, cols = a.shape
        r = _row_tile(h)
        own_specs.append(pl.BlockSpec((1, r, cols), lambda i, s, p: (p[1], i, 0)))
        part_specs.append(pl.BlockSpec((1, r, cols), lambda i, s, p: (jnp.where(s == p[1], (s + 1) % NSH, s), i, 0)))
        out_shape.append(_sds(fl.shape, f32))
        out_specs.append(pl.BlockSpec((1, 1, r, cols), lambda i, s, p, l=l: (l, p[0], i, 0)))
    return pl.pallas_call(
        body, out_shape=out_shape,
        grid_spec=pltpu.PrefetchScalarGridSpec(num_scalar_prefetch=1, grid=(2, NSH), in_specs=own_specs + part_specs + [ANY] * n,
                                               out_specs=out_specs),
        input_output_aliases={1 + 2 * n + t: t for t in range(n)}, name=name,
        compiler_params=_cp("parallel", "arbitrary"))(place, *psum, *parts, *full)


def _sibling_share(full, l, name):
    n = len(full)

    def body(*refs):
        outs = refs[n:2 * n]
        send_sems, recv_sems = refs[2 * n:]
        x, y, c, _ = _place()
        sibling = (x, y, 1 - c)
        cps = []
        for t in range(n):
            blk = outs[t].at[l, c]
            cp = pltpu.make_async_remote_copy(
                src_ref=blk, dst_ref=blk, send_sem=send_sems.at[t], recv_sem=recv_sems.at[t], device_id=sibling, device_id_type=MESH)
            cp.start()
            cps.append(cp)
        for t in range(n):
            blk = outs[t].at[l, 1 - c]
            pltpu.make_async_remote_copy(
                src_ref=blk, dst_ref=blk, send_sem=send_sems.at[t], recv_sem=recv_sems.at[t],
                device_id=sibling, device_id_type=MESH).wait_recv()
        for cp in cps:
            cp.wait_send()

    out_shape = [_sds(a.shape, f32) for a in full]
    return pl.pallas_call(
        body, out_shape=out_shape, in_specs=[ANY] * n, out_specs=[ANY] * n, input_output_aliases={t: t for t in range(n)},
        scratch_shapes=[pltpu.SemaphoreType.DMA((n,)), pltpu.SemaphoreType.DMA((n,))], name=name)(*full)


SMALL_ROWS = 656


def _pack_small(per_layer, final_vec, loss_tile):
    rows = []
    for l in range(DEPTH):
        for k in ("ffn1_norm", "mix_norm", "ffn2_norm"):
            rows.append(per_layer[k][l].reshape(8, 128))
        rows.append(per_layer["pool_w"][l].reshape(128, 128))
        rows.append(jnp.pad(per_layer["pool_scale"][l].reshape(2, 128), ((0, 6), (0, 0))))
    rows.append(final_vec.reshape(8, 128))
    rows.append(loss_tile)
    return jnp.concatenate(rows, axis=0)


def _unpack_small(buf):
    out = {k: [] for k in ("ffn1_norm", "mix_norm", "ffn2_norm", "pool_w", "pool_scale")}
    r = 0
    for l in range(DEPTH):
        for k in ("ffn1_norm", "mix_norm", "ffn2_norm"):
            out[k].append(buf[r:r + 8].reshape(D))
            r += 8
        out["pool_w"].append(buf[r:r + 128].reshape(4, 64, 64))
        r += 128
        out["pool_scale"].append(buf[r:r + 2].reshape(PW))
        r += 8
    res = {k: jnp.stack(v) for k, v in out.items()}
    res["final_norm"] = buf[r:r + 8].reshape(D)
    res["loss"] = buf[r + 8, 0]
    return res


def _allreduce_small(buf):
    def body(in_ref, out_ref, slots, send_sems, recv_sems):
        x, y, c, _ = _place()
        me = 4 * x + 2 * y + c
        slots[me] = in_ref[...]
        peers = []
        for k in range(1, 8):
            px, py, pc = x ^ (k >> 2), y ^ ((k >> 1) & 1), c ^ (k & 1)
            cp = pltpu.make_async_remote_copy(
                src_ref=in_ref, dst_ref=slots.at[me], send_sem=send_sems.at[k - 1], recv_sem=recv_sems.at[k - 1],
                device_id=(px, py, pc), device_id_type=MESH)
            cp.start()
            peers.append(cp)
        for k in range(1, 8):
            px, py, pc = x ^ (k >> 2), y ^ ((k >> 1) & 1), c ^ (k & 1)
            slot = 4 * px + 2 * py + pc
            pltpu.make_async_remote_copy(
                src_ref=slots.at[slot], dst_ref=slots.at[slot], send_sem=send_sems.at[k - 1], recv_sem=recv_sems.at[k - 1],
                device_id=(px, py, pc), device_id_type=MESH).wait_recv()
        for cp in peers:
            cp.wait_send()
        acc = slots[0]
        for j in range(1, 8):
            acc = acc + slots[j]
        out_ref[...] = acc

    return pl.pallas_call(
        body, out_shape=_sds((SMALL_ROWS, 128), f32),
        in_specs=[pl.BlockSpec(memory_space=pltpu.VMEM)], out_specs=pl.BlockSpec(memory_space=pltpu.VMEM),
        scratch_shapes=[pltpu.VMEM((8, SMALL_ROWS, 128), f32), pltpu.SemaphoreType.DMA((7,)), pltpu.SemaphoreType.DMA((7,))],
        name="allreduce_small", compiler_params=_cp())(buf)


def _adamw_math(w, g, m, v):
    m = ADAM_B1 * m + (1.0 - ADAM_B1) * g
    v = ADAM_B2 * v + (1.0 - ADAM_B2) * (g * g)
    m_hat = m / (1.0 - ADAM_B1 ** ADAM_STEP)
    v_hat = v / (1.0 - ADAM_B2 ** ADAM_STEP)
    return -ADAM_LR * (m_hat / (jnp.sqrt(v_hat) + ADAM_EPS) + ADAM_WD * w), m, v


def _adamw(w, g, m, v, name, first=0, prev=None, dep=None):
    def body(w_ref, g_ref, m_ref, v_ref, *rest):
        go_ref, d_ref, mo_ref, vo_ref = rest[-4:]
        g = g_ref[...]
        d, mn, vn = _adamw_math(w_ref[...], g, m_ref[...], v_ref[...])
        go_ref[...] = g
        d_ref[...] = d
        mo_ref[...] = mn
        vo_ref[...] = vn

    _, rows, cols = w.shape
    r = rows // 4 if rows % 32 == 0 else rows
    spec = pl.BlockSpec((1, r, cols), lambda i, j: (first + i, j, 0))
    gspec = pl.BlockSpec((1, r, cols), lambda i, j: (i, j, 0))
    out = _sds(w.shape, f32)
    extra = [] if prev is None else list(prev)
    dspec, dop = _dep(dep)
    return pl.pallas_call(
        body, out_shape=(out, out, out, out), grid=(g.shape[0], rows // r), in_specs=[spec, gspec, spec, spec] + [ANY] * len(extra) + dspec,
        out_specs=(spec,) * 4, input_output_aliases={4 + i: i for i in range(len(extra))}, name=name,
        compiler_params=_cp("parallel", "parallel"))(w, g, m, v, *extra, *dop)


SMALL_NAMES = ("ffn1_norm", "mix_norm", "pool_w", "pool_scale", "ffn2_norm", "final_norm")
WEIGHT_ORDER = ("ffn1_norm", "ffn1_w_gate", "ffn1_w_up", "ffn1_w_down", "mix_norm", "w_in", "pool_w", "pool_scale", "w_out",
                "ffn2_norm", "ffn2_w_gate", "ffn2_w_up", "ffn2_w_down", "final_norm")


def _pack_small_params(p):
    per_layer = {k: [p[k][l] for l in range(DEPTH)] for k in ("ffn1_norm", "mix_norm", "ffn2_norm", "pool_w", "pool_scale")}
    return _pack_small(per_layer, p["final_norm"], jnp.zeros((8, 128), f32))


def kernel(x, positions, ffn1_norm, ffn1_w_gate, ffn1_w_up, ffn1_w_down, mix_norm, w_in, pool_w, pool_scale, w_out, ffn2_norm, ffn2_w_gate, ffn2_w_up, ffn2_w_down, final_norm, loss_target, m_ffn1_norm, m_ffn1_w_gate, m_ffn1_w_up, m_ffn1_w_down, m_mix_norm, m_w_in, m_pool_w, m_pool_scale, m_w_out, m_ffn2_norm, m_ffn2_w_gate, m_ffn2_w_up, m_ffn2_w_down, m_final_norm, v_ffn1_norm, v_ffn1_w_gate, v_ffn1_w_up, v_ffn1_w_down, v_mix_norm, v_w_in, v_pool_w, v_pool_scale, v_w_out, v_ffn2_norm, v_ffn2_w_gate, v_ffn2_w_up, v_ffn2_w_down, v_final_norm):
    params = dict(ffn1_norm=ffn1_norm, ffn1_w_gate=ffn1_w_gate, ffn1_w_up=ffn1_w_up, ffn1_w_down=ffn1_w_down,
                  mix_norm=mix_norm, w_in=w_in, pool_w=pool_w, pool_scale=pool_scale, w_out=w_out, ffn2_norm=ffn2_norm,
                  ffn2_w_gate=ffn2_w_gate, ffn2_w_up=ffn2_w_up, ffn2_w_down=ffn2_w_down, final_norm=final_norm)
    mom_m = dict(ffn1_norm=m_ffn1_norm, ffn1_w_gate=m_ffn1_w_gate, ffn1_w_up=m_ffn1_w_up, ffn1_w_down=m_ffn1_w_down,
                 mix_norm=m_mix_norm, w_in=m_w_in, pool_w=m_pool_w, pool_scale=m_pool_scale, w_out=m_w_out,
                 ffn2_norm=m_ffn2_norm, ffn2_w_gate=m_ffn2_w_gate, ffn2_w_up=m_ffn2_w_up, ffn2_w_down=m_ffn2_w_down,
                 final_norm=m_final_norm)
    mom_v = dict(ffn1_norm=v_ffn1_norm, ffn1_w_gate=v_ffn1_w_gate, ffn1_w_up=v_ffn1_w_up, ffn1_w_down=v_ffn1_w_down,
                 mix_norm=v_mix_norm, w_in=v_w_in, pool_w=v_pool_w, pool_scale=v_pool_scale, w_out=v_w_out,
                 ffn2_norm=v_ffn2_norm, ffn2_w_gate=v_ffn2_w_gate, ffn2_w_up=v_ffn2_w_up, ffn2_w_down=v_ffn2_w_down,
                 final_norm=v_final_norm)
    names = [t[0] for t in BIG]
    for d in (params, mom_m, mom_v):
        for k in TRANSPOSED:
            d[k] = jnp.swapaxes(d[k], 1, 2)

    place = jnp.stack([lax.axis_index("c"), 2 * lax.axis_index("x") + lax.axis_index("y")]).astype(jnp.int32)
    def gather_start(l, after):
        return _exchange_start(_cast_layer(params, l, place), [], after, _own_half, _slot_half, f"gather_start{l}")

    def gather_end(started, after, l):
        send_sems, recv_sems, bufs, _, _ = started
        bufs, _ = _exchange_wait(send_sems, recv_sems, bufs, [], after, _own_half, _slot_half, f"gather_wait{l}")
        return {nm: a.reshape(NSH, rows, cols) for (nm, rows, cols), a in zip(BIG, _gather_forward(bufs))}

    tabs = _rope_tables(*_rope_inputs(positions))
    h = x.reshape(S, D)
    weights, saved = [], []
    started = gather_start(0, place)
    after = started[-1]
    for l in range(DEPTH):
        weights.append(gather_end(started, after, l))
        dep = None
        if l + 1 < DEPTH:
            started = gather_start(l + 1, weights[l]["g1"])
            dep = started[-1]
        h, sv = _layer_fwd(h, weights[l], params, l, tabs, dep)
        saved.append(sv)
        after = h
    loss, dx, dgf = _final_loss(h, final_norm.reshape(1, D), loss_target.reshape(S, D))

    upper = [lax.empty((DEPTH - 1, 2, rows // 2, cols), f32) for _, rows, cols in BIG]
    lower = [lax.empty((1, 2, rows // 2, cols), f32) for _, rows, cols in BIG]
    sg = {k: [None] * DEPTH for k in ("ffn1_norm", "mix_norm", "pool_w", "pool_scale", "ffn2_norm")}
    sg["final_norm"] = dgf

    def reduce_end(started, after, l, full, slot):
        send_sems, recv_sems, psum, parts, _ = started
        psum, parts = _exchange_wait(send_sems, recv_sems, psum, parts, after, _slot, _slot, f"grad_wait{l}")
        return _sibling_share(_chip_sum(psum, parts, full, place, slot, f"chip_sum{l}"), slot, f"sibling_share{l}")

    started, dep = None, None
    for l in reversed(range(DEPTH)):
        dx, gr, sgl = _layer_bwd(dx, weights[l], saved[l], tabs, dep)
        for k, v in sgl.items():
            sg[k][l] = v
        if started is not None:
            upper = reduce_end(started, dx, l + 1, upper, l)
        grads = [gr[nm] for nm in names]
        psum = _pair_sum(grads, _sibling_swap(grads), place)
        parts = [lax.empty(a.shape, bf16) for a in psum]
        started = _exchange_start(psum, parts, place, _slot, _slot, f"grad_start{l}")
        dep = started[-1]

    big_out = {}
    for (nm, rows, cols), g in zip(BIG, upper):
        k = BIG_SRC[nm]
        big_out[k] = _adamw(params[k], g.reshape(DEPTH - 1, rows, cols), mom_m[k], mom_v[k], "adamw_upper_" + k, first=1, dep=dep)
        dep = big_out[k][1]
    lower = reduce_end(started, dep, 0, lower, 0)
    for (nm, rows, cols), g in zip(BIG, lower):
        k = BIG_SRC[nm]
        big_out[k] = _adamw(params[k], g.reshape(1, rows, cols), mom_m[k], mom_v[k], "adamw_lower_" + k, first=0, prev=big_out[k])

    per_layer = {k: sg[k] for k in ("ffn1_norm", "mix_norm", "ffn2_norm", "pool_w", "pool_scale")}
    small_sum = _allreduce_small(_pack_small(per_layer, sg["final_norm"], loss))
    gs, ds_, ms, vs = _adamw(_pack_small_params(params).reshape(1, SMALL_ROWS, 128), small_sum.reshape(1, SMALL_ROWS, 128),
                             _pack_small_params(mom_m).reshape(1, SMALL_ROWS, 128),
                             _pack_small_params(mom_v).reshape(1, SMALL_ROWS, 128), "adamw_small")
    small_out = [_unpack_small(a.reshape(SMALL_ROWS, 128)) for a in (gs, ds_, ms, vs)]

    grad, delta, new_m, new_v = {}, {}, {}, {}
    for k in WEIGHT_ORDER:
        if k in SMALL_NAMES:
            grad[k], delta[k], new_m[k], new_v[k] = (so[k] for so in small_out)
        else:
            grad[k], delta[k], new_m[k], new_v[k] = big_out[k]
    for d in (grad, delta, new_m, new_v):
        for k in TRANSPOSED:
            d[k] = jnp.swapaxes(d[k], 1, 2)
    return (small_out[0]["loss"], dx.reshape(1, S, D), *[grad[k] for k in WEIGHT_ORDER], *[delta[k] for k in WEIGHT_ORDER],
            *[new_m[k] for k in WEIGHT_ORDER], *[new_v[k] for k in WEIGHT_ORDER])
```

```python
import functools

import jax
import jax.numpy as jnp
import numpy as np
from jax import lax
from jax.experimental import pallas as pl
from jax.experimental.pallas import tpu as pltpu

f32 = jnp.float32
bf16 = jnp.bfloat16

S = 2048
D = 1024
DEPTH = 4
NSH = 4
FS = 704
PROJ = 2560
PS = 640
PW = 256
AW = 768
NPAIR = 6
NORM_EPS = 1e-6
MASK_VALUE = -1e30
ROPE_THETA = 500000.0
DILATIONS = (1, 4, 16)
QBLK = 128
NBLK = S // QBLK
TM = 512
EW_ROWS = 16
VMEM_LIMIT = 56 * 1024 * 1024

ADAM_LR = 0.001
ADAM_B1 = 0.9
ADAM_B2 = 0.999
ADAM_EPS = 1e-08
ADAM_WD = 0.01
ADAM_STEP = 10

MESH = pl.DeviceIdType.MESH
ANY = pl.BlockSpec(memory_space=pl.ANY)

BIG = (("g1", FS, D), ("u1", FS, D), ("d1", FS, D), ("wi", D, PS), ("wo", PW, D), ("g2", FS, D), ("u2", FS, D), ("d2", FS, D))
TRANSPOSED = ("ffn1_w_gate", "ffn1_w_up", "ffn2_w_gate", "ffn2_w_up")
BIG_SRC = {"g1": "ffn1_w_gate", "u1": "ffn1_w_up", "d1": "ffn1_w_down", "wi": "w_in", "wo": "w_out",
           "g2": "ffn2_w_gate", "u2": "ffn2_w_up", "d2": "ffn2_w_down"}


def _cp(*sem):
    return pltpu.CompilerParams(dimension_semantics=sem if sem else None, vmem_limit_bytes=VMEM_LIMIT)


def _sds(shape, dt):
    return jax.ShapeDtypeStruct(shape, dt)


def _dot(a, b):
    return jnp.dot(a, b, preferred_element_type=f32)


def _dot_nt(a, b):
    return lax.dot_general(a, b, (((1,), (1,)), ((), ())), preferred_element_type=f32)


def _dot_tn(a, b):
    return lax.dot_general(a, b, (((0,), (0,)), ((), ())), preferred_element_type=f32)


def _dep(dep):
    return ([], []) if dep is None else ([ANY], [dep])


def _resident(hbm_refs, vmem_refs, sems):
    first = pl.program_id(0) == 0

    def copies(s):
        return [pltpu.make_async_copy(h.at[s], v.at[s], sems.at[j, s]) for j, (h, v) in enumerate(zip(hbm_refs, vmem_refs))]

    def wait(s):
        @pl.when(first)
        def _():
            if s == 0:
                for t in range(NSH):
                    for cp in copies(t):
                        cp.start()
            for cp in copies(s):
                cp.wait()

    return wait


def _ffn_fwd(x, g, wg, wu, wd, dep=None):
    def body(x_ref, g_ref, wg_hbm, wu_hbm, wd_hbm, *rest):
        xo_ref, h_ref, a_ref, b_ref, wg_v, wu_v, wd_v, sems = rest[-8:]
        wait = _resident((wg_hbm, wu_hbm, wd_hbm), (wg_v, wu_v, wd_v), sems)
        xf = x_ref[...]
        r = lax.rsqrt(jnp.mean(xf * xf, axis=-1, keepdims=True) + NORM_EPS)
        hh = ((xf * r) * g_ref[...]).astype(bf16)
        h_ref[...] = hh
        acc = None
        for s in range(NSH):
            wait(s)
            a = _dot_nt(hh, wg_v[s])
            b = _dot_nt(hh, wu_v[s])
            a_ref[s] = a.astype(bf16)
            b_ref[s] = b.astype(bf16)
            p = _dot((a * (1.0 / (1.0 + jnp.exp(-a))) * b).astype(bf16), wd_v[s])
            acc = p if acc is None else acc + p
        xo_ref[...] = xf + 0.5 * acc

    tok = pl.BlockSpec((TM, D), lambda i: (i, 0))
    hid = pl.BlockSpec((NSH, TM, FS), lambda i: (0, i, 0))
    wscr = pltpu.VMEM((NSH, FS, D), bf16)
    dspec, dop = _dep(dep)
    return pl.pallas_call(
        body, out_shape=(_sds((S, D), f32), _sds((S, D), bf16), _sds((NSH, S, FS), bf16), _sds((NSH, S, FS), bf16)),
        grid=(S // TM,), in_specs=[tok, pl.BlockSpec((1, D), lambda i: (0, 0)), ANY, ANY, ANY] + dspec,
        out_specs=(tok, tok, hid, hid), scratch_shapes=[wscr, wscr, wscr, pltpu.SemaphoreType.DMA((3, NSH))],
        name="ffn_fwd", compiler_params=_cp("arbitrary"))(x, g, wg, wu, wd, *dop)


def _in_proj(x, g, wi):
    def body(x_ref, g_ref, w_hbm, o_ref, h_ref, w_v, sems):
        wait = _resident((w_hbm,), (w_v,), sems)
        xf = x_ref[...]
        r = lax.rsqrt(jnp.mean(xf * xf, axis=-1, keepdims=True) + NORM_EPS)
        hh = ((xf * r) * g_ref[...]).astype(bf16)
        h_ref[...] = hh
        for s in range(NSH):
            wait(s)
            o_ref[:, PS * s:PS * (s + 1)] = _dot(hh, w_v[s])

    tok = pl.BlockSpec((TM, D), lambda i: (i, 0))
    return pl.pallas_call(
        body, out_shape=(_sds((S, PROJ), f32), _sds((S, D), bf16)), grid=(S // TM,),
        in_specs=[tok, pl.BlockSpec((1, D), lambda i: (0, 0)), ANY],
        out_specs=(pl.BlockSpec((TM, PROJ), lambda i: (i, 0)), tok),
        scratch_shapes=[pltpu.VMEM((NSH, D, PS), bf16), pltpu.SemaphoreType.DMA((1, NSH))],
        name="in_proj", compiler_params=_cp("arbitrary"))(x, g, wi)


def _out_proj(x, mixed, wo):
    def body(x_ref, m_ref, w_ref, o_ref):
        o_ref[...] = x_ref[...] + _dot(m_ref[...], w_ref[...].reshape(D, D))

    return pl.pallas_call(
        body, out_shape=_sds((S, D), f32), grid=(S // TM,),
        in_specs=[pl.BlockSpec((TM, D), lambda i: (i, 0)), pl.BlockSpec((TM, D), lambda i: (i, 0)),
                  pl.BlockSpec((NSH, PW, D), lambda i: (0, 0, 0))],
        out_specs=pl.BlockSpec((TM, D), lambda i: (i, 0)), name="out_proj", compiler_params=_cp("parallel"))(x, mixed, wo)


def _out_proj_bwd(dx, wo):
    def body(dx_ref, w_ref, o_ref):
        o_ref[...] = _dot_nt(dx_ref[...].astype(bf16), w_ref[...].reshape(D, D))

    return pl.pallas_call(
        body, out_shape=_sds((S, D), f32), grid=(S // TM,),
        in_specs=[pl.BlockSpec((TM, D), lambda i: (i, 0)), pl.BlockSpec((NSH, PW, D), lambda i: (0, 0, 0))],
        out_specs=pl.BlockSpec((TM, D), lambda i: (i, 0)), name="out_proj_bwd", compiler_params=_cp("parallel"))(dx, wo)


def _ffn_bwd_mid(dx, h, a, b, wd, dep=None):
    nt = S // TM

    def body(dx_ref, h_ref, a_ref, b_ref, wd_ref, *rest):
        da_ref, db_ref, dwd_ref, dwg_ref, dwu_ref, acc_d, acc_g, acc_u, du_s, u_s = rest[-10:]
        i = pl.program_id(1)
        dy = (0.5 * dx_ref[...]).astype(bf16)
        hh = h_ref[...]
        du_s[...] = _dot_nt(dy, wd_ref[0])

        for c in range(TM // EW_ROWS):
            r = slice(c * EW_ROWS, (c + 1) * EW_ROWS)
            du = du_s[r, :]
            a = a_ref[0, r, :].astype(f32)
            b = b_ref[0, r, :].astype(f32)
            sig = 1.0 / (1.0 + jnp.exp(-a))
            silu = a * sig
            da_ref[0, r, :] = (du * b * (sig * (1.0 + a * (1.0 - sig)))).astype(bf16)
            db_ref[0, r, :] = (du * silu).astype(bf16)
            u_s[r, :] = (silu * b).astype(bf16)
        pd = _dot_tn(u_s[...], dy)
        pg = _dot_tn(da_ref[0], hh)
        pu = _dot_tn(db_ref[0], hh)

        @pl.when(i == 0)
        def _():
            acc_d[...] = pd
            acc_g[...] = pg
            acc_u[...] = pu

        @pl.when(i != 0)
        def _():
            acc_d[...] += pd
            acc_g[...] += pg
            acc_u[...] += pu

        @pl.when(i == nt - 1)
        def _():
            dwd_ref[...] = acc_d[...].astype(bf16).reshape(dwd_ref.shape)
            dwg_ref[...] = acc_g[...].astype(bf16).reshape(dwg_ref.shape)
            dwu_ref[...] = acc_u[...].astype(bf16).reshape(dwu_ref.shape)

    tok = pl.BlockSpec((TM, D), lambda s, i: (i, 0))
    hid = pl.BlockSpec((1, TM, FS), lambda s, i: (s, i, 0))
    wsp = pl.BlockSpec((1, 2, FS // 2, D), lambda s, i: (s, 0, 0, 0))
    hidden = _sds((NSH, S, FS), bf16)
    wgrad = _sds((NSH, 2, FS // 2, D), bf16)
    dspec, dop = _dep(dep)
    return pl.pallas_call(
        body, out_shape=(hidden, hidden, wgrad, wgrad, wgrad), grid=(NSH, nt),
        in_specs=[tok, tok, hid, hid, pl.BlockSpec((1, FS, D), lambda s, i: (s, 0, 0))] + dspec,
        out_specs=(hid, hid, wsp, wsp, wsp),
        scratch_shapes=[pltpu.VMEM((FS, D), f32)] * 3 + [pltpu.VMEM((TM, FS), f32), pltpu.VMEM((TM, FS), bf16)], name="ffn_bwd_mid",
        compiler_params=_cp("parallel", "arbitrary"))(dx, h, a, b, wd, *dop)


def _norm_bwd_tail(acc, x_ref, dxin_ref, g_ref, dxo_ref, dg_ref, first):
    xf = x_ref[...]
    r = lax.rsqrt(jnp.mean(xf * xf, axis=-1, keepdims=True) + NORM_EPS)
    xhat = xf * r
    dhg = acc * g_ref[...]
    dxo_ref[...] = dxin_ref[...] + r * (dhg - xhat * jnp.mean(dhg * xhat, axis=-1, keepdims=True))
    part = jnp.sum(acc * xhat, axis=0, keepdims=True)

    @pl.when(first)
    def _():
        dg_ref[...] = part

    @pl.when(jnp.logical_not(first))
    def _():
        dg_ref[...] += part


def _ffn_bwd_dx(dx, x_in, g, da, db, wg, wu):
    def body(dx_ref, x_ref, g_ref, da_ref, db_ref, wg_hbm, wu_hbm, dxo_ref, dg_ref, wg_v, wu_v, sems):
        wait = _resident((wg_hbm, wu_hbm), (wg_v, wu_v), sems)
        acc = None
        for s in range(NSH):
            wait(s)
            p = _dot(da_ref[s], wg_v[s])
            acc = p if acc is None else acc + p
            acc = acc + _dot(db_ref[s], wu_v[s])
        _norm_bwd_tail(acc, x_ref, dx_ref, g_ref, dxo_ref, dg_ref, pl.program_id(0) == 0)

    tok = pl.BlockSpec((TM, D), lambda i: (i, 0))
    vec = pl.BlockSpec((1, D), lambda i: (0, 0))
    hid = pl.BlockSpec((NSH, TM, FS), lambda i: (0, i, 0))
    wscr = pltpu.VMEM((NSH, FS, D), bf16)
    return pl.pallas_call(
        body, out_shape=(_sds((S, D), f32), _sds((1, D), f32)), grid=(S // TM,),
        in_specs=[tok, tok, vec, hid, hid, ANY, ANY], out_specs=(tok, vec),
        scratch_shapes=[wscr, wscr, pltpu.SemaphoreType.DMA((2, NSH))],
        name="ffn_bwd_dx", compiler_params=_cp("arbitrary"))(dx, x_in, g, da, db, wg, wu)


def _in_proj_bwd_dx(dx, x_in, g, dproj, wi):
    def body(dx_ref, x_ref, g_ref, dp_ref, w_hbm, dxo_ref, dg_ref, w_v, sems):
        wait = _resident((w_hbm,), (w_v,), sems)
        acc = None
        for s in range(NSH):
            wait(s)
            p = _dot_nt(dp_ref[:, PS * s:PS * (s + 1)], w_v[s])
            acc = p if acc is None else acc + p
        _norm_bwd_tail(acc, x_ref, dx_ref, g_ref, dxo_ref, dg_ref, pl.program_id(0) == 0)

    tok = pl.BlockSpec((TM, D), lambda i: (i, 0))
    vec = pl.BlockSpec((1, D), lambda i: (0, 0))
    return pl.pallas_call(
        body, out_shape=(_sds((S, D), f32), _sds((1, D), f32)), grid=(S // TM,),
        in_specs=[tok, tok, vec, pl.BlockSpec((TM, PROJ), lambda i: (i, 0)), ANY], out_specs=(tok, vec),
        scratch_shapes=[pltpu.VMEM((NSH, D, PS), bf16), pltpu.SemaphoreType.DMA((1, NSH))],
        name="in_proj_bwd_dx", compiler_params=_cp("arbitrary"))(dx, x_in, g, dproj, wi)


def _dw(lhs, rhs, lhs_spec, rhs_spec, rows, cols, name, cast_rhs=False):
    def body(l_ref, r_ref, o_ref):
        r = r_ref[...].astype(bf16) if cast_rhs else r_ref[...]
        o_ref[...] = _dot_tn(l_ref[...], r).astype(bf16).reshape(1, 2, rows // 2, cols)

    return pl.pallas_call(
        body, out_shape=_sds((NSH, 2, rows // 2, cols), bf16), grid=(NSH,), in_specs=[lhs_spec, rhs_spec],
        out_specs=pl.BlockSpec((1, 2, rows // 2, cols), lambda s: (s, 0, 0, 0)), name=name, compiler_params=_cp("parallel"))(lhs, rhs)


_WHOLE_TOK = pl.BlockSpec((S, D), lambda s: (0, 0))


def _dw_in(h, dproj):
    return _dw(h, dproj, _WHOLE_TOK, pl.BlockSpec((S, PS), lambda s: (0, s)), D, PS, "dw_in")


def _dw_out(mixed, dx):
    return _dw(mixed, dx, pl.BlockSpec((S, PW), lambda s: (0, s)), _WHOLE_TOK, PW, D, "dw_out", cast_rhs=True)


def _final_loss(x, g, target):
    def body(x_ref, g_ref, t_ref, loss_ref, dx_ref, dg_ref):
        i = pl.program_id(0)
        xf = x_ref[...]
        r = lax.rsqrt(jnp.mean(xf * xf, axis=-1, keepdims=True) + NORM_EPS)
        xhat = xf * r
        err = xhat * g_ref[...] - t_ref[...]
        dy = err * (1.0 / D)
        dhg = dy * g_ref[...]
        dx_ref[...] = r * (dhg - xhat * jnp.mean(dhg * xhat, axis=-1, keepdims=True))
        part = jnp.sum(dy * xhat, axis=0, keepdims=True)
        lpart = jnp.zeros((8, 128), f32) + 0.5 * jnp.sum(jnp.mean(err * err, axis=-1, keepdims=True))

        @pl.when(i == 0)
        def _():
            dg_ref[...] = part
            loss_ref[...] = lpart

        @pl.when(i != 0)
        def _():
            dg_ref[...] += part
            loss_ref[...] += lpart

    tok = pl.BlockSpec((TM, D), lambda i: (i, 0))
    vec = pl.BlockSpec((1, D), lambda i: (0, 0))
    return pl.pallas_call(
        body, out_shape=(_sds((8, 128), f32), _sds((S, D), f32), _sds((1, D), f32)), grid=(S // TM,),
        in_specs=[tok, vec, tok], out_specs=(pl.BlockSpec((8, 128), lambda i: (0, 0)), tok, vec),
        name="final_loss", compiler_params=_cp("arbitrary"))(x, g, target)


def _shift_down(x, k, row):
    return jnp.where(row >= k, pltpu.roll(x, k, axis=0), 0.0)


def _shift_up(x, k, row):
    return jnp.where(row < S - k, pltpu.roll(x, S - k, axis=0), 0.0)


def _pool_geometry():
    row = lax.broadcasted_iota(jnp.int32, (S, PW), 0)
    grp = lax.broadcasted_iota(jnp.int32, (S, PW), 1) // 64
    half = jnp.where(grp == 0, 1, jnp.where(grp == 1, 2, jnp.where(grp == 2, 4, 8)))
    hi = jnp.minimum(row + half - 1, S - 1)
    lo = jnp.maximum(row - half, 0)
    return row, grp, (hi - lo + 1).astype(f32)


def _by_group(grp, v0, v1, v2, v3):
    return jnp.where(grp == 0, v0, jnp.where(grp == 1, v1, jnp.where(grp == 2, v2, v3)))


def _window_sums(x, row, grp, transpose):
    l1, r1 = x, x
    l2, r2 = l1 + _shift_down(l1, 1, row), r1 + _shift_up(r1, 1, row)
    l4, r4 = l2 + _shift_down(l2, 2, row), r2 + _shift_up(r2, 2, row)
    l8, r8 = l4 + _shift_down(l4, 4, row), r4 + _shift_up(r4, 4, row)
    lsel = _by_group(grp, l1, l2, l4, l8)
    rsel = _by_group(grp, r1, r2, r4, r8)
    if transpose:
        return lsel + _shift_up(rsel, 1, row)
    return _shift_down(lsel, 1, row) + rsel


def _pool_fwd(proj, wbd, scale):
    def body(v_ref, w_ref, sc_ref, mixed_ref, diff_ref):
        row, grp, cnt = _pool_geometry()
        v = v_ref[...]
        diff = (_window_sums(v, row, grp, False) / cnt - v).astype(bf16)
        diff_ref[...] = diff
        mixed_ref[...] = (_dot(diff, w_ref[...].astype(bf16)) * sc_ref[...]).astype(bf16)

    col = pl.BlockSpec((S, PW), lambda i: (0, 0))
    return pl.pallas_call(
        body, out_shape=(_sds((S, D), bf16), _sds((S, PW), bf16)), grid=(1,),
        in_specs=[col, pl.BlockSpec((PW, PW), lambda i: (0, 0)), pl.BlockSpec((1, PW), lambda i: (0, 0))],
        out_specs=(col, col), name="pool_fwd", compiler_params=_cp("arbitrary"))(proj, wbd, scale)


def _pool_bwd(dmixed, diff, wbd, scale, dproj):
    def body(dy_ref, diff_ref, w_ref, sc_ref, dproj_in, dv_ref, dw_ref, dsc_ref):
        del dproj_in
        row, grp, cnt = _pool_geometry()
        dy = dy_ref[...]
        diff = diff_ref[...]
        w = w_ref[...].astype(bf16)
        dsc_ref[...] = jnp.sum(dy * _dot(diff, w), axis=0, keepdims=True)
        dys = (dy * sc_ref[...]).astype(bf16)
        dw_ref[...] = _dot_tn(diff, dys)
        ddiff = _dot_nt(dys, w)
        dv_ref[...] = (_window_sums(ddiff / cnt, row, grp, True) - ddiff).astype(bf16)

    col = pl.BlockSpec((S, PW), lambda i: (0, 0))
    return pl.pallas_call(
        body, out_shape=(_sds((S, PROJ), bf16), _sds((PW, PW), f32), _sds((1, PW), f32)), grid=(1,),
        in_specs=[col, col, pl.BlockSpec((PW, PW), lambda i: (0, 0)), pl.BlockSpec((1, PW), lambda i: (0, 0)), ANY],
        out_specs=(col, pl.BlockSpec((PW, PW), lambda i: (0, 0)), pl.BlockSpec((1, PW), lambda i: (0, 0))),
        input_output_aliases={4: 0}, name="pool_bwd", compiler_params=_cp("arbitrary"))(dmixed, diff, wbd, scale, dproj)


def _rope_tables(pos_col, freq_row):
    def body(p_ref, f_ref, c_ref, a_ref, b_ref):
        ang = p_ref[...].astype(f32) * f_ref[...]
        l64 = lax.broadcasted_iota(jnp.int32, (S, 128), 1) % 64
        cos, sin = jnp.cos(ang), jnp.sin(ang)
        c_ref[...] = jnp.where(l64 < 16, cos, 1.0)
        a_ref[...] = jnp.where(l64 < 8, -sin, 0.0)
        b_ref[...] = jnp.where((l64 >= 8) & (l64 < 16), sin, 0.0)

    t = _sds((S, 128), f32)
    return pl.pallas_call(body, out_shape=(t, t, t), name="rope_tables", compiler_params=_cp())(pos_col, freq_row)


def _rope(t, c, a, b):
    return t * c + pltpu.roll(t, 120, axis=1) * a + pltpu.roll(t, 8, axis=1) * b


def _rope_bwd(g, c, a, b):
    return g * c + pltpu.roll(g * a, 8, axis=1) + pltpu.roll(g * b, 120, axis=1)


def _perm_load(ref, d):
    if d == 1:
        return ref[...]
    n = S // d
    return jnp.concatenate([ref[pl.ds(r, n, stride=d), :] for r in range(d)], axis=0)


def _unperm_store(ref, val, d):
    if d == 1:
        ref[...] = val
        return
    n = S // d
    for r in range(d):
        ref[pl.ds(r, n, stride=d), :] = val[r * n:(r + 1) * n, :]


def _band(xp):
    z = jnp.zeros((64, 128), bf16)
    p = jnp.concatenate([z, xp, z], axis=0).reshape(NBLK + 1, QBLK, 128)
    return jnp.concatenate([p[:NBLK], p[1:]], axis=1)


def _unband(xb):
    z = jnp.zeros((1, QBLK, 128), f32)
    p = jnp.concatenate([xb[:, :QBLK], z], axis=0) + jnp.concatenate([z, xb[:, QBLK:]], axis=0)
    return p.reshape(S + QBLK, 128)[64:S + 64]


def _band_mask(d):
    blocks_per_class = NBLK // d
    n = lax.broadcasted_iota(jnp.int32, (NBLK, 1, 2 * QBLK), 0) & (blocks_per_class - 1)
    be = lax.broadcasted_iota(jnp.int32, (NBLK, 1, 2 * QBLK), 2)
    a = lax.broadcasted_iota(jnp.int32, (1, 2 * QBLK, 2 * QBLK), 1) & (QBLK - 1)
    b = lax.broadcasted_iota(jnp.int32, (1, 2 * QBLK, 2 * QBLK), 2)
    band = (b >= a) & (b <= a + 128)
    edge = ((be >= 64) | (n != 0)) & ((be < QBLK + 64) | (n != blocks_per_class - 1))
    return band & edge


def _stack_heads(xb, lo):
    z = jnp.zeros_like(xb)
    return jnp.concatenate([jnp.where(lo, xb, z), jnp.where(lo, z, xb)], axis=1)


def _unstack_heads(x2, lo):
    return jnp.where(lo, x2[:, :QBLK], x2[:, QBLK:])


def _rows_to_lanes(col2, lo):
    return jnp.where(lo, jnp.broadcast_to(col2[:, :QBLK], (NBLK, QBLK, 128)), jnp.broadcast_to(col2[:, QBLK:], (NBLK, QBLK, 128)))


def _bmm_nt(a, b):
    return jnp.einsum('nqd,nkd->nqk', a, b, preferred_element_type=f32)


def _bmm_nn(a, b):
    return jnp.einsum('nqk,nkd->nqd', a, b, preferred_element_type=f32)


def _bmm_tn(a, b):
    return jnp.einsum('nqk,nqd->nkd', a, b, preferred_element_type=f32)


def _attn_fwd(proj, tc, ta, tb, mixed):
    def body(q_ref, k_ref, v_ref, c_ref, a_ref, b_ref, mixed_in, mixed_ref, o_ref, lse_ref, qn, kn, t_num, t_m, t_den):
        del mixed_in
        lo = lax.broadcasted_iota(jnp.int32, (1, 1, 128), 2) < 64
        c, a, b = c_ref[...], a_ref[...], b_ref[...]
        qn[...] = _rope(q_ref[...], c, a, b)
        kn[...] = _rope(k_ref[...], c, a, b)
        run = None
        for d in DILATIONS:
            q2 = _stack_heads(_perm_load(qn, d).astype(bf16).reshape(NBLK, QBLK, 128), lo)
            kb = _band(_perm_load(kn, d).astype(bf16))
            vb = _band(_perm_load(v_ref, d).astype(bf16))
            s = jnp.where(_band_mask(d), _bmm_nt(q2, kb) * 0.125, MASK_VALUE)
            m = jnp.max(s, axis=2, keepdims=True)
            p = jnp.exp(s - m)
            den = jnp.sum(p, axis=2, keepdims=True)
            num = _unstack_heads(_bmm_nn(p.astype(bf16), vb), lo)
            _unperm_store(t_num, num.reshape(S, 128), d)
            _unperm_store(t_m, _rows_to_lanes(m, lo).reshape(S, 128), d)
            _unperm_store(t_den, _rows_to_lanes(den, lo).reshape(S, 128), d)
            if run is None:
                run = (t_m[...], t_num[...], t_den[...])
            else:
                m_new = jnp.maximum(run[0], t_m[...])
                w_old, w_new = jnp.exp(run[0] - m_new), jnp.exp(t_m[...] - m_new)
                run = (m_new, w_old * run[1] + w_new * t_num[...], w_old * run[2] + w_new * t_den[...])
        out = run[1] / run[2]
        o_ref[...] = out
        mixed_ref[...] = out.astype(bf16)
        lse_ref[...] = run[0] + jnp.log(run[2])

    def col(off):
        return pl.BlockSpec((S, 128), lambda j, off=off: (0, off + j))

    tab = pl.BlockSpec((S, 128), lambda j: (0, 0))
    scr = pltpu.VMEM((S, 128), f32)
    return pl.pallas_call(
        body, out_shape=(_sds((S, D), bf16), _sds((S, AW), f32), _sds((S, AW), f32)), grid=(NPAIR,),
        in_specs=[col(2), col(8), col(14), tab, tab, tab, ANY], out_specs=(col(2), col(0), col(0)),
        scratch_shapes=[scr, scr, scr, scr, scr], input_output_aliases={6: 0}, name="attn_fwd",
        compiler_params=_cp("arbitrary"))(proj, proj, proj, tc, ta, tb, mixed)


def _attn_bwd(proj, tc, ta, tb, o, lse, dmixed):
    def body(q_ref, k_ref, v_ref, c_ref, a_ref, b_ref, o_ref, lse_ref, do_ref, dp_ref, qn, kn, tmp, dk_s, dv_s):
        t = pl.program_id(1)

        @pl.when(t == 0)
        def _():
            lo = lax.broadcasted_iota(jnp.int32, (1, 1, 128), 2) < 64
            c, a, b = c_ref[...], a_ref[...], b_ref[...]
            qn[...] = _rope(q_ref[...], c, a, b)
            kn[...] = _rope(k_ref[...], c, a, b)
            dq = dk = dv = None
            for d in DILATIONS:
                q2 = _stack_heads(_perm_load(qn, d).astype(bf16).reshape(NBLK, QBLK, 128), lo)
                kb = _band(_perm_load(kn, d).astype(bf16))
                vb = _band(_perm_load(v_ref, d).astype(bf16))
                dob = _perm_load(do_ref, d).reshape(NBLK, QBLK, 128)
                ob = _perm_load(o_ref, d).reshape(NBLK, QBLK, 128)
                lsb = _perm_load(lse_ref, d).reshape(NBLK, QBLK, 128)
                do2 = _stack_heads(dob.astype(bf16), lo)
                delta2 = jnp.sum(_stack_heads(dob * ob, lo), axis=2, keepdims=True)
                lse2 = jnp.max(jnp.concatenate([jnp.where(lo, lsb, MASK_VALUE), jnp.where(lo, MASK_VALUE, lsb)], axis=1),
                               axis=2, keepdims=True)
                s = _bmm_nt(q2, kb) * 0.125
                p = jnp.where(_band_mask(d), jnp.exp(s - lse2), 0.0)
                ds = (p * (_bmm_nt(do2, vb) - delta2) * 0.125).astype(bf16)
                pb = p.astype(bf16)
                dq_b = _unstack_heads(_bmm_nn(ds, kb), lo).reshape(S, 128)
                dk_b = _unband(_bmm_tn(ds, q2))
                dv_b = _unband(_bmm_tn(pb, do2))
                acc = []
                for prev, new in ((dq, dq_b), (dk, dk_b), (dv, dv_b)):
                    _unperm_store(tmp, new, d)
                    acc.append(tmp[...] if prev is None else prev + tmp[...])
                dq, dk, dv = acc
            dp_ref[...] = _rope_bwd(dq, c, a, b).astype(bf16)
            dk_s[...] = _rope_bwd(dk, c, a, b).astype(bf16)
            dv_s[...] = dv.astype(bf16)

        @pl.when(t == 1)
        def _():
            dp_ref[...] = dk_s[...]

        @pl.when(t == 2)
        def _():
            dp_ref[...] = dv_s[...]

    def col(off):
        return pl.BlockSpec((S, 128), lambda j, t, off=off: (0, off + j))

    tab = pl.BlockSpec((S, 128), lambda j, t: (0, 0))
    scr = pltpu.VMEM((S, 128), f32)
    scb = pltpu.VMEM((S, 128), bf16)
    return pl.pallas_call(
        body, out_shape=_sds((S, PROJ), bf16), grid=(NPAIR, 3),
        in_specs=[col(2), col(8), col(14), tab, tab, tab, col(0), col(0), col(2)],
        out_specs=pl.BlockSpec((S, 128), lambda j, t: (0, 2 + NPAIR * t + j)),
        scratch_shapes=[scr, scr, scr, scb, scb], name="attn_bwd",
        compiler_params=_cp("arbitrary", "arbitrary"))(proj, proj, proj, tc, ta, tb, o, lse, dmixed)


def _block_diag(w4):
    out = jnp.zeros((PW, PW), w4.dtype)
    for g in range(4):
        out = out.at[64 * g:64 * (g + 1), 64 * g:64 * (g + 1)].set(w4[g])
    return out


def _diag_blocks(w):
    return jnp.stack([w[64 * g:64 * (g + 1), 64 * g:64 * (g + 1)] for g in range(4)])


def _rope_inputs(positions):
    inv_freq = ROPE_THETA ** (-jnp.arange(0, 16, 2, dtype=f32) / 16)
    l64 = np.arange(128) % 64
    idx = np.where(l64 < 16, l64 % 8, 0)
    return positions.reshape(S, 1), inv_freq[idx].reshape(1, 128)


def _layer_fwd(x, w, small, l, tabs, dep=None):
    g1, gm, g2 = (small[k][l].reshape(1, D) for k in ("ffn1_norm", "mix_norm", "ffn2_norm"))
    wbd = _block_diag(small["pool_w"][l])
    psc = small["pool_scale"][l].reshape(1, PW)
    x1, h1, a1, b1 = _ffn_fwd(x, g1, w["g1"], w["u1"], w["d1"], dep)
    proj, h2 = _in_proj(x1, gm, w["wi"])
    mixed, diff = _pool_fwd(proj, wbd, psc)
    mixed, o, lse = _attn_fwd(proj, *tabs, mixed)
    x2 = _out_proj(x1, mixed, w["wo"])
    out, h3, a2, b2 = _ffn_fwd(x2, g2, w["g2"], w["u2"], w["d2"])
    return out, dict(x0=x, h1=h1, a1=a1, b1=b1, x1=x1, h2=h2, proj=proj, mixed=mixed, diff=diff, o=o, lse=lse,
                     x2=x2, h3=h3, a2=a2, b2=b2, g1=g1, gm=gm, g2=g2, wbd=wbd, psc=psc)


def _layer_bwd(dx, w, sv, tabs, dep=None):
    gr, sg = {}, {}
    da, db, gr["d2"], gr["g2"], gr["u2"] = _ffn_bwd_mid(dx, sv["h3"], sv["a2"], sv["b2"], w["d2"], dep)
    dx, sg["ffn2_norm"] = _ffn_bwd_dx(dx, sv["x2"], sv["g2"], da, db, w["g2"], w["u2"])
    gr["wo"] = _dw_out(sv["mixed"], dx)
    dmixed = _out_proj_bwd(dx, w["wo"])
    dproj = _attn_bwd(sv["proj"], *tabs, sv["o"], sv["lse"], dmixed)
    dproj, dwbd, sg["pool_scale"] = _pool_bwd(dmixed, sv["diff"], sv["wbd"], sv["psc"], dproj)
    sg["pool_w"] = _diag_blocks(dwbd)
    gr["wi"] = _dw_in(sv["h2"], dproj)
    dx, sg["mix_norm"] = _in_proj_bwd_dx(dx, sv["x1"], sv["gm"], dproj, w["wi"])
    da, db, gr["d1"], gr["g1"], gr["u1"] = _ffn_bwd_mid(dx, sv["h1"], sv["a1"], sv["b1"], w["d1"])
    dx, sg["ffn1_norm"] = _ffn_bwd_dx(dx, sv["x0"], sv["g1"], da, db, w["g1"], w["u1"])
    return dx, gr, sg


def _forward_backward(x, positions, target, gathered, small):
    tabs = _rope_tables(*_rope_inputs(positions))
    saved = []
    for l in range(DEPTH):
        x, sv = _layer_fwd(x, gathered[l], small, l, tabs)
        saved.append(sv)
    loss, dx, dgf = _final_loss(x, small["final_norm"].reshape(1, D), target)
    big = [None] * DEPTH
    sg = {k: [None] * DEPTH for k in ("ffn1_norm", "mix_norm", "pool_w", "pool_scale", "ffn2_norm")}
    for l in reversed(range(DEPTH)):
        dx, big[l], sgl = _layer_bwd(dx, gathered[l], saved[l], tabs)
        for k, v in sgl.items():
            sg[k][l] = v
    sg["final_norm"] = dgf
    return loss, dx, big, sg


def _place():
    x, y, c = lax.axis_index("x"), lax.axis_index("y"), lax.axis_index("c")
    chips = [(1 - x, y), (x, 1 - y), (1 - x, 1 - y)]
    return x, y, c, chips


def _cast_layer(params, l, place):
    def body(p_ref, *refs):
        del p_ref
        for i_ref, o_ref in zip(refs[:8], refs[8:]):
            o_ref[...] = i_ref[...].astype(bf16).reshape(o_ref.shape)

    ins, in_specs, out_shape, out_specs = [], [], [], []
    for name, rows, cols in BIG:
        q = rows // 4
        ins.append(params[BIG_SRC[name]])
        in_specs.append(pl.BlockSpec((1, q, cols), lambda i, p, l=l: (l, i, 0)))
        out_shape.append(_sds((NSH, 2, rows // 2, cols), bf16))
        out_specs.append(pl.BlockSpec((1, 1, q, cols), lambda i, p: (p[1], i // 2, i % 2, 0)))
    return pl.pallas_call(
        body, out_shape=out_shape,
        grid_spec=pltpu.PrefetchScalarGridSpec(num_scalar_prefetch=1, grid=(4,), in_specs=in_specs, out_specs=out_specs),
        name=f"cast_layer{l}", compiler_params=_cp("parallel"))(place, *ins)


HBM = pl.BlockSpec(memory_space=pltpu.HBM)
SEM = pl.BlockSpec(memory_space=pltpu.SEMAPHORE)
_SPLIT = pltpu.CompilerParams(has_side_effects=pltpu.SideEffectType.DATAFLOW_SIDE_EFFECTING)


def _hbm(arrays):
    return [pltpu.with_memory_space_constraint(a, pltpu.HBM) for a in arrays]


def _chip_copies(src_of, dst_of, send_sems, recv_sems, n):
    x, y, c, chips = _place()
    me = 2 * x + y
    out = []
    for t in range(n):
        for k, chip in enumerate(chips):
            peer = 2 * chip[0] + chip[1]
            send = pltpu.make_async_remote_copy(
                src_ref=src_of(t, peer), dst_ref=dst_of(t, me), send_sem=send_sems.at[3 * t + k], recv_sem=recv_sems.at[3 * t + k],
                device_id=(chip[0], chip[1], c), device_id_type=MESH)
            land = pltpu.make_async_remote_copy(
                src_ref=src_of(t, peer), dst_ref=dst_of(t, peer), send_sem=send_sems.at[3 * t + k], recv_sem=recv_sems.at[3 * t + k],
                device_id=(chip[0], chip[1], c), device_id_type=MESH)
            out.append((send, land))
    return out


def _exchange_start(src, land, after, src_of, dst_of, name):
    n, m = len(src), len(src) + len(land)

    def body(*refs):
        src_refs = refs[:n]
        land_refs = refs[n:m] if land else src_refs
        send_sems, recv_sems = refs[m + 1], refs[m + 2]
        token = refs[-1]
        for send, _ in _chip_copies(lambda t, s: src_of(src_refs[t], s), lambda t, s: dst_of(land_refs[t], s), send_sems, recv_sems, n):
            send.start()
        token[...] = jnp.zeros_like(token)

    arrays = list(src) + list(land)
    out_shape = ([pltpu.SemaphoreType.DMA((3 * n,)), pltpu.SemaphoreType.DMA((3 * n,))] + [pltpu.HBM(a.shape, a.dtype) for a in arrays]
                 + [_sds((8, 128), f32)])
    res = pl.pallas_call(
        body, out_shape=out_shape, in_specs=[HBM] * m + [ANY], out_specs=[SEM, SEM] + [HBM] * m + [pl.BlockSpec(memory_space=pltpu.VMEM)],
        input_output_aliases={i: 2 + i for i in range(m)}, name=name, compiler_params=_SPLIT)(*_hbm(arrays), after)
    return res[0], res[1], list(res[2:2 + n]), list(res[2 + n:2 + m]), res[-1]


def _exchange_wait(send_sems, recv_sems, src, land, after, src_of, dst_of, name):
    n, m = len(src), len(src) + len(land)

    def body(*refs):
        src_refs = refs[:n]
        land_refs = refs[n:m] if land else src_refs
        send_sems, recv_sems = refs[m], refs[m + 1]
        for send, land_cp in _chip_copies(lambda t, s: src_of(src_refs[t], s), lambda t, s: dst_of(land_refs[t], s), send_sems, recv_sems, n):
            send.wait_send()
            land_cp.wait_recv()

    arrays = list(src) + list(land)
    res = pl.pallas_call(
        body, out_shape=[pltpu.HBM(a.shape, a.dtype) for a in arrays], in_specs=[HBM] * m + [SEM, SEM, ANY], out_specs=[HBM] * m,
        input_output_aliases={i: i for i in range(m)}, name=name, compiler_params=_SPLIT)(*arrays, send_sems, recv_sems, after)
    return list(res[:n]), list(res[n:])


def _own_half(ref, s):
    x, y, c, _ = _place()
    return ref.at[2 * x + y, c]


def _slot_half(ref, s):
    return ref.at[s, lax.axis_index("c")]


def _slot(ref, s):
    return ref.at[s]


def _gather_forward(bufs):
    n = len(bufs)

    def body(*refs):
        outs = refs[n:2 * n]
        send_sems, recv_sems = refs[2 * n:]
        x, y, c, chips = _place()
        sibling = (x, y, 1 - c)
        passed = []
        for t in range(n):
            for k, chip in enumerate(chips):
                blk = outs[t].at[2 * chip[0] + chip[1], c]
                cp = pltpu.make_async_remote_copy(
                    src_ref=blk, dst_ref=blk, send_sem=send_sems.at[t, k], recv_sem=recv_sems.at[t, k],
                    device_id=sibling, device_id_type=MESH)
                cp.start()
                passed.append(cp)
        for t in range(n):
            for k, chip in enumerate(chips):
                blk = outs[t].at[2 * chip[0] + chip[1], 1 - c]
                pltpu.make_async_remote_copy(
                    src_ref=blk, dst_ref=blk, send_sem=send_sems.at[t, k], recv_sem=recv_sems.at[t, k],
                    device_id=sibling, device_id_type=MESH).wait_recv()
        for cp in passed:
            cp.wait_send()

    out_shape = [_sds(a.shape, bf16) for a in bufs]
    return pl.pallas_call(
        body, out_shape=out_shape, in_specs=[ANY] * n, out_specs=[ANY] * n, input_output_aliases={t: t for t in range(n)},
        scratch_shapes=[pltpu.SemaphoreType.DMA((n, 3)), pltpu.SemaphoreType.DMA((n, 3))], name="gather_forward")(*bufs)


def _sibling_swap(grads):
    n = len(grads)

    def body(*refs):
        ins, outs = refs[:n], refs[n:2 * n]
        send_sems, recv_sems = refs[2 * n:]
        x, y, c, _ = _place()
        cps = []
        for t in range(n):
            for s in range(NSH):
                cp = pltpu.make_async_remote_copy(
                    src_ref=ins[t].at[s, 1 - c], dst_ref=outs[t].at[s], send_sem=send_sems.at[t, s], recv_sem=recv_sems.at[t, s],
                    device_id=(x, y, 1 - c), device_id_type=MESH)
                cp.start()
                cps.append(cp)
        for cp in cps:
            cp.wait()

    out_shape = [_sds((NSH,) + a.shape[2:], bf16) for a in grads]
    return pl.pallas_call(
        body, out_shape=out_shape, in_specs=[ANY] * n, out_specs=[ANY] * n,
        scratch_shapes=[pltpu.SemaphoreType.DMA((n, NSH)), pltpu.SemaphoreType.DMA((n, NSH))],
        name="sibling_swap")(*grads)


def _row_tile(h):
    return h // 2 if h % 32 == 0 else h


def _pair_sum(grads, got, c_idx):
    n = len(grads)

    def body(c_ref, *refs):
        del c_ref
        for t in range(n):
            refs[2 * n + t][...] = (refs[t][...].astype(f32).reshape(refs[n + t].shape) + refs[n + t][...].astype(f32)).astype(bf16)

    in_specs, out_shape, out_specs = [], [], []
    for a in grads:
        h, cols = a.shape[2:]
        in_specs.append(pl.BlockSpec((1, 1, _row_tile(h), cols), lambda s, i, c: (s, c[0], i, 0)))
    for a in grads:
        h, cols = a.shape[2:]
        in_specs.append(pl.BlockSpec((1, _row_tile(h), cols), lambda s, i, c: (s, i, 0)))
        out_shape.append(_sds((NSH, h, cols), bf16))
        out_specs.append(pl.BlockSpec((1, _row_tile(h), cols), lambda s, i, c: (s, i, 0)))
    return pl.pallas_call(
        body, out_shape=out_shape,
        grid_spec=pltpu.PrefetchScalarGridSpec(num_scalar_prefetch=1, grid=(NSH, 2), in_specs=in_specs, out_specs=out_specs),
        name="pair_sum", compiler_params=_cp("parallel", "parallel"))(c_idx, *grads, *got)


def _chip_sum(psum, parts, full, place, l, name):
    n = len(parts)

    def body(p_ref, *refs):
        s = pl.program_id(1)
        for t in range(n):
            val = jnp.where(s == p_ref[1], refs[t][0], refs[n + t][0]).astype(f32)
            out = refs[3 * n + t]

            @pl.when(s == 0)
            def _(out=out, val=val):
                out[0, 0] = val

            @pl.when(s != 0)
            def _(out=out, val=val):
                out[0, 0] += val

    own_specs, part_specs, out_shape, out_specs = [], [], [], []
    for a, fl in zip(parts, full):
        _, h, cols = a.shape
        r = _row_tile(h)
        own_specs.append(pl.BlockSpec((1, r, cols), lambda i, s, p: (p[1], i, 0)))
        part_specs.append(pl.BlockSpec((1, r, cols), lambda i, s, p: (jnp.where(s == p[1], (s + 1) % NSH, s), i, 0)))
        out_shape.append(_sds(fl.shape, f32))
        out_specs.append(pl.BlockSpec((1, 1, r, cols), lambda i, s, p, l=l: (l, p[0], i, 0)))
    return pl.pallas_call(
        body, out_shape=out_shape,
        grid_spec=pltpu.PrefetchScalarGridSpec(num_scalar_prefetch=1, grid=(2, NSH), in_specs=own_specs + part_specs + [ANY] * n,
                                               out_specs=out_specs),
        input_output_aliases={1 + 2 * n + t: t for t in range(n)}, name=name,
        compiler_params=_cp("parallel", "arbitrary"))(place, *psum, *parts, *full)


def _sibling_share(full, l, name):
    n = len(full)

    def body(*refs):
        outs = refs[n:2 * n]
        send_sems, recv_sems = refs[2 * n:]
        x, y, c, _ = _place()
        sibling = (x, y, 1 - c)
        cps = []
        for t in range(n):
            blk = outs[t].at[l, c]
            cp = pltpu.make_async_remote_copy(
                src_ref=blk, dst_ref=blk, send_sem=send_sems.at[t], recv_sem=recv_sems.at[t], device_id=sibling, device_id_type=MESH)
            cp.start()
            cps.append(cp)
        for t in range(n):
            blk = outs[t].at[l, 1 - c]
            pltpu.make_async_remote_copy(
                src_ref=blk, dst_ref=blk, send_sem=send_sems.at[t], recv_sem=recv_sems.at[t],
                device_id=sibling, device_id_type=MESH).wait_recv()
        for cp in cps:
            cp.wait_send()

    out_shape = [_sds(a.shape, f32) for a in full]
    return pl.pallas_call(
        body, out_shape=out_shape, in_specs=[ANY] * n, out_specs=[ANY] * n, input_output_aliases={t: t for t in range(n)},
        scratch_shapes=[pltpu.SemaphoreType.DMA((n,)), pltpu.SemaphoreType.DMA((n,))], name=name)(*full)


SMALL_ROWS = 656


def _pack_small(per_layer, final_vec, loss_tile):
    rows = []
    for l in range(DEPTH):
        for k in ("ffn1_norm", "mix_norm", "ffn2_norm"):
            rows.append(per_layer[k][l].reshape(8, 128))
        rows.append(per_layer["pool_w"][l].reshape(128, 128))
        rows.append(jnp.pad(per_layer["pool_scale"][l].reshape(2, 128), ((0, 6), (0, 0))))
    rows.append(final_vec.reshape(8, 128))
    rows.append(loss_tile)
    return jnp.concatenate(rows, axis=0)


def _unpack_small(buf):
    out = {k: [] for k in ("ffn1_norm", "mix_norm", "ffn2_norm", "pool_w", "pool_scale")}
    r = 0
    for l in range(DEPTH):
        for k in ("ffn1_norm", "mix_norm", "ffn2_norm"):
            out[k].append(buf[r:r + 8].reshape(D))
            r += 8
        out["pool_w"].append(buf[r:r + 128].reshape(4, 64, 64))
        r += 128
        out["pool_scale"].append(buf[r:r + 2].reshape(PW))
        r += 8
    res = {k: jnp.stack(v) for k, v in out.items()}
    res["final_norm"] = buf[r:r + 8].reshape(D)
    res["loss"] = buf[r + 8, 0]
    return res


def _allreduce_small(buf):
    def body(in_ref, out_ref, slots, send_sems, recv_sems):
        x, y, c, _ = _place()
        me = 4 * x + 2 * y + c
        slots[me] = in_ref[...]
        peers = []
        for k in range(1, 8):
            px, py, pc = x ^ (k >> 2), y ^ ((k >> 1) & 1), c ^ (k & 1)
            cp = pltpu.make_async_remote_copy(
                src_ref=in_ref, dst_ref=slots.at[me], send_sem=send_sems.at[k - 1], recv_sem=recv_sems.at[k - 1],
                device_id=(px, py, pc), device_id_type=MESH)
            cp.start()
            peers.append(cp)
        for k in range(1, 8):
            px, py, pc = x ^ (k >> 2), y ^ ((k >> 1) & 1), c ^ (k & 1)
            slot = 4 * px + 2 * py + pc
            pltpu.make_async_remote_copy(
                src_ref=slots.at[slot], dst_ref=slots.at[slot], send_sem=send_sems.at[k - 1], recv_sem=recv_sems.at[k - 1],
                device_id=(px, py, pc), device_id_type=MESH).wait_recv()
        for cp in peers:
            cp.wait_send()
        acc = slots[0]
        for j in range(1, 8):
            acc = acc + slots[j]
        out_ref[...] = acc

    return pl.pallas_call(
        body, out_shape=_sds((SMALL_ROWS, 128), f32),
        in_specs=[pl.BlockSpec(memory_space=pltpu.VMEM)], out_specs=pl.BlockSpec(memory_space=pltpu.VMEM),
        scratch_shapes=[pltpu.VMEM((8, SMALL_ROWS, 128), f32), pltpu.SemaphoreType.DMA((7,)), pltpu.SemaphoreType.DMA((7,))],
        name="allreduce_small", compiler_params=_cp())(buf)


def _adamw_math(w, g, m, v):
    m = ADAM_B1 * m + (1.0 - ADAM_B1) * g
    v = ADAM_B2 * v + (1.0 - ADAM_B2) * (g * g)
    m_hat = m / (1.0 - ADAM_B1 ** ADAM_STEP)
    v_hat = v / (1.0 - ADAM_B2 ** ADAM_STEP)
    return -ADAM_LR * (m_hat / (jnp.sqrt(v_hat) + ADAM_EPS) + ADAM_WD * w), m, v


def _adamw(w, g, m, v, name, first=0, prev=None, dep=None):
    def body(w_ref, g_ref, m_ref, v_ref, *rest):
        go_ref, d_ref, mo_ref, vo_ref = rest[-4:]
        g = g_ref[...]
        d, mn, vn = _adamw_math(w_ref[...], g, m_ref[...], v_ref[...])
        go_ref[...] = g
        d_ref[...] = d
        mo_ref[...] = mn
        vo_ref[...] = vn

    _, rows, cols = w.shape
    r = rows // 4 if rows % 32 == 0 else rows
    spec = pl.BlockSpec((1, r, cols), lambda i, j: (first + i, j, 0))
    gspec = pl.BlockSpec((1, r, cols), lambda i, j: (i, j, 0))
    out = _sds(w.shape, f32)
    extra = [] if prev is None else list(prev)
    dspec, dop = _dep(dep)
    return pl.pallas_call(
        body, out_shape=(out, out, out, out), grid=(g.shape[0], rows // r), in_specs=[spec, gspec, spec, spec] + [ANY] * len(extra) + dspec,
        out_specs=(spec,) * 4, input_output_aliases={4 + i: i for i in range(len(extra))}, name=name,
        compiler_params=_cp("parallel", "parallel"))(w, g, m, v, *extra, *dop)


SMALL_NAMES = ("ffn1_norm", "mix_norm", "pool_w", "pool_scale", "ffn2_norm", "final_norm")
WEIGHT_ORDER = ("ffn1_norm", "ffn1_w_gate", "ffn1_w_up", "ffn1_w_down", "mix_norm", "w_in", "pool_w", "pool_scale", "w_out",
                "ffn2_norm", "ffn2_w_gate", "ffn2_w_up", "ffn2_w_down", "final_norm")


def _pack_small_params(p):
    per_layer = {k: [p[k][l] for l in range(DEPTH)] for k in ("ffn1_norm", "mix_norm", "ffn2_norm", "pool_w", "pool_scale")}
    return _pack_small(per_layer, p["final_norm"], jnp.zeros((8, 128), f32))


def kernel(x, positions, ffn1_norm, ffn1_w_gate, ffn1_w_up, ffn1_w_down, mix_norm, w_in, pool_w, pool_scale, w_out, ffn2_norm, ffn2_w_gate, ffn2_w_up, ffn2_w_down, final_norm, loss_target, m_ffn1_norm, m_ffn1_w_gate, m_ffn1_w_up, m_ffn1_w_down, m_mix_norm, m_w_in, m_pool_w, m_pool_scale, m_w_out, m_ffn2_norm, m_ffn2_w_gate, m_ffn2_w_up, m_ffn2_w_down, m_final_norm, v_ffn1_norm, v_ffn1_w_gate, v_ffn1_w_up, v_ffn1_w_down, v_mix_norm, v_w_in, v_pool_w, v_pool_scale, v_w_out, v_ffn2_norm, v_ffn2_w_gate, v_ffn2_w_up, v_ffn2_w_down, v_final_norm):
    params = dict(ffn1_norm=ffn1_norm, ffn1_w_gate=ffn1_w_gate, ffn1_w_up=ffn1_w_up, ffn1_w_down=ffn1_w_down,
                  mix_norm=mix_norm, w_in=w_in, pool_w=pool_w, pool_scale=pool_scale, w_out=w_out, ffn2_norm=ffn2_norm,
                  ffn2_w_gate=ffn2_w_gate, ffn2_w_up=ffn2_w_up, ffn2_w_down=ffn2_w_down, final_norm=final_norm)
    mom_m = dict(ffn1_norm=m_ffn1_norm, ffn1_w_gate=m_ffn1_w_gate, ffn1_w_up=m_ffn1_w_up, ffn1_w_down=m_ffn1_w_down,
                 mix_norm=m_mix_norm, w_in=m_w_in, pool_w=m_pool_w, pool_scale=m_pool_scale, w_out=m_w_out,
                 ffn2_norm=m_ffn2_norm, ffn2_w_gate=m_ffn2_w_gate, ffn2_w_up=m_ffn2_w_up, ffn2_w_down=m_ffn2_w_down,
                 final_norm=m_final_norm)
    mom_v = dict(ffn1_norm=v_ffn1_norm, ffn1_w_gate=v_ffn1_w_gate, ffn1_w_up=v_ffn1_w_up, ffn1_w_down=v_ffn1_w_down,
                 mix_norm=v_mix_norm, w_in=v_w_in, pool_w=v_pool_w, pool_scale=v_pool_scale, w_out=v_w_out,
                 ffn2_norm=v_ffn2_norm, ffn2_w_gate=v_ffn2_w_gate, ffn2_w_up=v_ffn2_w_up, ffn2_w_down=v_ffn2_w_down,
                 final_norm=v_final_norm)
    names = [t[0] for t in BIG]
    for d in (params, mom_m, mom_v):
        for k in TRANSPOSED:
            d[k] = jnp.swapaxes(d[k], 1, 2)

    place = jnp.stack([lax.axis_index("c"), 2 * lax.axis_index("x") + lax.axis_index("y")]).astype(jnp.int32)
    def gather_start(l, after):
        return _exchange_start(_cast_layer(params, l, place), [], after, _own_half, _slot_half, f"gather_start{l}")

    def gather_end(started, after, l):
        send_sems, recv_sems, bufs, _, _ = started
        bufs, _ = _exchange_wait(send_sems, recv_sems, bufs, [], after, _own_half, _slot_half, f"gather_wait{l}")
        return {nm: a.reshape(NSH, rows, cols) for (nm, rows, cols), a in zip(BIG, _gather_forward(bufs))}

    tabs = _rope_tables(*_rope_inputs(positions))
    h = x.reshape(S, D)
    weights, saved = [], []
    started = gather_start(0, place)
    after = started[-1]
    for l in range(DEPTH):
        weights.append(gather_end(started, after, l))
        dep = None
        if l + 1 < DEPTH:
            started = gather_start(l + 1, weights[l]["g1"])
            dep = started[-1]
        h, sv = _layer_fwd(h, weights[l], params, l, tabs, dep)
        saved.append(sv)
        after = h
    loss, dx, dgf = _final_loss(h, final_norm.reshape(1, D), loss_target.reshape(S, D))

    upper = [lax.empty((DEPTH - 1, 2, rows // 2, cols), f32) for _, rows, cols in BIG]
    lower = [lax.empty((1, 2, rows // 2, cols), f32) for _, rows, cols in BIG]
    sg = {k: [None] * DEPTH for k in ("ffn1_norm", "mix_norm", "pool_w", "pool_scale", "ffn2_norm")}
    sg["final_norm"] = dgf

    def reduce_end(started, after, l, full, slot):
        send_sems, recv_sems, psum, parts, _ = started
        psum, parts = _exchange_wait(send_sems, recv_sems, psum, parts, after, _slot, _slot, f"grad_wait{l}")
        return _sibling_share(_chip_sum(psum, parts, full, place, slot, f"chip_sum{l}"), slot, f"sibling_share{l}")

    started, dep = None, None
    for l in reversed(range(DEPTH)):
        dx, gr, sgl = _layer_bwd(dx, weights[l], saved[l], tabs, dep)
        for k, v in sgl.items():
            sg[k][l] = v
        if started is not None:
            upper = reduce_end(started, dx, l + 1, upper, l)
        grads = [gr[nm] for nm in names]
        psum = _pair_sum(grads, _sibling_swap(grads), place)
        parts = [lax.empty(a.shape, bf16) for a in psum]
        started = _exchange_start(psum, parts, place, _slot, _slot, f"grad_start{l}")
        dep = started[-1]

    big_out = {}
    for (nm, rows, cols), g in zip(BIG, upper):
        k = BIG_SRC[nm]
        big_out[k] = _adamw(params[k], g.reshape(DEPTH - 1, rows, cols), mom_m[k], mom_v[k], "adamw_upper_" + k, first=1, dep=dep)
        dep = big_out[k][1]
    lower = reduce_end(started, dep, 0, lower, 0)
    for (nm, rows, cols), g in zip(BIG, lower):
        k = BIG_SRC[nm]
        big_out[k] = _adamw(params[k], g.reshape(1, rows, cols), mom_m[k], mom_v[k], "adamw_lower_" + k, first=0, prev=big_out[k])

    per_layer = {k: sg[k] for k in ("ffn1_norm", "mix_norm", "ffn2_norm", "pool_w", "pool_scale")}
    small_sum = _allreduce_small(_pack_small(per_layer, sg["final_norm"], loss))
    gs, ds_, ms, vs = _adamw(_pack_small_params(params).reshape(1, SMALL_ROWS, 128), small_sum.reshape(1, SMALL_ROWS, 128),
                             _pack_small_params(mom_m).reshape(1, SMALL_ROWS, 128),
                             _pack_small_params(mom_v).reshape(1, SMALL_ROWS, 128), "adamw_small")
    small_out = [_unpack_small(a.reshape(SMALL_ROWS, 128)) for a in (gs, ds_, ms, vs)]

    grad, delta, new_m, new_v = {}, {}, {}, {}
    for k in WEIGHT_ORDER:
        if k in SMALL_NAMES:
            grad[k], delta[k], new_m[k], new_v[k] = (so[k] for so in small_out)
        else:
            grad[k], delta[k], new_m[k], new_v[k] = big_out[k]
    for d in (grad, delta, new_m, new_v):
        for k in TRANSPOSED:
            d[k] = jnp.swapaxes(d[k], 1, 2)
    return (small_out[0]["loss"], dx.reshape(1, S, D), *[grad[k] for k in WEIGHT_ORDER], *[delta[k] for k in WEIGHT_ORDER],
            *[new_m[k] for k in WEIGHT_ORDER], *[new_v[k] for k in WEIGHT_ORDER])
```

```python
import functools

import jax
import jax.numpy as jnp
import numpy as np
from jax import lax
from jax.experimental import pallas as pl
from jax.experimental.pallas import tpu as pltpu

f32 = jnp.float32
bf16 = jnp.bfloat16

S = 2048
D = 1024
DEPTH = 4
NSH = 4
FS = 704
PROJ = 2560
PS = 640
PW = 256
AW = 768
NPAIR = 6
NORM_EPS = 1e-6
MASK_VALUE = -1e30
ROPE_THETA = 500000.0
DILATIONS = (1, 4, 16)
QBLK = 128
NBLK = S // QBLK
TM = 512
EW_ROWS = 16
VMEM_LIMIT = 56 * 1024 * 1024

ADAM_LR = 0.001
ADAM_B1 = 0.9
ADAM_B2 = 0.999
ADAM_EPS = 1e-08
ADAM_WD = 0.01
ADAM_STEP = 10

MESH = pl.DeviceIdType.MESH
ANY = pl.BlockSpec(memory_space=pl.ANY)

BIG = (("g1", FS, D), ("u1", FS, D), ("d1", FS, D), ("wi", D, PS), ("wo", PW, D), ("g2", FS, D), ("u2", FS, D), ("d2", FS, D))
TRANSPOSED = ("ffn1_w_gate", "ffn1_w_up", "ffn2_w_gate", "ffn2_w_up")
BIG_SRC = {"g1": "ffn1_w_gate", "u1": "ffn1_w_up", "d1": "ffn1_w_down", "wi": "w_in", "wo": "w_out",
           "g2": "ffn2_w_gate", "u2": "ffn2_w_up", "d2": "ffn2_w_down"}


def _cp(*sem):
    return pltpu.CompilerParams(dimension_semantics=sem if sem else None, vmem_limit_bytes=VMEM_LIMIT)


def _sds(shape, dt):
    return jax.ShapeDtypeStruct(shape, dt)


def _dot(a, b):
    return jnp.dot(a, b, preferred_element_type=f32)


def _dot_nt(a, b):
    return lax.dot_general(a, b, (((1,), (1,)), ((), ())), preferred_element_type=f32)


def _dot_tn(a, b):
    return lax.dot_general(a, b, (((0,), (0,)), ((), ())), preferred_element_type=f32)


def _dep(dep):
    return ([], []) if dep is None else ([ANY], [dep])


def _resident(shape):
    return pl.BlockSpec(shape, lambda i: (0,) * len(shape), pipeline_mode=pl.Buffered(1))


def _ffn_fwd(x, g, wg, wu, wd, dep=None):
    def body(x_ref, g_ref, wg_ref, wu_ref, wd_ref, *rest):
        xo_ref, h_ref, a_ref, b_ref = rest[-4:]
        xf = x_ref[...]
        r = lax.rsqrt(jnp.mean(xf * xf, axis=-1, keepdims=True) + NORM_EPS)
        hh = ((xf * r) * g_ref[...]).astype(bf16)
        h_ref[...] = hh
        acc = None
        for s in range(NSH):
            a = _dot_nt(hh, wg_ref[s])
            b = _dot_nt(hh, wu_ref[s])
            a_ref[s] = a.astype(bf16)
            b_ref[s] = b.astype(bf16)
            p = _dot((a * (1.0 / (1.0 + jnp.exp(-a))) * b).astype(bf16), wd_ref[s])
            acc = p if acc is None else acc + p
        xo_ref[...] = xf + 0.5 * acc

    tok = pl.BlockSpec((TM, D), lambda i: (i, 0))
    hid = pl.BlockSpec((NSH, TM, FS), lambda i: (0, i, 0))
    wsp = _resident((NSH, FS, D))
    dspec, dop = _dep(dep)
    return pl.pallas_call(
        body, out_shape=(_sds((S, D), f32), _sds((S, D), bf16), _sds((NSH, S, FS), bf16), _sds((NSH, S, FS), bf16)),
        grid=(S // TM,), in_specs=[tok, pl.BlockSpec((1, D), lambda i: (0, 0)), wsp, wsp, wsp] + dspec,
        out_specs=(tok, tok, hid, hid), name="ffn_fwd", compiler_params=_cp("parallel"))(x, g, wg, wu, wd, *dop)


def _in_proj(x, g, wi):
    def body(x_ref, g_ref, w_ref, o_ref, h_ref):
        xf = x_ref[...]
        r = lax.rsqrt(jnp.mean(xf * xf, axis=-1, keepdims=True) + NORM_EPS)
        hh = ((xf * r) * g_ref[...]).astype(bf16)
        h_ref[...] = hh
        for s in range(NSH):
            o_ref[:, PS * s:PS * (s + 1)] = _dot(hh, w_ref[s])

    tok = pl.BlockSpec((TM, D), lambda i: (i, 0))
    return pl.pallas_call(
        body, out_shape=(_sds((S, PROJ), f32), _sds((S, D), bf16)), grid=(S // TM,),
        in_specs=[tok, pl.BlockSpec((1, D), lambda i: (0, 0)), _resident((NSH, D, PS))],
        out_specs=(pl.BlockSpec((TM, PROJ), lambda i: (i, 0)), tok), name="in_proj", compiler_params=_cp("parallel"))(x, g, wi)


def _out_proj(x, mixed, wo):
    def body(x_ref, m_ref, w_ref, o_ref):
        o_ref[...] = x_ref[...] + _dot(m_ref[...], w_ref[...].reshape(D, D))

    return pl.pallas_call(
        body, out_shape=_sds((S, D), f32), grid=(S // TM,),
        in_specs=[pl.BlockSpec((TM, D), lambda i: (i, 0)), pl.BlockSpec((TM, D), lambda i: (i, 0)),
                  pl.BlockSpec((NSH, PW, D), lambda i: (0, 0, 0))],
        out_specs=pl.BlockSpec((TM, D), lambda i: (i, 0)), name="out_proj", compiler_params=_cp("parallel"))(x, mixed, wo)


def _out_proj_bwd(dx, wo):
    def body(dx_ref, w_ref, o_ref):
        o_ref[...] = _dot_nt(dx_ref[...].astype(bf16), w_ref[...].reshape(D, D))

    return pl.pallas_call(
        body, out_shape=_sds((S, D), f32), grid=(S // TM,),
        in_specs=[pl.BlockSpec((TM, D), lambda i: (i, 0)), pl.BlockSpec((NSH, PW, D), lambda i: (0, 0, 0))],
        out_specs=pl.BlockSpec((TM, D), lambda i: (i, 0)), name="out_proj_bwd", compiler_params=_cp("parallel"))(dx, wo)


def _ffn_bwd_mid(dx, h, a, b, wd, dep=None):
    nt = S // TM

    def body(dx_ref, h_ref, a_ref, b_ref, wd_ref, *rest):
        da_ref, db_ref, dwd_ref, dwg_ref, dwu_ref, acc_d, acc_g, acc_u, du_s, u_s = rest[-10:]
        i = pl.program_id(1)
        dy = (0.5 * dx_ref[...]).astype(bf16)
        hh = h_ref[...]
        du_s[...] = _dot_nt(dy, wd_ref[0])

        for c in range(TM // EW_ROWS):
            r = slice(c * EW_ROWS, (c + 1) * EW_ROWS)
            du = du_s[r, :]
            a = a_ref[0, r, :].astype(f32)
            b = b_ref[0, r, :].astype(f32)
            sig = 1.0 / (1.0 + jnp.exp(-a))
            silu = a * sig
            da_ref[0, r, :] = (du * b * (sig * (1.0 + a * (1.0 - sig)))).astype(bf16)
            db_ref[0, r, :] = (du * silu).astype(bf16)
            u_s[r, :] = (silu * b).astype(bf16)
        pd = _dot_tn(u_s[...], dy)
        pg = _dot_tn(da_ref[0], hh)
        pu = _dot_tn(db_ref[0], hh)

        @pl.when(i == 0)
        def _():
            acc_d[...] = pd
            acc_g[...] = pg
            acc_u[...] = pu

        @pl.when(i != 0)
        def _():
            acc_d[...] += pd
            acc_g[...] += pg
            acc_u[...] += pu

        @pl.when(i == nt - 1)
        def _():
            dwd_ref[...] = acc_d[...].astype(bf16).reshape(dwd_ref.shape)
            dwg_ref[...] = acc_g[...].astype(bf16).reshape(dwg_ref.shape)
            dwu_ref[...] = acc_u[...].astype(bf16).reshape(dwu_ref.shape)

    tok = pl.BlockSpec((TM, D), lambda s, i: (i, 0))
    hid = pl.BlockSpec((1, TM, FS), lambda s, i: (s, i, 0))
    wsp = pl.BlockSpec((1, 2, FS // 2, D), lambda s, i: (s, 0, 0, 0))
    hidden = _sds((NSH, S, FS), bf16)
    wgrad = _sds((NSH, 2, FS // 2, D), bf16)
    dspec, dop = _dep(dep)
    return pl.pallas_call(
        body, out_shape=(hidden, hidden, wgrad, wgrad, wgrad), grid=(NSH, nt),
        in_specs=[tok, tok, hid, hid, pl.BlockSpec((1, FS, D), lambda s, i: (s, 0, 0))] + dspec,
        out_specs=(hid, hid, wsp, wsp, wsp),
        scratch_shapes=[pltpu.VMEM((FS, D), f32)] * 3 + [pltpu.VMEM((TM, FS), f32), pltpu.VMEM((TM, FS), bf16)], name="ffn_bwd_mid",
        compiler_params=_cp("parallel", "arbitrary"))(dx, h, a, b, wd, *dop)


def _norm_bwd_tail(acc, x_ref, dxin_ref, g_ref, dxo_ref, dg_ref, first):
    xf = x_ref[...]
    r = lax.rsqrt(jnp.mean(xf * xf, axis=-1, keepdims=True) + NORM_EPS)
    xhat = xf * r
    dhg = acc * g_ref[...]
    dxo_ref[...] = dxin_ref[...] + r * (dhg - xhat * jnp.mean(dhg * xhat, axis=-1, keepdims=True))
    part = jnp.sum(acc * xhat, axis=0, keepdims=True)

    @pl.when(first)
    def _():
        dg_ref[...] = part

    @pl.when(jnp.logical_not(first))
    def _():
        dg_ref[...] += part


def _ffn_bwd_dx(dx, x_in, g, da, db, wg, wu):
    def body(dx_ref, x_ref, g_ref, da_ref, db_ref, wg_ref, wu_ref, dxo_ref, dg_ref):
        acc = None
        for s in range(NSH):
            p = _dot(da_ref[s], wg_ref[s])
            acc = p if acc is None else acc + p
            acc = acc + _dot(db_ref[s], wu_ref[s])
        _norm_bwd_tail(acc, x_ref, dx_ref, g_ref, dxo_ref, dg_ref, pl.program_id(0) == 0)

    tok = pl.BlockSpec((TM, D), lambda i: (i, 0))
    vec = pl.BlockSpec((1, D), lambda i: (0, 0))
    hid = pl.BlockSpec((NSH, TM, FS), lambda i: (0, i, 0))
    wsp = _resident((NSH, FS, D))
    return pl.pallas_call(
        body, out_shape=(_sds((S, D), f32), _sds((1, D), f32)), grid=(S // TM,),
        in_specs=[tok, tok, vec, hid, hid, wsp, wsp], out_specs=(tok, vec),
        name="ffn_bwd_dx", compiler_params=_cp("arbitrary"))(dx, x_in, g, da, db, wg, wu)


def _in_proj_bwd_dx(dx, x_in, g, dproj, wi):
    def body(dx_ref, x_ref, g_ref, dp_ref, w_ref, dxo_ref, dg_ref):
        acc = None
        for s in range(NSH):
            p = _dot_nt(dp_ref[:, PS * s:PS * (s + 1)], w_ref[s])
            acc = p if acc is None else acc + p
        _norm_bwd_tail(acc, x_ref, dx_ref, g_ref, dxo_ref, dg_ref, pl.program_id(0) == 0)

    tok = pl.BlockSpec((TM, D), lambda i: (i, 0))
    vec = pl.BlockSpec((1, D), lambda i: (0, 0))
    return pl.pallas_call(
        body, out_shape=(_sds((S, D), f32), _sds((1, D), f32)), grid=(S // TM,),
        in_specs=[tok, tok, vec, pl.BlockSpec((TM, PROJ), lambda i: (i, 0)), _resident((NSH, D, PS))], out_specs=(tok, vec),
        name="in_proj_bwd_dx", compiler_params=_cp("arbitrary"))(dx, x_in, g, dproj, wi)


def _dw(lhs, rhs, lhs_spec, rhs_spec, rows, cols, name, cast_rhs=False):
    def body(l_ref, r_ref, o_ref):
        r = r_ref[...].astype(bf16) if cast_rhs else r_ref[...]
        o_ref[...] = _dot_tn(l_ref[...], r).astype(bf16).reshape(1, 2, rows // 2, cols)

    return pl.pallas_call(
        body, out_shape=_sds((NSH, 2, rows // 2, cols), bf16), grid=(NSH,), in_specs=[lhs_spec, rhs_spec],
        out_specs=pl.BlockSpec((1, 2, rows // 2, cols), lambda s: (s, 0, 0, 0)), name=name, compiler_params=_cp("parallel"))(lhs, rhs)


_WHOLE_TOK = pl.BlockSpec((S, D), lambda s: (0, 0))


def _dw_in(h, dproj):
    return _dw(h, dproj, _WHOLE_TOK, pl.BlockSpec((S, PS), lambda s: (0, s)), D, PS, "dw_in")


def _dw_out(mixed, dx):
    return _dw(mixed, dx, pl.BlockSpec((S, PW), lambda s: (0, s)), _WHOLE_TOK, PW, D, "dw_out", cast_rhs=True)


def _final_loss(x, g, target):
    def body(x_ref, g_ref, t_ref, loss_ref, dx_ref, dg_ref):
        i = pl.program_id(0)
        xf = x_ref[...]
        r = lax.rsqrt(jnp.mean(xf * xf, axis=-1, keepdims=True) + NORM_EPS)
        xhat = xf * r
        err = xhat * g_ref[...] - t_ref[...]
        dy = err * (1.0 / D)
        dhg = dy * g_ref[...]
        dx_ref[...] = r * (dhg - xhat * jnp.mean(dhg * xhat, axis=-1, keepdims=True))
        part = jnp.sum(dy * xhat, axis=0, keepdims=True)
        lpart = jnp.zeros((8, 128), f32) + 0.5 * jnp.sum(jnp.mean(err * err, axis=-1, keepdims=True))

        @pl.when(i == 0)
        def _():
            dg_ref[...] = part
            loss_ref[...] = lpart

        @pl.when(i != 0)
        def _():
            dg_ref[...] += part
            loss_ref[...] += lpart

    tok = pl.BlockSpec((TM, D), lambda i: (i, 0))
    vec = pl.BlockSpec((1, D), lambda i: (0, 0))
    return pl.pallas_call(
        body, out_shape=(_sds((8, 128), f32), _sds((S, D), f32), _sds((1, D), f32)), grid=(S // TM,),
        in_specs=[tok, vec, tok], out_specs=(pl.BlockSpec((8, 128), lambda i: (0, 0)), tok, vec),
        name="final_loss", compiler_params=_cp("arbitrary"))(x, g, target)


def _shift_down(x, k, row):
    return jnp.where(row >= k, pltpu.roll(x, k, axis=0), 0.0)


def _shift_up(x, k, row):
    return jnp.where(row < S - k, pltpu.roll(x, S - k, axis=0), 0.0)


def _pool_geometry():
    row = lax.broadcasted_iota(jnp.int32, (S, PW), 0)
    grp = lax.broadcasted_iota(jnp.int32, (S, PW), 1) // 64
    half = jnp.where(grp == 0, 1, jnp.where(grp == 1, 2, jnp.where(grp == 2, 4, 8)))
    hi = jnp.minimum(row + half - 1, S - 1)
    lo = jnp.maximum(row - half, 0)
    return row, grp, (hi - lo + 1).astype(f32)


def _by_group(grp, v0, v1, v2, v3):
    return jnp.where(grp == 0, v0, jnp.where(grp == 1, v1, jnp.where(grp == 2, v2, v3)))


def _window_sums(x, row, grp, transpose):
    l1, r1 = x, x
    l2, r2 = l1 + _shift_down(l1, 1, row), r1 + _shift_up(r1, 1, row)
    l4, r4 = l2 + _shift_down(l2, 2, row), r2 + _shift_up(r2, 2, row)
    l8, r8 = l4 + _shift_down(l4, 4, row), r4 + _shift_up(r4, 4, row)
    lsel = _by_group(grp, l1, l2, l4, l8)
    rsel = _by_group(grp, r1, r2, r4, r8)
    if transpose:
        return lsel + _shift_up(rsel, 1, row)
    return _shift_down(lsel, 1, row) + rsel


def _pool_fwd(proj, wbd, scale):
    def body(v_ref, w_ref, sc_ref, mixed_ref, diff_ref):
        row, grp, cnt = _pool_geometry()
        v = v_ref[...]
        diff = (_window_sums(v, row, grp, False) / cnt - v).astype(bf16)
        diff_ref[...] = diff
        mixed_ref[...] = (_dot(diff, w_ref[...].astype(bf16)) * sc_ref[...]).astype(bf16)

    col = pl.BlockSpec((S, PW), lambda i: (0, 0))
    return pl.pallas_call(
        body, out_shape=(_sds((S, D), bf16), _sds((S, PW), bf16)), grid=(1,),
        in_specs=[col, pl.BlockSpec((PW, PW), lambda i: (0, 0)), pl.BlockSpec((1, PW), lambda i: (0, 0))],
        out_specs=(col, col), name="pool_fwd", compiler_params=_cp("arbitrary"))(proj, wbd, scale)


def _pool_bwd(dmixed, diff, wbd, scale, dproj):
    def body(dy_ref, diff_ref, w_ref, sc_ref, dproj_in, dv_ref, dw_ref, dsc_ref):
        del dproj_in
        row, grp, cnt = _pool_geometry()
        dy = dy_ref[...]
        diff = diff_ref[...]
        w = w_ref[...].astype(bf16)
        dsc_ref[...] = jnp.sum(dy * _dot(diff, w), axis=0, keepdims=True)
        dys = (dy * sc_ref[...]).astype(bf16)
        dw_ref[...] = _dot_tn(diff, dys)
        ddiff = _dot_nt(dys, w)
        dv_ref[...] = (_window_sums(ddiff / cnt, row, grp, True) - ddiff).astype(bf16)

    col = pl.BlockSpec((S, PW), lambda i: (0, 0))
    return pl.pallas_call(
        body, out_shape=(_sds((S, PROJ), bf16), _sds((PW, PW), f32), _sds((1, PW), f32)), grid=(1,),
        in_specs=[col, col, pl.BlockSpec((PW, PW), lambda i: (0, 0)), pl.BlockSpec((1, PW), lambda i: (0, 0)), ANY],
        out_specs=(col, pl.BlockSpec((PW, PW), lambda i: (0, 0)), pl.BlockSpec((1, PW), lambda i: (0, 0))),
        input_output_aliases={4: 0}, name="pool_bwd", compiler_params=_cp("arbitrary"))(dmixed, diff, wbd, scale, dproj)


def _rope_tables(pos_col, freq_row):
    def body(p_ref, f_ref, c_ref, a_ref, b_ref):
        ang = p_ref[...].astype(f32) * f_ref[...]
        l64 = lax.broadcasted_iota(jnp.int32, (S, 128), 1) % 64
        cos, sin = jnp.cos(ang), jnp.sin(ang)
        c_ref[...] = jnp.where(l64 < 16, cos, 1.0)
        a_ref[...] = jnp.where(l64 < 8, -sin, 0.0)
        b_ref[...] = jnp.where((l64 >= 8) & (l64 < 16), sin, 0.0)

    t = _sds((S, 128), f32)
    return pl.pallas_call(body, out_shape=(t, t, t), name="rope_tables", compiler_params=_cp())(pos_col, freq_row)


def _rope(t, c, a, b):
    return t * c + pltpu.roll(t, 120, axis=1) * a + pltpu.roll(t, 8, axis=1) * b


def _rope_bwd(g, c, a, b):
    return g * c + pltpu.roll(g * a, 8, axis=1) + pltpu.roll(g * b, 120, axis=1)


def _perm_load(ref, d):
    if d == 1:
        return ref[...]
    n = S // d
    return jnp.concatenate([ref[pl.ds(r, n, stride=d), :] for r in range(d)], axis=0)


def _unperm_store(ref, val, d):
    if d == 1:
        ref[...] = val
        return
    n = S // d
    for r in range(d):
        ref[pl.ds(r, n, stride=d), :] = val[r * n:(r + 1) * n, :]


def _band(xp):
    z = jnp.zeros((64, 128), bf16)
    p = jnp.concatenate([z, xp, z], axis=0).reshape(NBLK + 1, QBLK, 128)
    return jnp.concatenate([p[:NBLK], p[1:]], axis=1)


def _unband(xb):
    z = jnp.zeros((1, QBLK, 128), f32)
    p = jnp.concatenate([xb[:, :QBLK], z], axis=0) + jnp.concatenate([z, xb[:, QBLK:]], axis=0)
    return p.reshape(S + QBLK, 128)[64:S + 64]


def _band_mask(d):
    blocks_per_class = NBLK // d
    n = lax.broadcasted_iota(jnp.int32, (NBLK, 1, 2 * QBLK), 0) & (blocks_per_class - 1)
    be = lax.broadcasted_iota(jnp.int32, (NBLK, 1, 2 * QBLK), 2)
    a = lax.broadcasted_iota(jnp.int32, (1, 2 * QBLK, 2 * QBLK), 1) & (QBLK - 1)
    b = lax.broadcasted_iota(jnp.int32, (1, 2 * QBLK, 2 * QBLK), 2)
    band = (b >= a) & (b <= a + 128)
    edge = ((be >= 64) | (n != 0)) & ((be < QBLK + 64) | (n != blocks_per_class - 1))
    return band & edge


def _stack_heads(xb, lo):
    z = jnp.zeros_like(xb)
    return jnp.concatenate([jnp.where(lo, xb, z), jnp.where(lo, z, xb)], axis=1)


def _unstack_heads(x2, lo):
    return jnp.where(lo, x2[:, :QBLK], x2[:, QBLK:])


def _rows_to_lanes(col2, lo):
    return jnp.where(lo, jnp.broadcast_to(col2[:, :QBLK], (NBLK, QBLK, 128)), jnp.broadcast_to(col2[:, QBLK:], (NBLK, QBLK, 128)))


def _bmm_nt(a, b):
    return jnp.einsum('nqd,nkd->nqk', a, b, preferred_element_type=f32)


def _bmm_nn(a, b):
    return jnp.einsum('nqk,nkd->nqd', a, b, preferred_element_type=f32)


def _bmm_tn(a, b):
    return jnp.einsum('nqk,nqd->nkd', a, b, preferred_element_type=f32)


def _attn_fwd(proj, tc, ta, tb, mixed):
    def body(q_ref, k_ref, v_ref, c_ref, a_ref, b_ref, mixed_in, mixed_ref, o_ref, lse_ref, qn, kn, t_num, t_m, t_den):
        del mixed_in
        lo = lax.broadcasted_iota(jnp.int32, (1, 1, 128), 2) < 64
        c, a, b = c_ref[...], a_ref[...], b_ref[...]
        qn[...] = _rope(q_ref[...], c, a, b)
        kn[...] = _rope(k_ref[...], c, a, b)
        run = None
        for d in DILATIONS:
            q2 = _stack_heads(_perm_load(qn, d).astype(bf16).reshape(NBLK, QBLK, 128), lo)
            kb = _band(_perm_load(kn, d).astype(bf16))
            vb = _band(_perm_load(v_ref, d).astype(bf16))
            s = jnp.where(_band_mask(d), _bmm_nt(q2, kb) * 0.125, MASK_VALUE)
            m = jnp.max(s, axis=2, keepdims=True)
            p = jnp.exp(s - m)
            den = jnp.sum(p, axis=2, keepdims=True)
            num = _unstack_heads(_bmm_nn(p.astype(bf16), vb), lo)
            _unperm_store(t_num, num.reshape(S, 128), d)
            _unperm_store(t_m, _rows_to_lanes(m, lo).reshape(S, 128), d)
            _unperm_store(t_den, _rows_to_lanes(den, lo).reshape(S, 128), d)
            if run is None:
                run = (t_m[...], t_num[...], t_den[...])
            else:
                m_new = jnp.maximum(run[0], t_m[...])
                w_old, w_new = jnp.exp(run[0] - m_new), jnp.exp(t_m[...] - m_new)
                run = (m_new, w_old * run[1] + w_new * t_num[...], w_old * run[2] + w_new * t_den[...])
        out = run[1] / run[2]
        o_ref[...] = out
        mixed_ref[...] = out.astype(bf16)
        lse_ref[...] = run[0] + jnp.log(run[2])

    def col(off):
        return pl.BlockSpec((S, 128), lambda j, off=off: (0, off + j))

    tab = pl.BlockSpec((S, 128), lambda j: (0, 0))
    scr = pltpu.VMEM((S, 128), f32)
    return pl.pallas_call(
        body, out_shape=(_sds((S, D), bf16), _sds((S, AW), f32), _sds((S, AW), f32)), grid=(NPAIR,),
        in_specs=[col(2), col(8), col(14), tab, tab, tab, ANY], out_specs=(col(2), col(0), col(0)),
        scratch_shapes=[scr, scr, scr, scr, scr], input_output_aliases={6: 0}, name="attn_fwd",
        compiler_params=_cp("arbitrary"))(proj, proj, proj, tc, ta, tb, mixed)


def _attn_bwd(proj, tc, ta, tb, o, lse, dmixed):
    def body(q_ref, k_ref, v_ref, c_ref, a_ref, b_ref, o_ref, lse_ref, do_ref, dp_ref, qn, kn, tmp, dk_s, dv_s):
        t = pl.program_id(1)

        @pl.when(t == 0)
        def _():
            lo = lax.broadcasted_iota(jnp.int32, (1, 1, 128), 2) < 64
            c, a, b = c_ref[...], a_ref[...], b_ref[...]
            qn[...] = _rope(q_ref[...], c, a, b)
            kn[...] = _rope(k_ref[...], c, a, b)
            dq = dk = dv = None
            for d in DILATIONS:
                q2 = _stack_heads(_perm_load(qn, d).astype(bf16).reshape(NBLK, QBLK, 128), lo)
                kb = _band(_perm_load(kn, d).astype(bf16))
                vb = _band(_perm_load(v_ref, d).astype(bf16))
                dob = _perm_load(do_ref, d).reshape(NBLK, QBLK, 128)
                ob = _perm_load(o_ref, d).reshape(NBLK, QBLK, 128)
                lsb = _perm_load(lse_ref, d).reshape(NBLK, QBLK, 128)
                do2 = _stack_heads(dob.astype(bf16), lo)
                delta2 = jnp.sum(_stack_heads(dob * ob, lo), axis=2, keepdims=True)
                lse2 = jnp.max(jnp.concatenate([jnp.where(lo, lsb, MASK_VALUE), jnp.where(lo, MASK_VALUE, lsb)], axis=1),
                               axis=2, keepdims=True)
                s = _bmm_nt(q2, kb) * 0.125
                p = jnp.where(_band_mask(d), jnp.exp(s - lse2), 0.0)
                ds = (p * (_bmm_nt(do2, vb) - delta2) * 0.125).astype(bf16)
                pb = p.astype(bf16)
                dq_b = _unstack_heads(_bmm_nn(ds, kb), lo).reshape(S, 128)
                dk_b = _unband(_bmm_tn(ds, q2))
                dv_b = _unband(_bmm_tn(pb, do2))
                acc = []
                for prev, new in ((dq, dq_b), (dk, dk_b), (dv, dv_b)):
                    _unperm_store(tmp, new, d)
                    acc.append(tmp[...] if prev is None else prev + tmp[...])
                dq, dk, dv = acc
            dp_ref[...] = _rope_bwd(dq, c, a, b).astype(bf16)
            dk_s[...] = _rope_bwd(dk, c, a, b).astype(bf16)
            dv_s[...] = dv.astype(bf16)

        @pl.when(t == 1)
        def _():
            dp_ref[...] = dk_s[...]

        @pl.when(t == 2)
        def _():
            dp_ref[...] = dv_s[...]

    def col(off):
        return pl.BlockSpec((S, 128), lambda j, t, off=off: (0, off + j))

    tab = pl.BlockSpec((S, 128), lambda j, t: (0, 0))
    scr = pltpu.VMEM((S, 128), f32)
    scb = pltpu.VMEM((S, 128), bf16)
    return pl.pallas_call(
        body, out_shape=_sds((S, PROJ), bf16), grid=(NPAIR, 3),
        in_specs=[col(2), col(8), col(14), tab, tab, tab, col(0), col(0), col(2)],
        out_specs=pl.BlockSpec((S, 128), lambda j, t: (0, 2 + NPAIR * t + j)),
        scratch_shapes=[scr, scr, scr, scb, scb], name="attn_bwd",
        compiler_params=_cp("arbitrary", "arbitrary"))(proj, proj, proj, tc, ta, tb, o, lse, dmixed)


def _block_diag(w4):
    out = jnp.zeros((PW, PW), w4.dtype)
    for g in range(4):
        out = out.at[64 * g:64 * (g + 1), 64 * g:64 * (g + 1)].set(w4[g])
    return out


def _diag_blocks(w):
    return jnp.stack([w[64 * g:64 * (g + 1), 64 * g:64 * (g + 1)] for g in range(4)])


def _rope_inputs(positions):
    inv_freq = ROPE_THETA ** (-jnp.arange(0, 16, 2, dtype=f32) / 16)
    l64 = np.arange(128) % 64
    idx = np.where(l64 < 16, l64 % 8, 0)
    return positions.reshape(S, 1), inv_freq[idx].reshape(1, 128)


def _layer_fwd(x, w, small, l, tabs, dep=None):
    g1, gm, g2 = (small[k][l].reshape(1, D) for k in ("ffn1_norm", "mix_norm", "ffn2_norm"))
    wbd = _block_diag(small["pool_w"][l])
    psc = small["pool_scale"][l].reshape(1, PW)
    x1, h1, a1, b1 = _ffn_fwd(x, g1, w["g1"], w["u1"], w["d1"], dep)
    proj, h2 = _in_proj(x1, gm, w["wi"])
    mixed, diff = _pool_fwd(proj, wbd, psc)
    mixed, o, lse = _attn_fwd(proj, *tabs, mixed)
    x2 = _out_proj(x1, mixed, w["wo"])
    out, h3, a2, b2 = _ffn_fwd(x2, g2, w["g2"], w["u2"], w["d2"])
    return out, dict(x0=x, h1=h1, a1=a1, b1=b1, x1=x1, h2=h2, proj=proj, mixed=mixed, diff=diff, o=o, lse=lse,
                     x2=x2, h3=h3, a2=a2, b2=b2, g1=g1, gm=gm, g2=g2, wbd=wbd, psc=psc)


def _layer_bwd(dx, w, sv, tabs, dep=None):
    gr, sg = {}, {}
    da, db, gr["d2"], gr["g2"], gr["u2"] = _ffn_bwd_mid(dx, sv["h3"], sv["a2"], sv["b2"], w["d2"], dep)
    dx, sg["ffn2_norm"] = _ffn_bwd_dx(dx, sv["x2"], sv["g2"], da, db, w["g2"], w["u2"])
    gr["wo"] = _dw_out(sv["mixed"], dx)
    dmixed = _out_proj_bwd(dx, w["wo"])
    dproj = _attn_bwd(sv["proj"], *tabs, sv["o"], sv["lse"], dmixed)
    dproj, dwbd, sg["pool_scale"] = _pool_bwd(dmixed, sv["diff"], sv["wbd"], sv["psc"], dproj)
    sg["pool_w"] = _diag_blocks(dwbd)
    gr["wi"] = _dw_in(sv["h2"], dproj)
    dx, sg["mix_norm"] = _in_proj_bwd_dx(dx, sv["x1"], sv["gm"], dproj, w["wi"])
    da, db, gr["d1"], gr["g1"], gr["u1"] = _ffn_bwd_mid(dx, sv["h1"], sv["a1"], sv["b1"], w["d1"])
    dx, sg["ffn1_norm"] = _ffn_bwd_dx(dx, sv["x0"], sv["g1"], da, db, w["g1"], w["u1"])
    return dx, gr, sg


def _forward_backward(x, positions, target, gathered, small):
    tabs = _rope_tables(*_rope_inputs(positions))
    saved = []
    for l in range(DEPTH):
        x, sv = _layer_fwd(x, gathered[l], small, l, tabs)
        saved.append(sv)
    loss, dx, dgf = _final_loss(x, small["final_norm"].reshape(1, D), target)
    big = [None] * DEPTH
    sg = {k: [None] * DEPTH for k in ("ffn1_norm", "mix_norm", "pool_w", "pool_scale", "ffn2_norm")}
    for l in reversed(range(DEPTH)):
        dx, big[l], sgl = _layer_bwd(dx, gathered[l], saved[l], tabs)
        for k, v in sgl.items():
            sg[k][l] = v
    sg["final_norm"] = dgf
    return loss, dx, big, sg


def _place():
    x, y, c = lax.axis_index("x"), lax.axis_index("y"), lax.axis_index("c")
    chips = [(1 - x, y), (x, 1 - y), (1 - x, 1 - y)]
    return x, y, c, chips


def _cast_layer(params, l, place, dep=None):
    def body(p_ref, *refs):
        del p_ref
        for i_ref, o_ref in zip(refs[:8], refs[-8:]):
            o_ref[...] = i_ref[...].astype(bf16).reshape(o_ref.shape)

    ins, in_specs, out_shape, out_specs = [], [], [], []
    for name, rows, cols in BIG:
        q = rows // 4
        ins.append(params[BIG_SRC[name]])
        in_specs.append(pl.BlockSpec((1, q, cols), lambda i, p, l=l: (l, i, 0)))
        out_shape.append(_sds((NSH, 2, rows // 2, cols), bf16))
        out_specs.append(pl.BlockSpec((1, 1, q, cols), lambda i, p: (p[1], i // 2, i % 2, 0)))
    dspec, dop = _dep(dep)
    return pl.pallas_call(
        body, out_shape=out_shape,
        grid_spec=pltpu.PrefetchScalarGridSpec(num_scalar_prefetch=1, grid=(4,), in_specs=in_specs + dspec, out_specs=out_specs),
        name=f"cast_layer{l}", compiler_params=_cp("parallel"))(place, *ins, *dop)


HBM = pl.BlockSpec(memory_space=pltpu.HBM)
SEM = pl.BlockSpec(memory_space=pltpu.SEMAPHORE)
_SPLIT = pltpu.CompilerParams(has_side_effects=pltpu.SideEffectType.DATAFLOW_SIDE_EFFECTING)


def _hbm(arrays):
    return [pltpu.with_memory_space_constraint(a, pltpu.HBM) for a in arrays]


def _chip_copies(src_of, dst_of, send_sems, recv_sems, n):
    x, y, c, chips = _place()
    me = 2 * x + y
    out = []
    for t in range(n):
        for k, chip in enumerate(chips):
            peer = 2 * chip[0] + chip[1]
            send = pltpu.make_async_remote_copy(
                src_ref=src_of(t, peer), dst_ref=dst_of(t, me), send_sem=send_sems.at[3 * t + k], recv_sem=recv_sems.at[3 * t + k],
                device_id=(chip[0], chip[1], c), device_id_type=MESH)
            land = pltpu.make_async_remote_copy(
                src_ref=src_of(t, peer), dst_ref=dst_of(t, peer), send_sem=send_sems.at[3 * t + k], recv_sem=recv_sems.at[3 * t + k],
                device_id=(chip[0], chip[1], c), device_id_type=MESH)
            out.append((send, land))
    return out


def _exchange_start(src, land, after, src_of, dst_of, name):
    n, m = len(src), len(src) + len(land)

    def body(*refs):
        src_refs = refs[:n]
        land_refs = refs[n:m] if land else src_refs
        send_sems, recv_sems = refs[m + 1], refs[m + 2]
        token = refs[-1]
        for send, _ in _chip_copies(lambda t, s: src_of(src_refs[t], s), lambda t, s: dst_of(land_refs[t], s), send_sems, recv_sems, n):
            send.start()
        token[...] = jnp.zeros_like(token)

    arrays = list(src) + list(land)
    out_shape = ([pltpu.SemaphoreType.DMA((3 * n,)), pltpu.SemaphoreType.DMA((3 * n,))] + [pltpu.HBM(a.shape, a.dtype) for a in arrays]
                 + [_sds((8, 128), f32)])
    res = pl.pallas_call(
        body, out_shape=out_shape, in_specs=[HBM] * m + [ANY], out_specs=[SEM, SEM] + [HBM] * m + [pl.BlockSpec(memory_space=pltpu.VMEM)],
        input_output_aliases={i: 2 + i for i in range(m)}, name=name, compiler_params=_SPLIT)(*_hbm(arrays), after)
    return res[0], res[1], list(res[2:2 + n]), list(res[2 + n:2 + m]), res[-1]


def _exchange_wait(send_sems, recv_sems, src, land, after, src_of, dst_of, name):
    n, m = len(src), len(src) + len(land)

    def body(*refs):
        src_refs = refs[:n]
        land_refs = refs[n:m] if land else src_refs
        send_sems, recv_sems = refs[m], refs[m + 1]
        for send, land_cp in _chip_copies(lambda t, s: src_of(src_refs[t], s), lambda t, s: dst_of(land_refs[t], s), send_sems, recv_sems, n):
            send.wait_send()
            land_cp.wait_recv()

    arrays = list(src) + list(land)
    res = pl.pallas_call(
        body, out_shape=[pltpu.HBM(a.shape, a.dtype) for a in arrays], in_specs=[HBM] * m + [SEM, SEM, ANY], out_specs=[HBM] * m,
        input_output_aliases={i: i for i in range(m)}, name=name, compiler_params=_SPLIT)(*arrays, send_sems, recv_sems, after)
    return list(res[:n]), list(res[n:])


def _own_half(ref, s):
    x, y, c, _ = _place()
    return ref.at[2 * x + y, c]


def _slot_half(ref, s):
    return ref.at[s, lax.axis_index("c")]


def _slot(ref, s):
    return ref.at[s]


def _gather_forward(bufs):
    n = len(bufs)

    def body(*refs):
        outs = refs[n:2 * n]
        send_sems, recv_sems = refs[2 * n:]
        x, y, c, chips = _place()
        sibling = (x, y, 1 - c)
        passed = []
        for t in range(n):
            for k, chip in enumerate(chips):
                blk = outs[t].at[2 * chip[0] + chip[1], c]
                cp = pltpu.make_async_remote_copy(
                    src_ref=blk, dst_ref=blk, send_sem=send_sems.at[t, k], recv_sem=recv_sems.at[t, k],
                    device_id=sibling, device_id_type=MESH)
                cp.start()
                passed.append(cp)
        for t in range(n):
            for k, chip in enumerate(chips):
                blk = outs[t].at[2 * chip[0] + chip[1], 1 - c]
                pltpu.make_async_remote_copy(
                    src_ref=blk, dst_ref=blk, send_sem=send_sems.at[t, k], recv_sem=recv_sems.at[t, k],
                    device_id=sibling, device_id_type=MESH).wait_recv()
        for cp in passed:
            cp.wait_send()

    out_shape = [_sds(a.shape, bf16) for a in bufs]
    return pl.pallas_call(
        body, out_shape=out_shape, in_specs=[ANY] * n, out_specs=[ANY] * n, input_output_aliases={t: t for t in range(n)},
        scratch_shapes=[pltpu.SemaphoreType.DMA((n, 3)), pltpu.SemaphoreType.DMA((n, 3))], name="gather_forward")(*bufs)


def _sibling_swap(grads):
    n = len(grads)

    def body(*refs):
        ins, outs = refs[:n], refs[n:2 * n]
        send_sems, recv_sems = refs[2 * n:]
        x, y, c, _ = _place()
        cps = []
        for t in range(n):
            for s in range(NSH):
                cp = pltpu.make_async_remote_copy(
                    src_ref=ins[t].at[s, 1 - c], dst_ref=outs[t].at[s], send_sem=send_sems.at[t, s], recv_sem=recv_sems.at[t, s],
                    device_id=(x, y, 1 - c), device_id_type=MESH)
                cp.start()
                cps.append(cp)
        for cp in cps:
            cp.wait()

    out_shape = [_sds((NSH,) + a.shape[2:], bf16) for a in grads]
    return pl.pallas_call(
        body, out_shape=out_shape, in_specs=[ANY] * n, out_specs=[ANY] * n,
        scratch_shapes=[pltpu.SemaphoreType.DMA((n, NSH)), pltpu.SemaphoreType.DMA((n, NSH))],
        name="sibling_swap")(*grads)


def _row_tile(h):
    return h // 2 if h % 32 == 0 else h


def _pair_sum(grads, got, c_idx):
    n = len(grads)

    def body(c_ref, *refs):
        del c_ref
        for t in range(n):
            refs[2 * n + t][...] = (refs[t][...].astype(f32).reshape(refs[n + t].shape) + refs[n + t][...].astype(f32)).astype(bf16)

    in_specs, out_shape, out_specs = [], [], []
    for a in grads:
        h, cols = a.shape[2:]
        in_specs.append(pl.BlockSpec((1, 1, _row_tile(h), cols), lambda s, i, c: (s, c[0], i, 0)))
    for a in grads:
        h, cols = a.shape[2:]
        in_specs.append(pl.BlockSpec((1, _row_tile(h), cols), lambda s, i, c: (s, i, 0)))
        out_shape.append(_sds((NSH, h, cols), bf16))
        out_specs.append(pl.BlockSpec((1, _row_tile(h), cols), lambda s, i, c: (s, i, 0)))
    return pl.pallas_call(
        body, out_shape=out_shape,
        grid_spec=pltpu.PrefetchScalarGridSpec(num_scalar_prefetch=1, grid=(NSH, 2), in_specs=in_specs, out_specs=out_specs),
        name="pair_sum", compiler_params=_cp("parallel", "parallel"))(c_idx, *grads, *got)


def _chip_sum(psum, parts, full, place, l, name):
    n = len(parts)

    def body(p_ref, *refs):
        s = pl.program_id(1)
        for t in range(n):
            val = jnp.where(s == p_ref[1], refs[t][0], refs[n + t][0]).astype(f32)
            out = refs[3 * n + t]

            @pl.when(s == 0)
            def _(out=out, val=val):
                out[0, 0] = val

            @pl.when(s != 0)
            def _(out=out, val=val):
                out[0, 0] += val

    own_specs, part_specs, out_shape, out_specs = [], [], [], []
    for a, fl in zip(parts, full):
        _, h, cols = a.shape
        r = _row_tile(h)
        own_specs.append(pl.BlockSpec((1, r, cols), lambda i, s, p: (p[1], i, 0)))
        part_specs.append(pl.BlockSpec((1, r, cols), lambda i, s, p: (jnp.where(s == p[1], (s + 1) % NSH, s), i, 0)))
        out_shape.append(_sds(fl.shape, f32))
        out_specs.append(pl.BlockSpec((1, 1, r, cols), lambda i, s, p, l=l: (l, p[0], i, 0)))
    return pl.pallas_call(
        body, out_shape=out_shape,
        grid_spec=pltpu.PrefetchScalarGridSpec(num_scalar_prefetch=1, grid=(2, NSH), in_specs=own_specs + part_specs + [ANY] * n,
                                               out_specs=out_specs),
        input_output_aliases={1 + 2 * n + t: t for t in range(n)}, name=name,
        compiler_params=_cp("parallel", "arbitrary"))(place, *psum, *parts, *full)


def _sibling_share(full, l, name):
    n = len(full)

    def body(*refs):
        outs = refs[n:2 * n]
        send_sems, recv_sems = refs[2 * n:]
        x, y, c, _ = _place()
        sibling = (x, y, 1 - c)
        cps = []
        for t in range(n):
            blk = outs[t].at[l, c]
            cp = pltpu.make_async_remote_copy(
                src_ref=blk, dst_ref=blk, send_sem=send_sems.at[t], recv_sem=recv_sems.at[t], device_id=sibling, device_id_type=MESH)
            cp.start()
            cps.append(cp)
        for t in range(n):
            blk = outs[t].at[l, 1 - c]
            pltpu.make_async_remote_copy(
                src_ref=blk, dst_ref=blk, send_sem=send_sems.at[t], recv_sem=recv_sems.at[t],
                device_id=sibling, device_id_type=MESH).wait_recv()
        for cp in cps:
            cp.wait_send()

    out_shape = [_sds(a.shape, f32) for a in full]
    return pl.pallas_call(
        body, out_shape=out_shape, in_specs=[ANY] * n, out_specs=[ANY] * n, input_output_aliases={t: t for t in range(n)},
        scratch_shapes=[pltpu.SemaphoreType.DMA((n,)), pltpu.SemaphoreType.DMA((n,))], name=name)(*full)


SMALL_ROWS = 656


def _pack_small(per_layer, final_vec, loss_tile):
    rows = []
    for l in range(DEPTH):
        for k in ("ffn1_norm", "mix_norm", "ffn2_norm"):
            rows.append(per_layer[k][l].reshape(8, 128))
        rows.append(per_layer["pool_w"][l].reshape(128, 128))
        rows.append(jnp.pad(per_layer["pool_scale"][l].reshape(2, 128), ((0, 6), (0, 0))))
    rows.append(final_vec.reshape(8, 128))
    rows.append(loss_tile)
    return jnp.concatenate(rows, axis=0)


def _unpack_small(buf):
    out = {k: [] for k in ("ffn1_norm", "mix_norm", "ffn2_norm", "pool_w", "pool_scale")}
    r = 0
    for l in range(DEPTH):
        for k in ("ffn1_norm", "mix_norm", "ffn2_norm"):
            out[k].append(buf[r:r + 8].reshape(D))
            r += 8
        out["pool_w"].append(buf[r:r + 128].reshape(4, 64, 64))
        r += 128
        out["pool_scale"].append(buf[r:r + 2].reshape(PW))
        r += 8
    res = {k: jnp.stack(v) for k, v in out.items()}
    res["final_norm"] = buf[r:r + 8].reshape(D)
    res["loss"] = buf[r + 8, 0]
    return res


def _allreduce_small(buf):
    def body(in_ref, out_ref, slots, send_sems, recv_sems):
        x, y, c, _ = _place()
        me = 4 * x + 2 * y + c
        slots[me] = in_ref[...]
        peers = []
        for k in range(1, 8):
            px, py, pc = x ^ (k >> 2), y ^ ((k >> 1) & 1), c ^ (k & 1)
            cp = pltpu.make_async_remote_copy(
                src_ref=in_ref, dst_ref=slots.at[me], send_sem=send_sems.at[k - 1], recv_sem=recv_sems.at[k - 1],
                device_id=(px, py, pc), device_id_type=MESH)
            cp.start()
            peers.append(cp)
        for k in range(1, 8):
            px, py, pc = x ^ (k >> 2), y ^ ((k >> 1) & 1), c ^ (k & 1)
            slot = 4 * px + 2 * py + pc
            pltpu.make_async_remote_copy(
                src_ref=slots.at[slot], dst_ref=slots.at[slot], send_sem=send_sems.at[k - 1], recv_sem=recv_sems.at[k - 1],
                device_id=(px, py, pc), device_id_type=MESH).wait_recv()
        for cp in peers:
            cp.wait_send()
        acc = slots[0]
        for j in range(1, 8):
            acc = acc + slots[j]
        out_ref[...] = acc

    return pl.pallas_call(
        body, out_shape=_sds((SMALL_ROWS, 128), f32),
        in_specs=[pl.BlockSpec(memory_space=pltpu.VMEM)], out_specs=pl.BlockSpec(memory_space=pltpu.VMEM),
        scratch_shapes=[pltpu.VMEM((8, SMALL_ROWS, 128), f32), pltpu.SemaphoreType.DMA((7,)), pltpu.SemaphoreType.DMA((7,))],
        name="allreduce_small", compiler_params=_cp())(buf)


def _adamw_math(w, g, m, v):
    m = ADAM_B1 * m + (1.0 - ADAM_B1) * g
    v = ADAM_B2 * v + (1.0 - ADAM_B2) * (g * g)
    m_hat = m / (1.0 - ADAM_B1 ** ADAM_STEP)
    v_hat = v / (1.0 - ADAM_B2 ** ADAM_STEP)
    return -ADAM_LR * (m_hat / (jnp.sqrt(v_hat) + ADAM_EPS) + ADAM_WD * w), m, v


def _adamw(w, g, m, v, name, first=0, prev=None, dep=None):
    def body(w_ref, g_ref, m_ref, v_ref, *rest):
        go_ref, d_ref, mo_ref, vo_ref = rest[-4:]
        g = g_ref[...]
        d, mn, vn = _adamw_math(w_ref[...], g, m_ref[...], v_ref[...])
        go_ref[...] = g
        d_ref[...] = d
        mo_ref[...] = mn
        vo_ref[...] = vn

    _, rows, cols = w.shape
    r = rows // 4 if rows % 32 == 0 else rows
    spec = pl.BlockSpec((1, r, cols), lambda i, j: (first + i, j, 0))
    gspec = pl.BlockSpec((1, r, cols), lambda i, j: (i, j, 0))
    out = _sds(w.shape, f32)
    extra = [] if prev is None else list(prev)
    dspec, dop = _dep(dep)
    return pl.pallas_call(
        body, out_shape=(out, out, out, out), grid=(g.shape[0], rows // r), in_specs=[spec, gspec, spec, spec] + [ANY] * len(extra) + dspec,
        out_specs=(spec,) * 4, input_output_aliases={4 + i: i for i in range(len(extra))}, name=name,
        compiler_params=_cp("parallel", "parallel"))(w, g, m, v, *extra, *dop)


SMALL_NAMES = ("ffn1_norm", "mix_norm", "pool_w", "pool_scale", "ffn2_norm", "final_norm")
WEIGHT_ORDER = ("ffn1_norm", "ffn1_w_gate", "ffn1_w_up", "ffn1_w_down", "mix_norm", "w_in", "pool_w", "pool_scale", "w_out",
                "ffn2_norm", "ffn2_w_gate", "ffn2_w_up", "ffn2_w_down", "final_norm")


def _pack_small_params(p):
    per_layer = {k: [p[k][l] for l in range(DEPTH)] for k in ("ffn1_norm", "mix_norm", "ffn2_norm", "pool_w", "pool_scale")}
    return _pack_small(per_layer, p["final_norm"], jnp.zeros((8, 128), f32))


def kernel(x, positions, ffn1_norm, ffn1_w_gate, ffn1_w_up, ffn1_w_down, mix_norm, w_in, pool_w, pool_scale, w_out, ffn2_norm, ffn2_w_gate, ffn2_w_up, ffn2_w_down, final_norm, loss_target, m_ffn1_norm, m_ffn1_w_gate, m_ffn1_w_up, m_ffn1_w_down, m_mix_norm, m_w_in, m_pool_w, m_pool_scale, m_w_out, m_ffn2_norm, m_ffn2_w_gate, m_ffn2_w_up, m_ffn2_w_down, m_final_norm, v_ffn1_norm, v_ffn1_w_gate, v_ffn1_w_up, v_ffn1_w_down, v_mix_norm, v_w_in, v_pool_w, v_pool_scale, v_w_out, v_ffn2_norm, v_ffn2_w_gate, v_ffn2_w_up, v_ffn2_w_down, v_final_norm):
    params = dict(ffn1_norm=ffn1_norm, ffn1_w_gate=ffn1_w_gate, ffn1_w_up=ffn1_w_up, ffn1_w_down=ffn1_w_down,
                  mix_norm=mix_norm, w_in=w_in, pool_w=pool_w, pool_scale=pool_scale, w_out=w_out, ffn2_norm=ffn2_norm,
                  ffn2_w_gate=ffn2_w_gate, ffn2_w_up=ffn2_w_up, ffn2_w_down=ffn2_w_down, final_norm=final_norm)
    mom_m = dict(ffn1_norm=m_ffn1_norm, ffn1_w_gate=m_ffn1_w_gate, ffn1_w_up=m_ffn1_w_up, ffn1_w_down=m_ffn1_w_down,
                 mix_norm=m_mix_norm, w_in=m_w_in, pool_w=m_pool_w, pool_scale=m_pool_scale, w_out=m_w_out,
                 ffn2_norm=m_ffn2_norm, ffn2_w_gate=m_ffn2_w_gate, ffn2_w_up=m_ffn2_w_up, ffn2_w_down=m_ffn2_w_down,
                 final_norm=m_final_norm)
    mom_v = dict(ffn1_norm=v_ffn1_norm, ffn1_w_gate=v_ffn1_w_gate, ffn1_w_up=v_ffn1_w_up, ffn1_w_down=v_ffn1_w_down,
                 mix_norm=v_mix_norm, w_in=v_w_in, pool_w=v_pool_w, pool_scale=v_pool_scale, w_out=v_w_out,
                 ffn2_norm=v_ffn2_norm, ffn2_w_gate=v_ffn2_w_gate, ffn2_w_up=v_ffn2_w_up, ffn2_w_down=v_ffn2_w_down,
                 final_norm=v_final_norm)
    names = [t[0] for t in BIG]
    for d in (params, mom_m, mom_v):
        for k in TRANSPOSED:
            d[k] = jnp.swapaxes(d[k], 1, 2)

    place = jnp.stack([lax.axis_index("c"), 2 * lax.axis_index("x") + lax.axis_index("y")]).astype(jnp.int32)
    def gather_start(l, cast, after):
        return _exchange_start(cast, [], after, _own_half, _slot_half, f"gather_start{l}")

    def gather_end(started, after, l):
        send_sems, recv_sems, bufs, _, _ = started
        bufs, _ = _exchange_wait(send_sems, recv_sems, bufs, [], after, _own_half, _slot_half, f"gather_wait{l}")
        return {nm: a.reshape(NSH, rows, cols) for (nm, rows, cols), a in zip(BIG, _gather_forward(bufs))}

    tabs = _rope_tables(*_rope_inputs(positions))
    h = x.reshape(S, D)
    weights, saved = [], []
    started = gather_start(0, _cast_layer(params, 0, place), place)
    after = started[-1]
    casts = {}
    for l in range(1, DEPTH):
        casts[l] = _cast_layer(params, l, place, after)
        after = casts[l][0]
    for l in range(DEPTH):
        weights.append(gather_end(started, after, l))
        dep = None
        if l + 1 < DEPTH:
            started = gather_start(l + 1, casts[l + 1], weights[l]["g1"])
            dep = started[-1]
        h, sv = _layer_fwd(h, weights[l], params, l, tabs, dep)
        saved.append(sv)
        after = h
    loss, dx, dgf = _final_loss(h, final_norm.reshape(1, D), loss_target.reshape(S, D))

    upper = [lax.empty((DEPTH - 1, 2, rows // 2, cols), f32) for _, rows, cols in BIG]
    lower = [lax.empty((1, 2, rows // 2, cols), f32) for _, rows, cols in BIG]
    sg = {k: [None] * DEPTH for k in ("ffn1_norm", "mix_norm", "pool_w", "pool_scale", "ffn2_norm")}
    sg["final_norm"] = dgf

    def reduce_end(started, after, l, full, slot):
        send_sems, recv_sems, psum, parts, _ = started
        psum, parts = _exchange_wait(send_sems, recv_sems, psum, parts, after, _slot, _slot, f"grad_wait{l}")
        return _sibling_share(_chip_sum(psum, parts, full, place, slot, f"chip_sum{l}"), slot, f"sibling_share{l}")

    started, dep = None, None
    for l in reversed(range(DEPTH)):
        dx, gr, sgl = _layer_bwd(dx, weights[l], saved[l], tabs, dep)
        for k, v in sgl.items():
            sg[k][l] = v
        if started is not None:
            upper = reduce_end(started, dx, l + 1, upper, l)
        grads = [gr[nm] for nm in names]
        psum = _pair_sum(grads, _sibling_swap(grads), place)
        parts = [lax.empty(a.shape, bf16) for a in psum]
        started = _exchange_start(psum, parts, place, _slot, _slot, f"grad_start{l}")
        dep = started[-1]

    big_out = {}
    for (nm, rows, cols), g in zip(BIG, upper):
        k = BIG_SRC[nm]
        big_out[k] = _adamw(params[k], g.reshape(DEPTH - 1, rows, cols), mom_m[k], mom_v[k], "adamw_upper_" + k, first=1, dep=dep)
        dep = big_out[k][1]
    lower = reduce_end(started, dep, 0, lower, 0)
    for (nm, rows, cols), g in zip(BIG, lower):
        k = BIG_SRC[nm]
        big_out[k] = _adamw(params[k], g.reshape(1, rows, cols), mom_m[k], mom_v[k], "adamw_lower_" + k, first=0, prev=big_out[k])

    per_layer = {k: sg[k] for k in ("ffn1_norm", "mix_norm", "ffn2_norm", "pool_w", "pool_scale")}
    small_sum = _allreduce_small(_pack_small(per_layer, sg["final_norm"], loss))
    gs, ds_, ms, vs = _adamw(_pack_small_params(params).reshape(1, SMALL_ROWS, 128), small_sum.reshape(1, SMALL_ROWS, 128),
                             _pack_small_params(mom_m).reshape(1, SMALL_ROWS, 128),
                             _pack_small_params(mom_v).reshape(1, SMALL_ROWS, 128), "adamw_small")
    small_out = [_unpack_small(a.reshape(SMALL_ROWS, 128)) for a in (gs, ds_, ms, vs)]

    grad, delta, new_m, new_v = {}, {}, {}, {}
    for k in WEIGHT_ORDER:
        if k in SMALL_NAMES:
            grad[k], delta[k], new_m[k], new_v[k] = (so[k] for so in small_out)
        else:
            grad[k], delta[k], new_m[k], new_v[k] = big_out[k]
    for d in (grad, delta, new_m, new_v):
        for k in TRANSPOSED:
            d[k] = jnp.swapaxes(d[k], 1, 2)
    return (small_out[0]["loss"], dx.reshape(1, S, D), *[grad[k] for k in WEIGHT_ORDER], *[delta[k] for k in WEIGHT_ORDER],
            *[new_m[k] for k in WEIGHT_ORDER], *[new_v[k] for k in WEIGHT_ORDER])
```

```python
import functools

import jax
import jax.numpy as jnp
import numpy as np
from jax import lax
from jax.experimental import pallas as pl
from jax.experimental.pallas import tpu as pltpu

f32 = jnp.float32
bf16 = jnp.bfloat16

S = 2048
D = 1024
DEPTH = 4
NSH = 4
FS = 704
PROJ = 2560
PS = 640
PW = 256
AW = 768
NPAIR = 6
NORM_EPS = 1e-6
MASK_VALUE = -1e30
ROPE_THETA = 500000.0
DILATIONS = (1, 4, 16)
QBLK = 128
NBLK = S // QBLK
TM = 512
EW_ROWS = 16
VMEM_LIMIT = 56 * 1024 * 1024

ADAM_LR = 0.001
ADAM_B1 = 0.9
ADAM_B2 = 0.999
ADAM_EPS = 1e-08
ADAM_WD = 0.01
ADAM_STEP = 10

MESH = pl.DeviceIdType.MESH
ANY = pl.BlockSpec(memory_space=pl.ANY)

BIG = (("g1", FS, D), ("u1", FS, D), ("d1", FS, D), ("wi", D, PS), ("wo", PW, D), ("g2", FS, D), ("u2", FS, D), ("d2", FS, D))
TRANSPOSED = ("ffn1_w_gate", "ffn1_w_up", "ffn2_w_gate", "ffn2_w_up")
BIG_SRC = {"g1": "ffn1_w_gate", "u1": "ffn1_w_up", "d1": "ffn1_w_down", "wi": "w_in", "wo": "w_out",
           "g2": "ffn2_w_gate", "u2": "ffn2_w_up", "d2": "ffn2_w_down"}


def _cp(*sem):
    return pltpu.CompilerParams(dimension_semantics=sem if sem else None, vmem_limit_bytes=VMEM_LIMIT)


def _sds(shape, dt):
    return jax.ShapeDtypeStruct(shape, dt)


def _dot(a, b):
    return jnp.dot(a, b, preferred_element_type=f32)


def _dot_nt(a, b):
    return lax.dot_general(a, b, (((1,), (1,)), ((), ())), preferred_element_type=f32)


def _dot_tn(a, b):
    return lax.dot_general(a, b, (((0,), (0,)), ((), ())), preferred_element_type=f32)


def _dep(dep):
    return ([], []) if dep is None else ([ANY], [dep])


def _resident(shape):
    return pl.BlockSpec(shape, lambda i: (0,) * len(shape), pipeline_mode=pl.Buffered(1))


def _ffn_fwd(x, g, wg, wu, wd, dep=None):
    def body(x_ref, g_ref, wg_ref, wu_ref, wd_ref, *rest):
        xo_ref, h_ref, a_ref, b_ref = rest[-4:]
        xf = x_ref[...]
        r = lax.rsqrt(jnp.mean(xf * xf, axis=-1, keepdims=True) + NORM_EPS)
        hh = ((xf * r) * g_ref[...]).astype(bf16)
        h_ref[...] = hh
        acc = None
        for s in range(NSH):
            a = _dot_nt(hh, wg_ref[s])
            b = _dot_nt(hh, wu_ref[s])
            a_ref[s] = a.astype(bf16)
            b_ref[s] = b.astype(bf16)
            p = _dot((a * (1.0 / (1.0 + jnp.exp(-a))) * b).astype(bf16), wd_ref[s])
            acc = p if acc is None else acc + p
        xo_ref[...] = xf + 0.5 * acc

    tok = pl.BlockSpec((TM, D), lambda i: (i, 0))
    hid = pl.BlockSpec((NSH, TM, FS), lambda i: (0, i, 0))
    wsp = _resident((NSH, FS, D))
    dspec, dop = _dep(dep)
    return pl.pallas_call(
        body, out_shape=(_sds((S, D), f32), _sds((S, D), bf16), _sds((NSH, S, FS), bf16), _sds((NSH, S, FS), bf16)),
        grid=(S // TM,), in_specs=[tok, pl.BlockSpec((1, D), lambda i: (0, 0)), wsp, wsp, wsp] + dspec,
        out_specs=(tok, tok, hid, hid), name="ffn_fwd", compiler_params=_cp("parallel"))(x, g, wg, wu, wd, *dop)


def _in_proj(x, g, wi):
    def body(x_ref, g_ref, w_ref, o_ref, h_ref):
        xf = x_ref[...]
        r = lax.rsqrt(jnp.mean(xf * xf, axis=-1, keepdims=True) + NORM_EPS)
        hh = ((xf * r) * g_ref[...]).astype(bf16)
        h_ref[...] = hh
        for s in range(NSH):
            o_ref[:, PS * s:PS * (s + 1)] = _dot(hh, w_ref[s])

    tok = pl.BlockSpec((TM, D), lambda i: (i, 0))
    return pl.pallas_call(
        body, out_shape=(_sds((S, PROJ), f32), _sds((S, D), bf16)), grid=(S // TM,),
        in_specs=[tok, pl.BlockSpec((1, D), lambda i: (0, 0)), _resident((NSH, D, PS))],
        out_specs=(pl.BlockSpec((TM, PROJ), lambda i: (i, 0)), tok), name="in_proj", compiler_params=_cp("parallel"))(x, g, wi)


def _out_proj(x, mixed, wo):
    def body(x_ref, m_ref, w_ref, o_ref):
        o_ref[...] = x_ref[...] + _dot(m_ref[...], w_ref[...].reshape(D, D))

    return pl.pallas_call(
        body, out_shape=_sds((S, D), f32), grid=(S // TM,),
        in_specs=[pl.BlockSpec((TM, D), lambda i: (i, 0)), pl.BlockSpec((TM, D), lambda i: (i, 0)),
                  pl.BlockSpec((NSH, PW, D), lambda i: (0, 0, 0))],
        out_specs=pl.BlockSpec((TM, D), lambda i: (i, 0)), name="out_proj", compiler_params=_cp("parallel"))(x, mixed, wo)


def _out_proj_bwd(dx, wo):
    def body(dx_ref, w_ref, o_ref):
        o_ref[...] = _dot_nt(dx_ref[...].astype(bf16), w_ref[...].reshape(D, D))

    return pl.pallas_call(
        body, out_shape=_sds((S, D), f32), grid=(S // TM,),
        in_specs=[pl.BlockSpec((TM, D), lambda i: (i, 0)), pl.BlockSpec((NSH, PW, D), lambda i: (0, 0, 0))],
        out_specs=pl.BlockSpec((TM, D), lambda i: (i, 0)), name="out_proj_bwd", compiler_params=_cp("parallel"))(dx, wo)


def _ffn_bwd_mid(dx, h, a, b, wd, dep=None):
    nt = S // TM

    def body(dx_ref, h_ref, a_ref, b_ref, wd_ref, *rest):
        da_ref, db_ref, dwd_ref, dwg_ref, dwu_ref, dy_s, u_s, da_s, db_s = rest[-9:]
        i = pl.program_id(1)
        rows = pl.ds(pl.multiple_of(i * TM, TM), TM)
        dy = (0.5 * dx_ref[...]).astype(bf16)
        dy_s[rows, :] = dy
        du = _dot_nt(dy, wd_ref[0])
        a = a_ref[0].astype(f32)
        b = b_ref[0].astype(f32)
        sig = 1.0 / (1.0 + jnp.exp(-a))
        silu = a * sig
        da = (du * b * (sig * (1.0 + a * (1.0 - sig)))).astype(bf16)
        db = (du * silu).astype(bf16)
        da_ref[0] = da
        db_ref[0] = db
        da_s[rows, :] = da
        db_s[rows, :] = db
        u_s[rows, :] = (silu * b).astype(bf16)

        @pl.when(i == nt - 1)
        def _():
            hh = h_ref[...]
            dwd_ref[...] = _dot_tn(u_s[...], dy_s[...]).astype(bf16).reshape(dwd_ref.shape)
            dwg_ref[...] = _dot_tn(da_s[...], hh).astype(bf16).reshape(dwg_ref.shape)
            dwu_ref[...] = _dot_tn(db_s[...], hh).astype(bf16).reshape(dwu_ref.shape)

    tok = pl.BlockSpec((TM, D), lambda s, i: (i, 0))
    hid = pl.BlockSpec((1, TM, FS), lambda s, i: (s, i, 0))
    wsp = pl.BlockSpec((1, 2, FS // 2, D), lambda s, i: (s, 0, 0, 0))
    hidden = _sds((NSH, S, FS), bf16)
    wgrad = _sds((NSH, 2, FS // 2, D), bf16)
    whole = pltpu.VMEM((S, FS), bf16)
    dspec, dop = _dep(dep)
    return pl.pallas_call(
        body, out_shape=(hidden, hidden, wgrad, wgrad, wgrad), grid=(NSH, nt),
        in_specs=[tok, pl.BlockSpec((S, D), lambda s, i: (0, 0), pipeline_mode=pl.Buffered(1)), hid, hid,
                  pl.BlockSpec((1, FS, D), lambda s, i: (s, 0, 0))] + dspec,
        out_specs=(hid, hid, wsp, wsp, wsp), scratch_shapes=[pltpu.VMEM((S, D), bf16), whole, whole, whole],
        name="ffn_bwd_mid", compiler_params=_cp("parallel", "arbitrary"))(dx, h, a, b, wd, *dop)


def _norm_bwd_tail(acc, x_ref, dxin_ref, g_ref, dxo_ref, dg_ref, first):
    xf = x_ref[...]
    r = lax.rsqrt(jnp.mean(xf * xf, axis=-1, keepdims=True) + NORM_EPS)
    xhat = xf * r
    dhg = acc * g_ref[...]
    dxo_ref[...] = dxin_ref[...] + r * (dhg - xhat * jnp.mean(dhg * xhat, axis=-1, keepdims=True))
    part = jnp.sum(acc * xhat, axis=0, keepdims=True)

    @pl.when(first)
    def _():
        dg_ref[...] = part

    @pl.when(jnp.logical_not(first))
    def _():
        dg_ref[...] += part


def _ffn_bwd_dx(dx, x_in, g, da, db, wg, wu):
    def body(dx_ref, x_ref, g_ref, da_ref, db_ref, wg_ref, wu_ref, dxo_ref, dg_ref):
        acc = None
        for s in range(NSH):
            p = _dot(da_ref[s], wg_ref[s])
            acc = p if acc is None else acc + p
            acc = acc + _dot(db_ref[s], wu_ref[s])
        _norm_bwd_tail(acc, x_ref, dx_ref, g_ref, dxo_ref, dg_ref, pl.program_id(0) == 0)

    tok = pl.BlockSpec((TM, D), lambda i: (i, 0))
    vec = pl.BlockSpec((1, D), lambda i: (0, 0))
    hid = pl.BlockSpec((NSH, TM, FS), lambda i: (0, i, 0))
    wsp = _resident((NSH, FS, D))
    return pl.pallas_call(
        body, out_shape=(_sds((S, D), f32), _sds((1, D), f32)), grid=(S // TM,),
        in_specs=[tok, tok, vec, hid, hid, wsp, wsp], out_specs=(tok, vec),
        name="ffn_bwd_dx", compiler_params=_cp("arbitrary"))(dx, x_in, g, da, db, wg, wu)


def _in_proj_bwd_dx(dx, x_in, g, dproj, wi):
    def body(dx_ref, x_ref, g_ref, dp_ref, w_ref, dxo_ref, dg_ref):
        acc = None
        for s in range(NSH):
            p = _dot_nt(dp_ref[:, PS * s:PS * (s + 1)], w_ref[s])
            acc = p if acc is None else acc + p
        _norm_bwd_tail(acc, x_ref, dx_ref, g_ref, dxo_ref, dg_ref, pl.program_id(0) == 0)

    tok = pl.BlockSpec((TM, D), lambda i: (i, 0))
    vec = pl.BlockSpec((1, D), lambda i: (0, 0))
    return pl.pallas_call(
        body, out_shape=(_sds((S, D), f32), _sds((1, D), f32)), grid=(S // TM,),
        in_specs=[tok, tok, vec, pl.BlockSpec((TM, PROJ), lambda i: (i, 0)), _resident((NSH, D, PS))], out_specs=(tok, vec),
        name="in_proj_bwd_dx", compiler_params=_cp("arbitrary"))(dx, x_in, g, dproj, wi)


def _dw(lhs, rhs, lhs_spec, rhs_spec, rows, cols, name, cast_rhs=False):
    def body(l_ref, r_ref, o_ref):
        r = r_ref[...].astype(bf16) if cast_rhs else r_ref[...]
        o_ref[...] = _dot_tn(l_ref[...], r).astype(bf16).reshape(1, 2, rows // 2, cols)

    return pl.pallas_call(
        body, out_shape=_sds((NSH, 2, rows // 2, cols), bf16), grid=(NSH,), in_specs=[lhs_spec, rhs_spec],
        out_specs=pl.BlockSpec((1, 2, rows // 2, cols), lambda s: (s, 0, 0, 0)), name=name, compiler_params=_cp("parallel"))(lhs, rhs)


_WHOLE_TOK = pl.BlockSpec((S, D), lambda s: (0, 0))


def _dw_in(h, dproj):
    return _dw(h, dproj, _WHOLE_TOK, pl.BlockSpec((S, PS), lambda s: (0, s)), D, PS, "dw_in")


def _dw_out(mixed, dx):
    return _dw(mixed, dx, pl.BlockSpec((S, PW), lambda s: (0, s)), _WHOLE_TOK, PW, D, "dw_out", cast_rhs=True)


def _final_loss(x, g, target):
    def body(x_ref, g_ref, t_ref, loss_ref, dx_ref, dg_ref):
        i = pl.program_id(0)
        xf = x_ref[...]
        r = lax.rsqrt(jnp.mean(xf * xf, axis=-1, keepdims=True) + NORM_EPS)
        xhat = xf * r
        err = xhat * g_ref[...] - t_ref[...]
        dy = err * (1.0 / D)
        dhg = dy * g_ref[...]
        dx_ref[...] = r * (dhg - xhat * jnp.mean(dhg * xhat, axis=-1, keepdims=True))
        part = jnp.sum(dy * xhat, axis=0, keepdims=True)
        lpart = jnp.zeros((8, 128), f32) + 0.5 * jnp.sum(jnp.mean(err * err, axis=-1, keepdims=True))

        @pl.when(i == 0)
        def _():
            dg_ref[...] = part
            loss_ref[...] = lpart

        @pl.when(i != 0)
        def _():
            dg_ref[...] += part
            loss_ref[...] += lpart

    tok = pl.BlockSpec((TM, D), lambda i: (i, 0))
    vec = pl.BlockSpec((1, D), lambda i: (0, 0))
    return pl.pallas_call(
        body, out_shape=(_sds((8, 128), f32), _sds((S, D), f32), _sds((1, D), f32)), grid=(S // TM,),
        in_specs=[tok, vec, tok], out_specs=(pl.BlockSpec((8, 128), lambda i: (0, 0)), tok, vec),
        name="final_loss", compiler_params=_cp("arbitrary"))(x, g, target)


def _shift_down(x, k, row):
    return jnp.where(row >= k, pltpu.roll(x, k, axis=0), 0.0)


def _shift_up(x, k, row):
    return jnp.where(row < S - k, pltpu.roll(x, S - k, axis=0), 0.0)


def _pool_geometry():
    row = lax.broadcasted_iota(jnp.int32, (S, PW), 0)
    grp = lax.broadcasted_iota(jnp.int32, (S, PW), 1) // 64
    half = jnp.where(grp == 0, 1, jnp.where(grp == 1, 2, jnp.where(grp == 2, 4, 8)))
    hi = jnp.minimum(row + half - 1, S - 1)
    lo = jnp.maximum(row - half, 0)
    return row, grp, (hi - lo + 1).astype(f32)


def _by_group(grp, v0, v1, v2, v3):
    return jnp.where(grp == 0, v0, jnp.where(grp == 1, v1, jnp.where(grp == 2, v2, v3)))


def _window_sums(x, row, grp, transpose):
    l1, r1 = x, x
    l2, r2 = l1 + _shift_down(l1, 1, row), r1 + _shift_up(r1, 1, row)
    l4, r4 = l2 + _shift_down(l2, 2, row), r2 + _shift_up(r2, 2, row)
    l8, r8 = l4 + _shift_down(l4, 4, row), r4 + _shift_up(r4, 4, row)
    lsel = _by_group(grp, l1, l2, l4, l8)
    rsel = _by_group(grp, r1, r2, r4, r8)
    if transpose:
        return lsel + _shift_up(rsel, 1, row)
    return _shift_down(lsel, 1, row) + rsel


def _pool_fwd(proj, wbd, scale):
    def body(v_ref, w_ref, sc_ref, mixed_ref, diff_ref):
        row, grp, cnt = _pool_geometry()
        v = v_ref[...]
        diff = (_window_sums(v, row, grp, False) / cnt - v).astype(bf16)
        diff_ref[...] = diff
        mixed_ref[...] = (_dot(diff, w_ref[...].astype(bf16)) * sc_ref[...]).astype(bf16)

    col = pl.BlockSpec((S, PW), lambda i: (0, 0))
    return pl.pallas_call(
        body, out_shape=(_sds((S, D), bf16), _sds((S, PW), bf16)), grid=(1,),
        in_specs=[col, pl.BlockSpec((PW, PW), lambda i: (0, 0)), pl.BlockSpec((1, PW), lambda i: (0, 0))],
        out_specs=(col, col), name="pool_fwd", compiler_params=_cp("arbitrary"))(proj, wbd, scale)


def _pool_bwd(dmixed, diff, wbd, scale, dproj):
    def body(dy_ref, diff_ref, w_ref, sc_ref, dproj_in, dv_ref, dw_ref, dsc_ref):
        del dproj_in
        row, grp, cnt = _pool_geometry()
        dy = dy_ref[...]
        diff = diff_ref[...]
        w = w_ref[...].astype(bf16)
        dsc_ref[...] = jnp.sum(dy * _dot(diff, w), axis=0, keepdims=True)
        dys = (dy * sc_ref[...]).astype(bf16)
        dw_ref[...] = _dot_tn(diff, dys)
        ddiff = _dot_nt(dys, w)
        dv_ref[...] = (_window_sums(ddiff / cnt, row, grp, True) - ddiff).astype(bf16)

    col = pl.BlockSpec((S, PW), lambda i: (0, 0))
    return pl.pallas_call(
        body, out_shape=(_sds((S, PROJ), bf16), _sds((PW, PW), f32), _sds((1, PW), f32)), grid=(1,),
        in_specs=[col, col, pl.BlockSpec((PW, PW), lambda i: (0, 0)), pl.BlockSpec((1, PW), lambda i: (0, 0)), ANY],
        out_specs=(col, pl.BlockSpec((PW, PW), lambda i: (0, 0)), pl.BlockSpec((1, PW), lambda i: (0, 0))),
        input_output_aliases={4: 0}, name="pool_bwd", compiler_params=_cp("arbitrary"))(dmixed, diff, wbd, scale, dproj)


def _rope_tables(pos_col, freq_row):
    def body(p_ref, f_ref, c_ref, a_ref, b_ref):
        ang = p_ref[...].astype(f32) * f_ref[...]
        l64 = lax.broadcasted_iota(jnp.int32, (S, 128), 1) % 64
        cos, sin = jnp.cos(ang), jnp.sin(ang)
        c_ref[...] = jnp.where(l64 < 16, cos, 1.0)
        a_ref[...] = jnp.where(l64 < 8, -sin, 0.0)
        b_ref[...] = jnp.where((l64 >= 8) & (l64 < 16), sin, 0.0)

    t = _sds((S, 128), f32)
    return pl.pallas_call(body, out_shape=(t, t, t), name="rope_tables", compiler_params=_cp())(pos_col, freq_row)


def _rope(t, c, a, b):
    return t * c + pltpu.roll(t, 120, axis=1) * a + pltpu.roll(t, 8, axis=1) * b


def _rope_bwd(g, c, a, b):
    return g * c + pltpu.roll(g * a, 8, axis=1) + pltpu.roll(g * b, 120, axis=1)


def _perm_load(ref, d):
    if d == 1:
        return ref[...]
    n = S // d
    return jnp.concatenate([ref[pl.ds(r, n, stride=d), :] for r in range(d)], axis=0)


def _unperm_store(ref, val, d):
    if d == 1:
        ref[...] = val
        return
    n = S // d
    for r in range(d):
        ref[pl.ds(r, n, stride=d), :] = val[r * n:(r + 1) * n, :]


def _band(xp):
    z = jnp.zeros((64, 128), bf16)
    p = jnp.concatenate([z, xp, z], axis=0).reshape(NBLK + 1, QBLK, 128)
    return jnp.concatenate([p[:NBLK], p[1:]], axis=1)


def _unband(xb):
    z = jnp.zeros((1, QBLK, 128), f32)
    p = jnp.concatenate([xb[:, :QBLK], z], axis=0) + jnp.concatenate([z, xb[:, QBLK:]], axis=0)
    return p.reshape(S + QBLK, 128)[64:S + 64]


def _band_mask(d):
    blocks_per_class = NBLK // d
    n = lax.broadcasted_iota(jnp.int32, (NBLK, 1, 2 * QBLK), 0) & (blocks_per_class - 1)
    be = lax.broadcasted_iota(jnp.int32, (NBLK, 1, 2 * QBLK), 2)
    a = lax.broadcasted_iota(jnp.int32, (1, 2 * QBLK, 2 * QBLK), 1) & (QBLK - 1)
    b = lax.broadcasted_iota(jnp.int32, (1, 2 * QBLK, 2 * QBLK), 2)
    band = (b >= a) & (b <= a + 128)
    edge = ((be >= 64) | (n != 0)) & ((be < QBLK + 64) | (n != blocks_per_class - 1))
    return band & edge


def _stack_heads(xb, lo):
    z = jnp.zeros_like(xb)
    return jnp.concatenate([jnp.where(lo, xb, z), jnp.where(lo, z, xb)], axis=1)


def _unstack_heads(x2, lo):
    return jnp.where(lo, x2[:, :QBLK], x2[:, QBLK:])


def _rows_to_lanes(col2, lo):
    return jnp.where(lo, jnp.broadcast_to(col2[:, :QBLK], (NBLK, QBLK, 128)), jnp.broadcast_to(col2[:, QBLK:], (NBLK, QBLK, 128)))


def _bmm_nt(a, b):
    return jnp.einsum('nqd,nkd->nqk', a, b, preferred_element_type=f32)


def _bmm_nn(a, b):
    return jnp.einsum('nqk,nkd->nqd', a, b, preferred_element_type=f32)


def _bmm_tn(a, b):
    return jnp.einsum('nqk,nqd->nkd', a, b, preferred_element_type=f32)


def _attn_fwd(proj, tc, ta, tb, mixed):
    def body(q_ref, k_ref, v_ref, c_ref, a_ref, b_ref, mixed_in, mixed_ref, o_ref, lse_ref, qn, kn, t_num, t_m, t_den):
        del mixed_in
        lo = lax.broadcasted_iota(jnp.int32, (1, 1, 128), 2) < 64
        c, a, b = c_ref[...], a_ref[...], b_ref[...]
        qn[...] = _rope(q_ref[...], c, a, b)
        kn[...] = _rope(k_ref[...], c, a, b)
        run = None
        for d in DILATIONS:
            q2 = _stack_heads(_perm_load(qn, d).astype(bf16).reshape(NBLK, QBLK, 128), lo)
            kb = _band(_perm_load(kn, d).astype(bf16))
            vb = _band(_perm_load(v_ref, d).astype(bf16))
            s = jnp.where(_band_mask(d), _bmm_nt(q2, kb) * 0.125, MASK_VALUE)
            m = jnp.max(s, axis=2, keepdims=True)
            p = jnp.exp(s - m)
            den = jnp.sum(p, axis=2, keepdims=True)
            num = _unstack_heads(_bmm_nn(p.astype(bf16), vb), lo)
            _unperm_store(t_num, num.reshape(S, 128), d)
            _unperm_store(t_m, _rows_to_lanes(m, lo).reshape(S, 128), d)
            _unperm_store(t_den, _rows_to_lanes(den, lo).reshape(S, 128), d)
            if run is None:
                run = (t_m[...], t_num[...], t_den[...])
            else:
                m_new = jnp.maximum(run[0], t_m[...])
                w_old, w_new = jnp.exp(run[0] - m_new), jnp.exp(t_m[...] - m_new)
                run = (m_new, w_old * run[1] + w_new * t_num[...], w_old * run[2] + w_new * t_den[...])
        out = run[1] / run[2]
        o_ref[...] = out
        mixed_ref[...] = out.astype(bf16)
        lse_ref[...] = run[0] + jnp.log(run[2])

    def col(off):
        return pl.BlockSpec((S, 128), lambda j, off=off: (0, off + j))

    tab = pl.BlockSpec((S, 128), lambda j: (0, 0))
    scr = pltpu.VMEM((S, 128), f32)
    return pl.pallas_call(
        body, out_shape=(_sds((S, D), bf16), _sds((S, AW), f32), _sds((S, AW), f32)), grid=(NPAIR,),
        in_specs=[col(2), col(8), col(14), tab, tab, tab, ANY], out_specs=(col(2), col(0), col(0)),
        scratch_shapes=[scr, scr, scr, scr, scr], input_output_aliases={6: 0}, name="attn_fwd",
        compiler_params=_cp("arbitrary"))(proj, proj, proj, tc, ta, tb, mixed)


def _attn_bwd(proj, tc, ta, tb, o, lse, dmixed):
    def body(q_ref, k_ref, v_ref, c_ref, a_ref, b_ref, o_ref, lse_ref, do_ref, dp_ref, qn, kn, tmp, dk_s, dv_s):
        t = pl.program_id(1)

        @pl.when(t == 0)
        def _():
            lo = lax.broadcasted_iota(jnp.int32, (1, 1, 128), 2) < 64
            c, a, b = c_ref[...], a_ref[...], b_ref[...]
            qn[...] = _rope(q_ref[...], c, a, b)
            kn[...] = _rope(k_ref[...], c, a, b)
            dq = dk = dv = None
            for d in DILATIONS:
                q2 = _stack_heads(_perm_load(qn, d).astype(bf16).reshape(NBLK, QBLK, 128), lo)
                kb = _band(_perm_load(kn, d).astype(bf16))
                vb = _band(_perm_load(v_ref, d).astype(bf16))
                dob = _perm_load(do_ref, d).reshape(NBLK, QBLK, 128)
                ob = _perm_load(o_ref, d).reshape(NBLK, QBLK, 128)
                lsb = _perm_load(lse_ref, d).reshape(NBLK, QBLK, 128)
                do2 = _stack_heads(dob.astype(bf16), lo)
                delta2 = jnp.sum(_stack_heads(dob * ob, lo), axis=2, keepdims=True)
                lse2 = jnp.max(jnp.concatenate([jnp.where(lo, lsb, MASK_VALUE), jnp.where(lo, MASK_VALUE, lsb)], axis=1),
                               axis=2, keepdims=True)
                s = _bmm_nt(q2, kb) * 0.125
                p = jnp.where(_band_mask(d), jnp.exp(s - lse2), 0.0)
                ds = (p * (_bmm_nt(do2, vb) - delta2) * 0.125).astype(bf16)
                pb = p.astype(bf16)
                dq_b = _unstack_heads(_bmm_nn(ds, kb), lo).reshape(S, 128)
                dk_b = _unband(_bmm_tn(ds, q2))
                dv_b = _unband(_bmm_tn(pb, do2))
                acc = []
                for prev, new in ((dq, dq_b), (dk, dk_b), (dv, dv_b)):
                    _unperm_store(tmp, new, d)
                    acc.append(tmp[...] if prev is None else prev + tmp[...])
                dq, dk, dv = acc
            dp_ref[...] = _rope_bwd(dq, c, a, b).astype(bf16)
            dk_s[...] = _rope_bwd(dk, c, a, b).astype(bf16)
            dv_s[...] = dv.astype(bf16)

        @pl.when(t == 1)
        def _():
            dp_ref[...] = dk_s[...]

        @pl.when(t == 2)
        def _():
            dp_ref[...] = dv_s[...]

    def col(off):
        return pl.BlockSpec((S, 128), lambda j, t, off=off: (0, off + j))

    tab = pl.BlockSpec((S, 128), lambda j, t: (0, 0))
    scr = pltpu.VMEM((S, 128), f32)
    scb = pltpu.VMEM((S, 128), bf16)
    return pl.pallas_call(
        body, out_shape=_sds((S, PROJ), bf16), grid=(NPAIR, 3),
        in_specs=[col(2), col(8), col(14), tab, tab, tab, col(0), col(0), col(2)],
        out_specs=pl.BlockSpec((S, 128), lambda j, t: (0, 2 + NPAIR * t + j)),
        scratch_shapes=[scr, scr, scr, scb, scb], name="attn_bwd",
        compiler_params=_cp("arbitrary", "arbitrary"))(proj, proj, proj, tc, ta, tb, o, lse, dmixed)


def _block_diag(w4):
    out = jnp.zeros((PW, PW), w4.dtype)
    for g in range(4):
        out = out.at[64 * g:64 * (g + 1), 64 * g:64 * (g + 1)].set(w4[g])
    return out


def _diag_blocks(w):
    return jnp.stack([w[64 * g:64 * (g + 1), 64 * g:64 * (g + 1)] for g in range(4)])


def _rope_inputs(positions):
    inv_freq = ROPE_THETA ** (-jnp.arange(0, 16, 2, dtype=f32) / 16)
    l64 = np.arange(128) % 64
    idx = np.where(l64 < 16, l64 % 8, 0)
    return positions.reshape(S, 1), inv_freq[idx].reshape(1, 128)


def _layer_fwd(x, w, small, l, tabs, dep=None):
    g1, gm, g2 = (small[k][l].reshape(1, D) for k in ("ffn1_norm", "mix_norm", "ffn2_norm"))
    wbd = _block_diag(small["pool_w"][l])
    psc = small["pool_scale"][l].reshape(1, PW)
    x1, h1, a1, b1 = _ffn_fwd(x, g1, w["g1"], w["u1"], w["d1"], dep)
    proj, h2 = _in_proj(x1, gm, w["wi"])
    mixed, diff = _pool_fwd(proj, wbd, psc)
    mixed, o, lse = _attn_fwd(proj, *tabs, mixed)
    x2 = _out_proj(x1, mixed, w["wo"])
    out, h3, a2, b2 = _ffn_fwd(x2, g2, w["g2"], w["u2"], w["d2"])
    return out, dict(x0=x, h1=h1, a1=a1, b1=b1, x1=x1, h2=h2, proj=proj, mixed=mixed, diff=diff, o=o, lse=lse,
                     x2=x2, h3=h3, a2=a2, b2=b2, g1=g1, gm=gm, g2=g2, wbd=wbd, psc=psc)


def _layer_bwd(dx, w, sv, tabs, dep=None):
    gr, sg = {}, {}
    da, db, gr["d2"], gr["g2"], gr["u2"] = _ffn_bwd_mid(dx, sv["h3"], sv["a2"], sv["b2"], w["d2"], dep)
    dx, sg["ffn2_norm"] = _ffn_bwd_dx(dx, sv["x2"], sv["g2"], da, db, w["g2"], w["u2"])
    gr["wo"] = _dw_out(sv["mixed"], dx)
    dmixed = _out_proj_bwd(dx, w["wo"])
    dproj = _attn_bwd(sv["proj"], *tabs, sv["o"], sv["lse"], dmixed)
    dproj, dwbd, sg["pool_scale"] = _pool_bwd(dmixed, sv["diff"], sv["wbd"], sv["psc"], dproj)
    sg["pool_w"] = _diag_blocks(dwbd)
    gr["wi"] = _dw_in(sv["h2"], dproj)
    dx, sg["mix_norm"] = _in_proj_bwd_dx(dx, sv["x1"], sv["gm"], dproj, w["wi"])
    da, db, gr["d1"], gr["g1"], gr["u1"] = _ffn_bwd_mid(dx, sv["h1"], sv["a1"], sv["b1"], w["d1"])
    dx, sg["ffn1_norm"] = _ffn_bwd_dx(dx, sv["x0"], sv["g1"], da, db, w["g1"], w["u1"])
    return dx, gr, sg


def _forward_backward(x, positions, target, gathered, small):
    tabs = _rope_tables(*_rope_inputs(positions))
    saved = []
    for l in range(DEPTH):
        x, sv = _layer_fwd(x, gathered[l], small, l, tabs)
        saved.append(sv)
    loss, dx, dgf = _final_loss(x, small["final_norm"].reshape(1, D), target)
    big = [None] * DEPTH
    sg = {k: [None] * DEPTH for k in ("ffn1_norm", "mix_norm", "pool_w", "pool_scale", "ffn2_norm")}
    for l in reversed(range(DEPTH)):
        dx, big[l], sgl = _layer_bwd(dx, gathered[l], saved[l], tabs)
        for k, v in sgl.items():
            sg[k][l] = v
    sg["final_norm"] = dgf
    return loss, dx, big, sg


def _place():
    x, y, c = lax.axis_index("x"), lax.axis_index("y"), lax.axis_index("c")
    chips = [(1 - x, y), (x, 1 - y), (1 - x, 1 - y)]
    return x, y, c, chips


def _cast_layer(params, l, place, dep=None):
    def body(p_ref, *refs):
        del p_ref
        for i_ref, o_ref in zip(refs[:8], refs[-8:]):
            o_ref[...] = i_ref[...].astype(bf16).reshape(o_ref.shape)

    ins, in_specs, out_shape, out_specs = [], [], [], []
    for name, rows, cols in BIG:
        q = rows // 4
        ins.append(params[BIG_SRC[name]])
        in_specs.append(pl.BlockSpec((1, q, cols), lambda i, p, l=l: (l, i, 0)))
        out_shape.append(_sds((NSH, 2, rows // 2, cols), bf16))
        out_specs.append(pl.BlockSpec((1, 1, q, cols), lambda i, p: (p[1], i // 2, i % 2, 0)))
    dspec, dop = _dep(dep)
    return pl.pallas_call(
        body, out_shape=out_shape,
        grid_spec=pltpu.PrefetchScalarGridSpec(num_scalar_prefetch=1, grid=(4,), in_specs=in_specs + dspec, out_specs=out_specs),
        name=f"cast_layer{l}", compiler_params=_cp("parallel"))(place, *ins, *dop)


HBM = pl.BlockSpec(memory_space=pltpu.HBM)
SEM = pl.BlockSpec(memory_space=pltpu.SEMAPHORE)
_SPLIT = pltpu.CompilerParams(has_side_effects=pltpu.SideEffectType.DATAFLOW_SIDE_EFFECTING)


def _hbm(arrays):
    return [pltpu.with_memory_space_constraint(a, pltpu.HBM) for a in arrays]


def _chip_copies(src_of, dst_of, send_sems, recv_sems, n):
    x, y, c, chips = _place()
    me = 2 * x + y
    out = []
    for t in range(n):
        for k, chip in enumerate(chips):
            peer = 2 * chip[0] + chip[1]
            send = pltpu.make_async_remote_copy(
                src_ref=src_of(t, peer), dst_ref=dst_of(t, me), send_sem=send_sems.at[3 * t + k], recv_sem=recv_sems.at[3 * t + k],
                device_id=(chip[0], chip[1], c), device_id_type=MESH)
            land = pltpu.make_async_remote_copy(
                src_ref=src_of(t, peer), dst_ref=dst_of(t, peer), send_sem=send_sems.at[3 * t + k], recv_sem=recv_sems.at[3 * t + k],
                device_id=(chip[0], chip[1], c), device_id_type=MESH)
            out.append((send, land))
    return out


def _exchange_start(src, land, after, src_of, dst_of, name):
    n, m = len(src), len(src) + len(land)

    def body(*refs):
        src_refs = refs[:n]
        land_refs = refs[n:m] if land else src_refs
        send_sems, recv_sems = refs[m + 1], refs[m + 2]
        token = refs[-1]
        for send, _ in _chip_copies(lambda t, s: src_of(src_refs[t], s), lambda t, s: dst_of(land_refs[t], s), send_sems, recv_sems, n):
            send.start()
        token[...] = jnp.zeros_like(token)

    arrays = list(src) + list(land)
    out_shape = ([pltpu.SemaphoreType.DMA((3 * n,)), pltpu.SemaphoreType.DMA((3 * n,))] + [pltpu.HBM(a.shape, a.dtype) for a in arrays]
                 + [_sds((8, 128), f32)])
    res = pl.pallas_call(
        body, out_shape=out_shape, in_specs=[HBM] * m + [ANY], out_specs=[SEM, SEM] + [HBM] * m + [pl.BlockSpec(memory_space=pltpu.VMEM)],
        input_output_aliases={i: 2 + i for i in range(m)}, name=name, compiler_params=_SPLIT)(*_hbm(arrays), after)
    return res[0], res[1], list(res[2:2 + n]), list(res[2 + n:2 + m]), res[-1]


def _exchange_wait(send_sems, recv_sems, src, land, after, src_of, dst_of, name):
    n, m = len(src), len(src) + len(land)

    def body(*refs):
        src_refs = refs[:n]
        land_refs = refs[n:m] if land else src_refs
        send_sems, recv_sems = refs[m], refs[m + 1]
        for send, land_cp in _chip_copies(lambda t, s: src_of(src_refs[t], s), lambda t, s: dst_of(land_refs[t], s), send_sems, recv_sems, n):
            send.wait_send()
            land_cp.wait_recv()

    arrays = list(src) + list(land)
    res = pl.pallas_call(
        body, out_shape=[pltpu.HBM(a.shape, a.dtype) for a in arrays], in_specs=[HBM] * m + [SEM, SEM, ANY], out_specs=[HBM] * m,
        input_output_aliases={i: i for i in range(m)}, name=name, compiler_params=_SPLIT)(*arrays, send_sems, recv_sems, after)
    return list(res[:n]), list(res[n:])


def _own_half(ref, s):
    x, y, c, _ = _place()
    return ref.at[2 * x + y, c]


def _slot_half(ref, s):
    return ref.at[s, lax.axis_index("c")]


def _slot(ref, s):
    return ref.at[s]


def _gather_forward(bufs):
    n = len(bufs)

    def body(*refs):
        outs = refs[n:2 * n]
        send_sems, recv_sems = refs[2 * n:]
        x, y, c, chips = _place()
        sibling = (x, y, 1 - c)
        passed = []
        for t in range(n):
            for k, chip in enumerate(chips):
                blk = outs[t].at[2 * chip[0] + chip[1], c]
                cp = pltpu.make_async_remote_copy(
                    src_ref=blk, dst_ref=blk, send_sem=send_sems.at[t, k], recv_sem=recv_sems.at[t, k],
                    device_id=sibling, device_id_type=MESH)
                cp.start()
                passed.append(cp)
        for t in range(n):
            for k, chip in enumerate(chips):
                blk = outs[t].at[2 * chip[0] + chip[1], 1 - c]
                pltpu.make_async_remote_copy(
                    src_ref=blk, dst_ref=blk, send_sem=send_sems.at[t, k], recv_sem=recv_sems.at[t, k],
                    device_id=sibling, device_id_type=MESH).wait_recv()
        for cp in passed:
            cp.wait_send()

    out_shape = [_sds(a.shape, bf16) for a in bufs]
    return pl.pallas_call(
        body, out_shape=out_shape, in_specs=[ANY] * n, out_specs=[ANY] * n, input_output_aliases={t: t for t in range(n)},
        scratch_shapes=[pltpu.SemaphoreType.DMA((n, 3)), pltpu.SemaphoreType.DMA((n, 3))], name="gather_forward")(*bufs)


def _sibling_swap(grads):
    n = len(grads)

    def body(*refs):
        ins, outs = refs[:n], refs[n:2 * n]
        send_sems, recv_sems = refs[2 * n:]
        x, y, c, _ = _place()
        cps = []
        for t in range(n):
            for s in range(NSH):
                cp = pltpu.make_async_remote_copy(
                    src_ref=ins[t].at[s, 1 - c], dst_ref=outs[t].at[s], send_sem=send_sems.at[t, s], recv_sem=recv_sems.at[t, s],
                    device_id=(x, y, 1 - c), device_id_type=MESH)
                cp.start()
                cps.append(cp)
        for cp in cps:
            cp.wait()

    out_shape = [_sds((NSH,) + a.shape[2:], bf16) for a in grads]
    return pl.pallas_call(
        body, out_shape=out_shape, in_specs=[ANY] * n, out_specs=[ANY] * n,
        scratch_shapes=[pltpu.SemaphoreType.DMA((n, NSH)), pltpu.SemaphoreType.DMA((n, NSH))],
        name="sibling_swap")(*grads)


def _row_tile(h):
    return h // 2 if h % 32 == 0 else h


def _pair_sum(grads, got, c_idx):
    n = len(grads)

    def body(c_ref, *refs):
        del c_ref
        for t in range(n):
            refs[2 * n + t][...] = (refs[t][...].astype(f32).reshape(refs[n + t].shape) + refs[n + t][...].astype(f32)).astype(bf16)

    in_specs, out_shape, out_specs = [], [], []
    for a in grads:
        h, cols = a.shape[2:]
        in_specs.append(pl.BlockSpec((1, 1, _row_tile(h), cols), lambda s, i, c: (s, c[0], i, 0)))
    for a in grads:
        h, cols = a.shape[2:]
        in_specs.append(pl.BlockSpec((1, _row_tile(h), cols), lambda s, i, c: (s, i, 0)))
        out_shape.append(_sds((NSH, h, cols), bf16))
        out_specs.append(pl.BlockSpec((1, _row_tile(h), cols), lambda s, i, c: (s, i, 0)))
    return pl.pallas_call(
        body, out_shape=out_shape,
        grid_spec=pltpu.PrefetchScalarGridSpec(num_scalar_prefetch=1, grid=(NSH, 2), in_specs=in_specs, out_specs=out_specs),
        name="pair_sum", compiler_params=_cp("parallel", "parallel"))(c_idx, *grads, *got)


def _chip_sum(psum, parts, full, place, l, name):
    n = len(parts)

    def body(p_ref, *refs):
        s = pl.program_id(1)
        for t in range(n):
            val = jnp.where(s == p_ref[1], refs[t][0], refs[n + t][0]).astype(f32)
            out = refs[3 * n + t]

            @pl.when(s == 0)
            def _(out=out, val=val):
                out[0, 0] = val

            @pl.when(s != 0)
            def _(out=out, val=val):
                out[0, 0] += val

    own_specs, part_specs, out_shape, out_specs = [], [], [], []
    for a, fl in zip(parts, full):
        _, h, cols = a.shape
        r = _row_tile(h)
        own_specs.append(pl.BlockSpec((1, r, cols), lambda i, s, p: (p[1], i, 0)))
        part_specs.append(pl.BlockSpec((1, r, cols), lambda i, s, p: (jnp.where(s == p[1], (s + 1) % NSH, s), i, 0)))
        out_shape.append(_sds(fl.shape, f32))
        out_specs.append(pl.BlockSpec((1, 1, r, cols), lambda i, s, p, l=l: (l, p[0], i, 0)))
    return pl.pallas_call(
        body, out_shape=out_shape,
        grid_spec=pltpu.PrefetchScalarGridSpec(num_scalar_prefetch=1, grid=(2, NSH), in_specs=own_specs + part_specs + [ANY] * n,
                                               out_specs=out_specs),
        input_output_aliases={1 + 2 * n + t: t for t in range(n)}, name=name,
        compiler_params=_cp("parallel", "arbitrary"))(place, *psum, *parts, *full)


def _sibling_share(full, l, name):
    n = len(full)

    def body(*refs):
        outs = refs[n:2 * n]
        send_sems, recv_sems = refs[2 * n:]
        x, y, c, _ = _place()
        sibling = (x, y, 1 - c)
        cps = []
        for t in range(n):
            blk = outs[t].at[l, c]
            cp = pltpu.make_async_remote_copy(
                src_ref=blk, dst_ref=blk, send_sem=send_sems.at[t], recv_sem=recv_sems.at[t], device_id=sibling, device_id_type=MESH)
            cp.start()
            cps.append(cp)
        for t in range(n):
            blk = outs[t].at[l, 1 - c]
            pltpu.make_async_remote_copy(
                src_ref=blk, dst_ref=blk, send_sem=send_sems.at[t], recv_sem=recv_sems.at[t],
                device_id=sibling, device_id_type=MESH).wait_recv()
        for cp in cps:
            cp.wait_send()

    out_shape = [_sds(a.shape, f32) for a in full]
    return pl.pallas_call(
        body, out_shape=out_shape, in_specs=[ANY] * n, out_specs=[ANY] * n, input_output_aliases={t: t for t in range(n)},
        scratch_shapes=[pltpu.SemaphoreType.DMA((n,)), pltpu.SemaphoreType.DMA((n,))], name=name)(*full)


SMALL_ROWS = 656


def _pack_small(per_layer, final_vec, loss_tile):
    rows = []
    for l in range(DEPTH):
        for k in ("ffn1_norm", "mix_norm", "ffn2_norm"):
            rows.append(per_layer[k][l].reshape(8, 128))
        rows.append(per_layer["pool_w"][l].reshape(128, 128))
        rows.append(jnp.pad(per_layer["pool_scale"][l].reshape(2, 128), ((0, 6), (0, 0))))
    rows.append(final_vec.reshape(8, 128))
    rows.append(loss_tile)
    return jnp.concatenate(rows, axis=0)


def _unpack_small(buf):
    out = {k: [] for k in ("ffn1_norm", "mix_norm", "ffn2_norm", "pool_w", "pool_scale")}
    r = 0
    for l in range(DEPTH):
        for k in ("ffn1_norm", "mix_norm", "ffn2_norm"):
            out[k].append(buf[r:r + 8].reshape(D))
            r += 8
        out["pool_w"].append(buf[r:r + 128].reshape(4, 64, 64))
        r += 128
        out["pool_scale"].append(buf[r:r + 2].reshape(PW))
        r += 8
    res = {k: jnp.stack(v) for k, v in out.items()}
    res["final_norm"] = buf[r:r + 8].reshape(D)
    res["loss"] = buf[r + 8, 0]
    return res


def _allreduce_small(buf):
    def body(in_ref, out_ref, slots, send_sems, recv_sems):
        x, y, c, _ = _place()
        me = 4 * x + 2 * y + c
        slots[me] = in_ref[...]
        peers = []
        for k in range(1, 8):
            px, py, pc = x ^ (k >> 2), y ^ ((k >> 1) & 1), c ^ (k & 1)
            cp = pltpu.make_async_remote_copy(
                src_ref=in_ref, dst_ref=slots.at[me], send_sem=send_sems.at[k - 1], recv_sem=recv_sems.at[k - 1],
                device_id=(px, py, pc), device_id_type=MESH)
            cp.start()
            peers.append(cp)
        for k in range(1, 8):
            px, py, pc = x ^ (k >> 2), y ^ ((k >> 1) & 1), c ^ (k & 1)
            slot = 4 * px + 2 * py + pc
            pltpu.make_async_remote_copy(
                src_ref=slots.at[slot], dst_ref=slots.at[slot], send_sem=send_sems.at[k - 1], recv_sem=recv_sems.at[k - 1],
                device_id=(px, py, pc), device_id_type=MESH).wait_recv()
        for cp in peers:
            cp.wait_send()
        acc = slots[0]
        for j in range(1, 8):
            acc = acc + slots[j]
        out_ref[...] = acc

    return pl.pallas_call(
        body, out_shape=_sds((SMALL_ROWS, 128), f32),
        in_specs=[pl.BlockSpec(memory_space=pltpu.VMEM)], out_specs=pl.BlockSpec(memory_space=pltpu.VMEM),
        scratch_shapes=[pltpu.VMEM((8, SMALL_ROWS, 128), f32), pltpu.SemaphoreType.DMA((7,)), pltpu.SemaphoreType.DMA((7,))],
        name="allreduce_small", compiler_params=_cp())(buf)


def _adamw_math(w, g, m, v):
    m = ADAM_B1 * m + (1.0 - ADAM_B1) * g
    v = ADAM_B2 * v + (1.0 - ADAM_B2) * (g * g)
    m_hat = m / (1.0 - ADAM_B1 ** ADAM_STEP)
    v_hat = v / (1.0 - ADAM_B2 ** ADAM_STEP)
    return -ADAM_LR * (m_hat / (jnp.sqrt(v_hat) + ADAM_EPS) + ADAM_WD * w), m, v


def _adamw(w, g, m, v, name, first=0, prev=None, dep=None):
    def body(w_ref, g_ref, m_ref, v_ref, *rest):
        go_ref, d_ref, mo_ref, vo_ref = rest[-4:]
        g = g_ref[...]
        d, mn, vn = _adamw_math(w_ref[...], g, m_ref[...], v_ref[...])
        go_ref[...] = g
        d_ref[...] = d
        mo_ref[...] = mn
        vo_ref[...] = vn

    _, rows, cols = w.shape
    r = rows // 4 if rows % 32 == 0 else rows
    spec = pl.BlockSpec((1, r, cols), lambda i, j: (first + i, j, 0))
    gspec = pl.BlockSpec((1, r, cols), lambda i, j: (i, j, 0))
    out = _sds(w.shape, f32)
    extra = [] if prev is None else list(prev)
    dspec, dop = _dep(dep)
    return pl.pallas_call(
        body, out_shape=(out, out, out, out), grid=(g.shape[0], rows // r), in_specs=[spec, gspec, spec, spec] + [ANY] * len(extra) + dspec,
        out_specs=(spec,) * 4, input_output_aliases={4 + i: i for i in range(len(extra))}, name=name,
        compiler_params=_cp("parallel", "parallel"))(w, g, m, v, *extra, *dop)


SMALL_NAMES = ("ffn1_norm", "mix_norm", "pool_w", "pool_scale", "ffn2_norm", "final_norm")
WEIGHT_ORDER = ("ffn1_norm", "ffn1_w_gate", "ffn1_w_up", "ffn1_w_down", "mix_norm", "w_in", "pool_w", "pool_scale", "w_out",
                "ffn2_norm", "ffn2_w_gate", "ffn2_w_up", "ffn2_w_down", "final_norm")


def _pack_small_params(p):
    per_layer = {k: [p[k][l] for l in range(DEPTH)] for k in ("ffn1_norm", "mix_norm", "ffn2_norm", "pool_w", "pool_scale")}
    return _pack_small(per_layer, p["final_norm"], jnp.zeros((8, 128), f32))


def kernel(x, positions, ffn1_norm, ffn1_w_gate, ffn1_w_up, ffn1_w_down, mix_norm, w_in, pool_w, pool_scale, w_out, ffn2_norm, ffn2_w_gate, ffn2_w_up, ffn2_w_down, final_norm, loss_target, m_ffn1_norm, m_ffn1_w_gate, m_ffn1_w_up, m_ffn1_w_down, m_mix_norm, m_w_in, m_pool_w, m_pool_scale, m_w_out, m_ffn2_norm, m_ffn2_w_gate, m_ffn2_w_up, m_ffn2_w_down, m_final_norm, v_ffn1_norm, v_ffn1_w_gate, v_ffn1_w_up, v_ffn1_w_down, v_mix_norm, v_w_in, v_pool_w, v_pool_scale, v_w_out, v_ffn2_norm, v_ffn2_w_gate, v_ffn2_w_up, v_ffn2_w_down, v_final_norm):
    params = dict(ffn1_norm=ffn1_norm, ffn1_w_gate=ffn1_w_gate, ffn1_w_up=ffn1_w_up, ffn1_w_down=ffn1_w_down,
                  mix_norm=mix_norm, w_in=w_in, pool_w=pool_w, pool_scale=pool_scale, w_out=w_out, ffn2_norm=ffn2_norm,
                  ffn2_w_gate=ffn2_w_gate, ffn2_w_up=ffn2_w_up, ffn2_w_down=ffn2_w_down, final_norm=final_norm)
    mom_m = dict(ffn1_norm=m_ffn1_norm, ffn1_w_gate=m_ffn1_w_gate, ffn1_w_up=m_ffn1_w_up, ffn1_w_down=m_ffn1_w_down,
                 mix_norm=m_mix_norm, w_in=m_w_in, pool_w=m_pool_w, pool_scale=m_pool_scale, w_out=m_w_out,
                 ffn2_norm=m_ffn2_norm, ffn2_w_gate=m_ffn2_w_gate, ffn2_w_up=m_ffn2_w_up, ffn2_w_down=m_ffn2_w_down,
                 final_norm=m_final_norm)
    mom_v = dict(ffn1_norm=v_ffn1_norm, ffn1_w_gate=v_ffn1_w_gate, ffn1_w_up=v_ffn1_w_up, ffn1_w_down=v_ffn1_w_down,
                 mix_norm=v_mix_norm, w_in=v_w_in, pool_w=v_pool_w, pool_scale=v_pool_scale, w_out=v_w_out,
                 ffn2_norm=v_ffn2_norm, ffn2_w_gate=v_ffn2_w_gate, ffn2_w_up=v_ffn2_w_up, ffn2_w_down=v_ffn2_w_down,
                 final_norm=v_final_norm)
    names = [t[0] for t in BIG]
    for d in (params, mom_m, mom_v):
        for k in TRANSPOSED:
            d[k] = jnp.swapaxes(d[k], 1, 2)

    place = jnp.stack([lax.axis_index("c"), 2 * lax.axis_index("x") + lax.axis_index("y")]).astype(jnp.int32)
    def gather_start(l, cast, after):
        return _exchange_start(cast, [], after, _own_half, _slot_half, f"gather_start{l}")

    def gather_end(started, after, l):
        send_sems, recv_sems, bufs, _, _ = started
        bufs, _ = _exchange_wait(send_sems, recv_sems, bufs, [], after, _own_half, _slot_half, f"gather_wait{l}")
        return {nm: a.reshape(NSH, rows, cols) for (nm, rows, cols), a in zip(BIG, _gather_forward(bufs))}

    tabs = _rope_tables(*_rope_inputs(positions))
    h = x.reshape(S, D)
    weights, saved = [], []
    started = gather_start(0, _cast_layer(params, 0, place), place)
    after = started[-1]
    casts = {}
    for l in range(1, DEPTH):
        casts[l] = _cast_layer(params, l, place, after)
        after = casts[l][0]
    for l in range(DEPTH):
        weights.append(gather_end(started, after, l))
        dep = None
        if l + 1 < DEPTH:
            started = gather_start(l + 1, casts[l + 1], weights[l]["g1"])
            dep = started[-1]
        h, sv = _layer_fwd(h, weights[l], params, l, tabs, dep)
        saved.append(sv)
        after = h
    loss, dx, dgf = _final_loss(h, final_norm.reshape(1, D), loss_target.reshape(S, D))

    upper = [lax.empty((DEPTH - 1, 2, rows // 2, cols), f32) for _, rows, cols in BIG]
    lower = [lax.empty((1, 2, rows // 2, cols), f32) for _, rows, cols in BIG]
    sg = {k: [None] * DEPTH for k in ("ffn1_norm", "mix_norm", "pool_w", "pool_scale", "ffn2_norm")}
    sg["final_norm"] = dgf

    def reduce_end(started, after, l, full, slot):
        send_sems, recv_sems, psum, parts, _ = started
        psum, parts = _exchange_wait(send_sems, recv_sems, psum, parts, after, _slot, _slot, f"grad_wait{l}")
        return _sibling_share(_chip_sum(psum, parts, full, place, slot, f"chip_sum{l}"), slot, f"sibling_share{l}")

    started, dep = None, None
    for l in reversed(range(DEPTH)):
        dx, gr, sgl = _layer_bwd(dx, weights[l], saved[l], tabs, dep)
        for k, v in sgl.items():
            sg[k][l] = v
        if started is not None:
            upper = reduce_end(started, dx, l + 1, upper, l)
        grads = [gr[nm] for nm in names]
        psum = _pair_sum(grads, _sibling_swap(grads), place)
        parts = [lax.empty(a.shape, bf16) for a in psum]
        started = _exchange_start(psum, parts, place, _slot, _slot, f"grad_start{l}")
        dep = started[-1]

    big_out = {}
    for (nm, rows, cols), g in zip(BIG, upper):
        k = BIG_SRC[nm]
        big_out[k] = _adamw(params[k], g.reshape(DEPTH - 1, rows, cols), mom_m[k], mom_v[k], "adamw_upper_" + k, first=1, dep=dep)
        dep = big_out[k][1]
    lower = reduce_end(started, dep, 0, lower, 0)
    for (nm, rows, cols), g in zip(BIG, lower):
        k = BIG_SRC[nm]
        big_out[k] = _adamw(params[k], g.reshape(1, rows, cols), mom_m[k], mom_v[k], "adamw_lower_" + k, first=0, prev=big_out[k])

    per_layer = {k: sg[k] for k in ("ffn1_norm", "mix_norm", "ffn2_norm", "pool_w", "pool_scale")}
    small_sum = _allreduce_small(_pack_small(per_layer, sg["final_norm"], loss))
    gs, ds_, ms, vs = _adamw(_pack_small_params(params).reshape(1, SMALL_ROWS, 128), small_sum.reshape(1, SMALL_ROWS, 128),
                             _pack_small_params(mom_m).reshape(1, SMALL_ROWS, 128),
                             _pack_small_params(mom_v).reshape(1, SMALL_ROWS, 128), "adamw_small")
    small_out = [_unpack_small(a.reshape(SMALL_ROWS, 128)) for a in (gs, ds_, ms, vs)]

    grad, delta, new_m, new_v = {}, {}, {}, {}
    for k in WEIGHT_ORDER:
        if k in SMALL_NAMES:
            grad[k], delta[k], new_m[k], new_v[k] = (so[k] for so in small_out)
        else:
            grad[k], delta[k], new_m[k], new_v[k] = big_out[k]
    for d in (grad, delta, new_m, new_v):
        for k in TRANSPOSED:
            d[k] = jnp.swapaxes(d[k], 1, 2)
    return (small_out[0]["loss"], dx.reshape(1, S, D), *[grad[k] for k in WEIGHT_ORDER], *[delta[k] for k in WEIGHT_ORDER],
            *[new_m[k] for k in WEIGHT_ORDER], *[new_v[k] for k in WEIGHT_ORDER])
```

```python
import functools

import jax
import jax.numpy as jnp
import numpy as np
from jax import lax
from jax.experimental import pallas as pl
from jax.experimental.pallas import tpu as pltpu

f32 = jnp.float32
bf16 = jnp.bfloat16

S = 2048
D = 1024
DEPTH = 4
NSH = 4
FS = 704
PROJ = 2560
PS = 640
PW = 256
AW = 768
NPAIR = 6
NORM_EPS = 1e-6
MASK_VALUE = -1e30
ROPE_THETA = 500000.0
DILATIONS = (1, 4, 16)
QBLK = 128
NBLK = S // QBLK
TM = 512
EW_ROWS = 16
VMEM_LIMIT = 56 * 1024 * 1024

ADAM_LR = 0.001
ADAM_B1 = 0.9
ADAM_B2 = 0.999
ADAM_EPS = 1e-08
ADAM_WD = 0.01
ADAM_STEP = 10

MESH = pl.DeviceIdType.MESH
ANY = pl.BlockSpec(memory_space=pl.ANY)

BIG = (("g1", FS, D), ("u1", FS, D), ("d1", FS, D), ("wi", D, PS), ("wo", PW, D), ("g2", FS, D), ("u2", FS, D), ("d2", FS, D))
TRANSPOSED = ("ffn1_w_gate", "ffn1_w_up", "ffn2_w_gate", "ffn2_w_up")
BIG_SRC = {"g1": "ffn1_w_gate", "u1": "ffn1_w_up", "d1": "ffn1_w_down", "wi": "w_in", "wo": "w_out",
           "g2": "ffn2_w_gate", "u2": "ffn2_w_up", "d2": "ffn2_w_down"}


def _cp(*sem):
    return pltpu.CompilerParams(dimension_semantics=sem if sem else None, vmem_limit_bytes=VMEM_LIMIT)


def _sds(shape, dt):
    return jax.ShapeDtypeStruct(shape, dt)


def _dot(a, b):
    return jnp.dot(a, b, preferred_element_type=f32)


def _dot_nt(a, b):
    return lax.dot_general(a, b, (((1,), (1,)), ((), ())), preferred_element_type=f32)


def _dot_tn(a, b):
    return lax.dot_general(a, b, (((0,), (0,)), ((), ())), preferred_element_type=f32)


def _dep(dep):
    return ([], []) if dep is None else ([ANY], [dep])


def _resident(shape):
    return pl.BlockSpec(shape, lambda i: (0,) * len(shape), pipeline_mode=pl.Buffered(1))


def _ffn_fwd(x, g, wg, wu, wd, dep=None):
    def body(x_ref, g_ref, wg_ref, wu_ref, wd_ref, *rest):
        xo_ref, h_ref, a_ref, b_ref = rest[-4:]
        xf = x_ref[...]
        r = lax.rsqrt(jnp.mean(xf * xf, axis=-1, keepdims=True) + NORM_EPS)
        hh = ((xf * r) * g_ref[...]).astype(bf16)
        h_ref[...] = hh
        acc = None
        for s in range(NSH):
            a = _dot_nt(hh, wg_ref[s])
            b = _dot_nt(hh, wu_ref[s])
            a_ref[s] = a.astype(bf16)
            b_ref[s] = b.astype(bf16)
            p = _dot((a * (1.0 / (1.0 + jnp.exp(-a))) * b).astype(bf16), wd_ref[s])
            acc = p if acc is None else acc + p
        xo_ref[...] = xf + 0.5 * acc

    tok = pl.BlockSpec((TM, D), lambda i: (i, 0))
    hid = pl.BlockSpec((NSH, TM, FS), lambda i: (0, i, 0))
    wsp = _resident((NSH, FS, D))
    dspec, dop = _dep(dep)
    return pl.pallas_call(
        body, out_shape=(_sds((S, D), f32), _sds((S, D), bf16), _sds((NSH, S, FS), bf16), _sds((NSH, S, FS), bf16)),
        grid=(S // TM,), in_specs=[tok, pl.BlockSpec((1, D), lambda i: (0, 0)), wsp, wsp, wsp] + dspec,
        out_specs=(tok, tok, hid, hid), name="ffn_fwd", compiler_params=_cp("parallel"))(x, g, wg, wu, wd, *dop)


def _in_proj(x, g, wi):
    def body(x_ref, g_ref, w_ref, o_ref, h_ref):
        xf = x_ref[...]
        r = lax.rsqrt(jnp.mean(xf * xf, axis=-1, keepdims=True) + NORM_EPS)
        hh = ((xf * r) * g_ref[...]).astype(bf16)
        h_ref[...] = hh
        for s in range(NSH):
            o_ref[:, PS * s:PS * (s + 1)] = _dot(hh, w_ref[s])

    tok = pl.BlockSpec((TM, D), lambda i: (i, 0))
    return pl.pallas_call(
        body, out_shape=(_sds((S, PROJ), f32), _sds((S, D), bf16)), grid=(S // TM,),
        in_specs=[tok, pl.BlockSpec((1, D), lambda i: (0, 0)), _resident((NSH, D, PS))],
        out_specs=(pl.BlockSpec((TM, PROJ), lambda i: (i, 0)), tok), name="in_proj", compiler_params=_cp("parallel"))(x, g, wi)


def _out_proj(x, mixed, wo):
    def body(x_ref, m_ref, w_ref, o_ref):
        o_ref[...] = x_ref[...] + _dot(m_ref[...], w_ref[...].reshape(D, D))

    return pl.pallas_call(
        body, out_shape=_sds((S, D), f32), grid=(S // TM,),
        in_specs=[pl.BlockSpec((TM, D), lambda i: (i, 0)), pl.BlockSpec((TM, D), lambda i: (i, 0)),
                  pl.BlockSpec((NSH, PW, D), lambda i: (0, 0, 0))],
        out_specs=pl.BlockSpec((TM, D), lambda i: (i, 0)), name="out_proj", compiler_params=_cp("parallel"))(x, mixed, wo)


def _out_proj_bwd(dx, wo):
    def body(dx_ref, w_ref, o_ref):
        o_ref[...] = _dot_nt(dx_ref[...].astype(bf16), w_ref[...].reshape(D, D))

    return pl.pallas_call(
        body, out_shape=_sds((S, D), f32), grid=(S // TM,),
        in_specs=[pl.BlockSpec((TM, D), lambda i: (i, 0)), pl.BlockSpec((NSH, PW, D), lambda i: (0, 0, 0))],
        out_specs=pl.BlockSpec((TM, D), lambda i: (i, 0)), name="out_proj_bwd", compiler_params=_cp("parallel"))(dx, wo)


def _ffn_bwd_mid(dx, h, a, b, wd, dep=None):
    nt = S // TM

    def body(dx_ref, h_ref, a_ref, b_ref, wd_ref, *rest):
        da_ref, db_ref, dwd_ref, dwg_ref, dwu_ref, dy_s, u_s, da_s, db_s = rest[-9:]
        i = pl.program_id(1)
        rows = pl.ds(pl.multiple_of(i * TM, TM), TM)
        dy = (0.5 * dx_ref[...]).astype(bf16)
        dy_s[rows, :] = dy
        du = _dot_nt(dy, wd_ref[0])
        a = a_ref[0].astype(f32)
        b = b_ref[0].astype(f32)
        sig = 1.0 / (1.0 + jnp.exp(-a))
        silu = a * sig
        da = (du * b * (sig * (1.0 + a * (1.0 - sig)))).astype(bf16)
        db = (du * silu).astype(bf16)
        da_ref[0] = da
        db_ref[0] = db
        da_s[rows, :] = da
        db_s[rows, :] = db
        u_s[rows, :] = (silu * b).astype(bf16)

        @pl.when(i == nt - 1)
        def _():
            hh = h_ref[...]
            dwd_ref[...] = _dot_tn(u_s[...], dy_s[...]).astype(bf16).reshape(dwd_ref.shape)
            dwg_ref[...] = _dot_tn(da_s[...], hh).astype(bf16).reshape(dwg_ref.shape)
            dwu_ref[...] = _dot_tn(db_s[...], hh).astype(bf16).reshape(dwu_ref.shape)

    tok = pl.BlockSpec((TM, D), lambda s, i: (i, 0))
    hid = pl.BlockSpec((1, TM, FS), lambda s, i: (s, i, 0))
    wsp = pl.BlockSpec((1, 2, FS // 2, D), lambda s, i: (s, 0, 0, 0))
    hidden = _sds((NSH, S, FS), bf16)
    wgrad = _sds((NSH, 2, FS // 2, D), bf16)
    whole = pltpu.VMEM((S, FS), bf16)
    dspec, dop = _dep(dep)
    return pl.pallas_call(
        body, out_shape=(hidden, hidden, wgrad, wgrad, wgrad), grid=(NSH, nt),
        in_specs=[tok, pl.BlockSpec((S, D), lambda s, i: (0, 0), pipeline_mode=pl.Buffered(1)), hid, hid,
                  pl.BlockSpec((1, FS, D), lambda s, i: (s, 0, 0))] + dspec,
        out_specs=(hid, hid, wsp, wsp, wsp), scratch_shapes=[pltpu.VMEM((S, D), bf16), whole, whole, whole],
        name="ffn_bwd_mid", compiler_params=_cp("parallel", "arbitrary"))(dx, h, a, b, wd, *dop)


def _norm_bwd_tail(acc, x_ref, dxin_ref, g_ref, dxo_ref, dg_ref, first):
    xf = x_ref[...]
    r = lax.rsqrt(jnp.mean(xf * xf, axis=-1, keepdims=True) + NORM_EPS)
    xhat = xf * r
    dhg = acc * g_ref[...]
    dxo_ref[...] = dxin_ref[...] + r * (dhg - xhat * jnp.mean(dhg * xhat, axis=-1, keepdims=True))
    part = jnp.sum(acc * xhat, axis=0, keepdims=True)

    @pl.when(first)
    def _():
        dg_ref[...] = part

    @pl.when(jnp.logical_not(first))
    def _():
        dg_ref[...] += part


def _ffn_bwd_dx(dx, x_in, g, da, db, wg, wu):
    def body(dx_ref, x_ref, g_ref, da_ref, db_ref, wg_ref, wu_ref, dxo_ref, dg_ref):
        acc = None
        for s in range(NSH):
            p = _dot(da_ref[s], wg_ref[s])
            acc = p if acc is None else acc + p
            acc = acc + _dot(db_ref[s], wu_ref[s])
        _norm_bwd_tail(acc, x_ref, dx_ref, g_ref, dxo_ref, dg_ref, pl.program_id(0) == 0)

    tok = pl.BlockSpec((TM, D), lambda i: (i, 0))
    vec = pl.BlockSpec((1, D), lambda i: (0, 0))
    hid = pl.BlockSpec((NSH, TM, FS), lambda i: (0, i, 0))
    wsp = _resident((NSH, FS, D))
    return pl.pallas_call(
        body, out_shape=(_sds((S, D), f32), _sds((1, D), f32)), grid=(S // TM,),
        in_specs=[tok, tok, vec, hid, hid, wsp, wsp], out_specs=(tok, vec),
        name="ffn_bwd_dx", compiler_params=_cp("arbitrary"))(dx, x_in, g, da, db, wg, wu)


def _in_proj_bwd_dx(dx, x_in, g, dproj, wi):
    def body(dx_ref, x_ref, g_ref, dp_ref, w_ref, dxo_ref, dg_ref):
        acc = None
        for s in range(NSH):
            p = _dot_nt(dp_ref[:, PS * s:PS * (s + 1)], w_ref[s])
            acc = p if acc is None else acc + p
        _norm_bwd_tail(acc, x_ref, dx_ref, g_ref, dxo_ref, dg_ref, pl.program_id(0) == 0)

    tok = pl.BlockSpec((TM, D), lambda i: (i, 0))
    vec = pl.BlockSpec((1, D), lambda i: (0, 0))
    return pl.pallas_call(
        body, out_shape=(_sds((S, D), f32), _sds((1, D), f32)), grid=(S // TM,),
        in_specs=[tok, tok, vec, pl.BlockSpec((TM, PROJ), lambda i: (i, 0)), _resident((NSH, D, PS))], out_specs=(tok, vec),
        name="in_proj_bwd_dx", compiler_params=_cp("arbitrary"))(dx, x_in, g, dproj, wi)


def _dw(lhs, rhs, lhs_spec, rhs_spec, rows, cols, name, cast_rhs=False):
    def body(l_ref, r_ref, o_ref):
        r = r_ref[...].astype(bf16) if cast_rhs else r_ref[...]
        o_ref[...] = _dot_tn(l_ref[...], r).astype(bf16).reshape(1, 2, rows // 2, cols)

    return pl.pallas_call(
        body, out_shape=_sds((NSH, 2, rows // 2, cols), bf16), grid=(NSH,), in_specs=[lhs_spec, rhs_spec],
        out_specs=pl.BlockSpec((1, 2, rows // 2, cols), lambda s: (s, 0, 0, 0)), name=name, compiler_params=_cp("parallel"))(lhs, rhs)


_WHOLE_TOK = pl.BlockSpec((S, D), lambda s: (0, 0))


def _dw_in(h, dproj):
    return _dw(h, dproj, _WHOLE_TOK, pl.BlockSpec((S, PS), lambda s: (0, s)), D, PS, "dw_in")


def _dw_out(mixed, dx):
    return _dw(mixed, dx, pl.BlockSpec((S, PW), lambda s: (0, s)), _WHOLE_TOK, PW, D, "dw_out", cast_rhs=True)


def _final_loss(x, g, target):
    def body(x_ref, g_ref, t_ref, loss_ref, dx_ref, dg_ref):
        i = pl.program_id(0)
        xf = x_ref[...]
        r = lax.rsqrt(jnp.mean(xf * xf, axis=-1, keepdims=True) + NORM_EPS)
        xhat = xf * r
        err = xhat * g_ref[...] - t_ref[...]
        dy = err * (1.0 / D)
        dhg = dy * g_ref[...]
        dx_ref[...] = r * (dhg - xhat * jnp.mean(dhg * xhat, axis=-1, keepdims=True))
        part = jnp.sum(dy * xhat, axis=0, keepdims=True)
        lpart = jnp.zeros((8, 128), f32) + 0.5 * jnp.sum(jnp.mean(err * err, axis=-1, keepdims=True))

        @pl.when(i == 0)
        def _():
            dg_ref[...] = part
            loss_ref[...] = lpart

        @pl.when(i != 0)
        def _():
            dg_ref[...] += part
            loss_ref[...] += lpart

    tok = pl.BlockSpec((TM, D), lambda i: (i, 0))
    vec = pl.BlockSpec((1, D), lambda i: (0, 0))
    return pl.pallas_call(
        body, out_shape=(_sds((8, 128), f32), _sds((S, D), f32), _sds((1, D), f32)), grid=(S // TM,),
        in_specs=[tok, vec, tok], out_specs=(pl.BlockSpec((8, 128), lambda i: (0, 0)), tok, vec),
        name="final_loss", compiler_params=_cp("arbitrary"))(x, g, target)


def _shift_down(x, k, row):
    return jnp.where(row >= k, pltpu.roll(x, k, axis=0), 0.0)


def _shift_up(x, k, row):
    return jnp.where(row < S - k, pltpu.roll(x, S - k, axis=0), 0.0)


def _pool_geometry():
    row = lax.broadcasted_iota(jnp.int32, (S, PW), 0)
    grp = lax.broadcasted_iota(jnp.int32, (S, PW), 1) // 64
    half = jnp.where(grp == 0, 1, jnp.where(grp == 1, 2, jnp.where(grp == 2, 4, 8)))
    hi = jnp.minimum(row + half - 1, S - 1)
    lo = jnp.maximum(row - half, 0)
    return row, grp, (hi - lo + 1).astype(f32)


def _by_group(grp, v0, v1, v2, v3):
    return jnp.where(grp == 0, v0, jnp.where(grp == 1, v1, jnp.where(grp == 2, v2, v3)))


def _window_sums(x, row, grp, transpose):
    l1, r1 = x, x
    l2, r2 = l1 + _shift_down(l1, 1, row), r1 + _shift_up(r1, 1, row)
    l4, r4 = l2 + _shift_down(l2, 2, row), r2 + _shift_up(r2, 2, row)
    l8, r8 = l4 + _shift_down(l4, 4, row), r4 + _shift_up(r4, 4, row)
    lsel = _by_group(grp, l1, l2, l4, l8)
    rsel = _by_group(grp, r1, r2, r4, r8)
    if transpose:
        return lsel + _shift_up(rsel, 1, row)
    return _shift_down(lsel, 1, row) + rsel


def _pool_fwd(proj, wbd, scale):
    def body(v_ref, w_ref, sc_ref, mixed_ref, diff_ref):
        row, grp, cnt = _pool_geometry()
        v = v_ref[...]
        diff = (_window_sums(v, row, grp, False) / cnt - v).astype(bf16)
        diff_ref[...] = diff
        mixed_ref[...] = (_dot(diff, w_ref[...].astype(bf16)) * sc_ref[...]).astype(bf16)

    col = pl.BlockSpec((S, PW), lambda i: (0, 0))
    return pl.pallas_call(
        body, out_shape=(_sds((S, D), bf16), _sds((S, PW), bf16)), grid=(1,),
        in_specs=[col, pl.BlockSpec((PW, PW), lambda i: (0, 0)), pl.BlockSpec((1, PW), lambda i: (0, 0))],
        out_specs=(col, col), name="pool_fwd", compiler_params=_cp("arbitrary"))(proj, wbd, scale)


def _pool_bwd(dmixed, diff, wbd, scale, dproj):
    def body(dy_ref, diff_ref, w_ref, sc_ref, dproj_in, dv_ref, dw_ref, dsc_ref):
        del dproj_in
        row, grp, cnt = _pool_geometry()
        dy = dy_ref[...]
        diff = diff_ref[...]
        w = w_ref[...].astype(bf16)
        dsc_ref[...] = jnp.sum(dy * _dot(diff, w), axis=0, keepdims=True)
        dys = (dy * sc_ref[...]).astype(bf16)
        dw_ref[...] = _dot_tn(diff, dys)
        ddiff = _dot_nt(dys, w)
        dv_ref[...] = (_window_sums(ddiff / cnt, row, grp, True) - ddiff).astype(bf16)

    col = pl.BlockSpec((S, PW), lambda i: (0, 0))
    return pl.pallas_call(
        body, out_shape=(_sds((S, PROJ), bf16), _sds((PW, PW), f32), _sds((1, PW), f32)), grid=(1,),
        in_specs=[col, col, pl.BlockSpec((PW, PW), lambda i: (0, 0)), pl.BlockSpec((1, PW), lambda i: (0, 0)), ANY],
        out_specs=(col, pl.BlockSpec((PW, PW), lambda i: (0, 0)), pl.BlockSpec((1, PW), lambda i: (0, 0))),
        input_output_aliases={4: 0}, name="pool_bwd", compiler_params=_cp("arbitrary"))(dmixed, diff, wbd, scale, dproj)


def _rope_tables(pos_col, freq_row):
    def body(p_ref, f_ref, c_ref, a_ref, b_ref):
        ang = p_ref[...].astype(f32) * f_ref[...]
        l64 = lax.broadcasted_iota(jnp.int32, (S, 128), 1) % 64
        cos, sin = jnp.cos(ang), jnp.sin(ang)
        c_ref[...] = jnp.where(l64 < 16, cos, 1.0)
        a_ref[...] = jnp.where(l64 < 8, -sin, 0.0)
        b_ref[...] = jnp.where((l64 >= 8) & (l64 < 16), sin, 0.0)

    t = _sds((S, 128), f32)
    return pl.pallas_call(body, out_shape=(t, t, t), name="rope_tables", compiler_params=_cp())(pos_col, freq_row)


def _rope(t, c, a, b):
    return t * c + pltpu.roll(t, 120, axis=1) * a + pltpu.roll(t, 8, axis=1) * b


def _rope_bwd(g, c, a, b):
    return g * c + pltpu.roll(g * a, 8, axis=1) + pltpu.roll(g * b, 120, axis=1)


def _perm_load(ref, d):
    if d == 1:
        return ref[...]
    n = S // d
    return jnp.concatenate([ref[pl.ds(r, n, stride=d), :] for r in range(d)], axis=0)


def _unperm_store(ref, val, d):
    if d == 1:
        ref[...] = val
        return
    n = S // d
    for r in range(d):
        ref[pl.ds(r, n, stride=d), :] = val[r * n:(r + 1) * n, :]


def _band(xp, d):
    if d == NBLK:
        return xp.reshape(NBLK, QBLK, 128)
    z = jnp.zeros((64, 128), bf16)
    p = jnp.concatenate([z, xp, z], axis=0).reshape(NBLK + 1, QBLK, 128)
    return jnp.concatenate([p[:NBLK], p[1:]], axis=1)


def _unband(xb, d):
    if d == NBLK:
        return xb.reshape(S, 128)
    z = jnp.zeros((1, QBLK, 128), f32)
    p = jnp.concatenate([xb[:, :QBLK], z], axis=0) + jnp.concatenate([z, xb[:, QBLK:]], axis=0)
    return p.reshape(S + QBLK, 128)[64:S + 64]


def _band_mask(d):
    if d == NBLK:
        a = lax.broadcasted_iota(jnp.int32, (1, 2 * QBLK, QBLK), 1) & (QBLK - 1)
        b = lax.broadcasted_iota(jnp.int32, (1, 2 * QBLK, QBLK), 2)
        return (b >= a - 64) & (b <= a + 64)
    blocks_per_class = NBLK // d
    n = lax.broadcasted_iota(jnp.int32, (NBLK, 1, 2 * QBLK), 0) & (blocks_per_class - 1)
    be = lax.broadcasted_iota(jnp.int32, (NBLK, 1, 2 * QBLK), 2)
    a = lax.broadcasted_iota(jnp.int32, (1, 2 * QBLK, 2 * QBLK), 1) & (QBLK - 1)
    b = lax.broadcasted_iota(jnp.int32, (1, 2 * QBLK, 2 * QBLK), 2)
    band = (b >= a) & (b <= a + 128)
    edge = ((be >= 64) | (n != 0)) & ((be < QBLK + 64) | (n != blocks_per_class - 1))
    return band & edge


def _stack_heads(xb, lo):
    z = jnp.zeros_like(xb)
    return jnp.concatenate([jnp.where(lo, xb, z), jnp.where(lo, z, xb)], axis=1)


def _unstack_heads(x2, lo):
    return jnp.where(lo, x2[:, :QBLK], x2[:, QBLK:])


def _rows_to_lanes(col2, lo):
    return jnp.where(lo, jnp.broadcast_to(col2[:, :QBLK], (NBLK, QBLK, 128)), jnp.broadcast_to(col2[:, QBLK:], (NBLK, QBLK, 128)))


def _bmm_nt(a, b):
    return jnp.einsum('nqd,nkd->nqk', a, b, preferred_element_type=f32)


def _bmm_nn(a, b):
    return jnp.einsum('nqk,nkd->nqd', a, b, preferred_element_type=f32)


def _bmm_tn(a, b):
    return jnp.einsum('nqk,nqd->nkd', a, b, preferred_element_type=f32)


def _attn_fwd(proj, tc, ta, tb, mixed):
    def body(q_ref, k_ref, v_ref, c_ref, a_ref, b_ref, mixed_in, mixed_ref, o_ref, lse_ref, qn, kn, t_num, t_m, t_den):
        del mixed_in
        lo = lax.broadcasted_iota(jnp.int32, (1, 1, 128), 2) < 64
        c, a, b = c_ref[...], a_ref[...], b_ref[...]
        qn[...] = _rope(q_ref[...], c, a, b)
        kn[...] = _rope(k_ref[...], c, a, b)
        run = None
        for d in DILATIONS:
            q2 = _stack_heads(_perm_load(qn, d).astype(bf16).reshape(NBLK, QBLK, 128), lo)
            kb = _band(_perm_load(kn, d).astype(bf16), d)
            vb = _band(_perm_load(v_ref, d).astype(bf16), d)
            s = jnp.where(_band_mask(d), _bmm_nt(q2, kb) * 0.125, MASK_VALUE)
            m = jnp.max(s, axis=2, keepdims=True)
            p = jnp.exp(s - m)
            den = jnp.sum(p, axis=2, keepdims=True)
            num = _unstack_heads(_bmm_nn(p.astype(bf16), vb), lo)
            _unperm_store(t_num, num.reshape(S, 128), d)
            _unperm_store(t_m, _rows_to_lanes(m, lo).reshape(S, 128), d)
            _unperm_store(t_den, _rows_to_lanes(den, lo).reshape(S, 128), d)
            if run is None:
                run = (t_m[...], t_num[...], t_den[...])
            else:
                m_new = jnp.maximum(run[0], t_m[...])
                w_old, w_new = jnp.exp(run[0] - m_new), jnp.exp(t_m[...] - m_new)
                run = (m_new, w_old * run[1] + w_new * t_num[...], w_old * run[2] + w_new * t_den[...])
        out = run[1] / run[2]
        o_ref[...] = out
        mixed_ref[...] = out.astype(bf16)
        lse_ref[...] = run[0] + jnp.log(run[2])

    def col(off):
        return pl.BlockSpec((S, 128), lambda j, off=off: (0, off + j))

    tab = pl.BlockSpec((S, 128), lambda j: (0, 0))
    scr = pltpu.VMEM((S, 128), f32)
    return pl.pallas_call(
        body, out_shape=(_sds((S, D), bf16), _sds((S, AW), f32), _sds((S, AW), f32)), grid=(NPAIR,),
        in_specs=[col(2), col(8), col(14), tab, tab, tab, ANY], out_specs=(col(2), col(0), col(0)),
        scratch_shapes=[scr, scr, scr, scr, scr], input_output_aliases={6: 0}, name="attn_fwd",
        compiler_params=_cp("arbitrary"))(proj, proj, proj, tc, ta, tb, mixed)


def _attn_bwd(proj, tc, ta, tb, o, lse, dmixed):
    def body(q_ref, k_ref, v_ref, c_ref, a_ref, b_ref, o_ref, lse_ref, do_ref, dp_ref, qn, kn, tmp, dk_s, dv_s):
        t = pl.program_id(1)

        @pl.when(t == 0)
        def _():
            lo = lax.broadcasted_iota(jnp.int32, (1, 1, 128), 2) < 64
            c, a, b = c_ref[...], a_ref[...], b_ref[...]
            qn[...] = _rope(q_ref[...], c, a, b)
            kn[...] = _rope(k_ref[...], c, a, b)
            dq = dk = dv = None
            for d in DILATIONS:
                q2 = _stack_heads(_perm_load(qn, d).astype(bf16).reshape(NBLK, QBLK, 128), lo)
                kb = _band(_perm_load(kn, d).astype(bf16), d)
                vb = _band(_perm_load(v_ref, d).astype(bf16), d)
                dob = _perm_load(do_ref, d).reshape(NBLK, QBLK, 128)
                ob = _perm_load(o_ref, d).reshape(NBLK, QBLK, 128)
                lsb = _perm_load(lse_ref, d).reshape(NBLK, QBLK, 128)
                do2 = _stack_heads(dob.astype(bf16), lo)
                delta2 = jnp.sum(_stack_heads(dob * ob, lo), axis=2, keepdims=True)
                lse2 = jnp.max(jnp.concatenate([jnp.where(lo, lsb, MASK_VALUE), jnp.where(lo, MASK_VALUE, lsb)], axis=1),
                               axis=2, keepdims=True)
                s = _bmm_nt(q2, kb) * 0.125
                p = jnp.where(_band_mask(d), jnp.exp(s - lse2), 0.0)
                ds = (p * (_bmm_nt(do2, vb) - delta2) * 0.125).astype(bf16)
                pb = p.astype(bf16)
                dq_b = _unstack_heads(_bmm_nn(ds, kb), lo).reshape(S, 128)
                dk_b = _unband(_bmm_tn(ds, q2), d)
                dv_b = _unband(_bmm_tn(pb, do2), d)
                acc = []
                for prev, new in ((dq, dq_b), (dk, dk_b), (dv, dv_b)):
                    _unperm_store(tmp, new, d)
                    acc.append(tmp[...] if prev is None else prev + tmp[...])
                dq, dk, dv = acc
            dp_ref[...] = _rope_bwd(dq, c, a, b).astype(bf16)
            dk_s[...] = _rope_bwd(dk, c, a, b).astype(bf16)
            dv_s[...] = dv.astype(bf16)

        @pl.when(t == 1)
        def _():
            dp_ref[...] = dk_s[...]

        @pl.when(t == 2)
        def _():
            dp_ref[...] = dv_s[...]

    def col(off):
        return pl.BlockSpec((S, 128), lambda j, t, off=off: (0, off + j))

    tab = pl.BlockSpec((S, 128), lambda j, t: (0, 0))
    scr = pltpu.VMEM((S, 128), f32)
    scb = pltpu.VMEM((S, 128), bf16)
    return pl.pallas_call(
        body, out_shape=_sds((S, PROJ), bf16), grid=(NPAIR, 3),
        in_specs=[col(2), col(8), col(14), tab, tab, tab, col(0), col(0), col(2)],
        out_specs=pl.BlockSpec((S, 128), lambda j, t: (0, 2 + NPAIR * t + j)),
        scratch_shapes=[scr, scr, scr, scb, scb], name="attn_bwd",
        compiler_params=_cp("arbitrary", "arbitrary"))(proj, proj, proj, tc, ta, tb, o, lse, dmixed)


def _block_diag(w4):
    out = jnp.zeros((PW, PW), w4.dtype)
    for g in range(4):
        out = out.at[64 * g:64 * (g + 1), 64 * g:64 * (g + 1)].set(w4[g])
    return out


def _diag_blocks(w):
    return jnp.stack([w[64 * g:64 * (g + 1), 64 * g:64 * (g + 1)] for g in range(4)])


def _rope_inputs(positions):
    inv_freq = ROPE_THETA ** (-jnp.arange(0, 16, 2, dtype=f32) / 16)
    l64 = np.arange(128) % 64
    idx = np.where(l64 < 16, l64 % 8, 0)
    return positions.reshape(S, 1), inv_freq[idx].reshape(1, 128)


def _layer_fwd(x, w, small, l, tabs, dep=None):
    g1, gm, g2 = (small[k][l].reshape(1, D) for k in ("ffn1_norm", "mix_norm", "ffn2_norm"))
    wbd = _block_diag(small["pool_w"][l])
    psc = small["pool_scale"][l].reshape(1, PW)
    x1, h1, a1, b1 = _ffn_fwd(x, g1, w["g1"], w["u1"], w["d1"], dep)
    proj, h2 = _in_proj(x1, gm, w["wi"])
    mixed, diff = _pool_fwd(proj, wbd, psc)
    mixed, o, lse = _attn_fwd(proj, *tabs, mixed)
    x2 = _out_proj(x1, mixed, w["wo"])
    out, h3, a2, b2 = _ffn_fwd(x2, g2, w["g2"], w["u2"], w["d2"])
    return out, dict(x0=x, h1=h1, a1=a1, b1=b1, x1=x1, h2=h2, proj=proj, mixed=mixed, diff=diff, o=o, lse=lse,
                     x2=x2, h3=h3, a2=a2, b2=b2, g1=g1, gm=gm, g2=g2, wbd=wbd, psc=psc)


def _layer_bwd(dx, w, sv, tabs, dep=None):
    gr, sg = {}, {}
    da, db, gr["d2"], gr["g2"], gr["u2"] = _ffn_bwd_mid(dx, sv["h3"], sv["a2"], sv["b2"], w["d2"], dep)
    dx, sg["ffn2_norm"] = _ffn_bwd_dx(dx, sv["x2"], sv["g2"], da, db, w["g2"], w["u2"])
    gr["wo"] = _dw_out(sv["mixed"], dx)
    dmixed = _out_proj_bwd(dx, w["wo"])
    dproj = _attn_bwd(sv["proj"], *tabs, sv["o"], sv["lse"], dmixed)
    dproj, dwbd, sg["pool_scale"] = _pool_bwd(dmixed, sv["diff"], sv["wbd"], sv["psc"], dproj)
    sg["pool_w"] = _diag_blocks(dwbd)
    gr["wi"] = _dw_in(sv["h2"], dproj)
    dx, sg["mix_norm"] = _in_proj_bwd_dx(dx, sv["x1"], sv["gm"], dproj, w["wi"])
    da, db, gr["d1"], gr["g1"], gr["u1"] = _ffn_bwd_mid(dx, sv["h1"], sv["a1"], sv["b1"], w["d1"])
    dx, sg["ffn1_norm"] = _ffn_bwd_dx(dx, sv["x0"], sv["g1"], da, db, w["g1"], w["u1"])
    return dx, gr, sg


def _forward_backward(x, positions, target, gathered, small):
    tabs = _rope_tables(*_rope_inputs(positions))
    saved = []
    for l in range(DEPTH):
        x, sv = _layer_fwd(x, gathered[l], small, l, tabs)
        saved.append(sv)
    loss, dx, dgf = _final_loss(x, small["final_norm"].reshape(1, D), target)
    big = [None] * DEPTH
    sg = {k: [None] * DEPTH for k in ("ffn1_norm", "mix_norm", "pool_w", "pool_scale", "ffn2_norm")}
    for l in reversed(range(DEPTH)):
        dx, big[l], sgl = _layer_bwd(dx, gathered[l], saved[l], tabs)
        for k, v in sgl.items():
            sg[k][l] = v
    sg["final_norm"] = dgf
    return loss, dx, big, sg


def _place():
    x, y, c = lax.axis_index("x"), lax.axis_index("y"), lax.axis_index("c")
    chips = [(1 - x, y), (x, 1 - y), (1 - x, 1 - y)]
    return x, y, c, chips


def _cast_layer(params, l, place, dep=None):
    def body(p_ref, *refs):
        del p_ref
        for i_ref, o_ref in zip(refs[:8], refs[-8:]):
            o_ref[...] = i_ref[...].astype(bf16).reshape(o_ref.shape)

    ins, in_specs, out_shape, out_specs = [], [], [], []
    for name, rows, cols in BIG:
        q = rows // 4
        ins.append(params[BIG_SRC[name]])
        in_specs.append(pl.BlockSpec((1, q, cols), lambda i, p, l=l: (l, i, 0)))
        out_shape.append(_sds((NSH, 2, rows // 2, cols), bf16))
        out_specs.append(pl.BlockSpec((1, 1, q, cols), lambda i, p: (p[1], i // 2, i % 2, 0)))
    dspec, dop = _dep(dep)
    return pl.pallas_call(
        body, out_shape=out_shape,
        grid_spec=pltpu.PrefetchScalarGridSpec(num_scalar_prefetch=1, grid=(4,), in_specs=in_specs + dspec, out_specs=out_specs),
        name=f"cast_layer{l}", compiler_params=_cp("parallel"))(place, *ins, *dop)


HBM = pl.BlockSpec(memory_space=pltpu.HBM)
SEM = pl.BlockSpec(memory_space=pltpu.SEMAPHORE)
_SPLIT = pltpu.CompilerParams(has_side_effects=pltpu.SideEffectType.DATAFLOW_SIDE_EFFECTING)


def _hbm(arrays):
    return [pltpu.with_memory_space_constraint(a, pltpu.HBM) for a in arrays]


def _chip_copies(src_of, dst_of, send_sems, recv_sems, n):
    x, y, c, chips = _place()
    me = 2 * x + y
    out = []
    for t in range(n):
        for k, chip in enumerate(chips):
            peer = 2 * chip[0] + chip[1]
            send = pltpu.make_async_remote_copy(
                src_ref=src_of(t, peer), dst_ref=dst_of(t, me), send_sem=send_sems.at[3 * t + k], recv_sem=recv_sems.at[3 * t + k],
                device_id=(chip[0], chip[1], c), device_id_type=MESH)
            land = pltpu.make_async_remote_copy(
                src_ref=src_of(t, peer), dst_ref=dst_of(t, peer), send_sem=send_sems.at[3 * t + k], recv_sem=recv_sems.at[3 * t + k],
                device_id=(chip[0], chip[1], c), device_id_type=MESH)
            out.append((send, land))
    return out


def _exchange_start(src, land, after, src_of, dst_of, name):
    n, m = len(src), len(src) + len(land)

    def body(*refs):
        src_refs = refs[:n]
        land_refs = refs[n:m] if land else src_refs
        send_sems, recv_sems = refs[m + 1], refs[m + 2]
        token = refs[-1]
        for send, _ in _chip_copies(lambda t, s: src_of(src_refs[t], s), lambda t, s: dst_of(land_refs[t], s), send_sems, recv_sems, n):
            send.start()
        token[...] = jnp.zeros_like(token)

    arrays = list(src) + list(land)
    out_shape = ([pltpu.SemaphoreType.DMA((3 * n,)), pltpu.SemaphoreType.DMA((3 * n,))] + [pltpu.HBM(a.shape, a.dtype) for a in arrays]
                 + [_sds((8, 128), f32)])
    res = pl.pallas_call(
        body, out_shape=out_shape, in_specs=[HBM] * m + [ANY], out_specs=[SEM, SEM] + [HBM] * m + [pl.BlockSpec(memory_space=pltpu.VMEM)],
        input_output_aliases={i: 2 + i for i in range(m)}, name=name, compiler_params=_SPLIT)(*_hbm(arrays), after)
    return res[0], res[1], list(res[2:2 + n]), list(res[2 + n:2 + m]), res[-1]


def _exchange_wait(send_sems, recv_sems, src, land, after, src_of, dst_of, name):
    n, m = len(src), len(src) + len(land)

    def body(*refs):
        src_refs = refs[:n]
        land_refs = refs[n:m] if land else src_refs
        send_sems, recv_sems = refs[m], refs[m + 1]
        for send, land_cp in _chip_copies(lambda t, s: src_of(src_refs[t], s), lambda t, s: dst_of(land_refs[t], s), send_sems, recv_sems, n):
            send.wait_send()
            land_cp.wait_recv()

    arrays = list(src) + list(land)
    res = pl.pallas_call(
        body, out_shape=[pltpu.HBM(a.shape, a.dtype) for a in arrays], in_specs=[HBM] * m + [SEM, SEM, ANY], out_specs=[HBM] * m,
        input_output_aliases={i: i for i in range(m)}, name=name, compiler_params=_SPLIT)(*arrays, send_sems, recv_sems, after)
    return list(res[:n]), list(res[n:])


def _own_half(ref, s):
    x, y, c, _ = _place()
    return ref.at[2 * x + y, c]


def _slot_half(ref, s):
    return ref.at[s, lax.axis_index("c")]


def _slot(ref, s):
    return ref.at[s]


def _gather_forward(bufs):
    n = len(bufs)

    def body(*refs):
        outs = refs[n:2 * n]
        send_sems, recv_sems = refs[2 * n:]
        x, y, c, chips = _place()
        sibling = (x, y, 1 - c)
        passed = []
        for t in range(n):
            for k, chip in enumerate(chips):
                blk = outs[t].at[2 * chip[0] + chip[1], c]
                cp = pltpu.make_async_remote_copy(
                    src_ref=blk, dst_ref=blk, send_sem=send_sems.at[t, k], recv_sem=recv_sems.at[t, k],
                    device_id=sibling, device_id_type=MESH)
                cp.start()
                passed.append(cp)
        for t in range(n):
            for k, chip in enumerate(chips):
                blk = outs[t].at[2 * chip[0] + chip[1], 1 - c]
                pltpu.make_async_remote_copy(
                    src_ref=blk, dst_ref=blk, send_sem=send_sems.at[t, k], recv_sem=recv_sems.at[t, k],
                    device_id=sibling, device_id_type=MESH).wait_recv()
        for cp in passed:
            cp.wait_send()

    out_shape = [_sds(a.shape, bf16) for a in bufs]
    return pl.pallas_call(
        body, out_shape=out_shape, in_specs=[ANY] * n, out_specs=[ANY] * n, input_output_aliases={t: t for t in range(n)},
        scratch_shapes=[pltpu.SemaphoreType.DMA((n, 3)), pltpu.SemaphoreType.DMA((n, 3))], name="gather_forward")(*bufs)


def _sibling_swap(grads):
    n = len(grads)

    def body(*refs):
        ins, outs = refs[:n], refs[n:2 * n]
        send_sems, recv_sems = refs[2 * n:]
        x, y, c, _ = _place()
        cps = []
        for t in range(n):
            for s in range(NSH):
                cp = pltpu.make_async_remote_copy(
                    src_ref=ins[t].at[s, 1 - c], dst_ref=outs[t].at[s], send_sem=send_sems.at[t, s], recv_sem=recv_sems.at[t, s],
                    device_id=(x, y, 1 - c), device_id_type=MESH)
                cp.start()
                cps.append(cp)
        for cp in cps:
            cp.wait()

    out_shape = [_sds((NSH,) + a.shape[2:], bf16) for a in grads]
    return pl.pallas_call(
        body, out_shape=out_shape, in_specs=[ANY] * n, out_specs=[ANY] * n,
        scratch_shapes=[pltpu.SemaphoreType.DMA((n, NSH)), pltpu.SemaphoreType.DMA((n, NSH))],
        name="sibling_swap")(*grads)


def _row_tile(h):
    return h // 2 if h % 32 == 0 else h


def _pair_sum(grads, got, c_idx):
    n = len(grads)

    def body(c_ref, *refs):
        del c_ref
        for t in range(n):
            refs[2 * n + t][...] = (refs[t][...].astype(f32).reshape(refs[n + t].shape) + refs[n + t][...].astype(f32)).astype(bf16)

    in_specs, out_shape, out_specs = [], [], []
    for a in grads:
        h, cols = a.shape[2:]
        in_specs.append(pl.BlockSpec((1, 1, _row_tile(h), cols), lambda s, i, c: (s, c[0], i, 0)))
    for a in grads:
        h, cols = a.shape[2:]
        in_specs.append(pl.BlockSpec((1, _row_tile(h), cols), lambda s, i, c: (s, i, 0)))
        out_shape.append(_sds((NSH, h, cols), bf16))
        out_specs.append(pl.BlockSpec((1, _row_tile(h), cols), lambda s, i, c: (s, i, 0)))
    return pl.pallas_call(
        body, out_shape=out_shape,
        grid_spec=pltpu.PrefetchScalarGridSpec(num_scalar_prefetch=1, grid=(NSH, 2), in_specs=in_specs, out_specs=out_specs),
        name="pair_sum", compiler_params=_cp("parallel", "parallel"))(c_idx, *grads, *got)


def _chip_sum(psum, parts, full, place, l, name):
    n = len(parts)

    def body(p_ref, *refs):
        s = pl.program_id(1)
        for t in range(n):
            val = jnp.where(s == p_ref[1], refs[t][0], refs[n + t][0]).astype(f32)
            out = refs[3 * n + t]

            @pl.when(s == 0)
            def _(out=out, val=val):
                out[0, 0] = val

            @pl.when(s != 0)
            def _(out=out, val=val):
                out[0, 0] += val

    own_specs, part_specs, out_shape, out_specs = [], [], [], []
    for a, fl in zip(parts, full):
        _, h, cols = a.shape
        r = _row_tile(h)
        own_specs.append(pl.BlockSpec((1, r, cols), lambda i, s, p: (p[1], i, 0)))
        part_specs.append(pl.BlockSpec((1, r, cols), lambda i, s, p: (jnp.where(s == p[1], (s + 1) % NSH, s), i, 0)))
        out_shape.append(_sds(fl.shape, f32))
        out_specs.append(pl.BlockSpec((1, 1, r, cols), lambda i, s, p, l=l: (l, p[0], i, 0)))
    return pl.pallas_call(
        body, out_shape=out_shape,
        grid_spec=pltpu.PrefetchScalarGridSpec(num_scalar_prefetch=1, grid=(2, NSH), in_specs=own_specs + part_specs + [ANY] * n,
                                               out_specs=out_specs),
        input_output_aliases={1 + 2 * n + t: t for t in range(n)}, name=name,
        compiler_params=_cp("parallel", "arbitrary"))(place, *psum, *parts, *full)


def _sibling_share(full, l, name):
    n = len(full)

    def body(*refs):
        outs = refs[n:2 * n]
        send_sems, recv_sems = refs[2 * n:]
        x, y, c, _ = _place()
        sibling = (x, y, 1 - c)
        cps = []
        for t in range(n):
            blk = outs[t].at[l, c]
            cp = pltpu.make_async_remote_copy(
                src_ref=blk, dst_ref=blk, send_sem=send_sems.at[t], recv_sem=recv_sems.at[t], device_id=sibling, device_id_type=MESH)
            cp.start()
            cps.append(cp)
        for t in range(n):
            blk = outs[t].at[l, 1 - c]
            pltpu.make_async_remote_copy(
                src_ref=blk, dst_ref=blk, send_sem=send_sems.at[t], recv_sem=recv_sems.at[t],
                device_id=sibling, device_id_type=MESH).wait_recv()
        for cp in cps:
            cp.wait_send()

    out_shape = [_sds(a.shape, f32) for a in full]
    return pl.pallas_call(
        body, out_shape=out_shape, in_specs=[ANY] * n, out_specs=[ANY] * n, input_output_aliases={t: t for t in range(n)},
        scratch_shapes=[pltpu.SemaphoreType.DMA((n,)), pltpu.SemaphoreType.DMA((n,))], name=name)(*full)


SMALL_ROWS = 656


def _pack_small(per_layer, final_vec, loss_tile):
    rows = []
    for l in range(DEPTH):
        for k in ("ffn1_norm", "mix_norm", "ffn2_norm"):
            rows.append(per_layer[k][l].reshape(8, 128))
        rows.append(per_layer["pool_w"][l].reshape(128, 128))
        rows.append(jnp.pad(per_layer["pool_scale"][l].reshape(2, 128), ((0, 6), (0, 0))))
    rows.append(final_vec.reshape(8, 128))
    rows.append(loss_tile)
    return jnp.concatenate(rows, axis=0)


def _unpack_small(buf):
    out = {k: [] for k in ("ffn1_norm", "mix_norm", "ffn2_norm", "pool_w", "pool_scale")}
    r = 0
    for l in range(DEPTH):
        for k in ("ffn1_norm", "mix_norm", "ffn2_norm"):
            out[k].append(buf[r:r + 8].reshape(D))
            r += 8
        out["pool_w"].append(buf[r:r + 128].reshape(4, 64, 64))
        r += 128
        out["pool_scale"].append(buf[r:r + 2].reshape(PW))
        r += 8
    res = {k: jnp.stack(v) for k, v in out.items()}
    res["final_norm"] = buf[r:r + 8].reshape(D)
    res["loss"] = buf[r + 8, 0]
    return res


def _allreduce_small(buf):
    def body(in_ref, out_ref, slots, send_sems, recv_sems):
        x, y, c, _ = _place()
        me = 4 * x + 2 * y + c
        slots[me] = in_ref[...]
        peers = []
        for k in range(1, 8):
            px, py, pc = x ^ (k >> 2), y ^ ((k >> 1) & 1), c ^ (k & 1)
            cp = pltpu.make_async_remote_copy(
                src_ref=in_ref, dst_ref=slots.at[me], send_sem=send_sems.at[k - 1], recv_sem=recv_sems.at[k - 1],
                device_id=(px, py, pc), device_id_type=MESH)
            cp.start()
            peers.append(cp)
        for k in range(1, 8):
            px, py, pc = x ^ (k >> 2), y ^ ((k >> 1) & 1), c ^ (k & 1)
            slot = 4 * px + 2 * py + pc
            pltpu.make_async_remote_copy(
                src_ref=slots.at[slot], dst_ref=slots.at[slot], send_sem=send_sems.at[k - 1], recv_sem=recv_sems.at[k - 1],
                device_id=(px, py, pc), device_id_type=MESH).wait_recv()
        for cp in peers:
            cp.wait_send()
        acc = slots[0]
        for j in range(1, 8):
            acc = acc + slots[j]
        out_ref[...] = acc

    return pl.pallas_call(
        body, out_shape=_sds((SMALL_ROWS, 128), f32),
        in_specs=[pl.BlockSpec(memory_space=pltpu.VMEM)], out_specs=pl.BlockSpec(memory_space=pltpu.VMEM),
        scratch_shapes=[pltpu.VMEM((8, SMALL_ROWS, 128), f32), pltpu.SemaphoreType.DMA((7,)), pltpu.SemaphoreType.DMA((7,))],
        name="allreduce_small", compiler_params=_cp())(buf)


def _adamw_math(w, g, m, v):
    m = ADAM_B1 * m + (1.0 - ADAM_B1) * g
    v = ADAM_B2 * v + (1.0 - ADAM_B2) * (g * g)
    m_hat = m / (1.0 - ADAM_B1 ** ADAM_STEP)
    v_hat = v / (1.0 - ADAM_B2 ** ADAM_STEP)
    return -ADAM_LR * (m_hat / (jnp.sqrt(v_hat) + ADAM_EPS) + ADAM_WD * w), m, v


def _adamw(w, g, m, v, name, first=0, prev=None, dep=None):
    def body(w_ref, g_ref, m_ref, v_ref, *rest):
        go_ref, d_ref, mo_ref, vo_ref = rest[-4:]
        g = g_ref[...]
        d, mn, vn = _adamw_math(w_ref[...], g, m_ref[...], v_ref[...])
        go_ref[...] = g
        d_ref[...] = d
        mo_ref[...] = mn
        vo_ref[...] = vn

    _, rows, cols = w.shape
    r = rows // 4 if rows % 32 == 0 else rows
    spec = pl.BlockSpec((1, r, cols), lambda i, j: (first + i, j, 0))
    gspec = pl.BlockSpec((1, r, cols), lambda i, j: (i, j, 0))
    out = _sds(w.shape, f32)
    extra = [] if prev is None else list(prev)
    dspec, dop = _dep(dep)
    return pl.pallas_call(
        body, out_shape=(out, out, out, out), grid=(g.shape[0], rows // r), in_specs=[spec, gspec, spec, spec] + [ANY] * len(extra) + dspec,
        out_specs=(spec,) * 4, input_output_aliases={4 + i: i for i in range(len(extra))}, name=name,
        compiler_params=_cp("parallel", "parallel"))(w, g, m, v, *extra, *dop)


SMALL_NAMES = ("ffn1_norm", "mix_norm", "pool_w", "pool_scale", "ffn2_norm", "final_norm")
WEIGHT_ORDER = ("ffn1_norm", "ffn1_w_gate", "ffn1_w_up", "ffn1_w_down", "mix_norm", "w_in", "pool_w", "pool_scale", "w_out",
                "ffn2_norm", "ffn2_w_gate", "ffn2_w_up", "ffn2_w_down", "final_norm")


def _pack_small_params(p):
    per_layer = {k: [p[k][l] for l in range(DEPTH)] for k in ("ffn1_norm", "mix_norm", "ffn2_norm", "pool_w", "pool_scale")}
    return _pack_small(per_layer, p["final_norm"], jnp.zeros((8, 128), f32))


def kernel(x, positions, ffn1_norm, ffn1_w_gate, ffn1_w_up, ffn1_w_down, mix_norm, w_in, pool_w, pool_scale, w_out, ffn2_norm, ffn2_w_gate, ffn2_w_up, ffn2_w_down, final_norm, loss_target, m_ffn1_norm, m_ffn1_w_gate, m_ffn1_w_up, m_ffn1_w_down, m_mix_norm, m_w_in, m_pool_w, m_pool_scale, m_w_out, m_ffn2_norm, m_ffn2_w_gate, m_ffn2_w_up, m_ffn2_w_down, m_final_norm, v_ffn1_norm, v_ffn1_w_gate, v_ffn1_w_up, v_ffn1_w_down, v_mix_norm, v_w_in, v_pool_w, v_pool_scale, v_w_out, v_ffn2_norm, v_ffn2_w_gate, v_ffn2_w_up, v_ffn2_w_down, v_final_norm):
    params = dict(ffn1_norm=ffn1_norm, ffn1_w_gate=ffn1_w_gate, ffn1_w_up=ffn1_w_up, ffn1_w_down=ffn1_w_down,
                  mix_norm=mix_norm, w_in=w_in, pool_w=pool_w, pool_scale=pool_scale, w_out=w_out, ffn2_norm=ffn2_norm,
                  ffn2_w_gate=ffn2_w_gate, ffn2_w_up=ffn2_w_up, ffn2_w_down=ffn2_w_down, final_norm=final_norm)
    mom_m = dict(ffn1_norm=m_ffn1_norm, ffn1_w_gate=m_ffn1_w_gate, ffn1_w_up=m_ffn1_w_up, ffn1_w_down=m_ffn1_w_down,
                 mix_norm=m_mix_norm, w_in=m_w_in, pool_w=m_pool_w, pool_scale=m_pool_scale, w_out=m_w_out,
                 ffn2_norm=m_ffn2_norm, ffn2_w_gate=m_ffn2_w_gate, ffn2_w_up=m_ffn2_w_up, ffn2_w_down=m_ffn2_w_down,
                 final_norm=m_final_norm)
    mom_v = dict(ffn1_norm=v_ffn1_norm, ffn1_w_gate=v_ffn1_w_gate, ffn1_w_up=v_ffn1_w_up, ffn1_w_down=v_ffn1_w_down,
                 mix_norm=v_mix_norm, w_in=v_w_in, pool_w=v_pool_w, pool_scale=v_pool_scale, w_out=v_w_out,
                 ffn2_norm=v_ffn2_norm, ffn2_w_gate=v_ffn2_w_gate, ffn2_w_up=v_ffn2_w_up, ffn2_w_down=v_ffn2_w_down,
                 final_norm=v_final_norm)
    names = [t[0] for t in BIG]
    for d in (params, mom_m, mom_v):
        for k in TRANSPOSED:
            d[k] = jnp.swapaxes(d[k], 1, 2)

    place = jnp.stack([lax.axis_index("c"), 2 * lax.axis_index("x") + lax.axis_index("y")]).astype(jnp.int32)
    def gather_start(l, cast, after):
        return _exchange_start(cast, [], after, _own_half, _slot_half, f"gather_start{l}")

    def gather_end(started, after, l):
        send_sems, recv_sems, bufs, _, _ = started
        bufs, _ = _exchange_wait(send_sems, recv_sems, bufs, [], after, _own_half, _slot_half, f"gather_wait{l}")
        return {nm: a.reshape(NSH, rows, cols) for (nm, rows, cols), a in zip(BIG, _gather_forward(bufs))}

    tabs = _rope_tables(*_rope_inputs(positions))
    h = x.reshape(S, D)
    weights, saved = [], []
    started = gather_start(0, _cast_layer(params, 0, place), place)
    after = started[-1]
    casts = {}
    for l in range(1, DEPTH):
        casts[l] = _cast_layer(params, l, place, after)
        after = casts[l][0]
    for l in range(DEPTH):
        weights.append(gather_end(started, after, l))
        dep = None
        if l + 1 < DEPTH:
            started = gather_start(l + 1, casts[l + 1], weights[l]["g1"])
            dep = started[-1]
        h, sv = _layer_fwd(h, weights[l], params, l, tabs, dep)
        saved.append(sv)
        after = h
    loss, dx, dgf = _final_loss(h, final_norm.reshape(1, D), loss_target.reshape(S, D))

    upper = [lax.empty((DEPTH - 1, 2, rows // 2, cols), f32) for _, rows, cols in BIG]
    lower = [lax.empty((1, 2, rows // 2, cols), f32) for _, rows, cols in BIG]
    sg = {k: [None] * DEPTH for k in ("ffn1_norm", "mix_norm", "pool_w", "pool_scale", "ffn2_norm")}
    sg["final_norm"] = dgf

    def reduce_end(started, after, l, full, slot):
        send_sems, recv_sems, psum, parts, _ = started
        psum, parts = _exchange_wait(send_sems, recv_sems, psum, parts, after, _slot, _slot, f"grad_wait{l}")
        return _sibling_share(_chip_sum(psum, parts, full, place, slot, f"chip_sum{l}"), slot, f"sibling_share{l}")

    started, dep = None, None
    for l in reversed(range(DEPTH)):
        dx, gr, sgl = _layer_bwd(dx, weights[l], saved[l], tabs, dep)
        for k, v in sgl.items():
            sg[k][l] = v
        if started is not None:
            upper = reduce_end(started, dx, l + 1, upper, l)
        grads = [gr[nm] for nm in names]
        psum = _pair_sum(grads, _sibling_swap(grads), place)
        parts = [lax.empty(a.shape, bf16) for a in psum]
        started = _exchange_start(psum, parts, place, _slot, _slot, f"grad_start{l}")
        dep = started[-1]

    big_out = {}
    for (nm, rows, cols), g in zip(BIG, upper):
        k = BIG_SRC[nm]
        big_out[k] = _adamw(params[k], g.reshape(DEPTH - 1, rows, cols), mom_m[k], mom_v[k], "adamw_upper_" + k, first=1, dep=dep)
        dep = big_out[k][1]
    lower = reduce_end(started, dep, 0, lower, 0)
    for (nm, rows, cols), g in zip(BIG, lower):
        k = BIG_SRC[nm]
        big_out[k] = _adamw(params[k], g.reshape(1, rows, cols), mom_m[k], mom_v[k], "adamw_lower_" + k, first=0, prev=big_out[k])

    per_layer = {k: sg[k] for k in ("ffn1_norm", "mix_norm", "ffn2_norm", "pool_w", "pool_scale")}
    small_sum = _allreduce_small(_pack_small(per_layer, sg["final_norm"], loss))
    gs, ds_, ms, vs = _adamw(_pack_small_params(params).reshape(1, SMALL_ROWS, 128), small_sum.reshape(1, SMALL_ROWS, 128),
                             _pack_small_params(mom_m).reshape(1, SMALL_ROWS, 128),
                             _pack_small_params(mom_v).reshape(1, SMALL_ROWS, 128), "adamw_small")
    small_out = [_unpack_small(a.reshape(SMALL_ROWS, 128)) for a in (gs, ds_, ms, vs)]

    grad, delta, new_m, new_v = {}, {}, {}, {}
    for k in WEIGHT_ORDER:
        if k in SMALL_NAMES:
            grad[k], delta[k], new_m[k], new_v[k] = (so[k] for so in small_out)
        else:
            grad[k], delta[k], new_m[k], new_v[k] = big_out[k]
    for d in (grad, delta, new_m, new_v):
        for k in TRANSPOSED:
            d[k] = jnp.swapaxes(d[k], 1, 2)
    return (small_out[0]["loss"], dx.reshape(1, S, D), *[grad[k] for k in WEIGHT_ORDER], *[delta[k] for k in WEIGHT_ORDER],
            *[new_m[k] for k in WEIGHT_ORDER], *[new_v[k] for k in WEIGHT_ORDER])
```

```python
import functools

import jax
import jax.numpy as jnp
import numpy as np
from jax import lax
from jax.experimental import pallas as pl
from jax.experimental.pallas import tpu as pltpu

f32 = jnp.float32
bf16 = jnp.bfloat16

S = 2048
D = 1024
DEPTH = 4
NSH = 4
FS = 704
PROJ = 2560
PS = 640
PW = 256
AW = 768
NPAIR = 6
NORM_EPS = 1e-6
MASK_VALUE = -1e30
ROPE_THETA = 500000.0
DILATIONS = (1, 4, 16)
QBLK = 128
NBLK = S // QBLK
TM = 512
EW_ROWS = 16
VMEM_LIMIT = 56 * 1024 * 1024

ADAM_LR = 0.001
ADAM_B1 = 0.9
ADAM_B2 = 0.999
ADAM_EPS = 1e-08
ADAM_WD = 0.01
ADAM_STEP = 10

MESH = pl.DeviceIdType.MESH
ANY = pl.BlockSpec(memory_space=pl.ANY)

BIG = (("g1", FS, D), ("u1", FS, D), ("d1", FS, D), ("wi", D, PS), ("wo", PW, D), ("g2", FS, D), ("u2", FS, D), ("d2", FS, D))
TRANSPOSED = ("ffn1_w_gate", "ffn1_w_up", "ffn2_w_gate", "ffn2_w_up")
FFN1 = 3
BIG_SRC = {"g1": "ffn1_w_gate", "u1": "ffn1_w_up", "d1": "ffn1_w_down", "wi": "w_in", "wo": "w_out",
           "g2": "ffn2_w_gate", "u2": "ffn2_w_up", "d2": "ffn2_w_down"}


def _cp(*sem):
    return pltpu.CompilerParams(dimension_semantics=sem if sem else None, vmem_limit_bytes=VMEM_LIMIT)


def _sds(shape, dt):
    return jax.ShapeDtypeStruct(shape, dt)


def _dot(a, b):
    return jnp.dot(a, b, preferred_element_type=f32)


def _dot_nt(a, b):
    return lax.dot_general(a, b, (((1,), (1,)), ((), ())), preferred_element_type=f32)


def _dot_tn(a, b):
    return lax.dot_general(a, b, (((0,), (0,)), ((), ())), preferred_element_type=f32)


def _dep(dep):
    return ([], []) if dep is None else ([ANY], [dep])


def _resident(shape):
    return pl.BlockSpec(shape, lambda i: (0,) * len(shape), pipeline_mode=pl.Buffered(1))


def _ffn_fwd(x, g, wg, wu, wd, dep=None):
    def body(x_ref, g_ref, wg_ref, wu_ref, wd_ref, *rest):
        xo_ref, h_ref, a_ref, b_ref = rest[-4:]
        xf = x_ref[...]
        r = lax.rsqrt(jnp.mean(xf * xf, axis=-1, keepdims=True) + NORM_EPS)
        hh = ((xf * r) * g_ref[...]).astype(bf16)
        h_ref[...] = hh
        acc = None
        for s in range(NSH):
            a = _dot_nt(hh, wg_ref[s])
            b = _dot_nt(hh, wu_ref[s])
            a_ref[s] = a.astype(bf16)
            b_ref[s] = b.astype(bf16)
            p = _dot((a * (1.0 / (1.0 + jnp.exp(-a))) * b).astype(bf16), wd_ref[s])
            acc = p if acc is None else acc + p
        xo_ref[...] = xf + 0.5 * acc

    tok = pl.BlockSpec((TM, D), lambda i: (i, 0))
    hid = pl.BlockSpec((NSH, TM, FS), lambda i: (0, i, 0))
    wsp = _resident((NSH, FS, D))
    dspec, dop = _dep(dep)
    return pl.pallas_call(
        body, out_shape=(_sds((S, D), f32), _sds((S, D), bf16), _sds((NSH, S, FS), bf16), _sds((NSH, S, FS), bf16)),
        grid=(S // TM,), in_specs=[tok, pl.BlockSpec((1, D), lambda i: (0, 0)), wsp, wsp, wsp] + dspec,
        out_specs=(tok, tok, hid, hid), name="ffn_fwd", compiler_params=_cp("parallel"))(x, g, wg, wu, wd, *dop)


def _in_proj(x, g, wi):
    def body(x_ref, g_ref, w_ref, o_ref, h_ref):
        xf = x_ref[...]
        r = lax.rsqrt(jnp.mean(xf * xf, axis=-1, keepdims=True) + NORM_EPS)
        hh = ((xf * r) * g_ref[...]).astype(bf16)
        h_ref[...] = hh
        for s in range(NSH):
            o_ref[:, PS * s:PS * (s + 1)] = _dot(hh, w_ref[s])

    tok = pl.BlockSpec((TM, D), lambda i: (i, 0))
    return pl.pallas_call(
        body, out_shape=(_sds((S, PROJ), f32), _sds((S, D), bf16)), grid=(S // TM,),
        in_specs=[tok, pl.BlockSpec((1, D), lambda i: (0, 0)), _resident((NSH, D, PS))],
        out_specs=(pl.BlockSpec((TM, PROJ), lambda i: (i, 0)), tok), name="in_proj", compiler_params=_cp("parallel"))(x, g, wi)


def _out_proj(x, mixed, wo):
    def body(x_ref, m_ref, w_ref, o_ref):
        o_ref[...] = x_ref[...] + _dot(m_ref[...], w_ref[...].reshape(D, D))

    return pl.pallas_call(
        body, out_shape=_sds((S, D), f32), grid=(S // TM,),
        in_specs=[pl.BlockSpec((TM, D), lambda i: (i, 0)), pl.BlockSpec((TM, D), lambda i: (i, 0)),
                  pl.BlockSpec((NSH, PW, D), lambda i: (0, 0, 0))],
        out_specs=pl.BlockSpec((TM, D), lambda i: (i, 0)), name="out_proj", compiler_params=_cp("parallel"))(x, mixed, wo)


def _out_proj_bwd(dx, wo):
    def body(dx_ref, w_ref, o_ref):
        o_ref[...] = _dot_nt(dx_ref[...].astype(bf16), w_ref[...].reshape(D, D))

    return pl.pallas_call(
        body, out_shape=_sds((S, D), f32), grid=(S // TM,),
        in_specs=[pl.BlockSpec((TM, D), lambda i: (i, 0)), pl.BlockSpec((NSH, PW, D), lambda i: (0, 0, 0))],
        out_specs=pl.BlockSpec((TM, D), lambda i: (i, 0)), name="out_proj_bwd", compiler_params=_cp("parallel"))(dx, wo)


def _ffn_bwd_mid(dx, h, a, b, wd, dep=None):
    nt = S // TM

    def body(dx_ref, h_ref, a_ref, b_ref, wd_ref, *rest):
        da_ref, db_ref, dwd_ref, dwg_ref, dwu_ref, dy_s, u_s, da_s, db_s = rest[-9:]
        i = pl.program_id(1)
        rows = pl.ds(pl.multiple_of(i * TM, TM), TM)
        dy = (0.5 * dx_ref[...]).astype(bf16)
        dy_s[rows, :] = dy
        du = _dot_nt(dy, wd_ref[0])
        a = a_ref[0].astype(f32)
        b = b_ref[0].astype(f32)
        sig = 1.0 / (1.0 + jnp.exp(-a))
        silu = a * sig
        da = (du * b * (sig * (1.0 + a * (1.0 - sig)))).astype(bf16)
        db = (du * silu).astype(bf16)
        da_ref[0] = da
        db_ref[0] = db
        da_s[rows, :] = da
        db_s[rows, :] = db
        u_s[rows, :] = (silu * b).astype(bf16)

        @pl.when(i == nt - 1)
        def _():
            hh = h_ref[...]
            dwd_ref[...] = _dot_tn(u_s[...], dy_s[...]).astype(bf16).reshape(dwd_ref.shape)
            dwg_ref[...] = _dot_tn(da_s[...], hh).astype(bf16).reshape(dwg_ref.shape)
            dwu_ref[...] = _dot_tn(db_s[...], hh).astype(bf16).reshape(dwu_ref.shape)

    tok = pl.BlockSpec((TM, D), lambda s, i: (i, 0))
    hid = pl.BlockSpec((1, TM, FS), lambda s, i: (s, i, 0))
    wsp = pl.BlockSpec((1, 2, FS // 2, D), lambda s, i: (s, 0, 0, 0))
    hidden = _sds((NSH, S, FS), bf16)
    wgrad = _sds((NSH, 2, FS // 2, D), bf16)
    whole = pltpu.VMEM((S, FS), bf16)
    dspec, dop = _dep(dep)
    return pl.pallas_call(
        body, out_shape=(hidden, hidden, wgrad, wgrad, wgrad), grid=(NSH, nt),
        in_specs=[tok, pl.BlockSpec((S, D), lambda s, i: (0, 0), pipeline_mode=pl.Buffered(1)), hid, hid,
                  pl.BlockSpec((1, FS, D), lambda s, i: (s, 0, 0))] + dspec,
        out_specs=(hid, hid, wsp, wsp, wsp), scratch_shapes=[pltpu.VMEM((S, D), bf16), whole, whole, whole],
        name="ffn_bwd_mid", compiler_params=_cp("parallel", "arbitrary"))(dx, h, a, b, wd, *dop)


def _norm_bwd_tail(acc, x_ref, dxin_ref, g_ref, dxo_ref, dg_ref, first):
    xf = x_ref[...]
    r = lax.rsqrt(jnp.mean(xf * xf, axis=-1, keepdims=True) + NORM_EPS)
    xhat = xf * r
    dhg = acc * g_ref[...]
    dxo_ref[...] = dxin_ref[...] + r * (dhg - xhat * jnp.mean(dhg * xhat, axis=-1, keepdims=True))
    part = jnp.sum(acc * xhat, axis=0, keepdims=True)

    @pl.when(first)
    def _():
        dg_ref[...] = part

    @pl.when(jnp.logical_not(first))
    def _():
        dg_ref[...] += part


def _ffn_bwd_dx(dx, x_in, g, da, db, wg, wu):
    def body(dx_ref, x_ref, g_ref, da_ref, db_ref, wg_ref, wu_ref, dxo_ref, dg_ref):
        acc = None
        for s in range(NSH):
            p = _dot(da_ref[s], wg_ref[s])
            acc = p if acc is None else acc + p
            acc = acc + _dot(db_ref[s], wu_ref[s])
        _norm_bwd_tail(acc, x_ref, dx_ref, g_ref, dxo_ref, dg_ref, pl.program_id(0) == 0)

    tok = pl.BlockSpec((TM, D), lambda i: (i, 0))
    vec = pl.BlockSpec((1, D), lambda i: (0, 0))
    hid = pl.BlockSpec((NSH, TM, FS), lambda i: (0, i, 0))
    wsp = _resident((NSH, FS, D))
    return pl.pallas_call(
        body, out_shape=(_sds((S, D), f32), _sds((1, D), f32)), grid=(S // TM,),
        in_specs=[tok, tok, vec, hid, hid, wsp, wsp], out_specs=(tok, vec),
        name="ffn_bwd_dx", compiler_params=_cp("arbitrary"))(dx, x_in, g, da, db, wg, wu)


def _in_proj_bwd_dx(dx, x_in, g, dproj, wi):
    def body(dx_ref, x_ref, g_ref, dp_ref, w_ref, dxo_ref, dg_ref):
        acc = None
        for s in range(NSH):
            p = _dot_nt(dp_ref[:, PS * s:PS * (s + 1)], w_ref[s])
            acc = p if acc is None else acc + p
        _norm_bwd_tail(acc, x_ref, dx_ref, g_ref, dxo_ref, dg_ref, pl.program_id(0) == 0)

    tok = pl.BlockSpec((TM, D), lambda i: (i, 0))
    vec = pl.BlockSpec((1, D), lambda i: (0, 0))
    return pl.pallas_call(
        body, out_shape=(_sds((S, D), f32), _sds((1, D), f32)), grid=(S // TM,),
        in_specs=[tok, tok, vec, pl.BlockSpec((TM, PROJ), lambda i: (i, 0)), _resident((NSH, D, PS))], out_specs=(tok, vec),
        name="in_proj_bwd_dx", compiler_params=_cp("arbitrary"))(dx, x_in, g, dproj, wi)


def _dw(lhs, rhs, lhs_spec, rhs_spec, rows, cols, name, cast_rhs=False):
    def body(l_ref, r_ref, o_ref):
        r = r_ref[...].astype(bf16) if cast_rhs else r_ref[...]
        o_ref[...] = _dot_tn(l_ref[...], r).astype(bf16).reshape(1, 2, rows // 2, cols)

    return pl.pallas_call(
        body, out_shape=_sds((NSH, 2, rows // 2, cols), bf16), grid=(NSH,), in_specs=[lhs_spec, rhs_spec],
        out_specs=pl.BlockSpec((1, 2, rows // 2, cols), lambda s: (s, 0, 0, 0)), name=name, compiler_params=_cp("parallel"))(lhs, rhs)


_WHOLE_TOK = pl.BlockSpec((S, D), lambda s: (0, 0))


def _dw_in(h, dproj):
    return _dw(h, dproj, _WHOLE_TOK, pl.BlockSpec((S, PS), lambda s: (0, s)), D, PS, "dw_in")


def _dw_out(mixed, dx):
    return _dw(mixed, dx, pl.BlockSpec((S, PW), lambda s: (0, s)), _WHOLE_TOK, PW, D, "dw_out", cast_rhs=True)


def _final_loss(x, g, target):
    def body(x_ref, g_ref, t_ref, loss_ref, dx_ref, dg_ref):
        i = pl.program_id(0)
        xf = x_ref[...]
        r = lax.rsqrt(jnp.mean(xf * xf, axis=-1, keepdims=True) + NORM_EPS)
        xhat = xf * r
        err = xhat * g_ref[...] - t_ref[...]
        dy = err * (1.0 / D)
        dhg = dy * g_ref[...]
        dx_ref[...] = r * (dhg - xhat * jnp.mean(dhg * xhat, axis=-1, keepdims=True))
        part = jnp.sum(dy * xhat, axis=0, keepdims=True)
        lpart = jnp.zeros((8, 128), f32) + 0.5 * jnp.sum(jnp.mean(err * err, axis=-1, keepdims=True))

        @pl.when(i == 0)
        def _():
            dg_ref[...] = part
            loss_ref[...] = lpart

        @pl.when(i != 0)
        def _():
            dg_ref[...] += part
            loss_ref[...] += lpart

    tok = pl.BlockSpec((TM, D), lambda i: (i, 0))
    vec = pl.BlockSpec((1, D), lambda i: (0, 0))
    return pl.pallas_call(
        body, out_shape=(_sds((8, 128), f32), _sds((S, D), f32), _sds((1, D), f32)), grid=(S // TM,),
        in_specs=[tok, vec, tok], out_specs=(pl.BlockSpec((8, 128), lambda i: (0, 0)), tok, vec),
        name="final_loss", compiler_params=_cp("arbitrary"))(x, g, target)


def _shift_down(x, k, row):
    return jnp.where(row >= k, pltpu.roll(x, k, axis=0), 0.0)


def _shift_up(x, k, row):
    return jnp.where(row < S - k, pltpu.roll(x, S - k, axis=0), 0.0)


def _pool_geometry():
    row = lax.broadcasted_iota(jnp.int32, (S, PW), 0)
    grp = lax.broadcasted_iota(jnp.int32, (S, PW), 1) // 64
    half = jnp.where(grp == 0, 1, jnp.where(grp == 1, 2, jnp.where(grp == 2, 4, 8)))
    hi = jnp.minimum(row + half - 1, S - 1)
    lo = jnp.maximum(row - half, 0)
    return row, grp, (hi - lo + 1).astype(f32)


def _by_group(grp, v0, v1, v2, v3):
    return jnp.where(grp == 0, v0, jnp.where(grp == 1, v1, jnp.where(grp == 2, v2, v3)))


def _window_sums(x, row, grp, transpose):
    l1, r1 = x, x
    l2, r2 = l1 + _shift_down(l1, 1, row), r1 + _shift_up(r1, 1, row)
    l4, r4 = l2 + _shift_down(l2, 2, row), r2 + _shift_up(r2, 2, row)
    l8, r8 = l4 + _shift_down(l4, 4, row), r4 + _shift_up(r4, 4, row)
    lsel = _by_group(grp, l1, l2, l4, l8)
    rsel = _by_group(grp, r1, r2, r4, r8)
    if transpose:
        return lsel + _shift_up(rsel, 1, row)
    return _shift_down(lsel, 1, row) + rsel


def _pool_fwd(proj, wbd, scale):
    def body(v_ref, w_ref, sc_ref, mixed_ref, diff_ref):
        row, grp, cnt = _pool_geometry()
        v = v_ref[...]
        diff = (_window_sums(v, row, grp, False) / cnt - v).astype(bf16)
        diff_ref[...] = diff
        mixed_ref[...] = (_dot(diff, w_ref[...].astype(bf16)) * sc_ref[...]).astype(bf16)

    col = pl.BlockSpec((S, PW), lambda i: (0, 0))
    return pl.pallas_call(
        body, out_shape=(_sds((S, D), bf16), _sds((S, PW), bf16)), grid=(1,),
        in_specs=[col, pl.BlockSpec((PW, PW), lambda i: (0, 0)), pl.BlockSpec((1, PW), lambda i: (0, 0))],
        out_specs=(col, col), name="pool_fwd", compiler_params=_cp("arbitrary"))(proj, wbd, scale)


def _pool_bwd(dmixed, diff, wbd, scale, dproj):
    def body(dy_ref, diff_ref, w_ref, sc_ref, dproj_in, dv_ref, dw_ref, dsc_ref):
        del dproj_in
        row, grp, cnt = _pool_geometry()
        dy = dy_ref[...]
        diff = diff_ref[...]
        w = w_ref[...].astype(bf16)
        dsc_ref[...] = jnp.sum(dy * _dot(diff, w), axis=0, keepdims=True)
        dys = (dy * sc_ref[...]).astype(bf16)
        dw_ref[...] = _dot_tn(diff, dys)
        ddiff = _dot_nt(dys, w)
        dv_ref[...] = (_window_sums(ddiff / cnt, row, grp, True) - ddiff).astype(bf16)

    col = pl.BlockSpec((S, PW), lambda i: (0, 0))
    return pl.pallas_call(
        body, out_shape=(_sds((S, PROJ), bf16), _sds((PW, PW), f32), _sds((1, PW), f32)), grid=(1,),
        in_specs=[col, col, pl.BlockSpec((PW, PW), lambda i: (0, 0)), pl.BlockSpec((1, PW), lambda i: (0, 0)), ANY],
        out_specs=(col, pl.BlockSpec((PW, PW), lambda i: (0, 0)), pl.BlockSpec((1, PW), lambda i: (0, 0))),
        input_output_aliases={4: 0}, name="pool_bwd", compiler_params=_cp("arbitrary"))(dmixed, diff, wbd, scale, dproj)


def _rope_tables(pos_col, freq_row):
    def body(p_ref, f_ref, c_ref, a_ref, b_ref):
        ang = p_ref[...].astype(f32) * f_ref[...]
        l64 = lax.broadcasted_iota(jnp.int32, (S, 128), 1) % 64
        cos, sin = jnp.cos(ang), jnp.sin(ang)
        c_ref[...] = jnp.where(l64 < 16, cos, 1.0)
        a_ref[...] = jnp.where(l64 < 8, -sin, 0.0)
        b_ref[...] = jnp.where((l64 >= 8) & (l64 < 16), sin, 0.0)

    t = _sds((S, 128), f32)
    return pl.pallas_call(body, out_shape=(t, t, t), name="rope_tables", compiler_params=_cp())(pos_col, freq_row)


def _rope(t, c, a, b):
    return t * c + pltpu.roll(t, 120, axis=1) * a + pltpu.roll(t, 8, axis=1) * b


def _rope_bwd(g, c, a, b):
    return g * c + pltpu.roll(g * a, 8, axis=1) + pltpu.roll(g * b, 120, axis=1)


def _perm_load(ref, d):
    if d == 1:
        return ref[...]
    n = S // d
    return jnp.concatenate([ref[pl.ds(r, n, stride=d), :] for r in range(d)], axis=0)


def _unperm_store(ref, val, d):
    if d == 1:
        ref[...] = val
        return
    n = S // d
    for r in range(d):
        ref[pl.ds(r, n, stride=d), :] = val[r * n:(r + 1) * n, :]


def _band(xp, d):
    if d == NBLK:
        return xp.reshape(NBLK, QBLK, 128)
    z = jnp.zeros((64, 128), bf16)
    p = jnp.concatenate([z, xp, z], axis=0).reshape(NBLK + 1, QBLK, 128)
    return jnp.concatenate([p[:NBLK], p[1:]], axis=1)


def _unband(xb, d):
    if d == NBLK:
        return xb.reshape(S, 128)
    z = jnp.zeros((1, QBLK, 128), f32)
    p = jnp.concatenate([xb[:, :QBLK], z], axis=0) + jnp.concatenate([z, xb[:, QBLK:]], axis=0)
    return p.reshape(S + QBLK, 128)[64:S + 64]


def _band_mask(d):
    if d == NBLK:
        a = lax.broadcasted_iota(jnp.int32, (1, 2 * QBLK, QBLK), 1) & (QBLK - 1)
        b = lax.broadcasted_iota(jnp.int32, (1, 2 * QBLK, QBLK), 2)
        return (b >= a - 64) & (b <= a + 64)
    blocks_per_class = NBLK // d
    n = lax.broadcasted_iota(jnp.int32, (NBLK, 1, 2 * QBLK), 0) & (blocks_per_class - 1)
    be = lax.broadcasted_iota(jnp.int32, (NBLK, 1, 2 * QBLK), 2)
    a = lax.broadcasted_iota(jnp.int32, (1, 2 * QBLK, 2 * QBLK), 1) & (QBLK - 1)
    b = lax.broadcasted_iota(jnp.int32, (1, 2 * QBLK, 2 * QBLK), 2)
    band = (b >= a) & (b <= a + 128)
    edge = ((be >= 64) | (n != 0)) & ((be < QBLK + 64) | (n != blocks_per_class - 1))
    return band & edge


def _stack_heads(xb, lo):
    z = jnp.zeros_like(xb)
    return jnp.concatenate([jnp.where(lo, xb, z), jnp.where(lo, z, xb)], axis=1)


def _unstack_heads(x2, lo):
    return jnp.where(lo, x2[:, :QBLK], x2[:, QBLK:])


def _rows_to_lanes(col2, lo):
    return jnp.where(lo, jnp.broadcast_to(col2[:, :QBLK], (NBLK, QBLK, 128)), jnp.broadcast_to(col2[:, QBLK:], (NBLK, QBLK, 128)))


def _bmm_nt(a, b):
    return jnp.einsum('nqd,nkd->nqk', a, b, preferred_element_type=f32)


def _bmm_nn(a, b):
    return jnp.einsum('nqk,nkd->nqd', a, b, preferred_element_type=f32)


def _bmm_tn(a, b):
    return jnp.einsum('nqk,nqd->nkd', a, b, preferred_element_type=f32)


def _attn_fwd(proj, tc, ta, tb, mixed):
    def body(q_ref, k_ref, v_ref, c_ref, a_ref, b_ref, mixed_in, mixed_ref, o_ref, lse_ref, qn, kn, t_num, t_m, t_den):
        del mixed_in
        lo = lax.broadcasted_iota(jnp.int32, (1, 1, 128), 2) < 64
        c, a, b = c_ref[...], a_ref[...], b_ref[...]
        qn[...] = _rope(q_ref[...], c, a, b)
        kn[...] = _rope(k_ref[...], c, a, b)
        run = None
        for d in DILATIONS:
            q2 = _stack_heads(_perm_load(qn, d).astype(bf16).reshape(NBLK, QBLK, 128), lo)
            kb = _band(_perm_load(kn, d).astype(bf16), d)
            vb = _band(_perm_load(v_ref, d).astype(bf16), d)
            s = jnp.where(_band_mask(d), _bmm_nt(q2, kb) * 0.125, MASK_VALUE)
            m = jnp.max(s, axis=2, keepdims=True)
            p = jnp.exp(s - m)
            den = jnp.sum(p, axis=2, keepdims=True)
            num = _unstack_heads(_bmm_nn(p.astype(bf16), vb), lo)
            _unperm_store(t_num, num.reshape(S, 128), d)
            _unperm_store(t_m, _rows_to_lanes(m, lo).reshape(S, 128), d)
            _unperm_store(t_den, _rows_to_lanes(den, lo).reshape(S, 128), d)
            if run is None:
                run = (t_m[...], t_num[...], t_den[...])
            else:
                m_new = jnp.maximum(run[0], t_m[...])
                w_old, w_new = jnp.exp(run[0] - m_new), jnp.exp(t_m[...] - m_new)
                run = (m_new, w_old * run[1] + w_new * t_num[...], w_old * run[2] + w_new * t_den[...])
        out = run[1] / run[2]
        o_ref[...] = out
        mixed_ref[...] = out.astype(bf16)
        lse_ref[...] = run[0] + jnp.log(run[2])

    def col(off):
        return pl.BlockSpec((S, 128), lambda j, off=off: (0, off + j))

    tab = pl.BlockSpec((S, 128), lambda j: (0, 0))
    scr = pltpu.VMEM((S, 128), f32)
    return pl.pallas_call(
        body, out_shape=(_sds((S, D), bf16), _sds((S, AW), f32), _sds((S, AW), f32)), grid=(NPAIR,),
        in_specs=[col(2), col(8), col(14), tab, tab, tab, ANY], out_specs=(col(2), col(0), col(0)),
        scratch_shapes=[scr, scr, scr, scr, scr], input_output_aliases={6: 0}, name="attn_fwd",
        compiler_params=_cp("arbitrary"))(proj, proj, proj, tc, ta, tb, mixed)


def _attn_bwd(proj, tc, ta, tb, o, lse, dmixed):
    def body(q_ref, k_ref, v_ref, c_ref, a_ref, b_ref, o_ref, lse_ref, do_ref, dp_ref, qn, kn, tmp, dk_s, dv_s):
        t = pl.program_id(1)

        @pl.when(t == 0)
        def _():
            lo = lax.broadcasted_iota(jnp.int32, (1, 1, 128), 2) < 64
            c, a, b = c_ref[...], a_ref[...], b_ref[...]
            qn[...] = _rope(q_ref[...], c, a, b)
            kn[...] = _rope(k_ref[...], c, a, b)
            dq = dk = dv = None
            for d in DILATIONS:
                q2 = _stack_heads(_perm_load(qn, d).astype(bf16).reshape(NBLK, QBLK, 128), lo)
                kb = _band(_perm_load(kn, d).astype(bf16), d)
                vb = _band(_perm_load(v_ref, d).astype(bf16), d)
                dob = _perm_load(do_ref, d).reshape(NBLK, QBLK, 128)
                ob = _perm_load(o_ref, d).reshape(NBLK, QBLK, 128)
                lsb = _perm_load(lse_ref, d).reshape(NBLK, QBLK, 128)
                do2 = _stack_heads(dob.astype(bf16), lo)
                delta2 = jnp.sum(_stack_heads(dob * ob, lo), axis=2, keepdims=True)
                lse2 = jnp.max(jnp.concatenate([jnp.where(lo, lsb, MASK_VALUE), jnp.where(lo, MASK_VALUE, lsb)], axis=1),
                               axis=2, keepdims=True)
                s = _bmm_nt(q2, kb) * 0.125
                p = jnp.where(_band_mask(d), jnp.exp(s - lse2), 0.0)
                ds = (p * (_bmm_nt(do2, vb) - delta2) * 0.125).astype(bf16)
                pb = p.astype(bf16)
                dq_b = _unstack_heads(_bmm_nn(ds, kb), lo).reshape(S, 128)
                dk_b = _unband(_bmm_tn(ds, q2), d)
                dv_b = _unband(_bmm_tn(pb, do2), d)
                acc = []
                for prev, new in ((dq, dq_b), (dk, dk_b), (dv, dv_b)):
                    _unperm_store(tmp, new, d)
                    acc.append(tmp[...] if prev is None else prev + tmp[...])
                dq, dk, dv = acc
            dp_ref[...] = _rope_bwd(dq, c, a, b).astype(bf16)
            dk_s[...] = _rope_bwd(dk, c, a, b).astype(bf16)
            dv_s[...] = dv.astype(bf16)

        @pl.when(t == 1)
        def _():
            dp_ref[...] = dk_s[...]

        @pl.when(t == 2)
        def _():
            dp_ref[...] = dv_s[...]

    def col(off):
        return pl.BlockSpec((S, 128), lambda j, t, off=off: (0, off + j))

    tab = pl.BlockSpec((S, 128), lambda j, t: (0, 0))
    scr = pltpu.VMEM((S, 128), f32)
    scb = pltpu.VMEM((S, 128), bf16)
    return pl.pallas_call(
        body, out_shape=_sds((S, PROJ), bf16), grid=(NPAIR, 3),
        in_specs=[col(2), col(8), col(14), tab, tab, tab, col(0), col(0), col(2)],
        out_specs=pl.BlockSpec((S, 128), lambda j, t: (0, 2 + NPAIR * t + j)),
        scratch_shapes=[scr, scr, scr, scb, scb], name="attn_bwd",
        compiler_params=_cp("arbitrary", "arbitrary"))(proj, proj, proj, tc, ta, tb, o, lse, dmixed)


def _block_diag(w4):
    out = jnp.zeros((PW, PW), w4.dtype)
    for g in range(4):
        out = out.at[64 * g:64 * (g + 1), 64 * g:64 * (g + 1)].set(w4[g])
    return out


def _diag_blocks(w):
    return jnp.stack([w[64 * g:64 * (g + 1), 64 * g:64 * (g + 1)] for g in range(4)])


def _rope_inputs(positions):
    inv_freq = ROPE_THETA ** (-jnp.arange(0, 16, 2, dtype=f32) / 16)
    l64 = np.arange(128) % 64
    idx = np.where(l64 < 16, l64 % 8, 0)
    return positions.reshape(S, 1), inv_freq[idx].reshape(1, 128)


def _layer_fwd(x, w, small, l, tabs, dep=None, rest=None):
    g1, gm, g2 = (small[k][l].reshape(1, D) for k in ("ffn1_norm", "mix_norm", "ffn2_norm"))
    wbd = _block_diag(small["pool_w"][l])
    psc = small["pool_scale"][l].reshape(1, PW)
    x1, h1, a1, b1 = _ffn_fwd(x, g1, w["g1"], w["u1"], w["d1"], dep)
    if rest is not None:
        w = {**w, **rest(x1)}
    proj, h2 = _in_proj(x1, gm, w["wi"])
    mixed, diff = _pool_fwd(proj, wbd, psc)
    mixed, o, lse = _attn_fwd(proj, *tabs, mixed)
    x2 = _out_proj(x1, mixed, w["wo"])
    out, h3, a2, b2 = _ffn_fwd(x2, g2, w["g2"], w["u2"], w["d2"])
    return out, dict(x0=x, h1=h1, a1=a1, b1=b1, x1=x1, h2=h2, proj=proj, mixed=mixed, diff=diff, o=o, lse=lse,
                     x2=x2, h3=h3, a2=a2, b2=b2, g1=g1, gm=gm, g2=g2, wbd=wbd, psc=psc), w


def _layer_bwd(dx, w, sv, tabs, dep=None):
    gr, sg = {}, {}
    da, db, gr["d2"], gr["g2"], gr["u2"] = _ffn_bwd_mid(dx, sv["h3"], sv["a2"], sv["b2"], w["d2"], dep)
    dx, sg["ffn2_norm"] = _ffn_bwd_dx(dx, sv["x2"], sv["g2"], da, db, w["g2"], w["u2"])
    gr["wo"] = _dw_out(sv["mixed"], dx)
    dmixed = _out_proj_bwd(dx, w["wo"])
    dproj = _attn_bwd(sv["proj"], *tabs, sv["o"], sv["lse"], dmixed)
    dproj, dwbd, sg["pool_scale"] = _pool_bwd(dmixed, sv["diff"], sv["wbd"], sv["psc"], dproj)
    sg["pool_w"] = _diag_blocks(dwbd)
    gr["wi"] = _dw_in(sv["h2"], dproj)
    dx, sg["mix_norm"] = _in_proj_bwd_dx(dx, sv["x1"], sv["gm"], dproj, w["wi"])
    da, db, gr["d1"], gr["g1"], gr["u1"] = _ffn_bwd_mid(dx, sv["h1"], sv["a1"], sv["b1"], w["d1"])
    dx, sg["ffn1_norm"] = _ffn_bwd_dx(dx, sv["x0"], sv["g1"], da, db, w["g1"], w["u1"])
    return dx, gr, sg


def _forward_backward(x, positions, target, gathered, small):
    tabs = _rope_tables(*_rope_inputs(positions))
    saved = []
    for l in range(DEPTH):
        x, sv, _ = _layer_fwd(x, gathered[l], small, l, tabs)
        saved.append(sv)
    loss, dx, dgf = _final_loss(x, small["final_norm"].reshape(1, D), target)
    big = [None] * DEPTH
    sg = {k: [None] * DEPTH for k in ("ffn1_norm", "mix_norm", "pool_w", "pool_scale", "ffn2_norm")}
    for l in reversed(range(DEPTH)):
        dx, big[l], sgl = _layer_bwd(dx, gathered[l], saved[l], tabs)
        for k, v in sgl.items():
            sg[k][l] = v
    sg["final_norm"] = dgf
    return loss, dx, big, sg


def _place():
    x, y, c = lax.axis_index("x"), lax.axis_index("y"), lax.axis_index("c")
    chips = [(1 - x, y), (x, 1 - y), (1 - x, 1 - y)]
    return x, y, c, chips


def _cast_layer(params, l, place, dep=None):
    def body(p_ref, *refs):
        del p_ref
        for i_ref, o_ref in zip(refs[:8], refs[-8:]):
            o_ref[...] = i_ref[...].astype(bf16).reshape(o_ref.shape)

    ins, in_specs, out_shape, out_specs = [], [], [], []
    for name, rows, cols in BIG:
        q = rows // 4
        ins.append(params[BIG_SRC[name]])
        in_specs.append(pl.BlockSpec((1, q, cols), lambda i, p, l=l: (l, i, 0)))
        out_shape.append(_sds((NSH, 2, rows // 2, cols), bf16))
        out_specs.append(pl.BlockSpec((1, 1, q, cols), lambda i, p: (p[1], i // 2, i % 2, 0)))
    dspec, dop = _dep(dep)
    return pl.pallas_call(
        body, out_shape=out_shape,
        grid_spec=pltpu.PrefetchScalarGridSpec(num_scalar_prefetch=1, grid=(4,), in_specs=in_specs + dspec, out_specs=out_specs),
        name=f"cast_layer{l}", compiler_params=_cp("parallel"))(place, *ins, *dop)


HBM = pl.BlockSpec(memory_space=pltpu.HBM)
SEM = pl.BlockSpec(memory_space=pltpu.SEMAPHORE)
_SPLIT = pltpu.CompilerParams(has_side_effects=pltpu.SideEffectType.DATAFLOW_SIDE_EFFECTING)


def _hbm(arrays):
    return [pltpu.with_memory_space_constraint(a, pltpu.HBM) for a in arrays]


def _chip_copies(src_of, dst_of, send_sems, recv_sems, n):
    x, y, c, chips = _place()
    me = 2 * x + y
    out = []
    for t in range(n):
        for k, chip in enumerate(chips):
            peer = 2 * chip[0] + chip[1]
            send = pltpu.make_async_remote_copy(
                src_ref=src_of(t, peer), dst_ref=dst_of(t, me), send_sem=send_sems.at[3 * t + k], recv_sem=recv_sems.at[3 * t + k],
                device_id=(chip[0], chip[1], c), device_id_type=MESH)
            land = pltpu.make_async_remote_copy(
                src_ref=src_of(t, peer), dst_ref=dst_of(t, peer), send_sem=send_sems.at[3 * t + k], recv_sem=recv_sems.at[3 * t + k],
                device_id=(chip[0], chip[1], c), device_id_type=MESH)
            out.append((send, land))
    return out


def _exchange_start(src, land, after, src_of, dst_of, name):
    n, m = len(src), len(src) + len(land)

    def body(*refs):
        src_refs = refs[:n]
        land_refs = refs[n:m] if land else src_refs
        send_sems, recv_sems = refs[m + 1], refs[m + 2]
        token = refs[-1]
        for send, _ in _chip_copies(lambda t, s: src_of(src_refs[t], s), lambda t, s: dst_of(land_refs[t], s), send_sems, recv_sems, n):
            send.start()
        token[...] = jnp.zeros_like(token)

    arrays = list(src) + list(land)
    out_shape = ([pltpu.SemaphoreType.DMA((3 * n,)), pltpu.SemaphoreType.DMA((3 * n,))] + [pltpu.HBM(a.shape, a.dtype) for a in arrays]
                 + [_sds((8, 128), f32)])
    res = pl.pallas_call(
        body, out_shape=out_shape, in_specs=[HBM] * m + [ANY], out_specs=[SEM, SEM] + [HBM] * m + [pl.BlockSpec(memory_space=pltpu.VMEM)],
        input_output_aliases={i: 2 + i for i in range(m)}, name=name, compiler_params=_SPLIT)(*_hbm(arrays), after)
    return res[0], res[1], list(res[2:2 + n]), list(res[2 + n:2 + m]), res[-1]


def _exchange_wait(send_sems, recv_sems, src, land, after, src_of, dst_of, name):
    n, m = len(src), len(src) + len(land)

    def body(*refs):
        src_refs = refs[:n]
        land_refs = refs[n:m] if land else src_refs
        send_sems, recv_sems = refs[m], refs[m + 1]
        for send, land_cp in _chip_copies(lambda t, s: src_of(src_refs[t], s), lambda t, s: dst_of(land_refs[t], s), send_sems, recv_sems, n):
            send.wait_send()
            land_cp.wait_recv()

    arrays = list(src) + list(land)
    res = pl.pallas_call(
        body, out_shape=[pltpu.HBM(a.shape, a.dtype) for a in arrays], in_specs=[HBM] * m + [SEM, SEM, ANY], out_specs=[HBM] * m,
        input_output_aliases={i: i for i in range(m)}, name=name, compiler_params=_SPLIT)(*arrays, send_sems, recv_sems, after)
    return list(res[:n]), list(res[n:])


def _own_half(ref, s):
    x, y, c, _ = _place()
    return ref.at[2 * x + y, c]


def _slot_half(ref, s):
    return ref.at[s, lax.axis_index("c")]


def _slot(ref, s):
    return ref.at[s]


def _gather_forward(bufs):
    n = len(bufs)

    def body(*refs):
        outs = refs[n:2 * n]
        send_sems, recv_sems = refs[2 * n:]
        x, y, c, chips = _place()
        sibling = (x, y, 1 - c)
        passed = []
        for t in range(n):
            for k, chip in enumerate(chips):
                blk = outs[t].at[2 * chip[0] + chip[1], c]
                cp = pltpu.make_async_remote_copy(
                    src_ref=blk, dst_ref=blk, send_sem=send_sems.at[t, k], recv_sem=recv_sems.at[t, k],
                    device_id=sibling, device_id_type=MESH)
                cp.start()
                passed.append(cp)
        for t in range(n):
            for k, chip in enumerate(chips):
                blk = outs[t].at[2 * chip[0] + chip[1], 1 - c]
                pltpu.make_async_remote_copy(
                    src_ref=blk, dst_ref=blk, send_sem=send_sems.at[t, k], recv_sem=recv_sems.at[t, k],
                    device_id=sibling, device_id_type=MESH).wait_recv()
        for cp in passed:
            cp.wait_send()

    out_shape = [_sds(a.shape, bf16) for a in bufs]
    return pl.pallas_call(
        body, out_shape=out_shape, in_specs=[ANY] * n, out_specs=[ANY] * n, input_output_aliases={t: t for t in range(n)},
        scratch_shapes=[pltpu.SemaphoreType.DMA((n, 3)), pltpu.SemaphoreType.DMA((n, 3))], name="gather_forward")(*bufs)


def _sibling_swap(grads):
    n = len(grads)

    def body(*refs):
        ins, outs = refs[:n], refs[n:2 * n]
        send_sems, recv_sems = refs[2 * n:]
        x, y, c, _ = _place()
        cps = []
        for t in range(n):
            for s in range(NSH):
                cp = pltpu.make_async_remote_copy(
                    src_ref=ins[t].at[s, 1 - c], dst_ref=outs[t].at[s], send_sem=send_sems.at[t, s], recv_sem=recv_sems.at[t, s],
                    device_id=(x, y, 1 - c), device_id_type=MESH)
                cp.start()
                cps.append(cp)
        for cp in cps:
            cp.wait()

    out_shape = [_sds((NSH,) + a.shape[2:], bf16) for a in grads]
    return pl.pallas_call(
        body, out_shape=out_shape, in_specs=[ANY] * n, out_specs=[ANY] * n,
        scratch_shapes=[pltpu.SemaphoreType.DMA((n, NSH)), pltpu.SemaphoreType.DMA((n, NSH))],
        name="sibling_swap")(*grads)


def _row_tile(h):
    return h // 2 if h % 32 == 0 else h


def _pair_sum(grads, got, c_idx):
    n = len(grads)

    def body(c_ref, *refs):
        del c_ref
        for t in range(n):
            refs[2 * n + t][...] = (refs[t][...].astype(f32).reshape(refs[n + t].shape) + refs[n + t][...].astype(f32)).astype(bf16)

    in_specs, out_shape, out_specs = [], [], []
    for a in grads:
        h, cols = a.shape[2:]
        in_specs.append(pl.BlockSpec((1, 1, _row_tile(h), cols), lambda s, i, c: (s, c[0], i, 0)))
    for a in grads:
        h, cols = a.shape[2:]
        in_specs.append(pl.BlockSpec((1, _row_tile(h), cols), lambda s, i, c: (s, i, 0)))
        out_shape.append(_sds((NSH, h, cols), bf16))
        out_specs.append(pl.BlockSpec((1, _row_tile(h), cols), lambda s, i, c: (s, i, 0)))
    return pl.pallas_call(
        body, out_shape=out_shape,
        grid_spec=pltpu.PrefetchScalarGridSpec(num_scalar_prefetch=1, grid=(NSH, 2), in_specs=in_specs, out_specs=out_specs),
        name="pair_sum", compiler_params=_cp("parallel", "parallel"))(c_idx, *grads, *got)


def _chip_sum(psum, parts, full, place, l, name):
    n = len(parts)

    def body(p_ref, *refs):
        s = pl.program_id(1)
        for t in range(n):
            val = jnp.where(s == p_ref[1], refs[t][0], refs[n + t][0]).astype(f32)
            out = refs[3 * n + t]

            @pl.when(s == 0)
            def _(out=out, val=val):
                out[0, 0] = val

            @pl.when(s != 0)
            def _(out=out, val=val):
                out[0, 0] += val

    own_specs, part_specs, out_shape, out_specs = [], [], [], []
    for a, fl in zip(parts, full):
        _, h, cols = a.shape
        r = _row_tile(h)
        own_specs.append(pl.BlockSpec((1, r, cols), lambda i, s, p: (p[1], i, 0)))
        part_specs.append(pl.BlockSpec((1, r, cols), lambda i, s, p: (jnp.where(s == p[1], (s + 1) % NSH, s), i, 0)))
        out_shape.append(_sds(fl.shape, f32))
        out_specs.append(pl.BlockSpec((1, 1, r, cols), lambda i, s, p, l=l: (l, p[0], i, 0)))
    return pl.pallas_call(
        body, out_shape=out_shape,
        grid_spec=pltpu.PrefetchScalarGridSpec(num_scalar_prefetch=1, grid=(2, NSH), in_specs=own_specs + part_specs + [ANY] * n,
                                               out_specs=out_specs),
        input_output_aliases={1 + 2 * n + t: t for t in range(n)}, name=name,
        compiler_params=_cp("parallel", "arbitrary"))(place, *psum, *parts, *full)


def _sibling_share(full, l, name):
    n = len(full)

    def body(*refs):
        outs = refs[n:2 * n]
        send_sems, recv_sems = refs[2 * n:]
        x, y, c, _ = _place()
        sibling = (x, y, 1 - c)
        cps = []
        for t in range(n):
            blk = outs[t].at[l, c]
            cp = pltpu.make_async_remote_copy(
                src_ref=blk, dst_ref=blk, send_sem=send_sems.at[t], recv_sem=recv_sems.at[t], device_id=sibling, device_id_type=MESH)
            cp.start()
            cps.append(cp)
        for t in range(n):
            blk = outs[t].at[l, 1 - c]
            pltpu.make_async_remote_copy(
                src_ref=blk, dst_ref=blk, send_sem=send_sems.at[t], recv_sem=recv_sems.at[t],
                device_id=sibling, device_id_type=MESH).wait_recv()
        for cp in cps:
            cp.wait_send()

    out_shape = [_sds(a.shape, f32) for a in full]
    return pl.pallas_call(
        body, out_shape=out_shape, in_specs=[ANY] * n, out_specs=[ANY] * n, input_output_aliases={t: t for t in range(n)},
        scratch_shapes=[pltpu.SemaphoreType.DMA((n,)), pltpu.SemaphoreType.DMA((n,))], name=name)(*full)


SMALL_ROWS = 656


def _pack_small(per_layer, final_vec, loss_tile):
    rows = []
    for l in range(DEPTH):
        for k in ("ffn1_norm", "mix_norm", "ffn2_norm"):
            rows.append(per_layer[k][l].reshape(8, 128))
        rows.append(per_layer["pool_w"][l].reshape(128, 128))
        rows.append(jnp.pad(per_layer["pool_scale"][l].reshape(2, 128), ((0, 6), (0, 0))))
    rows.append(final_vec.reshape(8, 128))
    rows.append(loss_tile)
    return jnp.concatenate(rows, axis=0)


def _unpack_small(buf):
    out = {k: [] for k in ("ffn1_norm", "mix_norm", "ffn2_norm", "pool_w", "pool_scale")}
    r = 0
    for l in range(DEPTH):
        for k in ("ffn1_norm", "mix_norm", "ffn2_norm"):
            out[k].append(buf[r:r + 8].reshape(D))
            r += 8
        out["pool_w"].append(buf[r:r + 128].reshape(4, 64, 64))
        r += 128
        out["pool_scale"].append(buf[r:r + 2].reshape(PW))
        r += 8
    res = {k: jnp.stack(v) for k, v in out.items()}
    res["final_norm"] = buf[r:r + 8].reshape(D)
    res["loss"] = buf[r + 8, 0]
    return res


def _allreduce_small(buf):
    def body(in_ref, out_ref, slots, send_sems, recv_sems):
        x, y, c, _ = _place()
        me = 4 * x + 2 * y + c
        slots[me] = in_ref[...]
        peers = []
        for k in range(1, 8):
            px, py, pc = x ^ (k >> 2), y ^ ((k >> 1) & 1), c ^ (k & 1)
            cp = pltpu.make_async_remote_copy(
                src_ref=in_ref, dst_ref=slots.at[me], send_sem=send_sems.at[k - 1], recv_sem=recv_sems.at[k - 1],
                device_id=(px, py, pc), device_id_type=MESH)
            cp.start()
            peers.append(cp)
        for k in range(1, 8):
            px, py, pc = x ^ (k >> 2), y ^ ((k >> 1) & 1), c ^ (k & 1)
            slot = 4 * px + 2 * py + pc
            pltpu.make_async_remote_copy(
                src_ref=slots.at[slot], dst_ref=slots.at[slot], send_sem=send_sems.at[k - 1], recv_sem=recv_sems.at[k - 1],
                device_id=(px, py, pc), device_id_type=MESH).wait_recv()
        for cp in peers:
            cp.wait_send()
        acc = slots[0]
        for j in range(1, 8):
            acc = acc + slots[j]
        out_ref[...] = acc

    return pl.pallas_call(
        body, out_shape=_sds((SMALL_ROWS, 128), f32),
        in_specs=[pl.BlockSpec(memory_space=pltpu.VMEM)], out_specs=pl.BlockSpec(memory_space=pltpu.VMEM),
        scratch_shapes=[pltpu.VMEM((8, SMALL_ROWS, 128), f32), pltpu.SemaphoreType.DMA((7,)), pltpu.SemaphoreType.DMA((7,))],
        name="allreduce_small", compiler_params=_cp())(buf)


def _adamw_math(w, g, m, v):
    m = ADAM_B1 * m + (1.0 - ADAM_B1) * g
    v = ADAM_B2 * v + (1.0 - ADAM_B2) * (g * g)
    m_hat = m / (1.0 - ADAM_B1 ** ADAM_STEP)
    v_hat = v / (1.0 - ADAM_B2 ** ADAM_STEP)
    return -ADAM_LR * (m_hat / (jnp.sqrt(v_hat) + ADAM_EPS) + ADAM_WD * w), m, v


def _adamw(w, g, m, v, name, first=0, prev=None, dep=None):
    def body(w_ref, g_ref, m_ref, v_ref, *rest):
        go_ref, d_ref, mo_ref, vo_ref = rest[-4:]
        g = g_ref[...]
        d, mn, vn = _adamw_math(w_ref[...], g, m_ref[...], v_ref[...])
        go_ref[...] = g
        d_ref[...] = d
        mo_ref[...] = mn
        vo_ref[...] = vn

    _, rows, cols = w.shape
    r = rows // 4 if rows % 32 == 0 else rows
    spec = pl.BlockSpec((1, r, cols), lambda i, j: (first + i, j, 0))
    gspec = pl.BlockSpec((1, r, cols), lambda i, j: (i, j, 0))
    out = _sds(w.shape, f32)
    extra = [] if prev is None else list(prev)
    dspec, dop = _dep(dep)
    return pl.pallas_call(
        body, out_shape=(out, out, out, out), grid=(g.shape[0], rows // r), in_specs=[spec, gspec, spec, spec] + [ANY] * len(extra) + dspec,
        out_specs=(spec,) * 4, input_output_aliases={4 + i: i for i in range(len(extra))}, name=name,
        compiler_params=_cp("parallel", "parallel"))(w, g, m, v, *extra, *dop)


SMALL_NAMES = ("ffn1_norm", "mix_norm", "pool_w", "pool_scale", "ffn2_norm", "final_norm")
WEIGHT_ORDER = ("ffn1_norm", "ffn1_w_gate", "ffn1_w_up", "ffn1_w_down", "mix_norm", "w_in", "pool_w", "pool_scale", "w_out",
                "ffn2_norm", "ffn2_w_gate", "ffn2_w_up", "ffn2_w_down", "final_norm")


def _pack_small_params(p):
    per_layer = {k: [p[k][l] for l in range(DEPTH)] for k in ("ffn1_norm", "mix_norm", "ffn2_norm", "pool_w", "pool_scale")}
    return _pack_small(per_layer, p["final_norm"], jnp.zeros((8, 128), f32))


def kernel(x, positions, ffn1_norm, ffn1_w_gate, ffn1_w_up, ffn1_w_down, mix_norm, w_in, pool_w, pool_scale, w_out, ffn2_norm, ffn2_w_gate, ffn2_w_up, ffn2_w_down, final_norm, loss_target, m_ffn1_norm, m_ffn1_w_gate, m_ffn1_w_up, m_ffn1_w_down, m_mix_norm, m_w_in, m_pool_w, m_pool_scale, m_w_out, m_ffn2_norm, m_ffn2_w_gate, m_ffn2_w_up, m_ffn2_w_down, m_final_norm, v_ffn1_norm, v_ffn1_w_gate, v_ffn1_w_up, v_ffn1_w_down, v_mix_norm, v_w_in, v_pool_w, v_pool_scale, v_w_out, v_ffn2_norm, v_ffn2_w_gate, v_ffn2_w_up, v_ffn2_w_down, v_final_norm):
    params = dict(ffn1_norm=ffn1_norm, ffn1_w_gate=ffn1_w_gate, ffn1_w_up=ffn1_w_up, ffn1_w_down=ffn1_w_down,
                  mix_norm=mix_norm, w_in=w_in, pool_w=pool_w, pool_scale=pool_scale, w_out=w_out, ffn2_norm=ffn2_norm,
                  ffn2_w_gate=ffn2_w_gate, ffn2_w_up=ffn2_w_up, ffn2_w_down=ffn2_w_down, final_norm=final_norm)
    mom_m = dict(ffn1_norm=m_ffn1_norm, ffn1_w_gate=m_ffn1_w_gate, ffn1_w_up=m_ffn1_w_up, ffn1_w_down=m_ffn1_w_down,
                 mix_norm=m_mix_norm, w_in=m_w_in, pool_w=m_pool_w, pool_scale=m_pool_scale, w_out=m_w_out,
                 ffn2_norm=m_ffn2_norm, ffn2_w_gate=m_ffn2_w_gate, ffn2_w_up=m_ffn2_w_up, ffn2_w_down=m_ffn2_w_down,
                 final_norm=m_final_norm)
    mom_v = dict(ffn1_norm=v_ffn1_norm, ffn1_w_gate=v_ffn1_w_gate, ffn1_w_up=v_ffn1_w_up, ffn1_w_down=v_ffn1_w_down,
                 mix_norm=v_mix_norm, w_in=v_w_in, pool_w=v_pool_w, pool_scale=v_pool_scale, w_out=v_w_out,
                 ffn2_norm=v_ffn2_norm, ffn2_w_gate=v_ffn2_w_gate, ffn2_w_up=v_ffn2_w_up, ffn2_w_down=v_ffn2_w_down,
                 final_norm=v_final_norm)
    names = [t[0] for t in BIG]
    for d in (params, mom_m, mom_v):
        for k in TRANSPOSED:
            d[k] = jnp.swapaxes(d[k], 1, 2)

    place = jnp.stack([lax.axis_index("c"), 2 * lax.axis_index("x") + lax.axis_index("y")]).astype(jnp.int32)
    def gather_start(tag, cast, after):
        return _exchange_start(cast, [], after, _own_half, _slot_half, f"gather_start{tag}")

    def gather_end(started, after, tag, spec):
        send_sems, recv_sems, bufs, _, _ = started
        bufs, _ = _exchange_wait(send_sems, recv_sems, bufs, [], after, _own_half, _slot_half, f"gather_wait{tag}")
        return {nm: a.reshape(NSH, rows, cols) for (nm, rows, cols), a in zip(spec, _gather_forward(bufs))}

    tabs = _rope_tables(*_rope_inputs(positions))
    h = x.reshape(S, D)
    weights, saved = [], []
    cast0 = _cast_layer(params, 0, place)
    first = gather_start("0a", cast0[:FFN1], place)
    second = gather_start("0b", cast0[FFN1:], first[-1])
    after = second[-1]
    casts = {}
    for l in range(1, DEPTH):
        casts[l] = _cast_layer(params, l, place, after)
        after = casts[l][0]
    for l in range(DEPTH):
        if l == 0:
            w = gather_end(first, after, "0a", BIG[:FFN1])
            rest = lambda x1: gather_end(second, x1, "0b", BIG[FFN1:])
        else:
            w, rest = gather_end(started, after, l, BIG), None
        dep = None
        if l + 1 < DEPTH:
            started = gather_start(l + 1, casts[l + 1], w["g1"])
            dep = started[-1]
        h, sv, w = _layer_fwd(h, w, params, l, tabs, dep, rest)
        weights.append(w)
        saved.append(sv)
        after = h
    loss, dx, dgf = _final_loss(h, final_norm.reshape(1, D), loss_target.reshape(S, D))

    upper = [lax.empty((DEPTH - 1, 2, rows // 2, cols), f32) for _, rows, cols in BIG]
    lower = [lax.empty((1, 2, rows // 2, cols), f32) for _, rows, cols in BIG]
    sg = {k: [None] * DEPTH for k in ("ffn1_norm", "mix_norm", "pool_w", "pool_scale", "ffn2_norm")}
    sg["final_norm"] = dgf

    def reduce_end(started, after, l, full, slot):
        send_sems, recv_sems, psum, parts, _ = started
        psum, parts = _exchange_wait(send_sems, recv_sems, psum, parts, after, _slot, _slot, f"grad_wait{l}")
        return _sibling_share(_chip_sum(psum, parts, full, place, slot, f"chip_sum{l}"), slot, f"sibling_share{l}")

    started, dep = None, None
    for l in reversed(range(DEPTH)):
        dx, gr, sgl = _layer_bwd(dx, weights[l], saved[l], tabs, dep)
        for k, v in sgl.items():
            sg[k][l] = v
        if started is not None:
            upper = reduce_end(started, dx, l + 1, upper, l)
        grads = [gr[nm] for nm in names]
        psum = _pair_sum(grads, _sibling_swap(grads), place)
        parts = [lax.empty(a.shape, bf16) for a in psum]
        started = _exchange_start(psum, parts, place, _slot, _slot, f"grad_start{l}")
        dep = started[-1]

    big_out = {}
    for (nm, rows, cols), g in zip(BIG, upper):
        k = BIG_SRC[nm]
        big_out[k] = _adamw(params[k], g.reshape(DEPTH - 1, rows, cols), mom_m[k], mom_v[k], "adamw_upper_" + k, first=1, dep=dep)
        dep = big_out[k][1]
    lower = reduce_end(started, dep, 0, lower, 0)
    for (nm, rows, cols), g in zip(BIG, lower):
        k = BIG_SRC[nm]
        big_out[k] = _adamw(params[k], g.reshape(1, rows, cols), mom_m[k], mom_v[k], "adamw_lower_" + k, first=0, prev=big_out[k])

    per_layer = {k: sg[k] for k in ("ffn1_norm", "mix_norm", "ffn2_norm", "pool_w", "pool_scale")}
    small_sum = _allreduce_small(_pack_small(per_layer, sg["final_norm"], loss))
    gs, ds_, ms, vs = _adamw(_pack_small_params(params).reshape(1, SMALL_ROWS, 128), small_sum.reshape(1, SMALL_ROWS, 128),
                             _pack_small_params(mom_m).reshape(1, SMALL_ROWS, 128),
                             _pack_small_params(mom_v).reshape(1, SMALL_ROWS, 128), "adamw_small")
    small_out = [_unpack_small(a.reshape(SMALL_ROWS, 128)) for a in (gs, ds_, ms, vs)]

    grad, delta, new_m, new_v = {}, {}, {}, {}
    for k in WEIGHT_ORDER:
        if k in SMALL_NAMES:
            grad[k], delta[k], new_m[k], new_v[k] = (so[k] for so in small_out)
        else:
            grad[k], delta[k], new_m[k], new_v[k] = big_out[k]
    for d in (grad, delta, new_m, new_v):
        for k in TRANSPOSED:
            d[k] = jnp.swapaxes(d[k], 1, 2)
    return (small_out[0]["loss"], dx.reshape(1, S, D), *[grad[k] for k in WEIGHT_ORDER], *[delta[k] for k in WEIGHT_ORDER],
            *[new_m[k] for k in WEIGHT_ORDER], *[new_v[k] for k in WEIGHT_ORDER])
```

```python
import functools

import jax
import jax.numpy as jnp
import numpy as np
from jax import lax
from jax.experimental import pallas as pl
from jax.experimental.pallas import tpu as pltpu

f32 = jnp.float32
bf16 = jnp.bfloat16

S = 2048
D = 1024
DEPTH = 4
NSH = 4
FS = 704
PROJ = 2560
PS = 640
PW = 256
AW = 768
NPAIR = 6
NORM_EPS = 1e-6
MASK_VALUE = -1e30
ROPE_THETA = 500000.0
DILATIONS = (1, 4, 16)
QBLK = 128
NBLK = S // QBLK
TM = 512
VMEM_LIMIT = 56 * 1024 * 1024

ADAM_LR = 0.001
ADAM_B1 = 0.9
ADAM_B2 = 0.999
ADAM_EPS = 1e-08
ADAM_WD = 0.01
ADAM_STEP = 10

MESH = pl.DeviceIdType.MESH
ANY = pl.BlockSpec(memory_space=pl.ANY)

BIG = (("g1", FS, D), ("u1", FS, D), ("d1", FS, D), ("wi", D, PS), ("wo", PW, D), ("g2", FS, D), ("u2", FS, D), ("d2", FS, D))
TRANSPOSED = ("ffn1_w_gate", "ffn1_w_up", "ffn2_w_gate", "ffn2_w_up")
FFN1 = 3
BIG_SRC = {"g1": "ffn1_w_gate", "u1": "ffn1_w_up", "d1": "ffn1_w_down", "wi": "w_in", "wo": "w_out",
           "g2": "ffn2_w_gate", "u2": "ffn2_w_up", "d2": "ffn2_w_down"}


def _cp(*sem):
    return pltpu.CompilerParams(dimension_semantics=sem if sem else None, vmem_limit_bytes=VMEM_LIMIT)


def _sds(shape, dt):
    return jax.ShapeDtypeStruct(shape, dt)


def _dot(a, b):
    return jnp.dot(a, b, preferred_element_type=f32)


def _dot_nt(a, b):
    return lax.dot_general(a, b, (((1,), (1,)), ((), ())), preferred_element_type=f32)


def _dot_tn(a, b):
    return lax.dot_general(a, b, (((0,), (0,)), ((), ())), preferred_element_type=f32)


def _dep(dep):
    return ([], []) if dep is None else ([ANY], [dep])


def _resident(shape):
    return pl.BlockSpec(shape, lambda i: (0,) * len(shape), pipeline_mode=pl.Buffered(1))


def _ffn_fwd(x, g, wg, wu, wd, dep=None):
    def body(x_ref, g_ref, wg_ref, wu_ref, wd_ref, *rest):
        xo_ref, h_ref, a_ref, b_ref = rest[-4:]
        xf = x_ref[...]
        r = lax.rsqrt(jnp.mean(xf * xf, axis=-1, keepdims=True) + NORM_EPS)
        hh = ((xf * r) * g_ref[...]).astype(bf16)
        h_ref[...] = hh
        acc = None
        for s in range(NSH):
            a = _dot_nt(hh, wg_ref[s])
            b = _dot_nt(hh, wu_ref[s])
            a_ref[s] = a.astype(bf16)
            b_ref[s] = b.astype(bf16)
            p = _dot((a * (1.0 / (1.0 + jnp.exp(-a))) * b).astype(bf16), wd_ref[s])
            acc = p if acc is None else acc + p
        xo_ref[...] = xf + 0.5 * acc

    tok = pl.BlockSpec((TM, D), lambda i: (i, 0))
    hid = pl.BlockSpec((NSH, TM, FS), lambda i: (0, i, 0))
    wsp = _resident((NSH, FS, D))
    dspec, dop = _dep(dep)
    return pl.pallas_call(
        body, out_shape=(_sds((S, D), f32), _sds((S, D), bf16), _sds((NSH, S, FS), bf16), _sds((NSH, S, FS), bf16)),
        grid=(S // TM,), in_specs=[tok, pl.BlockSpec((1, D), lambda i: (0, 0)), wsp, wsp, wsp] + dspec,
        out_specs=(tok, tok, hid, hid), name="ffn_fwd", compiler_params=_cp("parallel"))(x, g, wg, wu, wd, *dop)


def _in_proj(x, g, wi):
    def body(x_ref, g_ref, w_ref, o_ref, h_ref):
        xf = x_ref[...]
        r = lax.rsqrt(jnp.mean(xf * xf, axis=-1, keepdims=True) + NORM_EPS)
        hh = ((xf * r) * g_ref[...]).astype(bf16)
        h_ref[...] = hh
        for s in range(NSH):
            o_ref[:, PS * s:PS * (s + 1)] = _dot(hh, w_ref[s])

    tok = pl.BlockSpec((TM, D), lambda i: (i, 0))
    return pl.pallas_call(
        body, out_shape=(_sds((S, PROJ), f32), _sds((S, D), bf16)), grid=(S // TM,),
        in_specs=[tok, pl.BlockSpec((1, D), lambda i: (0, 0)), _resident((NSH, D, PS))],
        out_specs=(pl.BlockSpec((TM, PROJ), lambda i: (i, 0)), tok), name="in_proj", compiler_params=_cp("parallel"))(x, g, wi)


def _out_proj(x, mixed, wo):
    def body(x_ref, m_ref, w_ref, o_ref):
        o_ref[...] = x_ref[...] + _dot(m_ref[...], w_ref[...].reshape(D, D))

    return pl.pallas_call(
        body, out_shape=_sds((S, D), f32), grid=(S // TM,),
        in_specs=[pl.BlockSpec((TM, D), lambda i: (i, 0)), pl.BlockSpec((TM, D), lambda i: (i, 0)),
                  pl.BlockSpec((NSH, PW, D), lambda i: (0, 0, 0))],
        out_specs=pl.BlockSpec((TM, D), lambda i: (i, 0)), name="out_proj", compiler_params=_cp("parallel"))(x, mixed, wo)


def _out_proj_bwd(dx, wo):
    def body(dx_ref, w_ref, o_ref):
        o_ref[...] = _dot_nt(dx_ref[...].astype(bf16), w_ref[...].reshape(D, D))

    return pl.pallas_call(
        body, out_shape=_sds((S, D), f32), grid=(S // TM,),
        in_specs=[pl.BlockSpec((TM, D), lambda i: (i, 0)), pl.BlockSpec((NSH, PW, D), lambda i: (0, 0, 0))],
        out_specs=pl.BlockSpec((TM, D), lambda i: (i, 0)), name="out_proj_bwd", compiler_params=_cp("parallel"))(dx, wo)


def _ffn_bwd_mid(dx, h, a, b, wd, dep=None):
    nt = S // TM

    def body(dx_ref, h_ref, a_ref, b_ref, wd_ref, *rest):
        da_ref, db_ref, dwd_ref, dwg_ref, dwu_ref, dy_s, u_s, da_s, db_s = rest[-9:]
        i = pl.program_id(1)
        rows = pl.ds(pl.multiple_of(i * TM, TM), TM)
        dy = (0.5 * dx_ref[...]).astype(bf16)
        dy_s[rows, :] = dy
        du = _dot_nt(dy, wd_ref[0])
        a = a_ref[0].astype(f32)
        b = b_ref[0].astype(f32)
        sig = 1.0 / (1.0 + jnp.exp(-a))
        silu = a * sig
        da = (du * b * (sig * (1.0 + a * (1.0 - sig)))).astype(bf16)
        db = (du * silu).astype(bf16)
        da_ref[0] = da
        db_ref[0] = db
        da_s[rows, :] = da
        db_s[rows, :] = db
        u_s[rows, :] = (silu * b).astype(bf16)

        @pl.when(i == nt - 1)
        def _():
            hh = h_ref[...]
            dwd_ref[...] = _dot_tn(u_s[...], dy_s[...]).astype(bf16).reshape(dwd_ref.shape)
            dwg_ref[...] = _dot_tn(da_s[...], hh).astype(bf16).reshape(dwg_ref.shape)
            dwu_ref[...] = _dot_tn(db_s[...], hh).astype(bf16).reshape(dwu_ref.shape)

    tok = pl.BlockSpec((TM, D), lambda s, i: (i, 0))
    hid = pl.BlockSpec((1, TM, FS), lambda s, i: (s, i, 0))
    wsp = pl.BlockSpec((1, 2, FS // 2, D), lambda s, i: (s, 0, 0, 0))
    hidden = _sds((NSH, S, FS), bf16)
    wgrad = _sds((NSH, 2, FS // 2, D), bf16)
    whole = pltpu.VMEM((S, FS), bf16)
    dspec, dop = _dep(dep)
    return pl.pallas_call(
        body, out_shape=(hidden, hidden, wgrad, wgrad, wgrad), grid=(NSH, nt),
        in_specs=[tok, pl.BlockSpec((S, D), lambda s, i: (0, 0), pipeline_mode=pl.Buffered(1)), hid, hid,
                  pl.BlockSpec((1, FS, D), lambda s, i: (s, 0, 0))] + dspec,
        out_specs=(hid, hid, wsp, wsp, wsp), scratch_shapes=[pltpu.VMEM((S, D), bf16), whole, whole, whole],
        name="ffn_bwd_mid", compiler_params=_cp("parallel", "arbitrary"))(dx, h, a, b, wd, *dop)


def _norm_bwd_tail(acc, x_ref, dxin_ref, g_ref, dxo_ref, dg_ref, first):
    xf = x_ref[...]
    r = lax.rsqrt(jnp.mean(xf * xf, axis=-1, keepdims=True) + NORM_EPS)
    xhat = xf * r
    dhg = acc * g_ref[...]
    dxo_ref[...] = dxin_ref[...] + r * (dhg - xhat * jnp.mean(dhg * xhat, axis=-1, keepdims=True))
    part = jnp.sum(acc * xhat, axis=0, keepdims=True)

    @pl.when(first)
    def _():
        dg_ref[...] = part

    @pl.when(jnp.logical_not(first))
    def _():
        dg_ref[...] += part


def _ffn_bwd_dx(dx, x_in, g, da, db, wg, wu):
    def body(dx_ref, x_ref, g_ref, da_ref, db_ref, wg_ref, wu_ref, dxo_ref, dg_ref):
        acc = None
        for s in range(NSH):
            p = _dot(da_ref[s], wg_ref[s])
            acc = p if acc is None else acc + p
            acc = acc + _dot(db_ref[s], wu_ref[s])
        _norm_bwd_tail(acc, x_ref, dx_ref, g_ref, dxo_ref, dg_ref, pl.program_id(0) == 0)

    tok = pl.BlockSpec((TM, D), lambda i: (i, 0))
    vec = pl.BlockSpec((1, D), lambda i: (0, 0))
    hid = pl.BlockSpec((NSH, TM, FS), lambda i: (0, i, 0))
    wsp = _resident((NSH, FS, D))
    return pl.pallas_call(
        body, out_shape=(_sds((S, D), f32), _sds((1, D), f32)), grid=(S // TM,),
        in_specs=[tok, tok, vec, hid, hid, wsp, wsp], out_specs=(tok, vec),
        name="ffn_bwd_dx", compiler_params=_cp("arbitrary"))(dx, x_in, g, da, db, wg, wu)


def _in_proj_bwd_dx(dx, x_in, g, dproj, wi):
    def body(dx_ref, x_ref, g_ref, dp_ref, w_ref, dxo_ref, dg_ref):
        acc = None
        for s in range(NSH):
            p = _dot_nt(dp_ref[:, PS * s:PS * (s + 1)], w_ref[s])
            acc = p if acc is None else acc + p
        _norm_bwd_tail(acc, x_ref, dx_ref, g_ref, dxo_ref, dg_ref, pl.program_id(0) == 0)

    tok = pl.BlockSpec((TM, D), lambda i: (i, 0))
    vec = pl.BlockSpec((1, D), lambda i: (0, 0))
    return pl.pallas_call(
        body, out_shape=(_sds((S, D), f32), _sds((1, D), f32)), grid=(S // TM,),
        in_specs=[tok, tok, vec, pl.BlockSpec((TM, PROJ), lambda i: (i, 0)), _resident((NSH, D, PS))], out_specs=(tok, vec),
        name="in_proj_bwd_dx", compiler_params=_cp("arbitrary"))(dx, x_in, g, dproj, wi)


def _dw(lhs, rhs, lhs_spec, rhs_spec, rows, cols, name, cast_rhs=False):
    def body(l_ref, r_ref, o_ref):
        r = r_ref[...].astype(bf16) if cast_rhs else r_ref[...]
        o_ref[...] = _dot_tn(l_ref[...], r).astype(bf16).reshape(1, 2, rows // 2, cols)

    return pl.pallas_call(
        body, out_shape=_sds((NSH, 2, rows // 2, cols), bf16), grid=(NSH,), in_specs=[lhs_spec, rhs_spec],
        out_specs=pl.BlockSpec((1, 2, rows // 2, cols), lambda s: (s, 0, 0, 0)), name=name, compiler_params=_cp("parallel"))(lhs, rhs)


_WHOLE_TOK = pl.BlockSpec((S, D), lambda s: (0, 0))


def _dw_in(h, dproj):
    return _dw(h, dproj, _WHOLE_TOK, pl.BlockSpec((S, PS), lambda s: (0, s)), D, PS, "dw_in")


def _dw_out(mixed, dx):
    return _dw(mixed, dx, pl.BlockSpec((S, PW), lambda s: (0, s)), _WHOLE_TOK, PW, D, "dw_out", cast_rhs=True)


def _final_loss(x, g, target):
    def body(x_ref, g_ref, t_ref, loss_ref, dx_ref, dg_ref):
        i = pl.program_id(0)
        xf = x_ref[...]
        r = lax.rsqrt(jnp.mean(xf * xf, axis=-1, keepdims=True) + NORM_EPS)
        xhat = xf * r
        err = xhat * g_ref[...] - t_ref[...]
        dy = err * (1.0 / D)
        dhg = dy * g_ref[...]
        dx_ref[...] = r * (dhg - xhat * jnp.mean(dhg * xhat, axis=-1, keepdims=True))
        part = jnp.sum(dy * xhat, axis=0, keepdims=True)
        lpart = jnp.zeros((8, 128), f32) + 0.5 * jnp.sum(jnp.mean(err * err, axis=-1, keepdims=True))

        @pl.when(i == 0)
        def _():
            dg_ref[...] = part
            loss_ref[...] = lpart

        @pl.when(i != 0)
        def _():
            dg_ref[...] += part
            loss_ref[...] += lpart

    tok = pl.BlockSpec((TM, D), lambda i: (i, 0))
    vec = pl.BlockSpec((1, D), lambda i: (0, 0))
    return pl.pallas_call(
        body, out_shape=(_sds((8, 128), f32), _sds((S, D), f32), _sds((1, D), f32)), grid=(S // TM,),
        in_specs=[tok, vec, tok], out_specs=(pl.BlockSpec((8, 128), lambda i: (0, 0)), tok, vec),
        name="final_loss", compiler_params=_cp("arbitrary"))(x, g, target)


def _shift_down(x, k, row):
    return jnp.where(row >= k, pltpu.roll(x, k, axis=0), 0.0)


def _shift_up(x, k, row):
    return jnp.where(row < S - k, pltpu.roll(x, S - k, axis=0), 0.0)


def _pool_geometry():
    row = lax.broadcasted_iota(jnp.int32, (S, PW), 0)
    grp = lax.broadcasted_iota(jnp.int32, (S, PW), 1) // 64
    half = jnp.where(grp == 0, 1, jnp.where(grp == 1, 2, jnp.where(grp == 2, 4, 8)))
    hi = jnp.minimum(row + half - 1, S - 1)
    lo = jnp.maximum(row - half, 0)
    return row, grp, (hi - lo + 1).astype(f32)


def _by_group(grp, v0, v1, v2, v3):
    return jnp.where(grp == 0, v0, jnp.where(grp == 1, v1, jnp.where(grp == 2, v2, v3)))


def _window_sums(x, row, grp, transpose):
    l1, r1 = x, x
    l2, r2 = l1 + _shift_down(l1, 1, row), r1 + _shift_up(r1, 1, row)
    l4, r4 = l2 + _shift_down(l2, 2, row), r2 + _shift_up(r2, 2, row)
    l8, r8 = l4 + _shift_down(l4, 4, row), r4 + _shift_up(r4, 4, row)
    lsel = _by_group(grp, l1, l2, l4, l8)
    rsel = _by_group(grp, r1, r2, r4, r8)
    if transpose:
        return lsel + _shift_up(rsel, 1, row)
    return _shift_down(lsel, 1, row) + rsel


def _pool_fwd(proj, wbd, scale):
    def body(v_ref, w_ref, sc_ref, mixed_ref, diff_ref):
        row, grp, cnt = _pool_geometry()
        v = v_ref[...]
        diff = (_window_sums(v, row, grp, False) / cnt - v).astype(bf16)
        diff_ref[...] = diff
        mixed_ref[...] = (_dot(diff, w_ref[...].astype(bf16)) * sc_ref[...]).astype(bf16)

    col = pl.BlockSpec((S, PW), lambda i: (0, 0))
    return pl.pallas_call(
        body, out_shape=(_sds((S, D), bf16), _sds((S, PW), bf16)), grid=(1,),
        in_specs=[col, pl.BlockSpec((PW, PW), lambda i: (0, 0)), pl.BlockSpec((1, PW), lambda i: (0, 0))],
        out_specs=(col, col), name="pool_fwd", compiler_params=_cp("arbitrary"))(proj, wbd, scale)


def _pool_bwd(dmixed, diff, wbd, scale, dproj):
    def body(dy_ref, diff_ref, w_ref, sc_ref, dproj_in, dv_ref, dw_ref, dsc_ref):
        del dproj_in
        row, grp, cnt = _pool_geometry()
        dy = dy_ref[...]
        diff = diff_ref[...]
        w = w_ref[...].astype(bf16)
        dsc_ref[...] = jnp.sum(dy * _dot(diff, w), axis=0, keepdims=True)
        dys = (dy * sc_ref[...]).astype(bf16)
        dw_ref[...] = _dot_tn(diff, dys)
        ddiff = _dot_nt(dys, w)
        dv_ref[...] = (_window_sums(ddiff / cnt, row, grp, True) - ddiff).astype(bf16)

    col = pl.BlockSpec((S, PW), lambda i: (0, 0))
    return pl.pallas_call(
        body, out_shape=(_sds((S, PROJ), bf16), _sds((PW, PW), f32), _sds((1, PW), f32)), grid=(1,),
        in_specs=[col, col, pl.BlockSpec((PW, PW), lambda i: (0, 0)), pl.BlockSpec((1, PW), lambda i: (0, 0)), ANY],
        out_specs=(col, pl.BlockSpec((PW, PW), lambda i: (0, 0)), pl.BlockSpec((1, PW), lambda i: (0, 0))),
        input_output_aliases={4: 0}, name="pool_bwd", compiler_params=_cp("arbitrary"))(dmixed, diff, wbd, scale, dproj)


def _rope_tables(pos_col, freq_row):
    def body(p_ref, f_ref, c_ref, a_ref, b_ref):
        ang = p_ref[...].astype(f32) * f_ref[...]
        l64 = lax.broadcasted_iota(jnp.int32, (S, 128), 1) % 64
        cos, sin = jnp.cos(ang), jnp.sin(ang)
        c_ref[...] = jnp.where(l64 < 16, cos, 1.0)
        a_ref[...] = jnp.where(l64 < 8, -sin, 0.0)
        b_ref[...] = jnp.where((l64 >= 8) & (l64 < 16), sin, 0.0)

    t = _sds((S, 128), f32)
    return pl.pallas_call(body, out_shape=(t, t, t), name="rope_tables", compiler_params=_cp())(pos_col, freq_row)


def _rope(t, c, a, b):
    return t * c + pltpu.roll(t, 120, axis=1) * a + pltpu.roll(t, 8, axis=1) * b


def _rope_bwd(g, c, a, b):
    return g * c + pltpu.roll(g * a, 8, axis=1) + pltpu.roll(g * b, 120, axis=1)


def _perm_load(ref, d):
    if d == 1:
        return ref[...]
    n = S // d
    return jnp.concatenate([ref[pl.ds(r, n, stride=d), :] for r in range(d)], axis=0)


def _unperm_store(ref, val, d):
    if d == 1:
        ref[...] = val
        return
    n = S // d
    for r in range(d):
        ref[pl.ds(r, n, stride=d), :] = val[r * n:(r + 1) * n, :]


def _band(xp, d):
    if d == NBLK:
        return xp.reshape(NBLK, QBLK, 128)
    z = jnp.zeros((64, 128), bf16)
    p = jnp.concatenate([z, xp, z], axis=0).reshape(NBLK + 1, QBLK, 128)
    return jnp.concatenate([p[:NBLK], p[1:]], axis=1)


def _unband(xb, d):
    if d == NBLK:
        return xb.reshape(S, 128)
    z = jnp.zeros((1, QBLK, 128), f32)
    p = jnp.concatenate([xb[:, :QBLK], z], axis=0) + jnp.concatenate([z, xb[:, QBLK:]], axis=0)
    return p.reshape(S + QBLK, 128)[64:S + 64]


def _band_mask(d):
    if d == NBLK:
        a = lax.broadcasted_iota(jnp.int32, (1, 2 * QBLK, QBLK), 1) & (QBLK - 1)
        b = lax.broadcasted_iota(jnp.int32, (1, 2 * QBLK, QBLK), 2)
        return (b >= a - 64) & (b <= a + 64)
    blocks_per_class = NBLK // d
    n = lax.broadcasted_iota(jnp.int32, (NBLK, 1, 2 * QBLK), 0) & (blocks_per_class - 1)
    be = lax.broadcasted_iota(jnp.int32, (NBLK, 1, 2 * QBLK), 2)
    a = lax.broadcasted_iota(jnp.int32, (1, 2 * QBLK, 2 * QBLK), 1) & (QBLK - 1)
    b = lax.broadcasted_iota(jnp.int32, (1, 2 * QBLK, 2 * QBLK), 2)
    band = (b >= a) & (b <= a + 128)
    edge = ((be >= 64) | (n != 0)) & ((be < QBLK + 64) | (n != blocks_per_class - 1))
    return band & edge


def _stack_heads(xb, lo):
    z = jnp.zeros_like(xb)
    return jnp.concatenate([jnp.where(lo, xb, z), jnp.where(lo, z, xb)], axis=1)


def _unstack_heads(x2, lo):
    return jnp.where(lo, x2[:, :QBLK], x2[:, QBLK:])


def _rows_to_lanes(col2, lo):
    return jnp.where(lo, jnp.broadcast_to(col2[:, :QBLK], (NBLK, QBLK, 128)), jnp.broadcast_to(col2[:, QBLK:], (NBLK, QBLK, 128)))


def _bmm_nt(a, b):
    return jnp.einsum('nqd,nkd->nqk', a, b, preferred_element_type=f32)


def _bmm_nn(a, b):
    return jnp.einsum('nqk,nkd->nqd', a, b, preferred_element_type=f32)


def _bmm_tn(a, b):
    return jnp.einsum('nqk,nqd->nkd', a, b, preferred_element_type=f32)


def _attn_fwd(proj, tc, ta, tb, mixed):
    def body(q_ref, k_ref, v_ref, c_ref, a_ref, b_ref, mixed_in, mixed_ref, o_ref, lse_ref, qn, kn, t_num, t_m, t_den):
        del mixed_in
        lo = lax.broadcasted_iota(jnp.int32, (1, 1, 128), 2) < 64
        c, a, b = c_ref[...], a_ref[...], b_ref[...]
        qn[...] = _rope(q_ref[...], c, a, b)
        kn[...] = _rope(k_ref[...], c, a, b)
        run = None
        for d in DILATIONS:
            q2 = _stack_heads(_perm_load(qn, d).astype(bf16).reshape(NBLK, QBLK, 128), lo)
            kb = _band(_perm_load(kn, d).astype(bf16), d)
            vb = _band(_perm_load(v_ref, d).astype(bf16), d)
            s = jnp.where(_band_mask(d), _bmm_nt(q2, kb) * 0.125, MASK_VALUE)
            m = jnp.max(s, axis=2, keepdims=True)
            p = jnp.exp(s - m)
            den = jnp.sum(p, axis=2, keepdims=True)
            num = _unstack_heads(_bmm_nn(p.astype(bf16), vb), lo)
            _unperm_store(t_num, num.reshape(S, 128), d)
            _unperm_store(t_m, _rows_to_lanes(m, lo).reshape(S, 128), d)
            _unperm_store(t_den, _rows_to_lanes(den, lo).reshape(S, 128), d)
            if run is None:
                run = (t_m[...], t_num[...], t_den[...])
            else:
                m_new = jnp.maximum(run[0], t_m[...])
                w_old, w_new = jnp.exp(run[0] - m_new), jnp.exp(t_m[...] - m_new)
                run = (m_new, w_old * run[1] + w_new * t_num[...], w_old * run[2] + w_new * t_den[...])
        out = run[1] / run[2]
        o_ref[...] = out
        mixed_ref[...] = out.astype(bf16)
        lse_ref[...] = run[0] + jnp.log(run[2])

    def col(off):
        return pl.BlockSpec((S, 128), lambda j, off=off: (0, off + j))

    tab = pl.BlockSpec((S, 128), lambda j: (0, 0))
    scr = pltpu.VMEM((S, 128), f32)
    return pl.pallas_call(
        body, out_shape=(_sds((S, D), bf16), _sds((S, AW), f32), _sds((S, AW), f32)), grid=(NPAIR,),
        in_specs=[col(2), col(8), col(14), tab, tab, tab, ANY], out_specs=(col(2), col(0), col(0)),
        scratch_shapes=[scr, scr, scr, scr, scr], input_output_aliases={6: 0}, name="attn_fwd",
        compiler_params=_cp("arbitrary"))(proj, proj, proj, tc, ta, tb, mixed)


def _attn_bwd(proj, tc, ta, tb, o, lse, dmixed):
    def body(q_ref, k_ref, v_ref, c_ref, a_ref, b_ref, o_ref, lse_ref, do_ref, dp_ref, qn, kn, tmp, dk_s, dv_s):
        t = pl.program_id(1)

        @pl.when(t == 0)
        def _():
            lo = lax.broadcasted_iota(jnp.int32, (1, 1, 128), 2) < 64
            c, a, b = c_ref[...], a_ref[...], b_ref[...]
            qn[...] = _rope(q_ref[...], c, a, b)
            kn[...] = _rope(k_ref[...], c, a, b)
            dq = dk = dv = None
            for d in DILATIONS:
                q2 = _stack_heads(_perm_load(qn, d).astype(bf16).reshape(NBLK, QBLK, 128), lo)
                kb = _band(_perm_load(kn, d).astype(bf16), d)
                vb = _band(_perm_load(v_ref, d).astype(bf16), d)
                dob = _perm_load(do_ref, d).reshape(NBLK, QBLK, 128)
                ob = _perm_load(o_ref, d).reshape(NBLK, QBLK, 128)
                lsb = _perm_load(lse_ref, d).reshape(NBLK, QBLK, 128)
                do2 = _stack_heads(dob.astype(bf16), lo)
                delta2 = jnp.sum(_stack_heads(dob * ob, lo), axis=2, keepdims=True)
                lse2 = jnp.max(jnp.concatenate([jnp.where(lo, lsb, MASK_VALUE), jnp.where(lo, MASK_VALUE, lsb)], axis=1),
                               axis=2, keepdims=True)
                s = _bmm_nt(q2, kb) * 0.125
                p = jnp.where(_band_mask(d), jnp.exp(s - lse2), 0.0)
                ds = (p * (_bmm_nt(do2, vb) - delta2) * 0.125).astype(bf16)
                pb = p.astype(bf16)
                dq_b = _unstack_heads(_bmm_nn(ds, kb), lo).reshape(S, 128)
                dk_b = _unband(_bmm_tn(ds, q2), d)
                dv_b = _unband(_bmm_tn(pb, do2), d)
                acc = []
                for prev, new in ((dq, dq_b), (dk, dk_b), (dv, dv_b)):
                    _unperm_store(tmp, new, d)
                    acc.append(tmp[...] if prev is None else prev + tmp[...])
                dq, dk, dv = acc
            dp_ref[...] = _rope_bwd(dq, c, a, b).astype(bf16)
            dk_s[...] = _rope_bwd(dk, c, a, b).astype(bf16)
            dv_s[...] = dv.astype(bf16)

        @pl.when(t == 1)
        def _():
            dp_ref[...] = dk_s[...]

        @pl.when(t == 2)
        def _():
            dp_ref[...] = dv_s[...]

    def col(off):
        return pl.BlockSpec((S, 128), lambda j, t, off=off: (0, off + j))

    tab = pl.BlockSpec((S, 128), lambda j, t: (0, 0))
    scr = pltpu.VMEM((S, 128), f32)
    scb = pltpu.VMEM((S, 128), bf16)
    return pl.pallas_call(
        body, out_shape=_sds((S, PROJ), bf16), grid=(NPAIR, 3),
        in_specs=[col(2), col(8), col(14), tab, tab, tab, col(0), col(0), col(2)],
        out_specs=pl.BlockSpec((S, 128), lambda j, t: (0, 2 + NPAIR * t + j)),
        scratch_shapes=[scr, scr, scr, scb, scb], name="attn_bwd",
        compiler_params=_cp("arbitrary", "arbitrary"))(proj, proj, proj, tc, ta, tb, o, lse, dmixed)


def _block_diag(w4):
    out = jnp.zeros((PW, PW), w4.dtype)
    for g in range(4):
        out = out.at[64 * g:64 * (g + 1), 64 * g:64 * (g + 1)].set(w4[g])
    return out


def _diag_blocks(w):
    return jnp.stack([w[64 * g:64 * (g + 1), 64 * g:64 * (g + 1)] for g in range(4)])


def _rope_inputs(positions):
    inv_freq = ROPE_THETA ** (-jnp.arange(0, 16, 2, dtype=f32) / 16)
    l64 = np.arange(128) % 64
    idx = np.where(l64 < 16, l64 % 8, 0)
    return positions.reshape(S, 1), inv_freq[idx].reshape(1, 128)


def _layer_fwd(x, w, small, l, tabs, dep=None, rest=None):
    g1, gm, g2 = (small[k][l].reshape(1, D) for k in ("ffn1_norm", "mix_norm", "ffn2_norm"))
    wbd = _block_diag(small["pool_w"][l])
    psc = small["pool_scale"][l].reshape(1, PW)
    x1, h1, a1, b1 = _ffn_fwd(x, g1, w["g1"], w["u1"], w["d1"], dep)
    if rest is not None:
        w = {**w, **rest(x1)}
    proj, h2 = _in_proj(x1, gm, w["wi"])
    mixed, diff = _pool_fwd(proj, wbd, psc)
    mixed, o, lse = _attn_fwd(proj, *tabs, mixed)
    x2 = _out_proj(x1, mixed, w["wo"])
    out, h3, a2, b2 = _ffn_fwd(x2, g2, w["g2"], w["u2"], w["d2"])
    return out, dict(x0=x, h1=h1, a1=a1, b1=b1, x1=x1, h2=h2, proj=proj, mixed=mixed, diff=diff, o=o, lse=lse,
                     x2=x2, h3=h3, a2=a2, b2=b2, g1=g1, gm=gm, g2=g2, wbd=wbd, psc=psc), w


def _layer_bwd(dx, w, sv, tabs, dep=None):
    gr, sg = {}, {}
    da, db, gr["d2"], gr["g2"], gr["u2"] = _ffn_bwd_mid(dx, sv["h3"], sv["a2"], sv["b2"], w["d2"], dep)
    dx, sg["ffn2_norm"] = _ffn_bwd_dx(dx, sv["x2"], sv["g2"], da, db, w["g2"], w["u2"])
    gr["wo"] = _dw_out(sv["mixed"], dx)
    dmixed = _out_proj_bwd(dx, w["wo"])
    dproj = _attn_bwd(sv["proj"], *tabs, sv["o"], sv["lse"], dmixed)
    dproj, dwbd, sg["pool_scale"] = _pool_bwd(dmixed, sv["diff"], sv["wbd"], sv["psc"], dproj)
    sg["pool_w"] = _diag_blocks(dwbd)
    gr["wi"] = _dw_in(sv["h2"], dproj)
    dx, sg["mix_norm"] = _in_proj_bwd_dx(dx, sv["x1"], sv["gm"], dproj, w["wi"])
    da, db, gr["d1"], gr["g1"], gr["u1"] = _ffn_bwd_mid(dx, sv["h1"], sv["a1"], sv["b1"], w["d1"])
    dx, sg["ffn1_norm"] = _ffn_bwd_dx(dx, sv["x0"], sv["g1"], da, db, w["g1"], w["u1"])
    return dx, gr, sg


def _forward_backward(x, positions, target, gathered, small):
    tabs = _rope_tables(*_rope_inputs(positions))
    saved = []
    for l in range(DEPTH):
        x, sv, _ = _layer_fwd(x, gathered[l], small, l, tabs)
        saved.append(sv)
    loss, dx, dgf = _final_loss(x, small["final_norm"].reshape(1, D), target)
    big = [None] * DEPTH
    sg = {k: [None] * DEPTH for k in ("ffn1_norm", "mix_norm", "pool_w", "pool_scale", "ffn2_norm")}
    for l in reversed(range(DEPTH)):
        dx, big[l], sgl = _layer_bwd(dx, gathered[l], saved[l], tabs)
        for k, v in sgl.items():
            sg[k][l] = v
    sg["final_norm"] = dgf
    return loss, dx, big, sg


def _place():
    x, y, c = lax.axis_index("x"), lax.axis_index("y"), lax.axis_index("c")
    chips = [(1 - x, y), (x, 1 - y), (1 - x, 1 - y)]
    return x, y, c, chips


def _cast_layer(params, l, place, dep=None):
    def body(p_ref, *refs):
        del p_ref
        for i_ref, o_ref in zip(refs[:8], refs[-8:]):
            o_ref[...] = i_ref[...].astype(bf16).reshape(o_ref.shape)

    ins, in_specs, out_shape, out_specs = [], [], [], []
    for name, rows, cols in BIG:
        q = rows // 4
        ins.append(params[BIG_SRC[name]])
        in_specs.append(pl.BlockSpec((1, q, cols), lambda i, p, l=l: (l, i, 0)))
        out_shape.append(_sds((NSH, 2, rows // 2, cols), bf16))
        out_specs.append(pl.BlockSpec((1, 1, q, cols), lambda i, p: (p[1], i // 2, i % 2, 0)))
    dspec, dop = _dep(dep)
    return pl.pallas_call(
        body, out_shape=out_shape,
        grid_spec=pltpu.PrefetchScalarGridSpec(num_scalar_prefetch=1, grid=(4,), in_specs=in_specs + dspec, out_specs=out_specs),
        name=f"cast_layer{l}", compiler_params=_cp("parallel"))(place, *ins, *dop)


HBM = pl.BlockSpec(memory_space=pltpu.HBM)
SEM = pl.BlockSpec(memory_space=pltpu.SEMAPHORE)
_SPLIT = pltpu.CompilerParams(has_side_effects=pltpu.SideEffectType.DATAFLOW_SIDE_EFFECTING)


def _hbm(arrays):
    return [pltpu.with_memory_space_constraint(a, pltpu.HBM) for a in arrays]


def _chip_copies(src_of, dst_of, send_sems, recv_sems, n):
    x, y, c, chips = _place()
    me = 2 * x + y
    out = []
    for t in range(n):
        for k, chip in enumerate(chips):
            peer = 2 * chip[0] + chip[1]
            send = pltpu.make_async_remote_copy(
                src_ref=src_of(t, peer), dst_ref=dst_of(t, me), send_sem=send_sems.at[3 * t + k], recv_sem=recv_sems.at[3 * t + k],
                device_id=(chip[0], chip[1], c), device_id_type=MESH)
            land = pltpu.make_async_remote_copy(
                src_ref=src_of(t, peer), dst_ref=dst_of(t, peer), send_sem=send_sems.at[3 * t + k], recv_sem=recv_sems.at[3 * t + k],
                device_id=(chip[0], chip[1], c), device_id_type=MESH)
            out.append((send, land))
    return out


def _exchange_start(src, land, after, src_of, dst_of, name):
    n, m = len(src), len(src) + len(land)

    def body(*refs):
        src_refs = refs[:n]
        land_refs = refs[n:m] if land else src_refs
        send_sems, recv_sems = refs[m + 1], refs[m + 2]
        token = refs[-1]
        for send, _ in _chip_copies(lambda t, s: src_of(src_refs[t], s), lambda t, s: dst_of(land_refs[t], s), send_sems, recv_sems, n):
            send.start()
        token[...] = jnp.zeros_like(token)

    arrays = list(src) + list(land)
    out_shape = ([pltpu.SemaphoreType.DMA((3 * n,)), pltpu.SemaphoreType.DMA((3 * n,))] + [pltpu.HBM(a.shape, a.dtype) for a in arrays]
                 + [_sds((8, 128), f32)])
    res = pl.pallas_call(
        body, out_shape=out_shape, in_specs=[HBM] * m + [ANY], out_specs=[SEM, SEM] + [HBM] * m + [pl.BlockSpec(memory_space=pltpu.VMEM)],
        input_output_aliases={i: 2 + i for i in range(m)}, name=name, compiler_params=_SPLIT)(*_hbm(arrays), after)
    return res[0], res[1], list(res[2:2 + n]), list(res[2 + n:2 + m]), res[-1]


def _exchange_wait(send_sems, recv_sems, src, land, after, src_of, dst_of, name):
    n, m = len(src), len(src) + len(land)

    def body(*refs):
        src_refs = refs[:n]
        land_refs = refs[n:m] if land else src_refs
        send_sems, recv_sems = refs[m], refs[m + 1]
        for send, land_cp in _chip_copies(lambda t, s: src_of(src_refs[t], s), lambda t, s: dst_of(land_refs[t], s), send_sems, recv_sems, n):
            send.wait_send()
            land_cp.wait_recv()

    arrays = list(src) + list(land)
    res = pl.pallas_call(
        body, out_shape=[pltpu.HBM(a.shape, a.dtype) for a in arrays], in_specs=[HBM] * m + [SEM, SEM, ANY], out_specs=[HBM] * m,
        input_output_aliases={i: i for i in range(m)}, name=name, compiler_params=_SPLIT)(*arrays, send_sems, recv_sems, after)
    return list(res[:n]), list(res[n:])


def _own_half(ref, s):
    x, y, c, _ = _place()
    return ref.at[2 * x + y, c]


def _slot_half(ref, s):
    return ref.at[s, lax.axis_index("c")]


def _slot(ref, s):
    return ref.at[s]


def _gather_forward(bufs):
    n = len(bufs)

    def body(*refs):
        outs = refs[n:2 * n]
        send_sems, recv_sems = refs[2 * n:]
        x, y, c, chips = _place()
        sibling = (x, y, 1 - c)
        passed = []
        for t in range(n):
            for k, chip in enumerate(chips):
                blk = outs[t].at[2 * chip[0] + chip[1], c]
                cp = pltpu.make_async_remote_copy(
                    src_ref=blk, dst_ref=blk, send_sem=send_sems.at[t, k], recv_sem=recv_sems.at[t, k],
                    device_id=sibling, device_id_type=MESH)
                cp.start()
                passed.append(cp)
        for t in range(n):
            for k, chip in enumerate(chips):
                blk = outs[t].at[2 * chip[0] + chip[1], 1 - c]
                pltpu.make_async_remote_copy(
                    src_ref=blk, dst_ref=blk, send_sem=send_sems.at[t, k], recv_sem=recv_sems.at[t, k],
                    device_id=sibling, device_id_type=MESH).wait_recv()
        for cp in passed:
            cp.wait_send()

    out_shape = [_sds(a.shape, bf16) for a in bufs]
    return pl.pallas_call(
        body, out_shape=out_shape, in_specs=[ANY] * n, out_specs=[ANY] * n, input_output_aliases={t: t for t in range(n)},
        scratch_shapes=[pltpu.SemaphoreType.DMA((n, 3)), pltpu.SemaphoreType.DMA((n, 3))], name="gather_forward")(*bufs)


def _sibling_swap(grads):
    n = len(grads)

    def body(*refs):
        ins, outs = refs[:n], refs[n:2 * n]
        send_sems, recv_sems = refs[2 * n:]
        x, y, c, _ = _place()
        cps = []
        for t in range(n):
            for s in range(NSH):
                cp = pltpu.make_async_remote_copy(
                    src_ref=ins[t].at[s, 1 - c], dst_ref=outs[t].at[s], send_sem=send_sems.at[t, s], recv_sem=recv_sems.at[t, s],
                    device_id=(x, y, 1 - c), device_id_type=MESH)
                cp.start()
                cps.append(cp)
        for cp in cps:
            cp.wait()

    out_shape = [_sds((NSH,) + a.shape[2:], bf16) for a in grads]
    return pl.pallas_call(
        body, out_shape=out_shape, in_specs=[ANY] * n, out_specs=[ANY] * n,
        scratch_shapes=[pltpu.SemaphoreType.DMA((n, NSH)), pltpu.SemaphoreType.DMA((n, NSH))],
        name="sibling_swap")(*grads)


def _row_tile(h):
    return h // 2 if h % 32 == 0 else h


def _pair_sum(grads, got, c_idx):
    n = len(grads)

    def body(c_ref, *refs):
        del c_ref
        for t in range(n):
            refs[2 * n + t][...] = (refs[t][...].astype(f32).reshape(refs[n + t].shape) + refs[n + t][...].astype(f32)).astype(bf16)

    in_specs, out_shape, out_specs = [], [], []
    for a in grads:
        h, cols = a.shape[2:]
        in_specs.append(pl.BlockSpec((1, 1, _row_tile(h), cols), lambda s, i, c: (s, c[0], i, 0)))
    for a in grads:
        h, cols = a.shape[2:]
        in_specs.append(pl.BlockSpec((1, _row_tile(h), cols), lambda s, i, c: (s, i, 0)))
        out_shape.append(_sds((NSH, h, cols), bf16))
        out_specs.append(pl.BlockSpec((1, _row_tile(h), cols), lambda s, i, c: (s, i, 0)))
    return pl.pallas_call(
        body, out_shape=out_shape,
        grid_spec=pltpu.PrefetchScalarGridSpec(num_scalar_prefetch=1, grid=(NSH, 2), in_specs=in_specs, out_specs=out_specs),
        name="pair_sum", compiler_params=_cp("parallel", "parallel"))(c_idx, *grads, *got)


def _chip_sum(psum, parts, full, place, l, name):
    n = len(parts)

    def body(p_ref, *refs):
        s = pl.program_id(1)
        for t in range(n):
            val = jnp.where(s == p_ref[1], refs[t][0], refs[n + t][0]).astype(f32)
            out = refs[3 * n + t]

            @pl.when(s == 0)
            def _(out=out, val=val):
                out[0, 0] = val

            @pl.when(s != 0)
            def _(out=out, val=val):
                out[0, 0] += val

    own_specs, part_specs, out_shape, out_specs = [], [], [], []
    for a, fl in zip(parts, full):
        _, h, cols = a.shape
        r = _row_tile(h)
        own_specs.append(pl.BlockSpec((1, r, cols), lambda i, s, p: (p[1], i, 0)))
        part_specs.append(pl.BlockSpec((1, r, cols), lambda i, s, p: (jnp.where(s == p[1], (s + 1) % NSH, s), i, 0)))
        out_shape.append(_sds(fl.shape, f32))
        out_specs.append(pl.BlockSpec((1, 1, r, cols), lambda i, s, p, l=l: (l, p[0], i, 0)))
    return pl.pallas_call(
        body, out_shape=out_shape,
        grid_spec=pltpu.PrefetchScalarGridSpec(num_scalar_prefetch=1, grid=(2, NSH), in_specs=own_specs + part_specs + [ANY] * n,
                                               out_specs=out_specs),
        input_output_aliases={1 + 2 * n + t: t for t in range(n)}, name=name,
        compiler_params=_cp("parallel", "arbitrary"))(place, *psum, *parts, *full)


def _sibling_share(full, l, name):
    n = len(full)

    def body(*refs):
        outs = refs[n:2 * n]
        send_sems, recv_sems = refs[2 * n:]
        x, y, c, _ = _place()
        sibling = (x, y, 1 - c)
        cps = []
        for t in range(n):
            blk = outs[t].at[l, c]
            cp = pltpu.make_async_remote_copy(
                src_ref=blk, dst_ref=blk, send_sem=send_sems.at[t], recv_sem=recv_sems.at[t], device_id=sibling, device_id_type=MESH)
            cp.start()
            cps.append(cp)
        for t in range(n):
            blk = outs[t].at[l, 1 - c]
            pltpu.make_async_remote_copy(
                src_ref=blk, dst_ref=blk, send_sem=send_sems.at[t], recv_sem=recv_sems.at[t],
                device_id=sibling, device_id_type=MESH).wait_recv()
        for cp in cps:
            cp.wait_send()

    out_shape = [_sds(a.shape, f32) for a in full]
    return pl.pallas_call(
        body, out_shape=out_shape, in_specs=[ANY] * n, out_specs=[ANY] * n, input_output_aliases={t: t for t in range(n)},
        scratch_shapes=[pltpu.SemaphoreType.DMA((n,)), pltpu.SemaphoreType.DMA((n,))], name=name)(*full)


def _share_copies(refs, l, send_sems, recv_sems):
    x, y, c, _ = _place()
    out = []
    for t, ref in enumerate(refs):
        mine, theirs = ref.at[l, c], ref.at[l, 1 - c]
        send = pltpu.make_async_remote_copy(src_ref=mine, dst_ref=mine, send_sem=send_sems.at[t], recv_sem=recv_sems.at[t],
                                            device_id=(x, y, 1 - c), device_id_type=MESH)
        land = pltpu.make_async_remote_copy(src_ref=mine, dst_ref=theirs, send_sem=send_sems.at[t], recv_sem=recv_sems.at[t],
                                            device_id=(x, y, 1 - c), device_id_type=MESH)
        out.append((send, land))
    return out


def _share_start(full, l, after, name):
    n = len(full)

    def body(*refs):
        for send, _ in _share_copies(refs[:n], l, refs[n + 1], refs[n + 2]):
            send.start()
        refs[-1][...] = jnp.zeros_like(refs[-1])

    out_shape = ([pltpu.SemaphoreType.DMA((n,)), pltpu.SemaphoreType.DMA((n,))] + [pltpu.HBM(a.shape, a.dtype) for a in full]
                 + [_sds((8, 128), f32)])
    res = pl.pallas_call(
        body, out_shape=out_shape, in_specs=[HBM] * n + [ANY], out_specs=[SEM, SEM] + [HBM] * n + [pl.BlockSpec(memory_space=pltpu.VMEM)],
        input_output_aliases={i: 2 + i for i in range(n)}, name=name, compiler_params=_SPLIT)(*_hbm(full), after)
    return res[0], res[1], list(res[2:2 + n])


def _share_wait(send_sems, recv_sems, full, l, after, name):
    n = len(full)

    def body(*refs):
        for send, land in _share_copies(refs[:n], l, refs[n], refs[n + 1]):
            send.wait_send()
            land.wait_recv()

    res = pl.pallas_call(
        body, out_shape=[pltpu.HBM(a.shape, a.dtype) for a in full], in_specs=[HBM] * n + [SEM, SEM, ANY], out_specs=[HBM] * n,
        input_output_aliases={i: i for i in range(n)}, name=name, compiler_params=_SPLIT)(*full, send_sems, recv_sems, after)
    return list(res)


SMALL_ROWS = 656


def _pack_small(per_layer, final_vec, loss_tile):
    rows = []
    for l in range(DEPTH):
        for k in ("ffn1_norm", "mix_norm", "ffn2_norm"):
            rows.append(per_layer[k][l].reshape(8, 128))
        rows.append(per_layer["pool_w"][l].reshape(128, 128))
        rows.append(jnp.pad(per_layer["pool_scale"][l].reshape(2, 128), ((0, 6), (0, 0))))
    rows.append(final_vec.reshape(8, 128))
    rows.append(loss_tile)
    return jnp.concatenate(rows, axis=0)


def _unpack_small(buf):
    out = {k: [] for k in ("ffn1_norm", "mix_norm", "ffn2_norm", "pool_w", "pool_scale")}
    r = 0
    for l in range(DEPTH):
        for k in ("ffn1_norm", "mix_norm", "ffn2_norm"):
            out[k].append(buf[r:r + 8].reshape(D))
            r += 8
        out["pool_w"].append(buf[r:r + 128].reshape(4, 64, 64))
        r += 128
        out["pool_scale"].append(buf[r:r + 2].reshape(PW))
        r += 8
    res = {k: jnp.stack(v) for k, v in out.items()}
    res["final_norm"] = buf[r:r + 8].reshape(D)
    res["loss"] = buf[r + 8, 0]
    return res


def _allreduce_small(buf):
    def body(in_ref, out_ref, slots, send_sems, recv_sems):
        x, y, c, _ = _place()
        me = 4 * x + 2 * y + c
        slots[me] = in_ref[...]
        peers = []
        for k in range(1, 8):
            px, py, pc = x ^ (k >> 2), y ^ ((k >> 1) & 1), c ^ (k & 1)
            cp = pltpu.make_async_remote_copy(
                src_ref=in_ref, dst_ref=slots.at[me], send_sem=send_sems.at[k - 1], recv_sem=recv_sems.at[k - 1],
                device_id=(px, py, pc), device_id_type=MESH)
            cp.start()
            peers.append(cp)
        for k in range(1, 8):
            px, py, pc = x ^ (k >> 2), y ^ ((k >> 1) & 1), c ^ (k & 1)
            slot = 4 * px + 2 * py + pc
            pltpu.make_async_remote_copy(
                src_ref=slots.at[slot], dst_ref=slots.at[slot], send_sem=send_sems.at[k - 1], recv_sem=recv_sems.at[k - 1],
                device_id=(px, py, pc), device_id_type=MESH).wait_recv()
        for cp in peers:
            cp.wait_send()
        acc = slots[0]
        for j in range(1, 8):
            acc = acc + slots[j]
        out_ref[...] = acc

    return pl.pallas_call(
        body, out_shape=_sds((SMALL_ROWS, 128), f32),
        in_specs=[pl.BlockSpec(memory_space=pltpu.VMEM)], out_specs=pl.BlockSpec(memory_space=pltpu.VMEM),
        scratch_shapes=[pltpu.VMEM((8, SMALL_ROWS, 128), f32), pltpu.SemaphoreType.DMA((7,)), pltpu.SemaphoreType.DMA((7,))],
        name="allreduce_small", compiler_params=_cp())(buf)


def _adamw_math(w, g, m, v):
    m = ADAM_B1 * m + (1.0 - ADAM_B1) * g
    v = ADAM_B2 * v + (1.0 - ADAM_B2) * (g * g)
    m_hat = m / (1.0 - ADAM_B1 ** ADAM_STEP)
    v_hat = v / (1.0 - ADAM_B2 ** ADAM_STEP)
    return -ADAM_LR * (m_hat / (jnp.sqrt(v_hat) + ADAM_EPS) + ADAM_WD * w), m, v


def _adamw(w, g, m, v, name, first=0, prev=None, dep=None):
    def body(w_ref, g_ref, m_ref, v_ref, *rest):
        go_ref, d_ref, mo_ref, vo_ref = rest[-4:]
        g = g_ref[...]
        d, mn, vn = _adamw_math(w_ref[...], g, m_ref[...], v_ref[...])
        go_ref[...] = g
        d_ref[...] = d
        mo_ref[...] = mn
        vo_ref[...] = vn

    _, rows, cols = w.shape
    r = rows // 4 if rows % 32 == 0 else rows
    spec = pl.BlockSpec((1, r, cols), lambda i, j: (first + i, j, 0))
    gspec = pl.BlockSpec((1, r, cols), lambda i, j: (i, j, 0))
    out = _sds(w.shape, f32)
    extra = [] if prev is None else list(prev)
    dspec, dop = _dep(dep)
    return pl.pallas_call(
        body, out_shape=(out, out, out, out), grid=(g.shape[0], rows // r), in_specs=[spec, gspec, spec, spec] + [ANY] * len(extra) + dspec,
        out_specs=(spec,) * 4, input_output_aliases={4 + i: i for i in range(len(extra))}, name=name,
        compiler_params=_cp("parallel", "parallel"))(w, g, m, v, *extra, *dop)


SMALL_NAMES = ("ffn1_norm", "mix_norm", "pool_w", "pool_scale", "ffn2_norm", "final_norm")
WEIGHT_ORDER = ("ffn1_norm", "ffn1_w_gate", "ffn1_w_up", "ffn1_w_down", "mix_norm", "w_in", "pool_w", "pool_scale", "w_out",
                "ffn2_norm", "ffn2_w_gate", "ffn2_w_up", "ffn2_w_down", "final_norm")


def _pack_small_params(p):
    per_layer = {k: [p[k][l] for l in range(DEPTH)] for k in ("ffn1_norm", "mix_norm", "ffn2_norm", "pool_w", "pool_scale")}
    return _pack_small(per_layer, p["final_norm"], jnp.zeros((8, 128), f32))


def kernel(x, positions, ffn1_norm, ffn1_w_gate, ffn1_w_up, ffn1_w_down, mix_norm, w_in, pool_w, pool_scale, w_out, ffn2_norm, ffn2_w_gate, ffn2_w_up, ffn2_w_down, final_norm, loss_target, m_ffn1_norm, m_ffn1_w_gate, m_ffn1_w_up, m_ffn1_w_down, m_mix_norm, m_w_in, m_pool_w, m_pool_scale, m_w_out, m_ffn2_norm, m_ffn2_w_gate, m_ffn2_w_up, m_ffn2_w_down, m_final_norm, v_ffn1_norm, v_ffn1_w_gate, v_ffn1_w_up, v_ffn1_w_down, v_mix_norm, v_w_in, v_pool_w, v_pool_scale, v_w_out, v_ffn2_norm, v_ffn2_w_gate, v_ffn2_w_up, v_ffn2_w_down, v_final_norm):
    params = dict(ffn1_norm=ffn1_norm, ffn1_w_gate=ffn1_w_gate, ffn1_w_up=ffn1_w_up, ffn1_w_down=ffn1_w_down,
                  mix_norm=mix_norm, w_in=w_in, pool_w=pool_w, pool_scale=pool_scale, w_out=w_out, ffn2_norm=ffn2_norm,
                  ffn2_w_gate=ffn2_w_gate, ffn2_w_up=ffn2_w_up, ffn2_w_down=ffn2_w_down, final_norm=final_norm)
    mom_m = dict(ffn1_norm=m_ffn1_norm, ffn1_w_gate=m_ffn1_w_gate, ffn1_w_up=m_ffn1_w_up, ffn1_w_down=m_ffn1_w_down,
                 mix_norm=m_mix_norm, w_in=m_w_in, pool_w=m_pool_w, pool_scale=m_pool_scale, w_out=m_w_out,
                 ffn2_norm=m_ffn2_norm, ffn2_w_gate=m_ffn2_w_gate, ffn2_w_up=m_ffn2_w_up, ffn2_w_down=m_ffn2_w_down,
                 final_norm=m_final_norm)
    mom_v = dict(ffn1_norm=v_ffn1_norm, ffn1_w_gate=v_ffn1_w_gate, ffn1_w_up=v_ffn1_w_up, ffn1_w_down=v_ffn1_w_down,
                 mix_norm=v_mix_norm, w_in=v_w_in, pool_w=v_pool_w, pool_scale=v_pool_scale, w_out=v_w_out,
                 ffn2_norm=v_ffn2_norm, ffn2_w_gate=v_ffn2_w_gate, ffn2_w_up=v_ffn2_w_up, ffn2_w_down=v_ffn2_w_down,
                 final_norm=v_final_norm)
    names = [t[0] for t in BIG]
    for d in (params, mom_m, mom_v):
        for k in TRANSPOSED:
            d[k] = jnp.swapaxes(d[k], 1, 2)

    place = jnp.stack([lax.axis_index("c"), 2 * lax.axis_index("x") + lax.axis_index("y")]).astype(jnp.int32)
    def gather_start(tag, cast, after):
        return _exchange_start(cast, [], after, _own_half, _slot_half, f"gather_start{tag}")

    def gather_end(started, after, tag, spec):
        send_sems, recv_sems, bufs, _, _ = started
        bufs, _ = _exchange_wait(send_sems, recv_sems, bufs, [], after, _own_half, _slot_half, f"gather_wait{tag}")
        return {nm: a.reshape(NSH, rows, cols) for (nm, rows, cols), a in zip(spec, _gather_forward(bufs))}

    tabs = _rope_tables(*_rope_inputs(positions))
    h = x.reshape(S, D)
    weights, saved = [], []
    cast0 = _cast_layer(params, 0, place)
    first = gather_start("0a", cast0[:FFN1], place)
    second = gather_start("0b", cast0[FFN1:], first[-1])
    after = second[-1]
    casts = {}
    for l in range(1, DEPTH):
        casts[l] = _cast_layer(params, l, place, after)
        after = casts[l][0]
    for l in range(DEPTH):
        if l == 0:
            w = gather_end(first, after, "0a", BIG[:FFN1])
            rest = lambda x1: gather_end(second, x1, "0b", BIG[FFN1:])
        else:
            w, rest = gather_end(started, after, l, BIG), None
        dep = None
        if l + 1 < DEPTH:
            started = gather_start(l + 1, casts[l + 1], w["g1"])
            dep = started[-1]
        h, sv, w = _layer_fwd(h, w, params, l, tabs, dep, rest)
        weights.append(w)
        saved.append(sv)
        after = h
    loss, dx, dgf = _final_loss(h, final_norm.reshape(1, D), loss_target.reshape(S, D))

    upper = [lax.empty((DEPTH - 1, 2, rows // 2, cols), f32) for _, rows, cols in BIG]
    lower = [lax.empty((1, 2, rows // 2, cols), f32) for _, rows, cols in BIG]
    sg = {k: [None] * DEPTH for k in ("ffn1_norm", "mix_norm", "pool_w", "pool_scale", "ffn2_norm")}
    sg["final_norm"] = dgf

    def reduce_end(started, after, l, full, slot):
        send_sems, recv_sems, psum, parts, _ = started
        psum, parts = _exchange_wait(send_sems, recv_sems, psum, parts, after, _slot, _slot, f"grad_wait{l}")
        return _chip_sum(psum, parts, full, place, slot, f"chip_sum{l}")

    started, dep, shares = None, None, []
    for l in reversed(range(DEPTH)):
        dx, gr, sgl = _layer_bwd(dx, weights[l], saved[l], tabs, dep)
        for k, v in sgl.items():
            sg[k][l] = v
        if started is not None:
            send_sems, recv_sems, upper = _share_start(reduce_end(started, dx, l + 1, upper, l), l, place, f"share_start{l + 1}")
            shares.append((send_sems, recv_sems, l, l + 1))
        grads = [gr[nm] for nm in names]
        psum = _pair_sum(grads, _sibling_swap(grads), place)
        parts = [lax.empty(a.shape, bf16) for a in psum]
        started = _exchange_start(psum, parts, place, _slot, _slot, f"grad_start{l}")
        dep = started[-1]

    big_out = {}
    for send_sems, recv_sems, slot, l in shares:
        upper = _share_wait(send_sems, recv_sems, upper, slot, place, f"share_wait{l}")
    for (nm, rows, cols), g in zip(BIG, upper):
        k = BIG_SRC[nm]
        big_out[k] = _adamw(params[k], g.reshape(DEPTH - 1, rows, cols), mom_m[k], mom_v[k], "adamw_upper_" + k, first=1, dep=dep)
        dep = big_out[k][1]
    lower = _sibling_share(reduce_end(started, dep, 0, lower, 0), 0, "sibling_share0")
    for (nm, rows, cols), g in zip(BIG, lower):
        k = BIG_SRC[nm]
        big_out[k] = _adamw(params[k], g.reshape(1, rows, cols), mom_m[k], mom_v[k], "adamw_lower_" + k, first=0, prev=big_out[k])

    per_layer = {k: sg[k] for k in ("ffn1_norm", "mix_norm", "ffn2_norm", "pool_w", "pool_scale")}
    small_sum = _allreduce_small(_pack_small(per_layer, sg["final_norm"], loss))
    gs, ds_, ms, vs = _adamw(_pack_small_params(params).reshape(1, SMALL_ROWS, 128), small_sum.reshape(1, SMALL_ROWS, 128),
                             _pack_small_params(mom_m).reshape(1, SMALL_ROWS, 128),
                             _pack_small_params(mom_v).reshape(1, SMALL_ROWS, 128), "adamw_small")
    small_out = [_unpack_small(a.reshape(SMALL_ROWS, 128)) for a in (gs, ds_, ms, vs)]

    grad, delta, new_m, new_v = {}, {}, {}, {}
    for k in WEIGHT_ORDER:
        if k in SMALL_NAMES:
            grad[k], delta[k], new_m[k], new_v[k] = (so[k] for so in small_out)
        else:
            grad[k], delta[k], new_m[k], new_v[k] = big_out[k]
    for d in (grad, delta, new_m, new_v):
        for k in TRANSPOSED:
            d[k] = jnp.swapaxes(d[k], 1, 2)
    return (small_out[0]["loss"], dx.reshape(1, S, D), *[grad[k] for k in WEIGHT_ORDER], *[delta[k] for k in WEIGHT_ORDER],
            *[new_m[k] for k in WEIGHT_ORDER], *[new_v[k] for k in WEIGHT_ORDER])
```

```python
import functools

import jax
import jax.numpy as jnp
import numpy as np
from jax import lax
from jax.experimental import pallas as pl
from jax.experimental.pallas import tpu as pltpu

f32 = jnp.float32
bf16 = jnp.bfloat16

S = 2048
D = 1024
DEPTH = 4
NSH = 4
FS = 704
PROJ = 2560
PS = 640
PW = 256
AW = 768
NPAIR = 6
NORM_EPS = 1e-6
MASK_VALUE = -1e30
ROPE_THETA = 500000.0
DILATIONS = (1, 4, 16)
QBLK = 128
NBLK = S // QBLK
TM = 512
VMEM_LIMIT = 56 * 1024 * 1024

ADAM_LR = 0.001
ADAM_B1 = 0.9
ADAM_B2 = 0.999
ADAM_EPS = 1e-08
ADAM_WD = 0.01
ADAM_STEP = 10

MESH = pl.DeviceIdType.MESH
ANY = pl.BlockSpec(memory_space=pl.ANY)

BIG = (("g1", FS, D), ("u1", FS, D), ("d1", FS, D), ("wi", D, PS), ("wo", PW, D), ("g2", FS, D), ("u2", FS, D), ("d2", FS, D))
TRANSPOSED = ("ffn1_w_gate", "ffn1_w_up", "ffn2_w_gate", "ffn2_w_up")
FFN1 = 3
BIG_SRC = {"g1": "ffn1_w_gate", "u1": "ffn1_w_up", "d1": "ffn1_w_down", "wi": "w_in", "wo": "w_out",
           "g2": "ffn2_w_gate", "u2": "ffn2_w_up", "d2": "ffn2_w_down"}


def _cp(*sem):
    return pltpu.CompilerParams(dimension_semantics=sem if sem else None, vmem_limit_bytes=VMEM_LIMIT)


def _sds(shape, dt):
    return jax.ShapeDtypeStruct(shape, dt)


def _dot(a, b):
    return jnp.dot(a, b, preferred_element_type=f32)


def _dot_nt(a, b):
    return lax.dot_general(a, b, (((1,), (1,)), ((), ())), preferred_element_type=f32)


def _dot_tn(a, b):
    return lax.dot_general(a, b, (((0,), (0,)), ((), ())), preferred_element_type=f32)


def _dep(dep):
    return ([], []) if dep is None else ([ANY], [dep])


def _resident(shape):
    return pl.BlockSpec(shape, lambda i: (0,) * len(shape), pipeline_mode=pl.Buffered(1))


def _ffn_fwd(x, g, wg, wu, wd, dep=None):
    def body(x_ref, g_ref, wg_ref, wu_ref, wd_ref, *rest):
        xo_ref, h_ref, a_ref, b_ref = rest[-4:]
        xf = x_ref[...]
        r = lax.rsqrt(jnp.mean(xf * xf, axis=-1, keepdims=True) + NORM_EPS)
        hh = ((xf * r) * g_ref[...]).astype(bf16)
        h_ref[...] = hh
        acc = None
        for s in range(NSH):
            a = _dot_nt(hh, wg_ref[s])
            b = _dot_nt(hh, wu_ref[s])
            a_ref[s] = a.astype(bf16)
            b_ref[s] = b.astype(bf16)
            p = _dot((a * (1.0 / (1.0 + jnp.exp(-a))) * b).astype(bf16), wd_ref[s])
            acc = p if acc is None else acc + p
        xo_ref[...] = xf + 0.5 * acc

    tok = pl.BlockSpec((TM, D), lambda i: (i, 0))
    hid = pl.BlockSpec((NSH, TM, FS), lambda i: (0, i, 0))
    wsp = _resident((NSH, FS, D))
    dspec, dop = _dep(dep)
    return pl.pallas_call(
        body, out_shape=(_sds((S, D), f32), _sds((S, D), bf16), _sds((NSH, S, FS), bf16), _sds((NSH, S, FS), bf16)),
        grid=(S // TM,), in_specs=[tok, pl.BlockSpec((1, D), lambda i: (0, 0)), wsp, wsp, wsp] + dspec,
        out_specs=(tok, tok, hid, hid), name="ffn_fwd", compiler_params=_cp("parallel"))(x, g, wg, wu, wd, *dop)


def _in_proj(x, g, wi):
    def body(x_ref, g_ref, w_ref, o_ref, h_ref):
        xf = x_ref[...]
        r = lax.rsqrt(jnp.mean(xf * xf, axis=-1, keepdims=True) + NORM_EPS)
        hh = ((xf * r) * g_ref[...]).astype(bf16)
        h_ref[...] = hh
        for s in range(NSH):
            o_ref[:, PS * s:PS * (s + 1)] = _dot(hh, w_ref[s])

    tok = pl.BlockSpec((TM, D), lambda i: (i, 0))
    return pl.pallas_call(
        body, out_shape=(_sds((S, PROJ), f32), _sds((S, D), bf16)), grid=(S // TM,),
        in_specs=[tok, pl.BlockSpec((1, D), lambda i: (0, 0)), _resident((NSH, D, PS))],
        out_specs=(pl.BlockSpec((TM, PROJ), lambda i: (i, 0)), tok), name="in_proj", compiler_params=_cp("parallel"))(x, g, wi)


def _out_proj(x, mixed, wo):
    def body(x_ref, m_ref, w_ref, o_ref):
        o_ref[...] = x_ref[...] + _dot(m_ref[...], w_ref[...].reshape(D, D))

    return pl.pallas_call(
        body, out_shape=_sds((S, D), f32), grid=(S // TM,),
        in_specs=[pl.BlockSpec((TM, D), lambda i: (i, 0)), pl.BlockSpec((TM, D), lambda i: (i, 0)),
                  pl.BlockSpec((NSH, PW, D), lambda i: (0, 0, 0))],
        out_specs=pl.BlockSpec((TM, D), lambda i: (i, 0)), name="out_proj", compiler_params=_cp("parallel"))(x, mixed, wo)


def _out_proj_bwd(dx, wo):
    def body(dx_ref, w_ref, o_ref):
        o_ref[...] = _dot_nt(dx_ref[...].astype(bf16), w_ref[...].reshape(D, D))

    return pl.pallas_call(
        body, out_shape=_sds((S, D), f32), grid=(S // TM,),
        in_specs=[pl.BlockSpec((TM, D), lambda i: (i, 0)), pl.BlockSpec((NSH, PW, D), lambda i: (0, 0, 0))],
        out_specs=pl.BlockSpec((TM, D), lambda i: (i, 0)), name="out_proj_bwd", compiler_params=_cp("parallel"))(dx, wo)


def _ffn_bwd_mid(dx, h, a, b, wd, dep=None):
    nt = S // TM

    def body(dx_ref, h_ref, a_ref, b_ref, wd_ref, *rest):
        da_ref, db_ref, dwd_ref, dwg_ref, dwu_ref, dy_s, u_s, da_s, db_s = rest[-9:]
        i = pl.program_id(1)
        rows = pl.ds(pl.multiple_of(i * TM, TM), TM)
        dy = (0.5 * dx_ref[...]).astype(bf16)
        dy_s[rows, :] = dy
        du = _dot_nt(dy, wd_ref[0])
        a = a_ref[0].astype(f32)
        b = b_ref[0].astype(f32)
        sig = 1.0 / (1.0 + jnp.exp(-a))
        silu = a * sig
        da = (du * b * (sig * (1.0 + a * (1.0 - sig)))).astype(bf16)
        db = (du * silu).astype(bf16)
        da_ref[0] = da
        db_ref[0] = db
        da_s[rows, :] = da
        db_s[rows, :] = db
        u_s[rows, :] = (silu * b).astype(bf16)

        @pl.when(i == nt - 1)
        def _():
            hh = h_ref[...]
            dwd_ref[...] = _dot_tn(u_s[...], dy_s[...]).astype(bf16).reshape(dwd_ref.shape)
            dwg_ref[...] = _dot_tn(da_s[...], hh).astype(bf16).reshape(dwg_ref.shape)
            dwu_ref[...] = _dot_tn(db_s[...], hh).astype(bf16).reshape(dwu_ref.shape)

    tok = pl.BlockSpec((TM, D), lambda s, i: (i, 0))
    hid = pl.BlockSpec((1, TM, FS), lambda s, i: (s, i, 0))
    wsp = pl.BlockSpec((1, 2, FS // 2, D), lambda s, i: (s, 0, 0, 0))
    hidden = _sds((NSH, S, FS), bf16)
    wgrad = _sds((NSH, 2, FS // 2, D), bf16)
    whole = pltpu.VMEM((S, FS), bf16)
    dspec, dop = _dep(dep)
    return pl.pallas_call(
        body, out_shape=(hidden, hidden, wgrad, wgrad, wgrad), grid=(NSH, nt),
        in_specs=[tok, pl.BlockSpec((S, D), lambda s, i: (0, 0), pipeline_mode=pl.Buffered(1)), hid, hid,
                  pl.BlockSpec((1, FS, D), lambda s, i: (s, 0, 0))] + dspec,
        out_specs=(hid, hid, wsp, wsp, wsp), scratch_shapes=[pltpu.VMEM((S, D), bf16), whole, whole, whole],
        name="ffn_bwd_mid", compiler_params=_cp("parallel", "arbitrary"))(dx, h, a, b, wd, *dop)


def _norm_bwd_tail(acc, x_ref, dxin_ref, g_ref, dxo_ref, dg_ref, first):
    xf = x_ref[...]
    r = lax.rsqrt(jnp.mean(xf * xf, axis=-1, keepdims=True) + NORM_EPS)
    xhat = xf * r
    dhg = acc * g_ref[...]
    dxo_ref[...] = dxin_ref[...] + r * (dhg - xhat * jnp.mean(dhg * xhat, axis=-1, keepdims=True))
    part = jnp.sum(acc * xhat, axis=0, keepdims=True)

    @pl.when(first)
    def _():
        dg_ref[...] = part

    @pl.when(jnp.logical_not(first))
    def _():
        dg_ref[...] += part


def _ffn_bwd_dx(dx, x_in, g, da, db, wg, wu):
    def body(dx_ref, x_ref, g_ref, da_ref, db_ref, wg_ref, wu_ref, dxo_ref, dg_ref):
        acc = None
        for s in range(NSH):
            p = _dot(da_ref[s], wg_ref[s])
            acc = p if acc is None else acc + p
            acc = acc + _dot(db_ref[s], wu_ref[s])
        _norm_bwd_tail(acc, x_ref, dx_ref, g_ref, dxo_ref, dg_ref, pl.program_id(0) == 0)

    tok = pl.BlockSpec((TM, D), lambda i: (i, 0))
    vec = pl.BlockSpec((1, D), lambda i: (0, 0))
    hid = pl.BlockSpec((NSH, TM, FS), lambda i: (0, i, 0))
    wsp = _resident((NSH, FS, D))
    return pl.pallas_call(
        body, out_shape=(_sds((S, D), f32), _sds((1, D), f32)), grid=(S // TM,),
        in_specs=[tok, tok, vec, hid, hid, wsp, wsp], out_specs=(tok, vec),
        name="ffn_bwd_dx", compiler_params=_cp("arbitrary"))(dx, x_in, g, da, db, wg, wu)


def _in_proj_bwd_dx(dx, x_in, g, dproj, wi):
    def body(dx_ref, x_ref, g_ref, dp_ref, w_ref, dxo_ref, dg_ref):
        acc = None
        for s in range(NSH):
            p = _dot_nt(dp_ref[:, PS * s:PS * (s + 1)], w_ref[s])
            acc = p if acc is None else acc + p
        _norm_bwd_tail(acc, x_ref, dx_ref, g_ref, dxo_ref, dg_ref, pl.program_id(0) == 0)

    tok = pl.BlockSpec((TM, D), lambda i: (i, 0))
    vec = pl.BlockSpec((1, D), lambda i: (0, 0))
    return pl.pallas_call(
        body, out_shape=(_sds((S, D), f32), _sds((1, D), f32)), grid=(S // TM,),
        in_specs=[tok, tok, vec, pl.BlockSpec((TM, PROJ), lambda i: (i, 0)), _resident((NSH, D, PS))], out_specs=(tok, vec),
        name="in_proj_bwd_dx", compiler_params=_cp("arbitrary"))(dx, x_in, g, dproj, wi)


def _dw(lhs, rhs, lhs_spec, rhs_spec, rows, cols, name, cast_rhs=False):
    def body(l_ref, r_ref, o_ref):
        r = r_ref[...].astype(bf16) if cast_rhs else r_ref[...]
        o_ref[...] = _dot_tn(l_ref[...], r).astype(bf16).reshape(1, 2, rows // 2, cols)

    return pl.pallas_call(
        body, out_shape=_sds((NSH, 2, rows // 2, cols), bf16), grid=(NSH,), in_specs=[lhs_spec, rhs_spec],
        out_specs=pl.BlockSpec((1, 2, rows // 2, cols), lambda s: (s, 0, 0, 0)), name=name, compiler_params=_cp("parallel"))(lhs, rhs)


_WHOLE_TOK = pl.BlockSpec((S, D), lambda s: (0, 0))


def _dw_in(h, dproj):
    return _dw(h, dproj, _WHOLE_TOK, pl.BlockSpec((S, PS), lambda s: (0, s)), D, PS, "dw_in")


def _dw_out(mixed, dx):
    return _dw(mixed, dx, pl.BlockSpec((S, PW), lambda s: (0, s)), _WHOLE_TOK, PW, D, "dw_out", cast_rhs=True)


def _final_loss(x, g, target):
    def body(x_ref, g_ref, t_ref, loss_ref, dx_ref, dg_ref):
        i = pl.program_id(0)
        xf = x_ref[...]
        r = lax.rsqrt(jnp.mean(xf * xf, axis=-1, keepdims=True) + NORM_EPS)
        xhat = xf * r
        err = xhat * g_ref[...] - t_ref[...]
        dy = err * (1.0 / D)
        dhg = dy * g_ref[...]
        dx_ref[...] = r * (dhg - xhat * jnp.mean(dhg * xhat, axis=-1, keepdims=True))
        part = jnp.sum(dy * xhat, axis=0, keepdims=True)
        lpart = jnp.zeros((8, 128), f32) + 0.5 * jnp.sum(jnp.mean(err * err, axis=-1, keepdims=True))

        @pl.when(i == 0)
        def _():
            dg_ref[...] = part
            loss_ref[...] = lpart

        @pl.when(i != 0)
        def _():
            dg_ref[...] += part
            loss_ref[...] += lpart

    tok = pl.BlockSpec((TM, D), lambda i: (i, 0))
    vec = pl.BlockSpec((1, D), lambda i: (0, 0))
    return pl.pallas_call(
        body, out_shape=(_sds((8, 128), f32), _sds((S, D), f32), _sds((1, D), f32)), grid=(S // TM,),
        in_specs=[tok, vec, tok], out_specs=(pl.BlockSpec((8, 128), lambda i: (0, 0)), tok, vec),
        name="final_loss", compiler_params=_cp("arbitrary"))(x, g, target)


def _shift_down(x, k, row):
    return jnp.where(row >= k, pltpu.roll(x, k, axis=0), 0.0)


def _shift_up(x, k, row):
    return jnp.where(row < S - k, pltpu.roll(x, S - k, axis=0), 0.0)


def _pool_geometry():
    row = lax.broadcasted_iota(jnp.int32, (S, PW), 0)
    grp = lax.broadcasted_iota(jnp.int32, (S, PW), 1) // 64
    half = jnp.where(grp == 0, 1, jnp.where(grp == 1, 2, jnp.where(grp == 2, 4, 8)))
    hi = jnp.minimum(row + half - 1, S - 1)
    lo = jnp.maximum(row - half, 0)
    return row, grp, (hi - lo + 1).astype(f32)


def _by_group(grp, v0, v1, v2, v3):
    return jnp.where(grp == 0, v0, jnp.where(grp == 1, v1, jnp.where(grp == 2, v2, v3)))


def _window_sums(x, row, grp, transpose):
    l1, r1 = x, x
    l2, r2 = l1 + _shift_down(l1, 1, row), r1 + _shift_up(r1, 1, row)
    l4, r4 = l2 + _shift_down(l2, 2, row), r2 + _shift_up(r2, 2, row)
    l8, r8 = l4 + _shift_down(l4, 4, row), r4 + _shift_up(r4, 4, row)
    lsel = _by_group(grp, l1, l2, l4, l8)
    rsel = _by_group(grp, r1, r2, r4, r8)
    if transpose:
        return lsel + _shift_up(rsel, 1, row)
    return _shift_down(lsel, 1, row) + rsel


def _pool_fwd(proj, wbd, scale):
    def body(v_ref, w_ref, sc_ref, mixed_ref, diff_ref):
        row, grp, cnt = _pool_geometry()
        v = v_ref[...]
        diff = (_window_sums(v, row, grp, False) / cnt - v).astype(bf16)
        diff_ref[...] = diff
        mixed_ref[...] = (_dot(diff, w_ref[...].astype(bf16)) * sc_ref[...]).astype(bf16)

    col = pl.BlockSpec((S, PW), lambda i: (0, 0))
    return pl.pallas_call(
        body, out_shape=(_sds((S, D), bf16), _sds((S, PW), bf16)), grid=(1,),
        in_specs=[col, pl.BlockSpec((PW, PW), lambda i: (0, 0)), pl.BlockSpec((1, PW), lambda i: (0, 0))],
        out_specs=(col, col), name="pool_fwd", compiler_params=_cp("arbitrary"))(proj, wbd, scale)


def _pool_bwd(dmixed, diff, wbd, scale, dproj):
    def body(dy_ref, diff_ref, w_ref, sc_ref, dproj_in, dv_ref, dw_ref, dsc_ref):
        del dproj_in
        row, grp, cnt = _pool_geometry()
        dy = dy_ref[...]
        diff = diff_ref[...]
        w = w_ref[...].astype(bf16)
        dsc_ref[...] = jnp.sum(dy * _dot(diff, w), axis=0, keepdims=True)
        dys = (dy * sc_ref[...]).astype(bf16)
        dw_ref[...] = _dot_tn(diff, dys)
        ddiff = _dot_nt(dys, w)
        dv_ref[...] = (_window_sums(ddiff / cnt, row, grp, True) - ddiff).astype(bf16)

    col = pl.BlockSpec((S, PW), lambda i: (0, 0))
    return pl.pallas_call(
        body, out_shape=(_sds((S, PROJ), bf16), _sds((PW, PW), f32), _sds((1, PW), f32)), grid=(1,),
        in_specs=[col, col, pl.BlockSpec((PW, PW), lambda i: (0, 0)), pl.BlockSpec((1, PW), lambda i: (0, 0)), ANY],
        out_specs=(col, pl.BlockSpec((PW, PW), lambda i: (0, 0)), pl.BlockSpec((1, PW), lambda i: (0, 0))),
        input_output_aliases={4: 0}, name="pool_bwd", compiler_params=_cp("arbitrary"))(dmixed, diff, wbd, scale, dproj)


def _rope_tables(pos_col, freq_row):
    def body(p_ref, f_ref, c_ref, a_ref, b_ref):
        ang = p_ref[...].astype(f32) * f_ref[...]
        l64 = lax.broadcasted_iota(jnp.int32, (S, 128), 1) % 64
        cos, sin = jnp.cos(ang), jnp.sin(ang)
        c_ref[...] = jnp.where(l64 < 16, cos, 1.0)
        a_ref[...] = jnp.where(l64 < 8, -sin, 0.0)
        b_ref[...] = jnp.where((l64 >= 8) & (l64 < 16), sin, 0.0)

    t = _sds((S, 128), f32)
    return pl.pallas_call(body, out_shape=(t, t, t), name="rope_tables", compiler_params=_cp())(pos_col, freq_row)


def _rope(t, c, a, b):
    return t * c + pltpu.roll(t, 120, axis=1) * a + pltpu.roll(t, 8, axis=1) * b


def _rope_bwd(g, c, a, b):
    return g * c + pltpu.roll(g * a, 8, axis=1) + pltpu.roll(g * b, 120, axis=1)


def _perm_load(ref, d):
    if d == 1:
        return ref[...]
    n = S // d
    return jnp.concatenate([ref[pl.ds(r, n, stride=d), :] for r in range(d)], axis=0)


def _unperm_store(ref, val, d):
    if d == 1:
        ref[...] = val
        return
    n = S // d
    for r in range(d):
        ref[pl.ds(r, n, stride=d), :] = val[r * n:(r + 1) * n, :]


def _band(xp, d):
    if d == NBLK:
        return xp.reshape(NBLK, QBLK, 128)
    z = jnp.zeros((64, 128), bf16)
    p = jnp.concatenate([z, xp, z], axis=0).reshape(NBLK + 1, QBLK, 128)
    return jnp.concatenate([p[:NBLK], p[1:]], axis=1)


def _unband(xb, d):
    if d == NBLK:
        return xb.reshape(S, 128)
    z = jnp.zeros((1, QBLK, 128), f32)
    p = jnp.concatenate([xb[:, :QBLK], z], axis=0) + jnp.concatenate([z, xb[:, QBLK:]], axis=0)
    return p.reshape(S + QBLK, 128)[64:S + 64]


def _band_mask(d):
    if d == NBLK:
        a = lax.broadcasted_iota(jnp.int32, (1, 2 * QBLK, QBLK), 1) & (QBLK - 1)
        b = lax.broadcasted_iota(jnp.int32, (1, 2 * QBLK, QBLK), 2)
        return (b >= a - 64) & (b <= a + 64)
    blocks_per_class = NBLK // d
    n = lax.broadcasted_iota(jnp.int32, (NBLK, 1, 2 * QBLK), 0) & (blocks_per_class - 1)
    be = lax.broadcasted_iota(jnp.int32, (NBLK, 1, 2 * QBLK), 2)
    a = lax.broadcasted_iota(jnp.int32, (1, 2 * QBLK, 2 * QBLK), 1) & (QBLK - 1)
    b = lax.broadcasted_iota(jnp.int32, (1, 2 * QBLK, 2 * QBLK), 2)
    band = (b >= a) & (b <= a + 128)
    edge = ((be >= 64) | (n != 0)) & ((be < QBLK + 64) | (n != blocks_per_class - 1))
    return band & edge


def _stack_heads(xb, lo):
    z = jnp.zeros_like(xb)
    return jnp.concatenate([jnp.where(lo, xb, z), jnp.where(lo, z, xb)], axis=1)


def _unstack_heads(x2, lo):
    return jnp.where(lo, x2[:, :QBLK], x2[:, QBLK:])


def _rows_to_lanes(col2, lo):
    return jnp.where(lo, jnp.broadcast_to(col2[:, :QBLK], (NBLK, QBLK, 128)), jnp.broadcast_to(col2[:, QBLK:], (NBLK, QBLK, 128)))


def _bmm_nt(a, b):
    return jnp.einsum('nqd,nkd->nqk', a, b, preferred_element_type=f32)


def _bmm_nn(a, b):
    return jnp.einsum('nqk,nkd->nqd', a, b, preferred_element_type=f32)


def _bmm_tn(a, b):
    return jnp.einsum('nqk,nqd->nkd', a, b, preferred_element_type=f32)


def _attn_fwd(proj, tc, ta, tb, mixed):
    def body(q_ref, k_ref, v_ref, c_ref, a_ref, b_ref, mixed_in, mixed_ref, o_ref, lse_ref, qn, kn, t_num, t_m, t_den):
        del mixed_in
        lo = lax.broadcasted_iota(jnp.int32, (1, 1, 128), 2) < 64
        c, a, b = c_ref[...], a_ref[...], b_ref[...]
        qn[...] = _rope(q_ref[...], c, a, b)
        kn[...] = _rope(k_ref[...], c, a, b)
        run = None
        for d in DILATIONS:
            q2 = _stack_heads(_perm_load(qn, d).astype(bf16).reshape(NBLK, QBLK, 128), lo)
            kb = _band(_perm_load(kn, d).astype(bf16), d)
            vb = _band(_perm_load(v_ref, d).astype(bf16), d)
            s = jnp.where(_band_mask(d), _bmm_nt(q2, kb) * 0.125, MASK_VALUE)
            m = jnp.max(s, axis=2, keepdims=True)
            p = jnp.exp(s - m)
            den = jnp.sum(p, axis=2, keepdims=True)
            num = _unstack_heads(_bmm_nn(p.astype(bf16), vb), lo)
            _unperm_store(t_num, num.reshape(S, 128), d)
            _unperm_store(t_m, _rows_to_lanes(m, lo).reshape(S, 128), d)
            _unperm_store(t_den, _rows_to_lanes(den, lo).reshape(S, 128), d)
            if run is None:
                run = (t_m[...], t_num[...], t_den[...])
            else:
                m_new = jnp.maximum(run[0], t_m[...])
                w_old, w_new = jnp.exp(run[0] - m_new), jnp.exp(t_m[...] - m_new)
                run = (m_new, w_old * run[1] + w_new * t_num[...], w_old * run[2] + w_new * t_den[...])
        out = run[1] / run[2]
        o_ref[...] = out
        mixed_ref[...] = out.astype(bf16)
        lse_ref[...] = run[0] + jnp.log(run[2])

    def col(off):
        return pl.BlockSpec((S, 128), lambda j, off=off: (0, off + j))

    tab = pl.BlockSpec((S, 128), lambda j: (0, 0))
    scr = pltpu.VMEM((S, 128), f32)
    return pl.pallas_call(
        body, out_shape=(_sds((S, D), bf16), _sds((S, AW), f32), _sds((S, AW), f32)), grid=(NPAIR,),
        in_specs=[col(2), col(8), col(14), tab, tab, tab, ANY], out_specs=(col(2), col(0), col(0)),
        scratch_shapes=[scr, scr, scr, scr, scr], input_output_aliases={6: 0}, name="attn_fwd",
        compiler_params=_cp("arbitrary"))(proj, proj, proj, tc, ta, tb, mixed)


def _attn_bwd(proj, tc, ta, tb, o, lse, dmixed):
    def body(q_ref, k_ref, v_ref, c_ref, a_ref, b_ref, o_ref, lse_ref, do_ref, dp_ref, qn, kn, tmp, dk_s, dv_s):
        t = pl.program_id(1)

        @pl.when(t == 0)
        def _():
            lo = lax.broadcasted_iota(jnp.int32, (1, 1, 128), 2) < 64
            c, a, b = c_ref[...], a_ref[...], b_ref[...]
            qn[...] = _rope(q_ref[...], c, a, b)
            kn[...] = _rope(k_ref[...], c, a, b)
            dq = dk = dv = None
            for d in DILATIONS:
                q2 = _stack_heads(_perm_load(qn, d).astype(bf16).reshape(NBLK, QBLK, 128), lo)
                kb = _band(_perm_load(kn, d).astype(bf16), d)
                vb = _band(_perm_load(v_ref, d).astype(bf16), d)
                dob = _perm_load(do_ref, d).reshape(NBLK, QBLK, 128)
                ob = _perm_load(o_ref, d).reshape(NBLK, QBLK, 128)
                lsb = _perm_load(lse_ref, d).reshape(NBLK, QBLK, 128)
                do2 = _stack_heads(dob.astype(bf16), lo)
                delta2 = jnp.sum(_stack_heads(dob * ob, lo), axis=2, keepdims=True)
                lse2 = jnp.max(jnp.concatenate([jnp.where(lo, lsb, MASK_VALUE), jnp.where(lo, MASK_VALUE, lsb)], axis=1),
                               axis=2, keepdims=True)
                s = _bmm_nt(q2, kb) * 0.125
                p = jnp.where(_band_mask(d), jnp.exp(s - lse2), 0.0)
                ds = (p * (_bmm_nt(do2, vb) - delta2) * 0.125).astype(bf16)
                pb = p.astype(bf16)
                dq_b = _unstack_heads(_bmm_nn(ds, kb), lo).reshape(S, 128)
                dk_b = _unband(_bmm_tn(ds, q2), d)
                dv_b = _unband(_bmm_tn(pb, do2), d)
                acc = []
                for prev, new in ((dq, dq_b), (dk, dk_b), (dv, dv_b)):
                    _unperm_store(tmp, new, d)
                    acc.append(tmp[...] if prev is None else prev + tmp[...])
                dq, dk, dv = acc
            dp_ref[...] = _rope_bwd(dq, c, a, b).astype(bf16)
            dk_s[...] = _rope_bwd(dk, c, a, b).astype(bf16)
            dv_s[...] = dv.astype(bf16)

        @pl.when(t == 1)
        def _():
            dp_ref[...] = dk_s[...]

        @pl.when(t == 2)
        def _():
            dp_ref[...] = dv_s[...]

    def col(off):
        return pl.BlockSpec((S, 128), lambda j, t, off=off: (0, off + j))

    tab = pl.BlockSpec((S, 128), lambda j, t: (0, 0))
    scr = pltpu.VMEM((S, 128), f32)
    scb = pltpu.VMEM((S, 128), bf16)
    return pl.pallas_call(
        body, out_shape=_sds((S, PROJ), bf16), grid=(NPAIR, 3),
        in_specs=[col(2), col(8), col(14), tab, tab, tab, col(0), col(0), col(2)],
        out_specs=pl.BlockSpec((S, 128), lambda j, t: (0, 2 + NPAIR * t + j)),
        scratch_shapes=[scr, scr, scr, scb, scb], name="attn_bwd",
        compiler_params=_cp("arbitrary", "arbitrary"))(proj, proj, proj, tc, ta, tb, o, lse, dmixed)


def _block_diag(w4):
    out = jnp.zeros((PW, PW), w4.dtype)
    for g in range(4):
        out = out.at[64 * g:64 * (g + 1), 64 * g:64 * (g + 1)].set(w4[g])
    return out


def _diag_blocks(w):
    return jnp.stack([w[64 * g:64 * (g + 1), 64 * g:64 * (g + 1)] for g in range(4)])


def _rope_inputs(positions):
    inv_freq = ROPE_THETA ** (-jnp.arange(0, 16, 2, dtype=f32) / 16)
    l64 = np.arange(128) % 64
    idx = np.where(l64 < 16, l64 % 8, 0)
    return positions.reshape(S, 1), inv_freq[idx].reshape(1, 128)


def _layer_fwd(x, w, small, l, tabs, dep=None, rest=None):
    g1, gm, g2 = (small[k][l].reshape(1, D) for k in ("ffn1_norm", "mix_norm", "ffn2_norm"))
    wbd = _block_diag(small["pool_w"][l])
    psc = small["pool_scale"][l].reshape(1, PW)
    x1, h1, a1, b1 = _ffn_fwd(x, g1, w["g1"], w["u1"], w["d1"], dep)
    if rest is not None:
        w = {**w, **rest(x1)}
    proj, h2 = _in_proj(x1, gm, w["wi"])
    mixed, diff = _pool_fwd(proj, wbd, psc)
    mixed, o, lse = _attn_fwd(proj, *tabs, mixed)
    x2 = _out_proj(x1, mixed, w["wo"])
    out, h3, a2, b2 = _ffn_fwd(x2, g2, w["g2"], w["u2"], w["d2"])
    return out, dict(x0=x, h1=h1, a1=a1, b1=b1, x1=x1, h2=h2, proj=proj, mixed=mixed, diff=diff, o=o, lse=lse,
                     x2=x2, h3=h3, a2=a2, b2=b2, g1=g1, gm=gm, g2=g2, wbd=wbd, psc=psc), w


def _layer_bwd(dx, w, sv, tabs, dep=None):
    gr, sg = {}, {}
    da, db, gr["d2"], gr["g2"], gr["u2"] = _ffn_bwd_mid(dx, sv["h3"], sv["a2"], sv["b2"], w["d2"], dep)
    dx, sg["ffn2_norm"] = _ffn_bwd_dx(dx, sv["x2"], sv["g2"], da, db, w["g2"], w["u2"])
    gr["wo"] = _dw_out(sv["mixed"], dx)
    dmixed = _out_proj_bwd(dx, w["wo"])
    dproj = _attn_bwd(sv["proj"], *tabs, sv["o"], sv["lse"], dmixed)
    dproj, dwbd, sg["pool_scale"] = _pool_bwd(dmixed, sv["diff"], sv["wbd"], sv["psc"], dproj)
    sg["pool_w"] = _diag_blocks(dwbd)
    gr["wi"] = _dw_in(sv["h2"], dproj)
    dx, sg["mix_norm"] = _in_proj_bwd_dx(dx, sv["x1"], sv["gm"], dproj, w["wi"])
    da, db, gr["d1"], gr["g1"], gr["u1"] = _ffn_bwd_mid(dx, sv["h1"], sv["a1"], sv["b1"], w["d1"])
    dx, sg["ffn1_norm"] = _ffn_bwd_dx(dx, sv["x0"], sv["g1"], da, db, w["g1"], w["u1"])
    return dx, gr, sg


def _forward_backward(x, positions, target, gathered, small):
    tabs = _rope_tables(*_rope_inputs(positions))
    saved = []
    for l in range(DEPTH):
        x, sv, _ = _layer_fwd(x, gathered[l], small, l, tabs)
        saved.append(sv)
    loss, dx, dgf = _final_loss(x, small["final_norm"].reshape(1, D), target)
    big = [None] * DEPTH
    sg = {k: [None] * DEPTH for k in ("ffn1_norm", "mix_norm", "pool_w", "pool_scale", "ffn2_norm")}
    for l in reversed(range(DEPTH)):
        dx, big[l], sgl = _layer_bwd(dx, gathered[l], saved[l], tabs)
        for k, v in sgl.items():
            sg[k][l] = v
    sg["final_norm"] = dgf
    return loss, dx, big, sg


def _place():
    x, y, c = lax.axis_index("x"), lax.axis_index("y"), lax.axis_index("c")
    chips = [(1 - x, y), (x, 1 - y), (1 - x, 1 - y)]
    return x, y, c, chips


def _cast_layer(params, l, place, dep=None):
    def body(p_ref, *refs):
        del p_ref
        for i_ref, o_ref in zip(refs[:8], refs[-8:]):
            o_ref[...] = i_ref[...].astype(bf16).reshape(o_ref.shape)

    ins, in_specs, out_shape, out_specs = [], [], [], []
    for name, rows, cols in BIG:
        q = rows // 4
        ins.append(params[BIG_SRC[name]])
        in_specs.append(pl.BlockSpec((1, q, cols), lambda i, p, l=l: (l, i, 0)))
        out_shape.append(_sds((NSH, 2, rows // 2, cols), bf16))
        out_specs.append(pl.BlockSpec((1, 1, q, cols), lambda i, p: (p[1], i // 2, i % 2, 0)))
    dspec, dop = _dep(dep)
    return pl.pallas_call(
        body, out_shape=out_shape,
        grid_spec=pltpu.PrefetchScalarGridSpec(num_scalar_prefetch=1, grid=(4,), in_specs=in_specs + dspec, out_specs=out_specs),
        name=f"cast_layer{l}", compiler_params=_cp("parallel"))(place, *ins, *dop)


HBM = pl.BlockSpec(memory_space=pltpu.HBM)
SEM = pl.BlockSpec(memory_space=pltpu.SEMAPHORE)
_SPLIT = pltpu.CompilerParams(has_side_effects=pltpu.SideEffectType.DATAFLOW_SIDE_EFFECTING)


def _hbm(arrays):
    return [pltpu.with_memory_space_constraint(a, pltpu.HBM) for a in arrays]


def _chip_copies(src_of, dst_of, send_sems, recv_sems, n):
    x, y, c, chips = _place()
    me = 2 * x + y
    out = []
    for t in range(n):
        for k, chip in enumerate(chips):
            peer = 2 * chip[0] + chip[1]
            send = pltpu.make_async_remote_copy(
                src_ref=src_of(t, peer), dst_ref=dst_of(t, me), send_sem=send_sems.at[3 * t + k], recv_sem=recv_sems.at[3 * t + k],
                device_id=(chip[0], chip[1], c), device_id_type=MESH)
            land = pltpu.make_async_remote_copy(
                src_ref=src_of(t, peer), dst_ref=dst_of(t, peer), send_sem=send_sems.at[3 * t + k], recv_sem=recv_sems.at[3 * t + k],
                device_id=(chip[0], chip[1], c), device_id_type=MESH)
            out.append((send, land))
    return out


def _exchange_start(src, land, after, src_of, dst_of, name):
    n, m = len(src), len(src) + len(land)

    def body(*refs):
        src_refs = refs[:n]
        land_refs = refs[n:m] if land else src_refs
        send_sems, recv_sems = refs[m + 1], refs[m + 2]
        token = refs[-1]
        for send, _ in _chip_copies(lambda t, s: src_of(src_refs[t], s), lambda t, s: dst_of(land_refs[t], s), send_sems, recv_sems, n):
            send.start()
        token[...] = jnp.zeros_like(token)

    arrays = list(src) + list(land)
    out_shape = ([pltpu.SemaphoreType.DMA((3 * n,)), pltpu.SemaphoreType.DMA((3 * n,))] + [pltpu.HBM(a.shape, a.dtype) for a in arrays]
                 + [_sds((8, 128), f32)])
    res = pl.pallas_call(
        body, out_shape=out_shape, in_specs=[HBM] * m + [ANY], out_specs=[SEM, SEM] + [HBM] * m + [pl.BlockSpec(memory_space=pltpu.VMEM)],
        input_output_aliases={i: 2 + i for i in range(m)}, name=name, compiler_params=_SPLIT)(*_hbm(arrays), after)
    return res[0], res[1], list(res[2:2 + n]), list(res[2 + n:2 + m]), res[-1]


def _exchange_wait(send_sems, recv_sems, src, land, after, src_of, dst_of, name):
    n, m = len(src), len(src) + len(land)

    def body(*refs):
        src_refs = refs[:n]
        land_refs = refs[n:m] if land else src_refs
        send_sems, recv_sems = refs[m], refs[m + 1]
        for send, land_cp in _chip_copies(lambda t, s: src_of(src_refs[t], s), lambda t, s: dst_of(land_refs[t], s), send_sems, recv_sems, n):
            send.wait_send()
            land_cp.wait_recv()

    arrays = list(src) + list(land)
    res = pl.pallas_call(
        body, out_shape=[pltpu.HBM(a.shape, a.dtype) for a in arrays], in_specs=[HBM] * m + [SEM, SEM, ANY], out_specs=[HBM] * m,
        input_output_aliases={i: i for i in range(m)}, name=name, compiler_params=_SPLIT)(*arrays, send_sems, recv_sems, after)
    return list(res[:n]), list(res[n:])


def _own_half(ref, s):
    x, y, c, _ = _place()
    return ref.at[2 * x + y, c]


def _slot_half(ref, s):
    return ref.at[s, lax.axis_index("c")]


def _slot(ref, s):
    return ref.at[s]


def _gather_forward(bufs):
    n = len(bufs)

    def body(*refs):
        outs = refs[n:2 * n]
        send_sems, recv_sems = refs[2 * n:]
        x, y, c, chips = _place()
        sibling = (x, y, 1 - c)
        passed = []
        for t in range(n):
            for k, chip in enumerate(chips):
                blk = outs[t].at[2 * chip[0] + chip[1], c]
                cp = pltpu.make_async_remote_copy(
                    src_ref=blk, dst_ref=blk, send_sem=send_sems.at[t, k], recv_sem=recv_sems.at[t, k],
                    device_id=sibling, device_id_type=MESH)
                cp.start()
                passed.append(cp)
        for t in range(n):
            for k, chip in enumerate(chips):
                blk = outs[t].at[2 * chip[0] + chip[1], 1 - c]
                pltpu.make_async_remote_copy(
                    src_ref=blk, dst_ref=blk, send_sem=send_sems.at[t, k], recv_sem=recv_sems.at[t, k],
                    device_id=sibling, device_id_type=MESH).wait_recv()
        for cp in passed:
            cp.wait_send()

    out_shape = [_sds(a.shape, bf16) for a in bufs]
    return pl.pallas_call(
        body, out_shape=out_shape, in_specs=[ANY] * n, out_specs=[ANY] * n, input_output_aliases={t: t for t in range(n)},
        scratch_shapes=[pltpu.SemaphoreType.DMA((n, 3)), pltpu.SemaphoreType.DMA((n, 3))], name="gather_forward")(*bufs)


def _sibling_swap(grads):
    n = len(grads)

    def body(*refs):
        ins, outs = refs[:n], refs[n:2 * n]
        send_sems, recv_sems = refs[2 * n:]
        x, y, c, _ = _place()
        cps = []
        for t in range(n):
            for s in range(NSH):
                cp = pltpu.make_async_remote_copy(
                    src_ref=ins[t].at[s, 1 - c], dst_ref=outs[t].at[s], send_sem=send_sems.at[t, s], recv_sem=recv_sems.at[t, s],
                    device_id=(x, y, 1 - c), device_id_type=MESH)
                cp.start()
                cps.append(cp)
        for cp in cps:
            cp.wait()

    out_shape = [_sds((NSH,) + a.shape[2:], bf16) for a in grads]
    return pl.pallas_call(
        body, out_shape=out_shape, in_specs=[ANY] * n, out_specs=[ANY] * n,
        scratch_shapes=[pltpu.SemaphoreType.DMA((n, NSH)), pltpu.SemaphoreType.DMA((n, NSH))],
        name="sibling_swap")(*grads)


def _row_tile(h):
    return h // 2 if h % 32 == 0 else h


def _pair_sum(grads, got, c_idx):
    n = len(grads)

    def body(c_ref, *refs):
        del c_ref
        for t in range(n):
            refs[2 * n + t][...] = (refs[t][...].astype(f32).reshape(refs[n + t].shape) + refs[n + t][...].astype(f32)).astype(bf16)

    in_specs, out_shape, out_specs = [], [], []
    for a in grads:
        h, cols = a.shape[2:]
        in_specs.append(pl.BlockSpec((1, 1, _row_tile(h), cols), lambda s, i, c: (s, c[0], i, 0)))
    for a in grads:
        h, cols = a.shape[2:]
        in_specs.append(pl.BlockSpec((1, _row_tile(h), cols), lambda s, i, c: (s, i, 0)))
        out_shape.append(_sds((NSH, h, cols), bf16))
        out_specs.append(pl.BlockSpec((1, _row_tile(h), cols), lambda s, i, c: (s, i, 0)))
    return pl.pallas_call(
        body, out_shape=out_shape,
        grid_spec=pltpu.PrefetchScalarGridSpec(num_scalar_prefetch=1, grid=(NSH, 2), in_specs=in_specs, out_specs=out_specs),
        name="pair_sum", compiler_params=_cp("parallel", "parallel"))(c_idx, *grads, *got)


def _chip_sum(psum, parts, full, place, l, name):
    n = len(parts)

    def body(p_ref, *refs):
        s = pl.program_id(1)
        for t in range(n):
            val = jnp.where(s == p_ref[1], refs[t][0], refs[n + t][0]).astype(f32)
            out = refs[3 * n + t]

            @pl.when(s == 0)
            def _(out=out, val=val):
                out[0, 0] = val

            @pl.when(s != 0)
            def _(out=out, val=val):
                out[0, 0] += val

    own_specs, part_specs, out_shape, out_specs = [], [], [], []
    for a, fl in zip(parts, full):
        _, h, cols = a.shape
        r = _row_tile(h)
        own_specs.append(pl.BlockSpec((1, r, cols), lambda i, s, p: (p[1], i, 0)))
        part_specs.append(pl.BlockSpec((1, r, cols), lambda i, s, p: (jnp.where(s == p[1], (s + 1) % NSH, s), i, 0)))
        out_shape.append(_sds(fl.shape, f32))
        out_specs.append(pl.BlockSpec((1, 1, r, cols), lambda i, s, p, l=l: (l, p[0], i, 0)))
    return pl.pallas_call(
        body, out_shape=out_shape,
        grid_spec=pltpu.PrefetchScalarGridSpec(num_scalar_prefetch=1, grid=(2, NSH), in_specs=own_specs + part_specs + [ANY] * n,
                                               out_specs=out_specs),
        input_output_aliases={1 + 2 * n + t: t for t in range(n)}, name=name,
        compiler_params=_cp("parallel", "arbitrary"))(place, *psum, *parts, *full)


def _sibling_share(full, l, name):
    n = len(full)

    def body(*refs):
        outs = refs[n:2 * n]
        send_sems, recv_sems = refs[2 * n:]
        x, y, c, _ = _place()
        sibling = (x, y, 1 - c)
        cps = []
        for t in range(n):
            blk = outs[t].at[l, c]
            cp = pltpu.make_async_remote_copy(
                src_ref=blk, dst_ref=blk, send_sem=send_sems.at[t], recv_sem=recv_sems.at[t], device_id=sibling, device_id_type=MESH)
            cp.start()
            cps.append(cp)
        for t in range(n):
            blk = outs[t].at[l, 1 - c]
            pltpu.make_async_remote_copy(
                src_ref=blk, dst_ref=blk, send_sem=send_sems.at[t], recv_sem=recv_sems.at[t],
                device_id=sibling, device_id_type=MESH).wait_recv()
        for cp in cps:
            cp.wait_send()

    out_shape = [_sds(a.shape, f32) for a in full]
    return pl.pallas_call(
        body, out_shape=out_shape, in_specs=[ANY] * n, out_specs=[ANY] * n, input_output_aliases={t: t for t in range(n)},
        scratch_shapes=[pltpu.SemaphoreType.DMA((n,)), pltpu.SemaphoreType.DMA((n,))], name=name)(*full)


def _share_copies(refs, l, send_sems, recv_sems):
    x, y, c, _ = _place()
    out = []
    for t, ref in enumerate(refs):
        mine, theirs = ref.at[l, c], ref.at[l, 1 - c]
        send = pltpu.make_async_remote_copy(src_ref=mine, dst_ref=mine, send_sem=send_sems.at[t], recv_sem=recv_sems.at[t],
                                            device_id=(x, y, 1 - c), device_id_type=MESH)
        land = pltpu.make_async_remote_copy(src_ref=mine, dst_ref=theirs, send_sem=send_sems.at[t], recv_sem=recv_sems.at[t],
                                            device_id=(x, y, 1 - c), device_id_type=MESH)
        out.append((send, land))
    return out


def _share_start(full, l, after, name):
    n = len(full)

    def body(*refs):
        for send, _ in _share_copies(refs[:n], l, refs[n + 1], refs[n + 2]):
            send.start()
        refs[-1][...] = jnp.zeros_like(refs[-1])

    out_shape = ([pltpu.SemaphoreType.DMA((n,)), pltpu.SemaphoreType.DMA((n,))] + [pltpu.HBM(a.shape, a.dtype) for a in full]
                 + [_sds((8, 128), f32)])
    res = pl.pallas_call(
        body, out_shape=out_shape, in_specs=[HBM] * n + [ANY], out_specs=[SEM, SEM] + [HBM] * n + [pl.BlockSpec(memory_space=pltpu.VMEM)],
        input_output_aliases={i: 2 + i for i in range(n)}, name=name, compiler_params=_SPLIT)(*_hbm(full), after)
    return res[0], res[1], list(res[2:2 + n])


def _share_wait(send_sems, recv_sems, full, l, after, name):
    n = len(full)

    def body(*refs):
        for send, land in _share_copies(refs[:n], l, refs[n], refs[n + 1]):
            send.wait_send()
            land.wait_recv()

    res = pl.pallas_call(
        body, out_shape=[pltpu.HBM(a.shape, a.dtype) for a in full], in_specs=[HBM] * n + [SEM, SEM, ANY], out_specs=[HBM] * n,
        input_output_aliases={i: i for i in range(n)}, name=name, compiler_params=_SPLIT)(*full, send_sems, recv_sems, after)
    return list(res)


SMALL_ROWS = 656


def _pack_small(per_layer, final_vec, loss_tile):
    rows = []
    for l in range(DEPTH):
        for k in ("ffn1_norm", "mix_norm", "ffn2_norm"):
            rows.append(per_layer[k][l].reshape(8, 128))
        rows.append(per_layer["pool_w"][l].reshape(128, 128))
        rows.append(jnp.pad(per_layer["pool_scale"][l].reshape(2, 128), ((0, 6), (0, 0))))
    rows.append(final_vec.reshape(8, 128))
    rows.append(loss_tile)
    return jnp.concatenate(rows, axis=0)


def _unpack_small(buf):
    out = {k: [] for k in ("ffn1_norm", "mix_norm", "ffn2_norm", "pool_w", "pool_scale")}
    r = 0
    for l in range(DEPTH):
        for k in ("ffn1_norm", "mix_norm", "ffn2_norm"):
            out[k].append(buf[r:r + 8].reshape(D))
            r += 8
        out["pool_w"].append(buf[r:r + 128].reshape(4, 64, 64))
        r += 128
        out["pool_scale"].append(buf[r:r + 2].reshape(PW))
        r += 8
    res = {k: jnp.stack(v) for k, v in out.items()}
    res["final_norm"] = buf[r:r + 8].reshape(D)
    res["loss"] = buf[r + 8, 0]
    return res


def _allreduce_small(buf):
    def body(in_ref, out_ref, slots, send_sems, recv_sems):
        x, y, c, _ = _place()
        me = 4 * x + 2 * y + c
        slots[me] = in_ref[...]
        peers = []
        for k in range(1, 8):
            px, py, pc = x ^ (k >> 2), y ^ ((k >> 1) & 1), c ^ (k & 1)
            cp = pltpu.make_async_remote_copy(
                src_ref=in_ref, dst_ref=slots.at[me], send_sem=send_sems.at[k - 1], recv_sem=recv_sems.at[k - 1],
                device_id=(px, py, pc), device_id_type=MESH)
            cp.start()
            peers.append(cp)
        for k in range(1, 8):
            px, py, pc = x ^ (k >> 2), y ^ ((k >> 1) & 1), c ^ (k & 1)
            slot = 4 * px + 2 * py + pc
            pltpu.make_async_remote_copy(
                src_ref=slots.at[slot], dst_ref=slots.at[slot], send_sem=send_sems.at[k - 1], recv_sem=recv_sems.at[k - 1],
                device_id=(px, py, pc), device_id_type=MESH).wait_recv()
        for cp in peers:
            cp.wait_send()
        acc = slots[0]
        for j in range(1, 8):
            acc = acc + slots[j]
        out_ref[...] = acc

    return pl.pallas_call(
        body, out_shape=_sds((SMALL_ROWS, 128), f32),
        in_specs=[pl.BlockSpec(memory_space=pltpu.VMEM)], out_specs=pl.BlockSpec(memory_space=pltpu.VMEM),
        scratch_shapes=[pltpu.VMEM((8, SMALL_ROWS, 128), f32), pltpu.SemaphoreType.DMA((7,)), pltpu.SemaphoreType.DMA((7,))],
        name="allreduce_small", compiler_params=_cp())(buf)


def _adamw_math(w, g, m, v):
    m = ADAM_B1 * m + (1.0 - ADAM_B1) * g
    v = ADAM_B2 * v + (1.0 - ADAM_B2) * (g * g)
    m_hat = m / (1.0 - ADAM_B1 ** ADAM_STEP)
    v_hat = v / (1.0 - ADAM_B2 ** ADAM_STEP)
    return -ADAM_LR * (m_hat / (jnp.sqrt(v_hat) + ADAM_EPS) + ADAM_WD * w), m, v


def _adamw(w, g, m, v, name, first=0, prev=None, dep=None):
    def body(w_ref, g_ref, m_ref, v_ref, *rest):
        go_ref, d_ref, mo_ref, vo_ref = rest[-4:]
        g = g_ref[...]
        d, mn, vn = _adamw_math(w_ref[...], g, m_ref[...], v_ref[...])
        go_ref[...] = g
        d_ref[...] = d
        mo_ref[...] = mn
        vo_ref[...] = vn

    _, rows, cols = w.shape
    r = rows // 4 if rows % 32 == 0 else rows
    spec = pl.BlockSpec((1, r, cols), lambda i, j: (first + i, j, 0))
    gspec = pl.BlockSpec((1, r, cols), lambda i, j: (i, j, 0))
    out = _sds(w.shape, f32)
    extra = [] if prev is None else list(prev)
    dspec, dop = _dep(dep)
    return pl.pallas_call(
        body, out_shape=(out, out, out, out), grid=(g.shape[0], rows // r), in_specs=[spec, gspec, spec, spec] + [ANY] * len(extra) + dspec,
        out_specs=(spec,) * 4, input_output_aliases={4 + i: i for i in range(len(extra))}, name=name,
        compiler_params=_cp("parallel", "parallel"))(w, g, m, v, *extra, *dop)


SMALL_NAMES = ("ffn1_norm", "mix_norm", "pool_w", "pool_scale", "ffn2_norm", "final_norm")
WEIGHT_ORDER = ("ffn1_norm", "ffn1_w_gate", "ffn1_w_up", "ffn1_w_down", "mix_norm", "w_in", "pool_w", "pool_scale", "w_out",
                "ffn2_norm", "ffn2_w_gate", "ffn2_w_up", "ffn2_w_down", "final_norm")


def _pack_small_params(p):
    per_layer = {k: [p[k][l] for l in range(DEPTH)] for k in ("ffn1_norm", "mix_norm", "ffn2_norm", "pool_w", "pool_scale")}
    return _pack_small(per_layer, p["final_norm"], jnp.zeros((8, 128), f32))


def kernel(x, positions, ffn1_norm, ffn1_w_gate, ffn1_w_up, ffn1_w_down, mix_norm, w_in, pool_w, pool_scale, w_out, ffn2_norm, ffn2_w_gate, ffn2_w_up, ffn2_w_down, final_norm, loss_target, m_ffn1_norm, m_ffn1_w_gate, m_ffn1_w_up, m_ffn1_w_down, m_mix_norm, m_w_in, m_pool_w, m_pool_scale, m_w_out, m_ffn2_norm, m_ffn2_w_gate, m_ffn2_w_up, m_ffn2_w_down, m_final_norm, v_ffn1_norm, v_ffn1_w_gate, v_ffn1_w_up, v_ffn1_w_down, v_mix_norm, v_w_in, v_pool_w, v_pool_scale, v_w_out, v_ffn2_norm, v_ffn2_w_gate, v_ffn2_w_up, v_ffn2_w_down, v_final_norm):
    params = dict(ffn1_norm=ffn1_norm, ffn1_w_gate=ffn1_w_gate, ffn1_w_up=ffn1_w_up, ffn1_w_down=ffn1_w_down,
                  mix_norm=mix_norm, w_in=w_in, pool_w=pool_w, pool_scale=pool_scale, w_out=w_out, ffn2_norm=ffn2_norm,
                  ffn2_w_gate=ffn2_w_gate, ffn2_w_up=ffn2_w_up, ffn2_w_down=ffn2_w_down, final_norm=final_norm)
    mom_m = dict(ffn1_norm=m_ffn1_norm, ffn1_w_gate=m_ffn1_w_gate, ffn1_w_up=m_ffn1_w_up, ffn1_w_down=m_ffn1_w_down,
                 mix_norm=m_mix_norm, w_in=m_w_in, pool_w=m_pool_w, pool_scale=m_pool_scale, w_out=m_w_out,
                 ffn2_norm=m_ffn2_norm, ffn2_w_gate=m_ffn2_w_gate, ffn2_w_up=m_ffn2_w_up, ffn2_w_down=m_ffn2_w_down,
                 final_norm=m_final_norm)
    mom_v = dict(ffn1_norm=v_ffn1_norm, ffn1_w_gate=v_ffn1_w_gate, ffn1_w_up=v_ffn1_w_up, ffn1_w_down=v_ffn1_w_down,
                 mix_norm=v_mix_norm, w_in=v_w_in, pool_w=v_pool_w, pool_scale=v_pool_scale, w_out=v_w_out,
                 ffn2_norm=v_ffn2_norm, ffn2_w_gate=v_ffn2_w_gate, ffn2_w_up=v_ffn2_w_up, ffn2_w_down=v_ffn2_w_down,
                 final_norm=v_final_norm)
    names = [t[0] for t in BIG]
    for d in (params, mom_m, mom_v):
        for k in TRANSPOSED:
            d[k] = jnp.swapaxes(d[k], 1, 2)

    place = jnp.stack([lax.axis_index("c"), 2 * lax.axis_index("x") + lax.axis_index("y")]).astype(jnp.int32)
    def gather_start(tag, cast, after):
        return _exchange_start(cast, [], after, _own_half, _slot_half, f"gather_start{tag}")

    def gather_end(started, after, tag, spec):
        send_sems, recv_sems, bufs, _, _ = started
        bufs, _ = _exchange_wait(send_sems, recv_sems, bufs, [], after, _own_half, _slot_half, f"gather_wait{tag}")
        return {nm: a.reshape(NSH, rows, cols) for (nm, rows, cols), a in zip(spec, _gather_forward(bufs))}

    tabs = _rope_tables(*_rope_inputs(positions))
    h = x.reshape(S, D)
    weights, saved = [], []
    cast0 = _cast_layer(params, 0, place)
    first = gather_start("0a", cast0[:FFN1], place)
    second = gather_start("0b", cast0[FFN1:], first[-1])
    after = second[-1]
    casts, started = {}, {}
    for l in range(1, DEPTH):
        casts[l] = _cast_layer(params, l, place, after)
        after = casts[l][0]
        if l == 1:
            started[1] = gather_start(1, casts[1], second[-1])
            after = started[1][-1]
    for l in range(DEPTH):
        if l == 0:
            w = gather_end(first, after, "0a", BIG[:FFN1])
            rest = lambda x1: gather_end(second, x1, "0b", BIG[FFN1:])
        else:
            w, rest = gather_end(started[l], after, l, BIG), None
        dep = None
        if l + 2 < DEPTH:
            started[l + 2] = gather_start(l + 2, casts[l + 2], w["g1"])
            dep = started[l + 2][-1]
        h, sv, w = _layer_fwd(h, w, params, l, tabs, dep, rest)
        weights.append(w)
        saved.append(sv)
        after = h
    loss, dx, dgf = _final_loss(h, final_norm.reshape(1, D), loss_target.reshape(S, D))

    upper = [lax.empty((DEPTH - 1, 2, rows // 2, cols), f32) for _, rows, cols in BIG]
    lower = [lax.empty((1, 2, rows // 2, cols), f32) for _, rows, cols in BIG]
    sg = {k: [None] * DEPTH for k in ("ffn1_norm", "mix_norm", "pool_w", "pool_scale", "ffn2_norm")}
    sg["final_norm"] = dgf

    def reduce_end(started, after, l, full, slot):
        send_sems, recv_sems, psum, parts, _ = started
        psum, parts = _exchange_wait(send_sems, recv_sems, psum, parts, after, _slot, _slot, f"grad_wait{l}")
        return _chip_sum(psum, parts, full, place, slot, f"chip_sum{l}")

    started, dep, shares = None, None, []
    for l in reversed(range(DEPTH)):
        dx, gr, sgl = _layer_bwd(dx, weights[l], saved[l], tabs, dep)
        for k, v in sgl.items():
            sg[k][l] = v
        if started is not None:
            send_sems, recv_sems, upper = _share_start(reduce_end(started, dx, l + 1, upper, l), l, place, f"share_start{l + 1}")
            shares.append((send_sems, recv_sems, l, l + 1))
        grads = [gr[nm] for nm in names]
        psum = _pair_sum(grads, _sibling_swap(grads), place)
        parts = [lax.empty(a.shape, bf16) for a in psum]
        started = _exchange_start(psum, parts, place, _slot, _slot, f"grad_start{l}")
        dep = started[-1]

    big_out = {}
    for send_sems, recv_sems, slot, l in shares:
        upper = _share_wait(send_sems, recv_sems, upper, slot, place, f"share_wait{l}")
    for (nm, rows, cols), g in zip(BIG, upper):
        k = BIG_SRC[nm]
        big_out[k] = _adamw(params[k], g.reshape(DEPTH - 1, rows, cols), mom_m[k], mom_v[k], "adamw_upper_" + k, first=1, dep=dep)
        dep = big_out[k][1]
    lower = _sibling_share(reduce_end(started, dep, 0, lower, 0), 0, "sibling_share0")
    for (nm, rows, cols), g in zip(BIG, lower):
        k = BIG_SRC[nm]
        big_out[k] = _adamw(params[k], g.reshape(1, rows, cols), mom_m[k], mom_v[k], "adamw_lower_" + k, first=0, prev=big_out[k])

    per_layer = {k: sg[k] for k in ("ffn1_norm", "mix_norm", "ffn2_norm", "pool_w", "pool_scale")}
    small_sum = _allreduce_small(_pack_small(per_layer, sg["final_norm"], loss))
    gs, ds_, ms, vs = _adamw(_pack_small_params(params).reshape(1, SMALL_ROWS, 128), small_sum.reshape(1, SMALL_ROWS, 128),
                             _pack_small_params(mom_m).reshape(1, SMALL_ROWS, 128),
                             _pack_small_params(mom_v).reshape(1, SMALL_ROWS, 128), "adamw_small")
    small_out = [_unpack_small(a.reshape(SMALL_ROWS, 128)) for a in (gs, ds_, ms, vs)]

    grad, delta, new_m, new_v = {}, {}, {}, {}
    for k in WEIGHT_ORDER:
        if k in SMALL_NAMES:
            grad[k], delta[k], new_m[k], new_v[k] = (so[k] for so in small_out)
        else:
            grad[k], delta[k], new_m[k], new_v[k] = big_out[k]
    for d in (grad, delta, new_m, new_v):
        for k in TRANSPOSED:
            d[k] = jnp.swapaxes(d[k], 1, 2)
    return (small_out[0]["loss"], dx.reshape(1, S, D), *[grad[k] for k in WEIGHT_ORDER], *[delta[k] for k in WEIGHT_ORDER],
            *[new_m[k] for k in WEIGHT_ORDER], *[new_v[k] for k in WEIGHT_ORDER])
```

```python
import functools

import jax
import jax.numpy as jnp
import numpy as np
from jax import lax
from jax.experimental import pallas as pl
from jax.experimental.pallas import tpu as pltpu

f32 = jnp.float32
bf16 = jnp.bfloat16

S = 2048
D = 1024
DEPTH = 4
NSH = 4
FS = 704
PROJ = 2560
PS = 640
PW = 256
AW = 768
NPAIR = 6
NORM_EPS = 1e-6
MASK_VALUE = -1e30
ROPE_THETA = 500000.0
DILATIONS = (1, 4, 16)
QBLK = 128
NBLK = S // QBLK
TM = 512
VMEM_LIMIT = 56 * 1024 * 1024

ADAM_LR = 0.001
ADAM_B1 = 0.9
ADAM_B2 = 0.999
ADAM_EPS = 1e-08
ADAM_WD = 0.01
ADAM_STEP = 10

MESH = pl.DeviceIdType.MESH
ANY = pl.BlockSpec(memory_space=pl.ANY)

BIG = (("g1", FS, D), ("u1", FS, D), ("d1", FS, D), ("wi", D, PS), ("wo", PW, D), ("g2", FS, D), ("u2", FS, D), ("d2", FS, D))
TRANSPOSED = ("ffn1_w_gate", "ffn1_w_up", "ffn2_w_gate", "ffn2_w_up")
FFN1 = 3
BIG_SRC = {"g1": "ffn1_w_gate", "u1": "ffn1_w_up", "d1": "ffn1_w_down", "wi": "w_in", "wo": "w_out",
           "g2": "ffn2_w_gate", "u2": "ffn2_w_up", "d2": "ffn2_w_down"}


def _cp(*sem):
    return pltpu.CompilerParams(dimension_semantics=sem if sem else None, vmem_limit_bytes=VMEM_LIMIT)


def _sds(shape, dt):
    return jax.ShapeDtypeStruct(shape, dt)


def _dot(a, b):
    return jnp.dot(a, b, preferred_element_type=f32)


def _dot_nt(a, b):
    return lax.dot_general(a, b, (((1,), (1,)), ((), ())), preferred_element_type=f32)


def _dot_tn(a, b):
    return lax.dot_general(a, b, (((0,), (0,)), ((), ())), preferred_element_type=f32)


def _dep(dep):
    return ([], []) if dep is None else ([ANY], [dep])


def _resident(shape):
    return pl.BlockSpec(shape, lambda i: (0,) * len(shape), pipeline_mode=pl.Buffered(1))


def _ffn_fwd(x, g, wg, wu, wd, dep=None):
    def body(x_ref, g_ref, wg_ref, wu_ref, wd_ref, *rest):
        xo_ref, h_ref, a_ref, b_ref = rest[-4:]
        xf = x_ref[...]
        r = lax.rsqrt(jnp.mean(xf * xf, axis=-1, keepdims=True) + NORM_EPS)
        hh = ((xf * r) * g_ref[...]).astype(bf16)
        h_ref[...] = hh
        acc = None
        for s in range(NSH):
            a = _dot_nt(hh, wg_ref[s])
            b = _dot_nt(hh, wu_ref[s])
            a_ref[s] = a.astype(bf16)
            b_ref[s] = b.astype(bf16)
            p = _dot((a * (1.0 / (1.0 + jnp.exp(-a))) * b).astype(bf16), wd_ref[s])
            acc = p if acc is None else acc + p
        xo_ref[...] = xf + 0.5 * acc

    tok = pl.BlockSpec((TM, D), lambda i: (i, 0))
    hid = pl.BlockSpec((NSH, TM, FS), lambda i: (0, i, 0))
    wsp = _resident((NSH, FS, D))
    dspec, dop = _dep(dep)
    return pl.pallas_call(
        body, out_shape=(_sds((S, D), f32), _sds((S, D), bf16), _sds((NSH, S, FS), bf16), _sds((NSH, S, FS), bf16)),
        grid=(S // TM,), in_specs=[tok, pl.BlockSpec((1, D), lambda i: (0, 0)), wsp, wsp, wsp] + dspec,
        out_specs=(tok, tok, hid, hid), name="ffn_fwd", compiler_params=_cp("parallel"))(x, g, wg, wu, wd, *dop)


def _in_proj(x, g, wi):
    def body(x_ref, g_ref, w_ref, o_ref, h_ref):
        xf = x_ref[...]
        r = lax.rsqrt(jnp.mean(xf * xf, axis=-1, keepdims=True) + NORM_EPS)
        hh = ((xf * r) * g_ref[...]).astype(bf16)
        h_ref[...] = hh
        for s in range(NSH):
            o_ref[:, PS * s:PS * (s + 1)] = _dot(hh, w_ref[s])

    tok = pl.BlockSpec((TM, D), lambda i: (i, 0))
    return pl.pallas_call(
        body, out_shape=(_sds((S, PROJ), f32), _sds((S, D), bf16)), grid=(S // TM,),
        in_specs=[tok, pl.BlockSpec((1, D), lambda i: (0, 0)), _resident((NSH, D, PS))],
        out_specs=(pl.BlockSpec((TM, PROJ), lambda i: (i, 0)), tok), name="in_proj", compiler_params=_cp("parallel"))(x, g, wi)


def _out_proj(x, mixed, wo):
    def body(x_ref, m_ref, w_ref, o_ref):
        o_ref[...] = x_ref[...] + _dot(m_ref[...], w_ref[...].reshape(D, D))

    return pl.pallas_call(
        body, out_shape=_sds((S, D), f32), grid=(S // TM,),
        in_specs=[pl.BlockSpec((TM, D), lambda i: (i, 0)), pl.BlockSpec((TM, D), lambda i: (i, 0)),
                  pl.BlockSpec((NSH, PW, D), lambda i: (0, 0, 0))],
        out_specs=pl.BlockSpec((TM, D), lambda i: (i, 0)), name="out_proj", compiler_params=_cp("parallel"))(x, mixed, wo)


def _out_proj_bwd(dx, wo):
    def body(dx_ref, w_ref, o_ref):
        o_ref[...] = _dot_nt(dx_ref[...].astype(bf16), w_ref[...].reshape(D, D))

    return pl.pallas_call(
        body, out_shape=_sds((S, D), f32), grid=(S // TM,),
        in_specs=[pl.BlockSpec((TM, D), lambda i: (i, 0)), pl.BlockSpec((NSH, PW, D), lambda i: (0, 0, 0))],
        out_specs=pl.BlockSpec((TM, D), lambda i: (i, 0)), name="out_proj_bwd", compiler_params=_cp("parallel"))(dx, wo)


def _ffn_bwd_mid(dx, h, a, b, wd, dep=None):
    nt = S // TM

    def body(dx_ref, h_ref, a_ref, b_ref, wd_ref, *rest):
        da_ref, db_ref, dwd_ref, dwg_ref, dwu_ref, dy_s, u_s, da_s, db_s = rest[-9:]
        i = pl.program_id(1)
        rows = pl.ds(pl.multiple_of(i * TM, TM), TM)
        dy = (0.5 * dx_ref[...]).astype(bf16)
        dy_s[rows, :] = dy
        du = _dot_nt(dy, wd_ref[0])
        a = a_ref[0].astype(f32)
        b = b_ref[0].astype(f32)
        sig = 1.0 / (1.0 + jnp.exp(-a))
        silu = a * sig
        da = (du * b * (sig * (1.0 + a * (1.0 - sig)))).astype(bf16)
        db = (du * silu).astype(bf16)
        da_ref[0] = da
        db_ref[0] = db
        da_s[rows, :] = da
        db_s[rows, :] = db
        u_s[rows, :] = (silu * b).astype(bf16)

        @pl.when(i == nt - 1)
        def _():
            hh = h_ref[...]
            dwd_ref[...] = _dot_tn(u_s[...], dy_s[...]).astype(bf16).reshape(dwd_ref.shape)
            dwg_ref[...] = _dot_tn(da_s[...], hh).astype(bf16).reshape(dwg_ref.shape)
            dwu_ref[...] = _dot_tn(db_s[...], hh).astype(bf16).reshape(dwu_ref.shape)

    tok = pl.BlockSpec((TM, D), lambda s, i: (i, 0))
    hid = pl.BlockSpec((1, TM, FS), lambda s, i: (s, i, 0))
    wsp = pl.BlockSpec((1, 2, FS // 2, D), lambda s, i: (s, 0, 0, 0))
    hidden = _sds((NSH, S, FS), bf16)
    wgrad = _sds((NSH, 2, FS // 2, D), bf16)
    whole = pltpu.VMEM((S, FS), bf16)
    dspec, dop = _dep(dep)
    return pl.pallas_call(
        body, out_shape=(hidden, hidden, wgrad, wgrad, wgrad), grid=(NSH, nt),
        in_specs=[tok, pl.BlockSpec((S, D), lambda s, i: (0, 0), pipeline_mode=pl.Buffered(1)), hid, hid,
                  pl.BlockSpec((1, FS, D), lambda s, i: (s, 0, 0))] + dspec,
        out_specs=(hid, hid, wsp, wsp, wsp), scratch_shapes=[pltpu.VMEM((S, D), bf16), whole, whole, whole],
        name="ffn_bwd_mid", compiler_params=_cp("parallel", "arbitrary"))(dx, h, a, b, wd, *dop)


def _norm_bwd_tail(acc, x_ref, dxin_ref, g_ref, dxo_ref, dg_ref, first):
    xf = x_ref[...]
    r = lax.rsqrt(jnp.mean(xf * xf, axis=-1, keepdims=True) + NORM_EPS)
    xhat = xf * r
    dhg = acc * g_ref[...]
    dxo_ref[...] = dxin_ref[...] + r * (dhg - xhat * jnp.mean(dhg * xhat, axis=-1, keepdims=True))
    part = jnp.sum(acc * xhat, axis=0, keepdims=True)

    @pl.when(first)
    def _():
        dg_ref[...] = part

    @pl.when(jnp.logical_not(first))
    def _():
        dg_ref[...] += part


def _ffn_bwd_dx(dx, x_in, g, da, db, wg, wu):
    def body(dx_ref, x_ref, g_ref, da_ref, db_ref, wg_ref, wu_ref, dxo_ref, dg_ref):
        acc = None
        for s in range(NSH):
            p = _dot(da_ref[s], wg_ref[s])
            acc = p if acc is None else acc + p
            acc = acc + _dot(db_ref[s], wu_ref[s])
        _norm_bwd_tail(acc, x_ref, dx_ref, g_ref, dxo_ref, dg_ref, pl.program_id(0) == 0)

    tok = pl.BlockSpec((TM, D), lambda i: (i, 0))
    vec = pl.BlockSpec((1, D), lambda i: (0, 0))
    hid = pl.BlockSpec((NSH, TM, FS), lambda i: (0, i, 0))
    wsp = _resident((NSH, FS, D))
    return pl.pallas_call(
        body, out_shape=(_sds((S, D), f32), _sds((1, D), f32)), grid=(S // TM,),
        in_specs=[tok, tok, vec, hid, hid, wsp, wsp], out_specs=(tok, vec),
        name="ffn_bwd_dx", compiler_params=_cp("arbitrary"))(dx, x_in, g, da, db, wg, wu)


def _in_proj_bwd_dx(dx, x_in, g, dproj, wi):
    def body(dx_ref, x_ref, g_ref, dp_ref, w_ref, dxo_ref, dg_ref):
        acc = None
        for s in range(NSH):
            p = _dot_nt(dp_ref[:, PS * s:PS * (s + 1)], w_ref[s])
            acc = p if acc is None else acc + p
        _norm_bwd_tail(acc, x_ref, dx_ref, g_ref, dxo_ref, dg_ref, pl.program_id(0) == 0)

    tok = pl.BlockSpec((TM, D), lambda i: (i, 0))
    vec = pl.BlockSpec((1, D), lambda i: (0, 0))
    return pl.pallas_call(
        body, out_shape=(_sds((S, D), f32), _sds((1, D), f32)), grid=(S // TM,),
        in_specs=[tok, tok, vec, pl.BlockSpec((TM, PROJ), lambda i: (i, 0)), _resident((NSH, D, PS))], out_specs=(tok, vec),
        name="in_proj_bwd_dx", compiler_params=_cp("arbitrary"))(dx, x_in, g, dproj, wi)


def _dw(lhs, rhs, lhs_spec, rhs_spec, rows, cols, name, cast_rhs=False):
    def body(l_ref, r_ref, o_ref):
        r = r_ref[...].astype(bf16) if cast_rhs else r_ref[...]
        o_ref[...] = _dot_tn(l_ref[...], r).astype(bf16).reshape(1, 2, rows // 2, cols)

    return pl.pallas_call(
        body, out_shape=_sds((NSH, 2, rows // 2, cols), bf16), grid=(NSH,), in_specs=[lhs_spec, rhs_spec],
        out_specs=pl.BlockSpec((1, 2, rows // 2, cols), lambda s: (s, 0, 0, 0)), name=name, compiler_params=_cp("parallel"))(lhs, rhs)


_WHOLE_TOK = pl.BlockSpec((S, D), lambda s: (0, 0))


def _dw_in(h, dproj):
    return _dw(h, dproj, _WHOLE_TOK, pl.BlockSpec((S, PS), lambda s: (0, s)), D, PS, "dw_in")


def _dw_out(mixed, dx):
    return _dw(mixed, dx, pl.BlockSpec((S, PW), lambda s: (0, s)), _WHOLE_TOK, PW, D, "dw_out", cast_rhs=True)


def _final_loss(x, g, target):
    def body(x_ref, g_ref, t_ref, loss_ref, dx_ref, dg_ref):
        i = pl.program_id(0)
        xf = x_ref[...]
        r = lax.rsqrt(jnp.mean(xf * xf, axis=-1, keepdims=True) + NORM_EPS)
        xhat = xf * r
        err = xhat * g_ref[...] - t_ref[...]
        dy = err * (1.0 / D)
        dhg = dy * g_ref[...]
        dx_ref[...] = r * (dhg - xhat * jnp.mean(dhg * xhat, axis=-1, keepdims=True))
        part = jnp.sum(dy * xhat, axis=0, keepdims=True)
        lpart = jnp.zeros((8, 128), f32) + 0.5 * jnp.sum(jnp.mean(err * err, axis=-1, keepdims=True))

        @pl.when(i == 0)
        def _():
            dg_ref[...] = part
            loss_ref[...] = lpart

        @pl.when(i != 0)
        def _():
            dg_ref[...] += part
            loss_ref[...] += lpart

    tok = pl.BlockSpec((TM, D), lambda i: (i, 0))
    vec = pl.BlockSpec((1, D), lambda i: (0, 0))
    return pl.pallas_call(
        body, out_shape=(_sds((8, 128), f32), _sds((S, D), f32), _sds((1, D), f32)), grid=(S // TM,),
        in_specs=[tok, vec, tok], out_specs=(pl.BlockSpec((8, 128), lambda i: (0, 0)), tok, vec),
        name="final_loss", compiler_params=_cp("arbitrary"))(x, g, target)


def _shift_down(x, k, row):
    return jnp.where(row >= k, pltpu.roll(x, k, axis=0), 0.0)


def _shift_up(x, k, row):
    return jnp.where(row < S - k, pltpu.roll(x, S - k, axis=0), 0.0)


def _pool_geometry():
    row = lax.broadcasted_iota(jnp.int32, (S, PW), 0)
    grp = lax.broadcasted_iota(jnp.int32, (S, PW), 1) // 64
    half = jnp.where(grp == 0, 1, jnp.where(grp == 1, 2, jnp.where(grp == 2, 4, 8)))
    hi = jnp.minimum(row + half - 1, S - 1)
    lo = jnp.maximum(row - half, 0)
    return row, grp, (hi - lo + 1).astype(f32)


def _by_group(grp, v0, v1, v2, v3):
    return jnp.where(grp == 0, v0, jnp.where(grp == 1, v1, jnp.where(grp == 2, v2, v3)))


def _window_sums(x, row, grp, transpose):
    l1, r1 = x, x
    l2, r2 = l1 + _shift_down(l1, 1, row), r1 + _shift_up(r1, 1, row)
    l4, r4 = l2 + _shift_down(l2, 2, row), r2 + _shift_up(r2, 2, row)
    l8, r8 = l4 + _shift_down(l4, 4, row), r4 + _shift_up(r4, 4, row)
    lsel = _by_group(grp, l1, l2, l4, l8)
    rsel = _by_group(grp, r1, r2, r4, r8)
    if transpose:
        return lsel + _shift_up(rsel, 1, row)
    return _shift_down(lsel, 1, row) + rsel


def _pool_fwd(proj, wbd, scale):
    def body(v_ref, w_ref, sc_ref, mixed_ref, diff_ref):
        row, grp, cnt = _pool_geometry()
        v = v_ref[...]
        diff = (_window_sums(v, row, grp, False) / cnt - v).astype(bf16)
        diff_ref[...] = diff
        mixed_ref[...] = (_dot(diff, w_ref[...].astype(bf16)) * sc_ref[...]).astype(bf16)

    col = pl.BlockSpec((S, PW), lambda i: (0, 0))
    return pl.pallas_call(
        body, out_shape=(_sds((S, D), bf16), _sds((S, PW), bf16)), grid=(1,),
        in_specs=[col, pl.BlockSpec((PW, PW), lambda i: (0, 0)), pl.BlockSpec((1, PW), lambda i: (0, 0))],
        out_specs=(col, col), name="pool_fwd", compiler_params=_cp("arbitrary"))(proj, wbd, scale)


def _pool_bwd(dmixed, diff, wbd, scale, dproj):
    def body(dy_ref, diff_ref, w_ref, sc_ref, dproj_in, dv_ref, dw_ref, dsc_ref):
        del dproj_in
        row, grp, cnt = _pool_geometry()
        dy = dy_ref[...]
        diff = diff_ref[...]
        w = w_ref[...].astype(bf16)
        dsc_ref[...] = jnp.sum(dy * _dot(diff, w), axis=0, keepdims=True)
        dys = (dy * sc_ref[...]).astype(bf16)
        dw_ref[...] = _dot_tn(diff, dys)
        ddiff = _dot_nt(dys, w)
        dv_ref[...] = (_window_sums(ddiff / cnt, row, grp, True) - ddiff).astype(bf16)

    col = pl.BlockSpec((S, PW), lambda i: (0, 0))
    return pl.pallas_call(
        body, out_shape=(_sds((S, PROJ), bf16), _sds((PW, PW), f32), _sds((1, PW), f32)), grid=(1,),
        in_specs=[col, col, pl.BlockSpec((PW, PW), lambda i: (0, 0)), pl.BlockSpec((1, PW), lambda i: (0, 0)), ANY],
        out_specs=(col, pl.BlockSpec((PW, PW), lambda i: (0, 0)), pl.BlockSpec((1, PW), lambda i: (0, 0))),
        input_output_aliases={4: 0}, name="pool_bwd", compiler_params=_cp("arbitrary"))(dmixed, diff, wbd, scale, dproj)


def _rope_tables(pos_col, freq_row):
    def body(p_ref, f_ref, c_ref, a_ref, b_ref):
        ang = p_ref[...].astype(f32) * f_ref[...]
        l64 = lax.broadcasted_iota(jnp.int32, (S, 128), 1) % 64
        cos, sin = jnp.cos(ang), jnp.sin(ang)
        c_ref[...] = jnp.where(l64 < 16, cos, 1.0)
        a_ref[...] = jnp.where(l64 < 8, -sin, 0.0)
        b_ref[...] = jnp.where((l64 >= 8) & (l64 < 16), sin, 0.0)

    t = _sds((S, 128), f32)
    return pl.pallas_call(body, out_shape=(t, t, t), name="rope_tables", compiler_params=_cp())(pos_col, freq_row)


def _rope(t, c, a, b):
    return t * c + pltpu.roll(t, 120, axis=1) * a + pltpu.roll(t, 8, axis=1) * b


def _rope_bwd(g, c, a, b):
    return g * c + pltpu.roll(g * a, 8, axis=1) + pltpu.roll(g * b, 120, axis=1)


def _perm_load(ref, d):
    if d == 1:
        return ref[...]
    n = S // d
    return jnp.concatenate([ref[pl.ds(r, n, stride=d), :] for r in range(d)], axis=0)


def _unperm_store(ref, val, d):
    if d == 1:
        ref[...] = val
        return
    n = S // d
    for r in range(d):
        ref[pl.ds(r, n, stride=d), :] = val[r * n:(r + 1) * n, :]


def _band(xp, d):
    if d == NBLK:
        return xp.reshape(NBLK, QBLK, 128)
    z = jnp.zeros((64, 128), bf16)
    p = jnp.concatenate([z, xp, z], axis=0).reshape(NBLK + 1, QBLK, 128)
    return jnp.concatenate([p[:NBLK], p[1:]], axis=1)


def _unband(xb, d):
    if d == NBLK:
        return xb.reshape(S, 128)
    z = jnp.zeros((1, QBLK, 128), f32)
    p = jnp.concatenate([xb[:, :QBLK], z], axis=0) + jnp.concatenate([z, xb[:, QBLK:]], axis=0)
    return p.reshape(S + QBLK, 128)[64:S + 64]


def _band_mask(d):
    if d == NBLK:
        a = lax.broadcasted_iota(jnp.int32, (1, 2 * QBLK, QBLK), 1) & (QBLK - 1)
        b = lax.broadcasted_iota(jnp.int32, (1, 2 * QBLK, QBLK), 2)
        return (b >= a - 64) & (b <= a + 64)
    blocks_per_class = NBLK // d
    n = lax.broadcasted_iota(jnp.int32, (NBLK, 1, 2 * QBLK), 0) & (blocks_per_class - 1)
    be = lax.broadcasted_iota(jnp.int32, (NBLK, 1, 2 * QBLK), 2)
    a = lax.broadcasted_iota(jnp.int32, (1, 2 * QBLK, 2 * QBLK), 1) & (QBLK - 1)
    b = lax.broadcasted_iota(jnp.int32, (1, 2 * QBLK, 2 * QBLK), 2)
    band = (b >= a) & (b <= a + 128)
    edge = ((be >= 64) | (n != 0)) & ((be < QBLK + 64) | (n != blocks_per_class - 1))
    return band & edge


def _stack_heads(xb, lo):
    z = jnp.zeros_like(xb)
    return jnp.concatenate([jnp.where(lo, xb, z), jnp.where(lo, z, xb)], axis=1)


def _unstack_heads(x2, lo):
    return jnp.where(lo, x2[:, :QBLK], x2[:, QBLK:])


def _rows_to_lanes(col2, lo):
    return jnp.where(lo, jnp.broadcast_to(col2[:, :QBLK], (NBLK, QBLK, 128)), jnp.broadcast_to(col2[:, QBLK:], (NBLK, QBLK, 128)))


def _bmm_nt(a, b):
    return jnp.einsum('nqd,nkd->nqk', a, b, preferred_element_type=f32)


def _bmm_nn(a, b):
    return jnp.einsum('nqk,nkd->nqd', a, b, preferred_element_type=f32)


def _bmm_tn(a, b):
    return jnp.einsum('nqk,nqd->nkd', a, b, preferred_element_type=f32)


def _attn_fwd(proj, tc, ta, tb, mixed):
    def body(q_ref, k_ref, v_ref, c_ref, a_ref, b_ref, mixed_in, mixed_ref, o_ref, lse_ref, qn, kn, t_num, t_m, t_den):
        del mixed_in
        lo = lax.broadcasted_iota(jnp.int32, (1, 1, 128), 2) < 64
        c, a, b = c_ref[...], a_ref[...], b_ref[...]
        qn[...] = _rope(q_ref[...], c, a, b)
        kn[...] = _rope(k_ref[...], c, a, b)
        run = None
        for d in DILATIONS:
            q2 = _stack_heads(_perm_load(qn, d).astype(bf16).reshape(NBLK, QBLK, 128), lo)
            kb = _band(_perm_load(kn, d).astype(bf16), d)
            vb = _band(_perm_load(v_ref, d).astype(bf16), d)
            s = jnp.where(_band_mask(d), _bmm_nt(q2, kb) * 0.125, MASK_VALUE)
            m = jnp.max(s, axis=2, keepdims=True)
            p = jnp.exp(s - m)
            den = jnp.sum(p, axis=2, keepdims=True)
            num = _unstack_heads(_bmm_nn(p.astype(bf16), vb), lo)
            _unperm_store(t_num, num.reshape(S, 128), d)
            _unperm_store(t_m, _rows_to_lanes(m, lo).reshape(S, 128), d)
            _unperm_store(t_den, _rows_to_lanes(den, lo).reshape(S, 128), d)
            if run is None:
                run = (t_m[...], t_num[...], t_den[...])
            else:
                m_new = jnp.maximum(run[0], t_m[...])
                w_old, w_new = jnp.exp(run[0] - m_new), jnp.exp(t_m[...] - m_new)
                run = (m_new, w_old * run[1] + w_new * t_num[...], w_old * run[2] + w_new * t_den[...])
        out = run[1] / run[2]
        o_ref[...] = out
        mixed_ref[...] = out.astype(bf16)
        lse_ref[...] = run[0] + jnp.log(run[2])

    def col(off):
        return pl.BlockSpec((S, 128), lambda j, off=off: (0, off + j))

    tab = pl.BlockSpec((S, 128), lambda j: (0, 0))
    scr = pltpu.VMEM((S, 128), f32)
    return pl.pallas_call(
        body, out_shape=(_sds((S, D), bf16), _sds((S, AW), f32), _sds((S, AW), f32)), grid=(NPAIR,),
        in_specs=[col(2), col(8), col(14), tab, tab, tab, ANY], out_specs=(col(2), col(0), col(0)),
        scratch_shapes=[scr, scr, scr, scr, scr], input_output_aliases={6: 0}, name="attn_fwd",
        compiler_params=_cp("arbitrary"))(proj, proj, proj, tc, ta, tb, mixed)


def _attn_bwd(proj, tc, ta, tb, o, lse, dmixed):
    def body(q_ref, k_ref, v_ref, c_ref, a_ref, b_ref, o_ref, lse_ref, do_ref, dp_ref, qn, kn, tmp, dk_s, dv_s):
        t = pl.program_id(1)

        @pl.when(t == 0)
        def _():
            lo = lax.broadcasted_iota(jnp.int32, (1, 1, 128), 2) < 64
            c, a, b = c_ref[...], a_ref[...], b_ref[...]
            qn[...] = _rope(q_ref[...], c, a, b)
            kn[...] = _rope(k_ref[...], c, a, b)
            dq = dk = dv = None
            for d in DILATIONS:
                q2 = _stack_heads(_perm_load(qn, d).astype(bf16).reshape(NBLK, QBLK, 128), lo)
                kb = _band(_perm_load(kn, d).astype(bf16), d)
                vb = _band(_perm_load(v_ref, d).astype(bf16), d)
                dob = _perm_load(do_ref, d).reshape(NBLK, QBLK, 128)
                ob = _perm_load(o_ref, d).reshape(NBLK, QBLK, 128)
                lsb = _perm_load(lse_ref, d).reshape(NBLK, QBLK, 128)
                do2 = _stack_heads(dob.astype(bf16), lo)
                delta2 = jnp.sum(_stack_heads(dob * ob, lo), axis=2, keepdims=True)
                lse2 = jnp.max(jnp.concatenate([jnp.where(lo, lsb, MASK_VALUE), jnp.where(lo, MASK_VALUE, lsb)], axis=1),
                               axis=2, keepdims=True)
                s = _bmm_nt(q2, kb) * 0.125
                p = jnp.where(_band_mask(d), jnp.exp(s - lse2), 0.0)
                ds = (p * (_bmm_nt(do2, vb) - delta2) * 0.125).astype(bf16)
                pb = p.astype(bf16)
                dq_b = _unstack_heads(_bmm_nn(ds, kb), lo).reshape(S, 128)
                dk_b = _unband(_bmm_tn(ds, q2), d)
                dv_b = _unband(_bmm_tn(pb, do2), d)
                acc = []
                for prev, new in ((dq, dq_b), (dk, dk_b), (dv, dv_b)):
                    _unperm_store(tmp, new, d)
                    acc.append(tmp[...] if prev is None else prev + tmp[...])
                dq, dk, dv = acc
            dp_ref[...] = _rope_bwd(dq, c, a, b).astype(bf16)
            dk_s[...] = _rope_bwd(dk, c, a, b).astype(bf16)
            dv_s[...] = dv.astype(bf16)

        @pl.when(t == 1)
        def _():
            dp_ref[...] = dk_s[...]

        @pl.when(t == 2)
        def _():
            dp_ref[...] = dv_s[...]

    def col(off):
        return pl.BlockSpec((S, 128), lambda j, t, off=off: (0, off + j))

    tab = pl.BlockSpec((S, 128), lambda j, t: (0, 0))
    scr = pltpu.VMEM((S, 128), f32)
    scb = pltpu.VMEM((S, 128), bf16)
    return pl.pallas_call(
        body, out_shape=_sds((S, PROJ), bf16), grid=(NPAIR, 3),
        in_specs=[col(2), col(8), col(14), tab, tab, tab, col(0), col(0), col(2)],
        out_specs=pl.BlockSpec((S, 128), lambda j, t: (0, 2 + NPAIR * t + j)),
        scratch_shapes=[scr, scr, scr, scb, scb], name="attn_bwd",
        compiler_params=_cp("arbitrary", "arbitrary"))(proj, proj, proj, tc, ta, tb, o, lse, dmixed)


def _block_diag(w4):
    out = jnp.zeros((PW, PW), w4.dtype)
    for g in range(4):
        out = out.at[64 * g:64 * (g + 1), 64 * g:64 * (g + 1)].set(w4[g])
    return out


def _diag_blocks(w):
    return jnp.stack([w[64 * g:64 * (g + 1), 64 * g:64 * (g + 1)] for g in range(4)])


def _rope_inputs(positions):
    inv_freq = ROPE_THETA ** (-jnp.arange(0, 16, 2, dtype=f32) / 16)
    l64 = np.arange(128) % 64
    idx = np.where(l64 < 16, l64 % 8, 0)
    return positions.reshape(S, 1), inv_freq[idx].reshape(1, 128)


def _layer_fwd(x, w, small, l, tabs, dep=None, rest=None):
    g1, gm, g2 = (small[k][l].reshape(1, D) for k in ("ffn1_norm", "mix_norm", "ffn2_norm"))
    wbd = _block_diag(small["pool_w"][l])
    psc = small["pool_scale"][l].reshape(1, PW)
    x1, h1, a1, b1 = _ffn_fwd(x, g1, w["g1"], w["u1"], w["d1"], dep)
    if rest is not None:
        w = {**w, **rest(x1)}
    proj, h2 = _in_proj(x1, gm, w["wi"])
    mixed, diff = _pool_fwd(proj, wbd, psc)
    mixed, o, lse = _attn_fwd(proj, *tabs, mixed)
    x2 = _out_proj(x1, mixed, w["wo"])
    out, h3, a2, b2 = _ffn_fwd(x2, g2, w["g2"], w["u2"], w["d2"])
    return out, dict(x0=x, h1=h1, a1=a1, b1=b1, x1=x1, h2=h2, proj=proj, mixed=mixed, diff=diff, o=o, lse=lse,
                     x2=x2, h3=h3, a2=a2, b2=b2, g1=g1, gm=gm, g2=g2, wbd=wbd, psc=psc), w


def _layer_bwd(dx, w, sv, tabs, dep=None):
    gr, sg = {}, {}
    da, db, gr["d2"], gr["g2"], gr["u2"] = _ffn_bwd_mid(dx, sv["h3"], sv["a2"], sv["b2"], w["d2"], dep)
    dx, sg["ffn2_norm"] = _ffn_bwd_dx(dx, sv["x2"], sv["g2"], da, db, w["g2"], w["u2"])
    gr["wo"] = _dw_out(sv["mixed"], dx)
    dmixed = _out_proj_bwd(dx, w["wo"])
    dproj = _attn_bwd(sv["proj"], *tabs, sv["o"], sv["lse"], dmixed)
    dproj, dwbd, sg["pool_scale"] = _pool_bwd(dmixed, sv["diff"], sv["wbd"], sv["psc"], dproj)
    sg["pool_w"] = _diag_blocks(dwbd)
    gr["wi"] = _dw_in(sv["h2"], dproj)
    dx, sg["mix_norm"] = _in_proj_bwd_dx(dx, sv["x1"], sv["gm"], dproj, w["wi"])
    da, db, gr["d1"], gr["g1"], gr["u1"] = _ffn_bwd_mid(dx, sv["h1"], sv["a1"], sv["b1"], w["d1"])
    dx, sg["ffn1_norm"] = _ffn_bwd_dx(dx, sv["x0"], sv["g1"], da, db, w["g1"], w["u1"])
    return dx, gr, sg


def _forward_backward(x, positions, target, gathered, small):
    tabs = _rope_tables(*_rope_inputs(positions))
    saved = []
    for l in range(DEPTH):
        x, sv, _ = _layer_fwd(x, gathered[l], small, l, tabs)
        saved.append(sv)
    loss, dx, dgf = _final_loss(x, small["final_norm"].reshape(1, D), target)
    big = [None] * DEPTH
    sg = {k: [None] * DEPTH for k in ("ffn1_norm", "mix_norm", "pool_w", "pool_scale", "ffn2_norm")}
    for l in reversed(range(DEPTH)):
        dx, big[l], sgl = _layer_bwd(dx, gathered[l], saved[l], tabs)
        for k, v in sgl.items():
            sg[k][l] = v
    sg["final_norm"] = dgf
    return loss, dx, big, sg


def _place():
    x, y, c = lax.axis_index("x"), lax.axis_index("y"), lax.axis_index("c")
    chips = [(1 - x, y), (x, 1 - y), (1 - x, 1 - y)]
    return x, y, c, chips


def _cast_layer(params, l, place, dep=None):
    def body(p_ref, *refs):
        del p_ref
        for i_ref, o_ref in zip(refs[:8], refs[-8:]):
            o_ref[...] = i_ref[...].astype(bf16).reshape(o_ref.shape)

    ins, in_specs, out_shape, out_specs = [], [], [], []
    for name, rows, cols in BIG:
        q = rows // 4
        ins.append(params[BIG_SRC[name]])
        in_specs.append(pl.BlockSpec((1, q, cols), lambda i, p, l=l: (l, i, 0)))
        out_shape.append(_sds((NSH, 2, rows // 2, cols), bf16))
        out_specs.append(pl.BlockSpec((1, 1, q, cols), lambda i, p: (p[1], i // 2, i % 2, 0)))
    dspec, dop = _dep(dep)
    return pl.pallas_call(
        body, out_shape=out_shape,
        grid_spec=pltpu.PrefetchScalarGridSpec(num_scalar_prefetch=1, grid=(4,), in_specs=in_specs + dspec, out_specs=out_specs),
        name=f"cast_layer{l}", compiler_params=_cp("parallel"))(place, *ins, *dop)


HBM = pl.BlockSpec(memory_space=pltpu.HBM)
SEM = pl.BlockSpec(memory_space=pltpu.SEMAPHORE)
_SPLIT = pltpu.CompilerParams(has_side_effects=pltpu.SideEffectType.DATAFLOW_SIDE_EFFECTING)


def _hbm(arrays):
    return [pltpu.with_memory_space_constraint(a, pltpu.HBM) for a in arrays]


def _chip_copies(src_of, dst_of, send_sems, recv_sems, n):
    x, y, c, chips = _place()
    me = 2 * x + y
    out = []
    for t in range(n):
        for k, chip in enumerate(chips):
            peer = 2 * chip[0] + chip[1]
            send = pltpu.make_async_remote_copy(
                src_ref=src_of(t, peer), dst_ref=dst_of(t, me), send_sem=send_sems.at[3 * t + k], recv_sem=recv_sems.at[3 * t + k],
                device_id=(chip[0], chip[1], c), device_id_type=MESH)
            land = pltpu.make_async_remote_copy(
                src_ref=src_of(t, peer), dst_ref=dst_of(t, peer), send_sem=send_sems.at[3 * t + k], recv_sem=recv_sems.at[3 * t + k],
                device_id=(chip[0], chip[1], c), device_id_type=MESH)
            out.append((send, land))
    return out


def _exchange_start(src, land, after, src_of, dst_of, name):
    n, m = len(src), len(src) + len(land)

    def body(*refs):
        src_refs = refs[:n]
        land_refs = refs[n:m] if land else src_refs
        send_sems, recv_sems = refs[m + 1], refs[m + 2]
        token = refs[-1]
        for send, _ in _chip_copies(lambda t, s: src_of(src_refs[t], s), lambda t, s: dst_of(land_refs[t], s), send_sems, recv_sems, n):
            send.start()
        token[...] = jnp.zeros_like(token)

    arrays = list(src) + list(land)
    out_shape = ([pltpu.SemaphoreType.DMA((3 * n,)), pltpu.SemaphoreType.DMA((3 * n,))] + [pltpu.HBM(a.shape, a.dtype) for a in arrays]
                 + [_sds((8, 128), f32)])
    res = pl.pallas_call(
        body, out_shape=out_shape, in_specs=[HBM] * m + [ANY], out_specs=[SEM, SEM] + [HBM] * m + [pl.BlockSpec(memory_space=pltpu.VMEM)],
        input_output_aliases={i: 2 + i for i in range(m)}, name=name, compiler_params=_SPLIT)(*_hbm(arrays), after)
    return res[0], res[1], list(res[2:2 + n]), list(res[2 + n:2 + m]), res[-1]


def _exchange_wait(send_sems, recv_sems, src, land, after, src_of, dst_of, name):
    n, m = len(src), len(src) + len(land)

    def body(*refs):
        src_refs = refs[:n]
        land_refs = refs[n:m] if land else src_refs
        send_sems, recv_sems = refs[m], refs[m + 1]
        for send, land_cp in _chip_copies(lambda t, s: src_of(src_refs[t], s), lambda t, s: dst_of(land_refs[t], s), send_sems, recv_sems, n):
            send.wait_send()
            land_cp.wait_recv()

    arrays = list(src) + list(land)
    res = pl.pallas_call(
        body, out_shape=[pltpu.HBM(a.shape, a.dtype) for a in arrays], in_specs=[HBM] * m + [SEM, SEM, ANY], out_specs=[HBM] * m,
        input_output_aliases={i: i for i in range(m)}, name=name, compiler_params=_SPLIT)(*arrays, send_sems, recv_sems, after)
    return list(res[:n]), list(res[n:])


def _own_half(ref, s):
    x, y, c, _ = _place()
    return ref.at[2 * x + y, c]


def _slot_half(ref, s):
    return ref.at[s, lax.axis_index("c")]


def _slot(ref, s):
    return ref.at[s]


def _gather_forward(bufs):
    n = len(bufs)

    def body(*refs):
        outs = refs[n:2 * n]
        send_sems, recv_sems = refs[2 * n:]
        x, y, c, chips = _place()
        sibling = (x, y, 1 - c)
        passed = []
        for t in range(n):
            for k, chip in enumerate(chips):
                blk = outs[t].at[2 * chip[0] + chip[1], c]
                cp = pltpu.make_async_remote_copy(
                    src_ref=blk, dst_ref=blk, send_sem=send_sems.at[t, k], recv_sem=recv_sems.at[t, k],
                    device_id=sibling, device_id_type=MESH)
                cp.start()
                passed.append(cp)
        for t in range(n):
            for k, chip in enumerate(chips):
                blk = outs[t].at[2 * chip[0] + chip[1], 1 - c]
                pltpu.make_async_remote_copy(
                    src_ref=blk, dst_ref=blk, send_sem=send_sems.at[t, k], recv_sem=recv_sems.at[t, k],
                    device_id=sibling, device_id_type=MESH).wait_recv()
        for cp in passed:
            cp.wait_send()

    out_shape = [_sds(a.shape, bf16) for a in bufs]
    return pl.pallas_call(
        body, out_shape=out_shape, in_specs=[ANY] * n, out_specs=[ANY] * n, input_output_aliases={t: t for t in range(n)},
        scratch_shapes=[pltpu.SemaphoreType.DMA((n, 3)), pltpu.SemaphoreType.DMA((n, 3))], name="gather_forward")(*bufs)


def _sibling_swap(grads):
    n = len(grads)

    def body(*refs):
        ins, outs = refs[:n], refs[n:2 * n]
        send_sems, recv_sems = refs[2 * n:]
        x, y, c, _ = _place()
        cps = []
        for t in range(n):
            for s in range(NSH):
                cp = pltpu.make_async_remote_copy(
                    src_ref=ins[t].at[s, 1 - c], dst_ref=outs[t].at[s], send_sem=send_sems.at[t, s], recv_sem=recv_sems.at[t, s],
                    device_id=(x, y, 1 - c), device_id_type=MESH)
                cp.start()
                cps.append(cp)
        for cp in cps:
            cp.wait()

    out_shape = [_sds((NSH,) + a.shape[2:], bf16) for a in grads]
    return pl.pallas_call(
        body, out_shape=out_shape, in_specs=[ANY] * n, out_specs=[ANY] * n,
        scratch_shapes=[pltpu.SemaphoreType.DMA((n, NSH)), pltpu.SemaphoreType.DMA((n, NSH))],
        name="sibling_swap")(*grads)


def _swap_copies(ins, outs, send_sems, recv_sems):
    x, y, c, _ = _place()
    out = []
    for t, (src, dst) in enumerate(zip(ins, outs)):
        for s in range(NSH):
            out.append(pltpu.make_async_remote_copy(
                src_ref=src.at[s, 1 - c], dst_ref=dst.at[s], send_sem=send_sems.at[NSH * t + s], recv_sem=recv_sems.at[NSH * t + s],
                device_id=(x, y, 1 - c), device_id_type=MESH))
    return out


def _swap_start(grads, after, name):
    n = len(grads)
    got = [lax.empty((NSH,) + a.shape[2:], bf16) for a in grads]

    def body(*refs):
        for cp in _swap_copies(refs[:n], refs[n:2 * n], refs[2 * n + 1], refs[2 * n + 2]):
            cp.start()
        refs[-1][...] = jnp.zeros_like(refs[-1])

    arrays = list(grads) + got
    out_shape = ([pltpu.SemaphoreType.DMA((n * NSH,)), pltpu.SemaphoreType.DMA((n * NSH,))]
                 + [pltpu.HBM(a.shape, a.dtype) for a in arrays] + [_sds((8, 128), f32)])
    res = pl.pallas_call(
        body, out_shape=out_shape, in_specs=[HBM] * (2 * n) + [ANY],
        out_specs=[SEM, SEM] + [HBM] * (2 * n) + [pl.BlockSpec(memory_space=pltpu.VMEM)],
        input_output_aliases={i: 2 + i for i in range(2 * n)}, name=name, compiler_params=_SPLIT)(*_hbm(arrays), after)
    return res[0], res[1], list(res[2:2 + n]), list(res[2 + n:2 + 2 * n])


def _swap_wait(send_sems, recv_sems, grads, got, after, name):
    n = len(grads)

    def body(*refs):
        for cp in _swap_copies(refs[:n], refs[n:2 * n], refs[2 * n], refs[2 * n + 1]):
            cp.wait()

    arrays = list(grads) + list(got)
    res = pl.pallas_call(
        body, out_shape=[pltpu.HBM(a.shape, a.dtype) for a in arrays], in_specs=[HBM] * (2 * n) + [SEM, SEM, ANY], out_specs=[HBM] * (2 * n),
        input_output_aliases={i: i for i in range(2 * n)}, name=name, compiler_params=_SPLIT)(*arrays, send_sems, recv_sems, after)
    return list(res[:n]), list(res[n:])


def _row_tile(h):
    return h // 2 if h % 32 == 0 else h


def _pair_sum(grads, got, c_idx):
    n = len(grads)

    def body(c_ref, *refs):
        del c_ref
        for t in range(n):
            refs[2 * n + t][...] = (refs[t][...].astype(f32).reshape(refs[n + t].shape) + refs[n + t][...].astype(f32)).astype(bf16)

    in_specs, out_shape, out_specs = [], [], []
    for a in grads:
        h, cols = a.shape[2:]
        in_specs.append(pl.BlockSpec((1, 1, _row_tile(h), cols), lambda s, i, c: (s, c[0], i, 0)))
    for a in grads:
        h, cols = a.shape[2:]
        in_specs.append(pl.BlockSpec((1, _row_tile(h), cols), lambda s, i, c: (s, i, 0)))
        out_shape.append(_sds((NSH, h, cols), bf16))
        out_specs.append(pl.BlockSpec((1, _row_tile(h), cols), lambda s, i, c: (s, i, 0)))
    return pl.pallas_call(
        body, out_shape=out_shape,
        grid_spec=pltpu.PrefetchScalarGridSpec(num_scalar_prefetch=1, grid=(NSH, 2), in_specs=in_specs, out_specs=out_specs),
        name="pair_sum", compiler_params=_cp("parallel", "parallel"))(c_idx, *grads, *got)


def _chip_sum(psum, parts, full, place, l, name):
    n = len(parts)

    def body(p_ref, *refs):
        s = pl.program_id(1)
        for t in range(n):
            val = jnp.where(s == p_ref[1], refs[t][0], refs[n + t][0]).astype(f32)
            out = refs[3 * n + t]

            @pl.when(s == 0)
            def _(out=out, val=val):
                out[0, 0] = val

            @pl.when(s != 0)
            def _(out=out, val=val):
                out[0, 0] += val

    own_specs, part_specs, out_shape, out_specs = [], [], [], []
    for a, fl in zip(parts, full):
        _, h, cols = a.shape
        r = _row_tile(h)
        own_specs.append(pl.BlockSpec((1, r, cols), lambda i, s, p: (p[1], i, 0)))
        part_specs.append(pl.BlockSpec((1, r, cols), lambda i, s, p: (jnp.where(s == p[1], (s + 1) % NSH, s), i, 0)))
        out_shape.append(_sds(fl.shape, f32))
        out_specs.append(pl.BlockSpec((1, 1, r, cols), lambda i, s, p, l=l: (l, p[0], i, 0)))
    return pl.pallas_call(
        body, out_shape=out_shape,
        grid_spec=pltpu.PrefetchScalarGridSpec(num_scalar_prefetch=1, grid=(2, NSH), in_specs=own_specs + part_specs + [ANY] * n,
                                               out_specs=out_specs),
        input_output_aliases={1 + 2 * n + t: t for t in range(n)}, name=name,
        compiler_params=_cp("parallel", "arbitrary"))(place, *psum, *parts, *full)


def _sibling_share(full, l, name):
    n = len(full)

    def body(*refs):
        outs = refs[n:2 * n]
        send_sems, recv_sems = refs[2 * n:]
        x, y, c, _ = _place()
        sibling = (x, y, 1 - c)
        cps = []
        for t in range(n):
            blk = outs[t].at[l, c]
            cp = pltpu.make_async_remote_copy(
                src_ref=blk, dst_ref=blk, send_sem=send_sems.at[t], recv_sem=recv_sems.at[t], device_id=sibling, device_id_type=MESH)
            cp.start()
            cps.append(cp)
        for t in range(n):
            blk = outs[t].at[l, 1 - c]
            pltpu.make_async_remote_copy(
                src_ref=blk, dst_ref=blk, send_sem=send_sems.at[t], recv_sem=recv_sems.at[t],
                device_id=sibling, device_id_type=MESH).wait_recv()
        for cp in cps:
            cp.wait_send()

    out_shape = [_sds(a.shape, f32) for a in full]
    return pl.pallas_call(
        body, out_shape=out_shape, in_specs=[ANY] * n, out_specs=[ANY] * n, input_output_aliases={t: t for t in range(n)},
        scratch_shapes=[pltpu.SemaphoreType.DMA((n,)), pltpu.SemaphoreType.DMA((n,))], name=name)(*full)


def _share_copies(refs, l, send_sems, recv_sems):
    x, y, c, _ = _place()
    out = []
    for t, ref in enumerate(refs):
        mine, theirs = ref.at[l, c], ref.at[l, 1 - c]
        send = pltpu.make_async_remote_copy(src_ref=mine, dst_ref=mine, send_sem=send_sems.at[t], recv_sem=recv_sems.at[t],
                                            device_id=(x, y, 1 - c), device_id_type=MESH)
        land = pltpu.make_async_remote_copy(src_ref=mine, dst_ref=theirs, send_sem=send_sems.at[t], recv_sem=recv_sems.at[t],
                                            device_id=(x, y, 1 - c), device_id_type=MESH)
        out.append((send, land))
    return out


def _share_start(full, l, after, name):
    n = len(full)

    def body(*refs):
        for send, _ in _share_copies(refs[:n], l, refs[n + 1], refs[n + 2]):
            send.start()
        refs[-1][...] = jnp.zeros_like(refs[-1])

    out_shape = ([pltpu.SemaphoreType.DMA((n,)), pltpu.SemaphoreType.DMA((n,))] + [pltpu.HBM(a.shape, a.dtype) for a in full]
                 + [_sds((8, 128), f32)])
    res = pl.pallas_call(
        body, out_shape=out_shape, in_specs=[HBM] * n + [ANY], out_specs=[SEM, SEM] + [HBM] * n + [pl.BlockSpec(memory_space=pltpu.VMEM)],
        input_output_aliases={i: 2 + i for i in range(n)}, name=name, compiler_params=_SPLIT)(*_hbm(full), after)
    return res[0], res[1], list(res[2:2 + n])


def _share_wait(send_sems, recv_sems, full, l, after, name):
    n = len(full)

    def body(*refs):
        for send, land in _share_copies(refs[:n], l, refs[n], refs[n + 1]):
            send.wait_send()
            land.wait_recv()

    res = pl.pallas_call(
        body, out_shape=[pltpu.HBM(a.shape, a.dtype) for a in full], in_specs=[HBM] * n + [SEM, SEM, ANY], out_specs=[HBM] * n,
        input_output_aliases={i: i for i in range(n)}, name=name, compiler_params=_SPLIT)(*full, send_sems, recv_sems, after)
    return list(res)


SMALL_ROWS = 656


def _pack_small(per_layer, final_vec, loss_tile):
    rows = []
    for l in range(DEPTH):
        for k in ("ffn1_norm", "mix_norm", "ffn2_norm"):
            rows.append(per_layer[k][l].reshape(8, 128))
        rows.append(per_layer["pool_w"][l].reshape(128, 128))
        rows.append(jnp.pad(per_layer["pool_scale"][l].reshape(2, 128), ((0, 6), (0, 0))))
    rows.append(final_vec.reshape(8, 128))
    rows.append(loss_tile)
    return jnp.concatenate(rows, axis=0)


def _unpack_small(buf):
    out = {k: [] for k in ("ffn1_norm", "mix_norm", "ffn2_norm", "pool_w", "pool_scale")}
    r = 0
    for l in range(DEPTH):
        for k in ("ffn1_norm", "mix_norm", "ffn2_norm"):
            out[k].append(buf[r:r + 8].reshape(D))
            r += 8
        out["pool_w"].append(buf[r:r + 128].reshape(4, 64, 64))
        r += 128
        out["pool_scale"].append(buf[r:r + 2].reshape(PW))
        r += 8
    res = {k: jnp.stack(v) for k, v in out.items()}
    res["final_norm"] = buf[r:r + 8].reshape(D)
    res["loss"] = buf[r + 8, 0]
    return res


def _allreduce_small(buf):
    def body(in_ref, out_ref, slots, send_sems, recv_sems):
        x, y, c, _ = _place()
        me = 4 * x + 2 * y + c
        slots[me] = in_ref[...]
        peers = []
        for k in range(1, 8):
            px, py, pc = x ^ (k >> 2), y ^ ((k >> 1) & 1), c ^ (k & 1)
            cp = pltpu.make_async_remote_copy(
                src_ref=in_ref, dst_ref=slots.at[me], send_sem=send_sems.at[k - 1], recv_sem=recv_sems.at[k - 1],
                device_id=(px, py, pc), device_id_type=MESH)
            cp.start()
            peers.append(cp)
        for k in range(1, 8):
            px, py, pc = x ^ (k >> 2), y ^ ((k >> 1) & 1), c ^ (k & 1)
            slot = 4 * px + 2 * py + pc
            pltpu.make_async_remote_copy(
                src_ref=slots.at[slot], dst_ref=slots.at[slot], send_sem=send_sems.at[k - 1], recv_sem=recv_sems.at[k - 1],
                device_id=(px, py, pc), device_id_type=MESH).wait_recv()
        for cp in peers:
            cp.wait_send()
        acc = slots[0]
        for j in range(1, 8):
            acc = acc + slots[j]
        out_ref[...] = acc

    return pl.pallas_call(
        body, out_shape=_sds((SMALL_ROWS, 128), f32),
        in_specs=[pl.BlockSpec(memory_space=pltpu.VMEM)], out_specs=pl.BlockSpec(memory_space=pltpu.VMEM),
        scratch_shapes=[pltpu.VMEM((8, SMALL_ROWS, 128), f32), pltpu.SemaphoreType.DMA((7,)), pltpu.SemaphoreType.DMA((7,))],
        name="allreduce_small", compiler_params=_cp())(buf)


def _adamw_math(w, g, m, v):
    m = ADAM_B1 * m + (1.0 - ADAM_B1) * g
    v = ADAM_B2 * v + (1.0 - ADAM_B2) * (g * g)
    m_hat = m / (1.0 - ADAM_B1 ** ADAM_STEP)
    v_hat = v / (1.0 - ADAM_B2 ** ADAM_STEP)
    return -ADAM_LR * (m_hat / (jnp.sqrt(v_hat) + ADAM_EPS) + ADAM_WD * w), m, v


def _adamw(w, g, m, v, name, first=0, prev=None, dep=None):
    def body(w_ref, g_ref, m_ref, v_ref, *rest):
        go_ref, d_ref, mo_ref, vo_ref = rest[-4:]
        g = g_ref[...]
        d, mn, vn = _adamw_math(w_ref[...], g, m_ref[...], v_ref[...])
        go_ref[...] = g
        d_ref[...] = d
        mo_ref[...] = mn
        vo_ref[...] = vn

    _, rows, cols = w.shape
    r = rows // 4 if rows % 32 == 0 else rows
    spec = pl.BlockSpec((1, r, cols), lambda i, j: (first + i, j, 0))
    gspec = pl.BlockSpec((1, r, cols), lambda i, j: (i, j, 0))
    out = _sds(w.shape, f32)
    extra = [] if prev is None else list(prev)
    dspec, dop = _dep(dep)
    return pl.pallas_call(
        body, out_shape=(out, out, out, out), grid=(g.shape[0], rows // r), in_specs=[spec, gspec, spec, spec] + [ANY] * len(extra) + dspec,
        out_specs=(spec,) * 4, input_output_aliases={4 + i: i for i in range(len(extra))}, name=name,
        compiler_params=_cp("parallel", "parallel"))(w, g, m, v, *extra, *dop)


SMALL_NAMES = ("ffn1_norm", "mix_norm", "pool_w", "pool_scale", "ffn2_norm", "final_norm")
WEIGHT_ORDER = ("ffn1_norm", "ffn1_w_gate", "ffn1_w_up", "ffn1_w_down", "mix_norm", "w_in", "pool_w", "pool_scale", "w_out",
                "ffn2_norm", "ffn2_w_gate", "ffn2_w_up", "ffn2_w_down", "final_norm")


def _pack_small_params(p):
    per_layer = {k: [p[k][l] for l in range(DEPTH)] for k in ("ffn1_norm", "mix_norm", "ffn2_norm", "pool_w", "pool_scale")}
    return _pack_small(per_layer, p["final_norm"], jnp.zeros((8, 128), f32))


def kernel(x, positions, ffn1_norm, ffn1_w_gate, ffn1_w_up, ffn1_w_down, mix_norm, w_in, pool_w, pool_scale, w_out, ffn2_norm, ffn2_w_gate, ffn2_w_up, ffn2_w_down, final_norm, loss_target, m_ffn1_norm, m_ffn1_w_gate, m_ffn1_w_up, m_ffn1_w_down, m_mix_norm, m_w_in, m_pool_w, m_pool_scale, m_w_out, m_ffn2_norm, m_ffn2_w_gate, m_ffn2_w_up, m_ffn2_w_down, m_final_norm, v_ffn1_norm, v_ffn1_w_gate, v_ffn1_w_up, v_ffn1_w_down, v_mix_norm, v_w_in, v_pool_w, v_pool_scale, v_w_out, v_ffn2_norm, v_ffn2_w_gate, v_ffn2_w_up, v_ffn2_w_down, v_final_norm):
    params = dict(ffn1_norm=ffn1_norm, ffn1_w_gate=ffn1_w_gate, ffn1_w_up=ffn1_w_up, ffn1_w_down=ffn1_w_down,
                  mix_norm=mix_norm, w_in=w_in, pool_w=pool_w, pool_scale=pool_scale, w_out=w_out, ffn2_norm=ffn2_norm,
                  ffn2_w_gate=ffn2_w_gate, ffn2_w_up=ffn2_w_up, ffn2_w_down=ffn2_w_down, final_norm=final_norm)
    mom_m = dict(ffn1_norm=m_ffn1_norm, ffn1_w_gate=m_ffn1_w_gate, ffn1_w_up=m_ffn1_w_up, ffn1_w_down=m_ffn1_w_down,
                 mix_norm=m_mix_norm, w_in=m_w_in, pool_w=m_pool_w, pool_scale=m_pool_scale, w_out=m_w_out,
                 ffn2_norm=m_ffn2_norm, ffn2_w_gate=m_ffn2_w_gate, ffn2_w_up=m_ffn2_w_up, ffn2_w_down=m_ffn2_w_down,
                 final_norm=m_final_norm)
    mom_v = dict(ffn1_norm=v_ffn1_norm, ffn1_w_gate=v_ffn1_w_gate, ffn1_w_up=v_ffn1_w_up, ffn1_w_down=v_ffn1_w_down,
                 mix_norm=v_mix_norm, w_in=v_w_in, pool_w=v_pool_w, pool_scale=v_pool_scale, w_out=v_w_out,
                 ffn2_norm=v_ffn2_norm, ffn2_w_gate=v_ffn2_w_gate, ffn2_w_up=v_ffn2_w_up, ffn2_w_down=v_ffn2_w_down,
                 final_norm=v_final_norm)
    names = [t[0] for t in BIG]
    for d in (params, mom_m, mom_v):
        for k in TRANSPOSED:
            d[k] = jnp.swapaxes(d[k], 1, 2)

    place = jnp.stack([lax.axis_index("c"), 2 * lax.axis_index("x") + lax.axis_index("y")]).astype(jnp.int32)
    def gather_start(tag, cast, after):
        return _exchange_start(cast, [], after, _own_half, _slot_half, f"gather_start{tag}")

    def gather_end(started, after, tag, spec):
        send_sems, recv_sems, bufs, _, _ = started
        bufs, _ = _exchange_wait(send_sems, recv_sems, bufs, [], after, _own_half, _slot_half, f"gather_wait{tag}")
        return {nm: a.reshape(NSH, rows, cols) for (nm, rows, cols), a in zip(spec, _gather_forward(bufs))}

    tabs = _rope_tables(*_rope_inputs(positions))
    h = x.reshape(S, D)
    weights, saved = [], []
    cast0 = _cast_layer(params, 0, place)
    first = gather_start("0a", cast0[:FFN1], place)
    second = gather_start("0b", cast0[FFN1:], first[-1])
    after = second[-1]
    casts, started = {}, {}
    for l in range(1, DEPTH):
        casts[l] = _cast_layer(params, l, place, after)
        after = casts[l][0]
        if l == 1:
            started[1] = gather_start(1, casts[1], second[-1])
            after = started[1][-1]
    for l in range(DEPTH):
        if l == 0:
            w = gather_end(first, after, "0a", BIG[:FFN1])
            rest = lambda x1: gather_end(second, x1, "0b", BIG[FFN1:])
        else:
            w, rest = gather_end(started[l], after, l, BIG), None
        dep = None
        if l + 2 < DEPTH:
            started[l + 2] = gather_start(l + 2, casts[l + 2], w["g1"])
            dep = started[l + 2][-1]
        h, sv, w = _layer_fwd(h, w, params, l, tabs, dep, rest)
        weights.append(w)
        saved.append(sv)
        after = h
    loss, dx, dgf = _final_loss(h, final_norm.reshape(1, D), loss_target.reshape(S, D))

    upper = [lax.empty((DEPTH - 1, 2, rows // 2, cols), f32) for _, rows, cols in BIG]
    lower = [lax.empty((1, 2, rows // 2, cols), f32) for _, rows, cols in BIG]
    sg = {k: [None] * DEPTH for k in ("ffn1_norm", "mix_norm", "pool_w", "pool_scale", "ffn2_norm")}
    sg["final_norm"] = dgf

    def reduce_end(started, after, l, full, slot):
        send_sems, recv_sems, psum, parts, _ = started
        psum, parts = _exchange_wait(send_sems, recv_sems, psum, parts, after, _slot, _slot, f"grad_wait{l}")
        return _chip_sum(psum, parts, full, place, slot, f"chip_sum{l}")

    started, dep, shares = None, None, []
    for l in reversed(range(DEPTH)):
        dx, gr, sgl = _layer_bwd(dx, weights[l], saved[l], tabs, dep)
        for k, v in sgl.items():
            sg[k][l] = v
        grads = [gr[nm] for nm in names]
        if started is None:
            got = _sibling_swap(grads)
        else:
            swap_send, swap_recv, grads, got = _swap_start(grads, place, f"swap_start{l}")
            send_sems, recv_sems, upper = _share_start(reduce_end(started, dx, l + 1, upper, l), l, place, f"share_start{l + 1}")
            shares.append((send_sems, recv_sems, l, l + 1))
            grads, got = _swap_wait(swap_send, swap_recv, grads, got, upper[0], f"swap_wait{l}")
        psum = _pair_sum(grads, got, place)
        parts = [lax.empty(a.shape, bf16) for a in psum]
        started = _exchange_start(psum, parts, place, _slot, _slot, f"grad_start{l}")
        dep = started[-1]

    big_out = {}
    for send_sems, recv_sems, slot, l in shares:
        upper = _share_wait(send_sems, recv_sems, upper, slot, place, f"share_wait{l}")
    for (nm, rows, cols), g in zip(BIG, upper):
        k = BIG_SRC[nm]
        big_out[k] = _adamw(params[k], g.reshape(DEPTH - 1, rows, cols), mom_m[k], mom_v[k], "adamw_upper_" + k, first=1, dep=dep)
        dep = big_out[k][1]
    lower = _sibling_share(reduce_end(started, dep, 0, lower, 0), 0, "sibling_share0")
    for (nm, rows, cols), g in zip(BIG, lower):
        k = BIG_SRC[nm]
        big_out[k] = _adamw(params[k], g.reshape(1, rows, cols), mom_m[k], mom_v[k], "adamw_lower_" + k, first=0, prev=big_out[k])

    per_layer = {k: sg[k] for k in ("ffn1_norm", "mix_norm", "ffn2_norm", "pool_w", "pool_scale")}
    small_sum = _allreduce_small(_pack_small(per_layer, sg["final_norm"], loss))
    gs, ds_, ms, vs = _adamw(_pack_small_params(params).reshape(1, SMALL_ROWS, 128), small_sum.reshape(1, SMALL_ROWS, 128),
                             _pack_small_params(mom_m).reshape(1, SMALL_ROWS, 128),
                             _pack_small_params(mom_v).reshape(1, SMALL_ROWS, 128), "adamw_small")
    small_out = [_unpack_small(a.reshape(SMALL_ROWS, 128)) for a in (gs, ds_, ms, vs)]

    grad, delta, new_m, new_v = {}, {}, {}, {}
    for k in WEIGHT_ORDER:
        if k in SMALL_NAMES:
            grad[k], delta[k], new_m[k], new_v[k] = (so[k] for so in small_out)
        else:
            grad[k], delta[k], new_m[k], new_v[k] = big_out[k]
    for d in (grad, delta, new_m, new_v):
        for k in TRANSPOSED:
            d[k] = jnp.swapaxes(d[k], 1, 2)
    return (small_out[0]["loss"], dx.reshape(1, S, D), *[grad[k] for k in WEIGHT_ORDER], *[delta[k] for k in WEIGHT_ORDER],
            *[new_m[k] for k in WEIGHT_ORDER], *[new_v[k] for k in WEIGHT_ORDER])
```

```python
import functools

import jax
import jax.numpy as jnp
import numpy as np
from jax import lax
from jax.experimental import pallas as pl
from jax.experimental.pallas import tpu as pltpu

f32 = jnp.float32
bf16 = jnp.bfloat16

S = 2048
D = 1024
DEPTH = 4
NSH = 4
FS = 704
PROJ = 2560
PS = 640
PW = 256
AW = 768
NPAIR = 6
NORM_EPS = 1e-6
MASK_VALUE = -1e30
ROPE_THETA = 500000.0
DILATIONS = (1, 4, 16)
QBLK = 128
NBLK = S // QBLK
TM = 512
VMEM_LIMIT = 56 * 1024 * 1024

ADAM_LR = 0.001
ADAM_B1 = 0.9
ADAM_B2 = 0.999
ADAM_EPS = 1e-08
ADAM_WD = 0.01
ADAM_STEP = 10

MESH = pl.DeviceIdType.MESH
ANY = pl.BlockSpec(memory_space=pl.ANY)

BIG = (("g1", FS, D), ("u1", FS, D), ("d1", FS, D), ("wi", D, PS), ("wo", PW, D), ("g2", FS, D), ("u2", FS, D), ("d2", FS, D))
TRANSPOSED = ("ffn1_w_gate", "ffn1_w_up", "ffn2_w_gate", "ffn2_w_up")
FFN1 = 3
BIG_SRC = {"g1": "ffn1_w_gate", "u1": "ffn1_w_up", "d1": "ffn1_w_down", "wi": "w_in", "wo": "w_out",
           "g2": "ffn2_w_gate", "u2": "ffn2_w_up", "d2": "ffn2_w_down"}


def _cp(*sem):
    return pltpu.CompilerParams(dimension_semantics=sem if sem else None, vmem_limit_bytes=VMEM_LIMIT)


def _sds(shape, dt):
    return jax.ShapeDtypeStruct(shape, dt)


def _dot(a, b):
    return jnp.dot(a, b, preferred_element_type=f32)


def _dot_nt(a, b):
    return lax.dot_general(a, b, (((1,), (1,)), ((), ())), preferred_element_type=f32)


def _dot_tn(a, b):
    return lax.dot_general(a, b, (((0,), (0,)), ((), ())), preferred_element_type=f32)


def _dep(dep):
    return ([], []) if dep is None else ([ANY], [dep])


def _resident(shape):
    return pl.BlockSpec(shape, lambda i: (0,) * len(shape), pipeline_mode=pl.Buffered(1))


def _ffn_fwd(x, g, wg, wu, wd, dep=None):
    def body(x_ref, g_ref, wg_ref, wu_ref, wd_ref, *rest):
        xo_ref, h_ref, a_ref, b_ref = rest[-4:]
        xf = x_ref[...]
        r = lax.rsqrt(jnp.mean(xf * xf, axis=-1, keepdims=True) + NORM_EPS)
        hh = ((xf * r) * g_ref[...]).astype(bf16)
        h_ref[...] = hh
        acc = None
        for s in range(NSH):
            a = _dot_nt(hh, wg_ref[s])
            b = _dot_nt(hh, wu_ref[s])
            a_ref[s] = a.astype(bf16)
            b_ref[s] = b.astype(bf16)
            p = _dot((a * (1.0 / (1.0 + jnp.exp(-a))) * b).astype(bf16), wd_ref[s])
            acc = p if acc is None else acc + p
        xo_ref[...] = xf + 0.5 * acc

    tok = pl.BlockSpec((TM, D), lambda i: (i, 0))
    hid = pl.BlockSpec((NSH, TM, FS), lambda i: (0, i, 0))
    wsp = _resident((NSH, FS, D))
    dspec, dop = _dep(dep)
    return pl.pallas_call(
        body, out_shape=(_sds((S, D), f32), _sds((S, D), bf16), _sds((NSH, S, FS), bf16), _sds((NSH, S, FS), bf16)),
        grid=(S // TM,), in_specs=[tok, pl.BlockSpec((1, D), lambda i: (0, 0)), wsp, wsp, wsp] + dspec,
        out_specs=(tok, tok, hid, hid), name="ffn_fwd", compiler_params=_cp("parallel"))(x, g, wg, wu, wd, *dop)


def _in_proj(x, g, wi, dep=None):
    def body(x_ref, g_ref, w_ref, *rest):
        o_ref, h_ref = rest[-2:]
        xf = x_ref[...]
        r = lax.rsqrt(jnp.mean(xf * xf, axis=-1, keepdims=True) + NORM_EPS)
        hh = ((xf * r) * g_ref[...]).astype(bf16)
        h_ref[...] = hh
        for s in range(NSH):
            o_ref[:, PS * s:PS * (s + 1)] = _dot(hh, w_ref[s])

    tok = pl.BlockSpec((TM, D), lambda i: (i, 0))
    dspec, dop = _dep(dep)
    return pl.pallas_call(
        body, out_shape=(_sds((S, PROJ), f32), _sds((S, D), bf16)), grid=(S // TM,),
        in_specs=[tok, pl.BlockSpec((1, D), lambda i: (0, 0)), _resident((NSH, D, PS))] + dspec,
        out_specs=(pl.BlockSpec((TM, PROJ), lambda i: (i, 0)), tok), name="in_proj", compiler_params=_cp("parallel"))(x, g, wi, *dop)


def _out_proj(x, mixed, wo):
    def body(x_ref, m_ref, w_ref, o_ref):
        o_ref[...] = x_ref[...] + _dot(m_ref[...], w_ref[...].reshape(D, D))

    return pl.pallas_call(
        body, out_shape=_sds((S, D), f32), grid=(S // TM,),
        in_specs=[pl.BlockSpec((TM, D), lambda i: (i, 0)), pl.BlockSpec((TM, D), lambda i: (i, 0)),
                  pl.BlockSpec((NSH, PW, D), lambda i: (0, 0, 0))],
        out_specs=pl.BlockSpec((TM, D), lambda i: (i, 0)), name="out_proj", compiler_params=_cp("parallel"))(x, mixed, wo)


def _out_proj_bwd(dx, wo):
    def body(dx_ref, w_ref, o_ref):
        o_ref[...] = _dot_nt(dx_ref[...].astype(bf16), w_ref[...].reshape(D, D))

    return pl.pallas_call(
        body, out_shape=_sds((S, D), f32), grid=(S // TM,),
        in_specs=[pl.BlockSpec((TM, D), lambda i: (i, 0)), pl.BlockSpec((NSH, PW, D), lambda i: (0, 0, 0))],
        out_specs=pl.BlockSpec((TM, D), lambda i: (i, 0)), name="out_proj_bwd", compiler_params=_cp("parallel"))(dx, wo)


def _ffn_bwd_mid(dx, h, a, b, wd, dep=None):
    nt = S // TM

    def body(dx_ref, h_ref, a_ref, b_ref, wd_ref, *rest):
        da_ref, db_ref, dwd_ref, dwg_ref, dwu_ref, dy_s, u_s, da_s, db_s = rest[-9:]
        i = pl.program_id(1)
        rows = pl.ds(pl.multiple_of(i * TM, TM), TM)
        dy = (0.5 * dx_ref[...]).astype(bf16)
        dy_s[rows, :] = dy
        du = _dot_nt(dy, wd_ref[0])
        a = a_ref[0].astype(f32)
        b = b_ref[0].astype(f32)
        sig = 1.0 / (1.0 + jnp.exp(-a))
        silu = a * sig
        da = (du * b * (sig * (1.0 + a * (1.0 - sig)))).astype(bf16)
        db = (du * silu).astype(bf16)
        da_ref[0] = da
        db_ref[0] = db
        da_s[rows, :] = da
        db_s[rows, :] = db
        u_s[rows, :] = (silu * b).astype(bf16)

        @pl.when(i == nt - 1)
        def _():
            hh = h_ref[...]
            dwd_ref[...] = _dot_tn(u_s[...], dy_s[...]).astype(bf16).reshape(dwd_ref.shape)
            dwg_ref[...] = _dot_tn(da_s[...], hh).astype(bf16).reshape(dwg_ref.shape)
            dwu_ref[...] = _dot_tn(db_s[...], hh).astype(bf16).reshape(dwu_ref.shape)

    tok = pl.BlockSpec((TM, D), lambda s, i: (i, 0))
    hid = pl.BlockSpec((1, TM, FS), lambda s, i: (s, i, 0))
    wsp = pl.BlockSpec((1, 2, FS // 2, D), lambda s, i: (s, 0, 0, 0))
    hidden = _sds((NSH, S, FS), bf16)
    wgrad = _sds((NSH, 2, FS // 2, D), bf16)
    whole = pltpu.VMEM((S, FS), bf16)
    dspec, dop = _dep(dep)
    return pl.pallas_call(
        body, out_shape=(hidden, hidden, wgrad, wgrad, wgrad), grid=(NSH, nt),
        in_specs=[tok, pl.BlockSpec((S, D), lambda s, i: (0, 0), pipeline_mode=pl.Buffered(1)), hid, hid,
                  pl.BlockSpec((1, FS, D), lambda s, i: (s, 0, 0))] + dspec,
        out_specs=(hid, hid, wsp, wsp, wsp), scratch_shapes=[pltpu.VMEM((S, D), bf16), whole, whole, whole],
        name="ffn_bwd_mid", compiler_params=_cp("parallel", "arbitrary"))(dx, h, a, b, wd, *dop)


def _norm_bwd_tail(acc, x_ref, dxin_ref, g_ref, dxo_ref, dg_ref, first):
    xf = x_ref[...]
    r = lax.rsqrt(jnp.mean(xf * xf, axis=-1, keepdims=True) + NORM_EPS)
    xhat = xf * r
    dhg = acc * g_ref[...]
    dxo_ref[...] = dxin_ref[...] + r * (dhg - xhat * jnp.mean(dhg * xhat, axis=-1, keepdims=True))
    part = jnp.sum(acc * xhat, axis=0, keepdims=True)

    @pl.when(first)
    def _():
        dg_ref[...] = part

    @pl.when(jnp.logical_not(first))
    def _():
        dg_ref[...] += part


def _ffn_bwd_dx(dx, x_in, g, da, db, wg, wu):
    def body(dx_ref, x_ref, g_ref, da_ref, db_ref, wg_ref, wu_ref, dxo_ref, dg_ref):
        acc = None
        for s in range(NSH):
            p = _dot(da_ref[s], wg_ref[s])
            acc = p if acc is None else acc + p
            acc = acc + _dot(db_ref[s], wu_ref[s])
        _norm_bwd_tail(acc, x_ref, dx_ref, g_ref, dxo_ref, dg_ref, pl.program_id(0) == 0)

    tok = pl.BlockSpec((TM, D), lambda i: (i, 0))
    vec = pl.BlockSpec((1, D), lambda i: (0, 0))
    hid = pl.BlockSpec((NSH, TM, FS), lambda i: (0, i, 0))
    wsp = _resident((NSH, FS, D))
    return pl.pallas_call(
        body, out_shape=(_sds((S, D), f32), _sds((1, D), f32)), grid=(S // TM,),
        in_specs=[tok, tok, vec, hid, hid, wsp, wsp], out_specs=(tok, vec),
        name="ffn_bwd_dx", compiler_params=_cp("arbitrary"))(dx, x_in, g, da, db, wg, wu)


def _in_proj_bwd_dx(dx, x_in, g, dproj, wi):
    def body(dx_ref, x_ref, g_ref, dp_ref, w_ref, dxo_ref, dg_ref):
        acc = None
        for s in range(NSH):
            p = _dot_nt(dp_ref[:, PS * s:PS * (s + 1)], w_ref[s])
            acc = p if acc is None else acc + p
        _norm_bwd_tail(acc, x_ref, dx_ref, g_ref, dxo_ref, dg_ref, pl.program_id(0) == 0)

    tok = pl.BlockSpec((TM, D), lambda i: (i, 0))
    vec = pl.BlockSpec((1, D), lambda i: (0, 0))
    return pl.pallas_call(
        body, out_shape=(_sds((S, D), f32), _sds((1, D), f32)), grid=(S // TM,),
        in_specs=[tok, tok, vec, pl.BlockSpec((TM, PROJ), lambda i: (i, 0)), _resident((NSH, D, PS))], out_specs=(tok, vec),
        name="in_proj_bwd_dx", compiler_params=_cp("arbitrary"))(dx, x_in, g, dproj, wi)


def _dw(lhs, rhs, lhs_spec, rhs_spec, rows, cols, name, cast_rhs=False):
    def body(l_ref, r_ref, o_ref):
        r = r_ref[...].astype(bf16) if cast_rhs else r_ref[...]
        o_ref[...] = _dot_tn(l_ref[...], r).astype(bf16).reshape(1, 2, rows // 2, cols)

    return pl.pallas_call(
        body, out_shape=_sds((NSH, 2, rows // 2, cols), bf16), grid=(NSH,), in_specs=[lhs_spec, rhs_spec],
        out_specs=pl.BlockSpec((1, 2, rows // 2, cols), lambda s: (s, 0, 0, 0)), name=name, compiler_params=_cp("parallel"))(lhs, rhs)


_WHOLE_TOK = pl.BlockSpec((S, D), lambda s: (0, 0))


def _dw_in(h, dproj):
    return _dw(h, dproj, _WHOLE_TOK, pl.BlockSpec((S, PS), lambda s: (0, s)), D, PS, "dw_in")


def _dw_out(mixed, dx):
    return _dw(mixed, dx, pl.BlockSpec((S, PW), lambda s: (0, s)), _WHOLE_TOK, PW, D, "dw_out", cast_rhs=True)


def _final_loss(x, g, target):
    def body(x_ref, g_ref, t_ref, loss_ref, dx_ref, dg_ref):
        i = pl.program_id(0)
        xf = x_ref[...]
        r = lax.rsqrt(jnp.mean(xf * xf, axis=-1, keepdims=True) + NORM_EPS)
        xhat = xf * r
        err = xhat * g_ref[...] - t_ref[...]
        dy = err * (1.0 / D)
        dhg = dy * g_ref[...]
        dx_ref[...] = r * (dhg - xhat * jnp.mean(dhg * xhat, axis=-1, keepdims=True))
        part = jnp.sum(dy * xhat, axis=0, keepdims=True)
        lpart = jnp.zeros((8, 128), f32) + 0.5 * jnp.sum(jnp.mean(err * err, axis=-1, keepdims=True))

        @pl.when(i == 0)
        def _():
            dg_ref[...] = part
            loss_ref[...] = lpart

        @pl.when(i != 0)
        def _():
            dg_ref[...] += part
            loss_ref[...] += lpart

    tok = pl.BlockSpec((TM, D), lambda i: (i, 0))
    vec = pl.BlockSpec((1, D), lambda i: (0, 0))
    return pl.pallas_call(
        body, out_shape=(_sds((8, 128), f32), _sds((S, D), f32), _sds((1, D), f32)), grid=(S // TM,),
        in_specs=[tok, vec, tok], out_specs=(pl.BlockSpec((8, 128), lambda i: (0, 0)), tok, vec),
        name="final_loss", compiler_params=_cp("arbitrary"))(x, g, target)


def _shift_down(x, k, row):
    return jnp.where(row >= k, pltpu.roll(x, k, axis=0), 0.0)


def _shift_up(x, k, row):
    return jnp.where(row < S - k, pltpu.roll(x, S - k, axis=0), 0.0)


def _pool_geometry():
    row = lax.broadcasted_iota(jnp.int32, (S, PW), 0)
    grp = lax.broadcasted_iota(jnp.int32, (S, PW), 1) // 64
    half = jnp.where(grp == 0, 1, jnp.where(grp == 1, 2, jnp.where(grp == 2, 4, 8)))
    hi = jnp.minimum(row + half - 1, S - 1)
    lo = jnp.maximum(row - half, 0)
    return row, grp, (hi - lo + 1).astype(f32)


def _by_group(grp, v0, v1, v2, v3):
    return jnp.where(grp == 0, v0, jnp.where(grp == 1, v1, jnp.where(grp == 2, v2, v3)))


def _window_sums(x, row, grp, transpose):
    l1, r1 = x, x
    l2, r2 = l1 + _shift_down(l1, 1, row), r1 + _shift_up(r1, 1, row)
    l4, r4 = l2 + _shift_down(l2, 2, row), r2 + _shift_up(r2, 2, row)
    l8, r8 = l4 + _shift_down(l4, 4, row), r4 + _shift_up(r4, 4, row)
    lsel = _by_group(grp, l1, l2, l4, l8)
    rsel = _by_group(grp, r1, r2, r4, r8)
    if transpose:
        return lsel + _shift_up(rsel, 1, row)
    return _shift_down(lsel, 1, row) + rsel


def _pool_fwd(proj, wbd, scale):
    def body(v_ref, w_ref, sc_ref, mixed_ref, diff_ref):
        row, grp, cnt = _pool_geometry()
        v = v_ref[...]
        diff = (_window_sums(v, row, grp, False) / cnt - v).astype(bf16)
        diff_ref[...] = diff
        mixed_ref[...] = (_dot(diff, w_ref[...].astype(bf16)) * sc_ref[...]).astype(bf16)

    col = pl.BlockSpec((S, PW), lambda i: (0, 0))
    return pl.pallas_call(
        body, out_shape=(_sds((S, D), bf16), _sds((S, PW), bf16)), grid=(1,),
        in_specs=[col, pl.BlockSpec((PW, PW), lambda i: (0, 0)), pl.BlockSpec((1, PW), lambda i: (0, 0))],
        out_specs=(col, col), name="pool_fwd", compiler_params=_cp("arbitrary"))(proj, wbd, scale)


def _pool_bwd(dmixed, diff, wbd, scale, dproj):
    def body(dy_ref, diff_ref, w_ref, sc_ref, dproj_in, dv_ref, dw_ref, dsc_ref):
        del dproj_in
        row, grp, cnt = _pool_geometry()
        dy = dy_ref[...]
        diff = diff_ref[...]
        w = w_ref[...].astype(bf16)
        dsc_ref[...] = jnp.sum(dy * _dot(diff, w), axis=0, keepdims=True)
        dys = (dy * sc_ref[...]).astype(bf16)
        dw_ref[...] = _dot_tn(diff, dys)
        ddiff = _dot_nt(dys, w)
        dv_ref[...] = (_window_sums(ddiff / cnt, row, grp, True) - ddiff).astype(bf16)

    col = pl.BlockSpec((S, PW), lambda i: (0, 0))
    return pl.pallas_call(
        body, out_shape=(_sds((S, PROJ), bf16), _sds((PW, PW), f32), _sds((1, PW), f32)), grid=(1,),
        in_specs=[col, col, pl.BlockSpec((PW, PW), lambda i: (0, 0)), pl.BlockSpec((1, PW), lambda i: (0, 0)), ANY],
        out_specs=(col, pl.BlockSpec((PW, PW), lambda i: (0, 0)), pl.BlockSpec((1, PW), lambda i: (0, 0))),
        input_output_aliases={4: 0}, name="pool_bwd", compiler_params=_cp("arbitrary"))(dmixed, diff, wbd, scale, dproj)


def _rope_tables(pos_col, freq_row):
    def body(p_ref, f_ref, c_ref, a_ref, b_ref):
        ang = p_ref[...].astype(f32) * f_ref[...]
        l64 = lax.broadcasted_iota(jnp.int32, (S, 128), 1) % 64
        cos, sin = jnp.cos(ang), jnp.sin(ang)
        c_ref[...] = jnp.where(l64 < 16, cos, 1.0)
        a_ref[...] = jnp.where(l64 < 8, -sin, 0.0)
        b_ref[...] = jnp.where((l64 >= 8) & (l64 < 16), sin, 0.0)

    t = _sds((S, 128), f32)
    return pl.pallas_call(body, out_shape=(t, t, t), name="rope_tables", compiler_params=_cp())(pos_col, freq_row)


def _rope(t, c, a, b):
    return t * c + pltpu.roll(t, 120, axis=1) * a + pltpu.roll(t, 8, axis=1) * b


def _rope_bwd(g, c, a, b):
    return g * c + pltpu.roll(g * a, 8, axis=1) + pltpu.roll(g * b, 120, axis=1)


def _perm_load(ref, d):
    if d == 1:
        return ref[...]
    n = S // d
    return jnp.concatenate([ref[pl.ds(r, n, stride=d), :] for r in range(d)], axis=0)


def _unperm_store(ref, val, d):
    if d == 1:
        ref[...] = val
        return
    n = S // d
    for r in range(d):
        ref[pl.ds(r, n, stride=d), :] = val[r * n:(r + 1) * n, :]


def _band(xp, d):
    if d == NBLK:
        return xp.reshape(NBLK, QBLK, 128)
    z = jnp.zeros((64, 128), bf16)
    p = jnp.concatenate([z, xp, z], axis=0).reshape(NBLK + 1, QBLK, 128)
    return jnp.concatenate([p[:NBLK], p[1:]], axis=1)


def _unband(xb, d):
    if d == NBLK:
        return xb.reshape(S, 128)
    z = jnp.zeros((1, QBLK, 128), f32)
    p = jnp.concatenate([xb[:, :QBLK], z], axis=0) + jnp.concatenate([z, xb[:, QBLK:]], axis=0)
    return p.reshape(S + QBLK, 128)[64:S + 64]


def _band_mask(d):
    if d == NBLK:
        a = lax.broadcasted_iota(jnp.int32, (1, 2 * QBLK, QBLK), 1) & (QBLK - 1)
        b = lax.broadcasted_iota(jnp.int32, (1, 2 * QBLK, QBLK), 2)
        return (b >= a - 64) & (b <= a + 64)
    blocks_per_class = NBLK // d
    n = lax.broadcasted_iota(jnp.int32, (NBLK, 1, 2 * QBLK), 0) & (blocks_per_class - 1)
    be = lax.broadcasted_iota(jnp.int32, (NBLK, 1, 2 * QBLK), 2)
    a = lax.broadcasted_iota(jnp.int32, (1, 2 * QBLK, 2 * QBLK), 1) & (QBLK - 1)
    b = lax.broadcasted_iota(jnp.int32, (1, 2 * QBLK, 2 * QBLK), 2)
    band = (b >= a) & (b <= a + 128)
    edge = ((be >= 64) | (n != 0)) & ((be < QBLK + 64) | (n != blocks_per_class - 1))
    return band & edge


def _stack_heads(xb, lo):
    z = jnp.zeros_like(xb)
    return jnp.concatenate([jnp.where(lo, xb, z), jnp.where(lo, z, xb)], axis=1)


def _unstack_heads(x2, lo):
    return jnp.where(lo, x2[:, :QBLK], x2[:, QBLK:])


def _rows_to_lanes(col2, lo):
    return jnp.where(lo, jnp.broadcast_to(col2[:, :QBLK], (NBLK, QBLK, 128)), jnp.broadcast_to(col2[:, QBLK:], (NBLK, QBLK, 128)))


def _bmm_nt(a, b):
    return jnp.einsum('nqd,nkd->nqk', a, b, preferred_element_type=f32)


def _bmm_nn(a, b):
    return jnp.einsum('nqk,nkd->nqd', a, b, preferred_element_type=f32)


def _bmm_tn(a, b):
    return jnp.einsum('nqk,nqd->nkd', a, b, preferred_element_type=f32)


def _attn_fwd(proj, tc, ta, tb, mixed):
    def body(q_ref, k_ref, v_ref, c_ref, a_ref, b_ref, mixed_in, mixed_ref, o_ref, lse_ref, qn, kn, t_num, t_m, t_den):
        del mixed_in
        lo = lax.broadcasted_iota(jnp.int32, (1, 1, 128), 2) < 64
        c, a, b = c_ref[...], a_ref[...], b_ref[...]
        qn[...] = _rope(q_ref[...], c, a, b)
        kn[...] = _rope(k_ref[...], c, a, b)
        run = None
        for d in DILATIONS:
            q2 = _stack_heads(_perm_load(qn, d).astype(bf16).reshape(NBLK, QBLK, 128), lo)
            kb = _band(_perm_load(kn, d).astype(bf16), d)
            vb = _band(_perm_load(v_ref, d).astype(bf16), d)
            s = jnp.where(_band_mask(d), _bmm_nt(q2, kb) * 0.125, MASK_VALUE)
            m = jnp.max(s, axis=2, keepdims=True)
            p = jnp.exp(s - m)
            den = jnp.sum(p, axis=2, keepdims=True)
            num = _unstack_heads(_bmm_nn(p.astype(bf16), vb), lo)
            _unperm_store(t_num, num.reshape(S, 128), d)
            _unperm_store(t_m, _rows_to_lanes(m, lo).reshape(S, 128), d)
            _unperm_store(t_den, _rows_to_lanes(den, lo).reshape(S, 128), d)
            if run is None:
                run = (t_m[...], t_num[...], t_den[...])
            else:
                m_new = jnp.maximum(run[0], t_m[...])
                w_old, w_new = jnp.exp(run[0] - m_new), jnp.exp(t_m[...] - m_new)
                run = (m_new, w_old * run[1] + w_new * t_num[...], w_old * run[2] + w_new * t_den[...])
        out = run[1] / run[2]
        o_ref[...] = out
        mixed_ref[...] = out.astype(bf16)
        lse_ref[...] = run[0] + jnp.log(run[2])

    def col(off):
        return pl.BlockSpec((S, 128), lambda j, off=off: (0, off + j))

    tab = pl.BlockSpec((S, 128), lambda j: (0, 0))
    scr = pltpu.VMEM((S, 128), f32)
    return pl.pallas_call(
        body, out_shape=(_sds((S, D), bf16), _sds((S, AW), f32), _sds((S, AW), f32)), grid=(NPAIR,),
        in_specs=[col(2), col(8), col(14), tab, tab, tab, ANY], out_specs=(col(2), col(0), col(0)),
        scratch_shapes=[scr, scr, scr, scr, scr], input_output_aliases={6: 0}, name="attn_fwd",
        compiler_params=_cp("arbitrary"))(proj, proj, proj, tc, ta, tb, mixed)


def _attn_bwd(proj, tc, ta, tb, o, lse, dmixed):
    def body(q_ref, k_ref, v_ref, c_ref, a_ref, b_ref, o_ref, lse_ref, do_ref, dp_ref, qn, kn, tmp, dk_s, dv_s):
        t = pl.program_id(1)

        @pl.when(t == 0)
        def _():
            lo = lax.broadcasted_iota(jnp.int32, (1, 1, 128), 2) < 64
            c, a, b = c_ref[...], a_ref[...], b_ref[...]
            qn[...] = _rope(q_ref[...], c, a, b)
            kn[...] = _rope(k_ref[...], c, a, b)
            dq = dk = dv = None
            for d in DILATIONS:
                q2 = _stack_heads(_perm_load(qn, d).astype(bf16).reshape(NBLK, QBLK, 128), lo)
                kb = _band(_perm_load(kn, d).astype(bf16), d)
                vb = _band(_perm_load(v_ref, d).astype(bf16), d)
                dob = _perm_load(do_ref, d).reshape(NBLK, QBLK, 128)
                ob = _perm_load(o_ref, d).reshape(NBLK, QBLK, 128)
                lsb = _perm_load(lse_ref, d).reshape(NBLK, QBLK, 128)
                do2 = _stack_heads(dob.astype(bf16), lo)
                delta2 = jnp.sum(_stack_heads(dob * ob, lo), axis=2, keepdims=True)
                lse2 = jnp.max(jnp.concatenate([jnp.where(lo, lsb, MASK_VALUE), jnp.where(lo, MASK_VALUE, lsb)], axis=1),
                               axis=2, keepdims=True)
                s = _bmm_nt(q2, kb) * 0.125
                p = jnp.where(_band_mask(d), jnp.exp(s - lse2), 0.0)
                ds = (p * (_bmm_nt(do2, vb) - delta2) * 0.125).astype(bf16)
                pb = p.astype(bf16)
                dq_b = _unstack_heads(_bmm_nn(ds, kb), lo).reshape(S, 128)
                dk_b = _unband(_bmm_tn(ds, q2), d)
                dv_b = _unband(_bmm_tn(pb, do2), d)
                acc = []
                for prev, new in ((dq, dq_b), (dk, dk_b), (dv, dv_b)):
                    _unperm_store(tmp, new, d)
                    acc.append(tmp[...] if prev is None else prev + tmp[...])
                dq, dk, dv = acc
            dp_ref[...] = _rope_bwd(dq, c, a, b).astype(bf16)
            dk_s[...] = _rope_bwd(dk, c, a, b).astype(bf16)
            dv_s[...] = dv.astype(bf16)

        @pl.when(t == 1)
        def _():
            dp_ref[...] = dk_s[...]

        @pl.when(t == 2)
        def _():
            dp_ref[...] = dv_s[...]

    def col(off):
        return pl.BlockSpec((S, 128), lambda j, t, off=off: (0, off + j))

    tab = pl.BlockSpec((S, 128), lambda j, t: (0, 0))
    scr = pltpu.VMEM((S, 128), f32)
    scb = pltpu.VMEM((S, 128), bf16)
    return pl.pallas_call(
        body, out_shape=_sds((S, PROJ), bf16), grid=(NPAIR, 3),
        in_specs=[col(2), col(8), col(14), tab, tab, tab, col(0), col(0), col(2)],
        out_specs=pl.BlockSpec((S, 128), lambda j, t: (0, 2 + NPAIR * t + j)),
        scratch_shapes=[scr, scr, scr, scb, scb], name="attn_bwd",
        compiler_params=_cp("arbitrary", "arbitrary"))(proj, proj, proj, tc, ta, tb, o, lse, dmixed)


def _block_diag(w4):
    out = jnp.zeros((PW, PW), w4.dtype)
    for g in range(4):
        out = out.at[64 * g:64 * (g + 1), 64 * g:64 * (g + 1)].set(w4[g])
    return out


def _diag_blocks(w):
    return jnp.stack([w[64 * g:64 * (g + 1), 64 * g:64 * (g + 1)] for g in range(4)])


def _rope_inputs(positions):
    inv_freq = ROPE_THETA ** (-jnp.arange(0, 16, 2, dtype=f32) / 16)
    l64 = np.arange(128) % 64
    idx = np.where(l64 < 16, l64 % 8, 0)
    return positions.reshape(S, 1), inv_freq[idx].reshape(1, 128)


def _layer_fwd(x, w, small, l, tabs, dep=None, rest=None):
    g1, gm, g2 = (small[k][l].reshape(1, D) for k in ("ffn1_norm", "mix_norm", "ffn2_norm"))
    wbd = _block_diag(small["pool_w"][l])
    psc = small["pool_scale"][l].reshape(1, PW)
    x1, h1, a1, b1 = _ffn_fwd(x, g1, w["g1"], w["u1"], w["d1"], dep)
    mid_dep = None
    if rest is not None:
        more, mid_dep = rest(x1)
        w = {**w, **more}
    proj, h2 = _in_proj(x1, gm, w["wi"], mid_dep)
    mixed, diff = _pool_fwd(proj, wbd, psc)
    mixed, o, lse = _attn_fwd(proj, *tabs, mixed)
    x2 = _out_proj(x1, mixed, w["wo"])
    out, h3, a2, b2 = _ffn_fwd(x2, g2, w["g2"], w["u2"], w["d2"])
    return out, dict(x0=x, h1=h1, a1=a1, b1=b1, x1=x1, h2=h2, proj=proj, mixed=mixed, diff=diff, o=o, lse=lse,
                     x2=x2, h3=h3, a2=a2, b2=b2, g1=g1, gm=gm, g2=g2, wbd=wbd, psc=psc), w


def _layer_bwd(dx, w, sv, tabs, dep=None):
    gr, sg = {}, {}
    da, db, gr["d2"], gr["g2"], gr["u2"] = _ffn_bwd_mid(dx, sv["h3"], sv["a2"], sv["b2"], w["d2"], dep)
    dx, sg["ffn2_norm"] = _ffn_bwd_dx(dx, sv["x2"], sv["g2"], da, db, w["g2"], w["u2"])
    gr["wo"] = _dw_out(sv["mixed"], dx)
    dmixed = _out_proj_bwd(dx, w["wo"])
    dproj = _attn_bwd(sv["proj"], *tabs, sv["o"], sv["lse"], dmixed)
    dproj, dwbd, sg["pool_scale"] = _pool_bwd(dmixed, sv["diff"], sv["wbd"], sv["psc"], dproj)
    sg["pool_w"] = _diag_blocks(dwbd)
    gr["wi"] = _dw_in(sv["h2"], dproj)
    dx, sg["mix_norm"] = _in_proj_bwd_dx(dx, sv["x1"], sv["gm"], dproj, w["wi"])
    da, db, gr["d1"], gr["g1"], gr["u1"] = _ffn_bwd_mid(dx, sv["h1"], sv["a1"], sv["b1"], w["d1"])
    dx, sg["ffn1_norm"] = _ffn_bwd_dx(dx, sv["x0"], sv["g1"], da, db, w["g1"], w["u1"])
    return dx, gr, sg


def _forward_backward(x, positions, target, gathered, small):
    tabs = _rope_tables(*_rope_inputs(positions))
    saved = []
    for l in range(DEPTH):
        x, sv, _ = _layer_fwd(x, gathered[l], small, l, tabs)
        saved.append(sv)
    loss, dx, dgf = _final_loss(x, small["final_norm"].reshape(1, D), target)
    big = [None] * DEPTH
    sg = {k: [None] * DEPTH for k in ("ffn1_norm", "mix_norm", "pool_w", "pool_scale", "ffn2_norm")}
    for l in reversed(range(DEPTH)):
        dx, big[l], sgl = _layer_bwd(dx, gathered[l], saved[l], tabs)
        for k, v in sgl.items():
            sg[k][l] = v
    sg["final_norm"] = dgf
    return loss, dx, big, sg


def _place():
    x, y, c = lax.axis_index("x"), lax.axis_index("y"), lax.axis_index("c")
    chips = [(1 - x, y), (x, 1 - y), (1 - x, 1 - y)]
    return x, y, c, chips


def _cast_layer(params, l, place, dep=None):
    def body(p_ref, *refs):
        del p_ref
        for i_ref, o_ref in zip(refs[:8], refs[-8:]):
            o_ref[...] = i_ref[...].astype(bf16).reshape(o_ref.shape)

    ins, in_specs, out_shape, out_specs = [], [], [], []
    for name, rows, cols in BIG:
        q = rows // 4
        ins.append(params[BIG_SRC[name]])
        in_specs.append(pl.BlockSpec((1, q, cols), lambda i, p, l=l: (l, i, 0)))
        out_shape.append(_sds((NSH, 2, rows // 2, cols), bf16))
        out_specs.append(pl.BlockSpec((1, 1, q, cols), lambda i, p: (p[1], i // 2, i % 2, 0)))
    dspec, dop = _dep(dep)
    return pl.pallas_call(
        body, out_shape=out_shape,
        grid_spec=pltpu.PrefetchScalarGridSpec(num_scalar_prefetch=1, grid=(4,), in_specs=in_specs + dspec, out_specs=out_specs),
        name=f"cast_layer{l}", compiler_params=_cp("parallel"))(place, *ins, *dop)


HBM = pl.BlockSpec(memory_space=pltpu.HBM)
SEM = pl.BlockSpec(memory_space=pltpu.SEMAPHORE)
_SPLIT = pltpu.CompilerParams(has_side_effects=pltpu.SideEffectType.DATAFLOW_SIDE_EFFECTING)


def _hbm(arrays):
    return [pltpu.with_memory_space_constraint(a, pltpu.HBM) for a in arrays]


def _chip_copies(src_of, dst_of, send_sems, recv_sems, n):
    x, y, c, chips = _place()
    me = 2 * x + y
    out = []
    for t in range(n):
        for k, chip in enumerate(chips):
            peer = 2 * chip[0] + chip[1]
            send = pltpu.make_async_remote_copy(
                src_ref=src_of(t, peer), dst_ref=dst_of(t, me), send_sem=send_sems.at[3 * t + k], recv_sem=recv_sems.at[3 * t + k],
                device_id=(chip[0], chip[1], c), device_id_type=MESH)
            land = pltpu.make_async_remote_copy(
                src_ref=src_of(t, peer), dst_ref=dst_of(t, peer), send_sem=send_sems.at[3 * t + k], recv_sem=recv_sems.at[3 * t + k],
                device_id=(chip[0], chip[1], c), device_id_type=MESH)
            out.append((send, land))
    return out


def _exchange_start(src, land, after, src_of, dst_of, name):
    n, m = len(src), len(src) + len(land)

    def body(*refs):
        src_refs = refs[:n]
        land_refs = refs[n:m] if land else src_refs
        send_sems, recv_sems = refs[m + 1], refs[m + 2]
        token = refs[-1]
        for send, _ in _chip_copies(lambda t, s: src_of(src_refs[t], s), lambda t, s: dst_of(land_refs[t], s), send_sems, recv_sems, n):
            send.start()
        token[...] = jnp.zeros_like(token)

    arrays = list(src) + list(land)
    out_shape = ([pltpu.SemaphoreType.DMA((3 * n,)), pltpu.SemaphoreType.DMA((3 * n,))] + [pltpu.HBM(a.shape, a.dtype) for a in arrays]
                 + [_sds((8, 128), f32)])
    res = pl.pallas_call(
        body, out_shape=out_shape, in_specs=[HBM] * m + [ANY], out_specs=[SEM, SEM] + [HBM] * m + [pl.BlockSpec(memory_space=pltpu.VMEM)],
        input_output_aliases={i: 2 + i for i in range(m)}, name=name, compiler_params=_SPLIT)(*_hbm(arrays), after)
    return res[0], res[1], list(res[2:2 + n]), list(res[2 + n:2 + m]), res[-1]


def _exchange_wait(send_sems, recv_sems, src, land, after, src_of, dst_of, name):
    n, m = len(src), len(src) + len(land)

    def body(*refs):
        src_refs = refs[:n]
        land_refs = refs[n:m] if land else src_refs
        send_sems, recv_sems = refs[m], refs[m + 1]
        for send, land_cp in _chip_copies(lambda t, s: src_of(src_refs[t], s), lambda t, s: dst_of(land_refs[t], s), send_sems, recv_sems, n):
            send.wait_send()
            land_cp.wait_recv()

    arrays = list(src) + list(land)
    res = pl.pallas_call(
        body, out_shape=[pltpu.HBM(a.shape, a.dtype) for a in arrays], in_specs=[HBM] * m + [SEM, SEM, ANY], out_specs=[HBM] * m,
        input_output_aliases={i: i for i in range(m)}, name=name, compiler_params=_SPLIT)(*arrays, send_sems, recv_sems, after)
    return list(res[:n]), list(res[n:])


def _own_half(ref, s):
    x, y, c, _ = _place()
    return ref.at[2 * x + y, c]


def _slot_half(ref, s):
    return ref.at[s, lax.axis_index("c")]


def _slot(ref, s):
    return ref.at[s]


def _gather_forward(bufs):
    n = len(bufs)

    def body(*refs):
        outs = refs[n:2 * n]
        send_sems, recv_sems = refs[2 * n:]
        x, y, c, chips = _place()
        sibling = (x, y, 1 - c)
        passed = []
        for t in range(n):
            for k, chip in enumerate(chips):
                blk = outs[t].at[2 * chip[0] + chip[1], c]
                cp = pltpu.make_async_remote_copy(
                    src_ref=blk, dst_ref=blk, send_sem=send_sems.at[t, k], recv_sem=recv_sems.at[t, k],
                    device_id=sibling, device_id_type=MESH)
                cp.start()
                passed.append(cp)
        for t in range(n):
            for k, chip in enumerate(chips):
                blk = outs[t].at[2 * chip[0] + chip[1], 1 - c]
                pltpu.make_async_remote_copy(
                    src_ref=blk, dst_ref=blk, send_sem=send_sems.at[t, k], recv_sem=recv_sems.at[t, k],
                    device_id=sibling, device_id_type=MESH).wait_recv()
        for cp in passed:
            cp.wait_send()

    out_shape = [_sds(a.shape, bf16) for a in bufs]
    return pl.pallas_call(
        body, out_shape=out_shape, in_specs=[ANY] * n, out_specs=[ANY] * n, input_output_aliases={t: t for t in range(n)},
        scratch_shapes=[pltpu.SemaphoreType.DMA((n, 3)), pltpu.SemaphoreType.DMA((n, 3))], name="gather_forward")(*bufs)


def _forward_copies(refs, send_sems, recv_sems):
    x, y, c, chips = _place()
    out = []
    for t, ref in enumerate(refs):
        for k, chip in enumerate(chips):
            mine, theirs = ref.at[2 * chip[0] + chip[1], c], ref.at[2 * chip[0] + chip[1], 1 - c]
            send = pltpu.make_async_remote_copy(src_ref=mine, dst_ref=mine, send_sem=send_sems.at[3 * t + k],
                                                recv_sem=recv_sems.at[3 * t + k], device_id=(x, y, 1 - c), device_id_type=MESH)
            land = pltpu.make_async_remote_copy(src_ref=mine, dst_ref=theirs, send_sem=send_sems.at[3 * t + k],
                                                recv_sem=recv_sems.at[3 * t + k], device_id=(x, y, 1 - c), device_id_type=MESH)
            out.append((send, land))
    return out


def _forward_start(bufs, after, name):
    n = len(bufs)

    def body(*refs):
        for send, _ in _forward_copies(refs[:n], refs[n + 1], refs[n + 2]):
            send.start()
        refs[-1][...] = jnp.zeros_like(refs[-1])

    out_shape = ([pltpu.SemaphoreType.DMA((3 * n,)), pltpu.SemaphoreType.DMA((3 * n,))] + [pltpu.HBM(a.shape, a.dtype) for a in bufs]
                 + [_sds((8, 128), f32)])
    res = pl.pallas_call(
        body, out_shape=out_shape, in_specs=[HBM] * n + [ANY], out_specs=[SEM, SEM] + [HBM] * n + [pl.BlockSpec(memory_space=pltpu.VMEM)],
        input_output_aliases={i: 2 + i for i in range(n)}, name=name, compiler_params=_SPLIT)(*_hbm(bufs), after)
    return res[0], res[1], list(res[2:2 + n]), res[-1]


def _forward_wait(send_sems, recv_sems, bufs, after, name):
    n = len(bufs)

    def body(*refs):
        for send, land in _forward_copies(refs[:n], refs[n], refs[n + 1]):
            send.wait_send()
            land.wait_recv()

    res = pl.pallas_call(
        body, out_shape=[pltpu.HBM(a.shape, a.dtype) for a in bufs], in_specs=[HBM] * n + [SEM, SEM, ANY], out_specs=[HBM] * n,
        input_output_aliases={i: i for i in range(n)}, name=name, compiler_params=_SPLIT)(*bufs, send_sems, recv_sems, after)
    return list(res)


def _sibling_swap(grads):
    n = len(grads)

    def body(*refs):
        ins, outs = refs[:n], refs[n:2 * n]
        send_sems, recv_sems = refs[2 * n:]
        x, y, c, _ = _place()
        cps = []
        for t in range(n):
            for s in range(NSH):
                cp = pltpu.make_async_remote_copy(
                    src_ref=ins[t].at[s, 1 - c], dst_ref=outs[t].at[s], send_sem=send_sems.at[t, s], recv_sem=recv_sems.at[t, s],
                    device_id=(x, y, 1 - c), device_id_type=MESH)
                cp.start()
                cps.append(cp)
        for cp in cps:
            cp.wait()

    out_shape = [_sds((NSH,) + a.shape[2:], bf16) for a in grads]
    return pl.pallas_call(
        body, out_shape=out_shape, in_specs=[ANY] * n, out_specs=[ANY] * n,
        scratch_shapes=[pltpu.SemaphoreType.DMA((n, NSH)), pltpu.SemaphoreType.DMA((n, NSH))],
        name="sibling_swap")(*grads)


def _swap_copies(ins, outs, send_sems, recv_sems):
    x, y, c, _ = _place()
    out = []
    for t, (src, dst) in enumerate(zip(ins, outs)):
        for s in range(NSH):
            out.append(pltpu.make_async_remote_copy(
                src_ref=src.at[s, 1 - c], dst_ref=dst.at[s], send_sem=send_sems.at[NSH * t + s], recv_sem=recv_sems.at[NSH * t + s],
                device_id=(x, y, 1 - c), device_id_type=MESH))
    return out


def _swap_start(grads, after, name):
    n = len(grads)
    got = [lax.empty((NSH,) + a.shape[2:], bf16) for a in grads]

    def body(*refs):
        for cp in _swap_copies(refs[:n], refs[n:2 * n], refs[2 * n + 1], refs[2 * n + 2]):
            cp.start()
        refs[-1][...] = jnp.zeros_like(refs[-1])

    arrays = list(grads) + got
    out_shape = ([pltpu.SemaphoreType.DMA((n * NSH,)), pltpu.SemaphoreType.DMA((n * NSH,))]
                 + [pltpu.HBM(a.shape, a.dtype) for a in arrays] + [_sds((8, 128), f32)])
    res = pl.pallas_call(
        body, out_shape=out_shape, in_specs=[HBM] * (2 * n) + [ANY],
        out_specs=[SEM, SEM] + [HBM] * (2 * n) + [pl.BlockSpec(memory_space=pltpu.VMEM)],
        input_output_aliases={i: 2 + i for i in range(2 * n)}, name=name, compiler_params=_SPLIT)(*_hbm(arrays), after)
    return res[0], res[1], list(res[2:2 + n]), list(res[2 + n:2 + 2 * n])


def _swap_wait(send_sems, recv_sems, grads, got, after, name):
    n = len(grads)

    def body(*refs):
        for cp in _swap_copies(refs[:n], refs[n:2 * n], refs[2 * n], refs[2 * n + 1]):
            cp.wait()

    arrays = list(grads) + list(got)
    res = pl.pallas_call(
        body, out_shape=[pltpu.HBM(a.shape, a.dtype) for a in arrays], in_specs=[HBM] * (2 * n) + [SEM, SEM, ANY], out_specs=[HBM] * (2 * n),
        input_output_aliases={i: i for i in range(2 * n)}, name=name, compiler_params=_SPLIT)(*arrays, send_sems, recv_sems, after)
    return list(res[:n]), list(res[n:])


def _row_tile(h):
    return h // 2 if h % 32 == 0 else h


def _pair_sum(grads, got, c_idx):
    n = len(grads)

    def body(c_ref, *refs):
        del c_ref
        for t in range(n):
            refs[2 * n + t][...] = (refs[t][...].astype(f32).reshape(refs[n + t].shape) + refs[n + t][...].astype(f32)).astype(bf16)

    in_specs, out_shape, out_specs = [], [], []
    for a in grads:
        h, cols = a.shape[2:]
        in_specs.append(pl.BlockSpec((1, 1, _row_tile(h), cols), lambda s, i, c: (s, c[0], i, 0)))
    for a in grads:
        h, cols = a.shape[2:]
        in_specs.append(pl.BlockSpec((1, _row_tile(h), cols), lambda s, i, c: (s, i, 0)))
        out_shape.append(_sds((NSH, h, cols), bf16))
        out_specs.append(pl.BlockSpec((1, _row_tile(h), cols), lambda s, i, c: (s, i, 0)))
    return pl.pallas_call(
        body, out_shape=out_shape,
        grid_spec=pltpu.PrefetchScalarGridSpec(num_scalar_prefetch=1, grid=(NSH, 2), in_specs=in_specs, out_specs=out_specs),
        name="pair_sum", compiler_params=_cp("parallel", "parallel"))(c_idx, *grads, *got)


def _chip_sum(psum, parts, full, place, l, name):
    n = len(parts)

    def body(p_ref, *refs):
        s = pl.program_id(1)
        for t in range(n):
            val = jnp.where(s == p_ref[1], refs[t][0], refs[n + t][0]).astype(f32)
            out = refs[3 * n + t]

            @pl.when(s == 0)
            def _(out=out, val=val):
                out[0, 0] = val

            @pl.when(s != 0)
            def _(out=out, val=val):
                out[0, 0] += val

    own_specs, part_specs, out_shape, out_specs = [], [], [], []
    for a, fl in zip(parts, full):
        _, h, cols = a.shape
        r = _row_tile(h)
        own_specs.append(pl.BlockSpec((1, r, cols), lambda i, s, p: (p[1], i, 0)))
        part_specs.append(pl.BlockSpec((1, r, cols), lambda i, s, p: (jnp.where(s == p[1], (s + 1) % NSH, s), i, 0)))
        out_shape.append(_sds(fl.shape, f32))
        out_specs.append(pl.BlockSpec((1, 1, r, cols), lambda i, s, p, l=l: (l, p[0], i, 0)))
    return pl.pallas_call(
        body, out_shape=out_shape,
        grid_spec=pltpu.PrefetchScalarGridSpec(num_scalar_prefetch=1, grid=(2, NSH), in_specs=own_specs + part_specs + [ANY] * n,
                                               out_specs=out_specs),
        input_output_aliases={1 + 2 * n + t: t for t in range(n)}, name=name,
        compiler_params=_cp("parallel", "arbitrary"))(place, *psum, *parts, *full)


def _sibling_share(full, l, name):
    n = len(full)

    def body(*refs):
        outs = refs[n:2 * n]
        send_sems, recv_sems = refs[2 * n:]
        x, y, c, _ = _place()
        sibling = (x, y, 1 - c)
        cps = []
        for t in range(n):
            blk = outs[t].at[l, c]
            cp = pltpu.make_async_remote_copy(
                src_ref=blk, dst_ref=blk, send_sem=send_sems.at[t], recv_sem=recv_sems.at[t], device_id=sibling, device_id_type=MESH)
            cp.start()
            cps.append(cp)
        for t in range(n):
            blk = outs[t].at[l, 1 - c]
            pltpu.make_async_remote_copy(
                src_ref=blk, dst_ref=blk, send_sem=send_sems.at[t], recv_sem=recv_sems.at[t],
                device_id=sibling, device_id_type=MESH).wait_recv()
        for cp in cps:
            cp.wait_send()

    out_shape = [_sds(a.shape, f32) for a in full]
    return pl.pallas_call(
        body, out_shape=out_shape, in_specs=[ANY] * n, out_specs=[ANY] * n, input_output_aliases={t: t for t in range(n)},
        scratch_shapes=[pltpu.SemaphoreType.DMA((n,)), pltpu.SemaphoreType.DMA((n,))], name=name)(*full)


def _share_copies(refs, l, send_sems, recv_sems):
    x, y, c, _ = _place()
    out = []
    for t, ref in enumerate(refs):
        mine, theirs = ref.at[l, c], ref.at[l, 1 - c]
        send = pltpu.make_async_remote_copy(src_ref=mine, dst_ref=mine, send_sem=send_sems.at[t], recv_sem=recv_sems.at[t],
                                            device_id=(x, y, 1 - c), device_id_type=MESH)
        land = pltpu.make_async_remote_copy(src_ref=mine, dst_ref=theirs, send_sem=send_sems.at[t], recv_sem=recv_sems.at[t],
                                            device_id=(x, y, 1 - c), device_id_type=MESH)
        out.append((send, land))
    return out


def _share_start(full, l, after, name):
    n = len(full)

    def body(*refs):
        for send, _ in _share_copies(refs[:n], l, refs[n + 1], refs[n + 2]):
            send.start()
        refs[-1][...] = jnp.zeros_like(refs[-1])

    out_shape = ([pltpu.SemaphoreType.DMA((n,)), pltpu.SemaphoreType.DMA((n,))] + [pltpu.HBM(a.shape, a.dtype) for a in full]
                 + [_sds((8, 128), f32)])
    res = pl.pallas_call(
        body, out_shape=out_shape, in_specs=[HBM] * n + [ANY], out_specs=[SEM, SEM] + [HBM] * n + [pl.BlockSpec(memory_space=pltpu.VMEM)],
        input_output_aliases={i: 2 + i for i in range(n)}, name=name, compiler_params=_SPLIT)(*_hbm(full), after)
    return res[0], res[1], list(res[2:2 + n])


def _share_wait(send_sems, recv_sems, full, l, after, name):
    n = len(full)

    def body(*refs):
        for send, land in _share_copies(refs[:n], l, refs[n], refs[n + 1]):
            send.wait_send()
            land.wait_recv()

    res = pl.pallas_call(
        body, out_shape=[pltpu.HBM(a.shape, a.dtype) for a in full], in_specs=[HBM] * n + [SEM, SEM, ANY], out_specs=[HBM] * n,
        input_output_aliases={i: i for i in range(n)}, name=name, compiler_params=_SPLIT)(*full, send_sems, recv_sems, after)
    return list(res)


SMALL_ROWS = 656


def _pack_small(per_layer, final_vec, loss_tile):
    rows = []
    for l in range(DEPTH):
        for k in ("ffn1_norm", "mix_norm", "ffn2_norm"):
            rows.append(per_layer[k][l].reshape(8, 128))
        rows.append(per_layer["pool_w"][l].reshape(128, 128))
        rows.append(jnp.pad(per_layer["pool_scale"][l].reshape(2, 128), ((0, 6), (0, 0))))
    rows.append(final_vec.reshape(8, 128))
    rows.append(loss_tile)
    return jnp.concatenate(rows, axis=0)


def _unpack_small(buf):
    out = {k: [] for k in ("ffn1_norm", "mix_norm", "ffn2_norm", "pool_w", "pool_scale")}
    r = 0
    for l in range(DEPTH):
        for k in ("ffn1_norm", "mix_norm", "ffn2_norm"):
            out[k].append(buf[r:r + 8].reshape(D))
            r += 8
        out["pool_w"].append(buf[r:r + 128].reshape(4, 64, 64))
        r += 128
        out["pool_scale"].append(buf[r:r + 2].reshape(PW))
        r += 8
    res = {k: jnp.stack(v) for k, v in out.items()}
    res["final_norm"] = buf[r:r + 8].reshape(D)
    res["loss"] = buf[r + 8, 0]
    return res


def _allreduce_small(buf):
    def body(in_ref, out_ref, slots, send_sems, recv_sems):
        x, y, c, _ = _place()
        me = 4 * x + 2 * y + c
        slots[me] = in_ref[...]
        peers = []
        for k in range(1, 8):
            px, py, pc = x ^ (k >> 2), y ^ ((k >> 1) & 1), c ^ (k & 1)
            cp = pltpu.make_async_remote_copy(
                src_ref=in_ref, dst_ref=slots.at[me], send_sem=send_sems.at[k - 1], recv_sem=recv_sems.at[k - 1],
                device_id=(px, py, pc), device_id_type=MESH)
            cp.start()
            peers.append(cp)
        for k in range(1, 8):
            px, py, pc = x ^ (k >> 2), y ^ ((k >> 1) & 1), c ^ (k & 1)
            slot = 4 * px + 2 * py + pc
            pltpu.make_async_remote_copy(
                src_ref=slots.at[slot], dst_ref=slots.at[slot], send_sem=send_sems.at[k - 1], recv_sem=recv_sems.at[k - 1],
                device_id=(px, py, pc), device_id_type=MESH).wait_recv()
        for cp in peers:
            cp.wait_send()
        acc = slots[0]
        for j in range(1, 8):
            acc = acc + slots[j]
        out_ref[...] = acc

    return pl.pallas_call(
        body, out_shape=_sds((SMALL_ROWS, 128), f32),
        in_specs=[pl.BlockSpec(memory_space=pltpu.VMEM)], out_specs=pl.BlockSpec(memory_space=pltpu.VMEM),
        scratch_shapes=[pltpu.VMEM((8, SMALL_ROWS, 128), f32), pltpu.SemaphoreType.DMA((7,)), pltpu.SemaphoreType.DMA((7,))],
        name="allreduce_small", compiler_params=_cp())(buf)


def _adamw_math(w, g, m, v):
    m = ADAM_B1 * m + (1.0 - ADAM_B1) * g
    v = ADAM_B2 * v + (1.0 - ADAM_B2) * (g * g)
    m_hat = m / (1.0 - ADAM_B1 ** ADAM_STEP)
    v_hat = v / (1.0 - ADAM_B2 ** ADAM_STEP)
    return -ADAM_LR * (m_hat / (jnp.sqrt(v_hat) + ADAM_EPS) + ADAM_WD * w), m, v


def _adamw(w, g, m, v, name, first=0, prev=None, dep=None):
    def body(w_ref, g_ref, m_ref, v_ref, *rest):
        go_ref, d_ref, mo_ref, vo_ref = rest[-4:]
        g = g_ref[...]
        d, mn, vn = _adamw_math(w_ref[...], g, m_ref[...], v_ref[...])
        go_ref[...] = g
        d_ref[...] = d
        mo_ref[...] = mn
        vo_ref[...] = vn

    _, rows, cols = w.shape
    r = rows // 4 if rows % 32 == 0 else rows
    spec = pl.BlockSpec((1, r, cols), lambda i, j: (first + i, j, 0))
    gspec = pl.BlockSpec((1, r, cols), lambda i, j: (i, j, 0))
    out = _sds(w.shape, f32)
    extra = [] if prev is None else list(prev)
    dspec, dop = _dep(dep)
    return pl.pallas_call(
        body, out_shape=(out, out, out, out), grid=(g.shape[0], rows // r), in_specs=[spec, gspec, spec, spec] + [ANY] * len(extra) + dspec,
        out_specs=(spec,) * 4, input_output_aliases={4 + i: i for i in range(len(extra))}, name=name,
        compiler_params=_cp("parallel", "parallel"))(w, g, m, v, *extra, *dop)


SMALL_NAMES = ("ffn1_norm", "mix_norm", "pool_w", "pool_scale", "ffn2_norm", "final_norm")
WEIGHT_ORDER = ("ffn1_norm", "ffn1_w_gate", "ffn1_w_up", "ffn1_w_down", "mix_norm", "w_in", "pool_w", "pool_scale", "w_out",
                "ffn2_norm", "ffn2_w_gate", "ffn2_w_up", "ffn2_w_down", "final_norm")


def _pack_small_params(p):
    per_layer = {k: [p[k][l] for l in range(DEPTH)] for k in ("ffn1_norm", "mix_norm", "ffn2_norm", "pool_w", "pool_scale")}
    return _pack_small(per_layer, p["final_norm"], jnp.zeros((8, 128), f32))


def kernel(x, positions, ffn1_norm, ffn1_w_gate, ffn1_w_up, ffn1_w_down, mix_norm, w_in, pool_w, pool_scale, w_out, ffn2_norm, ffn2_w_gate, ffn2_w_up, ffn2_w_down, final_norm, loss_target, m_ffn1_norm, m_ffn1_w_gate, m_ffn1_w_up, m_ffn1_w_down, m_mix_norm, m_w_in, m_pool_w, m_pool_scale, m_w_out, m_ffn2_norm, m_ffn2_w_gate, m_ffn2_w_up, m_ffn2_w_down, m_final_norm, v_ffn1_norm, v_ffn1_w_gate, v_ffn1_w_up, v_ffn1_w_down, v_mix_norm, v_w_in, v_pool_w, v_pool_scale, v_w_out, v_ffn2_norm, v_ffn2_w_gate, v_ffn2_w_up, v_ffn2_w_down, v_final_norm):
    params = dict(ffn1_norm=ffn1_norm, ffn1_w_gate=ffn1_w_gate, ffn1_w_up=ffn1_w_up, ffn1_w_down=ffn1_w_down,
                  mix_norm=mix_norm, w_in=w_in, pool_w=pool_w, pool_scale=pool_scale, w_out=w_out, ffn2_norm=ffn2_norm,
                  ffn2_w_gate=ffn2_w_gate, ffn2_w_up=ffn2_w_up, ffn2_w_down=ffn2_w_down, final_norm=final_norm)
    mom_m = dict(ffn1_norm=m_ffn1_norm, ffn1_w_gate=m_ffn1_w_gate, ffn1_w_up=m_ffn1_w_up, ffn1_w_down=m_ffn1_w_down,
                 mix_norm=m_mix_norm, w_in=m_w_in, pool_w=m_pool_w, pool_scale=m_pool_scale, w_out=m_w_out,
                 ffn2_norm=m_ffn2_norm, ffn2_w_gate=m_ffn2_w_gate, ffn2_w_up=m_ffn2_w_up, ffn2_w_down=m_ffn2_w_down,
                 final_norm=m_final_norm)
    mom_v = dict(ffn1_norm=v_ffn1_norm, ffn1_w_gate=v_ffn1_w_gate, ffn1_w_up=v_ffn1_w_up, ffn1_w_down=v_ffn1_w_down,
                 mix_norm=v_mix_norm, w_in=v_w_in, pool_w=v_pool_w, pool_scale=v_pool_scale, w_out=v_w_out,
                 ffn2_norm=v_ffn2_norm, ffn2_w_gate=v_ffn2_w_gate, ffn2_w_up=v_ffn2_w_up, ffn2_w_down=v_ffn2_w_down,
                 final_norm=v_final_norm)
    names = [t[0] for t in BIG]
    for d in (params, mom_m, mom_v):
        for k in TRANSPOSED:
            d[k] = jnp.swapaxes(d[k], 1, 2)

    place = jnp.stack([lax.axis_index("c"), 2 * lax.axis_index("x") + lax.axis_index("y")]).astype(jnp.int32)
    def gather_start(tag, cast, after):
        return _exchange_start(cast, [], after, _own_half, _slot_half, f"gather_start{tag}")

    def gather_end(started, after, tag, spec):
        send_sems, recv_sems, bufs, _, _ = started
        bufs, _ = _exchange_wait(send_sems, recv_sems, bufs, [], after, _own_half, _slot_half, f"gather_wait{tag}")
        return {nm: a.reshape(NSH, rows, cols) for (nm, rows, cols), a in zip(spec, _gather_forward(bufs))}

    tabs = _rope_tables(*_rope_inputs(positions))
    h = x.reshape(S, D)
    weights, saved = [], []
    cast0 = _cast_layer(params, 0, place)
    first = gather_start("0a", cast0[:FFN1], place)
    second = gather_start("0b", cast0[FFN1:], first[-1])
    after = second[-1]
    casts, started = {}, {}
    for l in range(1, DEPTH):
        casts[l] = _cast_layer(params, l, place, after)
        after = casts[l][0]
        if l == 1:
            started[1] = gather_start(1, casts[1], second[-1])
            after = started[1][-1]
    handed = {}

    def hand_over(nxt):
        def hook(x1):
            send_sems, recv_sems, bufs, _, _ = started[nxt]
            bufs, _ = _exchange_wait(send_sems, recv_sems, bufs, [], x1, _own_half, _slot_half, f"gather_wait{nxt}")
            handed[nxt] = _forward_start(bufs, x1, f"forward_start{nxt}")
            return {}, handed[nxt][-1]
        return hook

    for l in range(DEPTH):
        if l == 0:
            w = gather_end(first, after, "0a", BIG[:FFN1])
            rest = lambda x1: (gather_end(second, x1, "0b", BIG[FFN1:]), None)
        elif l in handed:
            send_sems, recv_sems, bufs, _ = handed[l]
            bufs = _forward_wait(send_sems, recv_sems, bufs, after, f"forward_wait{l}")
            w, rest = {nm: a.reshape(NSH, rows, cols) for (nm, rows, cols), a in zip(BIG, bufs)}, None
        else:
            w, rest = gather_end(started[l], after, l, BIG), None
        if 1 <= l < DEPTH - 1:
            rest = hand_over(l + 1)
        dep = None
        if l + 2 < DEPTH:
            started[l + 2] = gather_start(l + 2, casts[l + 2], w["g1"])
            dep = started[l + 2][-1]
        h, sv, w = _layer_fwd(h, w, params, l, tabs, dep, rest)
        weights.append(w)
        saved.append(sv)
        after = h
    loss, dx, dgf = _final_loss(h, final_norm.reshape(1, D), loss_target.reshape(S, D))

    upper = [lax.empty((DEPTH - 1, 2, rows // 2, cols), f32) for _, rows, cols in BIG]
    lower = [lax.empty((1, 2, rows // 2, cols), f32) for _, rows, cols in BIG]
    sg = {k: [None] * DEPTH for k in ("ffn1_norm", "mix_norm", "pool_w", "pool_scale", "ffn2_norm")}
    sg["final_norm"] = dgf

    def reduce_end(started, after, l, full, slot):
        send_sems, recv_sems, psum, parts, _ = started
        psum, parts = _exchange_wait(send_sems, recv_sems, psum, parts, after, _slot, _slot, f"grad_wait{l}")
        return _chip_sum(psum, parts, full, place, slot, f"chip_sum{l}")

    started, dep, shares = None, None, []
    for l in reversed(range(DEPTH)):
        dx, gr, sgl = _layer_bwd(dx, weights[l], saved[l], tabs, dep)
        for k, v in sgl.items():
            sg[k][l] = v
        grads = [gr[nm] for nm in names]
        if started is None:
            got = _sibling_swap(grads)
        else:
            swap_send, swap_recv, grads, got = _swap_start(grads, place, f"swap_start{l}")
            send_sems, recv_sems, upper = _share_start(reduce_end(started, dx, l + 1, upper, l), l, place, f"share_start{l + 1}")
            shares.append((send_sems, recv_sems, l, l + 1))
            grads, got = _swap_wait(swap_send, swap_recv, grads, got, upper[0], f"swap_wait{l}")
        psum = _pair_sum(grads, got, place)
        parts = [lax.empty(a.shape, bf16) for a in psum]
        started = _exchange_start(psum, parts, place, _slot, _slot, f"grad_start{l}")
        dep = started[-1]

    big_out = {}
    for send_sems, recv_sems, slot, l in shares:
        upper = _share_wait(send_sems, recv_sems, upper, slot, place, f"share_wait{l}")
    for (nm, rows, cols), g in zip(BIG, upper):
        k = BIG_SRC[nm]
        big_out[k] = _adamw(params[k], g.reshape(DEPTH - 1, rows, cols), mom_m[k], mom_v[k], "adamw_upper_" + k, first=1, dep=dep)
        dep = big_out[k][1]
    lower = _sibling_share(reduce_end(started, dep, 0, lower, 0), 0, "sibling_share0")
    for (nm, rows, cols), g in zip(BIG, lower):
        k = BIG_SRC[nm]
        big_out[k] = _adamw(params[k], g.reshape(1, rows, cols), mom_m[k], mom_v[k], "adamw_lower_" + k, first=0, prev=big_out[k])

    per_layer = {k: sg[k] for k in ("ffn1_norm", "mix_norm", "ffn2_norm", "pool_w", "pool_scale")}
    small_sum = _allreduce_small(_pack_small(per_layer, sg["final_norm"], loss))
    gs, ds_, ms, vs = _adamw(_pack_small_params(params).reshape(1, SMALL_ROWS, 128), small_sum.reshape(1, SMALL_ROWS, 128),
                             _pack_small_params(mom_m).reshape(1, SMALL_ROWS, 128),
                             _pack_small_params(mom_v).reshape(1, SMALL_ROWS, 128), "adamw_small")
    small_out = [_unpack_small(a.reshape(SMALL_ROWS, 128)) for a in (gs, ds_, ms, vs)]

    grad, delta, new_m, new_v = {}, {}, {}, {}
    for k in WEIGHT_ORDER:
        if k in SMALL_NAMES:
            grad[k], delta[k], new_m[k], new_v[k] = (so[k] for so in small_out)
        else:
            grad[k], delta[k], new_m[k], new_v[k] = big_out[k]
    for d in (grad, delta, new_m, new_v):
        for k in TRANSPOSED:
            d[k] = jnp.swapaxes(d[k], 1, 2)
    return (small_out[0]["loss"], dx.reshape(1, S, D), *[grad[k] for k in WEIGHT_ORDER], *[delta[k] for k in WEIGHT_ORDER],
            *[new_m[k] for k in WEIGHT_ORDER], *[new_v[k] for k in WEIGHT_ORDER])
```

```python
import functools

import jax
import jax.numpy as jnp
import numpy as np
from jax import lax
from jax.experimental import pallas as pl
from jax.experimental.pallas import tpu as pltpu

f32 = jnp.float32
bf16 = jnp.bfloat16

S = 2048
D = 1024
DEPTH = 4
NSH = 4
FS = 704
PROJ = 2560
PS = 640
PW = 256
AW = 768
NPAIR = 6
NORM_EPS = 1e-6
MASK_VALUE = -1e30
ROPE_THETA = 500000.0
DILATIONS = (1, 4, 16)
QBLK = 128
NBLK = S // QBLK
TM = 512
VMEM_LIMIT = 56 * 1024 * 1024

ADAM_LR = 0.001
ADAM_B1 = 0.9
ADAM_B2 = 0.999
ADAM_EPS = 1e-08
ADAM_WD = 0.01
ADAM_STEP = 10

MESH = pl.DeviceIdType.MESH
ANY = pl.BlockSpec(memory_space=pl.ANY)

BIG = (("g1", FS, D), ("u1", FS, D), ("d1", FS, D), ("wi", D, PS), ("wo", PW, D), ("g2", FS, D), ("u2", FS, D), ("d2", FS, D))
TRANSPOSED = ("ffn1_w_gate", "ffn1_w_up", "ffn2_w_gate", "ffn2_w_up")
FFN1 = 3
BIG_SRC = {"g1": "ffn1_w_gate", "u1": "ffn1_w_up", "d1": "ffn1_w_down", "wi": "w_in", "wo": "w_out",
           "g2": "ffn2_w_gate", "u2": "ffn2_w_up", "d2": "ffn2_w_down"}


def _cp(*sem):
    return pltpu.CompilerParams(dimension_semantics=sem if sem else None, vmem_limit_bytes=VMEM_LIMIT)


def _sds(shape, dt):
    return jax.ShapeDtypeStruct(shape, dt)


def _dot(a, b):
    return jnp.dot(a, b, preferred_element_type=f32)


def _dot_nt(a, b):
    return lax.dot_general(a, b, (((1,), (1,)), ((), ())), preferred_element_type=f32)


def _dot_tn(a, b):
    return lax.dot_general(a, b, (((0,), (0,)), ((), ())), preferred_element_type=f32)


def _dep(dep):
    return ([], []) if dep is None else ([ANY], [dep])


def _resident(shape):
    return pl.BlockSpec(shape, lambda i: (0,) * len(shape), pipeline_mode=pl.Buffered(1))


def _ffn_fwd(x, g, wg, wu, wd, dep=None):
    def body(x_ref, g_ref, wg_ref, wu_ref, wd_ref, *rest):
        xo_ref, h_ref, a_ref, b_ref = rest[-4:]
        xf = x_ref[...]
        r = lax.rsqrt(jnp.mean(xf * xf, axis=-1, keepdims=True) + NORM_EPS)
        hh = ((xf * r) * g_ref[...]).astype(bf16)
        h_ref[...] = hh
        acc = None
        for s in range(NSH):
            a = _dot_nt(hh, wg_ref[s])
            b = _dot_nt(hh, wu_ref[s])
            a_ref[s] = a.astype(bf16)
            b_ref[s] = b.astype(bf16)
            p = _dot((a * (1.0 / (1.0 + jnp.exp(-a))) * b).astype(bf16), wd_ref[s])
            acc = p if acc is None else acc + p
        xo_ref[...] = xf + 0.5 * acc

    tok = pl.BlockSpec((TM, D), lambda i: (i, 0))
    hid = pl.BlockSpec((NSH, TM, FS), lambda i: (0, i, 0))
    wsp = _resident((NSH, FS, D))
    dspec, dop = _dep(dep)
    return pl.pallas_call(
        body, out_shape=(_sds((S, D), f32), _sds((S, D), bf16), _sds((NSH, S, FS), bf16), _sds((NSH, S, FS), bf16)),
        grid=(S // TM,), in_specs=[tok, pl.BlockSpec((1, D), lambda i: (0, 0)), wsp, wsp, wsp] + dspec,
        out_specs=(tok, tok, hid, hid), name="ffn_fwd", compiler_params=_cp("parallel"))(x, g, wg, wu, wd, *dop)


def _in_proj(x, g, wi):
    def body(x_ref, g_ref, w_ref, o_ref, h_ref):
        xf = x_ref[...]
        r = lax.rsqrt(jnp.mean(xf * xf, axis=-1, keepdims=True) + NORM_EPS)
        hh = ((xf * r) * g_ref[...]).astype(bf16)
        h_ref[...] = hh
        for s in range(NSH):
            o_ref[:, PS * s:PS * (s + 1)] = _dot(hh, w_ref[s])

    tok = pl.BlockSpec((TM, D), lambda i: (i, 0))
    return pl.pallas_call(
        body, out_shape=(_sds((S, PROJ), f32), _sds((S, D), bf16)), grid=(S // TM,),
        in_specs=[tok, pl.BlockSpec((1, D), lambda i: (0, 0)), _resident((NSH, D, PS))],
        out_specs=(pl.BlockSpec((TM, PROJ), lambda i: (i, 0)), tok), name="in_proj", compiler_params=_cp("parallel"))(x, g, wi)


def _out_proj(x, mixed, wo):
    def body(x_ref, m_ref, w_ref, o_ref):
        o_ref[...] = x_ref[...] + _dot(m_ref[...], w_ref[...].reshape(D, D))

    return pl.pallas_call(
        body, out_shape=_sds((S, D), f32), grid=(S // TM,),
        in_specs=[pl.BlockSpec((TM, D), lambda i: (i, 0)), pl.BlockSpec((TM, D), lambda i: (i, 0)),
                  pl.BlockSpec((NSH, PW, D), lambda i: (0, 0, 0))],
        out_specs=pl.BlockSpec((TM, D), lambda i: (i, 0)), name="out_proj", compiler_params=_cp("parallel"))(x, mixed, wo)


def _out_proj_bwd(dx, wo):
    def body(dx_ref, w_ref, o_ref):
        o_ref[...] = _dot_nt(dx_ref[...].astype(bf16), w_ref[...].reshape(D, D))

    return pl.pallas_call(
        body, out_shape=_sds((S, D), f32), grid=(S // TM,),
        in_specs=[pl.BlockSpec((TM, D), lambda i: (i, 0)), pl.BlockSpec((NSH, PW, D), lambda i: (0, 0, 0))],
        out_specs=pl.BlockSpec((TM, D), lambda i: (i, 0)), name="out_proj_bwd", compiler_params=_cp("parallel"))(dx, wo)


def _ffn_bwd_mid(dx, h, a, b, wd, dep=None):
    nt = S // TM

    def body(dx_ref, h_ref, a_ref, b_ref, wd_ref, *rest):
        da_ref, db_ref, dwd_ref, dwg_ref, dwu_ref, dy_s, u_s, da_s, db_s = rest[-9:]
        i = pl.program_id(1)
        rows = pl.ds(pl.multiple_of(i * TM, TM), TM)
        dy = (0.5 * dx_ref[...]).astype(bf16)
        dy_s[rows, :] = dy
        du = _dot_nt(dy, wd_ref[0])
        a = a_ref[0].astype(f32)
        b = b_ref[0].astype(f32)
        sig = 1.0 / (1.0 + jnp.exp(-a))
        silu = a * sig
        da = (du * b * (sig * (1.0 + a * (1.0 - sig)))).astype(bf16)
        db = (du * silu).astype(bf16)
        da_ref[0] = da
        db_ref[0] = db
        da_s[rows, :] = da
        db_s[rows, :] = db
        u_s[rows, :] = (silu * b).astype(bf16)

        @pl.when(i == nt - 1)
        def _():
            hh = h_ref[...]
            dwd_ref[...] = _dot_tn(u_s[...], dy_s[...]).astype(bf16).reshape(dwd_ref.shape)
            dwg_ref[...] = _dot_tn(da_s[...], hh).astype(bf16).reshape(dwg_ref.shape)
            dwu_ref[...] = _dot_tn(db_s[...], hh).astype(bf16).reshape(dwu_ref.shape)

    tok = pl.BlockSpec((TM, D), lambda s, i: (i, 0))
    hid = pl.BlockSpec((1, TM, FS), lambda s, i: (s, i, 0))
    wsp = pl.BlockSpec((1, 2, FS // 2, D), lambda s, i: (s, 0, 0, 0))
    hidden = _sds((NSH, S, FS), bf16)
    wgrad = _sds((NSH, 2, FS // 2, D), bf16)
    whole = pltpu.VMEM((S, FS), bf16)
    dspec, dop = _dep(dep)
    return pl.pallas_call(
        body, out_shape=(hidden, hidden, wgrad, wgrad, wgrad), grid=(NSH, nt),
        in_specs=[tok, pl.BlockSpec((S, D), lambda s, i: (0, 0), pipeline_mode=pl.Buffered(1)), hid, hid,
                  pl.BlockSpec((1, FS, D), lambda s, i: (s, 0, 0))] + dspec,
        out_specs=(hid, hid, wsp, wsp, wsp), scratch_shapes=[pltpu.VMEM((S, D), bf16), whole, whole, whole],
        name="ffn_bwd_mid", compiler_params=_cp("parallel", "arbitrary"))(dx, h, a, b, wd, *dop)


def _norm_bwd_tail(acc, x_ref, dxin_ref, g_ref, dxo_ref, dg_ref, first):
    xf = x_ref[...]
    r = lax.rsqrt(jnp.mean(xf * xf, axis=-1, keepdims=True) + NORM_EPS)
    xhat = xf * r
    dhg = acc * g_ref[...]
    dxo_ref[...] = dxin_ref[...] + r * (dhg - xhat * jnp.mean(dhg * xhat, axis=-1, keepdims=True))
    part = jnp.sum(acc * xhat, axis=0, keepdims=True)

    @pl.when(first)
    def _():
        dg_ref[...] = part

    @pl.when(jnp.logical_not(first))
    def _():
        dg_ref[...] += part


def _ffn_bwd_dx(dx, x_in, g, da, db, wg, wu):
    def body(dx_ref, x_ref, g_ref, da_ref, db_ref, wg_ref, wu_ref, dxo_ref, dg_ref):
        acc = None
        for s in range(NSH):
            p = _dot(da_ref[s], wg_ref[s])
            acc = p if acc is None else acc + p
            acc = acc + _dot(db_ref[s], wu_ref[s])
        _norm_bwd_tail(acc, x_ref, dx_ref, g_ref, dxo_ref, dg_ref, pl.program_id(0) == 0)

    tok = pl.BlockSpec((TM, D), lambda i: (i, 0))
    vec = pl.BlockSpec((1, D), lambda i: (0, 0))
    hid = pl.BlockSpec((NSH, TM, FS), lambda i: (0, i, 0))
    wsp = _resident((NSH, FS, D))
    return pl.pallas_call(
        body, out_shape=(_sds((S, D), f32), _sds((1, D), f32)), grid=(S // TM,),
        in_specs=[tok, tok, vec, hid, hid, wsp, wsp], out_specs=(tok, vec),
        name="ffn_bwd_dx", compiler_params=_cp("arbitrary"))(dx, x_in, g, da, db, wg, wu)


def _in_proj_bwd_dx(dx, x_in, g, dproj, wi):
    def body(dx_ref, x_ref, g_ref, dp_ref, w_ref, dxo_ref, dg_ref):
        acc = None
        for s in range(NSH):
            p = _dot_nt(dp_ref[:, PS * s:PS * (s + 1)], w_ref[s])
            acc = p if acc is None else acc + p
        _norm_bwd_tail(acc, x_ref, dx_ref, g_ref, dxo_ref, dg_ref, pl.program_id(0) == 0)

    tok = pl.BlockSpec((TM, D), lambda i: (i, 0))
    vec = pl.BlockSpec((1, D), lambda i: (0, 0))
    return pl.pallas_call(
        body, out_shape=(_sds((S, D), f32), _sds((1, D), f32)), grid=(S // TM,),
        in_specs=[tok, tok, vec, pl.BlockSpec((TM, PROJ), lambda i: (i, 0)), _resident((NSH, D, PS))], out_specs=(tok, vec),
        name="in_proj_bwd_dx", compiler_params=_cp("arbitrary"))(dx, x_in, g, dproj, wi)


def _dw(lhs, rhs, lhs_spec, rhs_spec, rows, cols, name, cast_rhs=False):
    def body(l_ref, r_ref, o_ref):
        r = r_ref[...].astype(bf16) if cast_rhs else r_ref[...]
        o_ref[...] = _dot_tn(l_ref[...], r).astype(bf16).reshape(1, 2, rows // 2, cols)

    return pl.pallas_call(
        body, out_shape=_sds((NSH, 2, rows // 2, cols), bf16), grid=(NSH,), in_specs=[lhs_spec, rhs_spec],
        out_specs=pl.BlockSpec((1, 2, rows // 2, cols), lambda s: (s, 0, 0, 0)), name=name, compiler_params=_cp("parallel"))(lhs, rhs)


_WHOLE_TOK = pl.BlockSpec((S, D), lambda s: (0, 0))


def _dw_in(h, dproj):
    return _dw(h, dproj, _WHOLE_TOK, pl.BlockSpec((S, PS), lambda s: (0, s)), D, PS, "dw_in")


def _dw_out(mixed, dx):
    return _dw(mixed, dx, pl.BlockSpec((S, PW), lambda s: (0, s)), _WHOLE_TOK, PW, D, "dw_out", cast_rhs=True)


def _final_loss(x, g, target):
    def body(x_ref, g_ref, t_ref, loss_ref, dx_ref, dg_ref):
        i = pl.program_id(0)
        xf = x_ref[...]
        r = lax.rsqrt(jnp.mean(xf * xf, axis=-1, keepdims=True) + NORM_EPS)
        xhat = xf * r
        err = xhat * g_ref[...] - t_ref[...]
        dy = err * (1.0 / D)
        dhg = dy * g_ref[...]
        dx_ref[...] = r * (dhg - xhat * jnp.mean(dhg * xhat, axis=-1, keepdims=True))
        part = jnp.sum(dy * xhat, axis=0, keepdims=True)
        lpart = jnp.zeros((8, 128), f32) + 0.5 * jnp.sum(jnp.mean(err * err, axis=-1, keepdims=True))

        @pl.when(i == 0)
        def _():
            dg_ref[...] = part
            loss_ref[...] = lpart

        @pl.when(i != 0)
        def _():
            dg_ref[...] += part
            loss_ref[...] += lpart

    tok = pl.BlockSpec((TM, D), lambda i: (i, 0))
    vec = pl.BlockSpec((1, D), lambda i: (0, 0))
    return pl.pallas_call(
        body, out_shape=(_sds((8, 128), f32), _sds((S, D), f32), _sds((1, D), f32)), grid=(S // TM,),
        in_specs=[tok, vec, tok], out_specs=(pl.BlockSpec((8, 128), lambda i: (0, 0)), tok, vec),
        name="final_loss", compiler_params=_cp("arbitrary"))(x, g, target)


def _shift_down(x, k, row):
    return jnp.where(row >= k, pltpu.roll(x, k, axis=0), 0.0)


def _shift_up(x, k, row):
    return jnp.where(row < S - k, pltpu.roll(x, S - k, axis=0), 0.0)


def _pool_geometry():
    row = lax.broadcasted_iota(jnp.int32, (S, PW), 0)
    grp = lax.broadcasted_iota(jnp.int32, (S, PW), 1) // 64
    half = jnp.where(grp == 0, 1, jnp.where(grp == 1, 2, jnp.where(grp == 2, 4, 8)))
    hi = jnp.minimum(row + half - 1, S - 1)
    lo = jnp.maximum(row - half, 0)
    return row, grp, (hi - lo + 1).astype(f32)


def _by_group(grp, v0, v1, v2, v3):
    return jnp.where(grp == 0, v0, jnp.where(grp == 1, v1, jnp.where(grp == 2, v2, v3)))


def _window_sums(x, row, grp, transpose):
    l1, r1 = x, x
    l2, r2 = l1 + _shift_down(l1, 1, row), r1 + _shift_up(r1, 1, row)
    l4, r4 = l2 + _shift_down(l2, 2, row), r2 + _shift_up(r2, 2, row)
    l8, r8 = l4 + _shift_down(l4, 4, row), r4 + _shift_up(r4, 4, row)
    lsel = _by_group(grp, l1, l2, l4, l8)
    rsel = _by_group(grp, r1, r2, r4, r8)
    if transpose:
        return lsel + _shift_up(rsel, 1, row)
    return _shift_down(lsel, 1, row) + rsel


def _pool_fwd(proj, wbd, scale):
    def body(v_ref, w_ref, sc_ref, mixed_ref, diff_ref):
        row, grp, cnt = _pool_geometry()
        v = v_ref[...]
        diff = (_window_sums(v, row, grp, False) / cnt - v).astype(bf16)
        diff_ref[...] = diff
        mixed_ref[...] = (_dot(diff, w_ref[...].astype(bf16)) * sc_ref[...]).astype(bf16)

    col = pl.BlockSpec((S, PW), lambda i: (0, 0))
    return pl.pallas_call(
        body, out_shape=(_sds((S, D), bf16), _sds((S, PW), bf16)), grid=(1,),
        in_specs=[col, pl.BlockSpec((PW, PW), lambda i: (0, 0)), pl.BlockSpec((1, PW), lambda i: (0, 0))],
        out_specs=(col, col), name="pool_fwd", compiler_params=_cp("arbitrary"))(proj, wbd, scale)


def _pool_bwd(dmixed, diff, wbd, scale, dproj):
    def body(dy_ref, diff_ref, w_ref, sc_ref, dproj_in, dv_ref, dw_ref, dsc_ref):
        del dproj_in
        row, grp, cnt = _pool_geometry()
        dy = dy_ref[...]
        diff = diff_ref[...]
        w = w_ref[...].astype(bf16)
        dsc_ref[...] = jnp.sum(dy * _dot(diff, w), axis=0, keepdims=True)
        dys = (dy * sc_ref[...]).astype(bf16)
        dw_ref[...] = _dot_tn(diff, dys)
        ddiff = _dot_nt(dys, w)
        dv_ref[...] = (_window_sums(ddiff / cnt, row, grp, True) - ddiff).astype(bf16)

    col = pl.BlockSpec((S, PW), lambda i: (0, 0))
    return pl.pallas_call(
        body, out_shape=(_sds((S, PROJ), bf16), _sds((PW, PW), f32), _sds((1, PW), f32)), grid=(1,),
        in_specs=[col, col, pl.BlockSpec((PW, PW), lambda i: (0, 0)), pl.BlockSpec((1, PW), lambda i: (0, 0)), ANY],
        out_specs=(col, pl.BlockSpec((PW, PW), lambda i: (0, 0)), pl.BlockSpec((1, PW), lambda i: (0, 0))),
        input_output_aliases={4: 0}, name="pool_bwd", compiler_params=_cp("arbitrary"))(dmixed, diff, wbd, scale, dproj)


def _rope_tables(pos_col, freq_row):
    def body(p_ref, f_ref, c_ref, a_ref, b_ref):
        ang = p_ref[...].astype(f32) * f_ref[...]
        l64 = lax.broadcasted_iota(jnp.int32, (S, 128), 1) % 64
        cos, sin = jnp.cos(ang), jnp.sin(ang)
        c_ref[...] = jnp.where(l64 < 16, cos, 1.0)
        a_ref[...] = jnp.where(l64 < 8, -sin, 0.0)
        b_ref[...] = jnp.where((l64 >= 8) & (l64 < 16), sin, 0.0)

    t = _sds((S, 128), f32)
    return pl.pallas_call(body, out_shape=(t, t, t), name="rope_tables", compiler_params=_cp())(pos_col, freq_row)


def _rope(t, c, a, b):
    return t * c + pltpu.roll(t, 120, axis=1) * a + pltpu.roll(t, 8, axis=1) * b


def _rope_bwd(g, c, a, b):
    return g * c + pltpu.roll(g * a, 8, axis=1) + pltpu.roll(g * b, 120, axis=1)


def _perm_load(ref, d):
    if d == 1:
        return ref[...]
    n = S // d
    return jnp.concatenate([ref[pl.ds(r, n, stride=d), :] for r in range(d)], axis=0)


def _unperm_store(ref, val, d):
    if d == 1:
        ref[...] = val
        return
    n = S // d
    for r in range(d):
        ref[pl.ds(r, n, stride=d), :] = val[r * n:(r + 1) * n, :]


def _band(xp, d):
    if d == NBLK:
        return xp.reshape(NBLK, QBLK, 128)
    z = jnp.zeros((64, 128), bf16)
    p = jnp.concatenate([z, xp, z], axis=0).reshape(NBLK + 1, QBLK, 128)
    return jnp.concatenate([p[:NBLK], p[1:]], axis=1)


def _unband(xb, d):
    if d == NBLK:
        return xb.reshape(S, 128)
    z = jnp.zeros((1, QBLK, 128), f32)
    p = jnp.concatenate([xb[:, :QBLK], z], axis=0) + jnp.concatenate([z, xb[:, QBLK:]], axis=0)
    return p.reshape(S + QBLK, 128)[64:S + 64]


def _band_mask(d):
    if d == NBLK:
        a = lax.broadcasted_iota(jnp.int32, (1, 2 * QBLK, QBLK), 1) & (QBLK - 1)
        b = lax.broadcasted_iota(jnp.int32, (1, 2 * QBLK, QBLK), 2)
        return (b >= a - 64) & (b <= a + 64)
    blocks_per_class = NBLK // d
    n = lax.broadcasted_iota(jnp.int32, (NBLK, 1, 2 * QBLK), 0) & (blocks_per_class - 1)
    be = lax.broadcasted_iota(jnp.int32, (NBLK, 1, 2 * QBLK), 2)
    a = lax.broadcasted_iota(jnp.int32, (1, 2 * QBLK, 2 * QBLK), 1) & (QBLK - 1)
    b = lax.broadcasted_iota(jnp.int32, (1, 2 * QBLK, 2 * QBLK), 2)
    band = (b >= a) & (b <= a + 128)
    edge = ((be >= 64) | (n != 0)) & ((be < QBLK + 64) | (n != blocks_per_class - 1))
    return band & edge


def _stack_heads(xb, lo):
    z = jnp.zeros_like(xb)
    return jnp.concatenate([jnp.where(lo, xb, z), jnp.where(lo, z, xb)], axis=1)


def _unstack_heads(x2, lo):
    return jnp.where(lo, x2[:, :QBLK], x2[:, QBLK:])


def _rows_to_lanes(col2, lo):
    return jnp.where(lo, jnp.broadcast_to(col2[:, :QBLK], (NBLK, QBLK, 128)), jnp.broadcast_to(col2[:, QBLK:], (NBLK, QBLK, 128)))


def _bmm_nt(a, b):
    return jnp.einsum('nqd,nkd->nqk', a, b, preferred_element_type=f32)


def _bmm_nn(a, b):
    return jnp.einsum('nqk,nkd->nqd', a, b, preferred_element_type=f32)


def _bmm_tn(a, b):
    return jnp.einsum('nqk,nqd->nkd', a, b, preferred_element_type=f32)


def _attn_fwd(proj, tc, ta, tb, mixed):
    def body(q_ref, k_ref, v_ref, c_ref, a_ref, b_ref, mixed_in, mixed_ref, o_ref, lse_ref, qn, kn, t_num, t_m, t_den):
        del mixed_in
        lo = lax.broadcasted_iota(jnp.int32, (1, 1, 128), 2) < 64
        c, a, b = c_ref[...], a_ref[...], b_ref[...]
        qn[...] = _rope(q_ref[...], c, a, b)
        kn[...] = _rope(k_ref[...], c, a, b)
        run = None
        for d in DILATIONS:
            q2 = _stack_heads(_perm_load(qn, d).astype(bf16).reshape(NBLK, QBLK, 128), lo)
            kb = _band(_perm_load(kn, d).astype(bf16), d)
            vb = _band(_perm_load(v_ref, d).astype(bf16), d)
            s = jnp.where(_band_mask(d), _bmm_nt(q2, kb) * 0.125, MASK_VALUE)
            m = jnp.max(s, axis=2, keepdims=True)
            p = jnp.exp(s - m)
            den = jnp.sum(p, axis=2, keepdims=True)
            num = _unstack_heads(_bmm_nn(p.astype(bf16), vb), lo)
            _unperm_store(t_num, num.reshape(S, 128), d)
            _unperm_store(t_m, _rows_to_lanes(m, lo).reshape(S, 128), d)
            _unperm_store(t_den, _rows_to_lanes(den, lo).reshape(S, 128), d)
            if run is None:
                run = (t_m[...], t_num[...], t_den[...])
            else:
                m_new = jnp.maximum(run[0], t_m[...])
                w_old, w_new = jnp.exp(run[0] - m_new), jnp.exp(t_m[...] - m_new)
                run = (m_new, w_old * run[1] + w_new * t_num[...], w_old * run[2] + w_new * t_den[...])
        out = run[1] / run[2]
        o_ref[...] = out
        mixed_ref[...] = out.astype(bf16)
        lse_ref[...] = run[0] + jnp.log(run[2])

    def col(off):
        return pl.BlockSpec((S, 128), lambda j, off=off: (0, off + j))

    tab = pl.BlockSpec((S, 128), lambda j: (0, 0))
    scr = pltpu.VMEM((S, 128), f32)
    return pl.pallas_call(
        body, out_shape=(_sds((S, D), bf16), _sds((S, AW), f32), _sds((S, AW), f32)), grid=(NPAIR,),
        in_specs=[col(2), col(8), col(14), tab, tab, tab, ANY], out_specs=(col(2), col(0), col(0)),
        scratch_shapes=[scr, scr, scr, scr, scr], input_output_aliases={6: 0}, name="attn_fwd",
        compiler_params=_cp("arbitrary"))(proj, proj, proj, tc, ta, tb, mixed)


def _attn_bwd(proj, tc, ta, tb, o, lse, dmixed):
    def body(q_ref, k_ref, v_ref, c_ref, a_ref, b_ref, o_ref, lse_ref, do_ref, dp_ref, qn, kn, tmp, dk_s, dv_s):
        t = pl.program_id(1)

        @pl.when(t == 0)
        def _():
            lo = lax.broadcasted_iota(jnp.int32, (1, 1, 128), 2) < 64
            c, a, b = c_ref[...], a_ref[...], b_ref[...]
            qn[...] = _rope(q_ref[...], c, a, b)
            kn[...] = _rope(k_ref[...], c, a, b)
            dq = dk = dv = None
            for d in DILATIONS:
                q2 = _stack_heads(_perm_load(qn, d).astype(bf16).reshape(NBLK, QBLK, 128), lo)
                kb = _band(_perm_load(kn, d).astype(bf16), d)
                vb = _band(_perm_load(v_ref, d).astype(bf16), d)
                dob = _perm_load(do_ref, d).reshape(NBLK, QBLK, 128)
                ob = _perm_load(o_ref, d).reshape(NBLK, QBLK, 128)
                lsb = _perm_load(lse_ref, d).reshape(NBLK, QBLK, 128)
                do2 = _stack_heads(dob.astype(bf16), lo)
                delta2 = jnp.sum(_stack_heads(dob * ob, lo), axis=2, keepdims=True)
                lse2 = jnp.max(jnp.concatenate([jnp.where(lo, lsb, MASK_VALUE), jnp.where(lo, MASK_VALUE, lsb)], axis=1),
                               axis=2, keepdims=True)
                s = _bmm_nt(q2, kb) * 0.125
                p = jnp.where(_band_mask(d), jnp.exp(s - lse2), 0.0)
                ds = (p * (_bmm_nt(do2, vb) - delta2) * 0.125).astype(bf16)
                pb = p.astype(bf16)
                dq_b = _unstack_heads(_bmm_nn(ds, kb), lo).reshape(S, 128)
                dk_b = _unband(_bmm_tn(ds, q2), d)
                dv_b = _unband(_bmm_tn(pb, do2), d)
                acc = []
                for prev, new in ((dq, dq_b), (dk, dk_b), (dv, dv_b)):
                    _unperm_store(tmp, new, d)
                    acc.append(tmp[...] if prev is None else prev + tmp[...])
                dq, dk, dv = acc
            dp_ref[...] = _rope_bwd(dq, c, a, b).astype(bf16)
            dk_s[...] = _rope_bwd(dk, c, a, b).astype(bf16)
            dv_s[...] = dv.astype(bf16)

        @pl.when(t == 1)
        def _():
            dp_ref[...] = dk_s[...]

        @pl.when(t == 2)
        def _():
            dp_ref[...] = dv_s[...]

    def col(off):
        return pl.BlockSpec((S, 128), lambda j, t, off=off: (0, off + j))

    tab = pl.BlockSpec((S, 128), lambda j, t: (0, 0))
    scr = pltpu.VMEM((S, 128), f32)
    scb = pltpu.VMEM((S, 128), bf16)
    return pl.pallas_call(
        body, out_shape=_sds((S, PROJ), bf16), grid=(NPAIR, 3),
        in_specs=[col(2), col(8), col(14), tab, tab, tab, col(0), col(0), col(2)],
        out_specs=pl.BlockSpec((S, 128), lambda j, t: (0, 2 + NPAIR * t + j)),
        scratch_shapes=[scr, scr, scr, scb, scb], name="attn_bwd",
        compiler_params=_cp("arbitrary", "arbitrary"))(proj, proj, proj, tc, ta, tb, o, lse, dmixed)


def _block_diag(w4):
    out = jnp.zeros((PW, PW), w4.dtype)
    for g in range(4):
        out = out.at[64 * g:64 * (g + 1), 64 * g:64 * (g + 1)].set(w4[g])
    return out


def _diag_blocks(w):
    return jnp.stack([w[64 * g:64 * (g + 1), 64 * g:64 * (g + 1)] for g in range(4)])


def _rope_inputs(positions):
    inv_freq = ROPE_THETA ** (-jnp.arange(0, 16, 2, dtype=f32) / 16)
    l64 = np.arange(128) % 64
    idx = np.where(l64 < 16, l64 % 8, 0)
    return positions.reshape(S, 1), inv_freq[idx].reshape(1, 128)


def _layer_fwd(x, w, small, l, tabs, dep=None, rest=None):
    g1, gm, g2 = (small[k][l].reshape(1, D) for k in ("ffn1_norm", "mix_norm", "ffn2_norm"))
    wbd = _block_diag(small["pool_w"][l])
    psc = small["pool_scale"][l].reshape(1, PW)
    x1, h1, a1, b1 = _ffn_fwd(x, g1, w["g1"], w["u1"], w["d1"], dep)
    if rest is not None:
        w = {**w, **rest(x1)}
    proj, h2 = _in_proj(x1, gm, w["wi"])
    mixed, diff = _pool_fwd(proj, wbd, psc)
    mixed, o, lse = _attn_fwd(proj, *tabs, mixed)
    x2 = _out_proj(x1, mixed, w["wo"])
    out, h3, a2, b2 = _ffn_fwd(x2, g2, w["g2"], w["u2"], w["d2"])
    return out, dict(x0=x, h1=h1, a1=a1, b1=b1, x1=x1, h2=h2, proj=proj, mixed=mixed, diff=diff, o=o, lse=lse,
                     x2=x2, h3=h3, a2=a2, b2=b2, g1=g1, gm=gm, g2=g2, wbd=wbd, psc=psc), w


def _layer_bwd(dx, w, sv, tabs, dep=None):
    gr, sg = {}, {}
    da, db, gr["d2"], gr["g2"], gr["u2"] = _ffn_bwd_mid(dx, sv["h3"], sv["a2"], sv["b2"], w["d2"], dep)
    dx, sg["ffn2_norm"] = _ffn_bwd_dx(dx, sv["x2"], sv["g2"], da, db, w["g2"], w["u2"])
    gr["wo"] = _dw_out(sv["mixed"], dx)
    dmixed = _out_proj_bwd(dx, w["wo"])
    dproj = _attn_bwd(sv["proj"], *tabs, sv["o"], sv["lse"], dmixed)
    dproj, dwbd, sg["pool_scale"] = _pool_bwd(dmixed, sv["diff"], sv["wbd"], sv["psc"], dproj)
    sg["pool_w"] = _diag_blocks(dwbd)
    gr["wi"] = _dw_in(sv["h2"], dproj)
    dx, sg["mix_norm"] = _in_proj_bwd_dx(dx, sv["x1"], sv["gm"], dproj, w["wi"])
    da, db, gr["d1"], gr["g1"], gr["u1"] = _ffn_bwd_mid(dx, sv["h1"], sv["a1"], sv["b1"], w["d1"])
    dx, sg["ffn1_norm"] = _ffn_bwd_dx(dx, sv["x0"], sv["g1"], da, db, w["g1"], w["u1"])
    return dx, gr, sg


def _forward_backward(x, positions, target, gathered, small):
    tabs = _rope_tables(*_rope_inputs(positions))
    saved = []
    for l in range(DEPTH):
        x, sv, _ = _layer_fwd(x, gathered[l], small, l, tabs)
        saved.append(sv)
    loss, dx, dgf = _final_loss(x, small["final_norm"].reshape(1, D), target)
    big = [None] * DEPTH
    sg = {k: [None] * DEPTH for k in ("ffn1_norm", "mix_norm", "pool_w", "pool_scale", "ffn2_norm")}
    for l in reversed(range(DEPTH)):
        dx, big[l], sgl = _layer_bwd(dx, gathered[l], saved[l], tabs)
        for k, v in sgl.items():
            sg[k][l] = v
    sg["final_norm"] = dgf
    return loss, dx, big, sg


def _place():
    x, y, c = lax.axis_index("x"), lax.axis_index("y"), lax.axis_index("c")
    chips = [(1 - x, y), (x, 1 - y), (1 - x, 1 - y)]
    return x, y, c, chips


def _cast_layer(params, l, place, dep=None):
    def body(p_ref, *refs):
        del p_ref
        for i_ref, o_ref in zip(refs[:8], refs[-8:]):
            o_ref[...] = i_ref[...].astype(bf16).reshape(o_ref.shape)

    ins, in_specs, out_shape, out_specs = [], [], [], []
    for name, rows, cols in BIG:
        q = rows // 4
        ins.append(params[BIG_SRC[name]])
        in_specs.append(pl.BlockSpec((1, q, cols), lambda i, p, l=l: (l, i, 0)))
        out_shape.append(_sds((NSH, 2, rows // 2, cols), bf16))
        out_specs.append(pl.BlockSpec((1, 1, q, cols), lambda i, p: (p[1], i // 2, i % 2, 0)))
    dspec, dop = _dep(dep)
    return pl.pallas_call(
        body, out_shape=out_shape,
        grid_spec=pltpu.PrefetchScalarGridSpec(num_scalar_prefetch=1, grid=(4,), in_specs=in_specs + dspec, out_specs=out_specs),
        name=f"cast_layer{l}", compiler_params=_cp("parallel"))(place, *ins, *dop)


HBM = pl.BlockSpec(memory_space=pltpu.HBM)
SEM = pl.BlockSpec(memory_space=pltpu.SEMAPHORE)
_SPLIT = pltpu.CompilerParams(has_side_effects=pltpu.SideEffectType.DATAFLOW_SIDE_EFFECTING)


def _hbm(arrays):
    return [pltpu.with_memory_space_constraint(a, pltpu.HBM) for a in arrays]


def _chip_copies(src_of, dst_of, send_sems, recv_sems, n):
    x, y, c, chips = _place()
    me = 2 * x + y
    out = []
    for t in range(n):
        for k, chip in enumerate(chips):
            peer = 2 * chip[0] + chip[1]
            send = pltpu.make_async_remote_copy(
                src_ref=src_of(t, peer), dst_ref=dst_of(t, me), send_sem=send_sems.at[3 * t + k], recv_sem=recv_sems.at[3 * t + k],
                device_id=(chip[0], chip[1], c), device_id_type=MESH)
            land = pltpu.make_async_remote_copy(
                src_ref=src_of(t, peer), dst_ref=dst_of(t, peer), send_sem=send_sems.at[3 * t + k], recv_sem=recv_sems.at[3 * t + k],
                device_id=(chip[0], chip[1], c), device_id_type=MESH)
            out.append((send, land))
    return out


def _exchange_start(src, land, after, src_of, dst_of, name):
    n, m = len(src), len(src) + len(land)

    def body(*refs):
        src_refs = refs[:n]
        land_refs = refs[n:m] if land else src_refs
        send_sems, recv_sems = refs[m + 1], refs[m + 2]
        token = refs[-1]
        for send, _ in _chip_copies(lambda t, s: src_of(src_refs[t], s), lambda t, s: dst_of(land_refs[t], s), send_sems, recv_sems, n):
            send.start()
        token[...] = jnp.zeros_like(token)

    arrays = list(src) + list(land)
    out_shape = ([pltpu.SemaphoreType.DMA((3 * n,)), pltpu.SemaphoreType.DMA((3 * n,))] + [pltpu.HBM(a.shape, a.dtype) for a in arrays]
                 + [_sds((8, 128), f32)])
    res = pl.pallas_call(
        body, out_shape=out_shape, in_specs=[HBM] * m + [ANY], out_specs=[SEM, SEM] + [HBM] * m + [pl.BlockSpec(memory_space=pltpu.VMEM)],
        input_output_aliases={i: 2 + i for i in range(m)}, name=name, compiler_params=_SPLIT)(*_hbm(arrays), after)
    return res[0], res[1], list(res[2:2 + n]), list(res[2 + n:2 + m]), res[-1]


def _exchange_wait(send_sems, recv_sems, src, land, after, src_of, dst_of, name):
    n, m = len(src), len(src) + len(land)

    def body(*refs):
        src_refs = refs[:n]
        land_refs = refs[n:m] if land else src_refs
        send_sems, recv_sems = refs[m], refs[m + 1]
        for send, land_cp in _chip_copies(lambda t, s: src_of(src_refs[t], s), lambda t, s: dst_of(land_refs[t], s), send_sems, recv_sems, n):
            send.wait_send()
            land_cp.wait_recv()

    arrays = list(src) + list(land)
    res = pl.pallas_call(
        body, out_shape=[pltpu.HBM(a.shape, a.dtype) for a in arrays], in_specs=[HBM] * m + [SEM, SEM, ANY], out_specs=[HBM] * m,
        input_output_aliases={i: i for i in range(m)}, name=name, compiler_params=_SPLIT)(*arrays, send_sems, recv_sems, after)
    return list(res[:n]), list(res[n:])


def _own_half(ref, s):
    x, y, c, _ = _place()
    return ref.at[2 * x + y, c]


def _slot_half(ref, s):
    return ref.at[s, lax.axis_index("c")]


def _slot(ref, s):
    return ref.at[s]


def _gather_forward(bufs):
    n = len(bufs)

    def body(*refs):
        outs = refs[n:2 * n]
        send_sems, recv_sems = refs[2 * n:]
        x, y, c, chips = _place()
        sibling = (x, y, 1 - c)
        passed = []
        for t in range(n):
            for k, chip in enumerate(chips):
                blk = outs[t].at[2 * chip[0] + chip[1], c]
                cp = pltpu.make_async_remote_copy(
                    src_ref=blk, dst_ref=blk, send_sem=send_sems.at[t, k], recv_sem=recv_sems.at[t, k],
                    device_id=sibling, device_id_type=MESH)
                cp.start()
                passed.append(cp)
        for t in range(n):
            for k, chip in enumerate(chips):
                blk = outs[t].at[2 * chip[0] + chip[1], 1 - c]
                pltpu.make_async_remote_copy(
                    src_ref=blk, dst_ref=blk, send_sem=send_sems.at[t, k], recv_sem=recv_sems.at[t, k],
                    device_id=sibling, device_id_type=MESH).wait_recv()
        for cp in passed:
            cp.wait_send()

    out_shape = [_sds(a.shape, bf16) for a in bufs]
    return pl.pallas_call(
        body, out_shape=out_shape, in_specs=[ANY] * n, out_specs=[ANY] * n, input_output_aliases={t: t for t in range(n)},
        scratch_shapes=[pltpu.SemaphoreType.DMA((n, 3)), pltpu.SemaphoreType.DMA((n, 3))], name="gather_forward")(*bufs)


def _sibling_swap(grads):
    n = len(grads)

    def body(*refs):
        ins, outs = refs[:n], refs[n:2 * n]
        send_sems, recv_sems = refs[2 * n:]
        x, y, c, _ = _place()
        cps = []
        for t in range(n):
            for s in range(NSH):
                cp = pltpu.make_async_remote_copy(
                    src_ref=ins[t].at[s, 1 - c], dst_ref=outs[t].at[s], send_sem=send_sems.at[t, s], recv_sem=recv_sems.at[t, s],
                    device_id=(x, y, 1 - c), device_id_type=MESH)
                cp.start()
                cps.append(cp)
        for cp in cps:
            cp.wait()

    out_shape = [_sds((NSH,) + a.shape[2:], bf16) for a in grads]
    return pl.pallas_call(
        body, out_shape=out_shape, in_specs=[ANY] * n, out_specs=[ANY] * n,
        scratch_shapes=[pltpu.SemaphoreType.DMA((n, NSH)), pltpu.SemaphoreType.DMA((n, NSH))],
        name="sibling_swap")(*grads)


def _swap_copies(ins, outs, send_sems, recv_sems):
    x, y, c, _ = _place()
    out = []
    for t, (src, dst) in enumerate(zip(ins, outs)):
        for s in range(NSH):
            out.append(pltpu.make_async_remote_copy(
                src_ref=src.at[s, 1 - c], dst_ref=dst.at[s], send_sem=send_sems.at[NSH * t + s], recv_sem=recv_sems.at[NSH * t + s],
                device_id=(x, y, 1 - c), device_id_type=MESH))
    return out


def _swap_start(grads, after, name):
    n = len(grads)
    got = [lax.empty((NSH,) + a.shape[2:], bf16) for a in grads]

    def body(*refs):
        for cp in _swap_copies(refs[:n], refs[n:2 * n], refs[2 * n + 1], refs[2 * n + 2]):
            cp.start()
        refs[-1][...] = jnp.zeros_like(refs[-1])

    arrays = list(grads) + got
    out_shape = ([pltpu.SemaphoreType.DMA((n * NSH,)), pltpu.SemaphoreType.DMA((n * NSH,))]
                 + [pltpu.HBM(a.shape, a.dtype) for a in arrays] + [_sds((8, 128), f32)])
    res = pl.pallas_call(
        body, out_shape=out_shape, in_specs=[HBM] * (2 * n) + [ANY],
        out_specs=[SEM, SEM] + [HBM] * (2 * n) + [pl.BlockSpec(memory_space=pltpu.VMEM)],
        input_output_aliases={i: 2 + i for i in range(2 * n)}, name=name, compiler_params=_SPLIT)(*_hbm(arrays), after)
    return res[0], res[1], list(res[2:2 + n]), list(res[2 + n:2 + 2 * n])


def _swap_wait(send_sems, recv_sems, grads, got, after, name):
    n = len(grads)

    def body(*refs):
        for cp in _swap_copies(refs[:n], refs[n:2 * n], refs[2 * n], refs[2 * n + 1]):
            cp.wait()

    arrays = list(grads) + list(got)
    res = pl.pallas_call(
        body, out_shape=[pltpu.HBM(a.shape, a.dtype) for a in arrays], in_specs=[HBM] * (2 * n) + [SEM, SEM, ANY], out_specs=[HBM] * (2 * n),
        input_output_aliases={i: i for i in range(2 * n)}, name=name, compiler_params=_SPLIT)(*arrays, send_sems, recv_sems, after)
    return list(res[:n]), list(res[n:])


def _row_tile(h):
    return h // 2 if h % 32 == 0 else h


def _pair_sum(grads, got, c_idx):
    n = len(grads)

    def body(c_ref, *refs):
        del c_ref
        for t in range(n):
            refs[2 * n + t][...] = (refs[t][...].astype(f32).reshape(refs[n + t].shape) + refs[n + t][...].astype(f32)).astype(bf16)

    in_specs, out_shape, out_specs = [], [], []
    for a in grads:
        h, cols = a.shape[2:]
        in_specs.append(pl.BlockSpec((1, 1, _row_tile(h), cols), lambda s, i, c: (s, c[0], i, 0)))
    for a in grads:
        h, cols = a.shape[2:]
        in_specs.append(pl.BlockSpec((1, _row_tile(h), cols), lambda s, i, c: (s, i, 0)))
        out_shape.append(_sds((NSH, h, cols), bf16))
        out_specs.append(pl.BlockSpec((1, _row_tile(h), cols), lambda s, i, c: (s, i, 0)))
    return pl.pallas_call(
        body, out_shape=out_shape,
        grid_spec=pltpu.PrefetchScalarGridSpec(num_scalar_prefetch=1, grid=(NSH, 2), in_specs=in_specs, out_specs=out_specs),
        name="pair_sum", compiler_params=_cp("parallel", "parallel"))(c_idx, *grads, *got)


def _chip_sum(psum, parts, full, place, l, name):
    n = len(parts)

    def body(p_ref, *refs):
        s = pl.program_id(1)
        for t in range(n):
            val = jnp.where(s == p_ref[1], refs[t][0], refs[n + t][0]).astype(f32)
            out = refs[3 * n + t]

            @pl.when(s == 0)
            def _(out=out, val=val):
                out[0, 0] = val

            @pl.when(s != 0)
            def _(out=out, val=val):
                out[0, 0] += val

    own_specs, part_specs, out_shape, out_specs = [], [], [], []
    for a, fl in zip(parts, full):
        _, h, cols = a.shape
        r = _row_tile(h)
        own_specs.append(pl.BlockSpec((1, r, cols), lambda i, s, p: (p[1], i, 0)))
        part_specs.append(pl.BlockSpec((1, r, cols), lambda i, s, p: (jnp.where(s == p[1], (s + 1) % NSH, s), i, 0)))
        out_shape.append(_sds(fl.shape, f32))
        out_specs.append(pl.BlockSpec((1, 1, r, cols), lambda i, s, p, l=l: (l, p[0], i, 0)))
    return pl.pallas_call(
        body, out_shape=out_shape,
        grid_spec=pltpu.PrefetchScalarGridSpec(num_scalar_prefetch=1, grid=(2, NSH), in_specs=own_specs + part_specs + [ANY] * n,
                                               out_specs=out_specs),
        input_output_aliases={1 + 2 * n + t: t for t in range(n)}, name=name,
        compiler_params=_cp("parallel", "arbitrary"))(place, *psum, *parts, *full)


def _sibling_share(full, l, name):
    n = len(full)

    def body(*refs):
        outs = refs[n:2 * n]
        send_sems, recv_sems = refs[2 * n:]
        x, y, c, _ = _place()
        sibling = (x, y, 1 - c)
        cps = []
        for t in range(n):
            blk = outs[t].at[l, c]
            cp = pltpu.make_async_remote_copy(
                src_ref=blk, dst_ref=blk, send_sem=send_sems.at[t], recv_sem=recv_sems.at[t], device_id=sibling, device_id_type=MESH)
            cp.start()
            cps.append(cp)
        for t in range(n):
            blk = outs[t].at[l, 1 - c]
            pltpu.make_async_remote_copy(
                src_ref=blk, dst_ref=blk, send_sem=send_sems.at[t], recv_sem=recv_sems.at[t],
                device_id=sibling, device_id_type=MESH).wait_recv()
        for cp in cps:
            cp.wait_send()

    out_shape = [_sds(a.shape, f32) for a in full]
    return pl.pallas_call(
        body, out_shape=out_shape, in_specs=[ANY] * n, out_specs=[ANY] * n, input_output_aliases={t: t for t in range(n)},
        scratch_shapes=[pltpu.SemaphoreType.DMA((n,)), pltpu.SemaphoreType.DMA((n,))], name=name)(*full)


def _share_copies(refs, l, send_sems, recv_sems):
    x, y, c, _ = _place()
    out = []
    for t, ref in enumerate(refs):
        mine, theirs = ref.at[l, c], ref.at[l, 1 - c]
        send = pltpu.make_async_remote_copy(src_ref=mine, dst_ref=mine, send_sem=send_sems.at[t], recv_sem=recv_sems.at[t],
                                            device_id=(x, y, 1 - c), device_id_type=MESH)
        land = pltpu.make_async_remote_copy(src_ref=mine, dst_ref=theirs, send_sem=send_sems.at[t], recv_sem=recv_sems.at[t],
                                            device_id=(x, y, 1 - c), device_id_type=MESH)
        out.append((send, land))
    return out


def _share_start(full, l, after, name):
    n = len(full)

    def body(*refs):
        for send, _ in _share_copies(refs[:n], l, refs[n + 1], refs[n + 2]):
            send.start()
        refs[-1][...] = jnp.zeros_like(refs[-1])

    out_shape = ([pltpu.SemaphoreType.DMA((n,)), pltpu.SemaphoreType.DMA((n,))] + [pltpu.HBM(a.shape, a.dtype) for a in full]
                 + [_sds((8, 128), f32)])
    res = pl.pallas_call(
        body, out_shape=out_shape, in_specs=[HBM] * n + [ANY], out_specs=[SEM, SEM] + [HBM] * n + [pl.BlockSpec(memory_space=pltpu.VMEM)],
        input_output_aliases={i: 2 + i for i in range(n)}, name=name, compiler_params=_SPLIT)(*_hbm(full), after)
    return res[0], res[1], list(res[2:2 + n])


def _share_wait(send_sems, recv_sems, full, l, after, name):
    n = len(full)

    def body(*refs):
        for send, land in _share_copies(refs[:n], l, refs[n], refs[n + 1]):
            send.wait_send()
            land.wait_recv()

    res = pl.pallas_call(
        body, out_shape=[pltpu.HBM(a.shape, a.dtype) for a in full], in_specs=[HBM] * n + [SEM, SEM, ANY], out_specs=[HBM] * n,
        input_output_aliases={i: i for i in range(n)}, name=name, compiler_params=_SPLIT)(*full, send_sems, recv_sems, after)
    return list(res)


SMALL_ROWS = 656


def _pack_small(per_layer, final_vec, loss_tile):
    rows = []
    for l in range(DEPTH):
        for k in ("ffn1_norm", "mix_norm", "ffn2_norm"):
            rows.append(per_layer[k][l].reshape(8, 128))
        rows.append(per_layer["pool_w"][l].reshape(128, 128))
        rows.append(jnp.pad(per_layer["pool_scale"][l].reshape(2, 128), ((0, 6), (0, 0))))
    rows.append(final_vec.reshape(8, 128))
    rows.append(loss_tile)
    return jnp.concatenate(rows, axis=0)


def _unpack_small(buf):
    out = {k: [] for k in ("ffn1_norm", "mix_norm", "ffn2_norm", "pool_w", "pool_scale")}
    r = 0
    for l in range(DEPTH):
        for k in ("ffn1_norm", "mix_norm", "ffn2_norm"):
            out[k].append(buf[r:r + 8].reshape(D))
            r += 8
        out["pool_w"].append(buf[r:r + 128].reshape(4, 64, 64))
        r += 128
        out["pool_scale"].append(buf[r:r + 2].reshape(PW))
        r += 8
    res = {k: jnp.stack(v) for k, v in out.items()}
    res["final_norm"] = buf[r:r + 8].reshape(D)
    res["loss"] = buf[r + 8, 0]
    return res


def _allreduce_small(buf):
    def body(in_ref, out_ref, slots, send_sems, recv_sems):
        x, y, c, _ = _place()
        me = 4 * x + 2 * y + c
        slots[me] = in_ref[...]
        peers = []
        for k in range(1, 8):
            px, py, pc = x ^ (k >> 2), y ^ ((k >> 1) & 1), c ^ (k & 1)
            cp = pltpu.make_async_remote_copy(
                src_ref=in_ref, dst_ref=slots.at[me], send_sem=send_sems.at[k - 1], recv_sem=recv_sems.at[k - 1],
                device_id=(px, py, pc), device_id_type=MESH)
            cp.start()
            peers.append(cp)
        for k in range(1, 8):
            px, py, pc = x ^ (k >> 2), y ^ ((k >> 1) & 1), c ^ (k & 1)
            slot = 4 * px + 2 * py + pc
            pltpu.make_async_remote_copy(
                src_ref=slots.at[slot], dst_ref=slots.at[slot], send_sem=send_sems.at[k - 1], recv_sem=recv_sems.at[k - 1],
                device_id=(px, py, pc), device_id_type=MESH).wait_recv()
        for cp in peers:
            cp.wait_send()
        acc = slots[0]
        for j in range(1, 8):
            acc = acc + slots[j]
        out_ref[...] = acc

    return pl.pallas_call(
        body, out_shape=_sds((SMALL_ROWS, 128), f32),
        in_specs=[pl.BlockSpec(memory_space=pltpu.VMEM)], out_specs=pl.BlockSpec(memory_space=pltpu.VMEM),
        scratch_shapes=[pltpu.VMEM((8, SMALL_ROWS, 128), f32), pltpu.SemaphoreType.DMA((7,)), pltpu.SemaphoreType.DMA((7,))],
        name="allreduce_small", compiler_params=_cp())(buf)


def _adamw_math(w, g, m, v):
    m = ADAM_B1 * m + (1.0 - ADAM_B1) * g
    v = ADAM_B2 * v + (1.0 - ADAM_B2) * (g * g)
    m_hat = m / (1.0 - ADAM_B1 ** ADAM_STEP)
    v_hat = v / (1.0 - ADAM_B2 ** ADAM_STEP)
    return -ADAM_LR * (m_hat / (jnp.sqrt(v_hat) + ADAM_EPS) + ADAM_WD * w), m, v


def _adamw(w, g, m, v, name, first=0, prev=None, dep=None):
    def body(w_ref, g_ref, m_ref, v_ref, *rest):
        go_ref, d_ref, mo_ref, vo_ref = rest[-4:]
        g = g_ref[...]
        d, mn, vn = _adamw_math(w_ref[...], g, m_ref[...], v_ref[...])
        go_ref[...] = g
        d_ref[...] = d
        mo_ref[...] = mn
        vo_ref[...] = vn

    _, rows, cols = w.shape
    r = rows // 2 if rows % 16 == 0 else rows
    spec = pl.BlockSpec((1, r, cols), lambda i, j: (first + i, j, 0))
    gspec = pl.BlockSpec((1, r, cols), lambda i, j: (i, j, 0))
    out = _sds(w.shape, f32)
    extra = [] if prev is None else list(prev)
    dspec, dop = _dep(dep)
    return pl.pallas_call(
        body, out_shape=(out, out, out, out), grid=(g.shape[0], rows // r), in_specs=[spec, gspec, spec, spec] + [ANY] * len(extra) + dspec,
        out_specs=(spec,) * 4, input_output_aliases={4 + i: i for i in range(len(extra))}, name=name,
        compiler_params=_cp("parallel", "parallel"))(w, g, m, v, *extra, *dop)


SMALL_NAMES = ("ffn1_norm", "mix_norm", "pool_w", "pool_scale", "ffn2_norm", "final_norm")
WEIGHT_ORDER = ("ffn1_norm", "ffn1_w_gate", "ffn1_w_up", "ffn1_w_down", "mix_norm", "w_in", "pool_w", "pool_scale", "w_out",
                "ffn2_norm", "ffn2_w_gate", "ffn2_w_up", "ffn2_w_down", "final_norm")


def _pack_small_params(p):
    per_layer = {k: [p[k][l] for l in range(DEPTH)] for k in ("ffn1_norm", "mix_norm", "ffn2_norm", "pool_w", "pool_scale")}
    return _pack_small(per_layer, p["final_norm"], jnp.zeros((8, 128), f32))


def kernel(x, positions, ffn1_norm, ffn1_w_gate, ffn1_w_up, ffn1_w_down, mix_norm, w_in, pool_w, pool_scale, w_out, ffn2_norm, ffn2_w_gate, ffn2_w_up, ffn2_w_down, final_norm, loss_target, m_ffn1_norm, m_ffn1_w_gate, m_ffn1_w_up, m_ffn1_w_down, m_mix_norm, m_w_in, m_pool_w, m_pool_scale, m_w_out, m_ffn2_norm, m_ffn2_w_gate, m_ffn2_w_up, m_ffn2_w_down, m_final_norm, v_ffn1_norm, v_ffn1_w_gate, v_ffn1_w_up, v_ffn1_w_down, v_mix_norm, v_w_in, v_pool_w, v_pool_scale, v_w_out, v_ffn2_norm, v_ffn2_w_gate, v_ffn2_w_up, v_ffn2_w_down, v_final_norm):
    params = dict(ffn1_norm=ffn1_norm, ffn1_w_gate=ffn1_w_gate, ffn1_w_up=ffn1_w_up, ffn1_w_down=ffn1_w_down,
                  mix_norm=mix_norm, w_in=w_in, pool_w=pool_w, pool_scale=pool_scale, w_out=w_out, ffn2_norm=ffn2_norm,
                  ffn2_w_gate=ffn2_w_gate, ffn2_w_up=ffn2_w_up, ffn2_w_down=ffn2_w_down, final_norm=final_norm)
    mom_m = dict(ffn1_norm=m_ffn1_norm, ffn1_w_gate=m_ffn1_w_gate, ffn1_w_up=m_ffn1_w_up, ffn1_w_down=m_ffn1_w_down,
                 mix_norm=m_mix_norm, w_in=m_w_in, pool_w=m_pool_w, pool_scale=m_pool_scale, w_out=m_w_out,
                 ffn2_norm=m_ffn2_norm, ffn2_w_gate=m_ffn2_w_gate, ffn2_w_up=m_ffn2_w_up, ffn2_w_down=m_ffn2_w_down,
                 final_norm=m_final_norm)
    mom_v = dict(ffn1_norm=v_ffn1_norm, ffn1_w_gate=v_ffn1_w_gate, ffn1_w_up=v_ffn1_w_up, ffn1_w_down=v_ffn1_w_down,
                 mix_norm=v_mix_norm, w_in=v_w_in, pool_w=v_pool_w, pool_scale=v_pool_scale, w_out=v_w_out,
                 ffn2_norm=v_ffn2_norm, ffn2_w_gate=v_ffn2_w_gate, ffn2_w_up=v_ffn2_w_up, ffn2_w_down=v_ffn2_w_down,
                 final_norm=v_final_norm)
    names = [t[0] for t in BIG]
    for d in (params, mom_m, mom_v):
        for k in TRANSPOSED:
            d[k] = jnp.swapaxes(d[k], 1, 2)

    place = jnp.stack([lax.axis_index("c"), 2 * lax.axis_index("x") + lax.axis_index("y")]).astype(jnp.int32)
    def gather_start(tag, cast, after):
        return _exchange_start(cast, [], after, _own_half, _slot_half, f"gather_start{tag}")

    def gather_end(started, after, tag, spec):
        send_sems, recv_sems, bufs, _, _ = started
        bufs, _ = _exchange_wait(send_sems, recv_sems, bufs, [], after, _own_half, _slot_half, f"gather_wait{tag}")
        return {nm: a.reshape(NSH, rows, cols) for (nm, rows, cols), a in zip(spec, _gather_forward(bufs))}

    tabs = _rope_tables(*_rope_inputs(positions))
    h = x.reshape(S, D)
    weights, saved = [], []
    cast0 = _cast_layer(params, 0, place)
    first = gather_start("0a", cast0[:FFN1], place)
    second = gather_start("0b", cast0[FFN1:], first[-1])
    after = second[-1]
    casts, started = {}, {}
    for l in range(1, DEPTH):
        casts[l] = _cast_layer(params, l, place, after)
        after = casts[l][0]
        if l == 1:
            started[1] = gather_start(1, casts[1], second[-1])
            after = started[1][-1]
    for l in range(DEPTH):
        if l == 0:
            w = gather_end(first, after, "0a", BIG[:FFN1])
            rest = lambda x1: gather_end(second, x1, "0b", BIG[FFN1:])
        else:
            w, rest = gather_end(started[l], after, l, BIG), None
        dep = None
        if l + 2 < DEPTH:
            started[l + 2] = gather_start(l + 2, casts[l + 2], w["g1"])
            dep = started[l + 2][-1]
        h, sv, w = _layer_fwd(h, w, params, l, tabs, dep, rest)
        weights.append(w)
        saved.append(sv)
        after = h
    loss, dx, dgf = _final_loss(h, final_norm.reshape(1, D), loss_target.reshape(S, D))

    upper = [lax.empty((DEPTH - 1, 2, rows // 2, cols), f32) for _, rows, cols in BIG]
    lower = [lax.empty((1, 2, rows // 2, cols), f32) for _, rows, cols in BIG]
    sg = {k: [None] * DEPTH for k in ("ffn1_norm", "mix_norm", "pool_w", "pool_scale", "ffn2_norm")}
    sg["final_norm"] = dgf

    def reduce_end(started, after, l, full, slot):
        send_sems, recv_sems, psum, parts, _ = started
        psum, parts = _exchange_wait(send_sems, recv_sems, psum, parts, after, _slot, _slot, f"grad_wait{l}")
        return _chip_sum(psum, parts, full, place, slot, f"chip_sum{l}")

    started, dep, shares = None, None, []
    for l in reversed(range(DEPTH)):
        dx, gr, sgl = _layer_bwd(dx, weights[l], saved[l], tabs, dep)
        for k, v in sgl.items():
            sg[k][l] = v
        grads = [gr[nm] for nm in names]
        if started is None:
            got = _sibling_swap(grads)
        else:
            swap_send, swap_recv, grads, got = _swap_start(grads, place, f"swap_start{l}")
            send_sems, recv_sems, upper = _share_start(reduce_end(started, dx, l + 1, upper, l), l, place, f"share_start{l + 1}")
            shares.append((send_sems, recv_sems, l, l + 1))
            grads, got = _swap_wait(swap_send, swap_recv, grads, got, upper[0], f"swap_wait{l}")
        psum = _pair_sum(grads, got, place)
        parts = [lax.empty(a.shape, bf16) for a in psum]
        started = _exchange_start(psum, parts, place, _slot, _slot, f"grad_start{l}")
        dep = started[-1]

    big_out = {}
    for send_sems, recv_sems, slot, l in shares:
        upper = _share_wait(send_sems, recv_sems, upper, slot, place, f"share_wait{l}")
    for (nm, rows, cols), g in zip(BIG, upper):
        k = BIG_SRC[nm]
        big_out[k] = _adamw(params[k], g.reshape(DEPTH - 1, rows, cols), mom_m[k], mom_v[k], "adamw_upper_" + k, first=1, dep=dep)
        dep = big_out[k][1]
    lower = _sibling_share(reduce_end(started, dep, 0, lower, 0), 0, "sibling_share0")
    for (nm, rows, cols), g in zip(BIG, lower):
        k = BIG_SRC[nm]
        big_out[k] = _adamw(params[k], g.reshape(1, rows, cols), mom_m[k], mom_v[k], "adamw_lower_" + k, first=0, prev=big_out[k])

    per_layer = {k: sg[k] for k in ("ffn1_norm", "mix_norm", "ffn2_norm", "pool_w", "pool_scale")}
    small_sum = _allreduce_small(_pack_small(per_layer, sg["final_norm"], loss))
    gs, ds_, ms, vs = _adamw(_pack_small_params(params).reshape(1, SMALL_ROWS, 128), small_sum.reshape(1, SMALL_ROWS, 128),
                             _pack_small_params(mom_m).reshape(1, SMALL_ROWS, 128),
                             _pack_small_params(mom_v).reshape(1, SMALL_ROWS, 128), "adamw_small")
    small_out = [_unpack_small(a.reshape(SMALL_ROWS, 128)) for a in (gs, ds_, ms, vs)]

    grad, delta, new_m, new_v = {}, {}, {}, {}
    for k in WEIGHT_ORDER:
        if k in SMALL_NAMES:
            grad[k], delta[k], new_m[k], new_v[k] = (so[k] for so in small_out)
        else:
            grad[k], delta[k], new_m[k], new_v[k] = big_out[k]
    for d in (grad, delta, new_m, new_v):
        for k in TRANSPOSED:
            d[k] = jnp.swapaxes(d[k], 1, 2)
    return (small_out[0]["loss"], dx.reshape(1, S, D), *[grad[k] for k in WEIGHT_ORDER], *[delta[k] for k in WEIGHT_ORDER],
            *[new_m[k] for k in WEIGHT_ORDER], *[new_v[k] for k in WEIGHT_ORDER])
```
